```python
import jax, jax.numpy as jnp
from jax import lax
import numpy as np

D_MODEL = 2048
BATCH = 8
SEQ = 2048
DEPTH = 1

P_DIM = 256
CHUNK = 64
MIX_WIDTH = D_MODEL
MLSTM_WIDTH = MIX_WIDTH // 2
MLSTM_HEADS = 8
MLSTM_HEAD_DIM = MLSTM_WIDTH // MLSTM_HEADS
QK_CONV = 4
POOL_WIDTH = MIX_WIDTH - MLSTM_WIDTH
POOL_WINDOWS = (2, 4, 8, 16)
POOL_GROUPS = len(POOL_WINDOWS)
POOL_GROUP_WIDTH = POOL_WIDTH // POOL_GROUPS
D_FF = 4 * D_MODEL
IN_COLS = 4 * MLSTM_WIDTH + 2 * MLSTM_HEADS + POOL_WIDTH
SPLITS = [MLSTM_WIDTH, 2 * MLSTM_WIDTH, 3 * MLSTM_WIDTH, 4 * MLSTM_WIDTH,
          4 * MLSTM_WIDTH + MLSTM_HEADS, 4 * MLSTM_WIDTH + 2 * MLSTM_HEADS]
EPS = 1e-6

kernel_name = "hybrid_mlstm_pool_stream_block"


def rms_norm(x, g):
    xf = x.astype(jnp.float32)
    y = xf * lax.rsqrt(jnp.mean(xf * xf, axis=-1, keepdims=True) + EPS)
    return (y * g.astype(jnp.float32)).astype(x.dtype)


def causal_depthwise_conv(x, w):
    K = w.shape[0]
    S = x.shape[1]
    xp = jnp.pad(x, ((0, 0), (K - 1, 0), (0, 0)))
    y = xp[:, 0:S] * w[0]
    for k in range(1, K):
        y = y + xp[:, k:k + S] * w[k]
    return y


def mlstm_chunkwise(q, k, v, log_i, log_f):
    B, S, H, Dh = q.shape
    L = CHUNK
    NC = S // L

    def to_chunks(t):
        t = t.reshape((B, NC, L, H) + t.shape[3:])
        return jnp.moveaxis(t, 3, 1)

    qc = to_chunks(q)
    kc = to_chunks(k) * (Dh ** -0.5)
    vc = to_chunks(v)
    li = to_chunks(log_i)
    lf = to_chunks(log_f)
    b = jnp.cumsum(lf, axis=-1)
    b_tot = b[..., -1]
    a = b_tot[..., None] - b + li
    a_max = jnp.max(a, axis=-1)

    def step(carry, xs):
        C, n, m = carry
        k_c, v_c, a_c, amax_c, btot_c = xs
        m_new = jnp.maximum(btot_c + m, amax_c)
        decay = jnp.exp(btot_c + m - m_new)
        w = jnp.exp(a_c - m_new[..., None])
        C_new = decay[..., None, None] * C + jnp.einsum('bhl,bhld,bhle->bhde', w, v_c, k_c)
        n_new = decay[..., None] * n + jnp.einsum('bhl,bhle->bhe', w, k_c)
        return (C_new, n_new, m_new), (C, n, m)

    xs = (jnp.moveaxis(kc, 2, 0), jnp.moveaxis(vc, 2, 0), jnp.moveaxis(a, 2, 0),
          jnp.moveaxis(a_max, 2, 0), jnp.moveaxis(b_tot, 2, 0))
    init = (jnp.zeros((B, H, Dh, Dh), jnp.float32),
            jnp.zeros((B, H, Dh), jnp.float32),
            jnp.zeros((B, H), jnp.float32))
    _, (C_prev, n_prev, m_prev) = lax.scan(step, init, xs)
    C_prev = jnp.moveaxis(C_prev, 0, 2)
    n_prev = jnp.moveaxis(n_prev, 0, 2)
    m_prev = jnp.moveaxis(m_prev, 0, 2)

    inter_log = b + m_prev[..., None]
    causal = jnp.tril(jnp.ones((L, L), dtype=bool))
    Dlog = b[..., :, None] - b[..., None, :] + li[..., None, :]
    Dlog = jnp.where(causal, Dlog, -jnp.inf)
    m_t = jnp.maximum(inter_log, jnp.max(Dlog, axis=-1))
    Dw = jnp.exp(Dlog - m_t[..., None])
    inter_w = jnp.exp(inter_log - m_t)
    s = jnp.einsum('bhcld,bhcjd->bhclj', qc, kc) * Dw
    num = (inter_w[..., None] * jnp.einsum('bhcde,bhcle->bhcld', C_prev, qc)
           + jnp.einsum('bhclj,bhcjd->bhcld', s, vc))
    den = inter_w * jnp.einsum('bhce,bhcle->bhcl', n_prev, qc) + jnp.sum(s, axis=-1)
    h = num / jnp.maximum(jnp.abs(den), jnp.exp(-m_t))[..., None]
    return jnp.moveaxis(h, 1, 3).reshape(B, S, H, Dh)


def multiscale_pool(u, w_pool, scale):
    B, S, _ = u.shape
    uf = u.astype(jnp.float32)
    cs = jnp.pad(jnp.cumsum(uf, axis=1), ((0, 0), (1, 0), (0, 0)))
    t1 = jnp.arange(1, S + 1, dtype=jnp.float32)
    outs = []
    for g, w in enumerate(POOL_WINDOWS):
        sl = slice(g * POOL_GROUP_WIDTH, (g + 1) * POOL_GROUP_WIDTH)
        hi = cs[:, 1:, sl]
        lo = jnp.pad(cs[:, :S + 1 - w, sl], ((0, 0), (w - 1, 0), (0, 0)))
        mean = (hi - lo) / jnp.minimum(t1, w)[None, :, None]
        outs.append(mean - uf[..., sl])
    pooled = jnp.stack(outs, axis=2).astype(u.dtype)
    mixed = jnp.einsum('bsgc,gcd->bsgd', pooled, w_pool)
    return mixed.reshape(B, S, POOL_WIDTH) * scale


def _fwd_setup_inputs(seed: int = 0) -> dict:
    key = jax.random.key(seed)
    ks = jax.random.split(key, 24)
    H = MLSTM_HEADS
    nrm = lambda k, shape, s: jax.random.normal(k, shape, jnp.float32) * s
    gain = lambda k, shape: 1.0 + 0.05 * jax.random.normal(k, shape, jnp.float32)
    f_bias = jnp.linspace(3.0, 6.0, H, dtype=jnp.float32)
    b_gates = jnp.concatenate([
        nrm(ks[3], (DEPTH, H), 0.1),
        f_bias[None, :] + nrm(ks[4], (DEPTH, H), 0.1)], axis=-1)
    return {
        "x": nrm(ks[0], (BATCH, SEQ, D_MODEL), 1.0),
        "p": nrm(ks[1], (DEPTH, BATCH, SEQ, P_DIM), 1.0),
        "w_in": nrm(ks[2], (DEPTH, D_MODEL, IN_COLS), D_MODEL ** -0.5),
        "b_gates": b_gates,
        "w_qk_conv": nrm(ks[5], (DEPTH, QK_CONV, 2 * MLSTM_WIDTH), QK_CONV ** -0.5),
        "g_mlstm": gain(ks[6], (DEPTH, MLSTM_WIDTH)),
        "w_pool": nrm(ks[7], (DEPTH, POOL_GROUPS, POOL_GROUP_WIDTH, POOL_GROUP_WIDTH), POOL_GROUP_WIDTH ** -0.5),
        "pool_scale": gain(ks[8], (DEPTH, POOL_WIDTH)),
        "w_out": nrm(ks[9], (DEPTH, MIX_WIDTH, D_MODEL), MIX_WIDTH ** -0.5),
        "g_mix_pre": gain(ks[10], (DEPTH, D_MODEL)),
        "g_mix_post": gain(ks[11], (DEPTH, D_MODEL)),
        "w_ff1": nrm(ks[12], (DEPTH, D_MODEL, D_FF), D_MODEL ** -0.5),
        "w_ff2": nrm(ks[13], (DEPTH, D_FF, D_MODEL), D_FF ** -0.5),
        "g_ff_pre": gain(ks[14], (DEPTH, D_MODEL)),
        "g_ff_post": gain(ks[15], (DEPTH, D_MODEL)),
        "w_ple_proj": nrm(ks[16], (DEPTH, P_DIM, D_MODEL), P_DIM ** -0.5),
        "w_ple_gate": nrm(ks[17], (DEPTH, D_MODEL, D_MODEL), D_MODEL ** -0.5),
        "g_ple_gate": gain(ks[18], (DEPTH, D_MODEL)),
        "g_ple_post": gain(ks[19], (DEPTH, D_MODEL)),
    }


def _fwd_reference(x, p, w_in, b_gates, w_qk_conv, g_mlstm, w_pool, pool_scale, w_out,
              g_mix_pre, g_mix_post, w_ff1, w_ff2, g_ff_pre, g_ff_post,
              w_ple_proj, w_ple_gate, g_ple_gate, g_ple_post):
    B, S, _ = x.shape
    H, Dh = MLSTM_HEADS, MLSTM_HEAD_DIM
    h = x
    for i in range(DEPTH):
        xn = rms_norm(h, g_mix_pre[i])
        proj = xn @ w_in[i]
        q, k, v, o, gi, gf, u = jnp.split(proj, SPLITS, axis=-1)
        qk = jax.nn.silu(causal_depthwise_conv(jnp.concatenate([q, k], axis=-1), w_qk_conv[i]))
        q, k = jnp.split(qk, 2, axis=-1)
        gates = jnp.concatenate([gi, gf], axis=-1).astype(jnp.float32) + b_gates[i].astype(jnp.float32)
        log_i = gates[..., :H]
        log_f = jax.nn.log_sigmoid(gates[..., H:])
        cell = mlstm_chunkwise(q.reshape(B, S, H, Dh).astype(jnp.float32),
                               k.reshape(B, S, H, Dh).astype(jnp.float32),
                               v.reshape(B, S, H, Dh).astype(jnp.float32),
                               log_i, log_f).astype(x.dtype)
        cell = rms_norm(cell, g_mlstm[i].reshape(H, Dh))
        h_ml = (jax.nn.sigmoid(o).reshape(B, S, H, Dh) * cell).reshape(B, S, MLSTM_WIDTH)
        h_pool = multiscale_pool(u, w_pool[i], pool_scale[i])
        mix = jnp.concatenate([h_ml, h_pool], axis=-1) @ w_out[i]
        h = h + rms_norm(mix, g_mix_post[i])
        hn = rms_norm(h, g_ff_pre[i])
        ff = jnp.square(jax.nn.relu(hn @ w_ff1[i])) @ w_ff2[i]
        h = h + rms_norm(ff, g_ff_post[i])
        gate = jax.nn.sigmoid(rms_norm(h, g_ple_gate[i]) @ w_ple_gate[i])
        e = p[i] @ w_ple_proj[i]
        h = h + rms_norm(e * gate, g_ple_post[i])
    return h


import jax as _jax
import jax.numpy as _jnp

TWIN_FORMAT = 'train_step'
FWD_PARAMS = ['x', 'p', 'w_in', 'b_gates', 'w_qk_conv', 'g_mlstm', 'w_pool', 'pool_scale', 'w_out', 'g_mix_pre', 'g_mix_post', 'w_ff1', 'w_ff2', 'g_ff_pre', 'g_ff_post', 'w_ple_proj', 'w_ple_gate', 'g_ple_gate', 'g_ple_post']
TWIN_WEIGHTS = ['w_in', 'b_gates', 'w_qk_conv', 'g_mlstm', 'w_pool', 'pool_scale', 'w_out', 'g_mix_pre', 'g_mix_post', 'w_ff1', 'w_ff2', 'g_ff_pre', 'g_ff_post', 'w_ple_proj', 'w_ple_gate', 'g_ple_gate', 'g_ple_post']
TWIN_DIFF_INPUT = 'x'
TWIN_INPUTS = ['x', 'p', 'w_in', 'b_gates', 'w_qk_conv', 'g_mlstm', 'w_pool', 'pool_scale', 'w_out', 'g_mix_pre', 'g_mix_post', 'w_ff1', 'w_ff2', 'g_ff_pre', 'g_ff_post', 'w_ple_proj', 'w_ple_gate', 'g_ple_gate', 'g_ple_post', 'loss_target', 'm_w_in', 'm_b_gates', 'm_w_qk_conv', 'm_g_mlstm', 'm_w_pool', 'm_pool_scale', 'm_w_out', 'm_g_mix_pre', 'm_g_mix_post', 'm_w_ff1', 'm_w_ff2', 'm_g_ff_pre', 'm_g_ff_post', 'm_w_ple_proj', 'm_w_ple_gate', 'm_g_ple_gate', 'm_g_ple_post', 'v_w_in', 'v_b_gates', 'v_w_qk_conv', 'v_g_mlstm', 'v_w_pool', 'v_pool_scale', 'v_w_out', 'v_g_mix_pre', 'v_g_mix_post', 'v_w_ff1', 'v_w_ff2', 'v_g_ff_pre', 'v_g_ff_post', 'v_w_ple_proj', 'v_w_ple_gate', 'v_g_ple_gate', 'v_g_ple_post']
TWIN_OUTPUTS = ['loss', 'grad_x', 'grad_w_in', 'grad_b_gates', 'grad_w_qk_conv', 'grad_g_mlstm', 'grad_w_pool', 'grad_pool_scale', 'grad_w_out', 'grad_g_mix_pre', 'grad_g_mix_post', 'grad_w_ff1', 'grad_w_ff2', 'grad_g_ff_pre', 'grad_g_ff_post', 'grad_w_ple_proj', 'grad_w_ple_gate', 'grad_g_ple_gate', 'grad_g_ple_post', 'delta_w_in', 'delta_b_gates', 'delta_w_qk_conv', 'delta_g_mlstm', 'delta_w_pool', 'delta_pool_scale', 'delta_w_out', 'delta_g_mix_pre', 'delta_g_mix_post', 'delta_w_ff1', 'delta_w_ff2', 'delta_g_ff_pre', 'delta_g_ff_post', 'delta_w_ple_proj', 'delta_w_ple_gate', 'delta_g_ple_gate', 'delta_g_ple_post', 'new_m_w_in', 'new_m_b_gates', 'new_m_w_qk_conv', 'new_m_g_mlstm', 'new_m_w_pool', 'new_m_pool_scale', 'new_m_w_out', 'new_m_g_mix_pre', 'new_m_g_mix_post', 'new_m_w_ff1', 'new_m_w_ff2', 'new_m_g_ff_pre', 'new_m_g_ff_post', 'new_m_w_ple_proj', 'new_m_w_ple_gate', 'new_m_g_ple_gate', 'new_m_g_ple_post', 'new_v_w_in', 'new_v_b_gates', 'new_v_w_qk_conv', 'new_v_g_mlstm', 'new_v_w_pool', 'new_v_pool_scale', 'new_v_w_out', 'new_v_g_mix_pre', 'new_v_g_mix_post', 'new_v_w_ff1', 'new_v_w_ff2', 'new_v_g_ff_pre', 'new_v_g_ff_post', 'new_v_w_ple_proj', 'new_v_w_ple_gate', 'new_v_g_ple_gate', 'new_v_g_ple_post']
TWIN_LEAF_KINDS = {'loss': 'loss', 'grad_x': 'grad_x', 'grad_w_in': 'grad_w', 'grad_b_gates': 'grad_w', 'grad_w_qk_conv': 'grad_w', 'grad_g_mlstm': 'grad_w', 'grad_w_pool': 'grad_w', 'grad_pool_scale': 'grad_w', 'grad_w_out': 'grad_w', 'grad_g_mix_pre': 'grad_w', 'grad_g_mix_post': 'grad_w', 'grad_w_ff1': 'grad_w', 'grad_w_ff2': 'grad_w', 'grad_g_ff_pre': 'grad_w', 'grad_g_ff_post': 'grad_w', 'grad_w_ple_proj': 'grad_w', 'grad_w_ple_gate': 'grad_w', 'grad_g_ple_gate': 'grad_w', 'grad_g_ple_post': 'grad_w', 'delta_w_in': 'delta_w', 'delta_b_gates': 'delta_w', 'delta_w_qk_conv': 'delta_w', 'delta_g_mlstm': 'delta_w', 'delta_w_pool': 'delta_w', 'delta_pool_scale': 'delta_w', 'delta_w_out': 'delta_w', 'delta_g_mix_pre': 'delta_w', 'delta_g_mix_post': 'delta_w', 'delta_w_ff1': 'delta_w', 'delta_w_ff2': 'delta_w', 'delta_g_ff_pre': 'delta_w', 'delta_g_ff_post': 'delta_w', 'delta_w_ple_proj': 'delta_w', 'delta_w_ple_gate': 'delta_w', 'delta_g_ple_gate': 'delta_w', 'delta_g_ple_post': 'delta_w', 'new_m_w_in': 'new_m', 'new_m_b_gates': 'new_m', 'new_m_w_qk_conv': 'new_m', 'new_m_g_mlstm': 'new_m', 'new_m_w_pool': 'new_m', 'new_m_pool_scale': 'new_m', 'new_m_w_out': 'new_m', 'new_m_g_mix_pre': 'new_m', 'new_m_g_mix_post': 'new_m', 'new_m_w_ff1': 'new_m', 'new_m_w_ff2': 'new_m', 'new_m_g_ff_pre': 'new_m', 'new_m_g_ff_post': 'new_m', 'new_m_w_ple_proj': 'new_m', 'new_m_w_ple_gate': 'new_m', 'new_m_g_ple_gate': 'new_m', 'new_m_g_ple_post': 'new_m', 'new_v_w_in': 'new_v', 'new_v_b_gates': 'new_v', 'new_v_w_qk_conv': 'new_v', 'new_v_g_mlstm': 'new_v', 'new_v_w_pool': 'new_v', 'new_v_pool_scale': 'new_v', 'new_v_w_out': 'new_v', 'new_v_g_mix_pre': 'new_v', 'new_v_g_mix_post': 'new_v', 'new_v_w_ff1': 'new_v', 'new_v_w_ff2': 'new_v', 'new_v_g_ff_pre': 'new_v', 'new_v_g_ff_post': 'new_v', 'new_v_w_ple_proj': 'new_v', 'new_v_w_ple_gate': 'new_v', 'new_v_g_ple_gate': 'new_v', 'new_v_g_ple_post': 'new_v'}


def _forward(args):
    return _fwd_reference(*[args[k] for k in FWD_PARAMS])


def _output_shape():
    out = _jax.eval_shape(lambda: _forward(_fwd_setup_inputs(0)))
    return out.shape, out.dtype

N_MICROBATCH = 1
ADAM_LR = 0.001
ADAM_B1 = 0.9
ADAM_B2 = 0.999
ADAM_EPS = 1e-08
ADAM_WD = 0.01
ADAM_STEP = 10
PER_EXAMPLE_BATCH_AXIS = {'x': 0, 'p': 1, 'loss_target': 0}
SHARED_INPUTS = []
_WEIGHT_DTYPES = {'w_in': _jnp.float32, 'b_gates': _jnp.float32, 'w_qk_conv': _jnp.float32, 'g_mlstm': _jnp.float32, 'w_pool': _jnp.float32, 'pool_scale': _jnp.float32, 'w_out': _jnp.float32, 'g_mix_pre': _jnp.float32, 'g_mix_post': _jnp.float32, 'w_ff1': _jnp.float32, 'w_ff2': _jnp.float32, 'g_ff_pre': _jnp.float32, 'g_ff_post': _jnp.float32, 'w_ple_proj': _jnp.float32, 'w_ple_gate': _jnp.float32, 'g_ple_gate': _jnp.float32, 'g_ple_post': _jnp.float32}
MOMENT_SCALE = {'w_in': 2.463653e-01, 'b_gates': 1.272629e+00, 'w_qk_conv': 1.016788e-01, 'g_mlstm': 3.278810e-01, 'w_pool': 2.936373e-01, 'pool_scale': 3.330845e-01, 'w_out': 3.381059e-01, 'g_mix_pre': 4.453780e-01, 'g_mix_post': 7.967099e+00, 'w_ff1': 1.137597e-01, 'w_ff2': 3.389485e-01, 'g_ff_pre': 2.354628e-01, 'g_ff_post': 8.235986e+00, 'w_ple_proj': 1.294457e-01, 'w_ple_gate': 6.857101e-02, 'g_ple_gate': 6.942809e-02, 'g_ple_post': 8.141029e+00}


def _to_microbatches(a, axis):
    t = _jnp.moveaxis(a, axis, 0)
    t = t.reshape((N_MICROBATCH, t.shape[0] // N_MICROBATCH) + t.shape[1:])
    return _jnp.moveaxis(t, 1, axis + 1)


def setup_inputs(seed: int = 0) -> dict:
    inp = _fwd_setup_inputs(seed)
    key = _jax.random.fold_in(_jax.random.key(seed), 7919)
    shape, _ = _output_shape()
    out = dict(inp)
    out["loss_target"] = _jax.random.normal(_jax.random.fold_in(key, 0), shape, _jnp.float32)
    for i, name in enumerate(TWIN_WEIGHTS):
        w = inp[name].astype(_jnp.float32)
        if MOMENT_SCALE is None:
            s = _jnp.sqrt(_jnp.mean(_jnp.square(w)) + 1e-30)
        else:
            s = MOMENT_SCALE[name]
        km, kv = _jax.random.split(_jax.random.fold_in(key, i + 1))
        out[name] = w
        out["m_" + name] = s * _jax.random.normal(km, w.shape, _jnp.float32)
        out["v_" + name] = (s * s) * _jax.random.uniform(kv, w.shape, _jnp.float32, 0.5, 1.5)
    if N_MICROBATCH > 1:
        for name, axis in PER_EXAMPLE_BATCH_AXIS.items():
            out[name] = _to_microbatches(out[name], axis)
    return {'x': out['x'], 'p': out['p'], 'w_in': out['w_in'], 'b_gates': out['b_gates'], 'w_qk_conv': out['w_qk_conv'], 'g_mlstm': out['g_mlstm'], 'w_pool': out['w_pool'], 'pool_scale': out['pool_scale'], 'w_out': out['w_out'], 'g_mix_pre': out['g_mix_pre'], 'g_mix_post': out['g_mix_post'], 'w_ff1': out['w_ff1'], 'w_ff2': out['w_ff2'], 'g_ff_pre': out['g_ff_pre'], 'g_ff_post': out['g_ff_post'], 'w_ple_proj': out['w_ple_proj'], 'w_ple_gate': out['w_ple_gate'], 'g_ple_gate': out['g_ple_gate'], 'g_ple_post': out['g_ple_post'], 'loss_target': out['loss_target'], 'm_w_in': out['m_w_in'], 'm_b_gates': out['m_b_gates'], 'm_w_qk_conv': out['m_w_qk_conv'], 'm_g_mlstm': out['m_g_mlstm'], 'm_w_pool': out['m_w_pool'], 'm_pool_scale': out['m_pool_scale'], 'm_w_out': out['m_w_out'], 'm_g_mix_pre': out['m_g_mix_pre'], 'm_g_mix_post': out['m_g_mix_post'], 'm_w_ff1': out['m_w_ff1'], 'm_w_ff2': out['m_w_ff2'], 'm_g_ff_pre': out['m_g_ff_pre'], 'm_g_ff_post': out['m_g_ff_post'], 'm_w_ple_proj': out['m_w_ple_proj'], 'm_w_ple_gate': out['m_w_ple_gate'], 'm_g_ple_gate': out['m_g_ple_gate'], 'm_g_ple_post': out['m_g_ple_post'], 'v_w_in': out['v_w_in'], 'v_b_gates': out['v_b_gates'], 'v_w_qk_conv': out['v_w_qk_conv'], 'v_g_mlstm': out['v_g_mlstm'], 'v_w_pool': out['v_w_pool'], 'v_pool_scale': out['v_pool_scale'], 'v_w_out': out['v_w_out'], 'v_g_mix_pre': out['v_g_mix_pre'], 'v_g_mix_post': out['v_g_mix_post'], 'v_w_ff1': out['v_w_ff1'], 'v_w_ff2': out['v_w_ff2'], 'v_g_ff_pre': out['v_g_ff_pre'], 'v_g_ff_post': out['v_g_ff_post'], 'v_w_ple_proj': out['v_w_ple_proj'], 'v_w_ple_gate': out['v_w_ple_gate'], 'v_g_ple_gate': out['v_g_ple_gate'], 'v_g_ple_post': out['v_g_ple_post']}


def _loss(weights, diff, rest, loss_target):
    with _jax.named_scope("forward"):
        args = {**rest, TWIN_DIFF_INPUT: diff, **{k: w.astype(_WEIGHT_DTYPES[k]) for k, w in weights.items()}}
        y = _forward(args)
    with _jax.named_scope("loss_head"):
        err = _jnp.square(y.astype(_jnp.float32) - loss_target)
        return 0.5 * _jnp.sum(_jnp.mean(err, axis=-1)) if err.ndim else 0.5 * err


def _adamw(w, g, m, v):
    m = ADAM_B1 * m + (1.0 - ADAM_B1) * g
    v = ADAM_B2 * v + (1.0 - ADAM_B2) * _jnp.square(g)
    m_hat = m / (1.0 - ADAM_B1 ** ADAM_STEP)
    v_hat = v / (1.0 - ADAM_B2 ** ADAM_STEP)
    delta = -ADAM_LR * (m_hat / (_jnp.sqrt(v_hat) + ADAM_EPS) + ADAM_WD * w)
    return delta, m, v


def reference(x, p, w_in, b_gates, w_qk_conv, g_mlstm, w_pool, pool_scale, w_out, g_mix_pre, g_mix_post, w_ff1, w_ff2, g_ff_pre, g_ff_post, w_ple_proj, w_ple_gate, g_ple_gate, g_ple_post, loss_target, m_w_in, m_b_gates, m_w_qk_conv, m_g_mlstm, m_w_pool, m_pool_scale, m_w_out, m_g_mix_pre, m_g_mix_post, m_w_ff1, m_w_ff2, m_g_ff_pre, m_g_ff_post, m_w_ple_proj, m_w_ple_gate, m_g_ple_gate, m_g_ple_post, v_w_in, v_b_gates, v_w_qk_conv, v_g_mlstm, v_w_pool, v_pool_scale, v_w_out, v_g_mix_pre, v_g_mix_post, v_w_ff1, v_w_ff2, v_g_ff_pre, v_g_ff_post, v_w_ple_proj, v_w_ple_gate, v_g_ple_gate, v_g_ple_post):
    given = dict(x=x, p=p, w_in=w_in, b_gates=b_gates, w_qk_conv=w_qk_conv, g_mlstm=g_mlstm, w_pool=w_pool, pool_scale=pool_scale, w_out=w_out, g_mix_pre=g_mix_pre, g_mix_post=g_mix_post, w_ff1=w_ff1, w_ff2=w_ff2, g_ff_pre=g_ff_pre, g_ff_post=g_ff_post, w_ple_proj=w_ple_proj, w_ple_gate=w_ple_gate, g_ple_gate=g_ple_gate, g_ple_post=g_ple_post, loss_target=loss_target, m_w_in=m_w_in, m_b_gates=m_b_gates, m_w_qk_conv=m_w_qk_conv, m_g_mlstm=m_g_mlstm, m_w_pool=m_w_pool, m_pool_scale=m_pool_scale, m_w_out=m_w_out, m_g_mix_pre=m_g_mix_pre, m_g_mix_post=m_g_mix_post, m_w_ff1=m_w_ff1, m_w_ff2=m_w_ff2, m_g_ff_pre=m_g_ff_pre, m_g_ff_post=m_g_ff_post, m_w_ple_proj=m_w_ple_proj, m_w_ple_gate=m_w_ple_gate, m_g_ple_gate=m_g_ple_gate, m_g_ple_post=m_g_ple_post, v_w_in=v_w_in, v_b_gates=v_b_gates, v_w_qk_conv=v_w_qk_conv, v_g_mlstm=v_g_mlstm, v_w_pool=v_w_pool, v_pool_scale=v_pool_scale, v_w_out=v_w_out, v_g_mix_pre=v_g_mix_pre, v_g_mix_post=v_g_mix_post, v_w_ff1=v_w_ff1, v_w_ff2=v_w_ff2, v_g_ff_pre=v_g_ff_pre, v_g_ff_post=v_g_ff_post, v_w_ple_proj=v_w_ple_proj, v_w_ple_gate=v_w_ple_gate, v_g_ple_gate=v_g_ple_gate, v_g_ple_post=v_g_ple_post)
    weights = {n: given[n] for n in TWIN_WEIGHTS}
    shared = {n: given[n] for n in SHARED_INPUTS}
    per_example = {n: given[n] for n in ['x', 'p']}
    grad_fn = _jax.value_and_grad(_loss, argnums=(0, 1))

    def one_microbatch(ex, loss_target):
        ex = dict(ex)
        diff = ex.pop(TWIN_DIFF_INPUT)
        return grad_fn(weights, diff, {**shared, **ex}, loss_target)

    if N_MICROBATCH == 1:
        loss, (grad_w, grad_x) = one_microbatch(per_example, given["loss_target"])
    else:
        def body(carry, xs):
            loss_sum, grad_sum = carry
            l_k, (gw_k, gx_k) = one_microbatch(xs[0], xs[1])
            with _jax.named_scope("update"):
                return (loss_sum + l_k, _jax.tree.map(_jnp.add, grad_sum, gw_k)), gx_k

        init = (_jnp.zeros((), _jnp.float32), _jax.tree.map(_jnp.zeros_like, weights))
        (loss, grad_w), grad_x = _jax.lax.scan(body, init, (per_example, given["loss_target"]))
    with _jax.named_scope("update"):
        delta_w, new_m, new_v = {}, {}, {}
        for n in TWIN_WEIGHTS:
            delta_w[n], new_m[n], new_v[n] = _adamw(weights[n], grad_w[n], given["m_" + n], given["v_" + n])
    return (loss, grad_x, *[grad_w[n] for n in TWIN_WEIGHTS], *[delta_w[n] for n in TWIN_WEIGHTS],
            *[new_m[n] for n in TWIN_WEIGHTS], *[new_v[n] for n in TWIN_WEIGHTS])
```

```python
import functools

import jax
import jax.numpy as jnp
from jax import lax
from jax.experimental import pallas as pl
from jax.experimental.pallas import tpu as pltpu

F32 = jnp.float32
BF16 = jnp.bfloat16
EPS = 1e-6
N_DEV = 8
N_CHIP = 4
LANES = 128
VMEM_LIMIT = 56 * 1024 * 1024
MLSTM_HEADS = 8
POOL_GROUPS = 4
QK_CONV = 4
CHUNK = 128
ROW_TILE = 256
ADAM_LR, ADAM_B1, ADAM_B2, ADAM_EPS, ADAM_WD, ADAM_STEP = 0.001, 0.9, 0.999, 1e-08, 0.01, 10
MESH_AXES = ("x", "y", "c")
MESH = pl.DeviceIdType.MESH
HBM_SPEC = pl.BlockSpec(memory_space=pltpu.HBM)


def _params(*sem):
    if sem:
        return pltpu.CompilerParams(dimension_semantics=sem, vmem_limit_bytes=VMEM_LIMIT)
    return pltpu.CompilerParams(vmem_limit_bytes=VMEM_LIMIT)


def _tile(n, pref):
    if n <= pref:
        return n
    t = (pref // LANES) * LANES
    while t >= LANES:
        if n % t == 0:
            return t
        t -= LANES
    return n


def _all_gather(arrs, name):
    n = len(arrs)

    def body(*refs):
        ins, outs = refs[:n], refs[n:2 * n]
        send, recv, loc = refs[2 * n:]
        x, y, c = lax.axis_index("x"), lax.axis_index("y"), lax.axis_index("c")
        me = (x, y, c)
        sibling = (x, y, 1 - c)
        chips = [(1 - x, y), (x, 1 - y), (1 - x, 1 - y)]

        def row(dev):
            return 4 * dev[0] + 2 * dev[1] + dev[2]

        def copy(w, k, block_of, to, src=None):
            dst = outs[w].at[row(block_of)]
            return pltpu.make_async_remote_copy(
                src_ref=dst if src is None else src, dst_ref=dst,
                send_sem=send.at[7 * w + k], recv_sem=recv.at[7 * w + k],
                device_id=to, device_id_type=MESH)

        started = []
        local = []
        for w in range(n):
            cp = pltpu.make_async_copy(ins[w], outs[w].at[row(me)], loc.at[w])
            cp.start()
            local.append(cp)
            first = [copy(w, 0, me, sibling, src=ins[w])]
            first += [copy(w, 1 + j, me, (*chip, c), src=ins[w]) for j, chip in enumerate(chips)]
            for cp in first:
                cp.start()
            started += first
        for j, chip in enumerate(chips):
            for w in range(n):
                copy(w, 1 + j, (*chip, c), me).wait_recv()
                fwd = copy(w, 4 + j, (*chip, c), sibling)
                fwd.start()
                started.append(fwd)
        for w in range(n):
            copy(w, 0, sibling, me).wait_recv()
            for j, chip in enumerate(chips):
                copy(w, 4 + j, (*chip, 1 - c), me).wait_recv()
        for cp in started:
            cp.wait_send()
        for cp in local:
            cp.wait()

    outs = pl.pallas_call(
        body, name=name,
        out_shape=[jax.ShapeDtypeStruct((N_DEV,) + a.shape, a.dtype) for a in arrs],
        in_specs=[HBM_SPEC] * n, out_specs=[HBM_SPEC] * n,
        scratch_shapes=[pltpu.SemaphoreType.DMA((7 * n,)), pltpu.SemaphoreType.DMA((7 * n,)),
                        pltpu.SemaphoreType.DMA((n,))],
    )(*arrs)
    return list(outs)


def _pair_exchange(pieces, name):
    n = len(pieces)

    def body(*refs):
        ins, outs = refs[:n], refs[n:2 * n]
        send, recv = refs[2 * n:]
        x, y, c = lax.axis_index("x"), lax.axis_index("y"), lax.axis_index("c")
        sibling = (x, y, 1 - c)

        def copy(w, chip, src_core):
            return pltpu.make_async_remote_copy(
                src_ref=ins[w].at[2 * chip + src_core], dst_ref=outs[w].at[chip],
                send_sem=send.at[N_CHIP * w + chip], recv_sem=recv.at[N_CHIP * w + chip],
                device_id=sibling, device_id_type=MESH)

        cps = [copy(w, chip, 1 - c) for w in range(n) for chip in range(N_CHIP)]
        for cp in cps:
            cp.start()
        for cp in cps:
            cp.wait_recv()
        for cp in cps:
            cp.wait_send()

    outs = pl.pallas_call(
        body, name=name,
        out_shape=[jax.ShapeDtypeStruct((N_CHIP,) + a.shape[1:], a.dtype) for a in pieces],
        in_specs=[HBM_SPEC] * n, out_specs=[HBM_SPEC] * n,
        scratch_shapes=[pltpu.SemaphoreType.DMA((N_CHIP * n,)), pltpu.SemaphoreType.DMA((N_CHIP * n,))],
    )(*pieces)
    return list(outs)


def _chip_exchange(sums, name):
    n = len(sums)

    def body(*refs):
        ins, outs = refs[:n], refs[n:2 * n]
        send, recv = refs[2 * n:]
        x, y, c = lax.axis_index("x"), lax.axis_index("y"), lax.axis_index("c")
        chips = [(1 - x, y), (x, 1 - y), (1 - x, 1 - y)]

        def copy(w, k):
            chip = chips[k]
            return pltpu.make_async_remote_copy(
                src_ref=ins[w].at[2 * chip[0] + chip[1]], dst_ref=outs[w].at[k],
                send_sem=send.at[3 * w + k], recv_sem=recv.at[3 * w + k],
                device_id=(*chip, c), device_id_type=MESH)

        cps = [copy(w, k) for k in range(3) for w in range(n)]
        for cp in cps:
            cp.start()
        for cp in cps:
            cp.wait_recv()
        for cp in cps:
            cp.wait_send()

    outs = pl.pallas_call(
        body, name=name,
        out_shape=[jax.ShapeDtypeStruct((3,) + a.shape[1:], a.dtype) for a in sums],
        in_specs=[HBM_SPEC] * n, out_specs=[HBM_SPEC] * n,
        scratch_shapes=[pltpu.SemaphoreType.DMA((3 * n,)), pltpu.SemaphoreType.DMA((3 * n,))],
    )(*sums)
    return list(outs)


def _pair_sum(pieces, got, core, name):
    _, rows, cols = pieces.shape
    tr = _tile_rows(rows, cols)

    def body(core_ref, mine_ref, got_ref, out_ref):
        del core_ref
        out_ref[...] = (mine_ref[...].astype(F32) + got_ref[...].astype(F32)).astype(out_ref.dtype)

    return pl.pallas_call(
        body, name=name,
        out_shape=jax.ShapeDtypeStruct((N_CHIP, rows, cols), pieces.dtype),
        grid_spec=pltpu.PrefetchScalarGridSpec(
            num_scalar_prefetch=1, grid=(N_CHIP, rows // tr),
            in_specs=[pl.BlockSpec((None, tr, cols), lambda ch, i, core: (2 * ch + core[0], i, 0)),
                      pl.BlockSpec((None, tr, cols), lambda ch, i, core: (ch, i, 0))],
            out_specs=pl.BlockSpec((None, tr, cols), lambda ch, i, core: (ch, i, 0))),
        compiler_params=_params("parallel", "parallel"),
    )(core, pieces, got)


def _tile_rows(rows, cols, budget=2 * 1024 * 1024):
    want = max(16, budget // (4 * cols))
    if rows <= want:
        return rows
    t = (want // 16) * 16
    while t >= 16:
        if rows % t == 0:
            return t
        t -= 16
    return rows


def _adamw_math(w, g, m, v):
    m = ADAM_B1 * m + (1.0 - ADAM_B1) * g
    v = ADAM_B2 * v + (1.0 - ADAM_B2) * (g * g)
    m_hat = m / (1.0 - ADAM_B1 ** ADAM_STEP)
    v_hat = v / (1.0 - ADAM_B2 ** ADAM_STEP)
    delta = -ADAM_LR * (m_hat / (jnp.sqrt(v_hat) + ADAM_EPS) + ADAM_WD * w)
    return delta, m, v


def _adamw_shard(w, m, v, sums, got, chip, name):
    rows, cols = w.shape
    tr = _tile_rows(rows, cols, budget=1024 * 1024)

    def body(chip_ref, w_ref, m_ref, v_ref, own_ref, got_ref, g_out, d_out, m_out, v_out):
        del chip_ref
        g = own_ref[...].astype(F32)
        for k in range(3):
            g = g + got_ref[k].astype(F32)
        delta, m_new, v_new = _adamw_math(w_ref[...], g, m_ref[...], v_ref[...])
        g_out[...] = g
        d_out[...] = delta
        m_out[...] = m_new
        v_out[...] = v_new

    blk = pl.BlockSpec((tr, cols), lambda i, chip: (i, 0))
    return pl.pallas_call(
        body, name=name,
        out_shape=[jax.ShapeDtypeStruct((rows, cols), F32)] * 4,
        grid_spec=pltpu.PrefetchScalarGridSpec(
            num_scalar_prefetch=1, grid=(rows // tr,),
            in_specs=[blk, blk, blk,
                      pl.BlockSpec((None, tr, cols), lambda i, chip: (chip[0], i, 0)),
                      pl.BlockSpec((3, tr, cols), lambda i, chip: (0, i, 0))],
            out_specs=[blk] * 4),
        compiler_params=_params("parallel"),
    )(chip, w, m, v, sums, got)


def _adamw_replicated(w, m, v, parts, name):
    rows, cols = w.shape

    def body(w_ref, m_ref, v_ref, parts_ref, g_out, d_out, m_out, v_out):
        g = parts_ref[0]
        for d in range(1, N_DEV):
            g = g + parts_ref[d]
        delta, m_new, v_new = _adamw_math(w_ref[...], g, m_ref[...], v_ref[...])
        g_out[...] = g
        d_out[...] = delta
        m_out[...] = m_new
        v_out[...] = v_new

    return pl.pallas_call(
        body, name=name,
        out_shape=[jax.ShapeDtypeStruct((rows, cols), F32)] * 4,
        compiler_params=_params(),
    )(w, m, v, parts)


NN = ((1,), (0,))
NT = ((1,), (1,))
TN = ((0,), (0,))


def _mm(name, a, b, *, dims, grid, a_spec, b_spec, outs, extra=(), extra_specs=(), epilogue=None, acc_shape=None):
    nk = grid[2]
    n_extra, n_out = len(extra), len(outs)

    def body(*refs):
        a_ref, b_ref = refs[0], refs[1]
        ex = refs[2:2 + n_extra]
        out_refs = refs[2 + n_extra:2 + n_extra + n_out]
        part = lax.dot_general(a_ref[...].astype(BF16), b_ref[...].astype(BF16), (dims, ((), ())),
                               preferred_element_type=F32)

        def finish(acc):
            res = epilogue(acc, *[e[...] for e in ex]) if epilogue else (acc,)
            for val, o in zip(res, out_refs):
                o[...] = val.astype(o.dtype)

        if nk == 1:
            finish(part)
        else:
            acc_ref = refs[-1]
            k = pl.program_id(2)

            @pl.when(k == 0)
            def _():
                acc_ref[...] = part

            @pl.when(k > 0)
            def _():
                acc_ref[...] += part

            @pl.when(k == nk - 1)
            def _():
                finish(acc_ref[...])

    res = pl.pallas_call(
        body, name=name,
        out_shape=[jax.ShapeDtypeStruct(s, d) for s, d, _ in outs],
        grid=grid,
        in_specs=[a_spec, b_spec, *extra_specs],
        out_specs=[sp for _, _, sp in outs],
        scratch_shapes=[pltpu.VMEM(acc_shape, F32)] if nk > 1 else [],
        compiler_params=_params("parallel", "parallel", "arbitrary"),
    )(a, b, *extra)
    return list(res)


def _mm_nn(name, a, b, out_dtype=F32, tm=1024, tn=1024, tk=2048, epilogue=None, n_out=1, extra=()):
    m, kd = a.shape
    n = b.shape[1]
    tm, tn, tk = _tile(m, tm), _tile(n, tn), _tile(kd, tk)
    dts = out_dtype if isinstance(out_dtype, (list, tuple)) else [out_dtype] * n_out
    o_spec = pl.BlockSpec((tm, tn), lambda i, j, k: (i, j))
    return _mm(name, a, b, dims=NN, grid=(m // tm, n // tn, kd // tk),
               a_spec=pl.BlockSpec((tm, tk), lambda i, j, k: (i, k)),
               b_spec=pl.BlockSpec((tk, tn), lambda i, j, k: (k, j)),
               outs=[((m, n), dt, o_spec) for dt in dts], extra=extra, extra_specs=[o_spec] * len(extra),
               epilogue=epilogue, acc_shape=(tm, tn))


def _mm_nn_pieces(name, a, b, out_dtype=F32, tm=1024, tk=2048, epilogue=None, n_out=1):
    m, kd = a.shape
    npc, _, ns = b.shape
    tm, tk = _tile(m, tm), _tile(kd, tk)
    tn = _tile(ns, 1024)
    r = ns // tn
    dts = out_dtype if isinstance(out_dtype, (list, tuple)) else [out_dtype] * n_out
    o_spec = pl.BlockSpec((tm, tn), lambda i, j, k: (i, j))
    return _mm(name, a, b, dims=NN, grid=(m // tm, npc * r, kd // tk),
               a_spec=pl.BlockSpec((tm, tk), lambda i, j, k: (i, k)),
               b_spec=pl.BlockSpec((None, tk, tn), lambda i, j, k: (j // r, k, j % r)),
               outs=[((m, npc * ns), dt, o_spec) for dt in dts], epilogue=epilogue, acc_shape=(tm, tn))


def _mm_nt(name, a, b, out_dtype=F32, tm=1024, tn=1024, tk=2048, epilogue=None, extra=()):
    m, kd = a.shape
    n = b.shape[0]
    tm, tn, tk = _tile(m, tm), _tile(n, tn), _tile(kd, tk)
    o_spec = pl.BlockSpec((tm, tn), lambda i, j, k: (i, j))
    return _mm(name, a, b, dims=NT, grid=(m // tm, n // tn, kd // tk),
               a_spec=pl.BlockSpec((tm, tk), lambda i, j, k: (i, k)),
               b_spec=pl.BlockSpec((tn, tk), lambda i, j, k: (j, k)),
               outs=[((m, n), out_dtype, o_spec)], extra=extra, extra_specs=[o_spec] * len(extra),
               epilogue=epilogue, acc_shape=(tm, tn))[0]


def _mm_nt_pieces(name, a, b, out_dtype=F32, tm=1024, tn=1024):
    m = a.shape[0]
    npc, n, ks = b.shape
    tm, tn = _tile(m, tm), _tile(n, tn)
    tk = _tile(ks, 1024)
    r = ks // tk
    return _mm(name, a, b, dims=NT, grid=(m // tm, n // tn, npc * r),
               a_spec=pl.BlockSpec((tm, tk), lambda i, j, k: (i, k)),
               b_spec=pl.BlockSpec((None, tn, tk), lambda i, j, k: (k // r, j, k % r)),
               outs=[((m, n), out_dtype, pl.BlockSpec((tm, tn), lambda i, j, k: (i, j)))],
               acc_shape=(tm, tn))[0]


def _mm_tn(name, a, b, out_dtype=BF16, tm=1024, tn=1024, pieces=0):
    t, m = a.shape
    n = b.shape[1]
    tm = _tile(m, tm)
    if pieces:
        ns = n // pieces
        tn = _tile(ns, tn)
        r = ns // tn
        out = ((pieces, m, ns), out_dtype, pl.BlockSpec((None, tm, tn), lambda i, j, k: (j // r, i, j % r)))
    else:
        tn = _tile(n, tn)
        out = ((m, n), out_dtype, pl.BlockSpec((tm, tn), lambda i, j, k: (i, j)))
    return _mm(name, a, b, dims=TN, grid=(m // tm, n // tn, 1),
               a_spec=pl.BlockSpec((t, tm), lambda i, j, k: (0, i)),
               b_spec=pl.BlockSpec((t, tn), lambda i, j, k: (0, j)),
               outs=[out])[0]


def _rms_f(x, g):
    r = lax.rsqrt(jnp.mean(x * x, axis=-1, keepdims=True) + EPS)
    xh = x * r
    return xh * g, xh, r


def _rms_b(dy, g, xh, r):
    dxh = dy * g
    dx = r * (dxh - xh * jnp.mean(dxh * xh, axis=-1, keepdims=True))
    dg = jnp.sum(dy * xh, axis=0, keepdims=True)
    return dx, dg


def _rowwise(name, fn, row_ins, vec_ins, row_outs, vec_outs):
    t = row_ins[0].shape[0]
    tr = min(ROW_TILE, t)
    nr, nv, no = len(row_ins), len(vec_ins), len(row_outs)

    def body(*refs):
        ri, vi = refs[:nr], refs[nr:nr + nv]
        ro, vo = refs[nr + nv:nr + nv + no], refs[nr + nv + no:]
        routs, vouts = fn(*[r[...] for r in ri], *[v[...] for v in vi])
        for o, val in zip(ro, routs):
            o[...] = val.astype(o.dtype)
        i = pl.program_id(0)
        for o, val in zip(vo, vouts):
            @pl.when(i == 0)
            def _():
                o[...] = val

            @pl.when(i > 0)
            def _():
                o[...] += val

    res = pl.pallas_call(
        body, name=name,
        out_shape=[jax.ShapeDtypeStruct((t, w), d) for w, d in row_outs]
        + [jax.ShapeDtypeStruct((1, w), F32) for w in vec_outs],
        grid=(t // tr,),
        in_specs=[pl.BlockSpec((tr, a.shape[1]), lambda i: (i, 0)) for a in row_ins]
        + [pl.BlockSpec((1, a.shape[1]), lambda i: (0, 0)) for a in vec_ins],
        out_specs=[pl.BlockSpec((tr, w), lambda i: (i, 0)) for w, _ in row_outs]
        + [pl.BlockSpec((1, w), lambda i: (0, 0)) for w in vec_outs],
        compiler_params=_params("arbitrary"),
    )(*row_ins, *vec_ins)
    return list(res)


def _norm_fwd(x, g, name):
    d = x.shape[1]
    return _rowwise(name, lambda x, g: ((_rms_f(x, g)[0],), ()), [x], [g], [(d, BF16)], [])[0]


def _norm_bwd(x, dxn, dres, g, name):
    d = x.shape[1]

    def fn(x, dxn, dres, g):
        _, xh, r = _rms_f(x, g)
        dx, dg = _rms_b(dxn, g, xh, r)
        return (dres + dx,), (dg,)

    return _rowwise(name, fn, [x, dxn, dres], [g], [(d, F32)], [d])


def _block_fwd(h_prev, y, g_y, g_n, name):
    d = h_prev.shape[1]

    def fn(h_prev, y, g_y, g_n):
        h = h_prev + _rms_f(y, g_y)[0]
        return (h, _rms_f(h, g_n)[0]), ()

    return _rowwise(name, fn, [h_prev, y], [g_y, g_n], [(d, F32), (d, BF16)], [])


def _block_bwd(h, y, d_up, d_n, g_y, g_n, name):
    d = h.shape[1]

    def fn(h, y, d_up, d_n, g_y, g_n):
        _, hh, hr = _rms_f(h, g_n)
        dh_n, dg_n = _rms_b(d_n, g_n, hh, hr)
        dh = d_up + dh_n
        _, yh, yr = _rms_f(y, g_y)
        dy, dg_y = _rms_b(dh, g_y, yh, yr)
        return (dh, dy), (dg_y, dg_n)

    return _rowwise(name, fn, [h, y, d_up, d_n], [g_y, g_n], [(d, F32), (d, BF16)], [d, d])


def _loss_head(h2, z, e, target, g, name):
    d = h2.shape[1]

    def fn(h2, z, e, target, g):
        gate = jax.nn.sigmoid(z)
        y, xh, r = _rms_f(e * gate, g)
        diff = h2 + y - target
        loss = 0.5 * jnp.sum(jnp.mean(diff * diff, axis=-1, keepdims=True), axis=0, keepdims=True)
        dh3 = diff / d
        deg, dg = _rms_b(dh3, g, xh, r)
        dz = deg * e * gate * (1.0 - gate)
        de = deg * gate
        return (dh3, dz, de), (dg, jnp.broadcast_to(loss, (1, LANES)))

    return _rowwise(name, fn, [h2, z, e, target], [g], [(d, F32), (d, BF16), (d, BF16)], [d, LANES])


def _shift_down(x, s, rows):
    return jnp.where(rows >= s, pltpu.roll(x, s, axis=0), 0.0)


def _shift_up(x, s, rows):
    t = x.shape[0]
    return jnp.where(rows < t - s, pltpu.roll(x, t - s, axis=0), 0.0)


def _conv_pre(x, w, rows):
    pre = x * w[QK_CONV - 1:QK_CONV, :]
    for s in range(1, QK_CONV):
        pre = pre + _shift_down(x, s, rows) * w[QK_CONV - 1 - s:QK_CONV - s, :]
    return pre


def _conv_fwd(proj_a, w, start, name):
    t = proj_a.shape[0]
    width = w.shape[1]
    tc = _tile(width, 256)
    off = start // tc

    def body(x_ref, w_ref, y_ref):
        x = x_ref[...]
        rows = lax.broadcasted_iota(jnp.int32, x.shape, 0)
        pre = _conv_pre(x, w_ref[...], rows)
        y_ref[...] = pre * jax.nn.sigmoid(pre)

    return pl.pallas_call(
        body, name=name,
        out_shape=jax.ShapeDtypeStruct((t, width), F32),
        grid=(width // tc,),
        in_specs=[pl.BlockSpec((t, tc), lambda j: (0, off + j)), pl.BlockSpec((QK_CONV, tc), lambda j: (0, j))],
        out_specs=pl.BlockSpec((t, tc), lambda j: (0, j)),
        compiler_params=_params("parallel"),
    )(proj_a, w)


def _conv_bwd(proj_a, w, dy, start, name):
    t = proj_a.shape[0]
    width = w.shape[1]
    tc = _tile(width, 256)
    off = start // tc

    def body(x_ref, w_ref, dy_ref, dx_ref, dw_ref):
        x, w = x_ref[...], w_ref[...]
        rows = lax.broadcasted_iota(jnp.int32, x.shape, 0)
        pre = _conv_pre(x, w, rows)
        sig = jax.nn.sigmoid(pre)
        dpre = dy_ref[...] * (sig * (1.0 + pre * (1.0 - sig)))
        dx = dpre * w[QK_CONV - 1:QK_CONV, :]
        dws = [jnp.sum(dpre * x, axis=0, keepdims=True)]
        for s in range(1, QK_CONV):
            dx = dx + _shift_up(dpre, s, rows) * w[QK_CONV - 1 - s:QK_CONV - s, :]
            dws.append(jnp.sum(dpre * _shift_down(x, s, rows), axis=0, keepdims=True))
        dx_ref[...] = dx.astype(dx_ref.dtype)
        for s in range(QK_CONV):
            dw_ref[QK_CONV - 1 - s:QK_CONV - s, :] = dws[s]

    return pl.pallas_call(
        body, name=name,
        out_shape=[jax.ShapeDtypeStruct((t, width), BF16), jax.ShapeDtypeStruct((QK_CONV, width), F32)],
        grid=(width // tc,),
        in_specs=[pl.BlockSpec((t, tc), lambda j: (0, off + j)), pl.BlockSpec((QK_CONV, tc), lambda j: (0, j)),
                  pl.BlockSpec((t, tc), lambda j: (0, j))],
        out_specs=[pl.BlockSpec((t, tc), lambda j: (0, j)), pl.BlockSpec((QK_CONV, tc), lambda j: (0, j))],
        compiler_params=_params("parallel"),
    )(proj_a, w, dy)


def _gates_fwd(proj_b, bias, col_block, heads, name):
    t = proj_b.shape[0]

    def body(x_ref, b_ref, o_ref):
        pre = x_ref[...] + b_ref[...]
        col = lax.broadcasted_iota(jnp.int32, pre.shape, 1)
        log_f = jnp.minimum(pre, 0.0) - jnp.log(1.0 + jnp.exp(-jnp.abs(pre)))
        o_ref[...] = jnp.where(col < heads, pre, jnp.where(col < 2 * heads, log_f, 0.0))

    return pl.pallas_call(
        body, name=name,
        out_shape=jax.ShapeDtypeStruct((t, LANES), F32),
        grid=(1,),
        in_specs=[pl.BlockSpec((t, LANES), lambda i: (0, col_block)), pl.BlockSpec((1, LANES), lambda i: (0, 0))],
        out_specs=pl.BlockSpec((t, LANES), lambda i: (0, 0)),
        compiler_params=_params("arbitrary"),
    )(proj_b, bias)


def _gates_bwd(proj_b, bias, dgl, col_block, heads, name):
    t = proj_b.shape[0]

    def body(x_ref, b_ref, d_ref, dx_ref, db_ref):
        pre = x_ref[...] + b_ref[...]
        col = lax.broadcasted_iota(jnp.int32, pre.shape, 1)
        d = d_ref[...]
        dx = jnp.where(col < heads, d, jnp.where(col < 2 * heads, d * jax.nn.sigmoid(-pre), 0.0))
        dx_ref[...] = dx.astype(dx_ref.dtype)
        db_ref[...] = jnp.sum(dx, axis=0, keepdims=True)

    return pl.pallas_call(
        body, name=name,
        out_shape=[jax.ShapeDtypeStruct((t, LANES), BF16), jax.ShapeDtypeStruct((1, LANES), F32)],
        grid=(1,),
        in_specs=[pl.BlockSpec((t, LANES), lambda i: (0, col_block)), pl.BlockSpec((1, LANES), lambda i: (0, 0)),
                  pl.BlockSpec((t, LANES), lambda i: (0, 0))],
        out_specs=[pl.BlockSpec((t, LANES), lambda i: (0, 0)), pl.BlockSpec((1, LANES), lambda i: (0, 0))],
        compiler_params=_params("arbitrary"),
    )(proj_b, bias, dgl)


def _dot(a, b, dims):
    return lax.dot_general(a.astype(BF16), b.astype(BF16), (dims, ((), ())), preferred_element_type=F32)


def _dot_exact(a, b):
    return lax.dot_general(a, b, (NN, ((), ())), preferred_element_type=F32, precision=lax.Precision.HIGHEST)


def _chunk_gates(li, lf):
    n = li.shape[0]
    r = lax.broadcasted_iota(jnp.int32, (n, n), 0)
    c = lax.broadcasted_iota(jnp.int32, (n, n), 1)
    tri = (c <= r).astype(F32)
    b_cols = _dot_exact(tri, jnp.broadcast_to(lf, (n, n)))
    b_rows = b_cols.T
    li_rows = jnp.broadcast_to(li, (n, n)).T
    b = b_cols[:, 0:1]
    b_tot = b_cols[n - 1:n, 0:1]
    return b_cols, b_rows, li_rows, b, b_tot, c <= r


def _chunk_state(k, v, li, b, b_tot, ct, nrow, m_prev):
    a = b_tot - b + li
    m_new = jnp.maximum(b_tot + m_prev, jnp.max(a, axis=0, keepdims=True))
    dec = jnp.exp(b_tot + m_prev - m_new)
    w = jnp.exp(a - m_new)
    wv = w * v
    ct_new = dec * ct + _dot(k, wv, TN)
    n_new = dec * nrow + jnp.sum(w * k, axis=0, keepdims=True)
    return ct_new, n_new, m_new, dec, w, wv


def _chunk_out(q, k, v, gates, ct, nrow, m_prev):
    b_cols, b_rows, li_rows, b, _, causal = gates
    dlog = jnp.where(causal, b_cols - b_rows + li_rows, -jnp.inf)
    inter = b + m_prev
    mt = jnp.maximum(inter, jnp.max(dlog, axis=-1, keepdims=True))
    dw = jnp.exp(dlog - mt)
    iw = jnp.exp(inter - mt)
    a = _dot(q, k, NT)
    sm = a * dw
    qc = _dot(q, ct, NN)
    qn = jnp.sum(q * nrow, axis=-1, keepdims=True)
    num = iw * qc + _dot(sm, v, NN)
    den = iw * qn + jnp.sum(sm, axis=-1, keepdims=True)
    floor = jnp.exp(-mt)
    dn = jnp.maximum(jnp.abs(den), floor)
    h = num / dn
    return h, (dw, iw, a, sm, qc, qn, den, floor, dn)


def _head_out(h, o, g):
    cell, hh, r = _rms_f(h, g)
    sig = jax.nn.sigmoid(o)
    return sig * cell, (cell, hh, r, sig)


def _mlstm_fwd(qc, kc, proj_a, gl, g_mlstm, heads, d_model, name):
    t = qc.shape[0]
    dh = g_mlstm.shape[1] // heads
    nc = t // CHUNK
    scale = dh ** -0.5

    def body(q_ref, k_ref, v_ref, o_ref, li_ref, lf_ref, g_ref, out_ref, ct_ref, n_ref, m_ref):
        ct_ref[...] = jnp.zeros_like(ct_ref)
        n_ref[...] = jnp.zeros_like(n_ref)
        m_ref[...] = jnp.zeros_like(m_ref)

        def chunk(ci, carry):
            sl = pl.ds(pl.multiple_of(ci * CHUNK, CHUNK), CHUNK)
            q, k, v = q_ref[sl, :], k_ref[sl, :] * scale, v_ref[sl, :]
            li, lf = li_ref[sl, :], lf_ref[sl, :]
            gates = _chunk_gates(li, lf)
            ct, nrow, m_prev = ct_ref[...], n_ref[0:1, :], m_ref[0:1, 0:1]
            h, _ = _chunk_out(q, k, v, gates, ct, nrow, m_prev)
            y, _ = _head_out(h, o_ref[sl, :], g_ref[...])
            out_ref[sl, :] = y.astype(out_ref.dtype)
            ct_new, n_new, m_new, _, _, _ = _chunk_state(k, v, li, gates[3], gates[4], ct, nrow, m_prev)
            ct_ref[...] = ct_new
            n_ref[0:1, :] = n_new
            m_ref[0:1, 0:1] = m_new
            return carry

        lax.fori_loop(0, nc, chunk, 0)

    col = lambda off: pl.BlockSpec((t, dh), lambda h: (0, off + h))
    gate = lambda off: pl.BlockSpec((None, t, 1), lambda h: (off + h, 0, 0))
    return pl.pallas_call(
        body, name=name,
        out_shape=jax.ShapeDtypeStruct((t, d_model), BF16),
        grid=(heads,),
        in_specs=[col(0), col(0), col(2 * heads), col(3 * heads), gate(0), gate(heads),
                  pl.BlockSpec((1, dh), lambda h: (0, h))],
        out_specs=pl.BlockSpec((t, dh), lambda h: (0, h)),
        scratch_shapes=[pltpu.VMEM((dh, dh), F32), pltpu.VMEM((8, dh), F32), pltpu.VMEM((8, LANES), F32)],
        compiler_params=_params("arbitrary"),
    )(qc, kc, proj_a, proj_a, gl, gl, g_mlstm)


def _mlstm_bwd(qc, kc, proj_a, gl, g_mlstm, dhcat, heads, name):
    t = qc.shape[0]
    wm = g_mlstm.shape[1]
    dh = wm // heads
    nc = t // CHUNK
    scale = dh ** -0.5

    def body(q_ref, k_ref, v_ref, o_ref, li_ref, lf_ref, g_ref, dy_ref,
             dq_ref, dk_ref, dv_ref, do_ref, dli_ref, dlf_ref, dg_ref,
             ct_all, n_all, m_all, dct_ref, dn_ref):
        g = g_ref[...]

        def load(ci):
            sl = pl.ds(pl.multiple_of(ci * CHUNK, CHUNK), CHUNK)
            return sl, q_ref[sl, :], k_ref[sl, :] * scale, v_ref[sl, :], li_ref[sl, :], lf_ref[sl, :]

        def fwd(ci, carry):
            ct, nrow, m_prev = carry
            ct_all[ci] = ct
            n_all[ci, 0:1, :] = nrow
            m_all[ci, 0:1, 0:1] = m_prev
            _, _, k, v, li, lf = load(ci)
            gates = _chunk_gates(li, lf)
            ct_new, n_new, m_new, _, _, _ = _chunk_state(k, v, li, gates[3], gates[4], ct, nrow, m_prev)
            return ct_new, n_new, m_new

        lax.fori_loop(0, nc, fwd, (jnp.zeros((dh, dh), F32), jnp.zeros((1, dh), F32), jnp.zeros((1, 1), F32)))

        dct_ref[...] = jnp.zeros_like(dct_ref)
        dn_ref[...] = jnp.zeros_like(dn_ref)
        dg_ref[...] = jnp.zeros_like(dg_ref)
        r_idx = lax.broadcasted_iota(jnp.int32, (CHUNK, 1), 0)
        rr = lax.broadcasted_iota(jnp.int32, (CHUNK, CHUNK), 0)
        cc = lax.broadcasted_iota(jnp.int32, (CHUNK, CHUNK), 1)
        tri_t = (cc >= rr).astype(F32)

        def bwd(step, carry):
            ci = nc - 1 - step
            sl, q, k, v, li, lf = load(ci)
            gates = _chunk_gates(li, lf)
            b, b_tot, causal = gates[3], gates[4], gates[5]
            ct, nrow, m_prev = ct_all[ci], n_all[ci, 0:1, :], m_all[ci, 0:1, 0:1]
            h, (dw, iw, a, sm, qc, qn, den, floor, dn) = _chunk_out(q, k, v, gates, ct, nrow, m_prev)
            o = o_ref[sl, :]
            _, (cell, hh, r, sig) = _head_out(h, o, g)
            dy = dy_ref[sl, :]
            dcell = dy * sig
            do_ref[sl, :] = (dy * cell * sig * (1.0 - sig)).astype(do_ref.dtype)
            dh_, dg = _rms_b(dcell, g, hh, r)
            dg_ref[...] += dg
            dnum = dh_ / dn
            ddn = -jnp.sum(dh_ * h, axis=-1, keepdims=True) / dn
            dden = jnp.where(jnp.abs(den) >= floor, ddn * jnp.sign(den), 0.0)
            idn = iw * dnum
            idd = iw * dden
            dq = _dot(idn, ct, NT) + idd * nrow
            dct = _dot(q, idn, TN)
            dnr = jnp.sum(q * idd, axis=0, keepdims=True)
            diw = jnp.sum(dnum * qc, axis=-1, keepdims=True) + dden * qn
            dsm = _dot(dnum, v, NT) + dden
            dv = _dot(sm, dnum, TN)
            da_ = dsm * dw
            e = jnp.where(causal, dsm * a * dw, 0.0)
            dq = dq + _dot(da_, k, NN)
            dk = _dot(da_, q, TN)
            e_cols = jnp.sum(e.T, axis=-1, keepdims=True)
            db = jnp.sum(e, axis=-1, keepdims=True) - e_cols + diw * iw
            dli = e_cols
            _, _, _, dec, w, wv = _chunk_state(k, v, li, b, b_tot, ct, nrow, m_prev)
            dct_new, dn_new = dct_ref[...], dn_ref[0:1, :]
            ddec = jnp.sum(jnp.sum(dct_new * ct, axis=-1, keepdims=True), axis=0, keepdims=True) \
                + jnp.sum(dn_new * nrow, axis=-1, keepdims=True)
            dk = dk + _dot(wv, dct_new, NT) + w * dn_new
            dwv = _dot(k, dct_new, NN)
            dv = dv + w * dwv
            dwt = jnp.sum(dwv * v, axis=-1, keepdims=True) + jnp.sum(k * dn_new, axis=-1, keepdims=True)
            da = dwt * w
            dbtot = jnp.sum(da, axis=0, keepdims=True) + ddec * dec
            db = db - da + jnp.where(r_idx == CHUNK - 1, dbtot, 0.0)
            dli = dli + da
            dlf = _dot_exact(tri_t, jnp.broadcast_to(db, (CHUNK, CHUNK)))[:, 0:1]
            dct_ref[...] = dec * dct_new + dct
            dn_ref[0:1, :] = dec * dn_new + dnr
            dq_ref[sl, :] = dq
            dk_ref[sl, :] = dk * scale
            dv_ref[sl, :] = dv.astype(dv_ref.dtype)
            dli_ref[sl, :] = dli
            dlf_ref[sl, :] = dlf
            return carry

        lax.fori_loop(0, nc, bwd, 0)

    col = lambda off: pl.BlockSpec((t, dh), lambda h: (0, off + h))
    gate = lambda off: pl.BlockSpec((None, t, 1), lambda h: (off + h, 0, 0))
    return pl.pallas_call(
        body, name=name,
        out_shape=[jax.ShapeDtypeStruct((t, wm), F32)] * 2 + [jax.ShapeDtypeStruct((t, wm), BF16)] * 2
        + [jax.ShapeDtypeStruct((heads, t, 1), F32)] * 2 + [jax.ShapeDtypeStruct((1, wm), F32)],
        grid=(heads,),
        in_specs=[col(0), col(0), col(2 * heads), col(3 * heads), gate(0), gate(heads),
                  pl.BlockSpec((1, dh), lambda h: (0, h)), col(0)],
        out_specs=[col(0), col(0), col(0), col(0), gate(0), gate(0), pl.BlockSpec((1, dh), lambda h: (0, h))],
        scratch_shapes=[pltpu.VMEM((nc, dh, dh), F32), pltpu.VMEM((nc, 8, dh), F32), pltpu.VMEM((nc, 8, LANES), F32),
                        pltpu.VMEM((dh, dh), F32), pltpu.VMEM((8, dh), F32)],
        compiler_params=_params("arbitrary"),
    )(qc, kc, proj_a, proj_a, gl, gl, g_mlstm, dhcat)


def _pool_window(gi):
    return jnp.where(gi == 0, 2.0, jnp.where(gi == 1, 4.0, jnp.where(gi == 2, 8.0, 16.0))).astype(F32)


def _pool_select(gi, levels):
    return jnp.where(gi == 0, levels[0], jnp.where(gi == 1, levels[1], jnp.where(gi == 2, levels[2], levels[3])))


def _pooled(u, gi, rows):
    s, levels = u, []
    for lvl in range(POOL_GROUPS):
        s = s + _shift_down(s, 1 << lvl, rows)
        levels.append(s)
    cnt = jnp.minimum((rows + 1).astype(F32), _pool_window(gi))
    return _pool_select(gi, levels) / cnt - u, cnt


def _pool_fwd(proj_b, w_pool, scale, hcat, name):
    t, d_model = hcat.shape
    g, cg, _ = w_pool.shape
    off = (d_model - g * cg) // cg

    def body(u_ref, w_ref, s_ref, hcat_ref, out_ref):
        del hcat_ref
        gi = pl.program_id(0)
        u = u_ref[...]
        rows = lax.broadcasted_iota(jnp.int32, u.shape, 0)
        pooled, _ = _pooled(u, gi, rows)
        out_ref[...] = (_dot(pooled, w_ref[...], NN) * s_ref[...]).astype(out_ref.dtype)

    return pl.pallas_call(
        body, name=name,
        out_shape=jax.ShapeDtypeStruct((t, d_model), BF16),
        grid=(g,),
        in_specs=[pl.BlockSpec((t, cg), lambda gi: (0, gi)), pl.BlockSpec((None, cg, cg), lambda gi: (gi, 0, 0)),
                  pl.BlockSpec((1, cg), lambda gi: (0, gi)), HBM_SPEC],
        out_specs=pl.BlockSpec((t, cg), lambda gi: (0, off + gi)),
        input_output_aliases={3: 0},
        compiler_params=_params("arbitrary"),
    )(proj_b, w_pool, scale, hcat)


def _pool_bwd(proj_b, w_pool, scale, dhcat, name):
    t, d_model = dhcat.shape
    g, cg, _ = w_pool.shape
    off = (d_model - g * cg) // cg

    def body(u_ref, w_ref, s_ref, dy_ref, du_ref, dw_ref, ds_ref):
        gi = pl.program_id(0)
        u, w, dy = u_ref[...], w_ref[...], dy_ref[...]
        rows = lax.broadcasted_iota(jnp.int32, u.shape, 0)
        pooled, cnt = _pooled(u, gi, rows)
        mixed = _dot(pooled, w, NN)
        ds_ref[...] = jnp.sum(dy * mixed, axis=0, keepdims=True)
        dmixed = dy * s_ref[...]
        dw_ref[...] = _dot(pooled, dmixed, TN).astype(dw_ref.dtype)
        dpooled = _dot(dmixed, w, NT)
        s, levels = dpooled / cnt, []
        for lvl in range(POOL_GROUPS):
            s = s + _shift_up(s, 1 << lvl, rows)
            levels.append(s)
        du_ref[...] = (_pool_select(gi, levels) - dpooled).astype(du_ref.dtype)

    return pl.pallas_call(
        body, name=name,
        out_shape=[jax.ShapeDtypeStruct((t, g * cg), BF16), jax.ShapeDtypeStruct((g, cg, cg), BF16),
                   jax.ShapeDtypeStruct((1, g * cg), F32)],
        grid=(g,),
        in_specs=[pl.BlockSpec((t, cg), lambda gi: (0, gi)), pl.BlockSpec((None, cg, cg), lambda gi: (gi, 0, 0)),
                  pl.BlockSpec((1, cg), lambda gi: (0, gi)), pl.BlockSpec((t, cg), lambda gi: (0, off + gi))],
        out_specs=[pl.BlockSpec((t, cg), lambda gi: (0, gi)), pl.BlockSpec((None, cg, cg), lambda gi: (gi, 0, 0)),
                   pl.BlockSpec((1, cg), lambda gi: (0, gi))],
        compiler_params=_params("arbitrary"),
    )(proj_b, w_pool, scale, dhcat)


def _pad_cols(a, width):
    return jnp.pad(a, ((0, 0), (0, width - a.shape[1])))


def kernel(x, p, w_in, b_gates, w_qk_conv, g_mlstm, w_pool, pool_scale, w_out, g_mix_pre, g_mix_post, w_ff1, w_ff2, g_ff_pre, g_ff_post, w_ple_proj, w_ple_gate, g_ple_gate, g_ple_post, loss_target, m_w_in, m_b_gates, m_w_qk_conv, m_g_mlstm, m_w_pool, m_pool_scale, m_w_out, m_g_mix_pre, m_g_mix_post, m_w_ff1, m_w_ff2, m_g_ff_pre, m_g_ff_post, m_w_ple_proj, m_w_ple_gate, m_g_ple_gate, m_g_ple_post, v_w_in, v_b_gates, v_w_qk_conv, v_g_mlstm, v_w_pool, v_pool_scale, v_w_out, v_g_mix_pre, v_g_mix_post, v_w_ff1, v_w_ff2, v_g_ff_pre, v_g_ff_post, v_w_ple_proj, v_w_ple_gate, v_g_ple_gate, v_g_ple_post):
    weights = dict(w_in=w_in, b_gates=b_gates, w_qk_conv=w_qk_conv, g_mlstm=g_mlstm, w_pool=w_pool,
                   pool_scale=pool_scale, w_out=w_out, g_mix_pre=g_mix_pre, g_mix_post=g_mix_post, w_ff1=w_ff1,
                   w_ff2=w_ff2, g_ff_pre=g_ff_pre, g_ff_post=g_ff_post, w_ple_proj=w_ple_proj,
                   w_ple_gate=w_ple_gate, g_ple_gate=g_ple_gate, g_ple_post=g_ple_post)
    mom1 = dict(w_in=m_w_in, b_gates=m_b_gates, w_qk_conv=m_w_qk_conv, g_mlstm=m_g_mlstm, w_pool=m_w_pool,
                pool_scale=m_pool_scale, w_out=m_w_out, g_mix_pre=m_g_mix_pre, g_mix_post=m_g_mix_post,
                w_ff1=m_w_ff1, w_ff2=m_w_ff2, g_ff_pre=m_g_ff_pre, g_ff_post=m_g_ff_post,
                w_ple_proj=m_w_ple_proj, w_ple_gate=m_w_ple_gate, g_ple_gate=m_g_ple_gate, g_ple_post=m_g_ple_post)
    mom2 = dict(w_in=v_w_in, b_gates=v_b_gates, w_qk_conv=v_w_qk_conv, g_mlstm=v_g_mlstm, w_pool=v_w_pool,
                pool_scale=v_pool_scale, w_out=v_w_out, g_mix_pre=v_g_mix_pre, g_mix_post=v_g_mix_post,
                w_ff1=v_w_ff1, w_ff2=v_w_ff2, g_ff_pre=v_g_ff_pre, g_ff_post=v_g_ff_post,
                w_ple_proj=v_w_ple_proj, w_ple_gate=v_w_ple_gate, g_ple_gate=v_g_ple_gate, g_ple_post=v_g_ple_post)
    order = list(weights)

    t, d = x.shape[1], x.shape[2]
    heads = MLSTM_HEADS
    wm = g_mlstm.shape[1]
    pw = pool_scale.shape[1]
    cg = pw // POOL_GROUPS
    in_cols = 4 * wm + 2 * heads + pw
    in_shard = w_in.shape[2]
    xs, ps, tgt = x[0], p[0, 0], loss_target[0]

    mx, my, mc = lax.axis_index("x"), lax.axis_index("y"), lax.axis_index("c")
    core = jnp.reshape(mc, (1,)).astype(jnp.int32)
    chip = jnp.reshape(2 * mx + my, (1,)).astype(jnp.int32)

    shards = [w_in[0].astype(BF16), w_qk_conv[0], w_pool[0].reshape(POOL_GROUPS * w_pool.shape[2], cg).astype(BF16),
              w_out[0].astype(BF16), w_ff1[0].astype(BF16), w_ff2[0].astype(BF16), w_ple_proj[0].astype(BF16),
              w_ple_gate[0].astype(BF16)]
    g_in, g_conv, g_pool, g_out, g_ff1, g_ff2, g_pproj, g_pgate = _all_gather(shards, "gather_weights")
    full_in = g_in.transpose(1, 0, 2).reshape(d, in_cols)
    w_a = full_in[:, :4 * wm]
    w_b = _pad_cols(jnp.concatenate([full_in[:, 4 * wm + 2 * heads:], full_in[:, 4 * wm:4 * wm + 2 * heads]], axis=1),
                    pw + LANES)
    conv_w = g_conv.transpose(1, 0, 2).reshape(QK_CONV, 2 * wm)
    pool_w = g_pool.reshape(N_DEV, POOL_GROUPS, cg // N_DEV, cg).transpose(1, 0, 2, 3).reshape(POOL_GROUPS, cg, cg)
    out_w = g_out.reshape(d, d)
    ff2_w = g_ff2.reshape(-1, d)
    pgate_w = g_pgate.reshape(d, d)
    bias = _pad_cols(b_gates, LANES)

    xn = _norm_fwd(xs, g_mix_pre, "norm_mix_pre")
    proj_a = _mm_nn("proj_qkvo", xn, w_a)[0]
    proj_b = _mm_nn("proj_pool_gates", xn, w_b)[0]
    qc = _conv_fwd(proj_a, conv_w[:, :wm], 0, "q_conv")
    kc = _conv_fwd(proj_a, conv_w[:, wm:], wm, "k_conv")
    gates = _gates_fwd(proj_b, bias, pw // LANES, heads, "gates")
    gl = gates[:, :2 * heads].T.reshape(2 * heads, t, 1)
    hcat = _mlstm_fwd(qc, kc, proj_a, gl, g_mlstm, heads, d, "mlstm")
    hcat = _pool_fwd(proj_b, pool_w, pool_scale, hcat, "pool")
    mix = _mm_nn("mix_out", hcat, out_w)[0]
    h1, hn = _block_fwd(xs, mix, g_mix_post, g_ff_pre, "residual_mix")
    relu_a, act = _mm_nn_pieces("ff1", hn, g_ff1, out_dtype=[BF16, BF16],
                                epilogue=lambda acc: (jnp.maximum(acc, 0.0), jnp.square(jnp.maximum(acc, 0.0))))
    ff = _mm_nn("ff2", act, ff2_w)[0]
    h2, hg = _block_fwd(h1, ff, g_ff_post, g_ple_gate, "residual_ff")
    z = _mm_nn("ple_gate", hg, pgate_w)[0]
    e = _mm_nn_pieces("ple_proj", ps, g_pproj)[0]
    dh3, dz, de, dg_ple_post, loss_part = _loss_head(h2, z, e, tgt, g_ple_post, "loss_head")
    loss = lax.psum(loss_part[0, 0], MESH_AXES)

    dhg = _mm_nt("d_ple_gate_in", dz, pgate_w)
    gw_pgate = _mm_tn("gw_ple_gate", hg, dz)
    gw_pproj = _mm_tn("gw_ple_proj", ps, de, pieces=N_DEV)
    dh2, dff, dg_ff_post, dg_ple_gate = _block_bwd(h2, ff, dh3, dhg, g_ff_post, g_ple_gate, "d_residual_ff")
    da = _mm_nt("d_ff2_in", dff, ff2_w, out_dtype=BF16, extra=[relu_a],
                epilogue=lambda acc, r: (acc * (2.0 * r.astype(F32)),))
    gw_ff2 = _mm_tn("gw_ff2", act, dff)
    dhn = _mm_nt_pieces("d_ff1_in", da, g_ff1)
    gw_ff1 = _mm_tn("gw_ff1", hn, da, pieces=N_DEV)
    dh1, dmix, dg_mix_post, dg_ff_pre = _block_bwd(h1, mix, dh2, dhn, g_mix_post, g_ff_pre, "d_residual_mix")
    dhcat = _mm_nt("d_mix_in", dmix, out_w)
    gw_out = _mm_tn("gw_out", hcat, dmix)
    du, gw_pool, dg_pool_scale = _pool_bwd(proj_b, pool_w, pool_scale, dhcat, "d_pool")
    dq, dk, dv, do, dli, dlf, dg_mlstm = _mlstm_bwd(qc, kc, proj_a, gl, g_mlstm, dhcat, heads, "d_mlstm")
    dgl = jnp.concatenate([dli, dlf], axis=0).reshape(2 * heads, t).T
    dgates, dbias = _gates_bwd(proj_b, bias, _pad_cols(dgl, LANES), pw // LANES, heads, "d_gates")
    dq_pre, gw_conv_q = _conv_bwd(proj_a, conv_w[:, :wm], dq, 0, "d_q_conv")
    dk_pre, gw_conv_k = _conv_bwd(proj_a, conv_w[:, wm:], dk, wm, "d_k_conv")
    gw_conv = jnp.concatenate([gw_conv_q, gw_conv_k], axis=1)
    dproj_a = jnp.concatenate([dq_pre, dk_pre, dv, do], axis=1)
    dproj_b = jnp.concatenate([du, dgates], axis=1)
    dxn_a = _mm_nt("d_proj_qkvo_in", dproj_a, w_a)
    dxn = _mm_nt("d_proj_in", dproj_b, w_b, extra=[dxn_a], epilogue=lambda acc, prev: (acc + prev,))
    gw_a = _mm_tn("gw_in_qkvo", xn, dproj_a)
    gw_b = _mm_tn("gw_in_pool_gates", xn, dproj_b)
    grad_x, dg_mix_pre = _norm_bwd(xs, dxn, dh1, g_mix_pre, "d_norm_mix_pre")

    gw_in = jnp.concatenate([gw_a, gw_b[:, pw:pw + 2 * heads], gw_b[:, :pw]], axis=1)
    pieces = {
        "w_in": gw_in.reshape(d, N_DEV, in_shard).transpose(1, 0, 2),
        "w_qk_conv": gw_conv.reshape(QK_CONV, N_DEV, -1).transpose(1, 0, 2).astype(BF16),
        "w_pool": gw_pool.reshape(POOL_GROUPS, N_DEV, cg // N_DEV, cg).transpose(1, 0, 2, 3).reshape(N_DEV, -1, cg),
        "w_out": gw_out.reshape(N_DEV, d // N_DEV, d),
        "w_ff1": gw_ff1,
        "w_ff2": gw_ff2.reshape(N_DEV, -1, d),
        "w_ple_proj": gw_pproj,
        "w_ple_gate": gw_pgate.reshape(N_DEV, d // N_DEV, d),
    }
    sharded = list(pieces)
    got_pair = _pair_exchange([pieces[k] for k in sharded], "reduce_pair")
    sums = [_pair_sum(pieces[k], got, core, "pair_sum_" + k) for k, got in zip(sharded, got_pair)]
    got_chip = _chip_exchange(sums, "reduce_chips")

    small = ["b_gates", "g_mlstm", "pool_scale", "g_mix_pre", "g_mix_post", "g_ff_pre", "g_ff_post", "g_ple_gate",
             "g_ple_post"]
    small_grads = dict(b_gates=dbias[:, :2 * heads], g_mlstm=dg_mlstm, pool_scale=dg_pool_scale, g_mix_pre=dg_mix_pre,
                       g_mix_post=dg_mix_post, g_ff_pre=dg_ff_pre, g_ff_post=dg_ff_post, g_ple_gate=dg_ple_gate,
                       g_ple_post=dg_ple_post)
    pack = lambda src: jnp.concatenate([_pad_cols(src[k], d) for k in small], axis=0)
    small_parts = _all_gather([_pad_rows16(pack(small_grads))], "gather_small_grads")[0]

    out = {}
    for k, s, got in zip(sharded, sums, got_chip):
        shape = weights[k].shape
        w2, m2, v2 = (a.reshape(s.shape[1:]) for a in (weights[k], mom1[k], mom2[k]))
        res = _adamw_shard(w2, m2, v2, s, got, chip, "adamw_" + k)
        out[k] = [r.reshape(shape) for r in res]
    res = _adamw_replicated(_pad_rows16(pack(weights)), _pad_rows16(pack(mom1)), _pad_rows16(pack(mom2)), small_parts,
                            "adamw_replicated")
    for i, k in enumerate(small):
        width = weights[k].shape[1]
        out[k] = [r[i:i + 1, :width] for r in res]

    return (loss, grad_x[None], *[out[k][0] for k in order], *[out[k][1] for k in order],
            *[out[k][2] for k in order], *[out[k][3] for k in order])


def _pad_rows16(a):
    return jnp.pad(a, ((0, 16 - a.shape[0]), (0, 0)))
```

```python
import functools

import jax
import jax.numpy as jnp
from jax import lax
from jax.experimental import pallas as pl
from jax.experimental.pallas import tpu as pltpu

F32 = jnp.float32
BF16 = jnp.bfloat16
EPS = 1e-6
N_DEV = 8
N_CHIP = 4
LANES = 128
VMEM_LIMIT = 56 * 1024 * 1024
MLSTM_HEADS = 8
POOL_GROUPS = 4
QK_CONV = 4
CHUNK = 128
ROW_TILE = 256
ADAM_LR, ADAM_B1, ADAM_B2, ADAM_EPS, ADAM_WD, ADAM_STEP = 0.001, 0.9, 0.999, 1e-08, 0.01, 10
MESH_AXES = ("x", "y", "c")
MESH = pl.DeviceIdType.MESH
HBM_SPEC = pl.BlockSpec(memory_space=pltpu.HBM)
SEM_SPEC = pl.BlockSpec(memory_space=pltpu.SEMAPHORE)
DATAFLOW = pltpu.SideEffectType.DATAFLOW_SIDE_EFFECTING


def _params(*sem):
    if sem:
        return pltpu.CompilerParams(dimension_semantics=sem, vmem_limit_bytes=VMEM_LIMIT)
    return pltpu.CompilerParams(vmem_limit_bytes=VMEM_LIMIT)


def _tile(n, pref):
    if n <= pref:
        return n
    t = (pref // LANES) * LANES
    while t >= LANES:
        if n % t == 0:
            return t
        t -= LANES
    return n


def _all_gather(arrs, name):
    n = len(arrs)

    def body(*refs):
        ins, outs = refs[:n], refs[n:2 * n]
        send, recv, loc = refs[2 * n:]
        x, y, c = lax.axis_index("x"), lax.axis_index("y"), lax.axis_index("c")
        me = (x, y, c)
        sibling = (x, y, 1 - c)
        chips = [(1 - x, y), (x, 1 - y), (1 - x, 1 - y)]

        def row(dev):
            return 4 * dev[0] + 2 * dev[1] + dev[2]

        def copy(w, k, block_of, to, src=None):
            dst = outs[w].at[row(block_of)]
            return pltpu.make_async_remote_copy(
                src_ref=dst if src is None else src, dst_ref=dst,
                send_sem=send.at[7 * w + k], recv_sem=recv.at[7 * w + k],
                device_id=to, device_id_type=MESH)

        started = []
        local = []
        for w in range(n):
            cp = pltpu.make_async_copy(ins[w], outs[w].at[row(me)], loc.at[w])
            cp.start()
            local.append(cp)
            first = [copy(w, 0, me, sibling, src=ins[w])]
            first += [copy(w, 1 + j, me, (*chip, c), src=ins[w]) for j, chip in enumerate(chips)]
            for cp in first:
                cp.start()
            started += first
        for j, chip in enumerate(chips):
            for w in range(n):
                copy(w, 1 + j, (*chip, c), me).wait_recv()
                fwd = copy(w, 4 + j, (*chip, c), sibling)
                fwd.start()
                started.append(fwd)
        for w in range(n):
            copy(w, 0, sibling, me).wait_recv()
            for j, chip in enumerate(chips):
                copy(w, 4 + j, (*chip, 1 - c), me).wait_recv()
        for cp in started:
            cp.wait_send()
        for cp in local:
            cp.wait()

    outs = pl.pallas_call(
        body, name=name,
        out_shape=[jax.ShapeDtypeStruct((N_DEV,) + a.shape, a.dtype) for a in arrs],
        in_specs=[HBM_SPEC] * n, out_specs=[HBM_SPEC] * n,
        scratch_shapes=[pltpu.SemaphoreType.DMA((7 * n,)), pltpu.SemaphoreType.DMA((7 * n,)),
                        pltpu.SemaphoreType.DMA((n,))],
    )(*arrs)
    return list(outs)


def _pair_exchange(pieces, name):
    n = len(pieces)

    def body(*refs):
        ins, outs = refs[:n], refs[n:2 * n]
        send, recv = refs[2 * n:]
        x, y, c = lax.axis_index("x"), lax.axis_index("y"), lax.axis_index("c")
        sibling = (x, y, 1 - c)

        def copy(w, chip, src_core):
            return pltpu.make_async_remote_copy(
                src_ref=ins[w].at[2 * chip + src_core], dst_ref=outs[w].at[chip],
                send_sem=send.at[N_CHIP * w + chip], recv_sem=recv.at[N_CHIP * w + chip],
                device_id=sibling, device_id_type=MESH)

        cps = [copy(w, chip, 1 - c) for w in range(n) for chip in range(N_CHIP)]
        for cp in cps:
            cp.start()
        for cp in cps:
            cp.wait_recv()
        for cp in cps:
            cp.wait_send()

    outs = pl.pallas_call(
        body, name=name,
        out_shape=[jax.ShapeDtypeStruct((N_CHIP,) + a.shape[1:], a.dtype) for a in pieces],
        in_specs=[HBM_SPEC] * n, out_specs=[HBM_SPEC] * n,
        scratch_shapes=[pltpu.SemaphoreType.DMA((N_CHIP * n,)), pltpu.SemaphoreType.DMA((N_CHIP * n,))],
    )(*pieces)
    return list(outs)


def _hbm(a):
    return pltpu.with_memory_space_constraint(a, pltpu.HBM)


def _mesh_place():
    x, y, c = lax.axis_index("x"), lax.axis_index("y"), lax.axis_index("c")
    return x, y, c, [(1 - x, y), (x, 1 - y), (1 - x, 1 - y)]


def _chip_copies(srcs, lands, send, recv):
    x, y, c, chips = _mesh_place()
    return [pltpu.make_async_remote_copy(
        src_ref=srcs[w].at[2 * chip[0] + chip[1]], dst_ref=lands[w].at[k],
        send_sem=send.at[3 * w + k], recv_sem=recv.at[3 * w + k],
        device_id=(*chip, c), device_id_type=MESH)
        for k, chip in enumerate(chips) for w in range(len(srcs))]


def _chip_start(sums, name):
    n = len(sums)
    lands = [lax.empty((3,) + a.shape[1:], a.dtype) for a in sums]

    def body(*refs):
        srcs, zones = refs[:n], refs[n:2 * n]
        send, recv = refs[2 * n], refs[2 * n + 1]
        token = refs[-1]
        for cp in _chip_copies(srcs, zones, send, recv):
            cp.start()
        token[...] = jnp.zeros_like(token)

    res = pl.pallas_call(
        body, name=name,
        out_shape=[pltpu.SemaphoreType.DMA((3 * n,)), pltpu.SemaphoreType.DMA((3 * n,))]
        + [pltpu.HBM(a.shape, a.dtype) for a in sums] + [pltpu.HBM(a.shape, a.dtype) for a in lands]
        + [jax.ShapeDtypeStruct((8, LANES), F32)],
        in_specs=[HBM_SPEC] * (2 * n),
        out_specs=[SEM_SPEC, SEM_SPEC] + [HBM_SPEC] * (2 * n) + [pl.BlockSpec(memory_space=pltpu.VMEM)],
        input_output_aliases={i: 2 + i for i in range(2 * n)},
        compiler_params=pltpu.CompilerParams(has_side_effects=DATAFLOW),
    )(*[_hbm(a) for a in sums], *[_hbm(a) for a in lands])
    return dict(send=res[0], recv=res[1], srcs=list(res[2:2 + n]), lands=list(res[2 + n:2 + 2 * n]), token=res[-1])


def _chip_wait(started, after, name):
    n = len(started["srcs"])

    def body(*refs):
        srcs, zones = refs[:n], refs[n:2 * n]
        send, recv = refs[2 * n], refs[2 * n + 1]
        for cp in _chip_copies(srcs, zones, send, recv):
            cp.wait_send()
            cp.wait_recv()

    res = pl.pallas_call(
        body, name=name,
        out_shape=[pltpu.HBM(a.shape, a.dtype) for a in started["srcs"] + started["lands"]],
        in_specs=[HBM_SPEC] * (2 * n) + [SEM_SPEC, SEM_SPEC, pl.BlockSpec(memory_space=pl.ANY)],
        out_specs=[HBM_SPEC] * (2 * n),
        input_output_aliases={i: i for i in range(2 * n)},
        compiler_params=pltpu.CompilerParams(has_side_effects=DATAFLOW),
    )(*started["srcs"], *started["lands"], started["send"], started["recv"], after)
    return list(res[:n]), list(res[n:])


def _gather_copies(srcs, lands, send, recv):
    x, y, c, chips = _mesh_place()
    me = 4 * x + 2 * y + c
    targets = [(x, y, 1 - c)] + [(*chip, c) for chip in chips]
    return [pltpu.make_async_remote_copy(
        src_ref=srcs[w], dst_ref=lands[w].at[me], send_sem=send.at[4 * w + k], recv_sem=recv.at[4 * w + k],
        device_id=to, device_id_type=MESH)
        for w in range(len(srcs)) for k, to in enumerate(targets)]


def _gather_start(groups, name):
    sizes = [len(g) for g in groups]
    flat = [a for g in groups for a in g]
    n, ng = len(flat), len(groups)
    me = 4 * lax.axis_index("x") + 2 * lax.axis_index("y") + lax.axis_index("c")
    lands = [lax.dynamic_update_slice(lax.empty((N_DEV,) + a.shape, a.dtype), a[None], (me,) + (0,) * a.ndim)
             for a in flat]

    def body(*refs):
        srcs, zones = refs[:n], refs[n:2 * n]
        sems = refs[2 * n:2 * n + 2 * ng]
        token = refs[-1]
        lo = 0
        for gi, size in enumerate(sizes):
            for cp in _gather_copies(srcs[lo:lo + size], zones[lo:lo + size], sems[2 * gi], sems[2 * gi + 1]):
                cp.start()
            lo += size
        token[...] = jnp.zeros_like(token)

    sem_shapes = []
    for size in sizes:
        sem_shapes += [pltpu.SemaphoreType.DMA((4 * size,))] * 2
    res = pl.pallas_call(
        body, name=name,
        out_shape=sem_shapes + [pltpu.HBM(a.shape, a.dtype) for a in flat + lands] + [jax.ShapeDtypeStruct((8, LANES), F32)],
        in_specs=[HBM_SPEC] * (2 * n),
        out_specs=[SEM_SPEC] * (2 * ng) + [HBM_SPEC] * (2 * n) + [pl.BlockSpec(memory_space=pltpu.VMEM)],
        input_output_aliases={i: 2 * ng + i for i in range(2 * n)},
        compiler_params=pltpu.CompilerParams(has_side_effects=DATAFLOW),
    )(*[_hbm(a) for a in flat], *[_hbm(a) for a in lands])
    out, lo = [], 0
    for gi, size in enumerate(sizes):
        out.append(dict(send=res[2 * gi], recv=res[2 * gi + 1],
                        srcs=list(res[2 * ng + lo:2 * ng + lo + size]),
                        lands=list(res[2 * ng + n + lo:2 * ng + n + lo + size])))
        lo += size
    return out, res[-1]


def _gather_wait(started, after, name):
    n = len(started["srcs"])

    def body(*refs):
        srcs, zones = refs[:n], refs[n:2 * n]
        send, recv = refs[2 * n], refs[2 * n + 1]
        for cp in _gather_copies(srcs, zones, send, recv):
            cp.wait_send()
            cp.wait_recv()

    res = pl.pallas_call(
        body, name=name,
        out_shape=[pltpu.HBM(a.shape, a.dtype) for a in started["srcs"] + started["lands"]],
        in_specs=[HBM_SPEC] * (2 * n) + [SEM_SPEC, SEM_SPEC, pl.BlockSpec(memory_space=pl.ANY)],
        out_specs=[HBM_SPEC] * (2 * n),
        input_output_aliases={i: i for i in range(2 * n)},
        compiler_params=pltpu.CompilerParams(has_side_effects=DATAFLOW),
    )(*started["srcs"], *started["lands"], started["send"], started["recv"], after)
    return list(res[n:])


def _forward_pair(lands, name):
    n = len(lands)

    def body(*refs):
        zones = refs[:n]
        send, recv = refs[2 * n:]
        x, y, c, chips = _mesh_place()

        def copy(w, j, core):
            rows = zones[w].at[4 * chips[j][0] + 2 * chips[j][1] + core]
            return pltpu.make_async_remote_copy(
                src_ref=rows, dst_ref=rows, send_sem=send.at[3 * w + j], recv_sem=recv.at[3 * w + j],
                device_id=(x, y, 1 - c), device_id_type=MESH)

        sends = [copy(w, j, c) for w in range(n) for j in range(3)]
        for cp in sends:
            cp.start()
        for w in range(n):
            for j in range(3):
                copy(w, j, 1 - c).wait_recv()
        for cp in sends:
            cp.wait_send()

    return list(pl.pallas_call(
        body, name=name,
        out_shape=[jax.ShapeDtypeStruct(a.shape, a.dtype) for a in lands],
        in_specs=[HBM_SPEC] * n, out_specs=[HBM_SPEC] * n,
        input_output_aliases={i: i for i in range(n)},
        scratch_shapes=[pltpu.SemaphoreType.DMA((3 * n,)), pltpu.SemaphoreType.DMA((3 * n,))],
    )(*lands))


def _pair_sum(pieces, got, core, name):
    _, rows, cols = pieces.shape
    tr, tc = _tile_2d(rows, cols)

    def body(core_ref, mine_ref, got_ref, out_ref):
        del core_ref
        out_ref[...] = (mine_ref[...].astype(F32) + got_ref[...].astype(F32)).astype(out_ref.dtype)

    return pl.pallas_call(
        body, name=name,
        out_shape=jax.ShapeDtypeStruct((N_CHIP, rows, cols), pieces.dtype),
        grid_spec=pltpu.PrefetchScalarGridSpec(
            num_scalar_prefetch=1, grid=(N_CHIP, rows // tr, cols // tc),
            in_specs=[pl.BlockSpec((None, tr, tc), lambda ch, i, j, core: (2 * ch + core[0], i, j)),
                      pl.BlockSpec((None, tr, tc), lambda ch, i, j, core: (ch, i, j))],
            out_specs=pl.BlockSpec((None, tr, tc), lambda ch, i, j, core: (ch, i, j))),
        compiler_params=_params("parallel", "parallel", "parallel"),
    )(core, pieces, got)


def _tile_2d(rows, cols, budget=2 * 1024 * 1024):
    want = max(16, budget // (4 * cols))
    if rows <= want:
        return rows, cols
    t = (want // 16) * 16
    while t >= 16:
        if rows % t == 0:
            return t, cols
        t -= 16
    return rows, _tile(cols, max(LANES, budget // (4 * rows)))


def _adamw_math(w, g, m, v):
    m = ADAM_B1 * m + (1.0 - ADAM_B1) * g
    v = ADAM_B2 * v + (1.0 - ADAM_B2) * (g * g)
    m_hat = m / (1.0 - ADAM_B1 ** ADAM_STEP)
    v_hat = v / (1.0 - ADAM_B2 ** ADAM_STEP)
    delta = -ADAM_LR * (m_hat / (jnp.sqrt(v_hat) + ADAM_EPS) + ADAM_WD * w)
    return delta, m, v


def _adamw_shard(w, m, v, sums, got, chip, name):
    rows, cols = w.shape
    tr, tc = _tile_2d(rows, cols, budget=1024 * 1024)

    def body(chip_ref, w_ref, m_ref, v_ref, own_ref, got_ref, g_out, d_out, m_out, v_out):
        del chip_ref
        g = own_ref[...].astype(F32)
        for k in range(3):
            g = g + got_ref[k].astype(F32)
        delta, m_new, v_new = _adamw_math(w_ref[...], g, m_ref[...], v_ref[...])
        g_out[...] = g
        d_out[...] = delta
        m_out[...] = m_new
        v_out[...] = v_new

    blk = pl.BlockSpec((tr, tc), lambda i, j, chip: (i, j))
    return pl.pallas_call(
        body, name=name,
        out_shape=[jax.ShapeDtypeStruct((rows, cols), F32)] * 4,
        grid_spec=pltpu.PrefetchScalarGridSpec(
            num_scalar_prefetch=1, grid=(rows // tr, cols // tc),
            in_specs=[blk, blk, blk,
                      pl.BlockSpec((None, tr, tc), lambda i, j, chip: (chip[0], i, j)),
                      pl.BlockSpec((3, tr, tc), lambda i, j, chip: (0, i, j))],
            out_specs=[blk] * 4),
        compiler_params=_params("parallel", "parallel"),
    )(chip, w, m, v, sums, got)


def _adamw_replicated(w, m, v, parts, name):
    rows, cols = w.shape

    def body(w_ref, m_ref, v_ref, parts_ref, g_out, d_out, m_out, v_out):
        g = parts_ref[0]
        for d in range(1, N_DEV):
            g = g + parts_ref[d]
        delta, m_new, v_new = _adamw_math(w_ref[...], g, m_ref[...], v_ref[...])
        g_out[...] = g
        d_out[...] = delta
        m_out[...] = m_new
        v_out[...] = v_new

    return pl.pallas_call(
        body, name=name,
        out_shape=[jax.ShapeDtypeStruct((rows, cols), F32)] * 4,
        compiler_params=_params(),
    )(w, m, v, parts)


NN = ((1,), (0,))
NT = ((1,), (1,))
TN = ((0,), (0,))


ANY_SPEC = pl.BlockSpec(memory_space=pl.ANY)


def _mm(name, a, b, *, dims, grid, a_spec, b_spec, outs, extra=(), extra_specs=(), epilogue=None, acc_shape=None,
        deps=()):
    nk = grid[2]
    n_extra, n_out = len(extra), len(outs)
    first_out = 2 + n_extra + len(deps)

    def body(*refs):
        a_ref, b_ref = refs[0], refs[1]
        ex = refs[2:2 + n_extra]
        out_refs = refs[first_out:first_out + n_out]
        part = lax.dot_general(a_ref[...].astype(BF16), b_ref[...].astype(BF16), (dims, ((), ())),
                               preferred_element_type=F32)

        def finish(acc):
            res = epilogue(acc, *[e[...] for e in ex]) if epilogue else (acc,)
            for val, o in zip(res, out_refs):
                o[...] = val.astype(o.dtype)

        if nk == 1:
            finish(part)
        else:
            acc_ref = refs[-1]
            k = pl.program_id(2)

            @pl.when(k == 0)
            def _():
                acc_ref[...] = part

            @pl.when(k > 0)
            def _():
                acc_ref[...] += part

            @pl.when(k == nk - 1)
            def _():
                finish(acc_ref[...])

    res = pl.pallas_call(
        body, name=name,
        out_shape=[jax.ShapeDtypeStruct(s, d) for s, d, _ in outs],
        grid=grid,
        in_specs=[a_spec, b_spec, *extra_specs] + [ANY_SPEC] * len(deps),
        out_specs=[sp for _, _, sp in outs],
        scratch_shapes=[pltpu.VMEM(acc_shape, F32)] if nk > 1 else [],
        compiler_params=_params("parallel", "parallel", "arbitrary"),
    )(a, b, *extra, *deps)
    return list(res)


def _mm_nn(name, a, b, out_dtype=F32, tm=1024, tn=1024, tk=2048, epilogue=None, n_out=1, extra=(), deps=()):
    m, kd = a.shape
    n = b.shape[1]
    tm, tn, tk = _tile(m, tm), _tile(n, tn), _tile(kd, tk)
    dts = out_dtype if isinstance(out_dtype, (list, tuple)) else [out_dtype] * n_out
    o_spec = pl.BlockSpec((tm, tn), lambda i, j, k: (i, j))
    return _mm(name, a, b, dims=NN, grid=(m // tm, n // tn, kd // tk),
               a_spec=pl.BlockSpec((tm, tk), lambda i, j, k: (i, k)),
               b_spec=pl.BlockSpec((tk, tn), lambda i, j, k: (k, j)),
               outs=[((m, n), dt, o_spec) for dt in dts], extra=extra, extra_specs=[o_spec] * len(extra),
               epilogue=epilogue, acc_shape=(tm, tn), deps=deps)


def _mm_nn_pieces(name, a, b, out_dtype=F32, tm=1024, tk=2048, epilogue=None, n_out=1):
    m, kd = a.shape
    npc, _, ns = b.shape
    tm, tk = _tile(m, tm), _tile(kd, tk)
    tn = _tile(ns, 1024)
    r = ns // tn
    dts = out_dtype if isinstance(out_dtype, (list, tuple)) else [out_dtype] * n_out
    o_spec = pl.BlockSpec((tm, tn), lambda i, j, k: (i, j))
    return _mm(name, a, b, dims=NN, grid=(m // tm, npc * r, kd // tk),
               a_spec=pl.BlockSpec((tm, tk), lambda i, j, k: (i, k)),
               b_spec=pl.BlockSpec((None, tk, tn), lambda i, j, k: (j // r, k, j % r)),
               outs=[((m, npc * ns), dt, o_spec) for dt in dts], epilogue=epilogue, acc_shape=(tm, tn))


def _mm_nt(name, a, b, out_dtype=F32, tm=1024, tn=1024, tk=2048, epilogue=None, extra=(), n_rows=None, deps=()):
    m, kd = a.shape
    n = n_rows or b.shape[0]
    tm, tn, tk = _tile(m, tm), _tile(n, tn), _tile(kd, tk)
    o_spec = pl.BlockSpec((tm, tn), lambda i, j, k: (i, j))
    return _mm(name, a, b, dims=NT, grid=(m // tm, n // tn, kd // tk),
               a_spec=pl.BlockSpec((tm, tk), lambda i, j, k: (i, k)),
               b_spec=pl.BlockSpec((tn, tk), lambda i, j, k: (j, k)),
               outs=[((m, n), out_dtype, o_spec)], extra=extra, extra_specs=[o_spec] * len(extra),
               epilogue=epilogue, acc_shape=(tm, tn), deps=deps)[0]


def _mm_nt_pieces(name, a, b, out_dtype=F32, tm=1024, tn=1024, deps=()):
    m = a.shape[0]
    npc, n, ks = b.shape
    tm, tn = _tile(m, tm), _tile(n, tn)
    tk = _tile(ks, 1024)
    r = ks // tk
    return _mm(name, a, b, dims=NT, grid=(m // tm, n // tn, npc * r),
               a_spec=pl.BlockSpec((tm, tk), lambda i, j, k: (i, k)),
               b_spec=pl.BlockSpec((None, tn, tk), lambda i, j, k: (k // r, j, k % r)),
               outs=[((m, n), out_dtype, pl.BlockSpec((tm, tn), lambda i, j, k: (i, j)))],
               acc_shape=(tm, tn), deps=deps)[0]


def _mm_tn(name, a, b, out_dtype=BF16, tm=1024, tn=1024, pieces=0):
    t, m = a.shape
    n = b.shape[1]
    tm = _tile(m, tm)
    if pieces:
        ns = n // pieces
        tn = _tile(ns, tn)
        r = ns // tn
        out = ((pieces, m, ns), out_dtype, pl.BlockSpec((None, tm, tn), lambda i, j, k: (j // r, i, j % r)))
    else:
        tn = _tile(n, tn)
        out = ((m, n), out_dtype, pl.BlockSpec((tm, tn), lambda i, j, k: (i, j)))
    return _mm(name, a, b, dims=TN, grid=(m // tm, n // tn, 1),
               a_spec=pl.BlockSpec((t, tm), lambda i, j, k: (0, i)),
               b_spec=pl.BlockSpec((t, tn), lambda i, j, k: (0, j)),
               outs=[out])[0]


def _rms_f(x, g):
    r = lax.rsqrt(jnp.mean(x * x, axis=-1, keepdims=True) + EPS)
    xh = x * r
    return xh * g, xh, r


def _rms_b(dy, g, xh, r):
    dxh = dy * g
    dx = r * (dxh - xh * jnp.mean(dxh * xh, axis=-1, keepdims=True))
    dg = jnp.sum(dy * xh, axis=0, keepdims=True)
    return dx, dg


def _rowwise(name, fn, row_ins, vec_ins, row_outs, vec_outs, deps=()):
    t = row_ins[0].shape[0]
    tr = min(ROW_TILE, t)
    nr, nv, no = len(row_ins), len(vec_ins), len(row_outs)
    first_out = nr + nv + len(deps)

    def body(*refs):
        ri, vi = refs[:nr], refs[nr:nr + nv]
        ro, vo = refs[first_out:first_out + no], refs[first_out + no:]
        routs, vouts = fn(*[r[...] for r in ri], *[v[...] for v in vi])
        for o, val in zip(ro, routs):
            o[...] = val.astype(o.dtype)
        i = pl.program_id(0)
        for o, val in zip(vo, vouts):
            @pl.when(i == 0)
            def _():
                o[...] = val

            @pl.when(i > 0)
            def _():
                o[...] += val

    res = pl.pallas_call(
        body, name=name,
        out_shape=[jax.ShapeDtypeStruct((t, w), d) for w, d in row_outs]
        + [jax.ShapeDtypeStruct((1, w), F32) for w in vec_outs],
        grid=(t // tr,),
        in_specs=[pl.BlockSpec((tr, a.shape[1]), lambda i: (i, 0)) for a in row_ins]
        + [pl.BlockSpec((1, a.shape[1]), lambda i: (0, 0)) for a in vec_ins] + [ANY_SPEC] * len(deps),
        out_specs=[pl.BlockSpec((tr, w), lambda i: (i, 0)) for w, _ in row_outs]
        + [pl.BlockSpec((1, w), lambda i: (0, 0)) for w in vec_outs],
        compiler_params=_params("arbitrary"),
    )(*row_ins, *vec_ins, *deps)
    return list(res)


def _norm_fwd(x, g, name, deps=()):
    d = x.shape[1]
    return _rowwise(name, lambda x, g: ((_rms_f(x, g)[0],), ()), [x], [g], [(d, BF16)], [], deps=deps)[0]


def _norm_bwd(x, dxn, dres, g, name):
    d = x.shape[1]

    def fn(x, dxn, dres, g):
        _, xh, r = _rms_f(x, g)
        dx, dg = _rms_b(dxn, g, xh, r)
        return (dres + dx,), (dg,)

    return _rowwise(name, fn, [x, dxn, dres], [g], [(d, F32)], [d])


def _block_fwd(h_prev, y, g_y, g_n, name):
    d = h_prev.shape[1]

    def fn(h_prev, y, g_y, g_n):
        h = h_prev + _rms_f(y, g_y)[0]
        return (h, _rms_f(h, g_n)[0]), ()

    return _rowwise(name, fn, [h_prev, y], [g_y, g_n], [(d, F32), (d, BF16)], [])


def _block_bwd(h, y, d_up, d_n, g_y, g_n, name, deps=()):
    d = h.shape[1]

    def fn(h, y, d_up, d_n, g_y, g_n):
        _, hh, hr = _rms_f(h, g_n)
        dh_n, dg_n = _rms_b(d_n, g_n, hh, hr)
        dh = d_up + dh_n
        _, yh, yr = _rms_f(y, g_y)
        dy, dg_y = _rms_b(dh, g_y, yh, yr)
        return (dh, dy), (dg_y, dg_n)

    return _rowwise(name, fn, [h, y, d_up, d_n], [g_y, g_n], [(d, F32), (d, BF16)], [d, d], deps=deps)


def _loss_head(h2, z, e, target, g, name):
    d = h2.shape[1]

    def fn(h2, z, e, target, g):
        gate = jax.nn.sigmoid(z)
        y, xh, r = _rms_f(e * gate, g)
        diff = h2 + y - target
        loss = 0.5 * jnp.sum(jnp.mean(diff * diff, axis=-1, keepdims=True), axis=0, keepdims=True)
        dh3 = diff / d
        deg, dg = _rms_b(dh3, g, xh, r)
        dz = deg * e * gate * (1.0 - gate)
        de = deg * gate
        return (dh3, dz, de), (dg, jnp.broadcast_to(loss, (1, LANES)))

    return _rowwise(name, fn, [h2, z, e, target], [g], [(d, F32), (d, BF16), (d, BF16)], [d, LANES])


def _shift_down(x, s, rows):
    return jnp.where(rows >= s, pltpu.roll(x, s, axis=0), 0.0)


def _shift_up(x, s, rows):
    t = x.shape[0]
    return jnp.where(rows < t - s, pltpu.roll(x, t - s, axis=0), 0.0)


def _conv_pre(x, w, rows):
    pre = x * w[QK_CONV - 1:QK_CONV, :]
    for s in range(1, QK_CONV):
        pre = pre + _shift_down(x, s, rows) * w[QK_CONV - 1 - s:QK_CONV - s, :]
    return pre


def _conv_fwd(proj_a, w, start, name):
    t = proj_a.shape[0]
    width = w.shape[1]
    tc = _tile(width, 256)
    off = start // tc

    def body(x_ref, w_ref, y_ref):
        x = x_ref[...]
        rows = lax.broadcasted_iota(jnp.int32, x.shape, 0)
        pre = _conv_pre(x, w_ref[...], rows)
        y_ref[...] = pre * jax.nn.sigmoid(pre)

    return pl.pallas_call(
        body, name=name,
        out_shape=jax.ShapeDtypeStruct((t, width), F32),
        grid=(width // tc,),
        in_specs=[pl.BlockSpec((t, tc), lambda j: (0, off + j)), pl.BlockSpec((QK_CONV, tc), lambda j: (0, j))],
        out_specs=pl.BlockSpec((t, tc), lambda j: (0, j)),
        compiler_params=_params("parallel"),
    )(proj_a, w)


def _conv_bwd(proj_a, w, dy, start, name):
    t = proj_a.shape[0]
    width = w.shape[1]
    tc = _tile(width, 256)
    off = start // tc

    def body(x_ref, w_ref, dy_ref, dx_ref, dw_ref):
        x, w = x_ref[...], w_ref[...]
        rows = lax.broadcasted_iota(jnp.int32, x.shape, 0)
        pre = _conv_pre(x, w, rows)
        sig = jax.nn.sigmoid(pre)
        dpre = dy_ref[...] * (sig * (1.0 + pre * (1.0 - sig)))
        dx = dpre * w[QK_CONV - 1:QK_CONV, :]
        dws = [jnp.sum(dpre * x, axis=0, keepdims=True)]
        for s in range(1, QK_CONV):
            dx = dx + _shift_up(dpre, s, rows) * w[QK_CONV - 1 - s:QK_CONV - s, :]
            dws.append(jnp.sum(dpre * _shift_down(x, s, rows), axis=0, keepdims=True))
        dx_ref[...] = dx.astype(dx_ref.dtype)
        for s in range(QK_CONV):
            dw_ref[QK_CONV - 1 - s:QK_CONV - s, :] = dws[s]

    return pl.pallas_call(
        body, name=name,
        out_shape=[jax.ShapeDtypeStruct((t, width), BF16), jax.ShapeDtypeStruct((QK_CONV, width), F32)],
        grid=(width // tc,),
        in_specs=[pl.BlockSpec((t, tc), lambda j: (0, off + j)), pl.BlockSpec((QK_CONV, tc), lambda j: (0, j)),
                  pl.BlockSpec((t, tc), lambda j: (0, j))],
        out_specs=[pl.BlockSpec((t, tc), lambda j: (0, j)), pl.BlockSpec((QK_CONV, tc), lambda j: (0, j))],
        compiler_params=_params("parallel"),
    )(proj_a, w, dy)


def _gates_fwd(proj_b, bias, col_block, heads, name):
    t = proj_b.shape[0]

    def body(x_ref, b_ref, o_ref):
        pre = x_ref[...] + b_ref[...]
        col = lax.broadcasted_iota(jnp.int32, pre.shape, 1)
        log_f = jnp.minimum(pre, 0.0) - jnp.log(1.0 + jnp.exp(-jnp.abs(pre)))
        o_ref[...] = jnp.where(col < heads, pre, jnp.where(col < 2 * heads, log_f, 0.0))

    return pl.pallas_call(
        body, name=name,
        out_shape=jax.ShapeDtypeStruct((t, LANES), F32),
        grid=(1,),
        in_specs=[pl.BlockSpec((t, LANES), lambda i: (0, col_block)), pl.BlockSpec((1, LANES), lambda i: (0, 0))],
        out_specs=pl.BlockSpec((t, LANES), lambda i: (0, 0)),
        compiler_params=_params("arbitrary"),
    )(proj_b, bias)


def _gates_bwd(proj_b, bias, dgl, col_block, heads, name):
    t = proj_b.shape[0]

    def body(x_ref, b_ref, d_ref, dx_ref, db_ref):
        pre = x_ref[...] + b_ref[...]
        col = lax.broadcasted_iota(jnp.int32, pre.shape, 1)
        d = d_ref[...]
        dx = jnp.where(col < heads, d, jnp.where(col < 2 * heads, d * jax.nn.sigmoid(-pre), 0.0))
        dx_ref[...] = dx.astype(dx_ref.dtype)
        db_ref[...] = jnp.sum(dx, axis=0, keepdims=True)

    return pl.pallas_call(
        body, name=name,
        out_shape=[jax.ShapeDtypeStruct((t, LANES), BF16), jax.ShapeDtypeStruct((1, LANES), F32)],
        grid=(1,),
        in_specs=[pl.BlockSpec((t, LANES), lambda i: (0, col_block)), pl.BlockSpec((1, LANES), lambda i: (0, 0)),
                  pl.BlockSpec((t, LANES), lambda i: (0, 0))],
        out_specs=[pl.BlockSpec((t, LANES), lambda i: (0, 0)), pl.BlockSpec((1, LANES), lambda i: (0, 0))],
        compiler_params=_params("arbitrary"),
    )(proj_b, bias, dgl)


def _dot(a, b, dims):
    return lax.dot_general(a.astype(BF16), b.astype(BF16), (dims, ((), ())), preferred_element_type=F32)


def _dot_exact(a, b):
    return lax.dot_general(a, b, (NN, ((), ())), preferred_element_type=F32, precision=lax.Precision.HIGHEST)


def _chunk_gates(li, lf):
    n = li.shape[0]
    r = lax.broadcasted_iota(jnp.int32, (n, n), 0)
    c = lax.broadcasted_iota(jnp.int32, (n, n), 1)
    tri = (c <= r).astype(F32)
    b_cols = _dot_exact(tri, jnp.broadcast_to(lf, (n, n)))
    b_rows = b_cols.T
    li_rows = jnp.broadcast_to(li, (n, n)).T
    b = b_cols[:, 0:1]
    b_tot = b_cols[n - 1:n, 0:1]
    return b_cols, b_rows, li_rows, b, b_tot, c <= r


def _chunk_state(k, v, li, b, b_tot, ct, nrow, m_prev):
    a = b_tot - b + li
    m_new = jnp.maximum(b_tot + m_prev, jnp.max(a, axis=0, keepdims=True))
    dec = jnp.exp(b_tot + m_prev - m_new)
    w = jnp.exp(a - m_new)
    wv = w * v
    ct_new = dec * ct + _dot(k, wv, TN)
    n_new = dec * nrow + jnp.sum(w * k, axis=0, keepdims=True)
    return ct_new, n_new, m_new, dec, w, wv


def _chunk_out(q, k, v, gates, ct, nrow, m_prev):
    b_cols, b_rows, li_rows, b, _, causal = gates
    dlog = jnp.where(causal, b_cols - b_rows + li_rows, -jnp.inf)
    inter = b + m_prev
    mt = jnp.maximum(inter, jnp.max(dlog, axis=-1, keepdims=True))
    dw = jnp.exp(dlog - mt)
    iw = jnp.exp(inter - mt)
    a = _dot(q, k, NT)
    sm = a * dw
    qc = _dot(q, ct, NN)
    qn = jnp.sum(q * nrow, axis=-1, keepdims=True)
    num = iw * qc + _dot(sm, v, NN)
    den = iw * qn + jnp.sum(sm, axis=-1, keepdims=True)
    floor = jnp.exp(-mt)
    dn = jnp.maximum(jnp.abs(den), floor)
    h = num / dn
    return h, (dw, iw, a, sm, qc, qn, den, floor, dn)


def _head_out(h, o, g):
    cell, hh, r = _rms_f(h, g)
    sig = jax.nn.sigmoid(o)
    return sig * cell, (cell, hh, r, sig)


def _head_gates(gl_tile, head, heads):
    lane = lax.broadcasted_iota(jnp.int32, gl_tile.shape, 1)
    li = jnp.sum(jnp.where(lane == head, gl_tile, 0.0), axis=-1, keepdims=True)
    lf = jnp.sum(jnp.where(lane == heads + head, gl_tile, 0.0), axis=-1, keepdims=True)
    return li, lf


def _mlstm_fwd(qc, kc, proj_a, gl, g_mlstm, heads, d_model, name):
    t = qc.shape[0]
    dh = g_mlstm.shape[1] // heads
    nc = t // CHUNK
    scale = dh ** -0.5

    def body(q_ref, k_ref, v_ref, o_ref, gl_ref, g_ref, out_ref, ct_ref, n_ref, m_ref):
        head = pl.program_id(0)
        ct_ref[...] = jnp.zeros_like(ct_ref)
        n_ref[...] = jnp.zeros_like(n_ref)
        m_ref[...] = jnp.zeros_like(m_ref)

        def chunk(ci, carry):
            sl = pl.ds(pl.multiple_of(ci * CHUNK, CHUNK), CHUNK)
            q, k, v = q_ref[sl, :], k_ref[sl, :] * scale, v_ref[sl, :]
            li, lf = _head_gates(gl_ref[sl, :], head, heads)
            gates = _chunk_gates(li, lf)
            ct, nrow, m_prev = ct_ref[...], n_ref[0:1, :], m_ref[0:1, 0:1]
            h, _ = _chunk_out(q, k, v, gates, ct, nrow, m_prev)
            y, _ = _head_out(h, o_ref[sl, :], g_ref[...])
            out_ref[sl, :] = y.astype(out_ref.dtype)
            ct_new, n_new, m_new, _, _, _ = _chunk_state(k, v, li, gates[3], gates[4], ct, nrow, m_prev)
            ct_ref[...] = ct_new
            n_ref[0:1, :] = n_new
            m_ref[0:1, 0:1] = m_new
            return carry

        lax.fori_loop(0, nc, chunk, 0)

    col = lambda off: pl.BlockSpec((t, dh), lambda h: (0, off + h))
    return pl.pallas_call(
        body, name=name,
        out_shape=jax.ShapeDtypeStruct((t, d_model), BF16),
        grid=(heads,),
        in_specs=[col(0), col(0), col(2 * heads), col(3 * heads), pl.BlockSpec((t, LANES), lambda h: (0, 0)),
                  pl.BlockSpec((1, dh), lambda h: (0, h))],
        out_specs=pl.BlockSpec((t, dh), lambda h: (0, h)),
        scratch_shapes=[pltpu.VMEM((dh, dh), F32), pltpu.VMEM((8, dh), F32), pltpu.VMEM((8, LANES), F32)],
        compiler_params=_params("arbitrary"),
    )(qc, kc, proj_a, proj_a, gl, g_mlstm)


def _mlstm_bwd(qc, kc, proj_a, gl, g_mlstm, dhcat, heads, name, dep):
    t = qc.shape[0]
    wm = g_mlstm.shape[1]
    dh = wm // heads
    nc = t // CHUNK
    scale = dh ** -0.5

    def body(q_ref, k_ref, v_ref, o_ref, gl_ref, g_ref, dy_ref, dep_ref,
             dq_ref, dk_ref, dv_ref, do_ref, dgl_ref, dg_ref,
             ct_all, n_all, m_all, dct_ref, dn_ref):
        del dep_ref
        g = g_ref[...]
        head = pl.program_id(0)

        @pl.when(head == 0)
        def _():
            dgl_ref[...] = jnp.zeros_like(dgl_ref)

        def load(ci):
            sl = pl.ds(pl.multiple_of(ci * CHUNK, CHUNK), CHUNK)
            li, lf = _head_gates(gl_ref[sl, :], head, heads)
            return sl, q_ref[sl, :], k_ref[sl, :] * scale, v_ref[sl, :], li, lf

        def fwd(ci, carry):
            ct, nrow, m_prev = carry
            ct_all[ci] = ct
            n_all[ci, 0:1, :] = nrow
            m_all[ci, 0:1, 0:1] = m_prev
            _, _, k, v, li, lf = load(ci)
            gates = _chunk_gates(li, lf)
            ct_new, n_new, m_new, _, _, _ = _chunk_state(k, v, li, gates[3], gates[4], ct, nrow, m_prev)
            return ct_new, n_new, m_new

        lax.fori_loop(0, nc, fwd, (jnp.zeros((dh, dh), F32), jnp.zeros((1, dh), F32), jnp.zeros((1, 1), F32)))

        dct_ref[...] = jnp.zeros_like(dct_ref)
        dn_ref[...] = jnp.zeros_like(dn_ref)
        dg_ref[...] = jnp.zeros_like(dg_ref)
        r_idx = lax.broadcasted_iota(jnp.int32, (CHUNK, 1), 0)
        rr = lax.broadcasted_iota(jnp.int32, (CHUNK, CHUNK), 0)
        cc = lax.broadcasted_iota(jnp.int32, (CHUNK, CHUNK), 1)
        tri_t = (cc >= rr).astype(F32)

        def bwd(step, carry):
            ci = nc - 1 - step
            sl, q, k, v, li, lf = load(ci)
            gates = _chunk_gates(li, lf)
            b, b_tot, causal = gates[3], gates[4], gates[5]
            ct, nrow, m_prev = ct_all[ci], n_all[ci, 0:1, :], m_all[ci, 0:1, 0:1]
            h, (dw, iw, a, sm, qc, qn, den, floor, dn) = _chunk_out(q, k, v, gates, ct, nrow, m_prev)
            o = o_ref[sl, :]
            _, (cell, hh, r, sig) = _head_out(h, o, g)
            dy = dy_ref[sl, :]
            dcell = dy * sig
            do_ref[sl, :] = (dy * cell * sig * (1.0 - sig)).astype(do_ref.dtype)
            dh_, dg = _rms_b(dcell, g, hh, r)
            dg_ref[...] += dg
            dnum = dh_ / dn
            ddn = -jnp.sum(dh_ * h, axis=-1, keepdims=True) / dn
            dden = jnp.where(jnp.abs(den) >= floor, ddn * jnp.sign(den), 0.0)
            idn = iw * dnum
            idd = iw * dden
            dq = _dot(idn, ct, NT) + idd * nrow
            dct = _dot(q, idn, TN)
            dnr = jnp.sum(q * idd, axis=0, keepdims=True)
            diw = jnp.sum(dnum * qc, axis=-1, keepdims=True) + dden * qn
            dsm = _dot(dnum, v, NT) + dden
            dv = _dot(sm, dnum, TN)
            da_ = dsm * dw
            e = jnp.where(causal, dsm * a * dw, 0.0)
            dq = dq + _dot(da_, k, NN)
            dk = _dot(da_, q, TN)
            e_cols = jnp.sum(e.T, axis=-1, keepdims=True)
            db = jnp.sum(e, axis=-1, keepdims=True) - e_cols + diw * iw
            dli = e_cols
            _, _, _, dec, w, wv = _chunk_state(k, v, li, b, b_tot, ct, nrow, m_prev)
            dct_new, dn_new = dct_ref[...], dn_ref[0:1, :]
            ddec = jnp.sum(jnp.sum(dct_new * ct, axis=-1, keepdims=True), axis=0, keepdims=True) \
                + jnp.sum(dn_new * nrow, axis=-1, keepdims=True)
            dk = dk + _dot(wv, dct_new, NT) + w * dn_new
            dwv = _dot(k, dct_new, NN)
            dv = dv + w * dwv
            dwt = jnp.sum(dwv * v, axis=-1, keepdims=True) + jnp.sum(k * dn_new, axis=-1, keepdims=True)
            da = dwt * w
            dbtot = jnp.sum(da, axis=0, keepdims=True) + ddec * dec
            db = db - da + jnp.where(r_idx == CHUNK - 1, dbtot, 0.0)
            dli = dli + da
            dlf = _dot_exact(tri_t, jnp.broadcast_to(db, (CHUNK, CHUNK)))[:, 0:1]
            dct_ref[...] = dec * dct_new + dct
            dn_ref[0:1, :] = dec * dn_new + dnr
            dq_ref[sl, :] = dq
            dk_ref[sl, :] = dk * scale
            dv_ref[sl, :] = dv.astype(dv_ref.dtype)
            lane = lax.broadcasted_iota(jnp.int32, (CHUNK, LANES), 1)
            dgl_ref[sl, :] += jnp.where(lane == head, dli, 0.0) + jnp.where(lane == heads + head, dlf, 0.0)
            return carry

        lax.fori_loop(0, nc, bwd, 0)

    col = lambda off: pl.BlockSpec((t, dh), lambda h: (0, off + h))
    whole = pl.BlockSpec((t, LANES), lambda h: (0, 0))
    return pl.pallas_call(
        body, name=name,
        out_shape=[jax.ShapeDtypeStruct((t, wm), F32)] * 2 + [jax.ShapeDtypeStruct((t, wm), BF16)] * 2
        + [jax.ShapeDtypeStruct((t, LANES), F32), jax.ShapeDtypeStruct((1, wm), F32)],
        grid=(heads,),
        in_specs=[col(0), col(0), col(2 * heads), col(3 * heads), whole, pl.BlockSpec((1, dh), lambda h: (0, h)), col(0),
                  ANY_SPEC],
        out_specs=[col(0), col(0), col(0), col(0), whole, pl.BlockSpec((1, dh), lambda h: (0, h))],
        scratch_shapes=[pltpu.VMEM((nc, dh, dh), F32), pltpu.VMEM((nc, 8, dh), F32), pltpu.VMEM((nc, 8, LANES), F32),
                        pltpu.VMEM((dh, dh), F32), pltpu.VMEM((8, dh), F32)],
        compiler_params=_params("arbitrary"),
    )(qc, kc, proj_a, proj_a, gl, g_mlstm, dhcat, dep)


def _pool_window(gi):
    return jnp.where(gi == 0, 2.0, jnp.where(gi == 1, 4.0, jnp.where(gi == 2, 8.0, 16.0))).astype(F32)


def _pool_select(gi, levels):
    return jnp.where(gi == 0, levels[0], jnp.where(gi == 1, levels[1], jnp.where(gi == 2, levels[2], levels[3])))


def _pooled(u, gi, rows):
    s, levels = u, []
    for lvl in range(POOL_GROUPS):
        s = s + _shift_down(s, 1 << lvl, rows)
        levels.append(s)
    cnt = jnp.minimum((rows + 1).astype(F32), _pool_window(gi))
    return _pool_select(gi, levels) / cnt - u, cnt


def _pool_fwd(proj_b, w_pool, scale, hcat, name):
    t, d_model = hcat.shape
    g, cg, _ = w_pool.shape
    off = (d_model - g * cg) // cg

    def body(u_ref, w_ref, s_ref, hcat_ref, out_ref):
        del hcat_ref
        gi = pl.program_id(0)
        u = u_ref[...]
        rows = lax.broadcasted_iota(jnp.int32, u.shape, 0)
        pooled, _ = _pooled(u, gi, rows)
        out_ref[...] = (_dot(pooled, w_ref[...], NN) * s_ref[...]).astype(out_ref.dtype)

    return pl.pallas_call(
        body, name=name,
        out_shape=jax.ShapeDtypeStruct((t, d_model), BF16),
        grid=(g,),
        in_specs=[pl.BlockSpec((t, cg), lambda gi: (0, gi)), pl.BlockSpec((None, cg, cg), lambda gi: (gi, 0, 0)),
                  pl.BlockSpec((1, cg), lambda gi: (0, gi)), HBM_SPEC],
        out_specs=pl.BlockSpec((t, cg), lambda gi: (0, off + gi)),
        input_output_aliases={3: 0},
        compiler_params=_params("arbitrary"),
    )(proj_b, w_pool, scale, hcat)


def _pool_bwd(proj_b, w_pool, scale, dhcat, name):
    t, d_model = dhcat.shape
    g, cg, _ = w_pool.shape
    off = (d_model - g * cg) // cg

    def body(u_ref, w_ref, s_ref, dy_ref, du_ref, dw_ref, ds_ref):
        gi = pl.program_id(0)
        u, w, dy = u_ref[...], w_ref[...], dy_ref[...]
        rows = lax.broadcasted_iota(jnp.int32, u.shape, 0)
        pooled, cnt = _pooled(u, gi, rows)
        mixed = _dot(pooled, w, NN)
        ds_ref[...] = jnp.sum(dy * mixed, axis=0, keepdims=True)
        dmixed = dy * s_ref[...]
        dw_ref[...] = _dot(pooled, dmixed, TN).astype(dw_ref.dtype)
        dpooled = _dot(dmixed, w, NT)
        s, levels = dpooled / cnt, []
        for lvl in range(POOL_GROUPS):
            s = s + _shift_up(s, 1 << lvl, rows)
            levels.append(s)
        du_ref[...] = (_pool_select(gi, levels) - dpooled).astype(du_ref.dtype)

    return pl.pallas_call(
        body, name=name,
        out_shape=[jax.ShapeDtypeStruct((t, g * cg), BF16), jax.ShapeDtypeStruct((g, cg, cg), BF16),
                   jax.ShapeDtypeStruct((1, g * cg), F32)],
        grid=(g,),
        in_specs=[pl.BlockSpec((t, cg), lambda gi: (0, gi)), pl.BlockSpec((None, cg, cg), lambda gi: (gi, 0, 0)),
                  pl.BlockSpec((1, cg), lambda gi: (0, gi)), pl.BlockSpec((t, cg), lambda gi: (0, off + gi))],
        out_specs=[pl.BlockSpec((t, cg), lambda gi: (0, gi)), pl.BlockSpec((None, cg, cg), lambda gi: (gi, 0, 0)),
                   pl.BlockSpec((1, cg), lambda gi: (0, gi))],
        compiler_params=_params("arbitrary"),
    )(proj_b, w_pool, scale, dhcat)


def _pad_cols(a, width):
    return jnp.pad(a, ((0, 0), (0, width - a.shape[1])))


def kernel(x, p, w_in, b_gates, w_qk_conv, g_mlstm, w_pool, pool_scale, w_out, g_mix_pre, g_mix_post, w_ff1, w_ff2, g_ff_pre, g_ff_post, w_ple_proj, w_ple_gate, g_ple_gate, g_ple_post, loss_target, m_w_in, m_b_gates, m_w_qk_conv, m_g_mlstm, m_w_pool, m_pool_scale, m_w_out, m_g_mix_pre, m_g_mix_post, m_w_ff1, m_w_ff2, m_g_ff_pre, m_g_ff_post, m_w_ple_proj, m_w_ple_gate, m_g_ple_gate, m_g_ple_post, v_w_in, v_b_gates, v_w_qk_conv, v_g_mlstm, v_w_pool, v_pool_scale, v_w_out, v_g_mix_pre, v_g_mix_post, v_w_ff1, v_w_ff2, v_g_ff_pre, v_g_ff_post, v_w_ple_proj, v_w_ple_gate, v_g_ple_gate, v_g_ple_post):
    weights = dict(w_in=w_in, b_gates=b_gates, w_qk_conv=w_qk_conv, g_mlstm=g_mlstm, w_pool=w_pool,
                   pool_scale=pool_scale, w_out=w_out, g_mix_pre=g_mix_pre, g_mix_post=g_mix_post, w_ff1=w_ff1,
                   w_ff2=w_ff2, g_ff_pre=g_ff_pre, g_ff_post=g_ff_post, w_ple_proj=w_ple_proj,
                   w_ple_gate=w_ple_gate, g_ple_gate=g_ple_gate, g_ple_post=g_ple_post)
    mom1 = dict(w_in=m_w_in, b_gates=m_b_gates, w_qk_conv=m_w_qk_conv, g_mlstm=m_g_mlstm, w_pool=m_w_pool,
                pool_scale=m_pool_scale, w_out=m_w_out, g_mix_pre=m_g_mix_pre, g_mix_post=m_g_mix_post,
                w_ff1=m_w_ff1, w_ff2=m_w_ff2, g_ff_pre=m_g_ff_pre, g_ff_post=m_g_ff_post,
                w_ple_proj=m_w_ple_proj, w_ple_gate=m_w_ple_gate, g_ple_gate=m_g_ple_gate, g_ple_post=m_g_ple_post)
    mom2 = dict(w_in=v_w_in, b_gates=v_b_gates, w_qk_conv=v_w_qk_conv, g_mlstm=v_g_mlstm, w_pool=v_w_pool,
                pool_scale=v_pool_scale, w_out=v_w_out, g_mix_pre=v_g_mix_pre, g_mix_post=v_g_mix_post,
                w_ff1=v_w_ff1, w_ff2=v_w_ff2, g_ff_pre=v_g_ff_pre, g_ff_post=v_g_ff_post,
                w_ple_proj=v_w_ple_proj, w_ple_gate=v_w_ple_gate, g_ple_gate=v_g_ple_gate, g_ple_post=v_g_ple_post)
    order = list(weights)

    t, d = x.shape[1], x.shape[2]
    heads = MLSTM_HEADS
    wm = g_mlstm.shape[1]
    pw = pool_scale.shape[1]
    cg = pw // POOL_GROUPS
    in_cols = 4 * wm + 2 * heads + pw
    in_shard = w_in.shape[2]
    xs, ps, tgt = x[0], p[0, 0], loss_target[0]

    mx, my, mc = lax.axis_index("x"), lax.axis_index("y"), lax.axis_index("c")
    core = jnp.reshape(mc, (1,)).astype(jnp.int32)
    chip = jnp.reshape(2 * mx + my, (1,)).astype(jnp.int32)

    groups = [[w_in[0].T.astype(BF16), w_qk_conv[0], w_pool[0].reshape(POOL_GROUPS * w_pool.shape[2], cg).astype(BF16)],
              [w_out[0].astype(BF16), w_ff1[0].astype(BF16)],
              [w_ff2[0].astype(BF16), w_ple_gate[0].astype(BF16), w_ple_proj[0].astype(BF16)]]
    started, token = _gather_start(groups, "gather_start")
    bias = _pad_cols(b_gates, LANES)

    xn = _norm_fwd(xs, g_mix_pre, "norm_mix_pre", deps=[token])
    g_in, g_conv, g_pool = _forward_pair(_gather_wait(started[0], xn, "gather_wait_in"), "gather_pair_in")
    in_t = g_in.reshape(in_cols, d)
    in_t_b = jnp.concatenate([in_t[4 * wm + 2 * heads:], in_t[4 * wm:4 * wm + 2 * heads],
                              jnp.zeros((LANES - 2 * heads, d), BF16)], axis=0)
    conv_w = g_conv.transpose(1, 0, 2).reshape(QK_CONV, 2 * wm)
    pool_w = g_pool.reshape(N_DEV, POOL_GROUPS, cg // N_DEV, cg).transpose(1, 0, 2, 3).reshape(POOL_GROUPS, cg, cg)
    proj_a = _mm_nt("proj_qkvo", xn, in_t, n_rows=4 * wm)
    proj_b = _mm_nt("proj_pool_gates", xn, in_t_b)
    qc = _conv_fwd(proj_a, conv_w[:, :wm], 0, "q_conv")
    kc = _conv_fwd(proj_a, conv_w[:, wm:], wm, "k_conv")
    gl = _gates_fwd(proj_b, bias, pw // LANES, heads, "gates")
    hcat = _mlstm_fwd(qc, kc, proj_a, gl, g_mlstm, heads, d, "mlstm")
    hcat = _pool_fwd(proj_b, pool_w, pool_scale, hcat, "pool")
    g_out, g_ff1 = _forward_pair(_gather_wait(started[1], hcat, "gather_wait_ff1"), "gather_pair_ff1")
    out_w = g_out.reshape(d, d)
    mix = _mm_nn("mix_out", hcat, out_w)[0]
    h1, hn = _block_fwd(xs, mix, g_mix_post, g_ff_pre, "residual_mix")
    relu_a, act = _mm_nn_pieces("ff1", hn, g_ff1, out_dtype=[BF16, BF16],
                                epilogue=lambda acc: (jnp.maximum(acc, 0.0), jnp.square(jnp.maximum(acc, 0.0))))
    g_ff2, g_pgate, g_pproj = _forward_pair(_gather_wait(started[2], act, "gather_wait_ff2"), "gather_pair_ff2")
    ff2_w = g_ff2.reshape(-1, d)
    pgate_w = g_pgate.reshape(d, d)
    ff = _mm_nn("ff2", act, ff2_w)[0]
    h2, hg = _block_fwd(h1, ff, g_ff_post, g_ple_gate, "residual_ff")
    z = _mm_nn("ple_gate", hg, pgate_w)[0]
    e = _mm_nn_pieces("ple_proj", ps, g_pproj)[0]
    dh3, dz, de, dg_ple_post, loss_part = _loss_head(h2, z, e, tgt, g_ple_post, "loss_head")
    loss = lax.psum(loss_part[0, 0], MESH_AXES)

    out = {}

    def view(k, a):
        return a[0].T if k == "w_in" else a.reshape(-1, a.shape[-1])

    def reduce_start(pieces, tag):
        keys = list(pieces)
        got = _pair_exchange([pieces[k] for k in keys], "reduce_pair_" + tag)
        sums = [_pair_sum(pieces[k], g, core, "pair_sum_" + k) for k, g in zip(keys, got)]
        return keys, _chip_start(sums, "reduce_start_" + tag)

    def reduce_finish(group, after, tag):
        keys, begun = group
        sums, got = _chip_wait(begun, after, "reduce_wait_" + tag)
        for k, s, g in zip(keys, sums, got):
            res = _adamw_shard(view(k, weights[k]), view(k, mom1[k]), view(k, mom2[k]), s, g, chip, "adamw_" + k)
            out[k] = [r.T[None] if k == "w_in" else r.reshape(weights[k].shape) for r in res]

    dhg = _mm_nt("d_ple_gate_in", dz, pgate_w)
    gw_pgate = _mm_tn("gw_ple_gate", hg, dz)
    gw_pproj = _mm_tn("gw_ple_proj", ps, de, pieces=N_DEV)
    red_ple = reduce_start({"w_ple_gate": gw_pgate.reshape(N_DEV, d // N_DEV, d), "w_ple_proj": gw_pproj}, "ple")
    dh2, dff, dg_ff_post, dg_ple_gate = _block_bwd(h2, ff, dh3, dhg, g_ff_post, g_ple_gate, "d_residual_ff",
                                                   deps=[red_ple[1]["token"]])
    da = _mm_nt("d_ff2_in", dff, ff2_w, out_dtype=BF16, extra=[relu_a],
                epilogue=lambda acc, r: (acc * (2.0 * r.astype(F32)),))
    gw_ff2 = _mm_tn("gw_ff2", act, dff)
    red_ff2 = reduce_start({"w_ff2": gw_ff2.reshape(N_DEV, -1, d)}, "ff2")
    reduce_finish(red_ple, red_ff2[1]["token"], "ple")
    dhn = _mm_nt_pieces("d_ff1_in", da, g_ff1, deps=[red_ff2[1]["token"]])
    gw_ff1 = _mm_tn("gw_ff1", hn, da, pieces=N_DEV)
    red_ff1 = reduce_start({"w_ff1": gw_ff1}, "ff1")
    reduce_finish(red_ff2, red_ff1[1]["token"], "ff2")
    dh1, dmix, dg_mix_post, dg_ff_pre = _block_bwd(h1, mix, dh2, dhn, g_mix_post, g_ff_pre, "d_residual_mix",
                                                   deps=[red_ff1[1]["token"]])
    dhcat = _mm_nt("d_mix_in", dmix, out_w)
    gw_out = _mm_tn("gw_out", hcat, dmix)
    du, gw_pool, dg_pool_scale = _pool_bwd(proj_b, pool_w, pool_scale, dhcat, "d_pool")
    red_out = reduce_start({
        "w_out": gw_out.reshape(N_DEV, d // N_DEV, d),
        "w_pool": gw_pool.reshape(POOL_GROUPS, N_DEV, cg // N_DEV, cg).transpose(1, 0, 2, 3).reshape(N_DEV, -1, cg)},
        "out")
    reduce_finish(red_ff1, red_out[1]["token"], "ff1")
    dq, dk, dv, do, dgl, dg_mlstm = _mlstm_bwd(qc, kc, proj_a, gl, g_mlstm, dhcat, heads, "d_mlstm",
                                               red_out[1]["token"])
    dgates, dbias = _gates_bwd(proj_b, bias, dgl, pw // LANES, heads, "d_gates")
    dq_pre, gw_conv_q = _conv_bwd(proj_a, conv_w[:, :wm], dq, 0, "d_q_conv")
    dk_pre, gw_conv_k = _conv_bwd(proj_a, conv_w[:, wm:], dk, wm, "d_k_conv")
    gw_conv = jnp.concatenate([gw_conv_q, gw_conv_k], axis=1)
    dproj_a = jnp.concatenate([dq_pre, dk_pre, dv, do], axis=1)
    dproj_b = jnp.concatenate([du, dgates], axis=1)
    gw_a = _mm_tn("gw_in_qkvo", dproj_a, xn)
    gw_b = _mm_tn("gw_in_pool_gates", dproj_b, xn)
    gw_in_t = jnp.concatenate([gw_a, gw_b[pw:pw + 2 * heads], gw_b[:pw]], axis=0)
    red_in = reduce_start({
        "w_in": gw_in_t.reshape(N_DEV, in_shard, d),
        "w_qk_conv": gw_conv.reshape(QK_CONV, N_DEV, -1).transpose(1, 0, 2).astype(BF16)}, "in")
    reduce_finish(red_out, red_in[1]["token"], "out")
    dxn_a = _mm_nn("d_proj_qkvo_in", dproj_a, in_t, deps=[red_in[1]["token"]])[0]
    dxn = _mm_nn("d_proj_in", dproj_b, in_t_b, extra=[dxn_a], epilogue=lambda acc, prev: (acc + prev,))[0]
    grad_x, dg_mix_pre = _norm_bwd(xs, dxn, dh1, g_mix_pre, "d_norm_mix_pre")

    small = ["b_gates", "g_mlstm", "pool_scale", "g_mix_pre", "g_mix_post", "g_ff_pre", "g_ff_post", "g_ple_gate",
             "g_ple_post"]
    small_grads = dict(b_gates=dbias[:, :2 * heads], g_mlstm=dg_mlstm, pool_scale=dg_pool_scale, g_mix_pre=dg_mix_pre,
                       g_mix_post=dg_mix_post, g_ff_pre=dg_ff_pre, g_ff_post=dg_ff_post, g_ple_gate=dg_ple_gate,
                       g_ple_post=dg_ple_post)
    pack = lambda src: jnp.concatenate([_pad_cols(src[k], d) for k in small], axis=0)
    small_parts = _all_gather([_pad_rows16(pack(small_grads))], "gather_small_grads")[0]
    res = _adamw_replicated(_pad_rows16(pack(weights)), _pad_rows16(pack(mom1)), _pad_rows16(pack(mom2)), small_parts,
                            "adamw_replicated")
    for i, k in enumerate(small):
        width = weights[k].shape[1]
        out[k] = [r[i:i + 1, :width] for r in res]
    reduce_finish(red_in, res[0], "in")

    return (loss, grad_x[None], *[out[k][0] for k in order], *[out[k][1] for k in order],
            *[out[k][2] for k in order], *[out[k][3] for k in order])


def _pad_rows16(a):
    return jnp.pad(a, ((0, 16 - a.shape[0]), (0, 0)))
```

```python
import functools

import jax
import jax.numpy as jnp
from jax import lax
from jax.experimental import pallas as pl
from jax.experimental.pallas import tpu as pltpu

F32 = jnp.float32
BF16 = jnp.bfloat16
EPS = 1e-6
N_DEV = 8
N_CHIP = 4
LANES = 128
VMEM_LIMIT = 56 * 1024 * 1024
MLSTM_HEADS = 8
POOL_GROUPS = 4
QK_CONV = 4
CHUNK = 128
ROW_TILE = 256
ADAM_LR, ADAM_B1, ADAM_B2, ADAM_EPS, ADAM_WD, ADAM_STEP = 0.001, 0.9, 0.999, 1e-08, 0.01, 10
MESH_AXES = ("x", "y", "c")
MESH = pl.DeviceIdType.MESH
HBM_SPEC = pl.BlockSpec(memory_space=pltpu.HBM)
SEM_SPEC = pl.BlockSpec(memory_space=pltpu.SEMAPHORE)
DATAFLOW = pltpu.SideEffectType.DATAFLOW_SIDE_EFFECTING


def _params(*sem):
    if sem:
        return pltpu.CompilerParams(dimension_semantics=sem, vmem_limit_bytes=VMEM_LIMIT)
    return pltpu.CompilerParams(vmem_limit_bytes=VMEM_LIMIT)


def _tile(n, pref):
    if n <= pref:
        return n
    t = (pref // LANES) * LANES
    while t >= LANES:
        if n % t == 0:
            return t
        t -= LANES
    return n


def _all_gather(arrs, name):
    n = len(arrs)

    def body(*refs):
        ins, outs = refs[:n], refs[n:2 * n]
        send, recv, loc = refs[2 * n:]
        x, y, c = lax.axis_index("x"), lax.axis_index("y"), lax.axis_index("c")
        me = (x, y, c)
        sibling = (x, y, 1 - c)
        chips = [(1 - x, y), (x, 1 - y), (1 - x, 1 - y)]

        def row(dev):
            return 4 * dev[0] + 2 * dev[1] + dev[2]

        def copy(w, k, block_of, to, src=None):
            dst = outs[w].at[row(block_of)]
            return pltpu.make_async_remote_copy(
                src_ref=dst if src is None else src, dst_ref=dst,
                send_sem=send.at[7 * w + k], recv_sem=recv.at[7 * w + k],
                device_id=to, device_id_type=MESH)

        started = []
        local = []
        for w in range(n):
            cp = pltpu.make_async_copy(ins[w], outs[w].at[row(me)], loc.at[w])
            cp.start()
            local.append(cp)
            first = [copy(w, 0, me, sibling, src=ins[w])]
            first += [copy(w, 1 + j, me, (*chip, c), src=ins[w]) for j, chip in enumerate(chips)]
            for cp in first:
                cp.start()
            started += first
        for j, chip in enumerate(chips):
            for w in range(n):
                copy(w, 1 + j, (*chip, c), me).wait_recv()
                fwd = copy(w, 4 + j, (*chip, c), sibling)
                fwd.start()
                started.append(fwd)
        for w in range(n):
            copy(w, 0, sibling, me).wait_recv()
            for j, chip in enumerate(chips):
                copy(w, 4 + j, (*chip, 1 - c), me).wait_recv()
        for cp in started:
            cp.wait_send()
        for cp in local:
            cp.wait()

    outs = pl.pallas_call(
        body, name=name,
        out_shape=[jax.ShapeDtypeStruct((N_DEV,) + a.shape, a.dtype) for a in arrs],
        in_specs=[HBM_SPEC] * n, out_specs=[HBM_SPEC] * n,
        scratch_shapes=[pltpu.SemaphoreType.DMA((7 * n,)), pltpu.SemaphoreType.DMA((7 * n,)),
                        pltpu.SemaphoreType.DMA((n,))],
    )(*arrs)
    return list(outs)


def _pair_exchange(pieces, name):
    n = len(pieces)

    def body(*refs):
        ins, outs = refs[:n], refs[n:2 * n]
        send, recv = refs[2 * n:]
        x, y, c = lax.axis_index("x"), lax.axis_index("y"), lax.axis_index("c")
        sibling = (x, y, 1 - c)

        def copy(w, chip, src_core):
            return pltpu.make_async_remote_copy(
                src_ref=ins[w].at[2 * chip + src_core], dst_ref=outs[w].at[chip],
                send_sem=send.at[N_CHIP * w + chip], recv_sem=recv.at[N_CHIP * w + chip],
                device_id=sibling, device_id_type=MESH)

        cps = [copy(w, chip, 1 - c) for w in range(n) for chip in range(N_CHIP)]
        for cp in cps:
            cp.start()
        for cp in cps:
            cp.wait_recv()
        for cp in cps:
            cp.wait_send()

    outs = pl.pallas_call(
        body, name=name,
        out_shape=[jax.ShapeDtypeStruct((N_CHIP,) + a.shape[1:], a.dtype) for a in pieces],
        in_specs=[HBM_SPEC] * n, out_specs=[HBM_SPEC] * n,
        scratch_shapes=[pltpu.SemaphoreType.DMA((N_CHIP * n,)), pltpu.SemaphoreType.DMA((N_CHIP * n,))],
    )(*pieces)
    return list(outs)


def _hbm(a):
    return pltpu.with_memory_space_constraint(a, pltpu.HBM)


def _mesh_place():
    x, y, c = lax.axis_index("x"), lax.axis_index("y"), lax.axis_index("c")
    return x, y, c, [(1 - x, y), (x, 1 - y), (1 - x, 1 - y)]


def _chip_copies(srcs, lands, send, recv):
    x, y, c, chips = _mesh_place()
    return [pltpu.make_async_remote_copy(
        src_ref=srcs[w].at[2 * chip[0] + chip[1]], dst_ref=lands[w].at[k],
        send_sem=send.at[3 * w + k], recv_sem=recv.at[3 * w + k],
        device_id=(*chip, c), device_id_type=MESH)
        for k, chip in enumerate(chips) for w in range(len(srcs))]


def _chip_start(sums, name):
    n = len(sums)
    lands = [lax.empty((3,) + a.shape[1:], a.dtype) for a in sums]

    def body(*refs):
        srcs, zones = refs[:n], refs[n:2 * n]
        send, recv = refs[2 * n], refs[2 * n + 1]
        token = refs[-1]
        for cp in _chip_copies(srcs, zones, send, recv):
            cp.start()
        token[...] = jnp.zeros_like(token)

    res = pl.pallas_call(
        body, name=name,
        out_shape=[pltpu.SemaphoreType.DMA((3 * n,)), pltpu.SemaphoreType.DMA((3 * n,))]
        + [pltpu.HBM(a.shape, a.dtype) for a in sums] + [pltpu.HBM(a.shape, a.dtype) for a in lands]
        + [jax.ShapeDtypeStruct((8, LANES), F32)],
        in_specs=[HBM_SPEC] * (2 * n),
        out_specs=[SEM_SPEC, SEM_SPEC] + [HBM_SPEC] * (2 * n) + [pl.BlockSpec(memory_space=pltpu.VMEM)],
        input_output_aliases={i: 2 + i for i in range(2 * n)},
        compiler_params=pltpu.CompilerParams(has_side_effects=DATAFLOW),
    )(*[_hbm(a) for a in sums], *[_hbm(a) for a in lands])
    return dict(send=res[0], recv=res[1], srcs=list(res[2:2 + n]), lands=list(res[2 + n:2 + 2 * n]), token=res[-1])


def _chip_wait(started, after, name):
    n = len(started["srcs"])

    def body(*refs):
        srcs, zones = refs[:n], refs[n:2 * n]
        send, recv = refs[2 * n], refs[2 * n + 1]
        for cp in _chip_copies(srcs, zones, send, recv):
            cp.wait_send()
            cp.wait_recv()

    res = pl.pallas_call(
        body, name=name,
        out_shape=[pltpu.HBM(a.shape, a.dtype) for a in started["srcs"] + started["lands"]],
        in_specs=[HBM_SPEC] * (2 * n) + [SEM_SPEC, SEM_SPEC, pl.BlockSpec(memory_space=pl.ANY)],
        out_specs=[HBM_SPEC] * (2 * n),
        input_output_aliases={i: i for i in range(2 * n)},
        compiler_params=pltpu.CompilerParams(has_side_effects=DATAFLOW),
    )(*started["srcs"], *started["lands"], started["send"], started["recv"], after)
    return list(res[:n]), list(res[n:])


def _gather_copies(srcs, lands, send, recv):
    x, y, c, chips = _mesh_place()
    me = 4 * x + 2 * y + c
    targets = [(x, y, 1 - c)] + [(*chip, c) for chip in chips]
    return [pltpu.make_async_remote_copy(
        src_ref=srcs[w], dst_ref=lands[w].at[me], send_sem=send.at[4 * w + k], recv_sem=recv.at[4 * w + k],
        device_id=to, device_id_type=MESH)
        for w in range(len(srcs)) for k, to in enumerate(targets)]


def _gather_start(groups, name):
    sizes = [len(g) for g in groups]
    flat = [a for g in groups for a in g]
    n, ng = len(flat), len(groups)
    me = 4 * lax.axis_index("x") + 2 * lax.axis_index("y") + lax.axis_index("c")
    lands = [lax.dynamic_update_slice(lax.empty((N_DEV,) + a.shape, a.dtype), a[None], (me,) + (0,) * a.ndim)
             for a in flat]

    def body(*refs):
        srcs, zones = refs[:n], refs[n:2 * n]
        sems = refs[2 * n:2 * n + 2 * ng]
        token = refs[-1]
        lo = 0
        for gi, size in enumerate(sizes):
            for cp in _gather_copies(srcs[lo:lo + size], zones[lo:lo + size], sems[2 * gi], sems[2 * gi + 1]):
                cp.start()
            lo += size
        token[...] = jnp.zeros_like(token)

    sem_shapes = []
    for size in sizes:
        sem_shapes += [pltpu.SemaphoreType.DMA((4 * size,))] * 2
    res = pl.pallas_call(
        body, name=name,
        out_shape=sem_shapes + [pltpu.HBM(a.shape, a.dtype) for a in flat + lands] + [jax.ShapeDtypeStruct((8, LANES), F32)],
        in_specs=[HBM_SPEC] * (2 * n),
        out_specs=[SEM_SPEC] * (2 * ng) + [HBM_SPEC] * (2 * n) + [pl.BlockSpec(memory_space=pltpu.VMEM)],
        input_output_aliases={i: 2 * ng + i for i in range(2 * n)},
        compiler_params=pltpu.CompilerParams(has_side_effects=DATAFLOW),
    )(*[_hbm(a) for a in flat], *[_hbm(a) for a in lands])
    out, lo = [], 0
    for gi, size in enumerate(sizes):
        out.append(dict(send=res[2 * gi], recv=res[2 * gi + 1],
                        srcs=list(res[2 * ng + lo:2 * ng + lo + size]),
                        lands=list(res[2 * ng + n + lo:2 * ng + n + lo + size])))
        lo += size
    return out, res[-1]


def _gather_wait(started, after, name):
    n = len(started["srcs"])

    def body(*refs):
        srcs, zones = refs[:n], refs[n:2 * n]
        send, recv = refs[2 * n], refs[2 * n + 1]
        for cp in _gather_copies(srcs, zones, send, recv):
            cp.wait_send()
            cp.wait_recv()

    res = pl.pallas_call(
        body, name=name,
        out_shape=[pltpu.HBM(a.shape, a.dtype) for a in started["srcs"] + started["lands"]],
        in_specs=[HBM_SPEC] * (2 * n) + [SEM_SPEC, SEM_SPEC, pl.BlockSpec(memory_space=pl.ANY)],
        out_specs=[HBM_SPEC] * (2 * n),
        input_output_aliases={i: i for i in range(2 * n)},
        compiler_params=pltpu.CompilerParams(has_side_effects=DATAFLOW),
    )(*started["srcs"], *started["lands"], started["send"], started["recv"], after)
    return list(res[n:])


def _forward_pair(lands, name):
    n = len(lands)

    def body(*refs):
        zones = refs[:n]
        send, recv = refs[2 * n:]
        x, y, c, chips = _mesh_place()

        def copy(w, j, core):
            rows = zones[w].at[4 * chips[j][0] + 2 * chips[j][1] + core]
            return pltpu.make_async_remote_copy(
                src_ref=rows, dst_ref=rows, send_sem=send.at[3 * w + j], recv_sem=recv.at[3 * w + j],
                device_id=(x, y, 1 - c), device_id_type=MESH)

        sends = [copy(w, j, c) for w in range(n) for j in range(3)]
        for cp in sends:
            cp.start()
        for w in range(n):
            for j in range(3):
                copy(w, j, 1 - c).wait_recv()
        for cp in sends:
            cp.wait_send()

    return list(pl.pallas_call(
        body, name=name,
        out_shape=[jax.ShapeDtypeStruct(a.shape, a.dtype) for a in lands],
        in_specs=[HBM_SPEC] * n, out_specs=[HBM_SPEC] * n,
        input_output_aliases={i: i for i in range(n)},
        scratch_shapes=[pltpu.SemaphoreType.DMA((3 * n,)), pltpu.SemaphoreType.DMA((3 * n,))],
    )(*lands))


def _pair_sum(pieces, got, core, name):
    _, rows, cols = pieces.shape
    tr, tc = _tile_2d(rows, cols)

    def body(core_ref, mine_ref, got_ref, out_ref):
        del core_ref
        out_ref[...] = (mine_ref[...].astype(F32) + got_ref[...].astype(F32)).astype(out_ref.dtype)

    return pl.pallas_call(
        body, name=name,
        out_shape=jax.ShapeDtypeStruct((N_CHIP, rows, cols), pieces.dtype),
        grid_spec=pltpu.PrefetchScalarGridSpec(
            num_scalar_prefetch=1, grid=(N_CHIP, rows // tr, cols // tc),
            in_specs=[pl.BlockSpec((None, tr, tc), lambda ch, i, j, core: (2 * ch + core[0], i, j)),
                      pl.BlockSpec((None, tr, tc), lambda ch, i, j, core: (ch, i, j))],
            out_specs=pl.BlockSpec((None, tr, tc), lambda ch, i, j, core: (ch, i, j))),
        compiler_params=_params("parallel", "parallel", "parallel"),
    )(core, pieces, got)


def _tile_2d(rows, cols, budget=2 * 1024 * 1024):
    want = max(16, budget // (4 * cols))
    if rows <= want:
        return rows, cols
    t = (want // 16) * 16
    while t >= 16:
        if rows % t == 0:
            return t, cols
        t -= 16
    return rows, _tile(cols, max(LANES, budget // (4 * rows)))


def _adamw_math(w, g, m, v):
    m = ADAM_B1 * m + (1.0 - ADAM_B1) * g
    v = ADAM_B2 * v + (1.0 - ADAM_B2) * (g * g)
    m_hat = m / (1.0 - ADAM_B1 ** ADAM_STEP)
    v_hat = v / (1.0 - ADAM_B2 ** ADAM_STEP)
    delta = -ADAM_LR * (m_hat / (jnp.sqrt(v_hat) + ADAM_EPS) + ADAM_WD * w)
    return delta, m, v


def _adamw_shard(w, m, v, sums, got, chip, name):
    rows, cols = w.shape
    tr, tc = _tile_2d(rows, cols, budget=1024 * 1024)

    def body(chip_ref, w_ref, m_ref, v_ref, own_ref, got_ref, g_out, d_out, m_out, v_out):
        del chip_ref
        g = own_ref[...].astype(F32)
        for k in range(3):
            g = g + got_ref[k].astype(F32)
        delta, m_new, v_new = _adamw_math(w_ref[...], g, m_ref[...], v_ref[...])
        g_out[...] = g
        d_out[...] = delta
        m_out[...] = m_new
        v_out[...] = v_new

    blk = pl.BlockSpec((tr, tc), lambda i, j, chip: (i, j))
    return pl.pallas_call(
        body, name=name,
        out_shape=[jax.ShapeDtypeStruct((rows, cols), F32)] * 4,
        grid_spec=pltpu.PrefetchScalarGridSpec(
            num_scalar_prefetch=1, grid=(rows // tr, cols // tc),
            in_specs=[blk, blk, blk,
                      pl.BlockSpec((None, tr, tc), lambda i, j, chip: (chip[0], i, j)),
                      pl.BlockSpec((3, tr, tc), lambda i, j, chip: (0, i, j))],
            out_specs=[blk] * 4),
        compiler_params=_params("parallel", "parallel"),
    )(chip, w, m, v, sums, got)


def _adamw_replicated(w, m, v, parts, name):
    rows, cols = w.shape

    def body(w_ref, m_ref, v_ref, parts_ref, g_out, d_out, m_out, v_out):
        g = parts_ref[0]
        for d in range(1, N_DEV):
            g = g + parts_ref[d]
        delta, m_new, v_new = _adamw_math(w_ref[...], g, m_ref[...], v_ref[...])
        g_out[...] = g
        d_out[...] = delta
        m_out[...] = m_new
        v_out[...] = v_new

    return pl.pallas_call(
        body, name=name,
        out_shape=[jax.ShapeDtypeStruct((rows, cols), F32)] * 4,
        compiler_params=_params(),
    )(w, m, v, parts)


NN = ((1,), (0,))
NT = ((1,), (1,))
TN = ((0,), (0,))


ANY_SPEC = pl.BlockSpec(memory_space=pl.ANY)


def _mm(name, a, b, *, dims, grid, a_spec, b_spec, outs, extra=(), extra_specs=(), epilogue=None, acc_shape=None,
        deps=()):
    nk = grid[2]
    n_extra, n_out = len(extra), len(outs)
    first_out = 2 + n_extra + len(deps)

    def body(*refs):
        a_ref, b_ref = refs[0], refs[1]
        ex = refs[2:2 + n_extra]
        out_refs = refs[first_out:first_out + n_out]
        part = lax.dot_general(a_ref[...].astype(BF16), b_ref[...].astype(BF16), (dims, ((), ())),
                               preferred_element_type=F32)

        def finish(acc):
            res = epilogue(acc, *[e[...] for e in ex]) if epilogue else (acc,)
            for val, o in zip(res, out_refs):
                o[...] = val.astype(o.dtype)

        if nk == 1:
            finish(part)
        else:
            acc_ref = refs[-1]
            k = pl.program_id(2)

            @pl.when(k == 0)
            def _():
                acc_ref[...] = part

            @pl.when(k > 0)
            def _():
                acc_ref[...] += part

            @pl.when(k == nk - 1)
            def _():
                finish(acc_ref[...])

    res = pl.pallas_call(
        body, name=name,
        out_shape=[jax.ShapeDtypeStruct(s, d) for s, d, _ in outs],
        grid=grid,
        in_specs=[a_spec, b_spec, *extra_specs] + [ANY_SPEC] * len(deps),
        out_specs=[sp for _, _, sp in outs],
        scratch_shapes=[pltpu.VMEM(acc_shape, F32)] if nk > 1 else [],
        compiler_params=_params("parallel", "parallel", "arbitrary"),
    )(a, b, *extra, *deps)
    return list(res)


def _mm_nn(name, a, b, out_dtype=F32, tm=1024, tn=1024, tk=2048, epilogue=None, n_out=1, extra=(), deps=()):
    m, kd = a.shape
    n = b.shape[1]
    tm, tn, tk = _tile(m, tm), _tile(n, tn), _tile(kd, tk)
    dts = out_dtype if isinstance(out_dtype, (list, tuple)) else [out_dtype] * n_out
    o_spec = pl.BlockSpec((tm, tn), lambda i, j, k: (i, j))
    return _mm(name, a, b, dims=NN, grid=(m // tm, n // tn, kd // tk),
               a_spec=pl.BlockSpec((tm, tk), lambda i, j, k: (i, k)),
               b_spec=pl.BlockSpec((tk, tn), lambda i, j, k: (k, j)),
               outs=[((m, n), dt, o_spec) for dt in dts], extra=extra, extra_specs=[o_spec] * len(extra),
               epilogue=epilogue, acc_shape=(tm, tn), deps=deps)


def _mm_nn_pieces(name, a, b, out_dtype=F32, tm=1024, tk=2048, epilogue=None, n_out=1):
    m, kd = a.shape
    npc, _, ns = b.shape
    tm, tk = _tile(m, tm), _tile(kd, tk)
    tn = _tile(ns, 1024)
    r = ns // tn
    dts = out_dtype if isinstance(out_dtype, (list, tuple)) else [out_dtype] * n_out
    o_spec = pl.BlockSpec((tm, tn), lambda i, j, k: (i, j))
    return _mm(name, a, b, dims=NN, grid=(m // tm, npc * r, kd // tk),
               a_spec=pl.BlockSpec((tm, tk), lambda i, j, k: (i, k)),
               b_spec=pl.BlockSpec((None, tk, tn), lambda i, j, k: (j // r, k, j % r)),
               outs=[((m, npc * ns), dt, o_spec) for dt in dts], epilogue=epilogue, acc_shape=(tm, tn))


def _mm_nt(name, a, b, out_dtype=F32, tm=1024, tn=1024, tk=2048, epilogue=None, extra=(), n_rows=None, deps=()):
    m, kd = a.shape
    n = n_rows or b.shape[0]
    tm, tn, tk = _tile(m, tm), _tile(n, tn), _tile(kd, tk)
    o_spec = pl.BlockSpec((tm, tn), lambda i, j, k: (i, j))
    return _mm(name, a, b, dims=NT, grid=(m // tm, n // tn, kd // tk),
               a_spec=pl.BlockSpec((tm, tk), lambda i, j, k: (i, k)),
               b_spec=pl.BlockSpec((tn, tk), lambda i, j, k: (j, k)),
               outs=[((m, n), out_dtype, o_spec)], extra=extra, extra_specs=[o_spec] * len(extra),
               epilogue=epilogue, acc_shape=(tm, tn), deps=deps)[0]


def _mm_nt_pieces(name, a, b, out_dtype=F32, tm=1024, tn=1024, deps=()):
    m = a.shape[0]
    npc, n, ks = b.shape
    tm, tn = _tile(m, tm), _tile(n, tn)
    tk = _tile(ks, 1024)
    r = ks // tk
    return _mm(name, a, b, dims=NT, grid=(m // tm, n // tn, npc * r),
               a_spec=pl.BlockSpec((tm, tk), lambda i, j, k: (i, k)),
               b_spec=pl.BlockSpec((None, tn, tk), lambda i, j, k: (k // r, j, k % r)),
               outs=[((m, n), out_dtype, pl.BlockSpec((tm, tn), lambda i, j, k: (i, j)))],
               acc_shape=(tm, tn), deps=deps)[0]


def _mm_tn(name, a, b, out_dtype=BF16, tm=1024, tn=1024, pieces=0):
    t, m = a.shape
    n = b.shape[1]
    tm = _tile(m, tm)
    if pieces:
        ns = n // pieces
        tn = _tile(ns, tn)
        r = ns // tn
        out = ((pieces, m, ns), out_dtype, pl.BlockSpec((None, tm, tn), lambda i, j, k: (j // r, i, j % r)))
    else:
        tn = _tile(n, tn)
        out = ((m, n), out_dtype, pl.BlockSpec((tm, tn), lambda i, j, k: (i, j)))
    return _mm(name, a, b, dims=TN, grid=(m // tm, n // tn, 1),
               a_spec=pl.BlockSpec((t, tm), lambda i, j, k: (0, i)),
               b_spec=pl.BlockSpec((t, tn), lambda i, j, k: (0, j)),
               outs=[out])[0]


def _rms_f(x, g):
    r = lax.rsqrt(jnp.mean(x * x, axis=-1, keepdims=True) + EPS)
    xh = x * r
    return xh * g, xh, r


def _rms_b(dy, g, xh, r):
    dxh = dy * g
    dx = r * (dxh - xh * jnp.mean(dxh * xh, axis=-1, keepdims=True))
    dg = jnp.sum(dy * xh, axis=0, keepdims=True)
    return dx, dg


def _rowwise(name, fn, row_ins, vec_ins, row_outs, vec_outs, deps=()):
    t = row_ins[0].shape[0]
    tr = min(ROW_TILE, t)
    nr, nv, no = len(row_ins), len(vec_ins), len(row_outs)
    first_out = nr + nv + len(deps)

    def body(*refs):
        ri, vi = refs[:nr], refs[nr:nr + nv]
        ro, vo = refs[first_out:first_out + no], refs[first_out + no:]
        routs, vouts = fn(*[r[...] for r in ri], *[v[...] for v in vi])
        for o, val in zip(ro, routs):
            o[...] = val.astype(o.dtype)
        i = pl.program_id(0)
        for o, val in zip(vo, vouts):
            @pl.when(i == 0)
            def _():
                o[...] = val

            @pl.when(i > 0)
            def _():
                o[...] += val

    res = pl.pallas_call(
        body, name=name,
        out_shape=[jax.ShapeDtypeStruct((t, w), d) for w, d in row_outs]
        + [jax.ShapeDtypeStruct((1, w), F32) for w in vec_outs],
        grid=(t // tr,),
        in_specs=[pl.BlockSpec((tr, a.shape[1]), lambda i: (i, 0)) for a in row_ins]
        + [pl.BlockSpec((1, a.shape[1]), lambda i: (0, 0)) for a in vec_ins] + [ANY_SPEC] * len(deps),
        out_specs=[pl.BlockSpec((tr, w), lambda i: (i, 0)) for w, _ in row_outs]
        + [pl.BlockSpec((1, w), lambda i: (0, 0)) for w in vec_outs],
        compiler_params=_params("arbitrary"),
    )(*row_ins, *vec_ins, *deps)
    return list(res)


def _norm_fwd(x, g, name, deps=()):
    d = x.shape[1]
    return _rowwise(name, lambda x, g: ((_rms_f(x, g)[0],), ()), [x], [g], [(d, BF16)], [], deps=deps)[0]


def _norm_bwd(x, dxn, dres, g, name):
    d = x.shape[1]

    def fn(x, dxn, dres, g):
        _, xh, r = _rms_f(x, g)
        dx, dg = _rms_b(dxn, g, xh, r)
        return (dres + dx,), (dg,)

    return _rowwise(name, fn, [x, dxn, dres], [g], [(d, F32)], [d])


def _block_fwd(h_prev, y, g_y, g_n, name):
    d = h_prev.shape[1]

    def fn(h_prev, y, g_y, g_n):
        h = h_prev + _rms_f(y, g_y)[0]
        return (h, _rms_f(h, g_n)[0]), ()

    return _rowwise(name, fn, [h_prev, y], [g_y, g_n], [(d, F32), (d, BF16)], [])


def _block_bwd(h, y, d_up, d_n, g_y, g_n, name, deps=()):
    d = h.shape[1]

    def fn(h, y, d_up, d_n, g_y, g_n):
        _, hh, hr = _rms_f(h, g_n)
        dh_n, dg_n = _rms_b(d_n, g_n, hh, hr)
        dh = d_up + dh_n
        _, yh, yr = _rms_f(y, g_y)
        dy, dg_y = _rms_b(dh, g_y, yh, yr)
        return (dh, dy), (dg_y, dg_n)

    return _rowwise(name, fn, [h, y, d_up, d_n], [g_y, g_n], [(d, F32), (d, BF16)], [d, d], deps=deps)


def _loss_head(h2, z, e, target, g, name):
    d = h2.shape[1]

    def fn(h2, z, e, target, g):
        gate = jax.nn.sigmoid(z)
        y, xh, r = _rms_f(e * gate, g)
        diff = h2 + y - target
        loss = 0.5 * jnp.sum(jnp.mean(diff * diff, axis=-1, keepdims=True), axis=0, keepdims=True)
        dh3 = diff / d
        deg, dg = _rms_b(dh3, g, xh, r)
        dz = deg * e * gate * (1.0 - gate)
        de = deg * gate
        return (dh3, dz, de), (dg, jnp.broadcast_to(loss, (1, LANES)))

    return _rowwise(name, fn, [h2, z, e, target], [g], [(d, F32), (d, BF16), (d, BF16)], [d, LANES])


def _shift_down(x, s, rows):
    return jnp.where(rows >= s, pltpu.roll(x, s, axis=0), 0.0)


def _shift_up(x, s, rows):
    t = x.shape[0]
    return jnp.where(rows < t - s, pltpu.roll(x, t - s, axis=0), 0.0)


def _conv_pre(x, w, rows):
    pre = x * w[QK_CONV - 1:QK_CONV, :]
    for s in range(1, QK_CONV):
        pre = pre + _shift_down(x, s, rows) * w[QK_CONV - 1 - s:QK_CONV - s, :]
    return pre


def _conv_fwd(proj_a, w, start, name):
    t = proj_a.shape[0]
    width = w.shape[1]
    tc = _tile(width, 256)
    off = start // tc

    def body(x_ref, w_ref, y_ref):
        x = x_ref[...]
        rows = lax.broadcasted_iota(jnp.int32, x.shape, 0)
        pre = _conv_pre(x, w_ref[...], rows)
        y_ref[...] = pre * jax.nn.sigmoid(pre)

    return pl.pallas_call(
        body, name=name,
        out_shape=jax.ShapeDtypeStruct((t, width), F32),
        grid=(width // tc,),
        in_specs=[pl.BlockSpec((t, tc), lambda j: (0, off + j)), pl.BlockSpec((QK_CONV, tc), lambda j: (0, j))],
        out_specs=pl.BlockSpec((t, tc), lambda j: (0, j)),
        compiler_params=_params("parallel"),
    )(proj_a, w)


def _conv_bwd(proj_a, w, dy, start, name):
    t = proj_a.shape[0]
    width = w.shape[1]
    tc = _tile(width, 256)
    off = start // tc

    def body(x_ref, w_ref, dy_ref, dx_ref, dw_ref):
        x, w = x_ref[...], w_ref[...]
        rows = lax.broadcasted_iota(jnp.int32, x.shape, 0)
        pre = _conv_pre(x, w, rows)
        sig = jax.nn.sigmoid(pre)
        dpre = dy_ref[...] * (sig * (1.0 + pre * (1.0 - sig)))
        dx = dpre * w[QK_CONV - 1:QK_CONV, :]
        dws = [jnp.sum(dpre * x, axis=0, keepdims=True)]
        for s in range(1, QK_CONV):
            dx = dx + _shift_up(dpre, s, rows) * w[QK_CONV - 1 - s:QK_CONV - s, :]
            dws.append(jnp.sum(dpre * _shift_down(x, s, rows), axis=0, keepdims=True))
        dx_ref[...] = dx.astype(dx_ref.dtype)
        for s in range(QK_CONV):
            dw_ref[QK_CONV - 1 - s:QK_CONV - s, :] = dws[s]

    return pl.pallas_call(
        body, name=name,
        out_shape=[jax.ShapeDtypeStruct((t, width), BF16), jax.ShapeDtypeStruct((QK_CONV, width), F32)],
        grid=(width // tc,),
        in_specs=[pl.BlockSpec((t, tc), lambda j: (0, off + j)), pl.BlockSpec((QK_CONV, tc), lambda j: (0, j)),
                  pl.BlockSpec((t, tc), lambda j: (0, j))],
        out_specs=[pl.BlockSpec((t, tc), lambda j: (0, j)), pl.BlockSpec((QK_CONV, tc), lambda j: (0, j))],
        compiler_params=_params("parallel"),
    )(proj_a, w, dy)


def _gates_fwd(proj_b, bias, col_block, heads, name):
    t = proj_b.shape[0]

    def body(x_ref, b_ref, o_ref):
        pre = x_ref[...] + b_ref[...]
        col = lax.broadcasted_iota(jnp.int32, pre.shape, 1)
        log_f = jnp.minimum(pre, 0.0) - jnp.log(1.0 + jnp.exp(-jnp.abs(pre)))
        o_ref[...] = jnp.where(col < heads, pre, jnp.where(col < 2 * heads, log_f, 0.0))

    return pl.pallas_call(
        body, name=name,
        out_shape=jax.ShapeDtypeStruct((t, LANES), F32),
        grid=(1,),
        in_specs=[pl.BlockSpec((t, LANES), lambda i: (0, col_block)), pl.BlockSpec((1, LANES), lambda i: (0, 0))],
        out_specs=pl.BlockSpec((t, LANES), lambda i: (0, 0)),
        compiler_params=_params("arbitrary"),
    )(proj_b, bias)


def _gates_bwd(proj_b, bias, dgl, col_block, heads, name):
    t = proj_b.shape[0]

    def body(x_ref, b_ref, d_ref, dx_ref, db_ref):
        pre = x_ref[...] + b_ref[...]
        col = lax.broadcasted_iota(jnp.int32, pre.shape, 1)
        d = d_ref[...]
        dx = jnp.where(col < heads, d, jnp.where(col < 2 * heads, d * jax.nn.sigmoid(-pre), 0.0))
        dx_ref[...] = dx.astype(dx_ref.dtype)
        db_ref[...] = jnp.sum(dx, axis=0, keepdims=True)

    return pl.pallas_call(
        body, name=name,
        out_shape=[jax.ShapeDtypeStruct((t, LANES), BF16), jax.ShapeDtypeStruct((1, LANES), F32)],
        grid=(1,),
        in_specs=[pl.BlockSpec((t, LANES), lambda i: (0, col_block)), pl.BlockSpec((1, LANES), lambda i: (0, 0)),
                  pl.BlockSpec((t, LANES), lambda i: (0, 0))],
        out_specs=[pl.BlockSpec((t, LANES), lambda i: (0, 0)), pl.BlockSpec((1, LANES), lambda i: (0, 0))],
        compiler_params=_params("arbitrary"),
    )(proj_b, bias, dgl)


def _dot(a, b, dims):
    return lax.dot_general(a.astype(BF16), b.astype(BF16), (dims, ((), ())), preferred_element_type=F32)


def _dot_split(ones, x):
    hi = x.astype(BF16)
    rest = x - hi.astype(F32)
    mid = rest.astype(BF16)
    lo = (rest - mid.astype(F32)).astype(BF16)
    dot = lambda part: lax.dot_general(ones, part, (NN, ((), ())), preferred_element_type=F32)
    return dot(hi) + dot(mid) + dot(lo)


def _chunk_gates(li, lf):
    n = li.shape[0]
    r = lax.broadcasted_iota(jnp.int32, (n, n), 0)
    c = lax.broadcasted_iota(jnp.int32, (n, n), 1)
    tri = (c <= r).astype(BF16)
    b_cols = _dot_split(tri, jnp.broadcast_to(lf, (n, n)))
    b_rows = b_cols.T
    li_rows = jnp.broadcast_to(li, (n, n)).T
    b = b_cols[:, 0:1]
    b_tot = b_cols[n - 1:n, 0:1]
    return b_cols, b_rows, li_rows, b, b_tot, c <= r


def _chunk_state(k, v, li, b, b_tot, ct, nrow, m_prev):
    a = b_tot - b + li
    m_new = jnp.maximum(b_tot + m_prev, jnp.max(a, axis=0, keepdims=True))
    dec = jnp.exp(b_tot + m_prev - m_new)
    w = jnp.exp(a - m_new)
    wv = w * v
    ct_new = dec * ct + _dot(k, wv, TN)
    n_new = dec * nrow + jnp.sum(w * k, axis=0, keepdims=True)
    return ct_new, n_new, m_new, dec, w, wv


def _chunk_out(q, k, v, gates, ct, nrow, m_prev):
    b_cols, b_rows, li_rows, b, _, causal = gates
    dlog = jnp.where(causal, b_cols - b_rows + li_rows, -jnp.inf)
    inter = b + m_prev
    mt = jnp.maximum(inter, jnp.max(dlog, axis=-1, keepdims=True))
    dw = jnp.exp(dlog - mt)
    iw = jnp.exp(inter - mt)
    a = _dot(q, k, NT)
    sm = a * dw
    qc = _dot(q, ct, NN)
    qn = jnp.sum(q * nrow, axis=-1, keepdims=True)
    num = iw * qc + _dot(sm, v, NN)
    den = iw * qn + jnp.sum(sm, axis=-1, keepdims=True)
    floor = jnp.exp(-mt)
    dn = jnp.maximum(jnp.abs(den), floor)
    h = num / dn
    return h, (dw, iw, a, sm, qc, qn, den, floor, dn)


def _head_out(h, o, g):
    cell, hh, r = _rms_f(h, g)
    sig = jax.nn.sigmoid(o)
    return sig * cell, (cell, hh, r, sig)


def _head_gates(gl_tile, head, heads):
    lane = lax.broadcasted_iota(jnp.int32, gl_tile.shape, 1)
    li = jnp.sum(jnp.where(lane == head, gl_tile, 0.0), axis=-1, keepdims=True)
    lf = jnp.sum(jnp.where(lane == heads + head, gl_tile, 0.0), axis=-1, keepdims=True)
    return li, lf


HEADS_PER_STEP = 2


def _mlstm_fwd(qc, kc, proj_a, gl, g_mlstm, heads, d_model, name):
    t = qc.shape[0]
    dh = g_mlstm.shape[1] // heads
    nc = t // CHUNK
    scale = dh ** -0.5
    hp = HEADS_PER_STEP

    def body(q_ref, k_ref, v_ref, o_ref, gl_ref, g_ref, out_ref, ct_ref, n_ref, m_ref):
        step = pl.program_id(0)
        ct_ref[...] = jnp.zeros_like(ct_ref)
        n_ref[...] = jnp.zeros_like(n_ref)
        m_ref[...] = jnp.zeros_like(m_ref)

        def chunk(ci, carry):
            sl = pl.ds(pl.multiple_of(ci * CHUNK, CHUNK), CHUNK)
            gl_tile = gl_ref[sl, :]
            for s in range(hp):
                cs = slice(s * dh, (s + 1) * dh)
                q, k, v = q_ref[sl, cs], k_ref[sl, cs] * scale, v_ref[sl, cs]
                li, lf = _head_gates(gl_tile, step * hp + s, heads)
                gates = _chunk_gates(li, lf)
                ct, nrow, m_prev = ct_ref[s], n_ref[s, 0:1, :], m_ref[s, 0:1, 0:1]
                h, _ = _chunk_out(q, k, v, gates, ct, nrow, m_prev)
                y, _ = _head_out(h, o_ref[sl, cs], g_ref[:, cs])
                out_ref[sl, cs] = y.astype(out_ref.dtype)
                ct_new, n_new, m_new, _, _, _ = _chunk_state(k, v, li, gates[3], gates[4], ct, nrow, m_prev)
                ct_ref[s] = ct_new
                n_ref[s, 0:1, :] = n_new
                m_ref[s, 0:1, 0:1] = m_new
            return carry

        lax.fori_loop(0, nc, chunk, 0, unroll=2)

    col = lambda off: pl.BlockSpec((t, hp * dh), lambda h: (0, off // hp + h))
    return pl.pallas_call(
        body, name=name,
        out_shape=jax.ShapeDtypeStruct((t, d_model), BF16),
        grid=(heads // hp,),
        in_specs=[col(0), col(0), col(2 * heads), col(3 * heads), pl.BlockSpec((t, LANES), lambda h: (0, 0)),
                  pl.BlockSpec((1, hp * dh), lambda h: (0, h))],
        out_specs=pl.BlockSpec((t, hp * dh), lambda h: (0, h)),
        scratch_shapes=[pltpu.VMEM((hp, dh, dh), F32), pltpu.VMEM((hp, 8, dh), F32), pltpu.VMEM((hp, 8, LANES), F32)],
        compiler_params=_params("arbitrary"),
    )(qc, kc, proj_a, proj_a, gl, g_mlstm)


def _mlstm_bwd(qc, kc, proj_a, gl, g_mlstm, dhcat, heads, name, dep):
    t = qc.shape[0]
    wm = g_mlstm.shape[1]
    dh = wm // heads
    nc = t // CHUNK
    scale = dh ** -0.5
    hp = HEADS_PER_STEP

    def body(q_ref, k_ref, v_ref, o_ref, gl_ref, g_ref, dy_ref, dep_ref,
             dq_ref, dk_ref, dv_ref, do_ref, dgl_ref, dg_ref,
             ct_all, n_all, m_all, dct_ref, dn_ref):
        del dep_ref
        grid_step = pl.program_id(0)

        @pl.when(grid_step == 0)
        def _():
            dgl_ref[...] = jnp.zeros_like(dgl_ref)

        def load(ci, s):
            sl = pl.ds(pl.multiple_of(ci * CHUNK, CHUNK), CHUNK)
            cs = slice(s * dh, (s + 1) * dh)
            li, lf = _head_gates(gl_ref[sl, :], grid_step * hp + s, heads)
            return sl, cs, q_ref[sl, cs], k_ref[sl, cs] * scale, v_ref[sl, cs], li, lf

        def fwd(ci, carry):
            new = []
            for s in range(hp):
                ct, nrow, m_prev = carry[s]
                ct_all[s, ci] = ct
                n_all[s, ci, 0:1, :] = nrow
                m_all[s, ci, 0:1, 0:1] = m_prev
                _, _, _, k, v, li, lf = load(ci, s)
                gates = _chunk_gates(li, lf)
                ct_new, n_new, m_new, _, _, _ = _chunk_state(k, v, li, gates[3], gates[4], ct, nrow, m_prev)
                new.append((ct_new, n_new, m_new))
            return tuple(new)

        zero = (jnp.zeros((dh, dh), F32), jnp.zeros((1, dh), F32), jnp.zeros((1, 1), F32))
        lax.fori_loop(0, nc, fwd, (zero,) * hp)

        dct_ref[...] = jnp.zeros_like(dct_ref)
        dn_ref[...] = jnp.zeros_like(dn_ref)
        dg_ref[...] = jnp.zeros_like(dg_ref)
        r_idx = lax.broadcasted_iota(jnp.int32, (CHUNK, 1), 0)
        rr = lax.broadcasted_iota(jnp.int32, (CHUNK, CHUNK), 0)
        cc = lax.broadcasted_iota(jnp.int32, (CHUNK, CHUNK), 1)
        tri_t = (cc >= rr).astype(BF16)

        def bwd(step, carry):
            for s in range(hp):
                bwd_head(nc - 1 - step, s)
            return carry

        def bwd_head(ci, s):
            sl, cs, q, k, v, li, lf = load(ci, s)
            head = grid_step * hp + s
            g = g_ref[:, cs]
            gates = _chunk_gates(li, lf)
            b, b_tot, causal = gates[3], gates[4], gates[5]
            ct, nrow, m_prev = ct_all[s, ci], n_all[s, ci, 0:1, :], m_all[s, ci, 0:1, 0:1]
            h, (dw, iw, a, sm, qc, qn, den, floor, dn) = _chunk_out(q, k, v, gates, ct, nrow, m_prev)
            o = o_ref[sl, cs]
            _, (cell, hh, r, sig) = _head_out(h, o, g)
            dy = dy_ref[sl, cs]
            dcell = dy * sig
            do_ref[sl, cs] = (dy * cell * sig * (1.0 - sig)).astype(do_ref.dtype)
            dh_, dg = _rms_b(dcell, g, hh, r)
            dg_ref[:, cs] += dg
            dnum = dh_ / dn
            ddn = -jnp.sum(dh_ * h, axis=-1, keepdims=True) / dn
            dden = jnp.where(jnp.abs(den) >= floor, ddn * jnp.sign(den), 0.0)
            idn = iw * dnum
            idd = iw * dden
            dq = _dot(idn, ct, NT) + idd * nrow
            dct = _dot(q, idn, TN)
            dnr = jnp.sum(q * idd, axis=0, keepdims=True)
            diw = jnp.sum(dnum * qc, axis=-1, keepdims=True) + dden * qn
            dsm = _dot(dnum, v, NT) + dden
            dv = _dot(sm, dnum, TN)
            da_ = dsm * dw
            e = jnp.where(causal, dsm * a * dw, 0.0)
            dq = dq + _dot(da_, k, NN)
            dk = _dot(da_, q, TN)
            e_cols = jnp.sum(e.T, axis=-1, keepdims=True)
            db = jnp.sum(e, axis=-1, keepdims=True) - e_cols + diw * iw
            dli = e_cols
            _, _, _, dec, w, wv = _chunk_state(k, v, li, b, b_tot, ct, nrow, m_prev)
            dct_new, dn_new = dct_ref[s], dn_ref[s, 0:1, :]
            ddec = jnp.sum(jnp.sum(dct_new * ct, axis=-1, keepdims=True), axis=0, keepdims=True) \
                + jnp.sum(dn_new * nrow, axis=-1, keepdims=True)
            dk = dk + _dot(wv, dct_new, NT) + w * dn_new
            dwv = _dot(k, dct_new, NN)
            dv = dv + w * dwv
            dwt = jnp.sum(dwv * v, axis=-1, keepdims=True) + jnp.sum(k * dn_new, axis=-1, keepdims=True)
            da = dwt * w
            dbtot = jnp.sum(da, axis=0, keepdims=True) + ddec * dec
            db = db - da + jnp.where(r_idx == CHUNK - 1, dbtot, 0.0)
            dli = dli + da
            dlf = _dot_split(tri_t, jnp.broadcast_to(db, (CHUNK, CHUNK)))[:, 0:1]
            dct_ref[s] = dec * dct_new + dct
            dn_ref[s, 0:1, :] = dec * dn_new + dnr
            dq_ref[sl, cs] = dq
            dk_ref[sl, cs] = dk * scale
            dv_ref[sl, cs] = dv.astype(dv_ref.dtype)
            lane = lax.broadcasted_iota(jnp.int32, (CHUNK, LANES), 1)
            dgl_ref[sl, :] += jnp.where(lane == head, dli, 0.0) + jnp.where(lane == heads + head, dlf, 0.0)

        lax.fori_loop(0, nc, bwd, 0, unroll=2)

    col = lambda off: pl.BlockSpec((t, hp * dh), lambda h: (0, off // hp + h))
    whole = pl.BlockSpec((t, LANES), lambda h: (0, 0))
    vec = pl.BlockSpec((1, hp * dh), lambda h: (0, h))
    return pl.pallas_call(
        body, name=name,
        out_shape=[jax.ShapeDtypeStruct((t, wm), F32)] * 2 + [jax.ShapeDtypeStruct((t, wm), BF16)] * 2
        + [jax.ShapeDtypeStruct((t, LANES), F32), jax.ShapeDtypeStruct((1, wm), F32)],
        grid=(heads // hp,),
        in_specs=[col(0), col(0), col(2 * heads), col(3 * heads), whole, vec, col(0), ANY_SPEC],
        out_specs=[col(0), col(0), col(0), col(0), whole, vec],
        scratch_shapes=[pltpu.VMEM((hp, nc, dh, dh), F32), pltpu.VMEM((hp, nc, 8, dh), F32),
                        pltpu.VMEM((hp, nc, 8, LANES), F32), pltpu.VMEM((hp, dh, dh), F32), pltpu.VMEM((hp, 8, dh), F32)],
        compiler_params=_params("arbitrary"),
    )(qc, kc, proj_a, proj_a, gl, g_mlstm, dhcat, dep)


def _pool_window(gi):
    return jnp.where(gi == 0, 2.0, jnp.where(gi == 1, 4.0, jnp.where(gi == 2, 8.0, 16.0))).astype(F32)


def _pool_select(gi, levels):
    return jnp.where(gi == 0, levels[0], jnp.where(gi == 1, levels[1], jnp.where(gi == 2, levels[2], levels[3])))


def _pooled(u, gi, rows):
    s, levels = u, []
    for lvl in range(POOL_GROUPS):
        s = s + _shift_down(s, 1 << lvl, rows)
        levels.append(s)
    cnt = jnp.minimum((rows + 1).astype(F32), _pool_window(gi))
    return _pool_select(gi, levels) / cnt - u, cnt


def _pool_fwd(proj_b, w_pool, scale, hcat, name):
    t, d_model = hcat.shape
    g, cg, _ = w_pool.shape
    off = (d_model - g * cg) // cg

    def body(u_ref, w_ref, s_ref, hcat_ref, out_ref):
        del hcat_ref
        gi = pl.program_id(0)
        u = u_ref[...]
        rows = lax.broadcasted_iota(jnp.int32, u.shape, 0)
        pooled, _ = _pooled(u, gi, rows)
        out_ref[...] = (_dot(pooled, w_ref[...], NN) * s_ref[...]).astype(out_ref.dtype)

    return pl.pallas_call(
        body, name=name,
        out_shape=jax.ShapeDtypeStruct((t, d_model), BF16),
        grid=(g,),
        in_specs=[pl.BlockSpec((t, cg), lambda gi: (0, gi)), pl.BlockSpec((None, cg, cg), lambda gi: (gi, 0, 0)),
                  pl.BlockSpec((1, cg), lambda gi: (0, gi)), HBM_SPEC],
        out_specs=pl.BlockSpec((t, cg), lambda gi: (0, off + gi)),
        input_output_aliases={3: 0},
        compiler_params=_params("arbitrary"),
    )(proj_b, w_pool, scale, hcat)


def _pool_bwd(proj_b, w_pool, scale, dhcat, name):
    t, d_model = dhcat.shape
    g, cg, _ = w_pool.shape
    off = (d_model - g * cg) // cg

    def body(u_ref, w_ref, s_ref, dy_ref, du_ref, dw_ref, ds_ref):
        gi = pl.program_id(0)
        u, w, dy = u_ref[...], w_ref[...], dy_ref[...]
        rows = lax.broadcasted_iota(jnp.int32, u.shape, 0)
        pooled, cnt = _pooled(u, gi, rows)
        mixed = _dot(pooled, w, NN)
        ds_ref[...] = jnp.sum(dy * mixed, axis=0, keepdims=True)
        dmixed = dy * s_ref[...]
        dw_ref[...] = _dot(pooled, dmixed, TN).astype(dw_ref.dtype)
        dpooled = _dot(dmixed, w, NT)
        s, levels = dpooled / cnt, []
        for lvl in range(POOL_GROUPS):
            s = s + _shift_up(s, 1 << lvl, rows)
            levels.append(s)
        du_ref[...] = (_pool_select(gi, levels) - dpooled).astype(du_ref.dtype)

    return pl.pallas_call(
        body, name=name,
        out_shape=[jax.ShapeDtypeStruct((t, g * cg), BF16), jax.ShapeDtypeStruct((g, cg, cg), BF16),
                   jax.ShapeDtypeStruct((1, g * cg), F32)],
        grid=(g,),
        in_specs=[pl.BlockSpec((t, cg), lambda gi: (0, gi)), pl.BlockSpec((None, cg, cg), lambda gi: (gi, 0, 0)),
                  pl.BlockSpec((1, cg), lambda gi: (0, gi)), pl.BlockSpec((t, cg), lambda gi: (0, off + gi))],
        out_specs=[pl.BlockSpec((t, cg), lambda gi: (0, gi)), pl.BlockSpec((None, cg, cg), lambda gi: (gi, 0, 0)),
                   pl.BlockSpec((1, cg), lambda gi: (0, gi))],
        compiler_params=_params("arbitrary"),
    )(proj_b, w_pool, scale, dhcat)


def _pad_cols(a, width):
    return jnp.pad(a, ((0, 0), (0, width - a.shape[1])))


def kernel(x, p, w_in, b_gates, w_qk_conv, g_mlstm, w_pool, pool_scale, w_out, g_mix_pre, g_mix_post, w_ff1, w_ff2, g_ff_pre, g_ff_post, w_ple_proj, w_ple_gate, g_ple_gate, g_ple_post, loss_target, m_w_in, m_b_gates, m_w_qk_conv, m_g_mlstm, m_w_pool, m_pool_scale, m_w_out, m_g_mix_pre, m_g_mix_post, m_w_ff1, m_w_ff2, m_g_ff_pre, m_g_ff_post, m_w_ple_proj, m_w_ple_gate, m_g_ple_gate, m_g_ple_post, v_w_in, v_b_gates, v_w_qk_conv, v_g_mlstm, v_w_pool, v_pool_scale, v_w_out, v_g_mix_pre, v_g_mix_post, v_w_ff1, v_w_ff2, v_g_ff_pre, v_g_ff_post, v_w_ple_proj, v_w_ple_gate, v_g_ple_gate, v_g_ple_post):
    weights = dict(w_in=w_in, b_gates=b_gates, w_qk_conv=w_qk_conv, g_mlstm=g_mlstm, w_pool=w_pool,
                   pool_scale=pool_scale, w_out=w_out, g_mix_pre=g_mix_pre, g_mix_post=g_mix_post, w_ff1=w_ff1,
                   w_ff2=w_ff2, g_ff_pre=g_ff_pre, g_ff_post=g_ff_post, w_ple_proj=w_ple_proj,
                   w_ple_gate=w_ple_gate, g_ple_gate=g_ple_gate, g_ple_post=g_ple_post)
    mom1 = dict(w_in=m_w_in, b_gates=m_b_gates, w_qk_conv=m_w_qk_conv, g_mlstm=m_g_mlstm, w_pool=m_w_pool,
                pool_scale=m_pool_scale, w_out=m_w_out, g_mix_pre=m_g_mix_pre, g_mix_post=m_g_mix_post,
                w_ff1=m_w_ff1, w_ff2=m_w_ff2, g_ff_pre=m_g_ff_pre, g_ff_post=m_g_ff_post,
                w_ple_proj=m_w_ple_proj, w_ple_gate=m_w_ple_gate, g_ple_gate=m_g_ple_gate, g_ple_post=m_g_ple_post)
    mom2 = dict(w_in=v_w_in, b_gates=v_b_gates, w_qk_conv=v_w_qk_conv, g_mlstm=v_g_mlstm, w_pool=v_w_pool,
                pool_scale=v_pool_scale, w_out=v_w_out, g_mix_pre=v_g_mix_pre, g_mix_post=v_g_mix_post,
                w_ff1=v_w_ff1, w_ff2=v_w_ff2, g_ff_pre=v_g_ff_pre, g_ff_post=v_g_ff_post,
                w_ple_proj=v_w_ple_proj, w_ple_gate=v_w_ple_gate, g_ple_gate=v_g_ple_gate, g_ple_post=v_g_ple_post)
    order = list(weights)

    t, d = x.shape[1], x.shape[2]
    heads = MLSTM_HEADS
    wm = g_mlstm.shape[1]
    pw = pool_scale.shape[1]
    cg = pw // POOL_GROUPS
    in_cols = 4 * wm + 2 * heads + pw
    in_shard = w_in.shape[2]
    xs, ps, tgt = x[0], p[0, 0], loss_target[0]

    mx, my, mc = lax.axis_index("x"), lax.axis_index("y"), lax.axis_index("c")
    core = jnp.reshape(mc, (1,)).astype(jnp.int32)
    chip = jnp.reshape(2 * mx + my, (1,)).astype(jnp.int32)

    groups = [[w_in[0].T.astype(BF16), w_qk_conv[0], w_pool[0].reshape(POOL_GROUPS * w_pool.shape[2], cg).astype(BF16)],
              [w_out[0].astype(BF16), w_ff1[0].astype(BF16)],
              [w_ff2[0].astype(BF16), w_ple_gate[0].astype(BF16), w_ple_proj[0].astype(BF16)]]
    started, token = _gather_start(groups, "gather_start")
    bias = _pad_cols(b_gates, LANES)

    xn = _norm_fwd(xs, g_mix_pre, "norm_mix_pre", deps=[token])
    g_in, g_conv, g_pool = _forward_pair(_gather_wait(started[0], xn, "gather_wait_in"), "gather_pair_in")
    in_t_a = g_in.reshape(in_cols, d)
    in_t_b = jnp.concatenate([in_t_a[4 * wm + 2 * heads:], in_t_a[4 * wm:4 * wm + 2 * heads],
                              jnp.zeros((LANES - 2 * heads, d), BF16)], axis=0)
    conv_w = g_conv.transpose(1, 0, 2).reshape(QK_CONV, 2 * wm)
    pool_w = g_pool.reshape(N_DEV, POOL_GROUPS, cg // N_DEV, cg).transpose(1, 0, 2, 3).reshape(POOL_GROUPS, cg, cg)
    proj_a = _mm_nt("proj_qkvo", xn, in_t_a, n_rows=4 * wm)
    proj_b = _mm_nt("proj_pool_gates", xn, in_t_b)
    qc = _conv_fwd(proj_a, conv_w[:, :wm], 0, "q_conv")
    kc = _conv_fwd(proj_a, conv_w[:, wm:], wm, "k_conv")
    gl = _gates_fwd(proj_b, bias, pw // LANES, heads, "gates")
    hcat = _mlstm_fwd(qc, kc, proj_a, gl, g_mlstm, heads, d, "mlstm")
    hcat = _pool_fwd(proj_b, pool_w, pool_scale, hcat, "pool")
    g_out, g_ff1 = _forward_pair(_gather_wait(started[1], hcat, "gather_wait_ff1"), "gather_pair_ff1")
    out_w = g_out.reshape(d, d)
    mix = _mm_nn("mix_out", hcat, out_w)[0]
    h1, hn = _block_fwd(xs, mix, g_mix_post, g_ff_pre, "residual_mix")
    relu_a, act = _mm_nn_pieces("ff1", hn, g_ff1, out_dtype=[BF16, BF16],
                                epilogue=lambda acc: (jnp.maximum(acc, 0.0), jnp.square(jnp.maximum(acc, 0.0))))
    g_ff2, g_pgate, g_pproj = _forward_pair(_gather_wait(started[2], act, "gather_wait_ff2"), "gather_pair_ff2")
    ff2_w = g_ff2.reshape(-1, d)
    pgate_w = g_pgate.reshape(d, d)
    ff = _mm_nn("ff2", act, ff2_w)[0]
    h2, hg = _block_fwd(h1, ff, g_ff_post, g_ple_gate, "residual_ff")
    z = _mm_nn("ple_gate", hg, pgate_w)[0]
    e = _mm_nn_pieces("ple_proj", ps, g_pproj)[0]
    dh3, dz, de, dg_ple_post, loss_part = _loss_head(h2, z, e, tgt, g_ple_post, "loss_head")
    loss = lax.psum(loss_part[0, 0], MESH_AXES)

    out = {}

    def view(k, a):
        return a[0].T if k == "w_in" else a.reshape(-1, a.shape[-1])

    def reduce_start(pieces, tag):
        keys = list(pieces)
        got = _pair_exchange([pieces[k] for k in keys], "reduce_pair_" + tag)
        sums = [_pair_sum(pieces[k], g, core, "pair_sum_" + k) for k, g in zip(keys, got)]
        return keys, _chip_start(sums, "reduce_start_" + tag)

    def reduce_finish(group, after, tag):
        keys, begun = group
        sums, got = _chip_wait(begun, after, "reduce_wait_" + tag)
        for k, s, g in zip(keys, sums, got):
            res = _adamw_shard(view(k, weights[k]), view(k, mom1[k]), view(k, mom2[k]), s, g, chip, "adamw_" + k)
            out[k] = [r.T[None] if k == "w_in" else r.reshape(weights[k].shape) for r in res]

    dhg = _mm_nt("d_ple_gate_in", dz, pgate_w)
    gw_pgate = _mm_tn("gw_ple_gate", hg, dz)
    gw_pproj = _mm_tn("gw_ple_proj", ps, de, pieces=N_DEV)
    red_ple = reduce_start({"w_ple_gate": gw_pgate.reshape(N_DEV, d // N_DEV, d), "w_ple_proj": gw_pproj}, "ple")
    dh2, dff, dg_ff_post, dg_ple_gate = _block_bwd(h2, ff, dh3, dhg, g_ff_post, g_ple_gate, "d_residual_ff",
                                                   deps=[red_ple[1]["token"]])
    da = _mm_nt("d_ff2_in", dff, ff2_w, out_dtype=BF16, extra=[relu_a],
                epilogue=lambda acc, r: (acc * (2.0 * r.astype(F32)),))
    gw_ff2 = _mm_tn("gw_ff2", act, dff)
    red_ff2 = reduce_start({"w_ff2": gw_ff2.reshape(N_DEV, -1, d)}, "ff2")
    reduce_finish(red_ple, red_ff2[1]["token"], "ple")
    dhn = _mm_nt_pieces("d_ff1_in", da, g_ff1, deps=[red_ff2[1]["token"]])
    gw_ff1 = _mm_tn("gw_ff1", hn, da, pieces=N_DEV)
    red_ff1 = reduce_start({"w_ff1": gw_ff1}, "ff1")
    reduce_finish(red_ff2, red_ff1[1]["token"], "ff2")
    dh1, dmix, dg_mix_post, dg_ff_pre = _block_bwd(h1, mix, dh2, dhn, g_mix_post, g_ff_pre, "d_residual_mix",
                                                   deps=[red_ff1[1]["token"]])
    dhcat = _mm_nt("d_mix_in", dmix, out_w)
    gw_out = _mm_tn("gw_out", hcat, dmix)
    du, gw_pool, dg_pool_scale = _pool_bwd(proj_b, pool_w, pool_scale, dhcat, "d_pool")
    red_out = reduce_start({
        "w_out": gw_out.reshape(N_DEV, d // N_DEV, d),
        "w_pool": gw_pool.reshape(POOL_GROUPS, N_DEV, cg // N_DEV, cg).transpose(1, 0, 2, 3).reshape(N_DEV, -1, cg)},
        "out")
    reduce_finish(red_ff1, red_out[1]["token"], "ff1")
    dq, dk, dv, do, dgl, dg_mlstm = _mlstm_bwd(qc, kc, proj_a, gl, g_mlstm, dhcat, heads, "d_mlstm",
                                               red_out[1]["token"])
    dgates, dbias = _gates_bwd(proj_b, bias, dgl, pw // LANES, heads, "d_gates")
    dq_pre, gw_conv_q = _conv_bwd(proj_a, conv_w[:, :wm], dq, 0, "d_q_conv")
    dk_pre, gw_conv_k = _conv_bwd(proj_a, conv_w[:, wm:], dk, wm, "d_k_conv")
    gw_conv = jnp.concatenate([gw_conv_q, gw_conv_k], axis=1)
    dproj_a = jnp.concatenate([dq_pre, dk_pre, dv, do], axis=1)
    dproj_b = jnp.concatenate([du, dgates], axis=1)
    gw_a = _mm_tn("gw_in_qkvo", dproj_a, xn)
    gw_b = _mm_tn("gw_in_pool_gates", dproj_b, xn)
    gw_in_t = jnp.concatenate([gw_a, gw_b[pw:pw + 2 * heads], gw_b[:pw]], axis=0)
    red_in = reduce_start({
        "w_in": gw_in_t.reshape(N_DEV, in_shard, d),
        "w_qk_conv": gw_conv.reshape(QK_CONV, N_DEV, -1).transpose(1, 0, 2).astype(BF16)}, "in")
    reduce_finish(red_out, red_in[1]["token"], "out")
    dxn_a = _mm_nn("d_proj_qkvo_in", dproj_a, in_t_a, deps=[red_in[1]["token"]])[0]
    dxn = _mm_nn("d_proj_in", dproj_b, in_t_b, extra=[dxn_a], epilogue=lambda acc, prev: (acc + prev,))[0]
    grad_x, dg_mix_pre = _norm_bwd(xs, dxn, dh1, g_mix_pre, "d_norm_mix_pre")

    small = ["b_gates", "g_mlstm", "pool_scale", "g_mix_pre", "g_mix_post", "g_ff_pre", "g_ff_post", "g_ple_gate",
             "g_ple_post"]
    small_grads = dict(b_gates=dbias[:, :2 * heads], g_mlstm=dg_mlstm, pool_scale=dg_pool_scale, g_mix_pre=dg_mix_pre,
                       g_mix_post=dg_mix_post, g_ff_pre=dg_ff_pre, g_ff_post=dg_ff_post, g_ple_gate=dg_ple_gate,
                       g_ple_post=dg_ple_post)
    pack = lambda src: jnp.concatenate([_pad_cols(src[k], d) for k in small], axis=0)
    small_parts = _all_gather([_pad_rows16(pack(small_grads))], "gather_small_grads")[0]
    res = _adamw_replicated(_pad_rows16(pack(weights)), _pad_rows16(pack(mom1)), _pad_rows16(pack(mom2)), small_parts,
                            "adamw_replicated")
    for i, k in enumerate(small):
        width = weights[k].shape[1]
        out[k] = [r[i:i + 1, :width] for r in res]
    reduce_finish(red_in, res[0], "in")

    return (loss, grad_x[None], *[out[k][0] for k in order], *[out[k][1] for k in order],
            *[out[k][2] for k in order], *[out[k][3] for k in order])


def _pad_rows16(a):
    return jnp.pad(a, ((0, 16 - a.shape[0]), (0, 0)))
```

```python
import functools

import jax
import jax.numpy as jnp
from jax import lax
from jax.experimental import pallas as pl
from jax.experimental.pallas import tpu as pltpu

F32 = jnp.float32
BF16 = jnp.bfloat16
EPS = 1e-6
N_DEV = 8
N_CHIP = 4
LANES = 128
VMEM_LIMIT = 56 * 1024 * 1024
MLSTM_HEADS = 8
POOL_GROUPS = 4
QK_CONV = 4
CHUNK = 128
ROW_TILE = 256
ADAM_LR, ADAM_B1, ADAM_B2, ADAM_EPS, ADAM_WD, ADAM_STEP = 0.001, 0.9, 0.999, 1e-08, 0.01, 10
MESH_AXES = ("x", "y", "c")
MESH = pl.DeviceIdType.MESH
HBM_SPEC = pl.BlockSpec(memory_space=pltpu.HBM)
SEM_SPEC = pl.BlockSpec(memory_space=pltpu.SEMAPHORE)
DATAFLOW = pltpu.SideEffectType.DATAFLOW_SIDE_EFFECTING


def _params(*sem):
    if sem:
        return pltpu.CompilerParams(dimension_semantics=sem, vmem_limit_bytes=VMEM_LIMIT)
    return pltpu.CompilerParams(vmem_limit_bytes=VMEM_LIMIT)


def _tile(n, pref):
    if n <= pref:
        return n
    t = (pref // LANES) * LANES
    while t >= LANES:
        if n % t == 0:
            return t
        t -= LANES
    return n


def _all_gather(arrs, name):
    n = len(arrs)

    def body(*refs):
        ins, outs = refs[:n], refs[n:2 * n]
        send, recv, loc = refs[2 * n:]
        x, y, c = lax.axis_index("x"), lax.axis_index("y"), lax.axis_index("c")
        me = (x, y, c)
        sibling = (x, y, 1 - c)
        chips = [(1 - x, y), (x, 1 - y), (1 - x, 1 - y)]

        def row(dev):
            return 4 * dev[0] + 2 * dev[1] + dev[2]

        def copy(w, k, block_of, to, src=None):
            dst = outs[w].at[row(block_of)]
            return pltpu.make_async_remote_copy(
                src_ref=dst if src is None else src, dst_ref=dst,
                send_sem=send.at[7 * w + k], recv_sem=recv.at[7 * w + k],
                device_id=to, device_id_type=MESH)

        started = []
        local = []
        for w in range(n):
            cp = pltpu.make_async_copy(ins[w], outs[w].at[row(me)], loc.at[w])
            cp.start()
            local.append(cp)
            first = [copy(w, 0, me, sibling, src=ins[w])]
            first += [copy(w, 1 + j, me, (*chip, c), src=ins[w]) for j, chip in enumerate(chips)]
            for cp in first:
                cp.start()
            started += first
        for j, chip in enumerate(chips):
            for w in range(n):
                copy(w, 1 + j, (*chip, c), me).wait_recv()
                fwd = copy(w, 4 + j, (*chip, c), sibling)
                fwd.start()
                started.append(fwd)
        for w in range(n):
            copy(w, 0, sibling, me).wait_recv()
            for j, chip in enumerate(chips):
                copy(w, 4 + j, (*chip, 1 - c), me).wait_recv()
        for cp in started:
            cp.wait_send()
        for cp in local:
            cp.wait()

    outs = pl.pallas_call(
        body, name=name,
        out_shape=[jax.ShapeDtypeStruct((N_DEV,) + a.shape, a.dtype) for a in arrs],
        in_specs=[HBM_SPEC] * n, out_specs=[HBM_SPEC] * n,
        scratch_shapes=[pltpu.SemaphoreType.DMA((7 * n,)), pltpu.SemaphoreType.DMA((7 * n,)),
                        pltpu.SemaphoreType.DMA((n,))],
    )(*arrs)
    return list(outs)


def _pair_exchange(pieces, name):
    n = len(pieces)

    def body(*refs):
        ins, outs = refs[:n], refs[n:2 * n]
        send, recv = refs[2 * n:]
        x, y, c = lax.axis_index("x"), lax.axis_index("y"), lax.axis_index("c")
        sibling = (x, y, 1 - c)

        def copy(w, chip, src_core):
            return pltpu.make_async_remote_copy(
                src_ref=ins[w].at[2 * chip + src_core], dst_ref=outs[w].at[chip],
                send_sem=send.at[N_CHIP * w + chip], recv_sem=recv.at[N_CHIP * w + chip],
                device_id=sibling, device_id_type=MESH)

        cps = [copy(w, chip, 1 - c) for w in range(n) for chip in range(N_CHIP)]
        for cp in cps:
            cp.start()
        for cp in cps:
            cp.wait_recv()
        for cp in cps:
            cp.wait_send()

    outs = pl.pallas_call(
        body, name=name,
        out_shape=[jax.ShapeDtypeStruct((N_CHIP,) + a.shape[1:], a.dtype) for a in pieces],
        in_specs=[HBM_SPEC] * n, out_specs=[HBM_SPEC] * n,
        scratch_shapes=[pltpu.SemaphoreType.DMA((N_CHIP * n,)), pltpu.SemaphoreType.DMA((N_CHIP * n,))],
    )(*pieces)
    return list(outs)


def _hbm(a):
    return pltpu.with_memory_space_constraint(a, pltpu.HBM)


def _mesh_place():
    x, y, c = lax.axis_index("x"), lax.axis_index("y"), lax.axis_index("c")
    return x, y, c, [(1 - x, y), (x, 1 - y), (1 - x, 1 - y)]


def _chip_copies(srcs, lands, send, recv):
    x, y, c, chips = _mesh_place()
    return [pltpu.make_async_remote_copy(
        src_ref=srcs[w].at[2 * chip[0] + chip[1]], dst_ref=lands[w].at[k],
        send_sem=send.at[3 * w + k], recv_sem=recv.at[3 * w + k],
        device_id=(*chip, c), device_id_type=MESH)
        for k, chip in enumerate(chips) for w in range(len(srcs))]


def _chip_start(sums, name):
    return _split_start(_chip_copies, sums, [lax.empty((3,) + a.shape[1:], a.dtype) for a in sums], 3, name)


def _chip_wait(started, after, name):
    return _split_wait(_chip_copies, started, after, name)


def _pair_start(pieces, name):
    return _split_start(_pair_copies, pieces, [lax.empty((N_CHIP,) + a.shape[1:], a.dtype) for a in pieces], N_CHIP, name)


def _pair_wait(started, after, name):
    return _split_wait(_pair_copies, started, after, name)


def _pair_copies(srcs, lands, send, recv):
    x, y, c, _ = _mesh_place()
    return [pltpu.make_async_remote_copy(
        src_ref=srcs[w].at[2 * ch + 1 - c], dst_ref=lands[w].at[ch],
        send_sem=send.at[N_CHIP * w + ch], recv_sem=recv.at[N_CHIP * w + ch],
        device_id=(x, y, 1 - c), device_id_type=MESH)
        for w in range(len(srcs)) for ch in range(N_CHIP)]


def _split_start(copies_of, srcs, lands, per_array, name):
    n = len(srcs)

    def body(*refs):
        for cp in copies_of(refs[:n], refs[n:2 * n], refs[2 * n], refs[2 * n + 1]):
            cp.start()
        refs[-1][...] = jnp.zeros_like(refs[-1])

    res = pl.pallas_call(
        body, name=name,
        out_shape=[pltpu.SemaphoreType.DMA((per_array * n,)), pltpu.SemaphoreType.DMA((per_array * n,))]
        + [pltpu.HBM(a.shape, a.dtype) for a in srcs + lands] + [jax.ShapeDtypeStruct((8, LANES), F32)],
        in_specs=[HBM_SPEC] * (2 * n),
        out_specs=[SEM_SPEC, SEM_SPEC] + [HBM_SPEC] * (2 * n) + [pl.BlockSpec(memory_space=pltpu.VMEM)],
        input_output_aliases={i: 2 + i for i in range(2 * n)},
        compiler_params=pltpu.CompilerParams(has_side_effects=DATAFLOW),
    )(*[_hbm(a) for a in srcs], *[_hbm(a) for a in lands])
    return dict(send=res[0], recv=res[1], srcs=list(res[2:2 + n]), lands=list(res[2 + n:2 + 2 * n]), token=res[-1])


def _split_wait(copies_of, started, after, name):
    n = len(started["srcs"])

    def body(*refs):
        for cp in copies_of(refs[:n], refs[n:2 * n], refs[2 * n], refs[2 * n + 1]):
            cp.wait_send()
            cp.wait_recv()

    res = pl.pallas_call(
        body, name=name,
        out_shape=[pltpu.HBM(a.shape, a.dtype) for a in started["srcs"] + started["lands"]],
        in_specs=[HBM_SPEC] * (2 * n) + [SEM_SPEC, SEM_SPEC, ANY_SPEC],
        out_specs=[HBM_SPEC] * (2 * n),
        input_output_aliases={i: i for i in range(2 * n)},
        compiler_params=pltpu.CompilerParams(has_side_effects=DATAFLOW),
    )(*started["srcs"], *started["lands"], started["send"], started["recv"], after)
    return list(res[:n]), list(res[n:])


def _gather_copies(srcs, lands, send, recv):
    x, y, c, chips = _mesh_place()
    me = 4 * x + 2 * y + c
    targets = [(x, y, 1 - c)] + [(*chip, c) for chip in chips]
    return [pltpu.make_async_remote_copy(
        src_ref=srcs[w], dst_ref=lands[w].at[me], send_sem=send.at[4 * w + k], recv_sem=recv.at[4 * w + k],
        device_id=to, device_id_type=MESH)
        for w in range(len(srcs)) for k, to in enumerate(targets)]


def _gather_start(groups, name):
    sizes = [len(g) for g in groups]
    flat = [a for g in groups for a in g]
    n, ng = len(flat), len(groups)
    me = 4 * lax.axis_index("x") + 2 * lax.axis_index("y") + lax.axis_index("c")
    lands = [lax.dynamic_update_slice(lax.empty((N_DEV,) + a.shape, a.dtype), a[None], (me,) + (0,) * a.ndim)
             for a in flat]

    def body(*refs):
        srcs, zones = refs[:n], refs[n:2 * n]
        sems = refs[2 * n:2 * n + 2 * ng]
        token = refs[-1]
        lo = 0
        for gi, size in enumerate(sizes):
            for cp in _gather_copies(srcs[lo:lo + size], zones[lo:lo + size], sems[2 * gi], sems[2 * gi + 1]):
                cp.start()
            lo += size
        token[...] = jnp.zeros_like(token)

    sem_shapes = []
    for size in sizes:
        sem_shapes += [pltpu.SemaphoreType.DMA((4 * size,))] * 2
    res = pl.pallas_call(
        body, name=name,
        out_shape=sem_shapes + [pltpu.HBM(a.shape, a.dtype) for a in flat + lands] + [jax.ShapeDtypeStruct((8, LANES), F32)],
        in_specs=[HBM_SPEC] * (2 * n),
        out_specs=[SEM_SPEC] * (2 * ng) + [HBM_SPEC] * (2 * n) + [pl.BlockSpec(memory_space=pltpu.VMEM)],
        input_output_aliases={i: 2 * ng + i for i in range(2 * n)},
        compiler_params=pltpu.CompilerParams(has_side_effects=DATAFLOW),
    )(*[_hbm(a) for a in flat], *[_hbm(a) for a in lands])
    out, lo = [], 0
    for gi, size in enumerate(sizes):
        out.append(dict(send=res[2 * gi], recv=res[2 * gi + 1],
                        srcs=list(res[2 * ng + lo:2 * ng + lo + size]),
                        lands=list(res[2 * ng + n + lo:2 * ng + n + lo + size])))
        lo += size
    return out, res[-1]


def _gather_wait(started, after, name):
    n = len(started["srcs"])

    def body(*refs):
        srcs, zones = refs[:n], refs[n:2 * n]
        send, recv = refs[2 * n], refs[2 * n + 1]
        for cp in _gather_copies(srcs, zones, send, recv):
            cp.wait_send()
            cp.wait_recv()

    res = pl.pallas_call(
        body, name=name,
        out_shape=[pltpu.HBM(a.shape, a.dtype) for a in started["srcs"] + started["lands"]],
        in_specs=[HBM_SPEC] * (2 * n) + [SEM_SPEC, SEM_SPEC, pl.BlockSpec(memory_space=pl.ANY)],
        out_specs=[HBM_SPEC] * (2 * n),
        input_output_aliases={i: i for i in range(2 * n)},
        compiler_params=pltpu.CompilerParams(has_side_effects=DATAFLOW),
    )(*started["srcs"], *started["lands"], started["send"], started["recv"], after)
    return list(res[n:])


def _forward_pair(lands, name):
    n = len(lands)

    def body(*refs):
        zones = refs[:n]
        send, recv = refs[2 * n:]
        x, y, c, chips = _mesh_place()

        def copy(w, j, core):
            rows = zones[w].at[4 * chips[j][0] + 2 * chips[j][1] + core]
            return pltpu.make_async_remote_copy(
                src_ref=rows, dst_ref=rows, send_sem=send.at[3 * w + j], recv_sem=recv.at[3 * w + j],
                device_id=(x, y, 1 - c), device_id_type=MESH)

        sends = [copy(w, j, c) for w in range(n) for j in range(3)]
        for cp in sends:
            cp.start()
        for w in range(n):
            for j in range(3):
                copy(w, j, 1 - c).wait_recv()
        for cp in sends:
            cp.wait_send()

    return list(pl.pallas_call(
        body, name=name,
        out_shape=[jax.ShapeDtypeStruct(a.shape, a.dtype) for a in lands],
        in_specs=[HBM_SPEC] * n, out_specs=[HBM_SPEC] * n,
        input_output_aliases={i: i for i in range(n)},
        scratch_shapes=[pltpu.SemaphoreType.DMA((3 * n,)), pltpu.SemaphoreType.DMA((3 * n,))],
    )(*lands))


def _pair_sum(pieces, got, core, name):
    _, rows, cols = pieces.shape
    tr, tc = _tile_2d(rows, cols)

    def body(core_ref, mine_ref, got_ref, out_ref):
        del core_ref
        out_ref[...] = (mine_ref[...].astype(F32) + got_ref[...].astype(F32)).astype(out_ref.dtype)

    return pl.pallas_call(
        body, name=name,
        out_shape=jax.ShapeDtypeStruct((N_CHIP, rows, cols), pieces.dtype),
        grid_spec=pltpu.PrefetchScalarGridSpec(
            num_scalar_prefetch=1, grid=(N_CHIP, rows // tr, cols // tc),
            in_specs=[pl.BlockSpec((None, tr, tc), lambda ch, i, j, core: (2 * ch + core[0], i, j)),
                      pl.BlockSpec((None, tr, tc), lambda ch, i, j, core: (ch, i, j))],
            out_specs=pl.BlockSpec((None, tr, tc), lambda ch, i, j, core: (ch, i, j))),
        compiler_params=_params("parallel", "parallel", "parallel"),
    )(core, pieces, got)


def _tile_2d(rows, cols, budget=2 * 1024 * 1024):
    want = max(16, budget // (4 * cols))
    if rows <= want:
        return rows, cols
    t = (want // 16) * 16
    while t >= 16:
        if rows % t == 0:
            return t, cols
        t -= 16
    return rows, _tile(cols, max(LANES, budget // (4 * rows)))


def _adamw_math(w, g, m, v):
    m = ADAM_B1 * m + (1.0 - ADAM_B1) * g
    v = ADAM_B2 * v + (1.0 - ADAM_B2) * (g * g)
    m_hat = m / (1.0 - ADAM_B1 ** ADAM_STEP)
    v_hat = v / (1.0 - ADAM_B2 ** ADAM_STEP)
    delta = -ADAM_LR * (m_hat / (jnp.sqrt(v_hat) + ADAM_EPS) + ADAM_WD * w)
    return delta, m, v


def _adamw_shard(w, m, v, sums, got, chip, name):
    rows, cols = w.shape
    tr, tc = _tile_2d(rows, cols, budget=1024 * 1024)

    def body(chip_ref, w_ref, m_ref, v_ref, own_ref, got_ref, g_out, d_out, m_out, v_out):
        del chip_ref
        g = own_ref[...].astype(F32)
        for k in range(3):
            g = g + got_ref[k].astype(F32)
        delta, m_new, v_new = _adamw_math(w_ref[...], g, m_ref[...], v_ref[...])
        g_out[...] = g
        d_out[...] = delta
        m_out[...] = m_new
        v_out[...] = v_new

    blk = pl.BlockSpec((tr, tc), lambda i, j, chip: (i, j))
    return pl.pallas_call(
        body, name=name,
        out_shape=[jax.ShapeDtypeStruct((rows, cols), F32)] * 4,
        grid_spec=pltpu.PrefetchScalarGridSpec(
            num_scalar_prefetch=1, grid=(rows // tr, cols // tc),
            in_specs=[blk, blk, blk,
                      pl.BlockSpec((None, tr, tc), lambda i, j, chip: (chip[0], i, j)),
                      pl.BlockSpec((3, tr, tc), lambda i, j, chip: (0, i, j))],
            out_specs=[blk] * 4),
        compiler_params=_params("parallel", "parallel"),
    )(chip, w, m, v, sums, got)


def _adamw_replicated(w, m, v, parts, name):
    rows, cols = w.shape

    def body(w_ref, m_ref, v_ref, parts_ref, g_out, d_out, m_out, v_out):
        g = parts_ref[0]
        for d in range(1, N_DEV):
            g = g + parts_ref[d]
        delta, m_new, v_new = _adamw_math(w_ref[...], g, m_ref[...], v_ref[...])
        g_out[...] = g
        d_out[...] = delta
        m_out[...] = m_new
        v_out[...] = v_new

    return pl.pallas_call(
        body, name=name,
        out_shape=[jax.ShapeDtypeStruct((rows, cols), F32)] * 4,
        compiler_params=_params(),
    )(w, m, v, parts)


NN = ((1,), (0,))
NT = ((1,), (1,))
TN = ((0,), (0,))


ANY_SPEC = pl.BlockSpec(memory_space=pl.ANY)


def _mm(name, a, b, *, dims, grid, a_spec, b_spec, outs, extra=(), extra_specs=(), epilogue=None, acc_shape=None,
        deps=()):
    nk = grid[2]
    n_extra, n_out = len(extra), len(outs)
    first_out = 2 + n_extra + len(deps)

    def body(*refs):
        a_ref, b_ref = refs[0], refs[1]
        ex = refs[2:2 + n_extra]
        out_refs = refs[first_out:first_out + n_out]
        part = lax.dot_general(a_ref[...].astype(BF16), b_ref[...].astype(BF16), (dims, ((), ())),
                               preferred_element_type=F32)

        def finish(acc):
            res = epilogue(acc, *[e[...] for e in ex]) if epilogue else (acc,)
            for val, o in zip(res, out_refs):
                o[...] = val.astype(o.dtype)

        if nk == 1:
            finish(part)
        else:
            acc_ref = refs[-1]
            k = pl.program_id(2)

            @pl.when(k == 0)
            def _():
                acc_ref[...] = part

            @pl.when(k > 0)
            def _():
                acc_ref[...] += part

            @pl.when(k == nk - 1)
            def _():
                finish(acc_ref[...])

    res = pl.pallas_call(
        body, name=name,
        out_shape=[jax.ShapeDtypeStruct(s, d) for s, d, _ in outs],
        grid=grid,
        in_specs=[a_spec, b_spec, *extra_specs] + [ANY_SPEC] * len(deps),
        out_specs=[sp for _, _, sp in outs],
        scratch_shapes=[pltpu.VMEM(acc_shape, F32)] if nk > 1 else [],
        compiler_params=_params("parallel", "parallel", "arbitrary"),
    )(a, b, *extra, *deps)
    return list(res)


def _mm_nn(name, a, b, out_dtype=F32, tm=1024, tn=1024, tk=2048, epilogue=None, n_out=1, extra=(), deps=()):
    m, kd = a.shape
    n = b.shape[1]
    tm, tn, tk = _tile(m, tm), _tile(n, tn), _tile(kd, tk)
    dts = out_dtype if isinstance(out_dtype, (list, tuple)) else [out_dtype] * n_out
    o_spec = pl.BlockSpec((tm, tn), lambda i, j, k: (i, j))
    return _mm(name, a, b, dims=NN, grid=(m // tm, n // tn, kd // tk),
               a_spec=pl.BlockSpec((tm, tk), lambda i, j, k: (i, k)),
               b_spec=pl.BlockSpec((tk, tn), lambda i, j, k: (k, j)),
               outs=[((m, n), dt, o_spec) for dt in dts], extra=extra, extra_specs=[o_spec] * len(extra),
               epilogue=epilogue, acc_shape=(tm, tn), deps=deps)


def _mm_nn_pieces(name, a, b, out_dtype=F32, tm=1024, tk=2048, epilogue=None, n_out=1):
    m, kd = a.shape
    npc, _, ns = b.shape
    tm, tk = _tile(m, tm), _tile(kd, tk)
    tn = _tile(ns, 1024)
    r = ns // tn
    dts = out_dtype if isinstance(out_dtype, (list, tuple)) else [out_dtype] * n_out
    o_spec = pl.BlockSpec((tm, tn), lambda i, j, k: (i, j))
    return _mm(name, a, b, dims=NN, grid=(m // tm, npc * r, kd // tk),
               a_spec=pl.BlockSpec((tm, tk), lambda i, j, k: (i, k)),
               b_spec=pl.BlockSpec((None, tk, tn), lambda i, j, k: (j // r, k, j % r)),
               outs=[((m, npc * ns), dt, o_spec) for dt in dts], epilogue=epilogue, acc_shape=(tm, tn))


def _mm_nt(name, a, b, out_dtype=F32, tm=1024, tn=1024, tk=2048, epilogue=None, extra=(), n_rows=None, deps=()):
    m, kd = a.shape
    n = n_rows or b.shape[0]
    tm, tn, tk = _tile(m, tm), _tile(n, tn), _tile(kd, tk)
    o_spec = pl.BlockSpec((tm, tn), lambda i, j, k: (i, j))
    return _mm(name, a, b, dims=NT, grid=(m // tm, n // tn, kd // tk),
               a_spec=pl.BlockSpec((tm, tk), lambda i, j, k: (i, k)),
               b_spec=pl.BlockSpec((tn, tk), lambda i, j, k: (j, k)),
               outs=[((m, n), out_dtype, o_spec)], extra=extra, extra_specs=[o_spec] * len(extra),
               epilogue=epilogue, acc_shape=(tm, tn), deps=deps)[0]


def _mm_nt_pieces(name, a, b, out_dtype=F32, tm=1024, tn=1024, deps=()):
    m = a.shape[0]
    npc, n, ks = b.shape
    tm, tn = _tile(m, tm), _tile(n, tn)
    tk = _tile(ks, 1024)
    r = ks // tk
    return _mm(name, a, b, dims=NT, grid=(m // tm, n // tn, npc * r),
               a_spec=pl.BlockSpec((tm, tk), lambda i, j, k: (i, k)),
               b_spec=pl.BlockSpec((None, tn, tk), lambda i, j, k: (k // r, j, k % r)),
               outs=[((m, n), out_dtype, pl.BlockSpec((tm, tn), lambda i, j, k: (i, j)))],
               acc_shape=(tm, tn), deps=deps)[0]


def _mm_tn(name, a, b, out_dtype=BF16, tm=1024, tn=1024, pieces=0, deps=()):
    t, m = a.shape
    n = b.shape[1]
    tm = _tile(m, tm)
    if pieces:
        ns = n // pieces
        tn = _tile(ns, tn)
        r = ns // tn
        out = ((pieces, m, ns), out_dtype, pl.BlockSpec((None, tm, tn), lambda i, j, k: (j // r, i, j % r)))
    else:
        tn = _tile(n, tn)
        out = ((m, n), out_dtype, pl.BlockSpec((tm, tn), lambda i, j, k: (i, j)))
    return _mm(name, a, b, dims=TN, grid=(m // tm, n // tn, 1),
               a_spec=pl.BlockSpec((t, tm), lambda i, j, k: (0, i)),
               b_spec=pl.BlockSpec((t, tn), lambda i, j, k: (0, j)),
               outs=[out], deps=deps)[0]


def _rms_f(x, g):
    r = lax.rsqrt(jnp.mean(x * x, axis=-1, keepdims=True) + EPS)
    xh = x * r
    return xh * g, xh, r


def _rms_b(dy, g, xh, r):
    dxh = dy * g
    dx = r * (dxh - xh * jnp.mean(dxh * xh, axis=-1, keepdims=True))
    dg = jnp.sum(dy * xh, axis=0, keepdims=True)
    return dx, dg


def _rowwise(name, fn, row_ins, vec_ins, row_outs, vec_outs, deps=()):
    t = row_ins[0].shape[0]
    tr = min(ROW_TILE, t)
    nr, nv, no = len(row_ins), len(vec_ins), len(row_outs)
    first_out = nr + nv + len(deps)

    def body(*refs):
        ri, vi = refs[:nr], refs[nr:nr + nv]
        ro, vo = refs[first_out:first_out + no], refs[first_out + no:]
        routs, vouts = fn(*[r[...] for r in ri], *[v[...] for v in vi])
        for o, val in zip(ro, routs):
            o[...] = val.astype(o.dtype)
        i = pl.program_id(0)
        for o, val in zip(vo, vouts):
            @pl.when(i == 0)
            def _():
                o[...] = val

            @pl.when(i > 0)
            def _():
                o[...] += val

    res = pl.pallas_call(
        body, name=name,
        out_shape=[jax.ShapeDtypeStruct((t, w), d) for w, d in row_outs]
        + [jax.ShapeDtypeStruct((1, w), F32) for w in vec_outs],
        grid=(t // tr,),
        in_specs=[pl.BlockSpec((tr, a.shape[1]), lambda i: (i, 0)) for a in row_ins]
        + [pl.BlockSpec((1, a.shape[1]), lambda i: (0, 0)) for a in vec_ins] + [ANY_SPEC] * len(deps),
        out_specs=[pl.BlockSpec((tr, w), lambda i: (i, 0)) for w, _ in row_outs]
        + [pl.BlockSpec((1, w), lambda i: (0, 0)) for w in vec_outs],
        compiler_params=_params("arbitrary"),
    )(*row_ins, *vec_ins, *deps)
    return list(res)


def _norm_fwd(x, g, name, deps=()):
    d = x.shape[1]
    return _rowwise(name, lambda x, g: ((_rms_f(x, g)[0],), ()), [x], [g], [(d, BF16)], [], deps=deps)[0]


def _norm_bwd(x, dxn, dres, g, name):
    d = x.shape[1]

    def fn(x, dxn, dres, g):
        _, xh, r = _rms_f(x, g)
        dx, dg = _rms_b(dxn, g, xh, r)
        return (dres + dx,), (dg,)

    return _rowwise(name, fn, [x, dxn, dres], [g], [(d, F32)], [d])


def _block_fwd(h_prev, y, g_y, g_n, name):
    d = h_prev.shape[1]

    def fn(h_prev, y, g_y, g_n):
        h = h_prev + _rms_f(y, g_y)[0]
        return (h, _rms_f(h, g_n)[0]), ()

    return _rowwise(name, fn, [h_prev, y], [g_y, g_n], [(d, F32), (d, BF16)], [])


def _block_bwd(h, y, d_up, d_n, g_y, g_n, name, deps=()):
    d = h.shape[1]

    def fn(h, y, d_up, d_n, g_y, g_n):
        _, hh, hr = _rms_f(h, g_n)
        dh_n, dg_n = _rms_b(d_n, g_n, hh, hr)
        dh = d_up + dh_n
        _, yh, yr = _rms_f(y, g_y)
        dy, dg_y = _rms_b(dh, g_y, yh, yr)
        return (dh, dy), (dg_y, dg_n)

    return _rowwise(name, fn, [h, y, d_up, d_n], [g_y, g_n], [(d, F32), (d, BF16)], [d, d], deps=deps)


def _loss_head(h2, z, e, target, g, name):
    d = h2.shape[1]

    def fn(h2, z, e, target, g):
        gate = jax.nn.sigmoid(z)
        y, xh, r = _rms_f(e * gate, g)
        diff = h2 + y - target
        loss = 0.5 * jnp.sum(jnp.mean(diff * diff, axis=-1, keepdims=True), axis=0, keepdims=True)
        dh3 = diff / d
        deg, dg = _rms_b(dh3, g, xh, r)
        dz = deg * e * gate * (1.0 - gate)
        de = deg * gate
        return (dh3, dz, de), (dg, jnp.broadcast_to(loss, (1, LANES)))

    return _rowwise(name, fn, [h2, z, e, target], [g], [(d, F32), (d, BF16), (d, BF16)], [d, LANES])


def _shift_down(x, s, rows):
    return jnp.where(rows >= s, pltpu.roll(x, s, axis=0), 0.0)


def _shift_up(x, s, rows):
    t = x.shape[0]
    return jnp.where(rows < t - s, pltpu.roll(x, t - s, axis=0), 0.0)


def _conv_pre(x, w, rows):
    pre = x * w[QK_CONV - 1:QK_CONV, :]
    for s in range(1, QK_CONV):
        pre = pre + _shift_down(x, s, rows) * w[QK_CONV - 1 - s:QK_CONV - s, :]
    return pre


def _conv_fwd(proj_a, w, start, name):
    t = proj_a.shape[0]
    width = w.shape[1]
    tc = _tile(width, 256)
    off = start // tc

    def body(x_ref, w_ref, y_ref):
        x = x_ref[...]
        rows = lax.broadcasted_iota(jnp.int32, x.shape, 0)
        pre = _conv_pre(x, w_ref[...], rows)
        y_ref[...] = pre * jax.nn.sigmoid(pre)

    return pl.pallas_call(
        body, name=name,
        out_shape=jax.ShapeDtypeStruct((t, width), F32),
        grid=(width // tc,),
        in_specs=[pl.BlockSpec((t, tc), lambda j: (0, off + j)), pl.BlockSpec((QK_CONV, tc), lambda j: (0, j))],
        out_specs=pl.BlockSpec((t, tc), lambda j: (0, j)),
        compiler_params=_params("parallel"),
    )(proj_a, w)


def _conv_bwd(proj_a, w, dy, start, name):
    t = proj_a.shape[0]
    width = w.shape[1]
    tc = _tile(width, 256)
    off = start // tc

    def body(x_ref, w_ref, dy_ref, dx_ref, dw_ref):
        x, w = x_ref[...], w_ref[...]
        rows = lax.broadcasted_iota(jnp.int32, x.shape, 0)
        pre = _conv_pre(x, w, rows)
        sig = jax.nn.sigmoid(pre)
        dpre = dy_ref[...] * (sig * (1.0 + pre * (1.0 - sig)))
        dx = dpre * w[QK_CONV - 1:QK_CONV, :]
        dws = [jnp.sum(dpre * x, axis=0, keepdims=True)]
        for s in range(1, QK_CONV):
            dx = dx + _shift_up(dpre, s, rows) * w[QK_CONV - 1 - s:QK_CONV - s, :]
            dws.append(jnp.sum(dpre * _shift_down(x, s, rows), axis=0, keepdims=True))
        dx_ref[...] = dx.astype(dx_ref.dtype)
        for s in range(QK_CONV):
            dw_ref[QK_CONV - 1 - s:QK_CONV - s, :] = dws[s]

    return pl.pallas_call(
        body, name=name,
        out_shape=[jax.ShapeDtypeStruct((t, width), BF16), jax.ShapeDtypeStruct((QK_CONV, width), F32)],
        grid=(width // tc,),
        in_specs=[pl.BlockSpec((t, tc), lambda j: (0, off + j)), pl.BlockSpec((QK_CONV, tc), lambda j: (0, j)),
                  pl.BlockSpec((t, tc), lambda j: (0, j))],
        out_specs=[pl.BlockSpec((t, tc), lambda j: (0, j)), pl.BlockSpec((QK_CONV, tc), lambda j: (0, j))],
        compiler_params=_params("parallel"),
    )(proj_a, w, dy)


def _gates_fwd(proj_b, bias, col_block, heads, name):
    t = proj_b.shape[0]

    def body(x_ref, b_ref, o_ref):
        pre = x_ref[...] + b_ref[...]
        col = lax.broadcasted_iota(jnp.int32, pre.shape, 1)
        log_f = jnp.minimum(pre, 0.0) - jnp.log(1.0 + jnp.exp(-jnp.abs(pre)))
        o_ref[...] = jnp.where(col < heads, pre, jnp.where(col < 2 * heads, log_f, 0.0))

    return pl.pallas_call(
        body, name=name,
        out_shape=jax.ShapeDtypeStruct((t, LANES), F32),
        grid=(1,),
        in_specs=[pl.BlockSpec((t, LANES), lambda i: (0, col_block)), pl.BlockSpec((1, LANES), lambda i: (0, 0))],
        out_specs=pl.BlockSpec((t, LANES), lambda i: (0, 0)),
        compiler_params=_params("arbitrary"),
    )(proj_b, bias)


def _gates_bwd(proj_b, bias, dgl, col_block, heads, name):
    t = proj_b.shape[0]

    def body(x_ref, b_ref, d_ref, dx_ref, db_ref):
        pre = x_ref[...] + b_ref[...]
        col = lax.broadcasted_iota(jnp.int32, pre.shape, 1)
        d = d_ref[...]
        dx = jnp.where(col < heads, d, jnp.where(col < 2 * heads, d * jax.nn.sigmoid(-pre), 0.0))
        dx_ref[...] = dx.astype(dx_ref.dtype)
        db_ref[...] = jnp.sum(dx, axis=0, keepdims=True)

    return pl.pallas_call(
        body, name=name,
        out_shape=[jax.ShapeDtypeStruct((t, LANES), BF16), jax.ShapeDtypeStruct((1, LANES), F32)],
        grid=(1,),
        in_specs=[pl.BlockSpec((t, LANES), lambda i: (0, col_block)), pl.BlockSpec((1, LANES), lambda i: (0, 0)),
                  pl.BlockSpec((t, LANES), lambda i: (0, 0))],
        out_specs=[pl.BlockSpec((t, LANES), lambda i: (0, 0)), pl.BlockSpec((1, LANES), lambda i: (0, 0))],
        compiler_params=_params("arbitrary"),
    )(proj_b, bias, dgl)


def _dot(a, b, dims):
    return lax.dot_general(a.astype(BF16), b.astype(BF16), (dims, ((), ())), preferred_element_type=F32)


def _dot_split(ones, x):
    hi = x.astype(BF16)
    rest = x - hi.astype(F32)
    mid = rest.astype(BF16)
    lo = (rest - mid.astype(F32)).astype(BF16)
    dot = lambda part: lax.dot_general(ones, part, (NN, ((), ())), preferred_element_type=F32)
    return dot(hi) + dot(mid) + dot(lo)


def _chunk_gates(li, lf):
    n = li.shape[0]
    r = lax.broadcasted_iota(jnp.int32, (n, n), 0)
    c = lax.broadcasted_iota(jnp.int32, (n, n), 1)
    tri = (c <= r).astype(BF16)
    b_cols = _dot_split(tri, jnp.broadcast_to(lf, (n, n)))
    b_rows = b_cols.T
    li_rows = jnp.broadcast_to(li, (n, n)).T
    b = b_cols[:, 0:1]
    b_tot = b_cols[n - 1:n, 0:1]
    return b_cols, b_rows, li_rows, b, b_tot, c <= r


def _chunk_state(k, v, li, b, b_tot, ct, nrow, m_prev):
    a = b_tot - b + li
    m_new = jnp.maximum(b_tot + m_prev, jnp.max(a, axis=0, keepdims=True))
    dec = jnp.exp(b_tot + m_prev - m_new)
    w = jnp.exp(a - m_new)
    wv = w * v
    ct_new = dec * ct + _dot(k, wv, TN)
    n_new = dec * nrow + jnp.sum(w * k, axis=0, keepdims=True)
    return ct_new, n_new, m_new, dec, w, wv


def _chunk_out(q, k, v, gates, ct, nrow, m_prev):
    b_cols, b_rows, li_rows, b, _, causal = gates
    dlog = jnp.where(causal, b_cols - b_rows + li_rows, -jnp.inf)
    inter = b + m_prev
    mt = jnp.maximum(inter, jnp.max(dlog, axis=-1, keepdims=True))
    dw = jnp.exp(dlog - mt)
    iw = jnp.exp(inter - mt)
    a = _dot(q, k, NT)
    sm = a * dw
    qc = _dot(q, ct, NN)
    qn = jnp.sum(q * nrow, axis=-1, keepdims=True)
    num = iw * qc + _dot(sm, v, NN)
    den = iw * qn + jnp.sum(sm, axis=-1, keepdims=True)
    floor = jnp.exp(-mt)
    dn = jnp.maximum(jnp.abs(den), floor)
    h = num / dn
    return h, (dw, iw, a, sm, qc, qn, den, floor, dn)


def _head_out(h, o, g):
    cell, hh, r = _rms_f(h, g)
    sig = jax.nn.sigmoid(o)
    return sig * cell, (cell, hh, r, sig)


def _head_gates(gl_tile, head, heads):
    lane = lax.broadcasted_iota(jnp.int32, gl_tile.shape, 1)
    li = jnp.sum(jnp.where(lane == head, gl_tile, 0.0), axis=-1, keepdims=True)
    lf = jnp.sum(jnp.where(lane == heads + head, gl_tile, 0.0), axis=-1, keepdims=True)
    return li, lf


HEADS_PER_STEP = 2


def _mlstm_fwd(qc, kc, proj_a, gl, g_mlstm, heads, d_model, name):
    t = qc.shape[0]
    dh = g_mlstm.shape[1] // heads
    nc = t // CHUNK
    scale = dh ** -0.5
    hp = HEADS_PER_STEP

    def body(q_ref, k_ref, v_ref, o_ref, gl_ref, g_ref, out_ref, ct_ref, n_ref, m_ref):
        step = pl.program_id(0)
        ct_ref[...] = jnp.zeros_like(ct_ref)
        n_ref[...] = jnp.zeros_like(n_ref)
        m_ref[...] = jnp.zeros_like(m_ref)

        def chunk(ci, carry):
            sl = pl.ds(pl.multiple_of(ci * CHUNK, CHUNK), CHUNK)
            gl_tile = gl_ref[sl, :]
            for s in range(hp):
                cs = slice(s * dh, (s + 1) * dh)
                q, k, v = q_ref[sl, cs], k_ref[sl, cs] * scale, v_ref[sl, cs]
                li, lf = _head_gates(gl_tile, step * hp + s, heads)
                gates = _chunk_gates(li, lf)
                ct, nrow, m_prev = ct_ref[s], n_ref[s, 0:1, :], m_ref[s, 0:1, 0:1]
                h, _ = _chunk_out(q, k, v, gates, ct, nrow, m_prev)
                y, _ = _head_out(h, o_ref[sl, cs], g_ref[:, cs])
                out_ref[sl, cs] = y.astype(out_ref.dtype)
                ct_new, n_new, m_new, _, _, _ = _chunk_state(k, v, li, gates[3], gates[4], ct, nrow, m_prev)
                ct_ref[s] = ct_new
                n_ref[s, 0:1, :] = n_new
                m_ref[s, 0:1, 0:1] = m_new
            return carry

        lax.fori_loop(0, nc, chunk, 0, unroll=2)

    col = lambda off: pl.BlockSpec((t, hp * dh), lambda h: (0, off // hp + h))
    return pl.pallas_call(
        body, name=name,
        out_shape=jax.ShapeDtypeStruct((t, d_model), BF16),
        grid=(heads // hp,),
        in_specs=[col(0), col(0), col(2 * heads), col(3 * heads), pl.BlockSpec((t, LANES), lambda h: (0, 0)),
                  pl.BlockSpec((1, hp * dh), lambda h: (0, h))],
        out_specs=pl.BlockSpec((t, hp * dh), lambda h: (0, h)),
        scratch_shapes=[pltpu.VMEM((hp, dh, dh), F32), pltpu.VMEM((hp, 8, dh), F32), pltpu.VMEM((hp, 8, LANES), F32)],
        compiler_params=_params("arbitrary"),
    )(qc, kc, proj_a, proj_a, gl, g_mlstm)


def _mlstm_bwd(qc, kc, proj_a, gl, g_mlstm, dhcat, heads, name, dep):
    t = qc.shape[0]
    wm = g_mlstm.shape[1]
    dh = wm // heads
    nc = t // CHUNK
    scale = dh ** -0.5
    hp = HEADS_PER_STEP

    def body(q_ref, k_ref, v_ref, o_ref, gl_ref, g_ref, dy_ref, dep_ref,
             dq_ref, dk_ref, dv_ref, do_ref, dgl_ref, dg_ref,
             ct_all, n_all, m_all, dct_ref, dn_ref):
        del dep_ref
        grid_step = pl.program_id(0)

        @pl.when(grid_step == 0)
        def _():
            dgl_ref[...] = jnp.zeros_like(dgl_ref)

        def load(ci, s):
            sl = pl.ds(pl.multiple_of(ci * CHUNK, CHUNK), CHUNK)
            cs = slice(s * dh, (s + 1) * dh)
            li, lf = _head_gates(gl_ref[sl, :], grid_step * hp + s, heads)
            return sl, cs, q_ref[sl, cs], k_ref[sl, cs] * scale, v_ref[sl, cs], li, lf

        def fwd(ci, carry):
            new = []
            for s in range(hp):
                ct, nrow, m_prev = carry[s]
                ct_all[s, ci] = ct
                n_all[s, ci, 0:1, :] = nrow
                m_all[s, ci, 0:1, 0:1] = m_prev
                _, _, _, k, v, li, lf = load(ci, s)
                gates = _chunk_gates(li, lf)
                ct_new, n_new, m_new, _, _, _ = _chunk_state(k, v, li, gates[3], gates[4], ct, nrow, m_prev)
                new.append((ct_new, n_new, m_new))
            return tuple(new)

        zero = (jnp.zeros((dh, dh), F32), jnp.zeros((1, dh), F32), jnp.zeros((1, 1), F32))
        lax.fori_loop(0, nc, fwd, (zero,) * hp)

        dct_ref[...] = jnp.zeros_like(dct_ref)
        dn_ref[...] = jnp.zeros_like(dn_ref)
        dg_ref[...] = jnp.zeros_like(dg_ref)
        r_idx = lax.broadcasted_iota(jnp.int32, (CHUNK, 1), 0)
        rr = lax.broadcasted_iota(jnp.int32, (CHUNK, CHUNK), 0)
        cc = lax.broadcasted_iota(jnp.int32, (CHUNK, CHUNK), 1)
        tri_t = (cc >= rr).astype(BF16)

        def bwd(step, carry):
            for s in range(hp):
                bwd_head(nc - 1 - step, s)
            return carry

        def bwd_head(ci, s):
            sl, cs, q, k, v, li, lf = load(ci, s)
            head = grid_step * hp + s
            g = g_ref[:, cs]
            gates = _chunk_gates(li, lf)
            b, b_tot, causal = gates[3], gates[4], gates[5]
            ct, nrow, m_prev = ct_all[s, ci], n_all[s, ci, 0:1, :], m_all[s, ci, 0:1, 0:1]
            h, (dw, iw, a, sm, qc, qn, den, floor, dn) = _chunk_out(q, k, v, gates, ct, nrow, m_prev)
            o = o_ref[sl, cs]
            _, (cell, hh, r, sig) = _head_out(h, o, g)
            dy = dy_ref[sl, cs]
            dcell = dy * sig
            do_ref[sl, cs] = (dy * cell * sig * (1.0 - sig)).astype(do_ref.dtype)
            dh_, dg = _rms_b(dcell, g, hh, r)
            dg_ref[:, cs] += dg
            dnum = dh_ / dn
            ddn = -jnp.sum(dh_ * h, axis=-1, keepdims=True) / dn
            dden = jnp.where(jnp.abs(den) >= floor, ddn * jnp.sign(den), 0.0)
            idn = iw * dnum
            idd = iw * dden
            dq = _dot(idn, ct, NT) + idd * nrow
            dct = _dot(q, idn, TN)
            dnr = jnp.sum(q * idd, axis=0, keepdims=True)
            diw = jnp.sum(dnum * qc, axis=-1, keepdims=True) + dden * qn
            dsm = _dot(dnum, v, NT) + dden
            dv = _dot(sm, dnum, TN)
            da_ = dsm * dw
            e = jnp.where(causal, dsm * a * dw, 0.0)
            dq = dq + _dot(da_, k, NN)
            dk = _dot(da_, q, TN)
            e_cols = jnp.sum(e.T, axis=-1, keepdims=True)
            db = jnp.sum(e, axis=-1, keepdims=True) - e_cols + diw * iw
            dli = e_cols
            _, _, _, dec, w, wv = _chunk_state(k, v, li, b, b_tot, ct, nrow, m_prev)
            dct_new, dn_new = dct_ref[s], dn_ref[s, 0:1, :]
            ddec = jnp.sum(jnp.sum(dct_new * ct, axis=-1, keepdims=True), axis=0, keepdims=True) \
                + jnp.sum(dn_new * nrow, axis=-1, keepdims=True)
            dk = dk + _dot(wv, dct_new, NT) + w * dn_new
            dwv = _dot(k, dct_new, NN)
            dv = dv + w * dwv
            dwt = jnp.sum(dwv * v, axis=-1, keepdims=True) + jnp.sum(k * dn_new, axis=-1, keepdims=True)
            da = dwt * w
            dbtot = jnp.sum(da, axis=0, keepdims=True) + ddec * dec
            db = db - da + jnp.where(r_idx == CHUNK - 1, dbtot, 0.0)
            dli = dli + da
            dlf = _dot_split(tri_t, jnp.broadcast_to(db, (CHUNK, CHUNK)))[:, 0:1]
            dct_ref[s] = dec * dct_new + dct
            dn_ref[s, 0:1, :] = dec * dn_new + dnr
            dq_ref[sl, cs] = dq
            dk_ref[sl, cs] = dk * scale
            dv_ref[sl, cs] = dv.astype(dv_ref.dtype)
            lane = lax.broadcasted_iota(jnp.int32, (CHUNK, LANES), 1)
            dgl_ref[sl, :] += jnp.where(lane == head, dli, 0.0) + jnp.where(lane == heads + head, dlf, 0.0)

        lax.fori_loop(0, nc, bwd, 0, unroll=2)

    col = lambda off: pl.BlockSpec((t, hp * dh), lambda h: (0, off // hp + h))
    whole = pl.BlockSpec((t, LANES), lambda h: (0, 0))
    vec = pl.BlockSpec((1, hp * dh), lambda h: (0, h))
    return pl.pallas_call(
        body, name=name,
        out_shape=[jax.ShapeDtypeStruct((t, wm), F32)] * 2 + [jax.ShapeDtypeStruct((t, wm), BF16)] * 2
        + [jax.ShapeDtypeStruct((t, LANES), F32), jax.ShapeDtypeStruct((1, wm), F32)],
        grid=(heads // hp,),
        in_specs=[col(0), col(0), col(2 * heads), col(3 * heads), whole, vec, col(0), ANY_SPEC],
        out_specs=[col(0), col(0), col(0), col(0), whole, vec],
        scratch_shapes=[pltpu.VMEM((hp, nc, dh, dh), F32), pltpu.VMEM((hp, nc, 8, dh), F32),
                        pltpu.VMEM((hp, nc, 8, LANES), F32), pltpu.VMEM((hp, dh, dh), F32), pltpu.VMEM((hp, 8, dh), F32)],
        compiler_params=_params("arbitrary"),
    )(qc, kc, proj_a, proj_a, gl, g_mlstm, dhcat, dep)


def _pool_window(gi):
    return jnp.where(gi == 0, 2.0, jnp.where(gi == 1, 4.0, jnp.where(gi == 2, 8.0, 16.0))).astype(F32)


def _pool_select(gi, levels):
    return jnp.where(gi == 0, levels[0], jnp.where(gi == 1, levels[1], jnp.where(gi == 2, levels[2], levels[3])))


def _pooled(u, gi, rows):
    s, levels = u, []
    for lvl in range(POOL_GROUPS):
        s = s + _shift_down(s, 1 << lvl, rows)
        levels.append(s)
    cnt = jnp.minimum((rows + 1).astype(F32), _pool_window(gi))
    return _pool_select(gi, levels) / cnt - u, cnt


def _pool_fwd(proj_b, w_pool, scale, hcat, name):
    t, d_model = hcat.shape
    g, cg, _ = w_pool.shape
    off = (d_model - g * cg) // cg

    def body(u_ref, w_ref, s_ref, hcat_ref, out_ref):
        del hcat_ref
        gi = pl.program_id(0)
        u = u_ref[...]
        rows = lax.broadcasted_iota(jnp.int32, u.shape, 0)
        pooled, _ = _pooled(u, gi, rows)
        out_ref[...] = (_dot(pooled, w_ref[...], NN) * s_ref[...]).astype(out_ref.dtype)

    return pl.pallas_call(
        body, name=name,
        out_shape=jax.ShapeDtypeStruct((t, d_model), BF16),
        grid=(g,),
        in_specs=[pl.BlockSpec((t, cg), lambda gi: (0, gi)), pl.BlockSpec((None, cg, cg), lambda gi: (gi, 0, 0)),
                  pl.BlockSpec((1, cg), lambda gi: (0, gi)), HBM_SPEC],
        out_specs=pl.BlockSpec((t, cg), lambda gi: (0, off + gi)),
        input_output_aliases={3: 0},
        compiler_params=_params("arbitrary"),
    )(proj_b, w_pool, scale, hcat)


def _pool_bwd(proj_b, w_pool, scale, dhcat, name):
    t, d_model = dhcat.shape
    g, cg, _ = w_pool.shape
    off = (d_model - g * cg) // cg

    def body(u_ref, w_ref, s_ref, dy_ref, du_ref, dw_ref, ds_ref):
        gi = pl.program_id(0)
        u, w, dy = u_ref[...], w_ref[...], dy_ref[...]
        rows = lax.broadcasted_iota(jnp.int32, u.shape, 0)
        pooled, cnt = _pooled(u, gi, rows)
        mixed = _dot(pooled, w, NN)
        ds_ref[...] = jnp.sum(dy * mixed, axis=0, keepdims=True)
        dmixed = dy * s_ref[...]
        dw_ref[...] = _dot(pooled, dmixed, TN).astype(dw_ref.dtype)
        dpooled = _dot(dmixed, w, NT)
        s, levels = dpooled / cnt, []
        for lvl in range(POOL_GROUPS):
            s = s + _shift_up(s, 1 << lvl, rows)
            levels.append(s)
        du_ref[...] = (_pool_select(gi, levels) - dpooled).astype(du_ref.dtype)

    return pl.pallas_call(
        body, name=name,
        out_shape=[jax.ShapeDtypeStruct((t, g * cg), BF16), jax.ShapeDtypeStruct((g, cg, cg), BF16),
                   jax.ShapeDtypeStruct((1, g * cg), F32)],
        grid=(g,),
        in_specs=[pl.BlockSpec((t, cg), lambda gi: (0, gi)), pl.BlockSpec((None, cg, cg), lambda gi: (gi, 0, 0)),
                  pl.BlockSpec((1, cg), lambda gi: (0, gi)), pl.BlockSpec((t, cg), lambda gi: (0, off + gi))],
        out_specs=[pl.BlockSpec((t, cg), lambda gi: (0, gi)), pl.BlockSpec((None, cg, cg), lambda gi: (gi, 0, 0)),
                   pl.BlockSpec((1, cg), lambda gi: (0, gi))],
        compiler_params=_params("arbitrary"),
    )(proj_b, w_pool, scale, dhcat)


def _pad_cols(a, width):
    return jnp.pad(a, ((0, 0), (0, width - a.shape[1])))


def kernel(x, p, w_in, b_gates, w_qk_conv, g_mlstm, w_pool, pool_scale, w_out, g_mix_pre, g_mix_post, w_ff1, w_ff2, g_ff_pre, g_ff_post, w_ple_proj, w_ple_gate, g_ple_gate, g_ple_post, loss_target, m_w_in, m_b_gates, m_w_qk_conv, m_g_mlstm, m_w_pool, m_pool_scale, m_w_out, m_g_mix_pre, m_g_mix_post, m_w_ff1, m_w_ff2, m_g_ff_pre, m_g_ff_post, m_w_ple_proj, m_w_ple_gate, m_g_ple_gate, m_g_ple_post, v_w_in, v_b_gates, v_w_qk_conv, v_g_mlstm, v_w_pool, v_pool_scale, v_w_out, v_g_mix_pre, v_g_mix_post, v_w_ff1, v_w_ff2, v_g_ff_pre, v_g_ff_post, v_w_ple_proj, v_w_ple_gate, v_g_ple_gate, v_g_ple_post):
    weights = dict(w_in=w_in, b_gates=b_gates, w_qk_conv=w_qk_conv, g_mlstm=g_mlstm, w_pool=w_pool,
                   pool_scale=pool_scale, w_out=w_out, g_mix_pre=g_mix_pre, g_mix_post=g_mix_post, w_ff1=w_ff1,
                   w_ff2=w_ff2, g_ff_pre=g_ff_pre, g_ff_post=g_ff_post, w_ple_proj=w_ple_proj,
                   w_ple_gate=w_ple_gate, g_ple_gate=g_ple_gate, g_ple_post=g_ple_post)
    mom1 = dict(w_in=m_w_in, b_gates=m_b_gates, w_qk_conv=m_w_qk_conv, g_mlstm=m_g_mlstm, w_pool=m_w_pool,
                pool_scale=m_pool_scale, w_out=m_w_out, g_mix_pre=m_g_mix_pre, g_mix_post=m_g_mix_post,
                w_ff1=m_w_ff1, w_ff2=m_w_ff2, g_ff_pre=m_g_ff_pre, g_ff_post=m_g_ff_post,
                w_ple_proj=m_w_ple_proj, w_ple_gate=m_w_ple_gate, g_ple_gate=m_g_ple_gate, g_ple_post=m_g_ple_post)
    mom2 = dict(w_in=v_w_in, b_gates=v_b_gates, w_qk_conv=v_w_qk_conv, g_mlstm=v_g_mlstm, w_pool=v_w_pool,
                pool_scale=v_pool_scale, w_out=v_w_out, g_mix_pre=v_g_mix_pre, g_mix_post=v_g_mix_post,
                w_ff1=v_w_ff1, w_ff2=v_w_ff2, g_ff_pre=v_g_ff_pre, g_ff_post=v_g_ff_post,
                w_ple_proj=v_w_ple_proj, w_ple_gate=v_w_ple_gate, g_ple_gate=v_g_ple_gate, g_ple_post=v_g_ple_post)
    order = list(weights)

    t, d = x.shape[1], x.shape[2]
    heads = MLSTM_HEADS
    wm = g_mlstm.shape[1]
    pw = pool_scale.shape[1]
    cg = pw // POOL_GROUPS
    in_cols = 4 * wm + 2 * heads + pw
    in_shard = w_in.shape[2]
    xs, ps, tgt = x[0], p[0, 0], loss_target[0]

    mx, my, mc = lax.axis_index("x"), lax.axis_index("y"), lax.axis_index("c")
    core = jnp.reshape(mc, (1,)).astype(jnp.int32)
    chip = jnp.reshape(2 * mx + my, (1,)).astype(jnp.int32)

    groups = [[w_in[0].T.astype(BF16), w_qk_conv[0], w_pool[0].reshape(POOL_GROUPS * w_pool.shape[2], cg).astype(BF16)],
              [w_out[0].astype(BF16)], [w_ff1[0].astype(BF16)], [w_ff2[0].astype(BF16)],
              [w_ple_gate[0].astype(BF16), w_ple_proj[0].astype(BF16)]]
    started, token = _gather_start(groups, "gather_start")
    bias = _pad_cols(b_gates, LANES)

    xn = _norm_fwd(xs, g_mix_pre, "norm_mix_pre", deps=[token])
    g_in, g_conv, g_pool = _forward_pair(_gather_wait(started[0], xn, "gather_wait_in"), "gather_pair_in")
    in_t_a = g_in.reshape(in_cols, d)
    in_t_b = jnp.concatenate([in_t_a[4 * wm + 2 * heads:], in_t_a[4 * wm:4 * wm + 2 * heads],
                              jnp.zeros((LANES - 2 * heads, d), BF16)], axis=0)
    conv_w = g_conv.transpose(1, 0, 2).reshape(QK_CONV, 2 * wm)
    pool_w = g_pool.reshape(N_DEV, POOL_GROUPS, cg // N_DEV, cg).transpose(1, 0, 2, 3).reshape(POOL_GROUPS, cg, cg)
    proj_a = _mm_nt("proj_qkvo", xn, in_t_a, n_rows=4 * wm)
    proj_b = _mm_nt("proj_pool_gates", xn, in_t_b)
    qc = _conv_fwd(proj_a, conv_w[:, :wm], 0, "q_conv")
    kc = _conv_fwd(proj_a, conv_w[:, wm:], wm, "k_conv")
    gl = _gates_fwd(proj_b, bias, pw // LANES, heads, "gates")
    (g_out,) = _forward_pair(_gather_wait(started[1], gl, "gather_wait_out"), "gather_pair_out")
    hcat = _mlstm_fwd(qc, kc, proj_a, gl, g_mlstm, heads, d, "mlstm")
    hcat = _pool_fwd(proj_b, pool_w, pool_scale, hcat, "pool")
    out_w = g_out.reshape(d, d)
    mix = _mm_nn("mix_out", hcat, out_w)[0]
    h1, hn = _block_fwd(xs, mix, g_mix_post, g_ff_pre, "residual_mix")
    (g_ff1,) = _forward_pair(_gather_wait(started[2], hn, "gather_wait_ff1"), "gather_pair_ff1")
    relu_a, act = _mm_nn_pieces("ff1", hn, g_ff1, out_dtype=[BF16, BF16],
                                epilogue=lambda acc: (jnp.maximum(acc, 0.0), jnp.square(jnp.maximum(acc, 0.0))))
    (g_ff2,) = _forward_pair(_gather_wait(started[3], act, "gather_wait_ff2"), "gather_pair_ff2")
    ff2_w = g_ff2.reshape(-1, d)
    ff = _mm_nn("ff2", act, ff2_w)[0]
    h2, hg = _block_fwd(h1, ff, g_ff_post, g_ple_gate, "residual_ff")
    g_pgate, g_pproj = _forward_pair(_gather_wait(started[4], hg, "gather_wait_ple"), "gather_pair_ple")
    pgate_w = g_pgate.reshape(d, d)
    z = _mm_nn("ple_gate", hg, pgate_w)[0]
    e = _mm_nn_pieces("ple_proj", ps, g_pproj)[0]
    dh3, dz, de, dg_ple_post, loss_part = _loss_head(h2, z, e, tgt, g_ple_post, "loss_head")
    loss = lax.psum(loss_part[0, 0], MESH_AXES)

    out = {}

    def view(k, a):
        return a[0].T if k == "w_in" else a.reshape(-1, a.shape[-1])

    def pair_begin(pieces, tag):
        keys = list(pieces)
        return keys, _pair_start([pieces[k] for k in keys], "reduce_pair_start_" + tag)

    def chip_begin(group, after, tag):
        keys, begun = group
        mine, got = _pair_wait(begun, after, "reduce_pair_wait_" + tag)
        sums = [_pair_sum(p_, g, core, "pair_sum_" + k) for k, p_, g in zip(keys, mine, got)]
        return keys, _chip_start(sums, "reduce_start_" + tag)

    def reduce_start(pieces, tag):
        keys = list(pieces)
        got = _pair_exchange([pieces[k] for k in keys], "reduce_pair_" + tag)
        sums = [_pair_sum(pieces[k], g, core, "pair_sum_" + k) for k, g in zip(keys, got)]
        return keys, _chip_start(sums, "reduce_start_" + tag)

    def reduce_finish(group, after, tag):
        keys, begun = group
        sums, got = _chip_wait(begun, after, "reduce_wait_" + tag)
        for k, s, g in zip(keys, sums, got):
            res = _adamw_shard(view(k, weights[k]), view(k, mom1[k]), view(k, mom2[k]), s, g, chip, "adamw_" + k)
            out[k] = [r.T[None] if k == "w_in" else r.reshape(weights[k].shape) for r in res]

    token_of = lambda group: group[1]["token"]
    dhg = _mm_nt("d_ple_gate_in", dz, pgate_w)
    gw_pgate = _mm_tn("gw_ple_gate", hg, dz)
    gw_pproj = _mm_tn("gw_ple_proj", ps, de, pieces=N_DEV)
    pair_ple = pair_begin({"w_ple_gate": gw_pgate.reshape(N_DEV, d // N_DEV, d), "w_ple_proj": gw_pproj}, "ple")
    dh2, dff, dg_ff_post, dg_ple_gate = _block_bwd(h2, ff, dh3, dhg, g_ff_post, g_ple_gate, "d_residual_ff",
                                                   deps=[token_of(pair_ple)])
    red_ple = chip_begin(pair_ple, dff, "ple")
    da = _mm_nt("d_ff2_in", dff, ff2_w, out_dtype=BF16, extra=[relu_a],
                epilogue=lambda acc, r: (acc * (2.0 * r.astype(F32)),), deps=[token_of(red_ple)])
    gw_ff2 = _mm_tn("gw_ff2", act, dff)
    pair_ff2 = pair_begin({"w_ff2": gw_ff2.reshape(N_DEV, -1, d)}, "ff2")
    dhn = _mm_nt_pieces("d_ff1_in", da, g_ff1, deps=[token_of(pair_ff2)])
    red_ff2 = chip_begin(pair_ff2, dhn, "ff2")
    reduce_finish(red_ple, token_of(red_ff2), "ple")
    gw_ff1 = _mm_tn("gw_ff1", hn, da, pieces=N_DEV, deps=[token_of(red_ff2)])
    pair_ff1 = pair_begin({"w_ff1": gw_ff1}, "ff1")
    dh1, dmix, dg_mix_post, dg_ff_pre = _block_bwd(h1, mix, dh2, dhn, g_mix_post, g_ff_pre, "d_residual_mix",
                                                   deps=[token_of(pair_ff1)])
    dhcat = _mm_nt("d_mix_in", dmix, out_w)
    red_ff1 = chip_begin(pair_ff1, dhcat, "ff1")
    reduce_finish(red_ff2, token_of(red_ff1), "ff2")
    gw_out = _mm_tn("gw_out", hcat, dmix, deps=[token_of(red_ff1)])
    du, gw_pool, dg_pool_scale = _pool_bwd(proj_b, pool_w, pool_scale, dhcat, "d_pool")
    pair_out = pair_begin({
        "w_out": gw_out.reshape(N_DEV, d // N_DEV, d),
        "w_pool": gw_pool.reshape(POOL_GROUPS, N_DEV, cg // N_DEV, cg).transpose(1, 0, 2, 3).reshape(N_DEV, -1, cg)},
        "out")
    dq, dk, dv, do, dgl, dg_mlstm = _mlstm_bwd(qc, kc, proj_a, gl, g_mlstm, dhcat, heads, "d_mlstm",
                                               token_of(pair_out))
    red_out = chip_begin(pair_out, dq, "out")
    reduce_finish(red_ff1, token_of(red_out), "ff1")
    dgates, dbias = _gates_bwd(proj_b, bias, dgl, pw // LANES, heads, "d_gates")
    dq_pre, gw_conv_q = _conv_bwd(proj_a, conv_w[:, :wm], dq, 0, "d_q_conv")
    dk_pre, gw_conv_k = _conv_bwd(proj_a, conv_w[:, wm:], dk, wm, "d_k_conv")
    gw_conv = jnp.concatenate([gw_conv_q, gw_conv_k], axis=1)
    dproj_a = jnp.concatenate([dq_pre, dk_pre, dv, do], axis=1)
    dproj_b = jnp.concatenate([du, dgates], axis=1)
    gw_a = _mm_tn("gw_in_qkvo", dproj_a, xn)
    gw_b = _mm_tn("gw_in_pool_gates", dproj_b, xn)
    gw_in_t = jnp.concatenate([gw_a, gw_b[pw:pw + 2 * heads], gw_b[:pw]], axis=0)
    red_in = reduce_start({
        "w_in": gw_in_t.reshape(N_DEV, in_shard, d),
        "w_qk_conv": gw_conv.reshape(QK_CONV, N_DEV, -1).transpose(1, 0, 2).astype(BF16)}, "in")
    reduce_finish(red_out, red_in[1]["token"], "out")
    dxn_a = _mm_nn("d_proj_qkvo_in", dproj_a, in_t_a, deps=[red_in[1]["token"]])[0]
    dxn = _mm_nn("d_proj_in", dproj_b, in_t_b, extra=[dxn_a], epilogue=lambda acc, prev: (acc + prev,))[0]
    grad_x, dg_mix_pre = _norm_bwd(xs, dxn, dh1, g_mix_pre, "d_norm_mix_pre")

    small = ["b_gates", "g_mlstm", "pool_scale", "g_mix_pre", "g_mix_post", "g_ff_pre", "g_ff_post", "g_ple_gate",
             "g_ple_post"]
    small_grads = dict(b_gates=dbias[:, :2 * heads], g_mlstm=dg_mlstm, pool_scale=dg_pool_scale, g_mix_pre=dg_mix_pre,
                       g_mix_post=dg_mix_post, g_ff_pre=dg_ff_pre, g_ff_post=dg_ff_post, g_ple_gate=dg_ple_gate,
                       g_ple_post=dg_ple_post)
    pack = lambda src: jnp.concatenate([_pad_cols(src[k], d) for k in small], axis=0)
    small_parts = _all_gather([_pad_rows16(pack(small_grads))], "gather_small_grads")[0]
    res = _adamw_replicated(_pad_rows16(pack(weights)), _pad_rows16(pack(mom1)), _pad_rows16(pack(mom2)), small_parts,
                            "adamw_replicated")
    for i, k in enumerate(small):
        width = weights[k].shape[1]
        out[k] = [r[i:i + 1, :width] for r in res]
    reduce_finish(red_in, res[0], "in")

    return (loss, grad_x[None], *[out[k][0] for k in order], *[out[k][1] for k in order],
            *[out[k][2] for k in order], *[out[k][3] for k in order])


def _pad_rows16(a):
    return jnp.pad(a, ((0, 16 - a.shape[0]), (0, 0)))
```

```python
import functools

import jax
import jax.numpy as jnp
from jax import lax
from jax.experimental import pallas as pl
from jax.experimental.pallas import tpu as pltpu

F32 = jnp.float32
BF16 = jnp.bfloat16
EPS = 1e-6
N_DEV = 8
N_CHIP = 4
LANES = 128
VMEM_LIMIT = 56 * 1024 * 1024
MLSTM_HEADS = 8
POOL_GROUPS = 4
QK_CONV = 4
CHUNK = 128
ROW_TILE = 256
ADAM_LR, ADAM_B1, ADAM_B2, ADAM_EPS, ADAM_WD, ADAM_STEP = 0.001, 0.9, 0.999, 1e-08, 0.01, 10
MESH_AXES = ("x", "y", "c")
MESH = pl.DeviceIdType.MESH
HBM_SPEC = pl.BlockSpec(memory_space=pltpu.HBM)
SEM_SPEC = pl.BlockSpec(memory_space=pltpu.SEMAPHORE)
DATAFLOW = pltpu.SideEffectType.DATAFLOW_SIDE_EFFECTING


def _params(*sem):
    if sem:
        return pltpu.CompilerParams(dimension_semantics=sem, vmem_limit_bytes=VMEM_LIMIT)
    return pltpu.CompilerParams(vmem_limit_bytes=VMEM_LIMIT)


def _tile(n, pref):
    if n <= pref:
        return n
    t = (pref // LANES) * LANES
    while t >= LANES:
        if n % t == 0:
            return t
        t -= LANES
    return n


def _all_gather(arrs, name):
    n = len(arrs)

    def body(*refs):
        ins, outs = refs[:n], refs[n:2 * n]
        send, recv, loc = refs[2 * n:]
        x, y, c = lax.axis_index("x"), lax.axis_index("y"), lax.axis_index("c")
        me = (x, y, c)
        sibling = (x, y, 1 - c)
        chips = [(1 - x, y), (x, 1 - y), (1 - x, 1 - y)]

        def row(dev):
            return 4 * dev[0] + 2 * dev[1] + dev[2]

        def copy(w, k, block_of, to, src=None):
            dst = outs[w].at[row(block_of)]
            return pltpu.make_async_remote_copy(
                src_ref=dst if src is None else src, dst_ref=dst,
                send_sem=send.at[7 * w + k], recv_sem=recv.at[7 * w + k],
                device_id=to, device_id_type=MESH)

        started = []
        local = []
        for w in range(n):
            cp = pltpu.make_async_copy(ins[w], outs[w].at[row(me)], loc.at[w])
            cp.start()
            local.append(cp)
            first = [copy(w, 0, me, sibling, src=ins[w])]
            first += [copy(w, 1 + j, me, (*chip, c), src=ins[w]) for j, chip in enumerate(chips)]
            for cp in first:
                cp.start()
            started += first
        for j, chip in enumerate(chips):
            for w in range(n):
                copy(w, 1 + j, (*chip, c), me).wait_recv()
                fwd = copy(w, 4 + j, (*chip, c), sibling)
                fwd.start()
                started.append(fwd)
        for w in range(n):
            copy(w, 0, sibling, me).wait_recv()
            for j, chip in enumerate(chips):
                copy(w, 4 + j, (*chip, 1 - c), me).wait_recv()
        for cp in started:
            cp.wait_send()
        for cp in local:
            cp.wait()

    outs = pl.pallas_call(
        body, name=name,
        out_shape=[jax.ShapeDtypeStruct((N_DEV,) + a.shape, a.dtype) for a in arrs],
        in_specs=[HBM_SPEC] * n, out_specs=[HBM_SPEC] * n,
        scratch_shapes=[pltpu.SemaphoreType.DMA((7 * n,)), pltpu.SemaphoreType.DMA((7 * n,)),
                        pltpu.SemaphoreType.DMA((n,))],
    )(*arrs)
    return list(outs)


def _pair_exchange(pieces, name):
    n = len(pieces)

    def body(*refs):
        ins, outs = refs[:n], refs[n:2 * n]
        send, recv = refs[2 * n:]
        x, y, c = lax.axis_index("x"), lax.axis_index("y"), lax.axis_index("c")
        sibling = (x, y, 1 - c)

        def copy(w, chip, src_core):
            return pltpu.make_async_remote_copy(
                src_ref=ins[w].at[2 * chip + src_core], dst_ref=outs[w].at[chip],
                send_sem=send.at[N_CHIP * w + chip], recv_sem=recv.at[N_CHIP * w + chip],
                device_id=sibling, device_id_type=MESH)

        cps = [copy(w, chip, 1 - c) for w in range(n) for chip in range(N_CHIP)]
        for cp in cps:
            cp.start()
        for cp in cps:
            cp.wait_recv()
        for cp in cps:
            cp.wait_send()

    outs = pl.pallas_call(
        body, name=name,
        out_shape=[jax.ShapeDtypeStruct((N_CHIP,) + a.shape[1:], a.dtype) for a in pieces],
        in_specs=[HBM_SPEC] * n, out_specs=[HBM_SPEC] * n,
        scratch_shapes=[pltpu.SemaphoreType.DMA((N_CHIP * n,)), pltpu.SemaphoreType.DMA((N_CHIP * n,))],
    )(*pieces)
    return list(outs)


def _hbm(a):
    return pltpu.with_memory_space_constraint(a, pltpu.HBM)


def _mesh_place():
    x, y, c = lax.axis_index("x"), lax.axis_index("y"), lax.axis_index("c")
    return x, y, c, [(1 - x, y), (x, 1 - y), (1 - x, 1 - y)]


def _chip_copies(srcs, lands, send, recv):
    x, y, c, chips = _mesh_place()
    return [pltpu.make_async_remote_copy(
        src_ref=srcs[w].at[2 * chip[0] + chip[1]], dst_ref=lands[w].at[k],
        send_sem=send.at[3 * w + k], recv_sem=recv.at[3 * w + k],
        device_id=(*chip, c), device_id_type=MESH)
        for k, chip in enumerate(chips) for w in range(len(srcs))]


def _chip_start(sums, name):
    return _split_start(_chip_copies, sums, [lax.empty((3,) + a.shape[1:], a.dtype) for a in sums], 3, name)


def _chip_wait(started, after, name):
    return _split_wait(_chip_copies, started, after, name)


def _pair_start(pieces, name):
    return _split_start(_pair_copies, pieces, [lax.empty((N_CHIP,) + a.shape[1:], a.dtype) for a in pieces], N_CHIP, name)


def _pair_wait(started, after, name):
    return _split_wait(_pair_copies, started, after, name)


def _pair_copies(srcs, lands, send, recv):
    x, y, c, _ = _mesh_place()
    return [pltpu.make_async_remote_copy(
        src_ref=srcs[w].at[2 * ch + 1 - c], dst_ref=lands[w].at[ch],
        send_sem=send.at[N_CHIP * w + ch], recv_sem=recv.at[N_CHIP * w + ch],
        device_id=(x, y, 1 - c), device_id_type=MESH)
        for w in range(len(srcs)) for ch in range(N_CHIP)]


def _split_start(copies_of, srcs, lands, per_array, name):
    n = len(srcs)

    def body(*refs):
        for cp in copies_of(refs[:n], refs[n:2 * n], refs[2 * n], refs[2 * n + 1]):
            cp.start()
        refs[-1][...] = jnp.zeros_like(refs[-1])

    res = pl.pallas_call(
        body, name=name,
        out_shape=[pltpu.SemaphoreType.DMA((per_array * n,)), pltpu.SemaphoreType.DMA((per_array * n,))]
        + [pltpu.HBM(a.shape, a.dtype) for a in srcs + lands] + [jax.ShapeDtypeStruct((8, LANES), F32)],
        in_specs=[HBM_SPEC] * (2 * n),
        out_specs=[SEM_SPEC, SEM_SPEC] + [HBM_SPEC] * (2 * n) + [pl.BlockSpec(memory_space=pltpu.VMEM)],
        input_output_aliases={i: 2 + i for i in range(2 * n)},
        compiler_params=pltpu.CompilerParams(has_side_effects=DATAFLOW),
    )(*[_hbm(a) for a in srcs], *[_hbm(a) for a in lands])
    return dict(send=res[0], recv=res[1], srcs=list(res[2:2 + n]), lands=list(res[2 + n:2 + 2 * n]), token=res[-1])


def _split_wait(copies_of, started, after, name):
    n = len(started["srcs"])

    def body(*refs):
        for cp in copies_of(refs[:n], refs[n:2 * n], refs[2 * n], refs[2 * n + 1]):
            cp.wait_send()
            cp.wait_recv()

    res = pl.pallas_call(
        body, name=name,
        out_shape=[pltpu.HBM(a.shape, a.dtype) for a in started["srcs"] + started["lands"]],
        in_specs=[HBM_SPEC] * (2 * n) + [SEM_SPEC, SEM_SPEC, ANY_SPEC],
        out_specs=[HBM_SPEC] * (2 * n),
        input_output_aliases={i: i for i in range(2 * n)},
        compiler_params=pltpu.CompilerParams(has_side_effects=DATAFLOW),
    )(*started["srcs"], *started["lands"], started["send"], started["recv"], after)
    return list(res[:n]), list(res[n:])


def _gather_copies(srcs, lands, send, recv):
    x, y, c, chips = _mesh_place()
    me = 4 * x + 2 * y + c
    targets = [(x, y, 1 - c)] + [(*chip, c) for chip in chips]
    return [pltpu.make_async_remote_copy(
        src_ref=srcs[w], dst_ref=lands[w].at[me], send_sem=send.at[4 * w + k], recv_sem=recv.at[4 * w + k],
        device_id=to, device_id_type=MESH)
        for w in range(len(srcs)) for k, to in enumerate(targets)]


def _gather_start(groups, name):
    sizes = [len(g) for g in groups]
    flat = [a for g in groups for a in g]
    n, ng = len(flat), len(groups)
    me = 4 * lax.axis_index("x") + 2 * lax.axis_index("y") + lax.axis_index("c")
    lands = [lax.dynamic_update_slice(lax.empty((N_DEV,) + a.shape, a.dtype), a[None], (me,) + (0,) * a.ndim)
             for a in flat]

    def body(*refs):
        srcs, zones = refs[:n], refs[n:2 * n]
        sems = refs[2 * n:2 * n + 2 * ng]
        token = refs[-1]
        lo = 0
        for gi, size in enumerate(sizes):
            for cp in _gather_copies(srcs[lo:lo + size], zones[lo:lo + size], sems[2 * gi], sems[2 * gi + 1]):
                cp.start()
            lo += size
        token[...] = jnp.zeros_like(token)

    sem_shapes = []
    for size in sizes:
        sem_shapes += [pltpu.SemaphoreType.DMA((4 * size,))] * 2
    res = pl.pallas_call(
        body, name=name,
        out_shape=sem_shapes + [pltpu.HBM(a.shape, a.dtype) for a in flat + lands] + [jax.ShapeDtypeStruct((8, LANES), F32)],
        in_specs=[HBM_SPEC] * (2 * n),
        out_specs=[SEM_SPEC] * (2 * ng) + [HBM_SPEC] * (2 * n) + [pl.BlockSpec(memory_space=pltpu.VMEM)],
        input_output_aliases={i: 2 * ng + i for i in range(2 * n)},
        compiler_params=pltpu.CompilerParams(has_side_effects=DATAFLOW),
    )(*[_hbm(a) for a in flat], *[_hbm(a) for a in lands])
    out, lo = [], 0
    for gi, size in enumerate(sizes):
        out.append(dict(send=res[2 * gi], recv=res[2 * gi + 1],
                        srcs=list(res[2 * ng + lo:2 * ng + lo + size]),
                        lands=list(res[2 * ng + n + lo:2 * ng + n + lo + size])))
        lo += size
    return out, res[-1]


def _gather_wait(started, after, name):
    n = len(started["srcs"])

    def body(*refs):
        srcs, zones = refs[:n], refs[n:2 * n]
        send, recv = refs[2 * n], refs[2 * n + 1]
        for cp in _gather_copies(srcs, zones, send, recv):
            cp.wait_send()
            cp.wait_recv()

    res = pl.pallas_call(
        body, name=name,
        out_shape=[pltpu.HBM(a.shape, a.dtype) for a in started["srcs"] + started["lands"]],
        in_specs=[HBM_SPEC] * (2 * n) + [SEM_SPEC, SEM_SPEC, pl.BlockSpec(memory_space=pl.ANY)],
        out_specs=[HBM_SPEC] * (2 * n),
        input_output_aliases={i: i for i in range(2 * n)},
        compiler_params=pltpu.CompilerParams(has_side_effects=DATAFLOW),
    )(*started["srcs"], *started["lands"], started["send"], started["recv"], after)
    return list(res[n:])


def _forward_pair(lands, name):
    n = len(lands)

    def body(*refs):
        zones = refs[:n]
        send, recv = refs[2 * n:]
        x, y, c, chips = _mesh_place()

        def copy(w, j, core):
            rows = zones[w].at[4 * chips[j][0] + 2 * chips[j][1] + core]
            return pltpu.make_async_remote_copy(
                src_ref=rows, dst_ref=rows, send_sem=send.at[3 * w + j], recv_sem=recv.at[3 * w + j],
                device_id=(x, y, 1 - c), device_id_type=MESH)

        sends = [copy(w, j, c) for w in range(n) for j in range(3)]
        for cp in sends:
            cp.start()
        for w in range(n):
            for j in range(3):
                copy(w, j, 1 - c).wait_recv()
        for cp in sends:
            cp.wait_send()

    return list(pl.pallas_call(
        body, name=name,
        out_shape=[jax.ShapeDtypeStruct(a.shape, a.dtype) for a in lands],
        in_specs=[HBM_SPEC] * n, out_specs=[HBM_SPEC] * n,
        input_output_aliases={i: i for i in range(n)},
        scratch_shapes=[pltpu.SemaphoreType.DMA((3 * n,)), pltpu.SemaphoreType.DMA((3 * n,))],
    )(*lands))


def _pair_sum(pieces, got, core, name):
    _, rows, cols = pieces.shape
    tr, tc = _tile_2d(rows, cols)

    def body(core_ref, mine_ref, got_ref, out_ref):
        del core_ref
        out_ref[...] = (mine_ref[...].astype(F32) + got_ref[...].astype(F32)).astype(out_ref.dtype)

    return pl.pallas_call(
        body, name=name,
        out_shape=jax.ShapeDtypeStruct((N_CHIP, rows, cols), pieces.dtype),
        grid_spec=pltpu.PrefetchScalarGridSpec(
            num_scalar_prefetch=1, grid=(N_CHIP, rows // tr, cols // tc),
            in_specs=[pl.BlockSpec((None, tr, tc), lambda ch, i, j, core: (2 * ch + core[0], i, j)),
                      pl.BlockSpec((None, tr, tc), lambda ch, i, j, core: (ch, i, j))],
            out_specs=pl.BlockSpec((None, tr, tc), lambda ch, i, j, core: (ch, i, j))),
        compiler_params=_params("parallel", "parallel", "parallel"),
    )(core, pieces, got)


def _tile_2d(rows, cols, budget=2 * 1024 * 1024):
    want = max(16, budget // (4 * cols))
    if rows <= want:
        return rows, cols
    t = (want // 16) * 16
    while t >= 16:
        if rows % t == 0:
            return t, cols
        t -= 16
    return rows, _tile(cols, max(LANES, budget // (4 * rows)))


def _adamw_math(w, g, m, v):
    m = ADAM_B1 * m + (1.0 - ADAM_B1) * g
    v = ADAM_B2 * v + (1.0 - ADAM_B2) * (g * g)
    m_hat = m / (1.0 - ADAM_B1 ** ADAM_STEP)
    v_hat = v / (1.0 - ADAM_B2 ** ADAM_STEP)
    delta = -ADAM_LR * (m_hat / (jnp.sqrt(v_hat) + ADAM_EPS) + ADAM_WD * w)
    return delta, m, v


def _adamw_shard(w, m, v, sums, got, chip, name):
    rows, cols = w.shape
    tr, tc = _tile_2d(rows, cols, budget=1024 * 1024)

    def body(chip_ref, w_ref, m_ref, v_ref, own_ref, got_ref, g_out, d_out, m_out, v_out):
        del chip_ref
        g = own_ref[...].astype(F32)
        for k in range(3):
            g = g + got_ref[k].astype(F32)
        delta, m_new, v_new = _adamw_math(w_ref[...], g, m_ref[...], v_ref[...])
        g_out[...] = g
        d_out[...] = delta
        m_out[...] = m_new
        v_out[...] = v_new

    blk = pl.BlockSpec((tr, tc), lambda i, j, chip: (i, j))
    return pl.pallas_call(
        body, name=name,
        out_shape=[jax.ShapeDtypeStruct((rows, cols), F32)] * 4,
        grid_spec=pltpu.PrefetchScalarGridSpec(
            num_scalar_prefetch=1, grid=(rows // tr, cols // tc),
            in_specs=[blk, blk, blk,
                      pl.BlockSpec((None, tr, tc), lambda i, j, chip: (chip[0], i, j)),
                      pl.BlockSpec((3, tr, tc), lambda i, j, chip: (0, i, j))],
            out_specs=[blk] * 4),
        compiler_params=_params("parallel", "parallel"),
    )(chip, w, m, v, sums, got)


def _adamw_replicated(w, m, v, parts, name):
    rows, cols = w.shape

    def body(w_ref, m_ref, v_ref, parts_ref, g_out, d_out, m_out, v_out):
        g = parts_ref[0]
        for d in range(1, N_DEV):
            g = g + parts_ref[d]
        delta, m_new, v_new = _adamw_math(w_ref[...], g, m_ref[...], v_ref[...])
        g_out[...] = g
        d_out[...] = delta
        m_out[...] = m_new
        v_out[...] = v_new

    return pl.pallas_call(
        body, name=name,
        out_shape=[jax.ShapeDtypeStruct((rows, cols), F32)] * 4,
        compiler_params=_params(),
    )(w, m, v, parts)


NN = ((1,), (0,))
NT = ((1,), (1,))
TN = ((0,), (0,))


ANY_SPEC = pl.BlockSpec(memory_space=pl.ANY)


def _mm(name, a, b, *, dims, grid, a_spec, b_spec, outs, extra=(), extra_specs=(), epilogue=None, acc_shape=None,
        deps=(), b_pieces=0):
    nk = grid[2]
    n_extra, n_out = len(extra), len(outs)
    first_out = 2 + n_extra + len(deps)

    def body(*refs):
        a_ref, b_ref = refs[0], refs[1]
        ex = refs[2:2 + n_extra]
        out_refs = refs[first_out:first_out + n_out]
        dot = lambda lhs, rhs: lax.dot_general(lhs.astype(BF16), rhs.astype(BF16), (dims, ((), ())),
                                               preferred_element_type=F32)
        if b_pieces:
            ks = b_ref.shape[-1]
            part = dot(a_ref[:, 0:ks], b_ref[0])
            for piece in range(1, b_pieces):
                part = part + dot(a_ref[:, piece * ks:(piece + 1) * ks], b_ref[piece])
        else:
            part = dot(a_ref[...], b_ref[...])

        def finish(acc):
            res = epilogue(acc, *[e[...] for e in ex]) if epilogue else (acc,)
            for val, o in zip(res, out_refs):
                o[...] = val.astype(o.dtype)

        if nk == 1:
            finish(part)
        else:
            acc_ref = refs[-1]
            k = pl.program_id(2)

            @pl.when(k == 0)
            def _():
                acc_ref[...] = part

            @pl.when(k > 0)
            def _():
                acc_ref[...] += part

            @pl.when(k == nk - 1)
            def _():
                finish(acc_ref[...])

    res = pl.pallas_call(
        body, name=name,
        out_shape=[jax.ShapeDtypeStruct(s, d) for s, d, _ in outs],
        grid=grid,
        in_specs=[a_spec, b_spec, *extra_specs] + [ANY_SPEC] * len(deps),
        out_specs=[sp for _, _, sp in outs],
        scratch_shapes=[pltpu.VMEM(acc_shape, F32)] if nk > 1 else [],
        compiler_params=_params("parallel", "parallel", "arbitrary"),
    )(a, b, *extra, *deps)
    return list(res)


def _mm_nn(name, a, b, out_dtype=F32, tm=1024, tn=1024, tk=2048, epilogue=None, n_out=1, extra=(), deps=()):
    m, kd = a.shape
    n = b.shape[1]
    tm, tn, tk = _tile(m, tm), _tile(n, tn), _tile(kd, tk)
    dts = out_dtype if isinstance(out_dtype, (list, tuple)) else [out_dtype] * n_out
    o_spec = pl.BlockSpec((tm, tn), lambda i, j, k: (i, j))
    return _mm(name, a, b, dims=NN, grid=(m // tm, n // tn, kd // tk),
               a_spec=pl.BlockSpec((tm, tk), lambda i, j, k: (i, k)),
               b_spec=pl.BlockSpec((tk, tn), lambda i, j, k: (k, j)),
               outs=[((m, n), dt, o_spec) for dt in dts], extra=extra, extra_specs=[o_spec] * len(extra),
               epilogue=epilogue, acc_shape=(tm, tn), deps=deps)


def _mm_nn_pieces(name, a, b, out_dtype=F32, tm=1024, tk=2048, epilogue=None, n_out=1):
    m, kd = a.shape
    npc, _, ns = b.shape
    tm, tk = _tile(m, tm), _tile(kd, tk)
    tn = _tile(ns, 1024)
    r = ns // tn
    dts = out_dtype if isinstance(out_dtype, (list, tuple)) else [out_dtype] * n_out
    o_spec = pl.BlockSpec((tm, tn), lambda i, j, k: (i, j))
    return _mm(name, a, b, dims=NN, grid=(m // tm, npc * r, kd // tk),
               a_spec=pl.BlockSpec((tm, tk), lambda i, j, k: (i, k)),
               b_spec=pl.BlockSpec((None, tk, tn), lambda i, j, k: (j // r, k, j % r)),
               outs=[((m, npc * ns), dt, o_spec) for dt in dts], epilogue=epilogue, acc_shape=(tm, tn))


def _mm_nt(name, a, b, out_dtype=F32, tm=1024, tn=1024, tk=2048, epilogue=None, extra=(), n_rows=None, deps=()):
    m, kd = a.shape
    n = n_rows or b.shape[0]
    tm, tn, tk = _tile(m, tm), _tile(n, tn), _tile(kd, tk)
    o_spec = pl.BlockSpec((tm, tn), lambda i, j, k: (i, j))
    return _mm(name, a, b, dims=NT, grid=(m // tm, n // tn, kd // tk),
               a_spec=pl.BlockSpec((tm, tk), lambda i, j, k: (i, k)),
               b_spec=pl.BlockSpec((tn, tk), lambda i, j, k: (j, k)),
               outs=[((m, n), out_dtype, o_spec)], extra=extra, extra_specs=[o_spec] * len(extra),
               epilogue=epilogue, acc_shape=(tm, tn), deps=deps)[0]


def _mm_nt_pieces(name, a, b, out_dtype=F32, tm=1024, tn=256, deps=()):
    m = a.shape[0]
    npc, n, ks = b.shape
    tm, tn = _tile(m, tm), _tile(n, tn)
    return _mm(name, a, b, dims=NT, grid=(m // tm, n // tn, 1),
               a_spec=pl.BlockSpec((tm, npc * ks), lambda i, j, k: (i, 0)),
               b_spec=pl.BlockSpec((npc, tn, ks), lambda i, j, k: (0, j, 0)),
               outs=[((m, n), out_dtype, pl.BlockSpec((tm, tn), lambda i, j, k: (i, j)))],
               deps=deps, b_pieces=npc)[0]


def _mm_tn(name, a, b, out_dtype=BF16, tm=1024, tn=1024, pieces=0, deps=()):
    t, m = a.shape
    n = b.shape[1]
    tm = _tile(m, tm)
    if pieces:
        ns = n // pieces
        tn = _tile(ns, tn)
        r = ns // tn
        out = ((pieces, m, ns), out_dtype, pl.BlockSpec((None, tm, tn), lambda i, j, k: (j // r, i, j % r)))
    else:
        tn = _tile(n, tn)
        out = ((m, n), out_dtype, pl.BlockSpec((tm, tn), lambda i, j, k: (i, j)))
    return _mm(name, a, b, dims=TN, grid=(m // tm, n // tn, 1),
               a_spec=pl.BlockSpec((t, tm), lambda i, j, k: (0, i)),
               b_spec=pl.BlockSpec((t, tn), lambda i, j, k: (0, j)),
               outs=[out], deps=deps)[0]


def _rms_f(x, g):
    r = lax.rsqrt(jnp.mean(x * x, axis=-1, keepdims=True) + EPS)
    xh = x * r
    return xh * g, xh, r


def _rms_b(dy, g, xh, r):
    dxh = dy * g
    dx = r * (dxh - xh * jnp.mean(dxh * xh, axis=-1, keepdims=True))
    dg = jnp.sum(dy * xh, axis=0, keepdims=True)
    return dx, dg


def _rowwise(name, fn, row_ins, vec_ins, row_outs, vec_outs, deps=()):
    t = row_ins[0].shape[0]
    tr = min(ROW_TILE, t)
    nr, nv, no = len(row_ins), len(vec_ins), len(row_outs)
    first_out = nr + nv + len(deps)

    def body(*refs):
        ri, vi = refs[:nr], refs[nr:nr + nv]
        ro, vo = refs[first_out:first_out + no], refs[first_out + no:]
        routs, vouts = fn(*[r[...] for r in ri], *[v[...] for v in vi])
        for o, val in zip(ro, routs):
            o[...] = val.astype(o.dtype)
        i = pl.program_id(0)
        for o, val in zip(vo, vouts):
            @pl.when(i == 0)
            def _():
                o[...] = val

            @pl.when(i > 0)
            def _():
                o[...] += val

    res = pl.pallas_call(
        body, name=name,
        out_shape=[jax.ShapeDtypeStruct((t, w), d) for w, d in row_outs]
        + [jax.ShapeDtypeStruct((1, w), F32) for w in vec_outs],
        grid=(t // tr,),
        in_specs=[pl.BlockSpec((tr, a.shape[1]), lambda i: (i, 0)) for a in row_ins]
        + [pl.BlockSpec((1, a.shape[1]), lambda i: (0, 0)) for a in vec_ins] + [ANY_SPEC] * len(deps),
        out_specs=[pl.BlockSpec((tr, w), lambda i: (i, 0)) for w, _ in row_outs]
        + [pl.BlockSpec((1, w), lambda i: (0, 0)) for w in vec_outs],
        compiler_params=_params("arbitrary"),
    )(*row_ins, *vec_ins, *deps)
    return list(res)


def _norm_fwd(x, g, name, deps=()):
    d = x.shape[1]
    return _rowwise(name, lambda x, g: ((_rms_f(x, g)[0],), ()), [x], [g], [(d, BF16)], [], deps=deps)[0]


def _norm_bwd(x, dxn, dres, g, name):
    d = x.shape[1]

    def fn(x, dxn, dres, g):
        _, xh, r = _rms_f(x, g)
        dx, dg = _rms_b(dxn, g, xh, r)
        return (dres + dx,), (dg,)

    return _rowwise(name, fn, [x, dxn, dres], [g], [(d, F32)], [d])


def _block_fwd(h_prev, y, g_y, g_n, name):
    d = h_prev.shape[1]

    def fn(h_prev, y, g_y, g_n):
        h = h_prev + _rms_f(y, g_y)[0]
        return (h, _rms_f(h, g_n)[0]), ()

    return _rowwise(name, fn, [h_prev, y], [g_y, g_n], [(d, F32), (d, BF16)], [])


def _block_bwd(h, y, d_up, d_n, g_y, g_n, name, deps=()):
    d = h.shape[1]

    def fn(h, y, d_up, d_n, g_y, g_n):
        _, hh, hr = _rms_f(h, g_n)
        dh_n, dg_n = _rms_b(d_n, g_n, hh, hr)
        dh = d_up + dh_n
        _, yh, yr = _rms_f(y, g_y)
        dy, dg_y = _rms_b(dh, g_y, yh, yr)
        return (dh, dy), (dg_y, dg_n)

    return _rowwise(name, fn, [h, y, d_up, d_n], [g_y, g_n], [(d, F32), (d, BF16)], [d, d], deps=deps)


def _loss_head(h2, z, e, target, g, name):
    d = h2.shape[1]

    def fn(h2, z, e, target, g):
        gate = jax.nn.sigmoid(z)
        y, xh, r = _rms_f(e * gate, g)
        diff = h2 + y - target
        loss = 0.5 * jnp.sum(jnp.mean(diff * diff, axis=-1, keepdims=True), axis=0, keepdims=True)
        dh3 = diff / d
        deg, dg = _rms_b(dh3, g, xh, r)
        dz = deg * e * gate * (1.0 - gate)
        de = deg * gate
        return (dh3, dz, de), (dg, jnp.broadcast_to(loss, (1, LANES)))

    return _rowwise(name, fn, [h2, z, e, target], [g], [(d, F32), (d, BF16), (d, BF16)], [d, LANES])


def _shift_down(x, s, rows):
    return jnp.where(rows >= s, pltpu.roll(x, s, axis=0), 0.0)


def _shift_up(x, s, rows):
    t = x.shape[0]
    return jnp.where(rows < t - s, pltpu.roll(x, t - s, axis=0), 0.0)


def _conv_pre(x, w, rows):
    pre = x * w[QK_CONV - 1:QK_CONV, :]
    for s in range(1, QK_CONV):
        pre = pre + _shift_down(x, s, rows) * w[QK_CONV - 1 - s:QK_CONV - s, :]
    return pre


def _conv_fwd(proj_a, w, start, name):
    t = proj_a.shape[0]
    width = w.shape[1]
    tc = _tile(width, 256)
    off = start // tc

    def body(x_ref, w_ref, y_ref):
        x = x_ref[...]
        rows = lax.broadcasted_iota(jnp.int32, x.shape, 0)
        pre = _conv_pre(x, w_ref[...], rows)
        y_ref[...] = pre * jax.nn.sigmoid(pre)

    return pl.pallas_call(
        body, name=name,
        out_shape=jax.ShapeDtypeStruct((t, width), F32),
        grid=(width // tc,),
        in_specs=[pl.BlockSpec((t, tc), lambda j: (0, off + j)), pl.BlockSpec((QK_CONV, tc), lambda j: (0, j))],
        out_specs=pl.BlockSpec((t, tc), lambda j: (0, j)),
        compiler_params=_params("parallel"),
    )(proj_a, w)


def _conv_bwd(proj_a, w, dy, start, name):
    t = proj_a.shape[0]
    width = w.shape[1]
    tc = _tile(width, 256)
    off = start // tc

    def body(x_ref, w_ref, dy_ref, dx_ref, dw_ref):
        x, w = x_ref[...], w_ref[...]
        rows = lax.broadcasted_iota(jnp.int32, x.shape, 0)
        pre = _conv_pre(x, w, rows)
        sig = jax.nn.sigmoid(pre)
        dpre = dy_ref[...] * (sig * (1.0 + pre * (1.0 - sig)))
        dx = dpre * w[QK_CONV - 1:QK_CONV, :]
        dws = [jnp.sum(dpre * x, axis=0, keepdims=True)]
        for s in range(1, QK_CONV):
            dx = dx + _shift_up(dpre, s, rows) * w[QK_CONV - 1 - s:QK_CONV - s, :]
            dws.append(jnp.sum(dpre * _shift_down(x, s, rows), axis=0, keepdims=True))
        dx_ref[...] = dx.astype(dx_ref.dtype)
        for s in range(QK_CONV):
            dw_ref[QK_CONV - 1 - s:QK_CONV - s, :] = dws[s]

    return pl.pallas_call(
        body, name=name,
        out_shape=[jax.ShapeDtypeStruct((t, width), BF16), jax.ShapeDtypeStruct((QK_CONV, width), F32)],
        grid=(width // tc,),
        in_specs=[pl.BlockSpec((t, tc), lambda j: (0, off + j)), pl.BlockSpec((QK_CONV, tc), lambda j: (0, j)),
                  pl.BlockSpec((t, tc), lambda j: (0, j))],
        out_specs=[pl.BlockSpec((t, tc), lambda j: (0, j)), pl.BlockSpec((QK_CONV, tc), lambda j: (0, j))],
        compiler_params=_params("parallel"),
    )(proj_a, w, dy)


def _dot(a, b, dims):
    return lax.dot_general(a.astype(BF16), b.astype(BF16), (dims, ((), ())), preferred_element_type=F32)


def _dot_split(ones, x):
    hi = x.astype(BF16)
    rest = x - hi.astype(F32)
    mid = rest.astype(BF16)
    lo = (rest - mid.astype(F32)).astype(BF16)
    dot = lambda part: lax.dot_general(ones, part, (NN, ((), ())), preferred_element_type=F32)
    return dot(hi) + dot(mid) + dot(lo)


def _chunk_iotas():
    rows = lax.broadcasted_iota(jnp.int32, (CHUNK, CHUNK), 0)
    cols = lax.broadcasted_iota(jnp.int32, (CHUNK, CHUNK), 1)
    return rows, cols


def _gates_fwd(proj_b, bias, col_block, heads, name):
    t = proj_b.shape[0]

    def body(x_ref, b_ref, gb_ref, gbt_ref):
        rows, cols = _chunk_iotas()
        tri = (cols <= rows).astype(BF16)
        for ci in range(t // CHUNK):
            sl = slice(ci * CHUNK, (ci + 1) * CHUNK)
            pre = x_ref[sl, :] + b_ref[...]
            log_f = jnp.minimum(pre, 0.0) - jnp.log(1.0 + jnp.exp(-jnp.abs(pre)))
            log_f = jnp.where((cols >= heads) & (cols < 2 * heads), log_f, 0.0)
            tile = jnp.where(cols < heads, pre, _dot_split(tri, log_f))
            gb_ref[sl, :] = tile
            gbt_ref[:, sl] = tile.T

    return pl.pallas_call(
        body, name=name,
        out_shape=[jax.ShapeDtypeStruct((t, LANES), F32), jax.ShapeDtypeStruct((LANES, t), F32)],
        grid=(1,),
        in_specs=[pl.BlockSpec((t, LANES), lambda i: (0, col_block)), pl.BlockSpec((1, LANES), lambda i: (0, 0))],
        out_specs=[pl.BlockSpec((t, LANES), lambda i: (0, 0)), pl.BlockSpec((LANES, t), lambda i: (0, 0))],
        compiler_params=_params("arbitrary"),
    )(proj_b, bias)


def _gate_columns(gb, heads, name):
    t = gb.shape[0]

    def body(gb_ref, out_ref):
        g = gb_ref[...]
        lane = lax.broadcasted_iota(jnp.int32, g.shape, 1)
        column = jnp.sum(jnp.where(lane == pl.program_id(0), g, 0.0), axis=-1, keepdims=True)
        out_ref[...] = jnp.broadcast_to(column, out_ref.shape)

    return pl.pallas_call(
        body, name=name,
        out_shape=jax.ShapeDtypeStruct((2 * heads, t, LANES), F32),
        grid=(2 * heads,),
        in_specs=[pl.BlockSpec((t, LANES), lambda h: (0, 0))],
        out_specs=pl.BlockSpec((None, t, LANES), lambda h: (h, 0, 0)),
        compiler_params=_params("parallel"),
    )(gb)


def _gates_bwd(proj_b, bias, dcol, drow, col_block, heads, name):
    t = proj_b.shape[0]

    def body(x_ref, b_ref, dc_ref, dr_ref, dx_ref, db_ref):
        rows, cols = _chunk_iotas()
        later = (cols >= rows).astype(BF16)
        total = jnp.zeros((1, LANES), F32)
        for ci in range(t // CHUNK):
            sl = slice(ci * CHUNK, (ci + 1) * CHUNK)
            by_row = dr_ref[:, sl].T
            d = dc_ref[sl, :] + jnp.where(cols < heads, by_row,
                                          jnp.where(cols < 2 * heads, -pltpu.roll(by_row, heads, axis=1), 0.0))
            d_log_f = _dot_split(later, jnp.where(cols >= heads, d, 0.0))
            pre = x_ref[sl, :] + b_ref[...]
            dx = jnp.where(cols < heads, d, jnp.where(cols < 2 * heads, d_log_f * jax.nn.sigmoid(-pre), 0.0))
            dx_ref[sl, :] = dx.astype(dx_ref.dtype)
            total = total + jnp.sum(dx, axis=0, keepdims=True)
        db_ref[...] = total

    return pl.pallas_call(
        body, name=name,
        out_shape=[jax.ShapeDtypeStruct((t, LANES), BF16), jax.ShapeDtypeStruct((1, LANES), F32)],
        grid=(1,),
        in_specs=[pl.BlockSpec((t, LANES), lambda i: (0, col_block)), pl.BlockSpec((1, LANES), lambda i: (0, 0)),
                  pl.BlockSpec((t, LANES), lambda i: (0, 0)), pl.BlockSpec((LANES, t), lambda i: (0, 0))],
        out_specs=[pl.BlockSpec((t, LANES), lambda i: (0, 0)), pl.BlockSpec((1, LANES), lambda i: (0, 0))],
        compiler_params=_params("arbitrary"),
    )(proj_b, bias, dcol, drow)


def _gate_views(li_ref, b_ref, gbt_ref, s, head, heads, sl):
    li_c, b_c = li_ref[s, sl, :], b_ref[s, sl, :]
    li_r = jnp.broadcast_to(gbt_ref[head, :, sl], (CHUNK, CHUNK))
    b_r = jnp.broadcast_to(gbt_ref[heads + head, :, sl], (CHUNK, CHUNK))
    return li_c, b_c, li_r, b_r


def _chunk_state(kt, k, v, li_c, b_c, ct, nrow, m_prev):
    b_tot = b_c[CHUNK - 1:CHUNK, 0:1]
    a = b_tot - b_c[:, 0:1] + li_c[:, 0:1]
    m_new = jnp.maximum(b_tot + m_prev, jnp.max(a, axis=0, keepdims=True))
    dec = jnp.exp(b_tot + m_prev - m_new)
    w = jnp.exp(a - m_new)
    wv = w * v
    ct_new = dec * ct + _dot(kt, wv, NN)
    n_new = dec * nrow + jnp.sum(w * k, axis=0, keepdims=True)
    return ct_new, n_new, m_new, dec, w, wv


def _chunk_scores(q, k, b_c, li_r, b_r, m_prev):
    rows, cols = _chunk_iotas()
    dlog = jnp.where(cols <= rows, b_c - b_r + li_r, -jnp.inf)
    inter = b_c[:, 0:1] + m_prev
    mt = jnp.maximum(inter, jnp.max(dlog, axis=-1, keepdims=True))
    dw = jnp.exp(dlog - mt)
    iw = jnp.exp(inter - mt)
    a = _dot(q, k, NT)
    return dw, iw, mt, a, a * dw


def _to_row(column):
    rows, cols = _chunk_iotas()
    return jnp.sum(jnp.where(rows == cols, jnp.broadcast_to(column, (CHUNK, CHUNK)), 0.0), axis=0, keepdims=True)


def _head_out(h, o, g):
    cell, hh, r = _rms_f(h, g)
    sig = jax.nn.sigmoid(o)
    return sig * cell, (cell, hh, r, sig)


HEADS_PER_STEP = 2


def _mlstm_fwd(qc, kc, kct, proj_a, gcols, gbt, g_mlstm, heads, d_model, name):
    t = qc.shape[0]
    dh = g_mlstm.shape[1] // heads
    nc = t // CHUNK
    scale = dh ** -0.5
    hp = HEADS_PER_STEP

    def body(q_ref, k_ref, kt_ref, v_ref, o_ref, li_ref, b_ref, gbt_ref, g_ref, out_ref, ct_ref, n_ref, m_ref):
        step = pl.program_id(0)
        ct_ref[...] = jnp.zeros_like(ct_ref)
        n_ref[...] = jnp.zeros_like(n_ref)
        m_ref[...] = jnp.zeros_like(m_ref)

        def chunk(ci, carry):
            sl = pl.ds(pl.multiple_of(ci * CHUNK, CHUNK), CHUNK)
            for s in range(hp):
                cs = slice(s * dh, (s + 1) * dh)
                q, k, v, kt = q_ref[sl, cs], k_ref[sl, cs] * scale, v_ref[sl, cs], kt_ref[cs, sl] * scale
                li_c, b_c, li_r, b_r = _gate_views(li_ref, b_ref, gbt_ref, s, step * hp + s, heads, sl)
                ct, nrow, m_prev = ct_ref[s], n_ref[s, 0:1, :], m_ref[s, 0:1, 0:1]
                dw, iw, mt, _, sm = _chunk_scores(q, k, b_c, li_r, b_r, m_prev)
                num = _dot(jnp.concatenate([iw * q, sm], axis=1), jnp.concatenate([ct, v], axis=0), NN)
                den = iw * jnp.sum(q * nrow, axis=-1, keepdims=True) + jnp.sum(sm, axis=-1, keepdims=True)
                h = num / jnp.maximum(jnp.abs(den), jnp.exp(-mt))
                y, _ = _head_out(h, o_ref[sl, cs], g_ref[:, cs])
                out_ref[sl, cs] = y.astype(out_ref.dtype)
                ct_new, n_new, m_new, _, _, _ = _chunk_state(kt, k, v, li_c, b_c, ct, nrow, m_prev)
                ct_ref[s] = ct_new
                n_ref[s, 0:1, :] = n_new
                m_ref[s, 0:1, 0:1] = m_new
            return carry

        lax.fori_loop(0, nc, chunk, 0, unroll=2)

    col = lambda off: pl.BlockSpec((t, hp * dh), lambda h: (0, off // hp + h))
    slab = lambda off: pl.BlockSpec((hp, t, LANES), lambda h: (off // hp + h, 0, 0))
    return pl.pallas_call(
        body, name=name,
        out_shape=jax.ShapeDtypeStruct((t, d_model), BF16),
        grid=(heads // hp,),
        in_specs=[col(0), col(0), pl.BlockSpec((hp * dh, t), lambda h: (h, 0)), col(2 * heads), col(3 * heads),
                  slab(0), slab(heads), pl.BlockSpec((2 * heads, 1, t), lambda h: (0, 0, 0)),
                  pl.BlockSpec((1, hp * dh), lambda h: (0, h))],
        out_specs=pl.BlockSpec((t, hp * dh), lambda h: (0, h)),
        scratch_shapes=[pltpu.VMEM((hp, dh, dh), F32), pltpu.VMEM((hp, 8, dh), F32), pltpu.VMEM((hp, 8, LANES), F32)],
        compiler_params=_params("arbitrary"),
    )(qc, kc, kct, proj_a, proj_a, gcols, gcols, gbt, g_mlstm)


def _mlstm_bwd(qc, kc, qct, kct, proj_a, gcols, gbt, g_mlstm, dhcat, heads, name, dep):
    t = qc.shape[0]
    wm = g_mlstm.shape[1]
    dh = wm // heads
    nc = t // CHUNK
    scale = dh ** -0.5

    def body(q_ref, k_ref, qt_ref, kt_ref, v_ref, o_ref, li_ref, b_ref, gbt_ref, g_ref, dy_ref, dep_ref,
             dq_ref, dk_ref, dv_ref, do_ref, dcol_ref, drow_ref, dg_ref,
             ct_all, n_all, m_all, dct_ref, dn_ref):
        del dep_ref
        head = pl.program_id(0)
        g = g_ref[...]

        @pl.when(head == 0)
        def _():
            dcol_ref[...] = jnp.zeros_like(dcol_ref)
            drow_ref[...] = jnp.zeros_like(drow_ref)

        def load(ci):
            sl = pl.ds(pl.multiple_of(ci * CHUNK, CHUNK), CHUNK)
            return (sl, q_ref[sl, :], k_ref[sl, :] * scale, v_ref[sl, :], kt_ref[:, sl] * scale) \
                + _gate_views(li_ref, b_ref, gbt_ref, 0, head, heads, sl)

        def fwd(ci, carry):
            ct, nrow, m_prev = carry
            ct_all[ci] = ct
            n_all[ci, 0:1, :] = nrow
            m_all[ci, 0:1, 0:1] = m_prev
            _, _, k, v, kt, li_c, b_c, _, _ = load(ci)
            return _chunk_state(kt, k, v, li_c, b_c, ct, nrow, m_prev)[:3]

        lax.fori_loop(0, nc, fwd, (jnp.zeros((dh, dh), F32), jnp.zeros((1, dh), F32), jnp.zeros((1, 1), F32)))

        dct_ref[...] = jnp.zeros_like(dct_ref)
        dn_ref[...] = jnp.zeros_like(dn_ref)
        dg_ref[...] = jnp.zeros_like(dg_ref)
        rows, cols = _chunk_iotas()
        last = lax.broadcasted_iota(jnp.int32, (CHUNK, 1), 0) == CHUNK - 1
        lane = lax.broadcasted_iota(jnp.int32, (CHUNK, LANES), 1)

        def bwd(step, carry):
            ci = nc - 1 - step
            sl, q, k, v, kt, li_c, b_c, li_r, b_r = load(ci)
            qt = qt_ref[:, sl]
            ct, nrow, m_prev = ct_all[ci], n_all[ci, 0:1, :], m_all[ci, 0:1, 0:1]
            dw, iw, mt, a, sm = _chunk_scores(q, k, b_c, li_r, b_r, m_prev)
            qc_ = _dot(q, ct, NN)
            qn = jnp.sum(q * nrow, axis=-1, keepdims=True)
            num = iw * qc_ + _dot(sm, v, NN)
            den = iw * qn + jnp.sum(sm, axis=-1, keepdims=True)
            floor = jnp.exp(-mt)
            dn = jnp.maximum(jnp.abs(den), floor)
            h = num / dn
            _, (cell, hh, r, sig) = _head_out(h, o_ref[sl, :], g)
            dy = dy_ref[sl, :]
            dcell = dy * sig
            do_ref[sl, :] = (dy * cell * sig * (1.0 - sig)).astype(do_ref.dtype)
            dh_, dg = _rms_b(dcell, g, hh, r)
            dg_ref[...] += dg
            dnum = dh_ / dn
            ddn = -jnp.sum(dh_ * h, axis=-1, keepdims=True) / dn
            dden = jnp.where(jnp.abs(den) >= floor, ddn * jnp.sign(den), 0.0)
            idn = iw * dnum
            idd = iw * dden
            dw_t = jnp.exp(jnp.where(cols >= rows, b_r - b_c + li_c, -jnp.inf) - _to_row(mt))
            sm_t = _dot(k, q, NT) * dw_t
            dsm = _dot(dnum, v, NT) + dden
            da_ = dsm * dw
            da_t = (_dot(v, dnum, NT) + _to_row(dden)) * dw_t
            e = da_ * a
            dq = _dot(idn, ct, NT) + _dot(da_, k, NN) + idd * nrow
            dk = _dot(da_t, q, NN)
            dv = _dot(sm_t, dnum, NN)
            dct = _dot(qt, idn, NN)
            dnr = jnp.sum(q * idd, axis=0, keepdims=True)
            diw = jnp.sum(dnum * qc_, axis=-1, keepdims=True) + dden * qn
            db = jnp.sum(e, axis=-1, keepdims=True) + diw * iw
            _, _, _, dec, w, wv = _chunk_state(kt, k, v, li_c, b_c, ct, nrow, m_prev)
            dct_new, dn_new = dct_ref[...], dn_ref[0:1, :]
            ddec = jnp.sum(jnp.sum(dct_new * ct, axis=-1, keepdims=True), axis=0, keepdims=True) \
                + jnp.sum(dn_new * nrow, axis=-1, keepdims=True)
            dk = dk + _dot(wv, dct_new, NT) + w * dn_new
            dwv = _dot(k, dct_new, NN)
            dv = dv + w * dwv
            dwt = jnp.sum(dwv * v, axis=-1, keepdims=True) + jnp.sum(k * dn_new, axis=-1, keepdims=True)
            da = dwt * w
            dbtot = jnp.sum(da, axis=0, keepdims=True) + ddec * dec
            db = db - da + jnp.where(last, dbtot, 0.0)
            dct_ref[...] = dec * dct_new + dct
            dn_ref[0:1, :] = dec * dn_new + dnr
            dq_ref[sl, :] = dq
            dk_ref[sl, :] = dk * scale
            dv_ref[sl, :] = dv.astype(dv_ref.dtype)
            dcol_ref[sl, :] += jnp.where(lane == head, da, 0.0) + jnp.where(lane == heads + head, db, 0.0)
            drow_ref[head, :, sl] = jnp.sum(e, axis=0, keepdims=True)
            return carry

        lax.fori_loop(0, nc, bwd, 0, unroll=2)

    col = lambda off: pl.BlockSpec((t, dh), lambda h: (0, off + h))
    rowb = pl.BlockSpec((dh, t), lambda h: (h, 0))
    slab = lambda off: pl.BlockSpec((1, t, LANES), lambda h: (off + h, 0, 0))
    vec = pl.BlockSpec((1, dh), lambda h: (0, h))
    return pl.pallas_call(
        body, name=name,
        out_shape=[jax.ShapeDtypeStruct((t, wm), F32)] * 2 + [jax.ShapeDtypeStruct((t, wm), BF16)] * 2
        + [jax.ShapeDtypeStruct((t, LANES), F32), jax.ShapeDtypeStruct((heads, 1, t), F32),
           jax.ShapeDtypeStruct((1, wm), F32)],
        grid=(heads,),
        in_specs=[col(0), col(0), rowb, rowb, col(2 * heads), col(3 * heads), slab(0), slab(heads),
                  pl.BlockSpec((2 * heads, 1, t), lambda h: (0, 0, 0)), vec, col(0), ANY_SPEC],
        out_specs=[col(0), col(0), col(0), col(0), pl.BlockSpec((t, LANES), lambda h: (0, 0)),
                   pl.BlockSpec((heads, 1, t), lambda h: (0, 0, 0)), vec],
        scratch_shapes=[pltpu.VMEM((nc, dh, dh), F32), pltpu.VMEM((nc, 8, dh), F32), pltpu.VMEM((nc, 8, LANES), F32),
                        pltpu.VMEM((dh, dh), F32), pltpu.VMEM((8, dh), F32)],
        compiler_params=_params("arbitrary"),
    )(qc, kc, qct, kct, proj_a, proj_a, gcols, gcols, gbt, g_mlstm, dhcat, dep)


def _pool_window(gi):
    return jnp.where(gi == 0, 2.0, jnp.where(gi == 1, 4.0, jnp.where(gi == 2, 8.0, 16.0))).astype(F32)


def _pool_select(gi, levels):
    return jnp.where(gi == 0, levels[0], jnp.where(gi == 1, levels[1], jnp.where(gi == 2, levels[2], levels[3])))


def _pooled(u, gi, rows):
    s, levels = u, []
    for lvl in range(POOL_GROUPS):
        s = s + _shift_down(s, 1 << lvl, rows)
        levels.append(s)
    cnt = jnp.minimum((rows + 1).astype(F32), _pool_window(gi))
    return _pool_select(gi, levels) / cnt - u, cnt


def _pool_fwd(proj_b, w_pool, scale, hcat, name):
    t, d_model = hcat.shape
    g, cg, _ = w_pool.shape
    off = (d_model - g * cg) // cg

    def body(u_ref, w_ref, s_ref, hcat_ref, out_ref):
        del hcat_ref
        gi = pl.program_id(0)
        u = u_ref[...]
        rows = lax.broadcasted_iota(jnp.int32, u.shape, 0)
        pooled, _ = _pooled(u, gi, rows)
        out_ref[...] = (_dot(pooled, w_ref[...], NN) * s_ref[...]).astype(out_ref.dtype)

    return pl.pallas_call(
        body, name=name,
        out_shape=jax.ShapeDtypeStruct((t, d_model), BF16),
        grid=(g,),
        in_specs=[pl.BlockSpec((t, cg), lambda gi: (0, gi)), pl.BlockSpec((None, cg, cg), lambda gi: (gi, 0, 0)),
                  pl.BlockSpec((1, cg), lambda gi: (0, gi)), HBM_SPEC],
        out_specs=pl.BlockSpec((t, cg), lambda gi: (0, off + gi)),
        input_output_aliases={3: 0},
        compiler_params=_params("arbitrary"),
    )(proj_b, w_pool, scale, hcat)


def _pool_bwd(proj_b, w_pool, scale, dhcat, name):
    t, d_model = dhcat.shape
    g, cg, _ = w_pool.shape
    off = (d_model - g * cg) // cg

    def body(u_ref, w_ref, s_ref, dy_ref, du_ref, dw_ref, ds_ref):
        gi = pl.program_id(0)
        u, w, dy = u_ref[...], w_ref[...], dy_ref[...]
        rows = lax.broadcasted_iota(jnp.int32, u.shape, 0)
        pooled, cnt = _pooled(u, gi, rows)
        mixed = _dot(pooled, w, NN)
        ds_ref[...] = jnp.sum(dy * mixed, axis=0, keepdims=True)
        dmixed = dy * s_ref[...]
        dw_ref[...] = _dot(pooled, dmixed, TN).astype(dw_ref.dtype)
        dpooled = _dot(dmixed, w, NT)
        s, levels = dpooled / cnt, []
        for lvl in range(POOL_GROUPS):
            s = s + _shift_up(s, 1 << lvl, rows)
            levels.append(s)
        du_ref[...] = (_pool_select(gi, levels) - dpooled).astype(du_ref.dtype)

    return pl.pallas_call(
        body, name=name,
        out_shape=[jax.ShapeDtypeStruct((t, g * cg), BF16), jax.ShapeDtypeStruct((g, cg, cg), BF16),
                   jax.ShapeDtypeStruct((1, g * cg), F32)],
        grid=(g,),
        in_specs=[pl.BlockSpec((t, cg), lambda gi: (0, gi)), pl.BlockSpec((None, cg, cg), lambda gi: (gi, 0, 0)),
                  pl.BlockSpec((1, cg), lambda gi: (0, gi)), pl.BlockSpec((t, cg), lambda gi: (0, off + gi))],
        out_specs=[pl.BlockSpec((t, cg), lambda gi: (0, gi)), pl.BlockSpec((None, cg, cg), lambda gi: (gi, 0, 0)),
                   pl.BlockSpec((1, cg), lambda gi: (0, gi))],
        compiler_params=_params("arbitrary"),
    )(proj_b, w_pool, scale, dhcat)


def _pad_cols(a, width):
    return jnp.pad(a, ((0, 0), (0, width - a.shape[1])))


def kernel(x, p, w_in, b_gates, w_qk_conv, g_mlstm, w_pool, pool_scale, w_out, g_mix_pre, g_mix_post, w_ff1, w_ff2, g_ff_pre, g_ff_post, w_ple_proj, w_ple_gate, g_ple_gate, g_ple_post, loss_target, m_w_in, m_b_gates, m_w_qk_conv, m_g_mlstm, m_w_pool, m_pool_scale, m_w_out, m_g_mix_pre, m_g_mix_post, m_w_ff1, m_w_ff2, m_g_ff_pre, m_g_ff_post, m_w_ple_proj, m_w_ple_gate, m_g_ple_gate, m_g_ple_post, v_w_in, v_b_gates, v_w_qk_conv, v_g_mlstm, v_w_pool, v_pool_scale, v_w_out, v_g_mix_pre, v_g_mix_post, v_w_ff1, v_w_ff2, v_g_ff_pre, v_g_ff_post, v_w_ple_proj, v_w_ple_gate, v_g_ple_gate, v_g_ple_post):
    weights = dict(w_in=w_in, b_gates=b_gates, w_qk_conv=w_qk_conv, g_mlstm=g_mlstm, w_pool=w_pool,
                   pool_scale=pool_scale, w_out=w_out, g_mix_pre=g_mix_pre, g_mix_post=g_mix_post, w_ff1=w_ff1,
                   w_ff2=w_ff2, g_ff_pre=g_ff_pre, g_ff_post=g_ff_post, w_ple_proj=w_ple_proj,
                   w_ple_gate=w_ple_gate, g_ple_gate=g_ple_gate, g_ple_post=g_ple_post)
    mom1 = dict(w_in=m_w_in, b_gates=m_b_gates, w_qk_conv=m_w_qk_conv, g_mlstm=m_g_mlstm, w_pool=m_w_pool,
                pool_scale=m_pool_scale, w_out=m_w_out, g_mix_pre=m_g_mix_pre, g_mix_post=m_g_mix_post,
                w_ff1=m_w_ff1, w_ff2=m_w_ff2, g_ff_pre=m_g_ff_pre, g_ff_post=m_g_ff_post,
                w_ple_proj=m_w_ple_proj, w_ple_gate=m_w_ple_gate, g_ple_gate=m_g_ple_gate, g_ple_post=m_g_ple_post)
    mom2 = dict(w_in=v_w_in, b_gates=v_b_gates, w_qk_conv=v_w_qk_conv, g_mlstm=v_g_mlstm, w_pool=v_w_pool,
                pool_scale=v_pool_scale, w_out=v_w_out, g_mix_pre=v_g_mix_pre, g_mix_post=v_g_mix_post,
                w_ff1=v_w_ff1, w_ff2=v_w_ff2, g_ff_pre=v_g_ff_pre, g_ff_post=v_g_ff_post,
                w_ple_proj=v_w_ple_proj, w_ple_gate=v_w_ple_gate, g_ple_gate=v_g_ple_gate, g_ple_post=v_g_ple_post)
    order = list(weights)

    t, d = x.shape[1], x.shape[2]
    heads = MLSTM_HEADS
    wm = g_mlstm.shape[1]
    pw = pool_scale.shape[1]
    cg = pw // POOL_GROUPS
    in_cols = 4 * wm + 2 * heads + pw
    in_shard = w_in.shape[2]
    xs, ps, tgt = x[0], p[0, 0], loss_target[0]

    mx, my, mc = lax.axis_index("x"), lax.axis_index("y"), lax.axis_index("c")
    core = jnp.reshape(mc, (1,)).astype(jnp.int32)
    chip = jnp.reshape(2 * mx + my, (1,)).astype(jnp.int32)

    groups = [[w_in[0].T.astype(BF16), w_qk_conv[0], w_pool[0].reshape(POOL_GROUPS * w_pool.shape[2], cg).astype(BF16)],
              [w_out[0].astype(BF16)], [w_ff1[0].astype(BF16)], [w_ff2[0].astype(BF16)],
              [w_ple_gate[0].astype(BF16), w_ple_proj[0].astype(BF16)]]
    started, token = _gather_start(groups, "gather_start")
    bias = _pad_cols(b_gates, LANES)

    xn = _norm_fwd(xs, g_mix_pre, "norm_mix_pre", deps=[token])
    g_in, g_conv, g_pool = _forward_pair(_gather_wait(started[0], xn, "gather_wait_in"), "gather_pair_in")
    in_t_a = g_in.reshape(in_cols, d)
    in_t_b = jnp.concatenate([in_t_a[4 * wm + 2 * heads:], in_t_a[4 * wm:4 * wm + 2 * heads],
                              jnp.zeros((LANES - 2 * heads, d), BF16)], axis=0)
    conv_w = g_conv.transpose(1, 0, 2).reshape(QK_CONV, 2 * wm)
    pool_w = g_pool.reshape(N_DEV, POOL_GROUPS, cg // N_DEV, cg).transpose(1, 0, 2, 3).reshape(POOL_GROUPS, cg, cg)
    proj_a = _mm_nt("proj_qkvo", xn, in_t_a, n_rows=4 * wm)
    proj_b = _mm_nt("proj_pool_gates", xn, in_t_b)
    qc = _conv_fwd(proj_a, conv_w[:, :wm], 0, "q_conv")
    kc = _conv_fwd(proj_a, conv_w[:, wm:], wm, "k_conv")
    gb, gbt = _gates_fwd(proj_b, bias, pw // LANES, heads, "gates")
    gcols = _gate_columns(gb, heads, "gate_columns")
    gbt = gbt[:2 * heads].reshape(2 * heads, 1, t)
    qct, kct = qc.T, kc.T
    (g_out,) = _forward_pair(_gather_wait(started[1], gb, "gather_wait_out"), "gather_pair_out")
    hcat = _mlstm_fwd(qc, kc, kct, proj_a, gcols, gbt, g_mlstm, heads, d, "mlstm")
    hcat = _pool_fwd(proj_b, pool_w, pool_scale, hcat, "pool")
    out_w = g_out.reshape(d, d)
    mix = _mm_nn("mix_out", hcat, out_w)[0]
    h1, hn = _block_fwd(xs, mix, g_mix_post, g_ff_pre, "residual_mix")
    (g_ff1,) = _forward_pair(_gather_wait(started[2], hn, "gather_wait_ff1"), "gather_pair_ff1")
    relu_a, act = _mm_nn_pieces("ff1", hn, g_ff1, out_dtype=[BF16, BF16],
                                epilogue=lambda acc: (jnp.maximum(acc, 0.0), jnp.square(jnp.maximum(acc, 0.0))))
    (g_ff2,) = _forward_pair(_gather_wait(started[3], act, "gather_wait_ff2"), "gather_pair_ff2")
    ff2_w = g_ff2.reshape(-1, d)
    ff = _mm_nn("ff2", act, ff2_w, tn=256, tk=4 * d)[0]
    h2, hg = _block_fwd(h1, ff, g_ff_post, g_ple_gate, "residual_ff")
    g_pgate, g_pproj = _forward_pair(_gather_wait(started[4], hg, "gather_wait_ple"), "gather_pair_ple")
    pgate_w = g_pgate.reshape(d, d)
    z = _mm_nn("ple_gate", hg, pgate_w)[0]
    e = _mm_nn_pieces("ple_proj", ps, g_pproj)[0]
    dh3, dz, de, dg_ple_post, loss_part = _loss_head(h2, z, e, tgt, g_ple_post, "loss_head")
    loss = lax.psum(loss_part[0, 0], MESH_AXES)

    out = {}

    def view(k, a):
        return a[0].T if k == "w_in" else a.reshape(-1, a.shape[-1])

    def pair_begin(pieces, tag):
        keys = list(pieces)
        return keys, _pair_start([pieces[k] for k in keys], "reduce_pair_start_" + tag)

    def chip_begin(group, after, tag):
        keys, begun = group
        mine, got = _pair_wait(begun, after, "reduce_pair_wait_" + tag)
        sums = [_pair_sum(p_, g, core, "pair_sum_" + k) for k, p_, g in zip(keys, mine, got)]
        return keys, _chip_start(sums, "reduce_start_" + tag)

    def reduce_start(pieces, tag):
        keys = list(pieces)
        got = _pair_exchange([pieces[k] for k in keys], "reduce_pair_" + tag)
        sums = [_pair_sum(pieces[k], g, core, "pair_sum_" + k) for k, g in zip(keys, got)]
        return keys, _chip_start(sums, "reduce_start_" + tag)

    def reduce_finish(group, after, tag):
        keys, begun = group
        sums, got = _chip_wait(begun, after, "reduce_wait_" + tag)
        for k, s, g in zip(keys, sums, got):
            res = _adamw_shard(view(k, weights[k]), view(k, mom1[k]), view(k, mom2[k]), s, g, chip, "adamw_" + k)
            out[k] = [r.T[None] if k == "w_in" else r.reshape(weights[k].shape) for r in res]

    token_of = lambda group: group[1]["token"]
    dhg = _mm_nt("d_ple_gate_in", dz, pgate_w)
    gw_pgate = _mm_tn("gw_ple_gate", hg, dz)
    gw_pproj = _mm_tn("gw_ple_proj", ps, de, pieces=N_DEV)
    pair_ple = pair_begin({"w_ple_gate": gw_pgate.reshape(N_DEV, d // N_DEV, d), "w_ple_proj": gw_pproj}, "ple")
    dh2, dff, dg_ff_post, dg_ple_gate = _block_bwd(h2, ff, dh3, dhg, g_ff_post, g_ple_gate, "d_residual_ff",
                                                   deps=[token_of(pair_ple)])
    red_ple = chip_begin(pair_ple, dff, "ple")
    da = _mm_nt("d_ff2_in", dff, ff2_w, out_dtype=BF16, extra=[relu_a],
                epilogue=lambda acc, r: (acc * (2.0 * r.astype(F32)),), deps=[token_of(red_ple)])
    gw_ff2 = _mm_tn("gw_ff2", act, dff)
    pair_ff2 = pair_begin({"w_ff2": gw_ff2.reshape(N_DEV, -1, d)}, "ff2")
    dhn = _mm_nt_pieces("d_ff1_in", da, g_ff1, deps=[token_of(pair_ff2)])
    red_ff2 = chip_begin(pair_ff2, dhn, "ff2")
    reduce_finish(red_ple, token_of(red_ff2), "ple")
    gw_ff1 = _mm_tn("gw_ff1", hn, da, pieces=N_DEV, deps=[token_of(red_ff2)])
    pair_ff1 = pair_begin({"w_ff1": gw_ff1}, "ff1")
    dh1, dmix, dg_mix_post, dg_ff_pre = _block_bwd(h1, mix, dh2, dhn, g_mix_post, g_ff_pre, "d_residual_mix",
                                                   deps=[token_of(pair_ff1)])
    dhcat = _mm_nt("d_mix_in", dmix, out_w)
    red_ff1 = chip_begin(pair_ff1, dhcat, "ff1")
    reduce_finish(red_ff2, token_of(red_ff1), "ff2")
    gw_out = _mm_tn("gw_out", hcat, dmix, deps=[token_of(red_ff1)])
    du, gw_pool, dg_pool_scale = _pool_bwd(proj_b, pool_w, pool_scale, dhcat, "d_pool")
    pair_out = pair_begin({
        "w_out": gw_out.reshape(N_DEV, d // N_DEV, d),
        "w_pool": gw_pool.reshape(POOL_GROUPS, N_DEV, cg // N_DEV, cg).transpose(1, 0, 2, 3).reshape(N_DEV, -1, cg)},
        "out")
    dq, dk, dv, do, dcol, drow, dg_mlstm = _mlstm_bwd(qc, kc, qct, kct, proj_a, gcols, gbt, g_mlstm, dhcat, heads,
                                                      "d_mlstm", token_of(pair_out))
    red_out = chip_begin(pair_out, dq, "out")
    reduce_finish(red_ff1, token_of(red_out), "ff1")
    drow = jnp.pad(drow.reshape(heads, t), ((0, LANES - heads), (0, 0)))
    dgates, dbias = _gates_bwd(proj_b, bias, dcol, drow, pw // LANES, heads, "d_gates")
    dq_pre, gw_conv_q = _conv_bwd(proj_a, conv_w[:, :wm], dq, 0, "d_q_conv")
    dk_pre, gw_conv_k = _conv_bwd(proj_a, conv_w[:, wm:], dk, wm, "d_k_conv")
    gw_conv = jnp.concatenate([gw_conv_q, gw_conv_k], axis=1)
    dproj_a = jnp.concatenate([dq_pre, dk_pre, dv, do], axis=1)
    dproj_b = jnp.concatenate([du, dgates], axis=1)
    gw_a = _mm_tn("gw_in_qkvo", dproj_a, xn)
    gw_b = _mm_tn("gw_in_pool_gates", dproj_b, xn)
    gw_in_t = jnp.concatenate([gw_a, gw_b[pw:pw + 2 * heads], gw_b[:pw]], axis=0)
    red_in = reduce_start({
        "w_in": gw_in_t.reshape(N_DEV, in_shard, d),
        "w_qk_conv": gw_conv.reshape(QK_CONV, N_DEV, -1).transpose(1, 0, 2).astype(BF16)}, "in")
    reduce_finish(red_out, red_in[1]["token"], "out")
    dxn_a = _mm_nn("d_proj_qkvo_in", dproj_a, in_t_a, tn=512, tk=4 * wm, deps=[red_in[1]["token"]])[0]
    dxn = _mm_nn("d_proj_in", dproj_b, in_t_b, extra=[dxn_a], epilogue=lambda acc, prev: (acc + prev,))[0]
    grad_x, dg_mix_pre = _norm_bwd(xs, dxn, dh1, g_mix_pre, "d_norm_mix_pre")

    small = ["b_gates", "g_mlstm", "pool_scale", "g_mix_pre", "g_mix_post", "g_ff_pre", "g_ff_post", "g_ple_gate",
             "g_ple_post"]
    small_grads = dict(b_gates=dbias[:, :2 * heads], g_mlstm=dg_mlstm, pool_scale=dg_pool_scale, g_mix_pre=dg_mix_pre,
                       g_mix_post=dg_mix_post, g_ff_pre=dg_ff_pre, g_ff_post=dg_ff_post, g_ple_gate=dg_ple_gate,
                       g_ple_post=dg_ple_post)
    pack = lambda src: jnp.concatenate([_pad_cols(src[k], d) for k in small], axis=0)
    small_parts = _all_gather([_pad_rows16(pack(small_grads))], "gather_small_grads")[0]
    res = _adamw_replicated(_pad_rows16(pack(weights)), _pad_rows16(pack(mom1)), _pad_rows16(pack(mom2)), small_parts,
                            "adamw_replicated")
    for i, k in enumerate(small):
        width = weights[k].shape[1]
        out[k] = [r[i:i + 1, :width] for r in res]
    reduce_finish(red_in, res[0], "in")

    return (loss, grad_x[None], *[out[k][0] for k in order], *[out[k][1] for k in order],
            *[out[k][2] for k in order], *[out[k][3] for k in order])


def _pad_rows16(a):
    return jnp.pad(a, ((0, 16 - a.shape[0]), (0, 0)))
```

```python
import functools

import jax
import jax.numpy as jnp
from jax import lax
from jax.experimental import pallas as pl
from jax.experimental.pallas import tpu as pltpu

F32 = jnp.float32
BF16 = jnp.bfloat16
EPS = 1e-6
N_DEV = 8
N_CHIP = 4
LANES = 128
VMEM_LIMIT = 56 * 1024 * 1024
MLSTM_HEADS = 8
POOL_GROUPS = 4
QK_CONV = 4
CHUNK = 128
ROW_TILE = 256
ADAM_LR, ADAM_B1, ADAM_B2, ADAM_EPS, ADAM_WD, ADAM_STEP = 0.001, 0.9, 0.999, 1e-08, 0.01, 10
MESH_AXES = ("x", "y", "c")
MESH = pl.DeviceIdType.MESH
HBM_SPEC = pl.BlockSpec(memory_space=pltpu.HBM)
SEM_SPEC = pl.BlockSpec(memory_space=pltpu.SEMAPHORE)
DATAFLOW = pltpu.SideEffectType.DATAFLOW_SIDE_EFFECTING


def _params(*sem):
    if sem:
        return pltpu.CompilerParams(dimension_semantics=sem, vmem_limit_bytes=VMEM_LIMIT)
    return pltpu.CompilerParams(vmem_limit_bytes=VMEM_LIMIT)


def _tile(n, pref):
    if n <= pref:
        return n
    t = (pref // LANES) * LANES
    while t >= LANES:
        if n % t == 0:
            return t
        t -= LANES
    return n


def _all_gather(arrs, name):
    n = len(arrs)

    def body(*refs):
        ins, outs = refs[:n], refs[n:2 * n]
        send, recv, loc = refs[2 * n:]
        x, y, c = lax.axis_index("x"), lax.axis_index("y"), lax.axis_index("c")
        me = (x, y, c)
        sibling = (x, y, 1 - c)
        chips = [(1 - x, y), (x, 1 - y), (1 - x, 1 - y)]

        def row(dev):
            return 4 * dev[0] + 2 * dev[1] + dev[2]

        def copy(w, k, block_of, to, src=None):
            dst = outs[w].at[row(block_of)]
            return pltpu.make_async_remote_copy(
                src_ref=dst if src is None else src, dst_ref=dst,
                send_sem=send.at[7 * w + k], recv_sem=recv.at[7 * w + k],
                device_id=to, device_id_type=MESH)

        started = []
        local = []
        for w in range(n):
            cp = pltpu.make_async_copy(ins[w], outs[w].at[row(me)], loc.at[w])
            cp.start()
            local.append(cp)
            first = [copy(w, 0, me, sibling, src=ins[w])]
            first += [copy(w, 1 + j, me, (*chip, c), src=ins[w]) for j, chip in enumerate(chips)]
            for cp in first:
                cp.start()
            started += first
        for j, chip in enumerate(chips):
            for w in range(n):
                copy(w, 1 + j, (*chip, c), me).wait_recv()
                fwd = copy(w, 4 + j, (*chip, c), sibling)
                fwd.start()
                started.append(fwd)
        for w in range(n):
            copy(w, 0, sibling, me).wait_recv()
            for j, chip in enumerate(chips):
                copy(w, 4 + j, (*chip, 1 - c), me).wait_recv()
        for cp in started:
            cp.wait_send()
        for cp in local:
            cp.wait()

    outs = pl.pallas_call(
        body, name=name,
        out_shape=[jax.ShapeDtypeStruct((N_DEV,) + a.shape, a.dtype) for a in arrs],
        in_specs=[HBM_SPEC] * n, out_specs=[HBM_SPEC] * n,
        scratch_shapes=[pltpu.SemaphoreType.DMA((7 * n,)), pltpu.SemaphoreType.DMA((7 * n,)),
                        pltpu.SemaphoreType.DMA((n,))],
    )(*arrs)
    return list(outs)


def _pair_exchange(pieces, name):
    n = len(pieces)

    def body(*refs):
        ins, outs = refs[:n], refs[n:2 * n]
        send, recv = refs[2 * n:]
        x, y, c = lax.axis_index("x"), lax.axis_index("y"), lax.axis_index("c")
        sibling = (x, y, 1 - c)

        def copy(w, chip, src_core):
            return pltpu.make_async_remote_copy(
                src_ref=ins[w].at[2 * chip + src_core], dst_ref=outs[w].at[chip],
                send_sem=send.at[N_CHIP * w + chip], recv_sem=recv.at[N_CHIP * w + chip],
                device_id=sibling, device_id_type=MESH)

        cps = [copy(w, chip, 1 - c) for w in range(n) for chip in range(N_CHIP)]
        for cp in cps:
            cp.start()
        for cp in cps:
            cp.wait_recv()
        for cp in cps:
            cp.wait_send()

    outs = pl.pallas_call(
        body, name=name,
        out_shape=[jax.ShapeDtypeStruct((N_CHIP,) + a.shape[1:], a.dtype) for a in pieces],
        in_specs=[HBM_SPEC] * n, out_specs=[HBM_SPEC] * n,
        scratch_shapes=[pltpu.SemaphoreType.DMA((N_CHIP * n,)), pltpu.SemaphoreType.DMA((N_CHIP * n,))],
    )(*pieces)
    return list(outs)


def _hbm(a):
    return pltpu.with_memory_space_constraint(a, pltpu.HBM)


def _mesh_place():
    x, y, c = lax.axis_index("x"), lax.axis_index("y"), lax.axis_index("c")
    return x, y, c, [(1 - x, y), (x, 1 - y), (1 - x, 1 - y)]


def _chip_copies(srcs, lands, send, recv):
    x, y, c, chips = _mesh_place()
    return [pltpu.make_async_remote_copy(
        src_ref=srcs[w].at[2 * chip[0] + chip[1]], dst_ref=lands[w].at[k],
        send_sem=send.at[3 * w + k], recv_sem=recv.at[3 * w + k],
        device_id=(*chip, c), device_id_type=MESH)
        for k, chip in enumerate(chips) for w in range(len(srcs))]


def _chip_start(sums, name):
    return _split_start(_chip_copies, sums, [lax.empty((3,) + a.shape[1:], a.dtype) for a in sums], 3, name)


def _chip_wait(started, after, name):
    return _split_wait(_chip_copies, started, after, name)


def _pair_start(pieces, name):
    return _split_start(_pair_copies, pieces, [lax.empty((N_CHIP,) + a.shape[1:], a.dtype) for a in pieces], N_CHIP, name)


def _pair_wait(started, after, name):
    return _split_wait(_pair_copies, started, after, name)


def _pair_copies(srcs, lands, send, recv):
    x, y, c, _ = _mesh_place()
    return [pltpu.make_async_remote_copy(
        src_ref=srcs[w].at[2 * ch + 1 - c], dst_ref=lands[w].at[ch],
        send_sem=send.at[N_CHIP * w + ch], recv_sem=recv.at[N_CHIP * w + ch],
        device_id=(x, y, 1 - c), device_id_type=MESH)
        for w in range(len(srcs)) for ch in range(N_CHIP)]


def _split_start(copies_of, srcs, lands, per_array, name):
    n = len(srcs)

    def body(*refs):
        for cp in copies_of(refs[:n], refs[n:2 * n], refs[2 * n], refs[2 * n + 1]):
            cp.start()
        refs[-1][...] = jnp.zeros_like(refs[-1])

    res = pl.pallas_call(
        body, name=name,
        out_shape=[pltpu.SemaphoreType.DMA((per_array * n,)), pltpu.SemaphoreType.DMA((per_array * n,))]
        + [pltpu.HBM(a.shape, a.dtype) for a in srcs + lands] + [jax.ShapeDtypeStruct((8, LANES), F32)],
        in_specs=[HBM_SPEC] * (2 * n),
        out_specs=[SEM_SPEC, SEM_SPEC] + [HBM_SPEC] * (2 * n) + [pl.BlockSpec(memory_space=pltpu.VMEM)],
        input_output_aliases={i: 2 + i for i in range(2 * n)},
        compiler_params=pltpu.CompilerParams(has_side_effects=DATAFLOW),
    )(*[_hbm(a) for a in srcs], *[_hbm(a) for a in lands])
    return dict(send=res[0], recv=res[1], srcs=list(res[2:2 + n]), lands=list(res[2 + n:2 + 2 * n]), token=res[-1])


def _split_wait(copies_of, started, after, name):
    n = len(started["srcs"])

    def body(*refs):
        for cp in copies_of(refs[:n], refs[n:2 * n], refs[2 * n], refs[2 * n + 1]):
            cp.wait_send()
            cp.wait_recv()

    res = pl.pallas_call(
        body, name=name,
        out_shape=[pltpu.HBM(a.shape, a.dtype) for a in started["srcs"] + started["lands"]],
        in_specs=[HBM_SPEC] * (2 * n) + [SEM_SPEC, SEM_SPEC, ANY_SPEC],
        out_specs=[HBM_SPEC] * (2 * n),
        input_output_aliases={i: i for i in range(2 * n)},
        compiler_params=pltpu.CompilerParams(has_side_effects=DATAFLOW),
    )(*started["srcs"], *started["lands"], started["send"], started["recv"], after)
    return list(res[:n]), list(res[n:])


def _gather_copies(srcs, lands, send, recv):
    x, y, c, chips = _mesh_place()
    me = 4 * x + 2 * y + c
    targets = [(x, y, 1 - c)] + [(*chip, c) for chip in chips]
    return [pltpu.make_async_remote_copy(
        src_ref=srcs[w], dst_ref=lands[w].at[me], send_sem=send.at[4 * w + k], recv_sem=recv.at[4 * w + k],
        device_id=to, device_id_type=MESH)
        for w in range(len(srcs)) for k, to in enumerate(targets)]


def _gather_start(groups, name):
    sizes = [len(g) for g in groups]
    flat = [a for g in groups for a in g]
    n, ng = len(flat), len(groups)
    me = 4 * lax.axis_index("x") + 2 * lax.axis_index("y") + lax.axis_index("c")
    lands = [lax.dynamic_update_slice(lax.empty((N_DEV,) + a.shape, a.dtype), a[None], (me,) + (0,) * a.ndim)
             for a in flat]

    def body(*refs):
        srcs, zones = refs[:n], refs[n:2 * n]
        sems = refs[2 * n:2 * n + 2 * ng]
        token = refs[-1]
        lo = 0
        for gi, size in enumerate(sizes):
            for cp in _gather_copies(srcs[lo:lo + size], zones[lo:lo + size], sems[2 * gi], sems[2 * gi + 1]):
                cp.start()
            lo += size
        token[...] = jnp.zeros_like(token)

    sem_shapes = []
    for size in sizes:
        sem_shapes += [pltpu.SemaphoreType.DMA((4 * size,))] * 2
    res = pl.pallas_call(
        body, name=name,
        out_shape=sem_shapes + [pltpu.HBM(a.shape, a.dtype) for a in flat + lands] + [jax.ShapeDtypeStruct((8, LANES), F32)],
        in_specs=[HBM_SPEC] * (2 * n),
        out_specs=[SEM_SPEC] * (2 * ng) + [HBM_SPEC] * (2 * n) + [pl.BlockSpec(memory_space=pltpu.VMEM)],
        input_output_aliases={i: 2 * ng + i for i in range(2 * n)},
        compiler_params=pltpu.CompilerParams(has_side_effects=DATAFLOW),
    )(*[_hbm(a) for a in flat], *[_hbm(a) for a in lands])
    out, lo = [], 0
    for gi, size in enumerate(sizes):
        out.append(dict(send=res[2 * gi], recv=res[2 * gi + 1],
                        srcs=list(res[2 * ng + lo:2 * ng + lo + size]),
                        lands=list(res[2 * ng + n + lo:2 * ng + n + lo + size])))
        lo += size
    return out, res[-1]


def _gather_wait(started, after, name):
    n = len(started["srcs"])
    after = list(after) if isinstance(after, (list, tuple)) else [after]

    def body(*refs):
        srcs, zones = refs[:n], refs[n:2 * n]
        send, recv = refs[2 * n], refs[2 * n + 1]
        for cp in _gather_copies(srcs, zones, send, recv):
            cp.wait_send()
            cp.wait_recv()

    res = pl.pallas_call(
        body, name=name,
        out_shape=[pltpu.HBM(a.shape, a.dtype) for a in started["srcs"] + started["lands"]],
        in_specs=[HBM_SPEC] * (2 * n) + [SEM_SPEC, SEM_SPEC] + [ANY_SPEC] * len(after),
        out_specs=[HBM_SPEC] * (2 * n),
        input_output_aliases={i: i for i in range(2 * n)},
        compiler_params=pltpu.CompilerParams(has_side_effects=DATAFLOW),
    )(*started["srcs"], *started["lands"], started["send"], started["recv"], *after)
    return list(res[n:])


def _forward_pair(lands, name):
    n = len(lands)

    def body(*refs):
        zones = refs[:n]
        send, recv = refs[2 * n:]
        x, y, c, chips = _mesh_place()

        def copy(w, j, core):
            rows = zones[w].at[4 * chips[j][0] + 2 * chips[j][1] + core]
            return pltpu.make_async_remote_copy(
                src_ref=rows, dst_ref=rows, send_sem=send.at[3 * w + j], recv_sem=recv.at[3 * w + j],
                device_id=(x, y, 1 - c), device_id_type=MESH)

        sends = [copy(w, j, c) for w in range(n) for j in range(3)]
        for cp in sends:
            cp.start()
        for w in range(n):
            for j in range(3):
                copy(w, j, 1 - c).wait_recv()
        for cp in sends:
            cp.wait_send()

    return list(pl.pallas_call(
        body, name=name,
        out_shape=[jax.ShapeDtypeStruct(a.shape, a.dtype) for a in lands],
        in_specs=[HBM_SPEC] * n, out_specs=[HBM_SPEC] * n,
        input_output_aliases={i: i for i in range(n)},
        scratch_shapes=[pltpu.SemaphoreType.DMA((3 * n,)), pltpu.SemaphoreType.DMA((3 * n,))],
    )(*lands))


def _pair_sum(pieces, got, core, name):
    _, rows, cols = pieces.shape
    tr, tc = _tile_2d(rows, cols)

    def body(core_ref, mine_ref, got_ref, out_ref):
        del core_ref
        out_ref[...] = (mine_ref[...].astype(F32) + got_ref[...].astype(F32)).astype(out_ref.dtype)

    return pl.pallas_call(
        body, name=name,
        out_shape=jax.ShapeDtypeStruct((N_CHIP, rows, cols), pieces.dtype),
        grid_spec=pltpu.PrefetchScalarGridSpec(
            num_scalar_prefetch=1, grid=(N_CHIP, rows // tr, cols // tc),
            in_specs=[pl.BlockSpec((None, tr, tc), lambda ch, i, j, core: (2 * ch + core[0], i, j)),
                      pl.BlockSpec((None, tr, tc), lambda ch, i, j, core: (ch, i, j))],
            out_specs=pl.BlockSpec((None, tr, tc), lambda ch, i, j, core: (ch, i, j))),
        compiler_params=_params("parallel", "parallel", "parallel"),
    )(core, pieces, got)


def _tile_2d(rows, cols, budget=2 * 1024 * 1024):
    want = max(16, budget // (4 * cols))
    if rows <= want:
        return rows, cols
    t = (want // 16) * 16
    while t >= 16:
        if rows % t == 0:
            return t, cols
        t -= 16
    return rows, _tile(cols, max(LANES, budget // (4 * rows)))


def _adamw_math(w, g, m, v):
    m = ADAM_B1 * m + (1.0 - ADAM_B1) * g
    v = ADAM_B2 * v + (1.0 - ADAM_B2) * (g * g)
    m_hat = m / (1.0 - ADAM_B1 ** ADAM_STEP)
    v_hat = v / (1.0 - ADAM_B2 ** ADAM_STEP)
    delta = -ADAM_LR * (m_hat / (jnp.sqrt(v_hat) + ADAM_EPS) + ADAM_WD * w)
    return delta, m, v


def _adamw_shard(w, m, v, sums, got, chip, name):
    rows, cols = w.shape
    tr, tc = _tile_2d(rows, cols, budget=1024 * 1024)

    def body(chip_ref, w_ref, m_ref, v_ref, own_ref, got_ref, g_out, d_out, m_out, v_out):
        del chip_ref
        g = own_ref[...].astype(F32)
        for k in range(3):
            g = g + got_ref[k].astype(F32)
        delta, m_new, v_new = _adamw_math(w_ref[...], g, m_ref[...], v_ref[...])
        g_out[...] = g
        d_out[...] = delta
        m_out[...] = m_new
        v_out[...] = v_new

    blk = pl.BlockSpec((tr, tc), lambda i, j, chip: (i, j))
    return pl.pallas_call(
        body, name=name,
        out_shape=[jax.ShapeDtypeStruct((rows, cols), F32)] * 4,
        grid_spec=pltpu.PrefetchScalarGridSpec(
            num_scalar_prefetch=1, grid=(rows // tr, cols // tc),
            in_specs=[blk, blk, blk,
                      pl.BlockSpec((None, tr, tc), lambda i, j, chip: (chip[0], i, j)),
                      pl.BlockSpec((3, tr, tc), lambda i, j, chip: (0, i, j))],
            out_specs=[blk] * 4),
        compiler_params=_params("parallel", "parallel"),
    )(chip, w, m, v, sums, got)


def _adamw_replicated(w, m, v, parts, name):
    rows, cols = w.shape

    def body(w_ref, m_ref, v_ref, parts_ref, g_out, d_out, m_out, v_out):
        g = parts_ref[0]
        for d in range(1, N_DEV):
            g = g + parts_ref[d]
        delta, m_new, v_new = _adamw_math(w_ref[...], g, m_ref[...], v_ref[...])
        g_out[...] = g
        d_out[...] = delta
        m_out[...] = m_new
        v_out[...] = v_new

    return pl.pallas_call(
        body, name=name,
        out_shape=[jax.ShapeDtypeStruct((rows, cols), F32)] * 4,
        compiler_params=_params(),
    )(w, m, v, parts)


NN = ((1,), (0,))
NT = ((1,), (1,))
TN = ((0,), (0,))


ANY_SPEC = pl.BlockSpec(memory_space=pl.ANY)


def _mm(name, a, b, *, dims, grid, a_spec, b_spec, outs, extra=(), extra_specs=(), epilogue=None, acc_shape=None,
        deps=(), b_pieces=0):
    nk = grid[2]
    n_extra, n_out = len(extra), len(outs)
    first_out = 2 + n_extra + len(deps)

    def body(*refs):
        a_ref, b_ref = refs[0], refs[1]
        ex = refs[2:2 + n_extra]
        out_refs = refs[first_out:first_out + n_out]
        dot = lambda lhs, rhs: lax.dot_general(lhs.astype(BF16), rhs.astype(BF16), (dims, ((), ())),
                                               preferred_element_type=F32)
        if b_pieces:
            ks = b_ref.shape[-1]
            part = dot(a_ref[:, 0:ks], b_ref[0])
            for piece in range(1, b_pieces):
                part = part + dot(a_ref[:, piece * ks:(piece + 1) * ks], b_ref[piece])
        else:
            part = dot(a_ref[...], b_ref[...])

        def finish(acc):
            res = epilogue(acc, *[e[...] for e in ex]) if epilogue else (acc,)
            for val, o in zip(res, out_refs):
                o[...] = val.astype(o.dtype)

        if nk == 1:
            finish(part)
        else:
            acc_ref = refs[-1]
            k = pl.program_id(2)

            @pl.when(k == 0)
            def _():
                acc_ref[...] = part

            @pl.when(k > 0)
            def _():
                acc_ref[...] += part

            @pl.when(k == nk - 1)
            def _():
                finish(acc_ref[...])

    res = pl.pallas_call(
        body, name=name,
        out_shape=[jax.ShapeDtypeStruct(s, d) for s, d, _ in outs],
        grid=grid,
        in_specs=[a_spec, b_spec, *extra_specs] + [ANY_SPEC] * len(deps),
        out_specs=[sp for _, _, sp in outs],
        scratch_shapes=[pltpu.VMEM(acc_shape, F32)] if nk > 1 else [],
        compiler_params=_params("parallel", "parallel", "arbitrary"),
    )(a, b, *extra, *deps)
    return list(res)


def _mm_nn(name, a, b, out_dtype=F32, tm=1024, tn=1024, tk=2048, epilogue=None, n_out=1, extra=(), deps=()):
    m, kd = a.shape
    n = b.shape[1]
    tm, tn, tk = _tile(m, tm), _tile(n, tn), _tile(kd, tk)
    dts = out_dtype if isinstance(out_dtype, (list, tuple)) else [out_dtype] * n_out
    o_spec = pl.BlockSpec((tm, tn), lambda i, j, k: (i, j))
    return _mm(name, a, b, dims=NN, grid=(m // tm, n // tn, kd // tk),
               a_spec=pl.BlockSpec((tm, tk), lambda i, j, k: (i, k)),
               b_spec=pl.BlockSpec((tk, tn), lambda i, j, k: (k, j)),
               outs=[((m, n), dt, o_spec) for dt in dts], extra=extra, extra_specs=[o_spec] * len(extra),
               epilogue=epilogue, acc_shape=(tm, tn), deps=deps)


def _mm_nn_pieces(name, a, b, out_dtype=F32, tm=1024, tk=2048, epilogue=None, n_out=1):
    m, kd = a.shape
    npc, _, ns = b.shape
    tm, tk = _tile(m, tm), _tile(kd, tk)
    tn = _tile(ns, 1024)
    r = ns // tn
    dts = out_dtype if isinstance(out_dtype, (list, tuple)) else [out_dtype] * n_out
    o_spec = pl.BlockSpec((tm, tn), lambda i, j, k: (i, j))
    return _mm(name, a, b, dims=NN, grid=(m // tm, npc * r, kd // tk),
               a_spec=pl.BlockSpec((tm, tk), lambda i, j, k: (i, k)),
               b_spec=pl.BlockSpec((None, tk, tn), lambda i, j, k: (j // r, k, j % r)),
               outs=[((m, npc * ns), dt, o_spec) for dt in dts], epilogue=epilogue, acc_shape=(tm, tn))


def _mm_nt(name, a, b, out_dtype=F32, tm=1024, tn=1024, tk=2048, epilogue=None, extra=(), n_rows=None, deps=()):
    m, kd = a.shape
    n = n_rows or b.shape[0]
    tm, tn, tk = _tile(m, tm), _tile(n, tn), _tile(kd, tk)
    o_spec = pl.BlockSpec((tm, tn), lambda i, j, k: (i, j))
    return _mm(name, a, b, dims=NT, grid=(m // tm, n // tn, kd // tk),
               a_spec=pl.BlockSpec((tm, tk), lambda i, j, k: (i, k)),
               b_spec=pl.BlockSpec((tn, tk), lambda i, j, k: (j, k)),
               outs=[((m, n), out_dtype, o_spec)], extra=extra, extra_specs=[o_spec] * len(extra),
               epilogue=epilogue, acc_shape=(tm, tn), deps=deps)[0]


def _mm_nt_pieces(name, a, b, out_dtype=F32, tm=1024, tn=256, deps=()):
    m = a.shape[0]
    npc, n, ks = b.shape
    tm, tn = _tile(m, tm), _tile(n, tn)
    return _mm(name, a, b, dims=NT, grid=(m // tm, n // tn, 1),
               a_spec=pl.BlockSpec((tm, npc * ks), lambda i, j, k: (i, 0)),
               b_spec=pl.BlockSpec((npc, tn, ks), lambda i, j, k: (0, j, 0)),
               outs=[((m, n), out_dtype, pl.BlockSpec((tm, tn), lambda i, j, k: (i, j)))],
               deps=deps, b_pieces=npc)[0]


def _mm_tn(name, a, b, out_dtype=BF16, tm=1024, tn=1024, pieces=0, deps=()):
    t, m = a.shape
    n = b.shape[1]
    tm = _tile(m, tm)
    if pieces:
        ns = n // pieces
        tn = _tile(ns, tn)
        r = ns // tn
        out = ((pieces, m, ns), out_dtype, pl.BlockSpec((None, tm, tn), lambda i, j, k: (j // r, i, j % r)))
    else:
        tn = _tile(n, tn)
        out = ((m, n), out_dtype, pl.BlockSpec((tm, tn), lambda i, j, k: (i, j)))
    return _mm(name, a, b, dims=TN, grid=(m // tm, n // tn, 1),
               a_spec=pl.BlockSpec((t, tm), lambda i, j, k: (0, i)),
               b_spec=pl.BlockSpec((t, tn), lambda i, j, k: (0, j)),
               outs=[out], deps=deps)[0]


def _rms_f(x, g):
    r = lax.rsqrt(jnp.mean(x * x, axis=-1, keepdims=True) + EPS)
    xh = x * r
    return xh * g, xh, r


def _rms_b(dy, g, xh, r):
    dxh = dy * g
    dx = r * (dxh - xh * jnp.mean(dxh * xh, axis=-1, keepdims=True))
    dg = jnp.sum(dy * xh, axis=0, keepdims=True)
    return dx, dg


def _rowwise(name, fn, row_ins, vec_ins, row_outs, vec_outs, deps=()):
    t = row_ins[0].shape[0]
    tr = min(ROW_TILE, t)
    nr, nv, no = len(row_ins), len(vec_ins), len(row_outs)
    first_out = nr + nv + len(deps)

    def body(*refs):
        ri, vi = refs[:nr], refs[nr:nr + nv]
        ro, vo = refs[first_out:first_out + no], refs[first_out + no:]
        routs, vouts = fn(*[r[...] for r in ri], *[v[...] for v in vi])
        for o, val in zip(ro, routs):
            o[...] = val.astype(o.dtype)
        i = pl.program_id(0)
        for o, val in zip(vo, vouts):
            @pl.when(i == 0)
            def _():
                o[...] = val

            @pl.when(i > 0)
            def _():
                o[...] += val

    res = pl.pallas_call(
        body, name=name,
        out_shape=[jax.ShapeDtypeStruct((t, w), d) for w, d in row_outs]
        + [jax.ShapeDtypeStruct((1, w), F32) for w in vec_outs],
        grid=(t // tr,),
        in_specs=[pl.BlockSpec((tr, a.shape[1]), lambda i: (i, 0)) for a in row_ins]
        + [pl.BlockSpec((1, a.shape[1]), lambda i: (0, 0)) for a in vec_ins] + [ANY_SPEC] * len(deps),
        out_specs=[pl.BlockSpec((tr, w), lambda i: (i, 0)) for w, _ in row_outs]
        + [pl.BlockSpec((1, w), lambda i: (0, 0)) for w in vec_outs],
        compiler_params=_params("arbitrary"),
    )(*row_ins, *vec_ins, *deps)
    return list(res)


def _norm_fwd(x, g, name, deps=()):
    d = x.shape[1]
    return _rowwise(name, lambda x, g: ((_rms_f(x, g)[0],), ()), [x], [g], [(d, BF16)], [], deps=deps)[0]


def _norm_bwd(x, dxn, dres, g, name):
    d = x.shape[1]

    def fn(x, dxn, dres, g):
        _, xh, r = _rms_f(x, g)
        dx, dg = _rms_b(dxn, g, xh, r)
        return (dres + dx,), (dg,)

    return _rowwise(name, fn, [x, dxn, dres], [g], [(d, F32)], [d])


def _block_fwd(h_prev, y, g_y, g_n, name):
    d = h_prev.shape[1]

    def fn(h_prev, y, g_y, g_n):
        h = h_prev + _rms_f(y, g_y)[0]
        return (h, _rms_f(h, g_n)[0]), ()

    return _rowwise(name, fn, [h_prev, y], [g_y, g_n], [(d, F32), (d, BF16)], [])


def _block_bwd(h, y, d_up, d_n, g_y, g_n, name, deps=()):
    d = h.shape[1]

    def fn(h, y, d_up, d_n, g_y, g_n):
        _, hh, hr = _rms_f(h, g_n)
        dh_n, dg_n = _rms_b(d_n, g_n, hh, hr)
        dh = d_up + dh_n
        _, yh, yr = _rms_f(y, g_y)
        dy, dg_y = _rms_b(dh, g_y, yh, yr)
        return (dh, dy), (dg_y, dg_n)

    return _rowwise(name, fn, [h, y, d_up, d_n], [g_y, g_n], [(d, F32), (d, BF16)], [d, d], deps=deps)


def _loss_head(h2, z, e, target, g, name):
    d = h2.shape[1]

    def fn(h2, z, e, target, g):
        gate = jax.nn.sigmoid(z)
        y, xh, r = _rms_f(e * gate, g)
        diff = h2 + y - target
        loss = 0.5 * jnp.sum(jnp.mean(diff * diff, axis=-1, keepdims=True), axis=0, keepdims=True)
        dh3 = diff / d
        deg, dg = _rms_b(dh3, g, xh, r)
        dz = deg * e * gate * (1.0 - gate)
        de = deg * gate
        return (dh3, dz, de), (dg, jnp.broadcast_to(loss, (1, LANES)))

    return _rowwise(name, fn, [h2, z, e, target], [g], [(d, F32), (d, BF16), (d, BF16)], [d, LANES])


def _shift_down(x, s, rows):
    return jnp.where(rows >= s, pltpu.roll(x, s, axis=0), 0.0)


def _shift_up(x, s, rows):
    t = x.shape[0]
    return jnp.where(rows < t - s, pltpu.roll(x, t - s, axis=0), 0.0)


def _conv_pre(x, w, rows):
    pre = x * w[QK_CONV - 1:QK_CONV, :]
    for s in range(1, QK_CONV):
        pre = pre + _shift_down(x, s, rows) * w[QK_CONV - 1 - s:QK_CONV - s, :]
    return pre


def _conv_fwd(proj_a, w, start, name):
    t = proj_a.shape[0]
    width = w.shape[1]
    tc = _tile(width, 256)
    off = start // tc

    def body(x_ref, w_ref, y_ref):
        x = x_ref[...]
        rows = lax.broadcasted_iota(jnp.int32, x.shape, 0)
        pre = _conv_pre(x, w_ref[...], rows)
        y_ref[...] = pre * jax.nn.sigmoid(pre)

    return pl.pallas_call(
        body, name=name,
        out_shape=jax.ShapeDtypeStruct((t, width), F32),
        grid=(width // tc,),
        in_specs=[pl.BlockSpec((t, tc), lambda j: (0, off + j)), pl.BlockSpec((QK_CONV, tc), lambda j: (0, j))],
        out_specs=pl.BlockSpec((t, tc), lambda j: (0, j)),
        compiler_params=_params("parallel"),
    )(proj_a, w)


def _conv_bwd(proj_a, w, dy, start, name):
    t = proj_a.shape[0]
    width = w.shape[1]
    tc = _tile(width, 256)
    off = start // tc

    def body(x_ref, w_ref, dy_ref, dx_ref, dw_ref):
        x, w = x_ref[...], w_ref[...]
        rows = lax.broadcasted_iota(jnp.int32, x.shape, 0)
        pre = _conv_pre(x, w, rows)
        sig = jax.nn.sigmoid(pre)
        dpre = dy_ref[...] * (sig * (1.0 + pre * (1.0 - sig)))
        dx = dpre * w[QK_CONV - 1:QK_CONV, :]
        dws = [jnp.sum(dpre * x, axis=0, keepdims=True)]
        for s in range(1, QK_CONV):
            dx = dx + _shift_up(dpre, s, rows) * w[QK_CONV - 1 - s:QK_CONV - s, :]
            dws.append(jnp.sum(dpre * _shift_down(x, s, rows), axis=0, keepdims=True))
        dx_ref[...] = dx.astype(dx_ref.dtype)
        for s in range(QK_CONV):
            dw_ref[QK_CONV - 1 - s:QK_CONV - s, :] = dws[s]

    return pl.pallas_call(
        body, name=name,
        out_shape=[jax.ShapeDtypeStruct((t, width), BF16), jax.ShapeDtypeStruct((QK_CONV, width), F32)],
        grid=(width // tc,),
        in_specs=[pl.BlockSpec((t, tc), lambda j: (0, off + j)), pl.BlockSpec((QK_CONV, tc), lambda j: (0, j)),
                  pl.BlockSpec((t, tc), lambda j: (0, j))],
        out_specs=[pl.BlockSpec((t, tc), lambda j: (0, j)), pl.BlockSpec((QK_CONV, tc), lambda j: (0, j))],
        compiler_params=_params("parallel"),
    )(proj_a, w, dy)


def _dot(a, b, dims):
    return lax.dot_general(a.astype(BF16), b.astype(BF16), (dims, ((), ())), preferred_element_type=F32)


def _dot_split(ones, x):
    hi = x.astype(BF16)
    rest = x - hi.astype(F32)
    mid = rest.astype(BF16)
    lo = (rest - mid.astype(F32)).astype(BF16)
    dot = lambda part: lax.dot_general(ones, part, (NN, ((), ())), preferred_element_type=F32)
    return dot(hi) + dot(mid) + dot(lo)


def _chunk_iotas():
    rows = lax.broadcasted_iota(jnp.int32, (CHUNK, CHUNK), 0)
    cols = lax.broadcasted_iota(jnp.int32, (CHUNK, CHUNK), 1)
    return rows, cols


def _gates_fwd(proj_b, bias, col_block, heads, name):
    t = proj_b.shape[0]

    def body(x_ref, b_ref, gb_ref, gbt_ref):
        rows, cols = _chunk_iotas()
        tri = (cols <= rows).astype(BF16)
        for ci in range(t // CHUNK):
            sl = slice(ci * CHUNK, (ci + 1) * CHUNK)
            pre = x_ref[sl, :] + b_ref[...]
            log_f = jnp.minimum(pre, 0.0) - jnp.log(1.0 + jnp.exp(-jnp.abs(pre)))
            log_f = jnp.where((cols >= heads) & (cols < 2 * heads), log_f, 0.0)
            tile = jnp.where(cols < heads, pre, _dot_split(tri, log_f))
            gb_ref[sl, :] = tile
            gbt_ref[:, sl] = tile.T

    return pl.pallas_call(
        body, name=name,
        out_shape=[jax.ShapeDtypeStruct((t, LANES), F32), jax.ShapeDtypeStruct((LANES, t), F32)],
        grid=(1,),
        in_specs=[pl.BlockSpec((t, LANES), lambda i: (0, col_block)), pl.BlockSpec((1, LANES), lambda i: (0, 0))],
        out_specs=[pl.BlockSpec((t, LANES), lambda i: (0, 0)), pl.BlockSpec((LANES, t), lambda i: (0, 0))],
        compiler_params=_params("arbitrary"),
    )(proj_b, bias)


def _gate_columns(gb, heads, name):
    t = gb.shape[0]

    def body(gb_ref, out_ref):
        g = gb_ref[...]
        lane = lax.broadcasted_iota(jnp.int32, g.shape, 1)
        column = jnp.sum(jnp.where(lane == pl.program_id(0), g, 0.0), axis=-1, keepdims=True)
        out_ref[...] = jnp.broadcast_to(column, out_ref.shape)

    return pl.pallas_call(
        body, name=name,
        out_shape=jax.ShapeDtypeStruct((2 * heads, t, LANES), F32),
        grid=(2 * heads,),
        in_specs=[pl.BlockSpec((t, LANES), lambda h: (0, 0))],
        out_specs=pl.BlockSpec((None, t, LANES), lambda h: (h, 0, 0)),
        compiler_params=_params("parallel"),
    )(gb)


def _gates_bwd(proj_b, bias, dcol, drow, col_block, heads, name):
    t = proj_b.shape[0]

    def body(x_ref, b_ref, dc_ref, dr_ref, dx_ref, db_ref):
        rows, cols = _chunk_iotas()
        later = (cols >= rows).astype(BF16)
        total = jnp.zeros((1, LANES), F32)
        for ci in range(t // CHUNK):
            sl = slice(ci * CHUNK, (ci + 1) * CHUNK)
            by_row = dr_ref[:, sl].T
            d = dc_ref[sl, :] + jnp.where(cols < heads, by_row,
                                          jnp.where(cols < 2 * heads, -pltpu.roll(by_row, heads, axis=1), 0.0))
            d_log_f = _dot_split(later, jnp.where(cols >= heads, d, 0.0))
            pre = x_ref[sl, :] + b_ref[...]
            dx = jnp.where(cols < heads, d, jnp.where(cols < 2 * heads, d_log_f * jax.nn.sigmoid(-pre), 0.0))
            dx_ref[sl, :] = dx.astype(dx_ref.dtype)
            total = total + jnp.sum(dx, axis=0, keepdims=True)
        db_ref[...] = total

    return pl.pallas_call(
        body, name=name,
        out_shape=[jax.ShapeDtypeStruct((t, LANES), BF16), jax.ShapeDtypeStruct((1, LANES), F32)],
        grid=(1,),
        in_specs=[pl.BlockSpec((t, LANES), lambda i: (0, col_block)), pl.BlockSpec((1, LANES), lambda i: (0, 0)),
                  pl.BlockSpec((t, LANES), lambda i: (0, 0)), pl.BlockSpec((LANES, t), lambda i: (0, 0))],
        out_specs=[pl.BlockSpec((t, LANES), lambda i: (0, 0)), pl.BlockSpec((1, LANES), lambda i: (0, 0))],
        compiler_params=_params("arbitrary"),
    )(proj_b, bias, dcol, drow)


def _gate_views(li_ref, b_ref, gbt_ref, s, head, heads, sl):
    li_c, b_c = li_ref[s, sl, :], b_ref[s, sl, :]
    li_r = jnp.broadcast_to(gbt_ref[head, :, sl], (CHUNK, CHUNK))
    b_r = jnp.broadcast_to(gbt_ref[heads + head, :, sl], (CHUNK, CHUNK))
    return li_c, b_c, li_r, b_r


def _chunk_state(kt, k, v, li_c, b_c, ct, nrow, m_prev):
    b_tot = b_c[CHUNK - 1:CHUNK, 0:1]
    a = b_tot - b_c[:, 0:1] + li_c[:, 0:1]
    m_new = jnp.maximum(b_tot + m_prev, jnp.max(a, axis=0, keepdims=True))
    dec = jnp.exp(b_tot + m_prev - m_new)
    w = jnp.exp(a - m_new)
    wv = w * v
    ct_new = dec * ct + _dot(kt, wv, NN)
    n_new = dec * nrow + jnp.sum(w * k, axis=0, keepdims=True)
    return ct_new, n_new, m_new, dec, w, wv


def _chunk_scores(q, k, b_c, li_r, b_r, m_prev):
    rows, cols = _chunk_iotas()
    dlog = jnp.where(cols <= rows, b_c - b_r + li_r, -jnp.inf)
    inter = b_c[:, 0:1] + m_prev
    mt = jnp.maximum(inter, jnp.max(dlog, axis=-1, keepdims=True))
    dw = jnp.exp(dlog - mt)
    iw = jnp.exp(inter - mt)
    a = _dot(q, k, NT)
    return dw, iw, mt, a, a * dw


def _to_row(column):
    rows, cols = _chunk_iotas()
    return jnp.sum(jnp.where(rows == cols, jnp.broadcast_to(column, (CHUNK, CHUNK)), 0.0), axis=0, keepdims=True)


def _head_out(h, o, g):
    cell, hh, r = _rms_f(h, g)
    sig = jax.nn.sigmoid(o)
    return sig * cell, (cell, hh, r, sig)


HEADS_PER_STEP = 2


def _mlstm_fwd(qc, kc, kct, proj_a, gcols, gbt, g_mlstm, heads, d_model, name):
    t = qc.shape[0]
    dh = g_mlstm.shape[1] // heads
    nc = t // CHUNK
    scale = dh ** -0.5
    hp = HEADS_PER_STEP

    def body(q_ref, k_ref, kt_ref, v_ref, o_ref, li_ref, b_ref, gbt_ref, g_ref, out_ref, ct_ref, n_ref, m_ref):
        step = pl.program_id(0)
        ct_ref[...] = jnp.zeros_like(ct_ref)
        n_ref[...] = jnp.zeros_like(n_ref)
        m_ref[...] = jnp.zeros_like(m_ref)

        def chunk(ci, carry):
            sl = pl.ds(pl.multiple_of(ci * CHUNK, CHUNK), CHUNK)
            for s in range(hp):
                cs = slice(s * dh, (s + 1) * dh)
                q, k, v, kt = q_ref[sl, cs], k_ref[sl, cs] * scale, v_ref[sl, cs], kt_ref[cs, sl] * scale
                li_c, b_c, li_r, b_r = _gate_views(li_ref, b_ref, gbt_ref, s, step * hp + s, heads, sl)
                ct, nrow, m_prev = ct_ref[s], n_ref[s, 0:1, :], m_ref[s, 0:1, 0:1]
                dw, iw, mt, _, sm = _chunk_scores(q, k, b_c, li_r, b_r, m_prev)
                num = _dot(jnp.concatenate([iw * q, sm], axis=1), jnp.concatenate([ct, v], axis=0), NN)
                den = iw * jnp.sum(q * nrow, axis=-1, keepdims=True) + jnp.sum(sm, axis=-1, keepdims=True)
                h = num / jnp.maximum(jnp.abs(den), jnp.exp(-mt))
                y, _ = _head_out(h, o_ref[sl, cs], g_ref[:, cs])
                out_ref[sl, cs] = y.astype(out_ref.dtype)
                ct_new, n_new, m_new, _, _, _ = _chunk_state(kt, k, v, li_c, b_c, ct, nrow, m_prev)
                ct_ref[s] = ct_new
                n_ref[s, 0:1, :] = n_new
                m_ref[s, 0:1, 0:1] = m_new
            return carry

        lax.fori_loop(0, nc, chunk, 0, unroll=2)

    col = lambda off: pl.BlockSpec((t, hp * dh), lambda h: (0, off // hp + h))
    slab = lambda off: pl.BlockSpec((hp, t, LANES), lambda h: (off // hp + h, 0, 0))
    return pl.pallas_call(
        body, name=name,
        out_shape=jax.ShapeDtypeStruct((t, d_model), BF16),
        grid=(heads // hp,),
        in_specs=[col(0), col(0), pl.BlockSpec((hp * dh, t), lambda h: (h, 0)), col(2 * heads), col(3 * heads),
                  slab(0), slab(heads), pl.BlockSpec((2 * heads, 1, t), lambda h: (0, 0, 0)),
                  pl.BlockSpec((1, hp * dh), lambda h: (0, h))],
        out_specs=pl.BlockSpec((t, hp * dh), lambda h: (0, h)),
        scratch_shapes=[pltpu.VMEM((hp, dh, dh), F32), pltpu.VMEM((hp, 8, dh), F32), pltpu.VMEM((hp, 8, LANES), F32)],
        compiler_params=_params("arbitrary"),
    )(qc, kc, kct, proj_a, proj_a, gcols, gcols, gbt, g_mlstm)


def _mlstm_bwd(qc, kc, qct, kct, proj_a, gcols, gbt, g_mlstm, dhcat, heads, name, dep):
    t = qc.shape[0]
    wm = g_mlstm.shape[1]
    dh = wm // heads
    nc = t // CHUNK
    scale = dh ** -0.5

    def body(q_ref, k_ref, qt_ref, kt_ref, v_ref, o_ref, li_ref, b_ref, gbt_ref, g_ref, dy_ref, dep_ref,
             dq_ref, dk_ref, dv_ref, do_ref, dcol_ref, drow_ref, dg_ref,
             ct_all, n_all, m_all, dct_ref, dn_ref):
        del dep_ref
        head = pl.program_id(0)
        g = g_ref[...]

        @pl.when(head == 0)
        def _():
            dcol_ref[...] = jnp.zeros_like(dcol_ref)
            drow_ref[...] = jnp.zeros_like(drow_ref)

        def load(ci):
            sl = pl.ds(pl.multiple_of(ci * CHUNK, CHUNK), CHUNK)
            return (sl, q_ref[sl, :], k_ref[sl, :] * scale, v_ref[sl, :], kt_ref[:, sl] * scale) \
                + _gate_views(li_ref, b_ref, gbt_ref, 0, head, heads, sl)

        def fwd(ci, carry):
            ct, nrow, m_prev = carry
            ct_all[ci] = ct
            n_all[ci, 0:1, :] = nrow
            m_all[ci, 0:1, 0:1] = m_prev
            _, _, k, v, kt, li_c, b_c, _, _ = load(ci)
            return _chunk_state(kt, k, v, li_c, b_c, ct, nrow, m_prev)[:3]

        lax.fori_loop(0, nc, fwd, (jnp.zeros((dh, dh), F32), jnp.zeros((1, dh), F32), jnp.zeros((1, 1), F32)))

        dct_ref[...] = jnp.zeros_like(dct_ref)
        dn_ref[...] = jnp.zeros_like(dn_ref)
        dg_ref[...] = jnp.zeros_like(dg_ref)
        rows, cols = _chunk_iotas()
        last = lax.broadcasted_iota(jnp.int32, (CHUNK, 1), 0) == CHUNK - 1
        lane = lax.broadcasted_iota(jnp.int32, (CHUNK, LANES), 1)

        def bwd(step, carry):
            ci = nc - 1 - step
            sl, q, k, v, kt, li_c, b_c, li_r, b_r = load(ci)
            qt = qt_ref[:, sl]
            ct, nrow, m_prev = ct_all[ci], n_all[ci, 0:1, :], m_all[ci, 0:1, 0:1]
            dw, iw, mt, a, sm = _chunk_scores(q, k, b_c, li_r, b_r, m_prev)
            qc_ = _dot(q, ct, NN)
            qn = jnp.sum(q * nrow, axis=-1, keepdims=True)
            num = iw * qc_ + _dot(sm, v, NN)
            den = iw * qn + jnp.sum(sm, axis=-1, keepdims=True)
            floor = jnp.exp(-mt)
            dn = jnp.maximum(jnp.abs(den), floor)
            h = num / dn
            _, (cell, hh, r, sig) = _head_out(h, o_ref[sl, :], g)
            dy = dy_ref[sl, :]
            dcell = dy * sig
            do_ref[sl, :] = (dy * cell * sig * (1.0 - sig)).astype(do_ref.dtype)
            dh_, dg = _rms_b(dcell, g, hh, r)
            dg_ref[...] += dg
            dnum = dh_ / dn
            ddn = -jnp.sum(dh_ * h, axis=-1, keepdims=True) / dn
            dden = jnp.where(jnp.abs(den) >= floor, ddn * jnp.sign(den), 0.0)
            idn = iw * dnum
            idd = iw * dden
            dw_t = jnp.exp(jnp.where(cols >= rows, b_r - b_c + li_c, -jnp.inf) - _to_row(mt))
            sm_t = _dot(k, q, NT) * dw_t
            dsm = _dot(dnum, v, NT) + dden
            da_ = dsm * dw
            da_t = (_dot(v, dnum, NT) + _to_row(dden)) * dw_t
            e = da_ * a
            dq = _dot(idn, ct, NT) + _dot(da_, k, NN) + idd * nrow
            dk = _dot(da_t, q, NN)
            dv = _dot(sm_t, dnum, NN)
            dct = _dot(qt, idn, NN)
            dnr = jnp.sum(q * idd, axis=0, keepdims=True)
            diw = jnp.sum(dnum * qc_, axis=-1, keepdims=True) + dden * qn
            db = jnp.sum(e, axis=-1, keepdims=True) + diw * iw
            _, _, _, dec, w, wv = _chunk_state(kt, k, v, li_c, b_c, ct, nrow, m_prev)
            dct_new, dn_new = dct_ref[...], dn_ref[0:1, :]
            ddec = jnp.sum(jnp.sum(dct_new * ct, axis=-1, keepdims=True), axis=0, keepdims=True) \
                + jnp.sum(dn_new * nrow, axis=-1, keepdims=True)
            dk = dk + _dot(wv, dct_new, NT) + w * dn_new
            dwv = _dot(k, dct_new, NN)
            dv = dv + w * dwv
            dwt = jnp.sum(dwv * v, axis=-1, keepdims=True) + jnp.sum(k * dn_new, axis=-1, keepdims=True)
            da = dwt * w
            dbtot = jnp.sum(da, axis=0, keepdims=True) + ddec * dec
            db = db - da + jnp.where(last, dbtot, 0.0)
            dct_ref[...] = dec * dct_new + dct
            dn_ref[0:1, :] = dec * dn_new + dnr
            dq_ref[sl, :] = dq
            dk_ref[sl, :] = dk * scale
            dv_ref[sl, :] = dv.astype(dv_ref.dtype)
            dcol_ref[sl, :] += jnp.where(lane == head, da, 0.0) + jnp.where(lane == heads + head, db, 0.0)
            drow_ref[head, :, sl] = jnp.sum(e, axis=0, keepdims=True)
            return carry

        lax.fori_loop(0, nc, bwd, 0, unroll=4)

    col = lambda off: pl.BlockSpec((t, dh), lambda h: (0, off + h))
    rowb = pl.BlockSpec((dh, t), lambda h: (h, 0))
    slab = lambda off: pl.BlockSpec((1, t, LANES), lambda h: (off + h, 0, 0))
    vec = pl.BlockSpec((1, dh), lambda h: (0, h))
    return pl.pallas_call(
        body, name=name,
        out_shape=[jax.ShapeDtypeStruct((t, wm), F32)] * 2 + [jax.ShapeDtypeStruct((t, wm), BF16)] * 2
        + [jax.ShapeDtypeStruct((t, LANES), F32), jax.ShapeDtypeStruct((heads, 1, t), F32),
           jax.ShapeDtypeStruct((1, wm), F32)],
        grid=(heads,),
        in_specs=[col(0), col(0), rowb, rowb, col(2 * heads), col(3 * heads), slab(0), slab(heads),
                  pl.BlockSpec((2 * heads, 1, t), lambda h: (0, 0, 0)), vec, col(0), ANY_SPEC],
        out_specs=[col(0), col(0), col(0), col(0), pl.BlockSpec((t, LANES), lambda h: (0, 0)),
                   pl.BlockSpec((heads, 1, t), lambda h: (0, 0, 0)), vec],
        scratch_shapes=[pltpu.VMEM((nc, dh, dh), F32), pltpu.VMEM((nc, 8, dh), F32), pltpu.VMEM((nc, 8, LANES), F32),
                        pltpu.VMEM((dh, dh), F32), pltpu.VMEM((8, dh), F32)],
        compiler_params=_params("arbitrary"),
    )(qc, kc, qct, kct, proj_a, proj_a, gcols, gcols, gbt, g_mlstm, dhcat, dep)


def _pool_window(gi):
    return jnp.where(gi == 0, 2.0, jnp.where(gi == 1, 4.0, jnp.where(gi == 2, 8.0, 16.0))).astype(F32)


def _pool_select(gi, levels):
    return jnp.where(gi == 0, levels[0], jnp.where(gi == 1, levels[1], jnp.where(gi == 2, levels[2], levels[3])))


def _pooled(u, gi, rows):
    s, levels = u, []
    for lvl in range(POOL_GROUPS):
        s = s + _shift_down(s, 1 << lvl, rows)
        levels.append(s)
    cnt = jnp.minimum((rows + 1).astype(F32), _pool_window(gi))
    return _pool_select(gi, levels) / cnt - u, cnt


def _pool_fwd(proj_b, w_pool, scale, hcat, name):
    t, d_model = hcat.shape
    g, cg, _ = w_pool.shape
    off = (d_model - g * cg) // cg

    def body(u_ref, w_ref, s_ref, hcat_ref, out_ref):
        del hcat_ref
        gi = pl.program_id(0)
        u = u_ref[...]
        rows = lax.broadcasted_iota(jnp.int32, u.shape, 0)
        pooled, _ = _pooled(u, gi, rows)
        out_ref[...] = (_dot(pooled, w_ref[...], NN) * s_ref[...]).astype(out_ref.dtype)

    return pl.pallas_call(
        body, name=name,
        out_shape=jax.ShapeDtypeStruct((t, d_model), BF16),
        grid=(g,),
        in_specs=[pl.BlockSpec((t, cg), lambda gi: (0, gi)), pl.BlockSpec((None, cg, cg), lambda gi: (gi, 0, 0)),
                  pl.BlockSpec((1, cg), lambda gi: (0, gi)), HBM_SPEC],
        out_specs=pl.BlockSpec((t, cg), lambda gi: (0, off + gi)),
        input_output_aliases={3: 0},
        compiler_params=_params("arbitrary"),
    )(proj_b, w_pool, scale, hcat)


def _pool_bwd(proj_b, w_pool, scale, dhcat, name):
    t, d_model = dhcat.shape
    g, cg, _ = w_pool.shape
    off = (d_model - g * cg) // cg

    def body(u_ref, w_ref, s_ref, dy_ref, du_ref, dw_ref, ds_ref):
        gi = pl.program_id(0)
        u, w, dy = u_ref[...], w_ref[...], dy_ref[...]
        rows = lax.broadcasted_iota(jnp.int32, u.shape, 0)
        pooled, cnt = _pooled(u, gi, rows)
        mixed = _dot(pooled, w, NN)
        ds_ref[...] = jnp.sum(dy * mixed, axis=0, keepdims=True)
        dmixed = dy * s_ref[...]
        dw_ref[...] = _dot(pooled, dmixed, TN).astype(dw_ref.dtype)
        dpooled = _dot(dmixed, w, NT)
        s, levels = dpooled / cnt, []
        for lvl in range(POOL_GROUPS):
            s = s + _shift_up(s, 1 << lvl, rows)
            levels.append(s)
        du_ref[...] = (_pool_select(gi, levels) - dpooled).astype(du_ref.dtype)

    return pl.pallas_call(
        body, name=name,
        out_shape=[jax.ShapeDtypeStruct((t, g * cg), BF16), jax.ShapeDtypeStruct((g, cg, cg), BF16),
                   jax.ShapeDtypeStruct((1, g * cg), F32)],
        grid=(g,),
        in_specs=[pl.BlockSpec((t, cg), lambda gi: (0, gi)), pl.BlockSpec((None, cg, cg), lambda gi: (gi, 0, 0)),
                  pl.BlockSpec((1, cg), lambda gi: (0, gi)), pl.BlockSpec((t, cg), lambda gi: (0, off + gi))],
        out_specs=[pl.BlockSpec((t, cg), lambda gi: (0, gi)), pl.BlockSpec((None, cg, cg), lambda gi: (gi, 0, 0)),
                   pl.BlockSpec((1, cg), lambda gi: (0, gi))],
        compiler_params=_params("arbitrary"),
    )(proj_b, w_pool, scale, dhcat)


def _pad_cols(a, width):
    return jnp.pad(a, ((0, 0), (0, width - a.shape[1])))


def kernel(x, p, w_in, b_gates, w_qk_conv, g_mlstm, w_pool, pool_scale, w_out, g_mix_pre, g_mix_post, w_ff1, w_ff2, g_ff_pre, g_ff_post, w_ple_proj, w_ple_gate, g_ple_gate, g_ple_post, loss_target, m_w_in, m_b_gates, m_w_qk_conv, m_g_mlstm, m_w_pool, m_pool_scale, m_w_out, m_g_mix_pre, m_g_mix_post, m_w_ff1, m_w_ff2, m_g_ff_pre, m_g_ff_post, m_w_ple_proj, m_w_ple_gate, m_g_ple_gate, m_g_ple_post, v_w_in, v_b_gates, v_w_qk_conv, v_g_mlstm, v_w_pool, v_pool_scale, v_w_out, v_g_mix_pre, v_g_mix_post, v_w_ff1, v_w_ff2, v_g_ff_pre, v_g_ff_post, v_w_ple_proj, v_w_ple_gate, v_g_ple_gate, v_g_ple_post):
    weights = dict(w_in=w_in, b_gates=b_gates, w_qk_conv=w_qk_conv, g_mlstm=g_mlstm, w_pool=w_pool,
                   pool_scale=pool_scale, w_out=w_out, g_mix_pre=g_mix_pre, g_mix_post=g_mix_post, w_ff1=w_ff1,
                   w_ff2=w_ff2, g_ff_pre=g_ff_pre, g_ff_post=g_ff_post, w_ple_proj=w_ple_proj,
                   w_ple_gate=w_ple_gate, g_ple_gate=g_ple_gate, g_ple_post=g_ple_post)
    mom1 = dict(w_in=m_w_in, b_gates=m_b_gates, w_qk_conv=m_w_qk_conv, g_mlstm=m_g_mlstm, w_pool=m_w_pool,
                pool_scale=m_pool_scale, w_out=m_w_out, g_mix_pre=m_g_mix_pre, g_mix_post=m_g_mix_post,
                w_ff1=m_w_ff1, w_ff2=m_w_ff2, g_ff_pre=m_g_ff_pre, g_ff_post=m_g_ff_post,
                w_ple_proj=m_w_ple_proj, w_ple_gate=m_w_ple_gate, g_ple_gate=m_g_ple_gate, g_ple_post=m_g_ple_post)
    mom2 = dict(w_in=v_w_in, b_gates=v_b_gates, w_qk_conv=v_w_qk_conv, g_mlstm=v_g_mlstm, w_pool=v_w_pool,
                pool_scale=v_pool_scale, w_out=v_w_out, g_mix_pre=v_g_mix_pre, g_mix_post=v_g_mix_post,
                w_ff1=v_w_ff1, w_ff2=v_w_ff2, g_ff_pre=v_g_ff_pre, g_ff_post=v_g_ff_post,
                w_ple_proj=v_w_ple_proj, w_ple_gate=v_w_ple_gate, g_ple_gate=v_g_ple_gate, g_ple_post=v_g_ple_post)
    order = list(weights)

    t, d = x.shape[1], x.shape[2]
    heads = MLSTM_HEADS
    wm = g_mlstm.shape[1]
    pw = pool_scale.shape[1]
    cg = pw // POOL_GROUPS
    in_cols = 4 * wm + 2 * heads + pw
    in_shard = w_in.shape[2]
    xs, ps, tgt = x[0], p[0, 0], loss_target[0]

    mx, my, mc = lax.axis_index("x"), lax.axis_index("y"), lax.axis_index("c")
    core = jnp.reshape(mc, (1,)).astype(jnp.int32)
    chip = jnp.reshape(2 * mx + my, (1,)).astype(jnp.int32)

    first = [w_in[0].T.astype(BF16), w_qk_conv[0], w_pool[0].reshape(POOL_GROUPS * w_pool.shape[2], cg).astype(BF16)]
    started, token = _gather_start([first], "gather_start_in")
    anchor = token[0, 0]
    later = lambda w: (w[0] + anchor).astype(BF16)
    rest, token = _gather_start([[later(w_out)], [later(w_ff1)], [later(w_ff2)], [later(w_ple_gate), later(w_ple_proj)]],
                                "gather_start_rest")
    started = started + rest
    bias = _pad_cols(b_gates, LANES)
    small = ["b_gates", "g_mlstm", "pool_scale", "g_mix_pre", "g_mix_post", "g_ff_pre", "g_ff_post", "g_ple_gate",
             "g_ple_post"]
    pack = lambda src: _pad_rows16(jnp.concatenate([_pad_cols(src[k] + anchor, d) for k in small], axis=0))
    small_state = [pack(weights), pack(mom1), pack(mom2)]
    in_state = [a[0].T + anchor for a in (w_in, m_w_in, v_w_in)]

    xn = _norm_fwd(xs, g_mix_pre, "norm_mix_pre", deps=[token])
    g_in, g_conv, g_pool = _forward_pair(
        _gather_wait(started[0], [xn] + small_state + in_state, "gather_wait_in"), "gather_pair_in")
    in_t_a = g_in.reshape(in_cols, d)
    in_t_b = jnp.concatenate([in_t_a[4 * wm + 2 * heads:], in_t_a[4 * wm:4 * wm + 2 * heads],
                              jnp.zeros((LANES - 2 * heads, d), BF16)], axis=0)
    conv_w = g_conv.transpose(1, 0, 2).reshape(QK_CONV, 2 * wm)
    pool_w = g_pool.reshape(N_DEV, POOL_GROUPS, cg // N_DEV, cg).transpose(1, 0, 2, 3).reshape(POOL_GROUPS, cg, cg)
    proj_a = _mm_nt("proj_qkvo", xn, in_t_a, n_rows=4 * wm)
    proj_b = _mm_nt("proj_pool_gates", xn, in_t_b)
    qc = _conv_fwd(proj_a, conv_w[:, :wm], 0, "q_conv")
    kc = _conv_fwd(proj_a, conv_w[:, wm:], wm, "k_conv")
    gb, gbt = _gates_fwd(proj_b, bias, pw // LANES, heads, "gates")
    gcols = _gate_columns(gb, heads, "gate_columns")
    gbt = gbt[:2 * heads].reshape(2 * heads, 1, t)
    qct, kct = qc.T, kc.T
    (g_out,) = _forward_pair(_gather_wait(started[1], gb, "gather_wait_out"), "gather_pair_out")
    hcat = _mlstm_fwd(qc, kc, kct, proj_a, gcols, gbt, g_mlstm, heads, d, "mlstm")
    hcat = _pool_fwd(proj_b, pool_w, pool_scale, hcat, "pool")
    out_w = g_out.reshape(d, d)
    mix = _mm_nn("mix_out", hcat, out_w)[0]
    h1, hn = _block_fwd(xs, mix, g_mix_post, g_ff_pre, "residual_mix")
    (g_ff1,) = _forward_pair(_gather_wait(started[2], hn, "gather_wait_ff1"), "gather_pair_ff1")
    relu_a, act = _mm_nn_pieces("ff1", hn, g_ff1, out_dtype=[BF16, BF16],
                                epilogue=lambda acc: (jnp.maximum(acc, 0.0), jnp.square(jnp.maximum(acc, 0.0))))
    (g_ff2,) = _forward_pair(_gather_wait(started[3], act, "gather_wait_ff2"), "gather_pair_ff2")
    ff2_w = g_ff2.reshape(-1, d)
    ff = _mm_nn("ff2", act, ff2_w, tn=256, tk=4 * d)[0]
    h2, hg = _block_fwd(h1, ff, g_ff_post, g_ple_gate, "residual_ff")
    g_pgate, g_pproj = _forward_pair(_gather_wait(started[4], hg, "gather_wait_ple"), "gather_pair_ple")
    pgate_w = g_pgate.reshape(d, d)
    z = _mm_nn("ple_gate", hg, pgate_w)[0]
    e = _mm_nn_pieces("ple_proj", ps, g_pproj)[0]
    dh3, dz, de, dg_ple_post, loss_part = _loss_head(h2, z, e, tgt, g_ple_post, "loss_head")
    loss = lax.psum(loss_part[0, 0], MESH_AXES)

    out = {}

    def state(k):
        if k == "w_in":
            return in_state
        return [a.reshape(-1, a.shape[-1]) for a in (weights[k], mom1[k], mom2[k])]

    def pair_begin(pieces, tag):
        keys = list(pieces)
        return keys, _pair_start([pieces[k] for k in keys], "reduce_pair_start_" + tag)

    def chip_begin(group, after, tag):
        keys, begun = group
        mine, got = _pair_wait(begun, after, "reduce_pair_wait_" + tag)
        sums = [_pair_sum(p_, g, core, "pair_sum_" + k) for k, p_, g in zip(keys, mine, got)]
        return keys, _chip_start(sums, "reduce_start_" + tag)

    def reduce_start(pieces, tag):
        keys = list(pieces)
        got = _pair_exchange([pieces[k] for k in keys], "reduce_pair_" + tag)
        sums = [_pair_sum(pieces[k], g, core, "pair_sum_" + k) for k, g in zip(keys, got)]
        return keys, _chip_start(sums, "reduce_start_" + tag)

    def reduce_finish(group, after, tag):
        keys, begun = group
        sums, got = _chip_wait(begun, after, "reduce_wait_" + tag)
        for k, s, g in zip(keys, sums, got):
            res = _adamw_shard(*state(k), s, g, chip, "adamw_" + k)
            out[k] = [r.T[None] if k == "w_in" else r.reshape(weights[k].shape) for r in res]

    token_of = lambda group: group[1]["token"]
    dhg = _mm_nt("d_ple_gate_in", dz, pgate_w)
    gw_pgate = _mm_tn("gw_ple_gate", hg, dz)
    gw_pproj = _mm_tn("gw_ple_proj", ps, de, pieces=N_DEV)
    pair_ple = pair_begin({"w_ple_gate": gw_pgate.reshape(N_DEV, d // N_DEV, d), "w_ple_proj": gw_pproj}, "ple")
    dh2, dff, dg_ff_post, dg_ple_gate = _block_bwd(h2, ff, dh3, dhg, g_ff_post, g_ple_gate, "d_residual_ff",
                                                   deps=[token_of(pair_ple)])
    red_ple = chip_begin(pair_ple, dff, "ple")
    da = _mm_nt("d_ff2_in", dff, ff2_w, out_dtype=BF16, extra=[relu_a],
                epilogue=lambda acc, r: (acc * (2.0 * r.astype(F32)),), deps=[token_of(red_ple)])
    gw_ff2 = _mm_tn("gw_ff2", act, dff)
    pair_ff2 = pair_begin({"w_ff2": gw_ff2.reshape(N_DEV, -1, d)}, "ff2")
    dhn = _mm_nt_pieces("d_ff1_in", da, g_ff1, deps=[token_of(pair_ff2)])
    red_ff2 = chip_begin(pair_ff2, dhn, "ff2")
    reduce_finish(red_ple, token_of(red_ff2), "ple")
    gw_ff1 = _mm_tn("gw_ff1", hn, da, pieces=N_DEV, deps=[token_of(red_ff2)])
    pair_ff1 = pair_begin({"w_ff1": gw_ff1}, "ff1")
    dh1, dmix, dg_mix_post, dg_ff_pre = _block_bwd(h1, mix, dh2, dhn, g_mix_post, g_ff_pre, "d_residual_mix",
                                                   deps=[token_of(pair_ff1)])
    dhcat = _mm_nt("d_mix_in", dmix, out_w)
    red_ff1 = chip_begin(pair_ff1, dhcat, "ff1")
    reduce_finish(red_ff2, token_of(red_ff1), "ff2")
    gw_out = _mm_tn("gw_out", hcat, dmix, deps=[token_of(red_ff1)])
    du, gw_pool, dg_pool_scale = _pool_bwd(proj_b, pool_w, pool_scale, dhcat, "d_pool")
    pair_out = pair_begin({
        "w_out": gw_out.reshape(N_DEV, d // N_DEV, d),
        "w_pool": gw_pool.reshape(POOL_GROUPS, N_DEV, cg // N_DEV, cg).transpose(1, 0, 2, 3).reshape(N_DEV, -1, cg)},
        "out")
    dq, dk, dv, do, dcol, drow, dg_mlstm = _mlstm_bwd(qc, kc, qct, kct, proj_a, gcols, gbt, g_mlstm, dhcat, heads,
                                                      "d_mlstm", token_of(pair_out))
    red_out = chip_begin(pair_out, dq, "out")
    reduce_finish(red_ff1, token_of(red_out), "ff1")
    drow = jnp.pad(drow.reshape(heads, t), ((0, LANES - heads), (0, 0)))
    dgates, dbias = _gates_bwd(proj_b, bias, dcol, drow, pw // LANES, heads, "d_gates")
    dq_pre, gw_conv_q = _conv_bwd(proj_a, conv_w[:, :wm], dq, 0, "d_q_conv")
    dk_pre, gw_conv_k = _conv_bwd(proj_a, conv_w[:, wm:], dk, wm, "d_k_conv")
    gw_conv = jnp.concatenate([gw_conv_q, gw_conv_k], axis=1)
    dproj_a = jnp.concatenate([dq_pre, dk_pre, dv, do], axis=1)
    dproj_b = jnp.concatenate([du, dgates], axis=1)
    gw_a = _mm_tn("gw_in_qkvo", dproj_a, xn)
    gw_b = _mm_tn("gw_in_pool_gates", dproj_b, xn)
    gw_in_t = jnp.concatenate([gw_a, gw_b[pw:pw + 2 * heads], gw_b[:pw]], axis=0)
    red_in = reduce_start({
        "w_in": gw_in_t.reshape(N_DEV, in_shard, d),
        "w_qk_conv": gw_conv.reshape(QK_CONV, N_DEV, -1).transpose(1, 0, 2).astype(BF16)}, "in")
    reduce_finish(red_out, red_in[1]["token"], "out")
    dxn_a = _mm_nn("d_proj_qkvo_in", dproj_a, in_t_a, tn=512, tk=4 * wm, deps=[red_in[1]["token"]])[0]
    dxn = _mm_nn("d_proj_in", dproj_b, in_t_b, extra=[dxn_a], epilogue=lambda acc, prev: (acc + prev,))[0]
    grad_x, dg_mix_pre = _norm_bwd(xs, dxn, dh1, g_mix_pre, "d_norm_mix_pre")

    small_grads = dict(b_gates=dbias[:, :2 * heads], g_mlstm=dg_mlstm, pool_scale=dg_pool_scale, g_mix_pre=dg_mix_pre,
                       g_mix_post=dg_mix_post, g_ff_pre=dg_ff_pre, g_ff_post=dg_ff_post, g_ple_gate=dg_ple_gate,
                       g_ple_post=dg_ple_post)
    small_parts = _all_gather([pack(small_grads)], "gather_small_grads")[0]
    res = _adamw_replicated(*small_state, small_parts, "adamw_replicated")
    for i, k in enumerate(small):
        width = weights[k].shape[1]
        out[k] = [r[i:i + 1, :width] for r in res]
    reduce_finish(red_in, res[0], "in")

    return (loss, grad_x[None], *[out[k][0] for k in order], *[out[k][1] for k in order],
            *[out[k][2] for k in order], *[out[k][3] for k in order])


def _pad_rows16(a):
    return jnp.pad(a, ((0, 16 - a.shape[0]), (0, 0)))
```

```python
import functools

import jax
import jax.numpy as jnp
from jax import lax
from jax.experimental import pallas as pl
from jax.experimental.pallas import tpu as pltpu

F32 = jnp.float32
BF16 = jnp.bfloat16
EPS = 1e-6
N_DEV = 8
N_CHIP = 4
LANES = 128
VMEM_LIMIT = 56 * 1024 * 1024
MLSTM_HEADS = 8
POOL_GROUPS = 4
QK_CONV = 4
CHUNK = 128
ROW_TILE = 256
ADAM_LR, ADAM_B1, ADAM_B2, ADAM_EPS, ADAM_WD, ADAM_STEP = 0.001, 0.9, 0.999, 1e-08, 0.01, 10
MESH_AXES = ("x", "y", "c")
MESH = pl.DeviceIdType.MESH
HBM_SPEC = pl.BlockSpec(memory_space=pltpu.HBM)
SEM_SPEC = pl.BlockSpec(memory_space=pltpu.SEMAPHORE)
DATAFLOW = pltpu.SideEffectType.DATAFLOW_SIDE_EFFECTING


def _params(*sem):
    if sem:
        return pltpu.CompilerParams(dimension_semantics=sem, vmem_limit_bytes=VMEM_LIMIT)
    return pltpu.CompilerParams(vmem_limit_bytes=VMEM_LIMIT)


def _tile(n, pref):
    if n <= pref:
        return n
    t = (pref // LANES) * LANES
    while t >= LANES:
        if n % t == 0:
            return t
        t -= LANES
    return n


def _all_gather(arrs, name):
    n = len(arrs)

    def body(*refs):
        ins, outs = refs[:n], refs[n:2 * n]
        send, recv, loc = refs[2 * n:]
        x, y, c = lax.axis_index("x"), lax.axis_index("y"), lax.axis_index("c")
        me = (x, y, c)
        sibling = (x, y, 1 - c)
        chips = [(1 - x, y), (x, 1 - y), (1 - x, 1 - y)]

        def row(dev):
            return 4 * dev[0] + 2 * dev[1] + dev[2]

        def copy(w, k, block_of, to, src=None):
            dst = outs[w].at[row(block_of)]
            return pltpu.make_async_remote_copy(
                src_ref=dst if src is None else src, dst_ref=dst,
                send_sem=send.at[7 * w + k], recv_sem=recv.at[7 * w + k],
                device_id=to, device_id_type=MESH)

        started = []
        local = []
        for w in range(n):
            cp = pltpu.make_async_copy(ins[w], outs[w].at[row(me)], loc.at[w])
            cp.start()
            local.append(cp)
            first = [copy(w, 0, me, sibling, src=ins[w])]
            first += [copy(w, 1 + j, me, (*chip, c), src=ins[w]) for j, chip in enumerate(chips)]
            for cp in first:
                cp.start()
            started += first
        for j, chip in enumerate(chips):
            for w in range(n):
                copy(w, 1 + j, (*chip, c), me).wait_recv()
                fwd = copy(w, 4 + j, (*chip, c), sibling)
                fwd.start()
                started.append(fwd)
        for w in range(n):
            copy(w, 0, sibling, me).wait_recv()
            for j, chip in enumerate(chips):
                copy(w, 4 + j, (*chip, 1 - c), me).wait_recv()
        for cp in started:
            cp.wait_send()
        for cp in local:
            cp.wait()

    outs = pl.pallas_call(
        body, name=name,
        out_shape=[jax.ShapeDtypeStruct((N_DEV,) + a.shape, a.dtype) for a in arrs],
        in_specs=[HBM_SPEC] * n, out_specs=[HBM_SPEC] * n,
        scratch_shapes=[pltpu.SemaphoreType.DMA((7 * n,)), pltpu.SemaphoreType.DMA((7 * n,)),
                        pltpu.SemaphoreType.DMA((n,))],
    )(*arrs)
    return list(outs)


def _pair_exchange(pieces, name):
    n = len(pieces)

    def body(*refs):
        ins, outs = refs[:n], refs[n:2 * n]
        send, recv = refs[2 * n:]
        x, y, c = lax.axis_index("x"), lax.axis_index("y"), lax.axis_index("c")
        sibling = (x, y, 1 - c)

        def copy(w, chip, src_core):
            return pltpu.make_async_remote_copy(
                src_ref=ins[w].at[2 * chip + src_core], dst_ref=outs[w].at[chip],
                send_sem=send.at[N_CHIP * w + chip], recv_sem=recv.at[N_CHIP * w + chip],
                device_id=sibling, device_id_type=MESH)

        cps = [copy(w, chip, 1 - c) for w in range(n) for chip in range(N_CHIP)]
        for cp in cps:
            cp.start()
        for cp in cps:
            cp.wait_recv()
        for cp in cps:
            cp.wait_send()

    outs = pl.pallas_call(
        body, name=name,
        out_shape=[jax.ShapeDtypeStruct((N_CHIP,) + a.shape[1:], a.dtype) for a in pieces],
        in_specs=[HBM_SPEC] * n, out_specs=[HBM_SPEC] * n,
        scratch_shapes=[pltpu.SemaphoreType.DMA((N_CHIP * n,)), pltpu.SemaphoreType.DMA((N_CHIP * n,))],
    )(*pieces)
    return list(outs)


def _hbm(a):
    return pltpu.with_memory_space_constraint(a, pltpu.HBM)


def _mesh_place():
    x, y, c = lax.axis_index("x"), lax.axis_index("y"), lax.axis_index("c")
    return x, y, c, [(1 - x, y), (x, 1 - y), (1 - x, 1 - y)]


def _chip_copies(srcs, lands, send, recv):
    x, y, c, chips = _mesh_place()
    return [pltpu.make_async_remote_copy(
        src_ref=srcs[w].at[2 * chip[0] + chip[1]], dst_ref=lands[w].at[k],
        send_sem=send.at[3 * w + k], recv_sem=recv.at[3 * w + k],
        device_id=(*chip, c), device_id_type=MESH)
        for k, chip in enumerate(chips) for w in range(len(srcs))]


def _chip_start(sums, name):
    return _split_start(_chip_copies, sums, [lax.empty((3,) + a.shape[1:], a.dtype) for a in sums], 3, name)


def _chip_wait(started, after, name):
    return _split_wait(_chip_copies, started, after, name)


def _pair_start(pieces, name):
    return _split_start(_pair_copies, pieces, [lax.empty((N_CHIP,) + a.shape[1:], a.dtype) for a in pieces], N_CHIP, name)


def _pair_wait(started, after, name):
    return _split_wait(_pair_copies, started, after, name)


def _pair_copies(srcs, lands, send, recv):
    x, y, c, _ = _mesh_place()
    return [pltpu.make_async_remote_copy(
        src_ref=srcs[w].at[2 * ch + 1 - c], dst_ref=lands[w].at[ch],
        send_sem=send.at[N_CHIP * w + ch], recv_sem=recv.at[N_CHIP * w + ch],
        device_id=(x, y, 1 - c), device_id_type=MESH)
        for w in range(len(srcs)) for ch in range(N_CHIP)]


def _split_start(copies_of, srcs, lands, per_array, name):
    n = len(srcs)

    def body(*refs):
        for cp in copies_of(refs[:n], refs[n:2 * n], refs[2 * n], refs[2 * n + 1]):
            cp.start()
        refs[-1][...] = jnp.zeros_like(refs[-1])

    res = pl.pallas_call(
        body, name=name,
        out_shape=[pltpu.SemaphoreType.DMA((per_array * n,)), pltpu.SemaphoreType.DMA((per_array * n,))]
        + [pltpu.HBM(a.shape, a.dtype) for a in srcs + lands] + [jax.ShapeDtypeStruct((8, LANES), F32)],
        in_specs=[HBM_SPEC] * (2 * n),
        out_specs=[SEM_SPEC, SEM_SPEC] + [HBM_SPEC] * (2 * n) + [pl.BlockSpec(memory_space=pltpu.VMEM)],
        input_output_aliases={i: 2 + i for i in range(2 * n)},
        compiler_params=pltpu.CompilerParams(has_side_effects=DATAFLOW),
    )(*[_hbm(a) for a in srcs], *[_hbm(a) for a in lands])
    return dict(send=res[0], recv=res[1], srcs=list(res[2:2 + n]), lands=list(res[2 + n:2 + 2 * n]), token=res[-1])


def _split_wait(copies_of, started, after, name):
    n = len(started["srcs"])

    def body(*refs):
        for cp in copies_of(refs[:n], refs[n:2 * n], refs[2 * n], refs[2 * n + 1]):
            cp.wait_send()
            cp.wait_recv()

    res = pl.pallas_call(
        body, name=name,
        out_shape=[pltpu.HBM(a.shape, a.dtype) for a in started["srcs"] + started["lands"]],
        in_specs=[HBM_SPEC] * (2 * n) + [SEM_SPEC, SEM_SPEC, ANY_SPEC],
        out_specs=[HBM_SPEC] * (2 * n),
        input_output_aliases={i: i for i in range(2 * n)},
        compiler_params=pltpu.CompilerParams(has_side_effects=DATAFLOW),
    )(*started["srcs"], *started["lands"], started["send"], started["recv"], after)
    return list(res[:n]), list(res[n:])


def _gather_copies(srcs, lands, send, recv):
    x, y, c, chips = _mesh_place()
    me = 4 * x + 2 * y + c
    targets = [(x, y, 1 - c)] + [(*chip, c) for chip in chips]
    return [pltpu.make_async_remote_copy(
        src_ref=srcs[w], dst_ref=lands[w].at[me], send_sem=send.at[4 * w + k], recv_sem=recv.at[4 * w + k],
        device_id=to, device_id_type=MESH)
        for w in range(len(srcs)) for k, to in enumerate(targets)]


def _gather_start(groups, name):
    sizes = [len(g) for g in groups]
    flat = [a for g in groups for a in g]
    n, ng = len(flat), len(groups)
    me = 4 * lax.axis_index("x") + 2 * lax.axis_index("y") + lax.axis_index("c")
    lands = [lax.dynamic_update_slice(lax.empty((N_DEV,) + a.shape, a.dtype), a[None], (me,) + (0,) * a.ndim)
             for a in flat]

    def body(*refs):
        srcs, zones = refs[:n], refs[n:2 * n]
        sems = refs[2 * n:2 * n + 2 * ng]
        token = refs[-1]
        lo = 0
        for gi, size in enumerate(sizes):
            for cp in _gather_copies(srcs[lo:lo + size], zones[lo:lo + size], sems[2 * gi], sems[2 * gi + 1]):
                cp.start()
            lo += size
        token[...] = jnp.zeros_like(token)

    sem_shapes = []
    for size in sizes:
        sem_shapes += [pltpu.SemaphoreType.DMA((4 * size,))] * 2
    res = pl.pallas_call(
        body, name=name,
        out_shape=sem_shapes + [pltpu.HBM(a.shape, a.dtype) for a in flat + lands] + [jax.ShapeDtypeStruct((8, LANES), F32)],
        in_specs=[HBM_SPEC] * (2 * n),
        out_specs=[SEM_SPEC] * (2 * ng) + [HBM_SPEC] * (2 * n) + [pl.BlockSpec(memory_space=pltpu.VMEM)],
        input_output_aliases={i: 2 * ng + i for i in range(2 * n)},
        compiler_params=pltpu.CompilerParams(has_side_effects=DATAFLOW),
    )(*[_hbm(a) for a in flat], *[_hbm(a) for a in lands])
    out, lo = [], 0
    for gi, size in enumerate(sizes):
        out.append(dict(send=res[2 * gi], recv=res[2 * gi + 1],
                        srcs=list(res[2 * ng + lo:2 * ng + lo + size]),
                        lands=list(res[2 * ng + n + lo:2 * ng + n + lo + size])))
        lo += size
    return out, res[-1]


def _gather_wait(started, after, name):
    n = len(started["srcs"])
    after = list(after) if isinstance(after, (list, tuple)) else [after]

    def body(*refs):
        srcs, zones = refs[:n], refs[n:2 * n]
        send, recv = refs[2 * n], refs[2 * n + 1]
        for cp in _gather_copies(srcs, zones, send, recv):
            cp.wait_send()
            cp.wait_recv()

    res = pl.pallas_call(
        body, name=name,
        out_shape=[pltpu.HBM(a.shape, a.dtype) for a in started["srcs"] + started["lands"]],
        in_specs=[HBM_SPEC] * (2 * n) + [SEM_SPEC, SEM_SPEC] + [ANY_SPEC] * len(after),
        out_specs=[HBM_SPEC] * (2 * n),
        input_output_aliases={i: i for i in range(2 * n)},
        compiler_params=pltpu.CompilerParams(has_side_effects=DATAFLOW),
    )(*started["srcs"], *started["lands"], started["send"], started["recv"], *after)
    return list(res[n:])


def _forward_pair(lands, name):
    n = len(lands)

    def body(*refs):
        zones = refs[:n]
        send, recv = refs[2 * n:]
        x, y, c, chips = _mesh_place()

        def copy(w, j, core):
            rows = zones[w].at[4 * chips[j][0] + 2 * chips[j][1] + core]
            return pltpu.make_async_remote_copy(
                src_ref=rows, dst_ref=rows, send_sem=send.at[3 * w + j], recv_sem=recv.at[3 * w + j],
                device_id=(x, y, 1 - c), device_id_type=MESH)

        sends = [copy(w, j, c) for w in range(n) for j in range(3)]
        for cp in sends:
            cp.start()
        for w in range(n):
            for j in range(3):
                copy(w, j, 1 - c).wait_recv()
        for cp in sends:
            cp.wait_send()

    return list(pl.pallas_call(
        body, name=name,
        out_shape=[jax.ShapeDtypeStruct(a.shape, a.dtype) for a in lands],
        in_specs=[HBM_SPEC] * n, out_specs=[HBM_SPEC] * n,
        input_output_aliases={i: i for i in range(n)},
        scratch_shapes=[pltpu.SemaphoreType.DMA((3 * n,)), pltpu.SemaphoreType.DMA((3 * n,))],
    )(*lands))


def _pair_sum(pieces, got, core, name):
    _, rows, cols = pieces.shape
    tr, tc = _tile_2d(rows, cols)

    def body(core_ref, mine_ref, got_ref, out_ref):
        del core_ref
        out_ref[...] = (mine_ref[...].astype(F32) + got_ref[...].astype(F32)).astype(out_ref.dtype)

    return pl.pallas_call(
        body, name=name,
        out_shape=jax.ShapeDtypeStruct((N_CHIP, rows, cols), pieces.dtype),
        grid_spec=pltpu.PrefetchScalarGridSpec(
            num_scalar_prefetch=1, grid=(N_CHIP, rows // tr, cols // tc),
            in_specs=[pl.BlockSpec((None, tr, tc), lambda ch, i, j, core: (2 * ch + core[0], i, j)),
                      pl.BlockSpec((None, tr, tc), lambda ch, i, j, core: (ch, i, j))],
            out_specs=pl.BlockSpec((None, tr, tc), lambda ch, i, j, core: (ch, i, j))),
        compiler_params=_params("parallel", "parallel", "parallel"),
    )(core, pieces, got)


def _tile_2d(rows, cols, budget=2 * 1024 * 1024):
    want = max(16, budget // (4 * cols))
    if rows <= want:
        return rows, cols
    t = (want // 16) * 16
    while t >= 16:
        if rows % t == 0:
            return t, cols
        t -= 16
    return rows, _tile(cols, max(LANES, budget // (4 * rows)))


def _adamw_math(w, g, m, v):
    m = ADAM_B1 * m + (1.0 - ADAM_B1) * g
    v = ADAM_B2 * v + (1.0 - ADAM_B2) * (g * g)
    m_hat = m / (1.0 - ADAM_B1 ** ADAM_STEP)
    v_hat = v / (1.0 - ADAM_B2 ** ADAM_STEP)
    delta = -ADAM_LR * (m_hat / (jnp.sqrt(v_hat) + ADAM_EPS) + ADAM_WD * w)
    return delta, m, v


def _adamw_shard(w, m, v, sums, got, chip, name):
    rows, cols = w.shape
    tr, tc = _tile_2d(rows, cols, budget=1024 * 1024)

    def body(chip_ref, w_ref, m_ref, v_ref, own_ref, got_ref, g_out, d_out, m_out, v_out):
        del chip_ref
        g = own_ref[...].astype(F32)
        for k in range(3):
            g = g + got_ref[k].astype(F32)
        delta, m_new, v_new = _adamw_math(w_ref[...], g, m_ref[...], v_ref[...])
        g_out[...] = g
        d_out[...] = delta
        m_out[...] = m_new
        v_out[...] = v_new

    blk = pl.BlockSpec((tr, tc), lambda i, j, chip: (i, j))
    return pl.pallas_call(
        body, name=name,
        out_shape=[jax.ShapeDtypeStruct((rows, cols), F32)] * 4,
        grid_spec=pltpu.PrefetchScalarGridSpec(
            num_scalar_prefetch=1, grid=(rows // tr, cols // tc),
            in_specs=[blk, blk, blk,
                      pl.BlockSpec((None, tr, tc), lambda i, j, chip: (chip[0], i, j)),
                      pl.BlockSpec((3, tr, tc), lambda i, j, chip: (0, i, j))],
            out_specs=[blk] * 4),
        compiler_params=_params("parallel", "parallel"),
    )(chip, w, m, v, sums, got)


def _adamw_replicated(w, m, v, parts, name):
    rows, cols = w.shape

    def body(w_ref, m_ref, v_ref, parts_ref, g_out, d_out, m_out, v_out):
        g = parts_ref[0]
        for d in range(1, N_DEV):
            g = g + parts_ref[d]
        delta, m_new, v_new = _adamw_math(w_ref[...], g, m_ref[...], v_ref[...])
        g_out[...] = g
        d_out[...] = delta
        m_out[...] = m_new
        v_out[...] = v_new

    return pl.pallas_call(
        body, name=name,
        out_shape=[jax.ShapeDtypeStruct((rows, cols), F32)] * 4,
        compiler_params=_params(),
    )(w, m, v, parts)


NN = ((1,), (0,))
NT = ((1,), (1,))
TN = ((0,), (0,))


ANY_SPEC = pl.BlockSpec(memory_space=pl.ANY)


def _mm(name, a, b, *, dims, grid, a_spec, b_spec, outs, extra=(), extra_specs=(), epilogue=None, acc_shape=None,
        deps=(), b_pieces=0, side=None):
    nk = grid[2]
    n_extra, n_out = len(extra), len(outs)
    side_ins = side["ins"] if side else []
    side_outs = side["outs"] if side else []
    first_side = 2 + n_extra + len(deps)
    first_out = first_side + len(side_ins)
    assert not side or nk == 1

    assert not side or (epilogue is None and n_out == 1 and not b_pieces)
    SIDE_SLICES = 4

    def body(*refs):
        dot = lambda lhs, rhs: lax.dot_general(lhs.astype(BF16), rhs.astype(BF16), (dims, ((), ())),
                                               preferred_element_type=F32)
        if side:
            refs = refs[1:]
            a_ref, b_ref, out_ref = refs[0], refs[1], refs[first_out]
            ins, sides = refs[first_side:first_out], refs[first_out + 1:first_out + 1 + len(side_outs)]
            rows = out_ref.shape[0] // SIDE_SLICES
            srows = sides[0].shape[-2] // SIDE_SLICES
            for part in range(SIDE_SLICES):
                rs = slice(part * srows, (part + 1) * srows)
                for val, o in zip(side["fn"](*[r[..., rs, :] for r in ins]), sides):
                    o[rs, :] = val.astype(o.dtype)
                ms = slice(part * rows, (part + 1) * rows)
                lhs = a_ref[:, ms] if dims == TN else a_ref[ms, :]
                out_ref[ms, :] = dot(lhs, b_ref[...]).astype(out_ref.dtype)
            return
        a_ref, b_ref = refs[0], refs[1]
        ex = refs[2:2 + n_extra]
        out_refs = refs[first_out:first_out + n_out]
        if b_pieces:
            ks = b_ref.shape[-1]
            part = dot(a_ref[:, 0:ks], b_ref[0])
            for piece in range(1, b_pieces):
                part = part + dot(a_ref[:, piece * ks:(piece + 1) * ks], b_ref[piece])
        else:
            part = dot(a_ref[...], b_ref[...])

        def finish(acc):
            res = epilogue(acc, *[e[...] for e in ex]) if epilogue else (acc,)
            for val, o in zip(res, out_refs):
                o[...] = val.astype(o.dtype)

        if nk == 1:
            finish(part)
        else:
            acc_ref = refs[-1]
            k = pl.program_id(2)

            @pl.when(k == 0)
            def _():
                acc_ref[...] = part

            @pl.when(k > 0)
            def _():
                acc_ref[...] += part

            @pl.when(k == nk - 1)
            def _():
                finish(acc_ref[...])

    scratch = [pltpu.VMEM(acc_shape, F32)] if nk > 1 else []
    out_shape = [jax.ShapeDtypeStruct(s, d) for s, d, _ in outs]
    in_specs = [a_spec, b_spec, *extra_specs] + [ANY_SPEC] * len(deps)
    out_specs = [sp for _, _, sp in outs]
    if not side:
        return list(pl.pallas_call(
            body, name=name, out_shape=out_shape, grid=grid, in_specs=in_specs, out_specs=out_specs,
            scratch_shapes=scratch, compiler_params=_params("parallel", "parallel", "arbitrary"),
        )(a, b, *extra, *deps))

    def with_scalars(spec):
        if spec.index_map is None:
            return spec
        return pl.BlockSpec(spec.block_shape, lambda i, j, k, s, f=spec.index_map: f(i, j, k))

    def by_step(block, index_map):
        return pl.BlockSpec(block, lambda i, j, k, s: index_map((i * grid[1] + j) * grid[2] + k, s))

    return list(pl.pallas_call(
        body, name=name,
        out_shape=out_shape + [jax.ShapeDtypeStruct(s, d) for s, d, _, _ in side_outs],
        grid_spec=pltpu.PrefetchScalarGridSpec(
            num_scalar_prefetch=1, grid=grid,
            in_specs=[with_scalars(sp) for sp in in_specs] + [by_step(blk, im) for _, blk, im in side_ins],
            out_specs=[with_scalars(sp) for sp in out_specs] + [by_step(blk, im) for _, _, blk, im in side_outs],
            scratch_shapes=scratch),
        compiler_params=_params("arbitrary", "arbitrary", "arbitrary"),
    )(side["scalars"], a, b, *extra, *deps, *[arr for arr, _, _ in side_ins]))


def _side_pair_sum(pieces, got, place):
    _, rows, cols = pieces.shape

    def build(steps):
        per_chip = steps // N_CHIP
        tr = rows // per_chip
        assert per_chip * N_CHIP == steps and tr * per_chip == rows and tr % 16 == 0
        blk = (None, tr, cols)
        here = lambda st, s: (st // per_chip, st % per_chip, 0)
        return dict(scalars=place,
                    ins=[(pieces, blk, lambda st, s: (2 * (st // per_chip) + s[0], st % per_chip, 0)), (got, blk, here)],
                    outs=[((N_CHIP, rows, cols), pieces.dtype, blk, here)],
                    fn=lambda mine, other: (mine.astype(F32) + other.astype(F32),))
    return build


def _side_adamw(w, m, v, sums, got, place):
    rows, cols = w.shape

    def build(steps):
        tr = rows // steps
        assert tr * steps == rows and tr % 16 == 0
        blk, here = (tr, cols), (lambda st, s: (st, 0))

        def fn(w, m, v, own, got):
            g = own.astype(F32)
            for k in range(3):
                g = g + got[k].astype(F32)
            delta, m_new, v_new = _adamw_math(w, g, m, v)
            return g, delta, m_new, v_new

        return dict(scalars=place,
                    ins=[(w, blk, here), (m, blk, here), (v, blk, here),
                         (sums, (None, tr, cols), lambda st, s: (s[1], st, 0)),
                         (got, (3, tr, cols), lambda st, s: (0, st, 0))],
                    outs=[((rows, cols), F32, blk, here)] * 4, fn=fn)
    return build


def _mm_nn(name, a, b, out_dtype=F32, tm=1024, tn=1024, tk=2048, epilogue=None, n_out=1, extra=(), deps=(), side=None):
    m, kd = a.shape
    n = b.shape[1]
    tm, tn, tk = _tile(m, tm), _tile(n, tn), _tile(kd, tk)
    dts = out_dtype if isinstance(out_dtype, (list, tuple)) else [out_dtype] * n_out
    o_spec = pl.BlockSpec((tm, tn), lambda i, j, k: (i, j))
    grid = (m // tm, n // tn, kd // tk)
    return _mm(name, a, b, dims=NN, grid=grid,
               a_spec=pl.BlockSpec((tm, tk), lambda i, j, k: (i, k)),
               b_spec=pl.BlockSpec((tk, tn), lambda i, j, k: (k, j)),
               outs=[((m, n), dt, o_spec) for dt in dts], extra=extra, extra_specs=[o_spec] * len(extra),
               epilogue=epilogue, acc_shape=(tm, tn), deps=deps, side=side and side(grid[0] * grid[1] * grid[2]))


def _mm_nn_pieces(name, a, b, out_dtype=F32, tm=1024, tk=2048, epilogue=None, n_out=1):
    m, kd = a.shape
    npc, _, ns = b.shape
    tm, tk = _tile(m, tm), _tile(kd, tk)
    tn = _tile(ns, 1024)
    r = ns // tn
    dts = out_dtype if isinstance(out_dtype, (list, tuple)) else [out_dtype] * n_out
    o_spec = pl.BlockSpec((tm, tn), lambda i, j, k: (i, j))
    return _mm(name, a, b, dims=NN, grid=(m // tm, npc * r, kd // tk),
               a_spec=pl.BlockSpec((tm, tk), lambda i, j, k: (i, k)),
               b_spec=pl.BlockSpec((None, tk, tn), lambda i, j, k: (j // r, k, j % r)),
               outs=[((m, npc * ns), dt, o_spec) for dt in dts], epilogue=epilogue, acc_shape=(tm, tn))


def _mm_nt(name, a, b, out_dtype=F32, tm=1024, tn=1024, tk=2048, epilogue=None, extra=(), n_rows=None, deps=()):
    m, kd = a.shape
    n = n_rows or b.shape[0]
    tm, tn, tk = _tile(m, tm), _tile(n, tn), _tile(kd, tk)
    o_spec = pl.BlockSpec((tm, tn), lambda i, j, k: (i, j))
    return _mm(name, a, b, dims=NT, grid=(m // tm, n // tn, kd // tk),
               a_spec=pl.BlockSpec((tm, tk), lambda i, j, k: (i, k)),
               b_spec=pl.BlockSpec((tn, tk), lambda i, j, k: (j, k)),
               outs=[((m, n), out_dtype, o_spec)], extra=extra, extra_specs=[o_spec] * len(extra),
               epilogue=epilogue, acc_shape=(tm, tn), deps=deps)[0]


def _mm_nt_pieces(name, a, b, out_dtype=F32, tm=1024, tn=256, deps=()):
    m = a.shape[0]
    npc, n, ks = b.shape
    tm, tn = _tile(m, tm), _tile(n, tn)
    return _mm(name, a, b, dims=NT, grid=(m // tm, n // tn, 1),
               a_spec=pl.BlockSpec((tm, npc * ks), lambda i, j, k: (i, 0)),
               b_spec=pl.BlockSpec((npc, tn, ks), lambda i, j, k: (0, j, 0)),
               outs=[((m, n), out_dtype, pl.BlockSpec((tm, tn), lambda i, j, k: (i, j)))],
               deps=deps, b_pieces=npc)[0]


def _mm_tn(name, a, b, out_dtype=BF16, tm=1024, tn=1024, pieces=0, deps=(), side=None):
    t, m = a.shape
    n = b.shape[1]
    tm = _tile(m, tm)
    if pieces:
        ns = n // pieces
        tn = _tile(ns, tn)
        r = ns // tn
        out = ((pieces, m, ns), out_dtype, pl.BlockSpec((None, tm, tn), lambda i, j, k: (j // r, i, j % r)))
    else:
        tn = _tile(n, tn)
        out = ((m, n), out_dtype, pl.BlockSpec((tm, tn), lambda i, j, k: (i, j)))
    grid = (m // tm, n // tn, 1)
    res = _mm(name, a, b, dims=TN, grid=grid,
              a_spec=pl.BlockSpec((t, tm), lambda i, j, k: (0, i)),
              b_spec=pl.BlockSpec((t, tn), lambda i, j, k: (0, j)),
              outs=[out], deps=deps, side=side and side(grid[0] * grid[1]))
    return res if side else res[0]


def _rms_f(x, g):
    r = lax.rsqrt(jnp.mean(x * x, axis=-1, keepdims=True) + EPS)
    xh = x * r
    return xh * g, xh, r


def _rms_b(dy, g, xh, r):
    dxh = dy * g
    dx = r * (dxh - xh * jnp.mean(dxh * xh, axis=-1, keepdims=True))
    dg = jnp.sum(dy * xh, axis=0, keepdims=True)
    return dx, dg


def _rowwise(name, fn, row_ins, vec_ins, row_outs, vec_outs, deps=()):
    t = row_ins[0].shape[0]
    tr = min(ROW_TILE, t)
    nr, nv, no = len(row_ins), len(vec_ins), len(row_outs)
    first_out = nr + nv + len(deps)

    def body(*refs):
        ri, vi = refs[:nr], refs[nr:nr + nv]
        ro, vo = refs[first_out:first_out + no], refs[first_out + no:]
        routs, vouts = fn(*[r[...] for r in ri], *[v[...] for v in vi])
        for o, val in zip(ro, routs):
            o[...] = val.astype(o.dtype)
        i = pl.program_id(0)
        for o, val in zip(vo, vouts):
            @pl.when(i == 0)
            def _():
                o[...] = val

            @pl.when(i > 0)
            def _():
                o[...] += val

    res = pl.pallas_call(
        body, name=name,
        out_shape=[jax.ShapeDtypeStruct((t, w), d) for w, d in row_outs]
        + [jax.ShapeDtypeStruct((1, w), F32) for w in vec_outs],
        grid=(t // tr,),
        in_specs=[pl.BlockSpec((tr, a.shape[1]), lambda i: (i, 0)) for a in row_ins]
        + [pl.BlockSpec((1, a.shape[1]), lambda i: (0, 0)) for a in vec_ins] + [ANY_SPEC] * len(deps),
        out_specs=[pl.BlockSpec((tr, w), lambda i: (i, 0)) for w, _ in row_outs]
        + [pl.BlockSpec((1, w), lambda i: (0, 0)) for w in vec_outs],
        compiler_params=_params("arbitrary"),
    )(*row_ins, *vec_ins, *deps)
    return list(res)


def _norm_fwd(x, g, name, deps=()):
    d = x.shape[1]
    return _rowwise(name, lambda x, g: ((_rms_f(x, g)[0],), ()), [x], [g], [(d, BF16)], [], deps=deps)[0]


def _norm_bwd(x, dxn, dres, g, name):
    d = x.shape[1]

    def fn(x, dxn, dres, g):
        _, xh, r = _rms_f(x, g)
        dx, dg = _rms_b(dxn, g, xh, r)
        return (dres + dx,), (dg,)

    return _rowwise(name, fn, [x, dxn, dres], [g], [(d, F32)], [d])


def _block_fwd(h_prev, y, g_y, g_n, name):
    d = h_prev.shape[1]

    def fn(h_prev, y, g_y, g_n):
        h = h_prev + _rms_f(y, g_y)[0]
        return (h, _rms_f(h, g_n)[0]), ()

    return _rowwise(name, fn, [h_prev, y], [g_y, g_n], [(d, F32), (d, BF16)], [])


def _block_bwd(h, y, d_up, d_n, g_y, g_n, name, deps=()):
    d = h.shape[1]

    def fn(h, y, d_up, d_n, g_y, g_n):
        _, hh, hr = _rms_f(h, g_n)
        dh_n, dg_n = _rms_b(d_n, g_n, hh, hr)
        dh = d_up + dh_n
        _, yh, yr = _rms_f(y, g_y)
        dy, dg_y = _rms_b(dh, g_y, yh, yr)
        return (dh, dy), (dg_y, dg_n)

    return _rowwise(name, fn, [h, y, d_up, d_n], [g_y, g_n], [(d, F32), (d, BF16)], [d, d], deps=deps)


def _loss_head(h2, z, e, target, g, name):
    d = h2.shape[1]

    def fn(h2, z, e, target, g):
        gate = jax.nn.sigmoid(z)
        y, xh, r = _rms_f(e * gate, g)
        diff = h2 + y - target
        loss = 0.5 * jnp.sum(jnp.mean(diff * diff, axis=-1, keepdims=True), axis=0, keepdims=True)
        dh3 = diff / d
        deg, dg = _rms_b(dh3, g, xh, r)
        dz = deg * e * gate * (1.0 - gate)
        de = deg * gate
        return (dh3, dz, de), (dg, jnp.broadcast_to(loss, (1, LANES)))

    return _rowwise(name, fn, [h2, z, e, target], [g], [(d, F32), (d, BF16), (d, BF16)], [d, LANES])


def _shift_down(x, s, rows):
    return jnp.where(rows >= s, pltpu.roll(x, s, axis=0), 0.0)


def _shift_up(x, s, rows):
    t = x.shape[0]
    return jnp.where(rows < t - s, pltpu.roll(x, t - s, axis=0), 0.0)


def _conv_pre(x, w, rows):
    pre = x * w[QK_CONV - 1:QK_CONV, :]
    for s in range(1, QK_CONV):
        pre = pre + _shift_down(x, s, rows) * w[QK_CONV - 1 - s:QK_CONV - s, :]
    return pre


def _conv_fwd(proj_a, w, start, name):
    t = proj_a.shape[0]
    width = w.shape[1]
    tc = _tile(width, 256)
    off = start // tc

    def body(x_ref, w_ref, y_ref):
        x = x_ref[...]
        rows = lax.broadcasted_iota(jnp.int32, x.shape, 0)
        pre = _conv_pre(x, w_ref[...], rows)
        y_ref[...] = pre * jax.nn.sigmoid(pre)

    return pl.pallas_call(
        body, name=name,
        out_shape=jax.ShapeDtypeStruct((t, width), F32),
        grid=(width // tc,),
        in_specs=[pl.BlockSpec((t, tc), lambda j: (0, off + j)), pl.BlockSpec((QK_CONV, tc), lambda j: (0, j))],
        out_specs=pl.BlockSpec((t, tc), lambda j: (0, j)),
        compiler_params=_params("parallel"),
    )(proj_a, w)


def _conv_bwd(proj_a, w, dy, start, name):
    t = proj_a.shape[0]
    width = w.shape[1]
    tc = _tile(width, 256)
    off = start // tc

    def body(x_ref, w_ref, dy_ref, dx_ref, dw_ref):
        x, w = x_ref[...], w_ref[...]
        rows = lax.broadcasted_iota(jnp.int32, x.shape, 0)
        pre = _conv_pre(x, w, rows)
        sig = jax.nn.sigmoid(pre)
        dpre = dy_ref[...] * (sig * (1.0 + pre * (1.0 - sig)))
        dx = dpre * w[QK_CONV - 1:QK_CONV, :]
        dws = [jnp.sum(dpre * x, axis=0, keepdims=True)]
        for s in range(1, QK_CONV):
            dx = dx + _shift_up(dpre, s, rows) * w[QK_CONV - 1 - s:QK_CONV - s, :]
            dws.append(jnp.sum(dpre * _shift_down(x, s, rows), axis=0, keepdims=True))
        dx_ref[...] = dx.astype(dx_ref.dtype)
        for s in range(QK_CONV):
            dw_ref[QK_CONV - 1 - s:QK_CONV - s, :] = dws[s]

    return pl.pallas_call(
        body, name=name,
        out_shape=[jax.ShapeDtypeStruct((t, width), BF16), jax.ShapeDtypeStruct((QK_CONV, width), F32)],
        grid=(width // tc,),
        in_specs=[pl.BlockSpec((t, tc), lambda j: (0, off + j)), pl.BlockSpec((QK_CONV, tc), lambda j: (0, j)),
                  pl.BlockSpec((t, tc), lambda j: (0, j))],
        out_specs=[pl.BlockSpec((t, tc), lambda j: (0, j)), pl.BlockSpec((QK_CONV, tc), lambda j: (0, j))],
        compiler_params=_params("parallel"),
    )(proj_a, w, dy)


def _dot(a, b, dims):
    return lax.dot_general(a.astype(BF16), b.astype(BF16), (dims, ((), ())), preferred_element_type=F32)


def _dot_split(ones, x):
    hi = x.astype(BF16)
    rest = x - hi.astype(F32)
    mid = rest.astype(BF16)
    lo = (rest - mid.astype(F32)).astype(BF16)
    dot = lambda part: lax.dot_general(ones, part, (NN, ((), ())), preferred_element_type=F32)
    return dot(hi) + dot(mid) + dot(lo)


def _chunk_iotas():
    rows = lax.broadcasted_iota(jnp.int32, (CHUNK, CHUNK), 0)
    cols = lax.broadcasted_iota(jnp.int32, (CHUNK, CHUNK), 1)
    return rows, cols


def _gates_fwd(proj_b, bias, col_block, heads, name):
    t = proj_b.shape[0]

    def body(x_ref, b_ref, gb_ref, gbt_ref):
        rows, cols = _chunk_iotas()
        tri = (cols <= rows).astype(BF16)
        for ci in range(t // CHUNK):
            sl = slice(ci * CHUNK, (ci + 1) * CHUNK)
            pre = x_ref[sl, :] + b_ref[...]
            log_f = jnp.minimum(pre, 0.0) - jnp.log(1.0 + jnp.exp(-jnp.abs(pre)))
            log_f = jnp.where((cols >= heads) & (cols < 2 * heads), log_f, 0.0)
            tile = jnp.where(cols < heads, pre, _dot_split(tri, log_f))
            gb_ref[sl, :] = tile
            gbt_ref[:, sl] = tile.T

    return pl.pallas_call(
        body, name=name,
        out_shape=[jax.ShapeDtypeStruct((t, LANES), F32), jax.ShapeDtypeStruct((LANES, t), F32)],
        grid=(1,),
        in_specs=[pl.BlockSpec((t, LANES), lambda i: (0, col_block)), pl.BlockSpec((1, LANES), lambda i: (0, 0))],
        out_specs=[pl.BlockSpec((t, LANES), lambda i: (0, 0)), pl.BlockSpec((LANES, t), lambda i: (0, 0))],
        compiler_params=_params("arbitrary"),
    )(proj_b, bias)


def _gate_columns(gb, heads, name):
    t = gb.shape[0]

    def body(gb_ref, out_ref):
        g = gb_ref[...]
        lane = lax.broadcasted_iota(jnp.int32, g.shape, 1)
        column = jnp.sum(jnp.where(lane == pl.program_id(0), g, 0.0), axis=-1, keepdims=True)
        out_ref[...] = jnp.broadcast_to(column, out_ref.shape)

    return pl.pallas_call(
        body, name=name,
        out_shape=jax.ShapeDtypeStruct((2 * heads, t, LANES), F32),
        grid=(2 * heads,),
        in_specs=[pl.BlockSpec((t, LANES), lambda h: (0, 0))],
        out_specs=pl.BlockSpec((None, t, LANES), lambda h: (h, 0, 0)),
        compiler_params=_params("parallel"),
    )(gb)


def _gates_bwd(proj_b, bias, dcol, drow, col_block, heads, name):
    t = proj_b.shape[0]

    def body(x_ref, b_ref, dc_ref, dr_ref, dx_ref, db_ref):
        rows, cols = _chunk_iotas()
        later = (cols >= rows).astype(BF16)
        total = jnp.zeros((1, LANES), F32)
        for ci in range(t // CHUNK):
            sl = slice(ci * CHUNK, (ci + 1) * CHUNK)
            by_row = dr_ref[:, sl].T
            d = dc_ref[sl, :] + jnp.where(cols < heads, by_row,
                                          jnp.where(cols < 2 * heads, -pltpu.roll(by_row, heads, axis=1), 0.0))
            d_log_f = _dot_split(later, jnp.where(cols >= heads, d, 0.0))
            pre = x_ref[sl, :] + b_ref[...]
            dx = jnp.where(cols < heads, d, jnp.where(cols < 2 * heads, d_log_f * jax.nn.sigmoid(-pre), 0.0))
            dx_ref[sl, :] = dx.astype(dx_ref.dtype)
            total = total + jnp.sum(dx, axis=0, keepdims=True)
        db_ref[...] = total

    return pl.pallas_call(
        body, name=name,
        out_shape=[jax.ShapeDtypeStruct((t, LANES), BF16), jax.ShapeDtypeStruct((1, LANES), F32)],
        grid=(1,),
        in_specs=[pl.BlockSpec((t, LANES), lambda i: (0, col_block)), pl.BlockSpec((1, LANES), lambda i: (0, 0)),
                  pl.BlockSpec((t, LANES), lambda i: (0, 0)), pl.BlockSpec((LANES, t), lambda i: (0, 0))],
        out_specs=[pl.BlockSpec((t, LANES), lambda i: (0, 0)), pl.BlockSpec((1, LANES), lambda i: (0, 0))],
        compiler_params=_params("arbitrary"),
    )(proj_b, bias, dcol, drow)


def _gate_views(li_ref, b_ref, gbt_ref, s, head, heads, sl):
    li_c, b_c = li_ref[s, sl, :], b_ref[s, sl, :]
    li_r = jnp.broadcast_to(gbt_ref[head, :, sl], (CHUNK, CHUNK))
    b_r = jnp.broadcast_to(gbt_ref[heads + head, :, sl], (CHUNK, CHUNK))
    return li_c, b_c, li_r, b_r


def _chunk_state(kt, k, v, li_c, b_c, ct, nrow, m_prev):
    b_tot = b_c[CHUNK - 1:CHUNK, 0:1]
    a = b_tot - b_c[:, 0:1] + li_c[:, 0:1]
    m_new = jnp.maximum(b_tot + m_prev, jnp.max(a, axis=0, keepdims=True))
    dec = jnp.exp(b_tot + m_prev - m_new)
    w = jnp.exp(a - m_new)
    wv = w * v
    ct_new = dec * ct + _dot(kt, wv, NN)
    n_new = dec * nrow + jnp.sum(w * k, axis=0, keepdims=True)
    return ct_new, n_new, m_new, dec, w, wv


def _chunk_scores(q, k, b_c, li_r, b_r, m_prev):
    rows, cols = _chunk_iotas()
    dlog = jnp.where(cols <= rows, b_c - b_r + li_r, -jnp.inf)
    inter = b_c[:, 0:1] + m_prev
    mt = jnp.maximum(inter, jnp.max(dlog, axis=-1, keepdims=True))
    dw = jnp.exp(dlog - mt)
    iw = jnp.exp(inter - mt)
    a = _dot(q, k, NT)
    return dw, iw, mt, a, a * dw


def _to_row(column):
    rows, cols = _chunk_iotas()
    return jnp.sum(jnp.where(rows == cols, jnp.broadcast_to(column, (CHUNK, CHUNK)), 0.0), axis=0, keepdims=True)


def _head_out(h, o, g):
    cell, hh, r = _rms_f(h, g)
    sig = jax.nn.sigmoid(o)
    return sig * cell, (cell, hh, r, sig)


HEADS_PER_STEP = 2


def _mlstm_fwd(qc, kc, kct, proj_a, gcols, gbt, g_mlstm, heads, d_model, name):
    t = qc.shape[0]
    dh = g_mlstm.shape[1] // heads
    nc = t // CHUNK
    scale = dh ** -0.5
    hp = HEADS_PER_STEP

    def body(q_ref, k_ref, kt_ref, v_ref, o_ref, li_ref, b_ref, gbt_ref, g_ref, out_ref, ct_ref, n_ref, m_ref):
        step = pl.program_id(0)
        ct_ref[...] = jnp.zeros_like(ct_ref)
        n_ref[...] = jnp.zeros_like(n_ref)
        m_ref[...] = jnp.zeros_like(m_ref)

        def chunk(ci, carry):
            sl = pl.ds(pl.multiple_of(ci * CHUNK, CHUNK), CHUNK)
            for s in range(hp):
                cs = slice(s * dh, (s + 1) * dh)
                q, k, v, kt = q_ref[sl, cs], k_ref[sl, cs] * scale, v_ref[sl, cs], kt_ref[cs, sl] * scale
                li_c, b_c, li_r, b_r = _gate_views(li_ref, b_ref, gbt_ref, s, step * hp + s, heads, sl)
                ct, nrow, m_prev = ct_ref[s], n_ref[s, 0:1, :], m_ref[s, 0:1, 0:1]
                dw, iw, mt, _, sm = _chunk_scores(q, k, b_c, li_r, b_r, m_prev)
                num = _dot(jnp.concatenate([iw * q, sm], axis=1), jnp.concatenate([ct, v], axis=0), NN)
                den = iw * jnp.sum(q * nrow, axis=-1, keepdims=True) + jnp.sum(sm, axis=-1, keepdims=True)
                h = num / jnp.maximum(jnp.abs(den), jnp.exp(-mt))
                y, _ = _head_out(h, o_ref[sl, cs], g_ref[:, cs])
                out_ref[sl, cs] = y.astype(out_ref.dtype)
                ct_new, n_new, m_new, _, _, _ = _chunk_state(kt, k, v, li_c, b_c, ct, nrow, m_prev)
                ct_ref[s] = ct_new
                n_ref[s, 0:1, :] = n_new
                m_ref[s, 0:1, 0:1] = m_new
            return carry

        lax.fori_loop(0, nc, chunk, 0, unroll=2)

    col = lambda off: pl.BlockSpec((t, hp * dh), lambda h: (0, off // hp + h))
    slab = lambda off: pl.BlockSpec((hp, t, LANES), lambda h: (off // hp + h, 0, 0))
    return pl.pallas_call(
        body, name=name,
        out_shape=jax.ShapeDtypeStruct((t, d_model), BF16),
        grid=(heads // hp,),
        in_specs=[col(0), col(0), pl.BlockSpec((hp * dh, t), lambda h: (h, 0)), col(2 * heads), col(3 * heads),
                  slab(0), slab(heads), pl.BlockSpec((2 * heads, 1, t), lambda h: (0, 0, 0)),
                  pl.BlockSpec((1, hp * dh), lambda h: (0, h))],
        out_specs=pl.BlockSpec((t, hp * dh), lambda h: (0, h)),
        scratch_shapes=[pltpu.VMEM((hp, dh, dh), F32), pltpu.VMEM((hp, 8, dh), F32), pltpu.VMEM((hp, 8, LANES), F32)],
        compiler_params=_params("arbitrary"),
    )(qc, kc, kct, proj_a, proj_a, gcols, gcols, gbt, g_mlstm)


def _mlstm_bwd(qc, kc, qct, kct, proj_a, gcols, gbt, g_mlstm, dhcat, heads, name, dep):
    t = qc.shape[0]
    wm = g_mlstm.shape[1]
    dh = wm // heads
    nc = t // CHUNK
    scale = dh ** -0.5

    def body(q_ref, k_ref, qt_ref, kt_ref, v_ref, o_ref, li_ref, b_ref, gbt_ref, g_ref, dy_ref, dep_ref,
             dq_ref, dk_ref, dv_ref, do_ref, dcol_ref, drow_ref, dg_ref,
             ct_all, n_all, m_all, dct_ref, dn_ref):
        del dep_ref
        head = pl.program_id(0)
        g = g_ref[...]

        @pl.when(head == 0)
        def _():
            dcol_ref[...] = jnp.zeros_like(dcol_ref)
            drow_ref[...] = jnp.zeros_like(drow_ref)

        def load(ci):
            sl = pl.ds(pl.multiple_of(ci * CHUNK, CHUNK), CHUNK)
            return (sl, q_ref[sl, :], k_ref[sl, :] * scale, v_ref[sl, :], kt_ref[:, sl] * scale) \
                + _gate_views(li_ref, b_ref, gbt_ref, 0, head, heads, sl)

        def fwd(ci, carry):
            ct, nrow, m_prev = carry
            ct_all[ci] = ct
            n_all[ci, 0:1, :] = nrow
            m_all[ci, 0:1, 0:1] = m_prev
            _, _, k, v, kt, li_c, b_c, _, _ = load(ci)
            return _chunk_state(kt, k, v, li_c, b_c, ct, nrow, m_prev)[:3]

        lax.fori_loop(0, nc, fwd, (jnp.zeros((dh, dh), F32), jnp.zeros((1, dh), F32), jnp.zeros((1, 1), F32)))

        dct_ref[...] = jnp.zeros_like(dct_ref)
        dn_ref[...] = jnp.zeros_like(dn_ref)
        dg_ref[...] = jnp.zeros_like(dg_ref)
        rows, cols = _chunk_iotas()
        last = lax.broadcasted_iota(jnp.int32, (CHUNK, 1), 0) == CHUNK - 1
        lane = lax.broadcasted_iota(jnp.int32, (CHUNK, LANES), 1)

        def bwd(step, carry):
            ci = nc - 1 - step
            sl, q, k, v, kt, li_c, b_c, li_r, b_r = load(ci)
            qt = qt_ref[:, sl]
            ct, nrow, m_prev = ct_all[ci], n_all[ci, 0:1, :], m_all[ci, 0:1, 0:1]
            dw, iw, mt, a, sm = _chunk_scores(q, k, b_c, li_r, b_r, m_prev)
            qc_ = _dot(q, ct, NN)
            qn = jnp.sum(q * nrow, axis=-1, keepdims=True)
            num = iw * qc_ + _dot(sm, v, NN)
            den = iw * qn + jnp.sum(sm, axis=-1, keepdims=True)
            floor = jnp.exp(-mt)
            dn = jnp.maximum(jnp.abs(den), floor)
            h = num / dn
            _, (cell, hh, r, sig) = _head_out(h, o_ref[sl, :], g)
            dy = dy_ref[sl, :]
            dcell = dy * sig
            do_ref[sl, :] = (dy * cell * sig * (1.0 - sig)).astype(do_ref.dtype)
            dh_, dg = _rms_b(dcell, g, hh, r)
            dg_ref[...] += dg
            dnum = dh_ / dn
            ddn = -jnp.sum(dh_ * h, axis=-1, keepdims=True) / dn
            dden = jnp.where(jnp.abs(den) >= floor, ddn * jnp.sign(den), 0.0)
            idn = iw * dnum
            idd = iw * dden
            dw_t = jnp.exp(jnp.where(cols >= rows, b_r - b_c + li_c, -jnp.inf) - _to_row(mt))
            sm_t = _dot(k, q, NT) * dw_t
            dsm = _dot(dnum, v, NT) + dden
            da_ = dsm * dw
            da_t = (_dot(v, dnum, NT) + _to_row(dden)) * dw_t
            e = da_ * a
            dq = _dot(idn, ct, NT) + _dot(da_, k, NN) + idd * nrow
            dk = _dot(da_t, q, NN)
            dv = _dot(sm_t, dnum, NN)
            dct = _dot(qt, idn, NN)
            dnr = jnp.sum(q * idd, axis=0, keepdims=True)
            diw = jnp.sum(dnum * qc_, axis=-1, keepdims=True) + dden * qn
            db = jnp.sum(e, axis=-1, keepdims=True) + diw * iw
            _, _, _, dec, w, wv = _chunk_state(kt, k, v, li_c, b_c, ct, nrow, m_prev)
            dct_new, dn_new = dct_ref[...], dn_ref[0:1, :]
            ddec = jnp.sum(jnp.sum(dct_new * ct, axis=-1, keepdims=True), axis=0, keepdims=True) \
                + jnp.sum(dn_new * nrow, axis=-1, keepdims=True)
            dk = dk + _dot(wv, dct_new, NT) + w * dn_new
            dwv = _dot(k, dct_new, NN)
            dv = dv + w * dwv
            dwt = jnp.sum(dwv * v, axis=-1, keepdims=True) + jnp.sum(k * dn_new, axis=-1, keepdims=True)
            da = dwt * w
            dbtot = jnp.sum(da, axis=0, keepdims=True) + ddec * dec
            db = db - da + jnp.where(last, dbtot, 0.0)
            dct_ref[...] = dec * dct_new + dct
            dn_ref[0:1, :] = dec * dn_new + dnr
            dq_ref[sl, :] = dq
            dk_ref[sl, :] = dk * scale
            dv_ref[sl, :] = dv.astype(dv_ref.dtype)
            dcol_ref[sl, :] += jnp.where(lane == head, da, 0.0) + jnp.where(lane == heads + head, db, 0.0)
            drow_ref[head, :, sl] = jnp.sum(e, axis=0, keepdims=True)
            return carry

        lax.fori_loop(0, nc, bwd, 0, unroll=4)

    col = lambda off: pl.BlockSpec((t, dh), lambda h: (0, off + h))
    rowb = pl.BlockSpec((dh, t), lambda h: (h, 0))
    slab = lambda off: pl.BlockSpec((1, t, LANES), lambda h: (off + h, 0, 0))
    vec = pl.BlockSpec((1, dh), lambda h: (0, h))
    return pl.pallas_call(
        body, name=name,
        out_shape=[jax.ShapeDtypeStruct((t, wm), F32)] * 2 + [jax.ShapeDtypeStruct((t, wm), BF16)] * 2
        + [jax.ShapeDtypeStruct((t, LANES), F32), jax.ShapeDtypeStruct((heads, 1, t), F32),
           jax.ShapeDtypeStruct((1, wm), F32)],
        grid=(heads,),
        in_specs=[col(0), col(0), rowb, rowb, col(2 * heads), col(3 * heads), slab(0), slab(heads),
                  pl.BlockSpec((2 * heads, 1, t), lambda h: (0, 0, 0)), vec, col(0), ANY_SPEC],
        out_specs=[col(0), col(0), col(0), col(0), pl.BlockSpec((t, LANES), lambda h: (0, 0)),
                   pl.BlockSpec((heads, 1, t), lambda h: (0, 0, 0)), vec],
        scratch_shapes=[pltpu.VMEM((nc, dh, dh), F32), pltpu.VMEM((nc, 8, dh), F32), pltpu.VMEM((nc, 8, LANES), F32),
                        pltpu.VMEM((dh, dh), F32), pltpu.VMEM((8, dh), F32)],
        compiler_params=_params("arbitrary"),
    )(qc, kc, qct, kct, proj_a, proj_a, gcols, gcols, gbt, g_mlstm, dhcat, dep)


def _pool_window(gi):
    return jnp.where(gi == 0, 2.0, jnp.where(gi == 1, 4.0, jnp.where(gi == 2, 8.0, 16.0))).astype(F32)


def _pool_select(gi, levels):
    return jnp.where(gi == 0, levels[0], jnp.where(gi == 1, levels[1], jnp.where(gi == 2, levels[2], levels[3])))


def _pooled(u, gi, rows):
    s, levels = u, []
    for lvl in range(POOL_GROUPS):
        s = s + _shift_down(s, 1 << lvl, rows)
        levels.append(s)
    cnt = jnp.minimum((rows + 1).astype(F32), _pool_window(gi))
    return _pool_select(gi, levels) / cnt - u, cnt


def _pool_fwd(proj_b, w_pool, scale, hcat, name):
    t, d_model = hcat.shape
    g, cg, _ = w_pool.shape
    off = (d_model - g * cg) // cg

    def body(u_ref, w_ref, s_ref, hcat_ref, out_ref):
        del hcat_ref
        gi = pl.program_id(0)
        u = u_ref[...]
        rows = lax.broadcasted_iota(jnp.int32, u.shape, 0)
        pooled, _ = _pooled(u, gi, rows)
        out_ref[...] = (_dot(pooled, w_ref[...], NN) * s_ref[...]).astype(out_ref.dtype)

    return pl.pallas_call(
        body, name=name,
        out_shape=jax.ShapeDtypeStruct((t, d_model), BF16),
        grid=(g,),
        in_specs=[pl.BlockSpec((t, cg), lambda gi: (0, gi)), pl.BlockSpec((None, cg, cg), lambda gi: (gi, 0, 0)),
                  pl.BlockSpec((1, cg), lambda gi: (0, gi)), HBM_SPEC],
        out_specs=pl.BlockSpec((t, cg), lambda gi: (0, off + gi)),
        input_output_aliases={3: 0},
        compiler_params=_params("arbitrary"),
    )(proj_b, w_pool, scale, hcat)


def _pool_bwd(proj_b, w_pool, scale, dhcat, name):
    t, d_model = dhcat.shape
    g, cg, _ = w_pool.shape
    off = (d_model - g * cg) // cg

    def body(u_ref, w_ref, s_ref, dy_ref, du_ref, dw_ref, ds_ref):
        gi = pl.program_id(0)
        u, w, dy = u_ref[...], w_ref[...], dy_ref[...]
        rows = lax.broadcasted_iota(jnp.int32, u.shape, 0)
        pooled, cnt = _pooled(u, gi, rows)
        mixed = _dot(pooled, w, NN)
        ds_ref[...] = jnp.sum(dy * mixed, axis=0, keepdims=True)
        dmixed = dy * s_ref[...]
        dw_ref[...] = _dot(pooled, dmixed, TN).astype(dw_ref.dtype)
        dpooled = _dot(dmixed, w, NT)
        s, levels = dpooled / cnt, []
        for lvl in range(POOL_GROUPS):
            s = s + _shift_up(s, 1 << lvl, rows)
            levels.append(s)
        du_ref[...] = (_pool_select(gi, levels) - dpooled).astype(du_ref.dtype)

    return pl.pallas_call(
        body, name=name,
        out_shape=[jax.ShapeDtypeStruct((t, g * cg), BF16), jax.ShapeDtypeStruct((g, cg, cg), BF16),
                   jax.ShapeDtypeStruct((1, g * cg), F32)],
        grid=(g,),
        in_specs=[pl.BlockSpec((t, cg), lambda gi: (0, gi)), pl.BlockSpec((None, cg, cg), lambda gi: (gi, 0, 0)),
                  pl.BlockSpec((1, cg), lambda gi: (0, gi)), pl.BlockSpec((t, cg), lambda gi: (0, off + gi))],
        out_specs=[pl.BlockSpec((t, cg), lambda gi: (0, gi)), pl.BlockSpec((None, cg, cg), lambda gi: (gi, 0, 0)),
                   pl.BlockSpec((1, cg), lambda gi: (0, gi))],
        compiler_params=_params("arbitrary"),
    )(proj_b, w_pool, scale, dhcat)


def _pad_cols(a, width):
    return jnp.pad(a, ((0, 0), (0, width - a.shape[1])))


def kernel(x, p, w_in, b_gates, w_qk_conv, g_mlstm, w_pool, pool_scale, w_out, g_mix_pre, g_mix_post, w_ff1, w_ff2, g_ff_pre, g_ff_post, w_ple_proj, w_ple_gate, g_ple_gate, g_ple_post, loss_target, m_w_in, m_b_gates, m_w_qk_conv, m_g_mlstm, m_w_pool, m_pool_scale, m_w_out, m_g_mix_pre, m_g_mix_post, m_w_ff1, m_w_ff2, m_g_ff_pre, m_g_ff_post, m_w_ple_proj, m_w_ple_gate, m_g_ple_gate, m_g_ple_post, v_w_in, v_b_gates, v_w_qk_conv, v_g_mlstm, v_w_pool, v_pool_scale, v_w_out, v_g_mix_pre, v_g_mix_post, v_w_ff1, v_w_ff2, v_g_ff_pre, v_g_ff_post, v_w_ple_proj, v_w_ple_gate, v_g_ple_gate, v_g_ple_post):
    weights = dict(w_in=w_in, b_gates=b_gates, w_qk_conv=w_qk_conv, g_mlstm=g_mlstm, w_pool=w_pool,
                   pool_scale=pool_scale, w_out=w_out, g_mix_pre=g_mix_pre, g_mix_post=g_mix_post, w_ff1=w_ff1,
                   w_ff2=w_ff2, g_ff_pre=g_ff_pre, g_ff_post=g_ff_post, w_ple_proj=w_ple_proj,
                   w_ple_gate=w_ple_gate, g_ple_gate=g_ple_gate, g_ple_post=g_ple_post)
    mom1 = dict(w_in=m_w_in, b_gates=m_b_gates, w_qk_conv=m_w_qk_conv, g_mlstm=m_g_mlstm, w_pool=m_w_pool,
                pool_scale=m_pool_scale, w_out=m_w_out, g_mix_pre=m_g_mix_pre, g_mix_post=m_g_mix_post,
                w_ff1=m_w_ff1, w_ff2=m_w_ff2, g_ff_pre=m_g_ff_pre, g_ff_post=m_g_ff_post,
                w_ple_proj=m_w_ple_proj, w_ple_gate=m_w_ple_gate, g_ple_gate=m_g_ple_gate, g_ple_post=m_g_ple_post)
    mom2 = dict(w_in=v_w_in, b_gates=v_b_gates, w_qk_conv=v_w_qk_conv, g_mlstm=v_g_mlstm, w_pool=v_w_pool,
                pool_scale=v_pool_scale, w_out=v_w_out, g_mix_pre=v_g_mix_pre, g_mix_post=v_g_mix_post,
                w_ff1=v_w_ff1, w_ff2=v_w_ff2, g_ff_pre=v_g_ff_pre, g_ff_post=v_g_ff_post,
                w_ple_proj=v_w_ple_proj, w_ple_gate=v_w_ple_gate, g_ple_gate=v_g_ple_gate, g_ple_post=v_g_ple_post)
    order = list(weights)

    t, d = x.shape[1], x.shape[2]
    heads = MLSTM_HEADS
    wm = g_mlstm.shape[1]
    pw = pool_scale.shape[1]
    cg = pw // POOL_GROUPS
    in_cols = 4 * wm + 2 * heads + pw
    in_shard = w_in.shape[2]
    xs, ps, tgt = x[0], p[0, 0], loss_target[0]

    mx, my, mc = lax.axis_index("x"), lax.axis_index("y"), lax.axis_index("c")
    core = jnp.reshape(mc, (1,)).astype(jnp.int32)
    chip = jnp.reshape(2 * mx + my, (1,)).astype(jnp.int32)

    first = [w_in[0].T.astype(BF16), w_qk_conv[0], w_pool[0].reshape(POOL_GROUPS * w_pool.shape[2], cg).astype(BF16)]
    started, token = _gather_start([first], "gather_start_in")
    anchor = token[0, 0]
    later = lambda w: (w[0] + anchor).astype(BF16)
    rest, token = _gather_start([[later(w_out)], [later(w_ff1)], [later(w_ff2)], [later(w_ple_gate), later(w_ple_proj)]],
                                "gather_start_rest")
    started = started + rest
    bias = _pad_cols(b_gates, LANES)
    small = ["b_gates", "g_mlstm", "pool_scale", "g_mix_pre", "g_mix_post", "g_ff_pre", "g_ff_post", "g_ple_gate",
             "g_ple_post"]
    pack = lambda src: _pad_rows16(jnp.concatenate([_pad_cols(src[k] + anchor, d) for k in small], axis=0))
    small_state = [pack(weights), pack(mom1), pack(mom2)]
    in_state = [a[0].T + anchor for a in (w_in, m_w_in, v_w_in)]

    xn = _norm_fwd(xs, g_mix_pre, "norm_mix_pre", deps=[token])
    g_in, g_conv, g_pool = _forward_pair(
        _gather_wait(started[0], [xn] + small_state + in_state, "gather_wait_in"), "gather_pair_in")
    in_t_a = g_in.reshape(in_cols, d)
    in_t_b = jnp.concatenate([in_t_a[4 * wm + 2 * heads:], in_t_a[4 * wm:4 * wm + 2 * heads],
                              jnp.zeros((LANES - 2 * heads, d), BF16)], axis=0)
    conv_w = g_conv.transpose(1, 0, 2).reshape(QK_CONV, 2 * wm)
    pool_w = g_pool.reshape(N_DEV, POOL_GROUPS, cg // N_DEV, cg).transpose(1, 0, 2, 3).reshape(POOL_GROUPS, cg, cg)
    proj_a = _mm_nt("proj_qkvo", xn, in_t_a, n_rows=4 * wm)
    proj_b = _mm_nt("proj_pool_gates", xn, in_t_b)
    qc = _conv_fwd(proj_a, conv_w[:, :wm], 0, "q_conv")
    kc = _conv_fwd(proj_a, conv_w[:, wm:], wm, "k_conv")
    gb, gbt = _gates_fwd(proj_b, bias, pw // LANES, heads, "gates")
    gcols = _gate_columns(gb, heads, "gate_columns")
    gbt = gbt[:2 * heads].reshape(2 * heads, 1, t)
    qct, kct = qc.T, kc.T
    (g_out,) = _forward_pair(_gather_wait(started[1], gb, "gather_wait_out"), "gather_pair_out")
    hcat = _mlstm_fwd(qc, kc, kct, proj_a, gcols, gbt, g_mlstm, heads, d, "mlstm")
    hcat = _pool_fwd(proj_b, pool_w, pool_scale, hcat, "pool")
    out_w = g_out.reshape(d, d)
    mix = _mm_nn("mix_out", hcat, out_w)[0]
    h1, hn = _block_fwd(xs, mix, g_mix_post, g_ff_pre, "residual_mix")
    (g_ff1,) = _forward_pair(_gather_wait(started[2], hn, "gather_wait_ff1"), "gather_pair_ff1")
    relu_a, act = _mm_nn_pieces("ff1", hn, g_ff1, out_dtype=[BF16, BF16],
                                epilogue=lambda acc: (jnp.maximum(acc, 0.0), jnp.square(jnp.maximum(acc, 0.0))))
    (g_ff2,) = _forward_pair(_gather_wait(started[3], act, "gather_wait_ff2"), "gather_pair_ff2")
    ff2_w = g_ff2.reshape(-1, d)
    ff = _mm_nn("ff2", act, ff2_w, tn=256, tk=4 * d)[0]
    h2, hg = _block_fwd(h1, ff, g_ff_post, g_ple_gate, "residual_ff")
    g_pgate, g_pproj = _forward_pair(_gather_wait(started[4], hg, "gather_wait_ple"), "gather_pair_ple")
    pgate_w = g_pgate.reshape(d, d)
    z = _mm_nn("ple_gate", hg, pgate_w)[0]
    e = _mm_nn_pieces("ple_proj", ps, g_pproj)[0]
    dh3, dz, de, dg_ple_post, loss_part = _loss_head(h2, z, e, tgt, g_ple_post, "loss_head")
    loss = lax.psum(loss_part[0, 0], MESH_AXES)

    out = {}

    def state(k):
        if k == "w_in":
            return in_state
        return [a.reshape(-1, a.shape[-1]) for a in (weights[k], mom1[k], mom2[k])]

    def pair_begin(pieces, tag):
        keys = list(pieces)
        return keys, _pair_start([pieces[k] for k in keys], "reduce_pair_start_" + tag)

    def chip_begin(group, after, tag):
        keys, begun = group
        mine, got = _pair_wait(begun, after, "reduce_pair_wait_" + tag)
        sums = [_pair_sum(p_, g, core, "pair_sum_" + k) for k, p_, g in zip(keys, mine, got)]
        return keys, _chip_start(sums, "reduce_start_" + tag)

    def reduce_start(pieces, tag):
        keys = list(pieces)
        got = _pair_exchange([pieces[k] for k in keys], "reduce_pair_" + tag)
        sums = [_pair_sum(pieces[k], g, core, "pair_sum_" + k) for k, g in zip(keys, got)]
        return keys, _chip_start(sums, "reduce_start_" + tag)

    def reduce_finish(group, after, tag):
        keys, begun = group
        sums, got = _chip_wait(begun, after, "reduce_wait_" + tag)
        for k, s, g in zip(keys, sums, got):
            res = _adamw_shard(*state(k), s, g, chip, "adamw_" + k)
            out[k] = [r.T[None] if k == "w_in" else r.reshape(weights[k].shape) for r in res]

    token_of = lambda group: group[1]["token"]
    dhg = _mm_nt("d_ple_gate_in", dz, pgate_w)
    gw_pgate = _mm_tn("gw_ple_gate", hg, dz)
    gw_pproj = _mm_tn("gw_ple_proj", ps, de, pieces=N_DEV)
    pair_ple = pair_begin({"w_ple_gate": gw_pgate.reshape(N_DEV, d // N_DEV, d), "w_ple_proj": gw_pproj}, "ple")
    dh2, dff, dg_ff_post, dg_ple_gate = _block_bwd(h2, ff, dh3, dhg, g_ff_post, g_ple_gate, "d_residual_ff",
                                                   deps=[token_of(pair_ple)])
    red_ple = chip_begin(pair_ple, dff, "ple")
    da = _mm_nt("d_ff2_in", dff, ff2_w, out_dtype=BF16, extra=[relu_a],
                epilogue=lambda acc, r: (acc * (2.0 * r.astype(F32)),), deps=[token_of(red_ple)])
    gw_ff2 = _mm_tn("gw_ff2", act, dff)
    pair_ff2 = pair_begin({"w_ff2": gw_ff2.reshape(N_DEV, -1, d)}, "ff2")
    dhn = _mm_nt_pieces("d_ff1_in", da, g_ff1, deps=[token_of(pair_ff2)])
    place = jnp.concatenate([core, chip])
    mine_ff2, got_ff2 = _pair_wait(pair_ff2[1], dhn, "reduce_pair_wait_ff2")
    gw_ff1, sums_ff2 = _mm_tn("gw_ff1", hn, da, pieces=N_DEV, side=_side_pair_sum(mine_ff2[0], got_ff2[0], place))
    red_ff2 = (["w_ff2"], _chip_start([sums_ff2], "reduce_start_ff2"))
    reduce_finish(red_ple, token_of(red_ff2), "ple")
    pair_ff1 = pair_begin({"w_ff1": gw_ff1}, "ff1")
    dh1, dmix, dg_mix_post, dg_ff_pre = _block_bwd(h1, mix, dh2, dhn, g_mix_post, g_ff_pre, "d_residual_mix",
                                                   deps=[token_of(pair_ff1), token_of(red_ff2)])
    dhcat = _mm_nt("d_mix_in", dmix, out_w)
    red_ff1 = chip_begin(pair_ff1, dhcat, "ff1")
    gw_out = _mm_tn("gw_out", hcat, dmix, deps=[token_of(red_ff1)])
    du, gw_pool, dg_pool_scale = _pool_bwd(proj_b, pool_w, pool_scale, dhcat, "d_pool")
    pair_out = pair_begin({
        "w_out": gw_out.reshape(N_DEV, d // N_DEV, d),
        "w_pool": gw_pool.reshape(POOL_GROUPS, N_DEV, cg // N_DEV, cg).transpose(1, 0, 2, 3).reshape(N_DEV, -1, cg)},
        "out")
    dq, dk, dv, do, dcol, drow, dg_mlstm = _mlstm_bwd(qc, kc, qct, kct, proj_a, gcols, gbt, g_mlstm, dhcat, heads,
                                                      "d_mlstm", token_of(pair_out))
    red_out = chip_begin(pair_out, dq, "out")
    drow = jnp.pad(drow.reshape(heads, t), ((0, LANES - heads), (0, 0)))
    dgates, dbias = _gates_bwd(proj_b, bias, dcol, drow, pw // LANES, heads, "d_gates")
    dq_pre, gw_conv_q = _conv_bwd(proj_a, conv_w[:, :wm], dq, 0, "d_q_conv")
    dk_pre, gw_conv_k = _conv_bwd(proj_a, conv_w[:, wm:], dk, wm, "d_k_conv")
    gw_conv = jnp.concatenate([gw_conv_q, gw_conv_k], axis=1)
    dproj_a = jnp.concatenate([dq_pre, dk_pre, dv, do], axis=1)
    dproj_b = jnp.concatenate([du, dgates], axis=1)
    sums_ff2, got_ff2 = _chip_wait(red_ff2[1], token_of(red_out), "reduce_wait_ff2")
    gw_a, *update = _mm_tn("gw_in_qkvo", dproj_a, xn,
                           side=_side_adamw(*state("w_ff2"), sums_ff2[0], got_ff2[0], place))
    out["w_ff2"] = [r.reshape(weights["w_ff2"].shape) for r in update]
    gw_b = _mm_tn("gw_in_pool_gates", dproj_b, xn)
    gw_in_t = jnp.concatenate([gw_a, gw_b[pw:pw + 2 * heads], gw_b[:pw]], axis=0)
    red_in = reduce_start({
        "w_in": gw_in_t.reshape(N_DEV, in_shard, d),
        "w_qk_conv": gw_conv.reshape(QK_CONV, N_DEV, -1).transpose(1, 0, 2).astype(BF16)}, "in")
    reduce_finish(red_out, red_in[1]["token"], "out")
    sums_ff1, got_ff1 = _chip_wait(red_ff1[1], token_of(red_in), "reduce_wait_ff1")
    dxn_a, *update = _mm_nn("d_proj_qkvo_in", dproj_a, in_t_a, tn=512, tk=4 * wm, deps=[token_of(red_in)],
                            side=_side_adamw(*state("w_ff1"), sums_ff1[0], got_ff1[0], place))
    out["w_ff1"] = [r.reshape(weights["w_ff1"].shape) for r in update]
    dxn = _mm_nn("d_proj_in", dproj_b, in_t_b, extra=[dxn_a], epilogue=lambda acc, prev: (acc + prev,))[0]
    grad_x, dg_mix_pre = _norm_bwd(xs, dxn, dh1, g_mix_pre, "d_norm_mix_pre")

    small_grads = dict(b_gates=dbias[:, :2 * heads], g_mlstm=dg_mlstm, pool_scale=dg_pool_scale, g_mix_pre=dg_mix_pre,
                       g_mix_post=dg_mix_post, g_ff_pre=dg_ff_pre, g_ff_post=dg_ff_post, g_ple_gate=dg_ple_gate,
                       g_ple_post=dg_ple_post)
    small_parts = _all_gather([pack(small_grads)], "gather_small_grads")[0]
    res = _adamw_replicated(*small_state, small_parts, "adamw_replicated")
    for i, k in enumerate(small):
        width = weights[k].shape[1]
        out[k] = [r[i:i + 1, :width] for r in res]
    reduce_finish(red_in, res[0], "in")

    return (loss, grad_x[None], *[out[k][0] for k in order], *[out[k][1] for k in order],
            *[out[k][2] for k in order], *[out[k][3] for k in order])


def _pad_rows16(a):
    return jnp.pad(a, ((0, 16 - a.shape[0]), (0, 0)))
```

```python
import functools

import jax
import jax.numpy as jnp
from jax import lax
from jax.experimental import pallas as pl
from jax.experimental.pallas import tpu as pltpu

F32 = jnp.float32
BF16 = jnp.bfloat16
EPS = 1e-6
N_DEV = 8
N_CHIP = 4
LANES = 128
VMEM_LIMIT = 56 * 1024 * 1024
MLSTM_HEADS = 8
POOL_GROUPS = 4
QK_CONV = 4
CHUNK = 128
ROW_TILE = 256
ADAM_LR, ADAM_B1, ADAM_B2, ADAM_EPS, ADAM_WD, ADAM_STEP = 0.001, 0.9, 0.999, 1e-08, 0.01, 10
MESH_AXES = ("x", "y", "c")
MESH = pl.DeviceIdType.MESH
HBM_SPEC = pl.BlockSpec(memory_space=pltpu.HBM)
SEM_SPEC = pl.BlockSpec(memory_space=pltpu.SEMAPHORE)
DATAFLOW = pltpu.SideEffectType.DATAFLOW_SIDE_EFFECTING


def _params(*sem):
    if sem:
        return pltpu.CompilerParams(dimension_semantics=sem, vmem_limit_bytes=VMEM_LIMIT)
    return pltpu.CompilerParams(vmem_limit_bytes=VMEM_LIMIT)


def _tile(n, pref):
    if n <= pref:
        return n
    t = (pref // LANES) * LANES
    while t >= LANES:
        if n % t == 0:
            return t
        t -= LANES
    return n


def _all_gather(arrs, name):
    n = len(arrs)

    def body(*refs):
        ins, outs = refs[:n], refs[n:2 * n]
        send, recv, loc = refs[2 * n:]
        x, y, c = lax.axis_index("x"), lax.axis_index("y"), lax.axis_index("c")
        me = (x, y, c)
        sibling = (x, y, 1 - c)
        chips = [(1 - x, y), (x, 1 - y), (1 - x, 1 - y)]

        def row(dev):
            return 4 * dev[0] + 2 * dev[1] + dev[2]

        def copy(w, k, block_of, to, src=None):
            dst = outs[w].at[row(block_of)]
            return pltpu.make_async_remote_copy(
                src_ref=dst if src is None else src, dst_ref=dst,
                send_sem=send.at[7 * w + k], recv_sem=recv.at[7 * w + k],
                device_id=to, device_id_type=MESH)

        started = []
        local = []
        for w in range(n):
            cp = pltpu.make_async_copy(ins[w], outs[w].at[row(me)], loc.at[w])
            cp.start()
            local.append(cp)
            first = [copy(w, 0, me, sibling, src=ins[w])]
            first += [copy(w, 1 + j, me, (*chip, c), src=ins[w]) for j, chip in enumerate(chips)]
            for cp in first:
                cp.start()
            started += first
        for j, chip in enumerate(chips):
            for w in range(n):
                copy(w, 1 + j, (*chip, c), me).wait_recv()
                fwd = copy(w, 4 + j, (*chip, c), sibling)
                fwd.start()
                started.append(fwd)
        for w in range(n):
            copy(w, 0, sibling, me).wait_recv()
            for j, chip in enumerate(chips):
                copy(w, 4 + j, (*chip, 1 - c), me).wait_recv()
        for cp in started:
            cp.wait_send()
        for cp in local:
            cp.wait()

    outs = pl.pallas_call(
        body, name=name,
        out_shape=[jax.ShapeDtypeStruct((N_DEV,) + a.shape, a.dtype) for a in arrs],
        in_specs=[HBM_SPEC] * n, out_specs=[HBM_SPEC] * n,
        scratch_shapes=[pltpu.SemaphoreType.DMA((7 * n,)), pltpu.SemaphoreType.DMA((7 * n,)),
                        pltpu.SemaphoreType.DMA((n,))],
    )(*arrs)
    return list(outs)


def _pair_exchange(pieces, name):
    n = len(pieces)

    def body(*refs):
        ins, outs = refs[:n], refs[n:2 * n]
        send, recv = refs[2 * n:]
        x, y, c = lax.axis_index("x"), lax.axis_index("y"), lax.axis_index("c")
        sibling = (x, y, 1 - c)

        def copy(w, chip, src_core):
            return pltpu.make_async_remote_copy(
                src_ref=ins[w].at[2 * chip + src_core], dst_ref=outs[w].at[chip],
                send_sem=send.at[N_CHIP * w + chip], recv_sem=recv.at[N_CHIP * w + chip],
                device_id=sibling, device_id_type=MESH)

        cps = [copy(w, chip, 1 - c) for w in range(n) for chip in range(N_CHIP)]
        for cp in cps:
            cp.start()
        for cp in cps:
            cp.wait_recv()
        for cp in cps:
            cp.wait_send()

    outs = pl.pallas_call(
        body, name=name,
        out_shape=[jax.ShapeDtypeStruct((N_CHIP,) + a.shape[1:], a.dtype) for a in pieces],
        in_specs=[HBM_SPEC] * n, out_specs=[HBM_SPEC] * n,
        scratch_shapes=[pltpu.SemaphoreType.DMA((N_CHIP * n,)), pltpu.SemaphoreType.DMA((N_CHIP * n,))],
    )(*pieces)
    return list(outs)


def _hbm(a):
    return pltpu.with_memory_space_constraint(a, pltpu.HBM)


def _mesh_place():
    x, y, c = lax.axis_index("x"), lax.axis_index("y"), lax.axis_index("c")
    return x, y, c, [(1 - x, y), (x, 1 - y), (1 - x, 1 - y)]


def _chip_copies(srcs, lands, send, recv):
    x, y, c, chips = _mesh_place()
    return [pltpu.make_async_remote_copy(
        src_ref=srcs[w].at[2 * chip[0] + chip[1]], dst_ref=lands[w].at[k],
        send_sem=send.at[3 * w + k], recv_sem=recv.at[3 * w + k],
        device_id=(*chip, c), device_id_type=MESH)
        for k, chip in enumerate(chips) for w in range(len(srcs))]


def _chip_start(sums, name):
    return _split_start(_chip_copies, sums, [lax.empty((3,) + a.shape[1:], a.dtype) for a in sums], 3, name)


def _chip_wait(started, after, name):
    return _split_wait(_chip_copies, started, after, name)


def _pair_start(pieces, name):
    return _split_start(_pair_copies, pieces, [lax.empty((N_CHIP,) + a.shape[1:], a.dtype) for a in pieces], N_CHIP, name)


def _pair_wait(started, after, name):
    return _split_wait(_pair_copies, started, after, name)


def _pair_copies(srcs, lands, send, recv):
    x, y, c, _ = _mesh_place()
    return [pltpu.make_async_remote_copy(
        src_ref=srcs[w].at[2 * ch + 1 - c], dst_ref=lands[w].at[ch],
        send_sem=send.at[N_CHIP * w + ch], recv_sem=recv.at[N_CHIP * w + ch],
        device_id=(x, y, 1 - c), device_id_type=MESH)
        for w in range(len(srcs)) for ch in range(N_CHIP)]


def _split_start(copies_of, srcs, lands, per_array, name):
    ns, nb = len(srcs), len(srcs) + len(lands)
    n_copies = per_array * len(lands)

    def body(*refs):
        for cp in copies_of(refs[:ns], refs[ns:nb], refs[nb], refs[nb + 1]):
            cp.start()
        refs[-1][...] = jnp.zeros_like(refs[-1])

    res = pl.pallas_call(
        body, name=name,
        out_shape=[pltpu.SemaphoreType.DMA((n_copies,)), pltpu.SemaphoreType.DMA((n_copies,))]
        + [pltpu.HBM(a.shape, a.dtype) for a in list(srcs) + list(lands)] + [jax.ShapeDtypeStruct((8, LANES), F32)],
        in_specs=[HBM_SPEC] * nb,
        out_specs=[SEM_SPEC, SEM_SPEC] + [HBM_SPEC] * nb + [pl.BlockSpec(memory_space=pltpu.VMEM)],
        input_output_aliases={i: 2 + i for i in range(nb)},
        compiler_params=pltpu.CompilerParams(has_side_effects=DATAFLOW),
    )(*[_hbm(a) for a in srcs], *[_hbm(a) for a in lands])
    return dict(send=res[0], recv=res[1], srcs=list(res[2:2 + ns]), lands=list(res[2 + ns:2 + nb]), token=res[-1])


def _split_wait(copies_of, started, after, name):
    ns = len(started["srcs"])
    nb = ns + len(started["lands"])

    def body(*refs):
        for cp in copies_of(refs[:ns], refs[ns:nb], refs[nb], refs[nb + 1]):
            cp.wait_send()
            cp.wait_recv()

    res = pl.pallas_call(
        body, name=name,
        out_shape=[pltpu.HBM(a.shape, a.dtype) for a in started["srcs"] + started["lands"]],
        in_specs=[HBM_SPEC] * nb + [SEM_SPEC, SEM_SPEC, ANY_SPEC],
        out_specs=[HBM_SPEC] * nb,
        input_output_aliases={i: i for i in range(nb)},
        compiler_params=pltpu.CompilerParams(has_side_effects=DATAFLOW),
    )(*started["srcs"], *started["lands"], started["send"], started["recv"], after)
    return list(res[:ns]), list(res[ns:])


def _forward_copies(srcs, lands, send, recv):
    del srcs
    x, y, c, chips = _mesh_place()
    copies = []
    for w in range(len(lands)):
        for j, chip in enumerate(chips):
            rows = lands[w].at[4 * chip[0] + 2 * chip[1] + c]
            copies.append(pltpu.make_async_remote_copy(
                src_ref=rows, dst_ref=rows, send_sem=send.at[3 * w + j], recv_sem=recv.at[3 * w + j],
                device_id=(x, y, 1 - c), device_id_type=MESH))
    return copies


def _forward_start(lands, name):
    return _split_start(_forward_copies, [], lands, 3, name)


def _forward_wait(started, after, name):
    return _split_wait(_forward_copies, started, after, name)[1]


def _everyone_copies(srcs, lands, send, recv):
    x, y, c, _ = _mesh_place()
    me = 4 * x + 2 * y + c
    copies = []
    for w in range(len(srcs)):
        for k in range(N_DEV - 1):
            flip = k + 1
            peer = (1 - x if flip & 4 else x, 1 - y if flip & 2 else y, 1 - c if flip & 1 else c)
            copies.append(pltpu.make_async_remote_copy(
                src_ref=srcs[w], dst_ref=lands[w].at[me], send_sem=send.at[7 * w + k], recv_sem=recv.at[7 * w + k],
                device_id=peer, device_id_type=MESH))
    return copies


def _everyone_start(blocks, name):
    me = 4 * lax.axis_index("x") + 2 * lax.axis_index("y") + lax.axis_index("c")
    lands = [lax.dynamic_update_slice(lax.empty((N_DEV,) + a.shape, a.dtype), a[None], (me,) + (0,) * a.ndim)
             for a in blocks]
    return _split_start(_everyone_copies, blocks, lands, N_DEV - 1, name)


def _everyone_wait(started, after, name):
    return _split_wait(_everyone_copies, started, after, name)[1]


def _gather_copies(srcs, lands, send, recv):
    x, y, c, chips = _mesh_place()
    me = 4 * x + 2 * y + c
    targets = [(x, y, 1 - c)] + [(*chip, c) for chip in chips]
    return [pltpu.make_async_remote_copy(
        src_ref=srcs[w], dst_ref=lands[w].at[me], send_sem=send.at[4 * w + k], recv_sem=recv.at[4 * w + k],
        device_id=to, device_id_type=MESH)
        for w in range(len(srcs)) for k, to in enumerate(targets)]


def _gather_start(groups, name):
    sizes = [len(g) for g in groups]
    flat = [a for g in groups for a in g]
    n, ng = len(flat), len(groups)
    me = 4 * lax.axis_index("x") + 2 * lax.axis_index("y") + lax.axis_index("c")
    lands = [lax.dynamic_update_slice(lax.empty((N_DEV,) + a.shape, a.dtype), a[None], (me,) + (0,) * a.ndim)
             for a in flat]

    def body(*refs):
        srcs, zones = refs[:n], refs[n:2 * n]
        sems = refs[2 * n:2 * n + 2 * ng]
        token = refs[-1]
        lo = 0
        for gi, size in enumerate(sizes):
            for cp in _gather_copies(srcs[lo:lo + size], zones[lo:lo + size], sems[2 * gi], sems[2 * gi + 1]):
                cp.start()
            lo += size
        token[...] = jnp.zeros_like(token)

    sem_shapes = []
    for size in sizes:
        sem_shapes += [pltpu.SemaphoreType.DMA((4 * size,))] * 2
    res = pl.pallas_call(
        body, name=name,
        out_shape=sem_shapes + [pltpu.HBM(a.shape, a.dtype) for a in flat + lands] + [jax.ShapeDtypeStruct((8, LANES), F32)],
        in_specs=[HBM_SPEC] * (2 * n),
        out_specs=[SEM_SPEC] * (2 * ng) + [HBM_SPEC] * (2 * n) + [pl.BlockSpec(memory_space=pltpu.VMEM)],
        input_output_aliases={i: 2 * ng + i for i in range(2 * n)},
        compiler_params=pltpu.CompilerParams(has_side_effects=DATAFLOW),
    )(*[_hbm(a) for a in flat], *[_hbm(a) for a in lands])
    out, lo = [], 0
    for gi, size in enumerate(sizes):
        out.append(dict(send=res[2 * gi], recv=res[2 * gi + 1],
                        srcs=list(res[2 * ng + lo:2 * ng + lo + size]),
                        lands=list(res[2 * ng + n + lo:2 * ng + n + lo + size])))
        lo += size
    return out, res[-1]


def _gather_wait(started, after, name):
    n = len(started["srcs"])
    after = list(after) if isinstance(after, (list, tuple)) else [after]

    def body(*refs):
        srcs, zones = refs[:n], refs[n:2 * n]
        send, recv = refs[2 * n], refs[2 * n + 1]
        for cp in _gather_copies(srcs, zones, send, recv):
            cp.wait_send()
            cp.wait_recv()

    res = pl.pallas_call(
        body, name=name,
        out_shape=[pltpu.HBM(a.shape, a.dtype) for a in started["srcs"] + started["lands"]],
        in_specs=[HBM_SPEC] * (2 * n) + [SEM_SPEC, SEM_SPEC] + [ANY_SPEC] * len(after),
        out_specs=[HBM_SPEC] * (2 * n),
        input_output_aliases={i: i for i in range(2 * n)},
        compiler_params=pltpu.CompilerParams(has_side_effects=DATAFLOW),
    )(*started["srcs"], *started["lands"], started["send"], started["recv"], *after)
    return list(res[n:])


def _forward_pair(lands, name):
    n = len(lands)

    def body(*refs):
        zones = refs[:n]
        send, recv = refs[2 * n:]
        x, y, c, chips = _mesh_place()

        def copy(w, j, core):
            rows = zones[w].at[4 * chips[j][0] + 2 * chips[j][1] + core]
            return pltpu.make_async_remote_copy(
                src_ref=rows, dst_ref=rows, send_sem=send.at[3 * w + j], recv_sem=recv.at[3 * w + j],
                device_id=(x, y, 1 - c), device_id_type=MESH)

        sends = [copy(w, j, c) for w in range(n) for j in range(3)]
        for cp in sends:
            cp.start()
        for w in range(n):
            for j in range(3):
                copy(w, j, 1 - c).wait_recv()
        for cp in sends:
            cp.wait_send()

    return list(pl.pallas_call(
        body, name=name,
        out_shape=[jax.ShapeDtypeStruct(a.shape, a.dtype) for a in lands],
        in_specs=[HBM_SPEC] * n, out_specs=[HBM_SPEC] * n,
        input_output_aliases={i: i for i in range(n)},
        scratch_shapes=[pltpu.SemaphoreType.DMA((3 * n,)), pltpu.SemaphoreType.DMA((3 * n,))],
    )(*lands))


def _pair_sum(pieces, got, core, name):
    _, rows, cols = pieces.shape
    tr, tc = _tile_2d(rows, cols)

    def body(core_ref, mine_ref, got_ref, out_ref):
        del core_ref
        out_ref[...] = (mine_ref[...].astype(F32) + got_ref[...].astype(F32)).astype(out_ref.dtype)

    return pl.pallas_call(
        body, name=name,
        out_shape=jax.ShapeDtypeStruct((N_CHIP, rows, cols), pieces.dtype),
        grid_spec=pltpu.PrefetchScalarGridSpec(
            num_scalar_prefetch=1, grid=(N_CHIP, rows // tr, cols // tc),
            in_specs=[pl.BlockSpec((None, tr, tc), lambda ch, i, j, core: (2 * ch + core[0], i, j)),
                      pl.BlockSpec((None, tr, tc), lambda ch, i, j, core: (ch, i, j))],
            out_specs=pl.BlockSpec((None, tr, tc), lambda ch, i, j, core: (ch, i, j))),
        compiler_params=_params("parallel", "parallel", "parallel"),
    )(core, pieces, got)


def _tile_2d(rows, cols, budget=2 * 1024 * 1024):
    want = max(16, budget // (4 * cols))
    if rows <= want:
        return rows, cols
    t = (want // 16) * 16
    while t >= 16:
        if rows % t == 0:
            return t, cols
        t -= 16
    return rows, _tile(cols, max(LANES, budget // (4 * rows)))


def _adamw_math(w, g, m, v):
    m = ADAM_B1 * m + (1.0 - ADAM_B1) * g
    v = ADAM_B2 * v + (1.0 - ADAM_B2) * (g * g)
    m_hat = m / (1.0 - ADAM_B1 ** ADAM_STEP)
    v_hat = v / (1.0 - ADAM_B2 ** ADAM_STEP)
    delta = -ADAM_LR * (m_hat / (jnp.sqrt(v_hat) + ADAM_EPS) + ADAM_WD * w)
    return delta, m, v


def _adamw_shard(w, m, v, sums, got, chip, name):
    rows, cols = w.shape
    tr, tc = _tile_2d(rows, cols, budget=1024 * 1024)

    def body(chip_ref, w_ref, m_ref, v_ref, own_ref, got_ref, g_out, d_out, m_out, v_out):
        del chip_ref
        g = own_ref[...].astype(F32)
        for k in range(3):
            g = g + got_ref[k].astype(F32)
        delta, m_new, v_new = _adamw_math(w_ref[...], g, m_ref[...], v_ref[...])
        g_out[...] = g
        d_out[...] = delta
        m_out[...] = m_new
        v_out[...] = v_new

    blk = pl.BlockSpec((tr, tc), lambda i, j, chip: (i, j))
    return pl.pallas_call(
        body, name=name,
        out_shape=[jax.ShapeDtypeStruct((rows, cols), F32)] * 4,
        grid_spec=pltpu.PrefetchScalarGridSpec(
            num_scalar_prefetch=1, grid=(rows // tr, cols // tc),
            in_specs=[blk, blk, blk,
                      pl.BlockSpec((None, tr, tc), lambda i, j, chip: (chip[0], i, j)),
                      pl.BlockSpec((3, tr, tc), lambda i, j, chip: (0, i, j))],
            out_specs=[blk] * 4),
        compiler_params=_params("parallel", "parallel"),
    )(chip, w, m, v, sums, got)


def _adamw_replicated(w, m, v, parts, name):
    rows, cols = w.shape

    def body(w_ref, m_ref, v_ref, parts_ref, g_out, d_out, m_out, v_out):
        g = parts_ref[0]
        for d in range(1, N_DEV):
            g = g + parts_ref[d]
        delta, m_new, v_new = _adamw_math(w_ref[...], g, m_ref[...], v_ref[...])
        g_out[...] = g
        d_out[...] = delta
        m_out[...] = m_new
        v_out[...] = v_new

    return pl.pallas_call(
        body, name=name,
        out_shape=[jax.ShapeDtypeStruct((rows, cols), F32)] * 4,
        compiler_params=_params(),
    )(w, m, v, parts)


NN = ((1,), (0,))
NT = ((1,), (1,))
TN = ((0,), (0,))


ANY_SPEC = pl.BlockSpec(memory_space=pl.ANY)


def _mm(name, a, b, *, dims, grid, a_spec, b_spec, outs, extra=(), extra_specs=(), epilogue=None, acc_shape=None,
        deps=(), b_pieces=0, side=None):
    nk = grid[2]
    n_extra, n_out = len(extra), len(outs)
    side_ins = side["ins"] if side else []
    side_outs = side["outs"] if side else []
    first_side = 2 + n_extra + len(deps)
    first_out = first_side + len(side_ins)
    assert not side or nk == 1

    assert not side or (epilogue is None and n_out == 1 and not b_pieces)
    SIDE_SLICES = 4

    def body(*refs):
        dot = lambda lhs, rhs: lax.dot_general(lhs.astype(BF16), rhs.astype(BF16), (dims, ((), ())),
                                               preferred_element_type=F32)
        if side:
            refs = refs[1:]
            a_ref, b_ref, out_ref = refs[0], refs[1], refs[first_out]
            ins, sides = refs[first_side:first_out], refs[first_out + 1:first_out + 1 + len(side_outs)]
            rows = out_ref.shape[0] // SIDE_SLICES
            srows = sides[0].shape[-2] // SIDE_SLICES
            for part in range(SIDE_SLICES):
                rs = slice(part * srows, (part + 1) * srows)
                for val, o in zip(side["fn"](*[r[..., rs, :] for r in ins]), sides):
                    o[rs, :] = val.astype(o.dtype)
                ms = slice(part * rows, (part + 1) * rows)
                lhs = a_ref[:, ms] if dims == TN else a_ref[ms, :]
                out_ref[ms, :] = dot(lhs, b_ref[...]).astype(out_ref.dtype)
            return
        a_ref, b_ref = refs[0], refs[1]
        ex = refs[2:2 + n_extra]
        out_refs = refs[first_out:first_out + n_out]
        if b_pieces:
            ks = b_ref.shape[-1]
            part = dot(a_ref[:, 0:ks], b_ref[0])
            for piece in range(1, b_pieces):
                part = part + dot(a_ref[:, piece * ks:(piece + 1) * ks], b_ref[piece])
        else:
            part = dot(a_ref[...], b_ref[...])

        def finish(acc):
            res = epilogue(acc, *[e[...] for e in ex]) if epilogue else (acc,)
            for val, o in zip(res, out_refs):
                o[...] = val.astype(o.dtype)

        if nk == 1:
            finish(part)
        else:
            acc_ref = refs[-1]
            k = pl.program_id(2)

            @pl.when(k == 0)
            def _():
                acc_ref[...] = part

            @pl.when(k > 0)
            def _():
                acc_ref[...] += part

            @pl.when(k == nk - 1)
            def _():
                finish(acc_ref[...])

    scratch = [pltpu.VMEM(acc_shape, F32)] if nk > 1 else []
    out_shape = [jax.ShapeDtypeStruct(s, d) for s, d, _ in outs]
    in_specs = [a_spec, b_spec, *extra_specs] + [ANY_SPEC] * len(deps)
    out_specs = [sp for _, _, sp in outs]
    if not side:
        return list(pl.pallas_call(
            body, name=name, out_shape=out_shape, grid=grid, in_specs=in_specs, out_specs=out_specs,
            scratch_shapes=scratch, compiler_params=_params("parallel", "parallel", "arbitrary"),
        )(a, b, *extra, *deps))

    def with_scalars(spec):
        if spec.index_map is None:
            return spec
        return pl.BlockSpec(spec.block_shape, lambda i, j, k, s, f=spec.index_map: f(i, j, k))

    def by_step(block, index_map):
        return pl.BlockSpec(block, lambda i, j, k, s: index_map((i * grid[1] + j) * grid[2] + k, s))

    return list(pl.pallas_call(
        body, name=name,
        out_shape=out_shape + [jax.ShapeDtypeStruct(s, d) for s, d, _, _ in side_outs],
        grid_spec=pltpu.PrefetchScalarGridSpec(
            num_scalar_prefetch=1, grid=grid,
            in_specs=[with_scalars(sp) for sp in in_specs] + [by_step(blk, im) for _, blk, im in side_ins],
            out_specs=[with_scalars(sp) for sp in out_specs] + [by_step(blk, im) for _, _, blk, im in side_outs],
            scratch_shapes=scratch),
        compiler_params=_params("arbitrary", "arbitrary", "arbitrary"),
    )(side["scalars"], a, b, *extra, *deps, *[arr for arr, _, _ in side_ins]))


def _side_pair_sum(pieces, got, place):
    _, rows, cols = pieces.shape

    def build(steps):
        per_chip = steps // N_CHIP
        tr = rows // per_chip
        assert per_chip * N_CHIP == steps and tr * per_chip == rows and tr % 16 == 0
        blk = (None, tr, cols)
        here = lambda st, s: (st // per_chip, st % per_chip, 0)
        return dict(scalars=place,
                    ins=[(pieces, blk, lambda st, s: (2 * (st // per_chip) + s[0], st % per_chip, 0)), (got, blk, here)],
                    outs=[((N_CHIP, rows, cols), pieces.dtype, blk, here)],
                    fn=lambda mine, other: (mine.astype(F32) + other.astype(F32),))
    return build


def _side_adamw(w, m, v, sums, got, place):
    rows, cols = w.shape

    def build(steps):
        tr = rows // steps
        assert tr * steps == rows and tr % 16 == 0
        blk, here = (tr, cols), (lambda st, s: (st, 0))

        def fn(w, m, v, own, got):
            g = own.astype(F32)
            for k in range(3):
                g = g + got[k].astype(F32)
            delta, m_new, v_new = _adamw_math(w, g, m, v)
            return g, delta, m_new, v_new

        return dict(scalars=place,
                    ins=[(w, blk, here), (m, blk, here), (v, blk, here),
                         (sums, (None, tr, cols), lambda st, s: (s[1], st, 0)),
                         (got, (3, tr, cols), lambda st, s: (0, st, 0))],
                    outs=[((rows, cols), F32, blk, here)] * 4, fn=fn)
    return build


def _mm_nn(name, a, b, out_dtype=F32, tm=1024, tn=1024, tk=2048, epilogue=None, n_out=1, extra=(), deps=(), side=None):
    m, kd = a.shape
    n = b.shape[1]
    tm, tn, tk = _tile(m, tm), _tile(n, tn), _tile(kd, tk)
    dts = out_dtype if isinstance(out_dtype, (list, tuple)) else [out_dtype] * n_out
    o_spec = pl.BlockSpec((tm, tn), lambda i, j, k: (i, j))
    grid = (m // tm, n // tn, kd // tk)
    return _mm(name, a, b, dims=NN, grid=grid,
               a_spec=pl.BlockSpec((tm, tk), lambda i, j, k: (i, k)),
               b_spec=pl.BlockSpec((tk, tn), lambda i, j, k: (k, j)),
               outs=[((m, n), dt, o_spec) for dt in dts], extra=extra, extra_specs=[o_spec] * len(extra),
               epilogue=epilogue, acc_shape=(tm, tn), deps=deps, side=side and side(grid[0] * grid[1] * grid[2]))


def _mm_nn_pieces(name, a, b, out_dtype=F32, tm=1024, tk=2048, epilogue=None, n_out=1):
    m, kd = a.shape
    npc, _, ns = b.shape
    tm, tk = _tile(m, tm), _tile(kd, tk)
    tn = _tile(ns, 1024)
    r = ns // tn
    dts = out_dtype if isinstance(out_dtype, (list, tuple)) else [out_dtype] * n_out
    o_spec = pl.BlockSpec((tm, tn), lambda i, j, k: (i, j))
    return _mm(name, a, b, dims=NN, grid=(m // tm, npc * r, kd // tk),
               a_spec=pl.BlockSpec((tm, tk), lambda i, j, k: (i, k)),
               b_spec=pl.BlockSpec((None, tk, tn), lambda i, j, k: (j // r, k, j % r)),
               outs=[((m, npc * ns), dt, o_spec) for dt in dts], epilogue=epilogue, acc_shape=(tm, tn))


def _mm_nt(name, a, b, out_dtype=F32, tm=1024, tn=1024, tk=2048, epilogue=None, extra=(), n_rows=None, deps=()):
    m, kd = a.shape
    n = n_rows or b.shape[0]
    tm, tn, tk = _tile(m, tm), _tile(n, tn), _tile(kd, tk)
    o_spec = pl.BlockSpec((tm, tn), lambda i, j, k: (i, j))
    return _mm(name, a, b, dims=NT, grid=(m // tm, n // tn, kd // tk),
               a_spec=pl.BlockSpec((tm, tk), lambda i, j, k: (i, k)),
               b_spec=pl.BlockSpec((tn, tk), lambda i, j, k: (j, k)),
               outs=[((m, n), out_dtype, o_spec)], extra=extra, extra_specs=[o_spec] * len(extra),
               epilogue=epilogue, acc_shape=(tm, tn), deps=deps)[0]


def _mm_nt_pieces(name, a, b, out_dtype=F32, tm=1024, tn=256, deps=()):
    m = a.shape[0]
    npc, n, ks = b.shape
    tm, tn = _tile(m, tm), _tile(n, tn)
    return _mm(name, a, b, dims=NT, grid=(m // tm, n // tn, 1),
               a_spec=pl.BlockSpec((tm, npc * ks), lambda i, j, k: (i, 0)),
               b_spec=pl.BlockSpec((npc, tn, ks), lambda i, j, k: (0, j, 0)),
               outs=[((m, n), out_dtype, pl.BlockSpec((tm, tn), lambda i, j, k: (i, j)))],
               deps=deps, b_pieces=npc)[0]


def _mm_tn(name, a, b, out_dtype=BF16, tm=1024, tn=1024, pieces=0, deps=(), side=None):
    t, m = a.shape
    n = b.shape[1]
    tm = _tile(m, tm)
    if pieces:
        ns = n // pieces
        tn = _tile(ns, tn)
        r = ns // tn
        out = ((pieces, m, ns), out_dtype, pl.BlockSpec((None, tm, tn), lambda i, j, k: (j // r, i, j % r)))
    else:
        tn = _tile(n, tn)
        out = ((m, n), out_dtype, pl.BlockSpec((tm, tn), lambda i, j, k: (i, j)))
    grid = (m // tm, n // tn, 1)
    res = _mm(name, a, b, dims=TN, grid=grid,
              a_spec=pl.BlockSpec((t, tm), lambda i, j, k: (0, i)),
              b_spec=pl.BlockSpec((t, tn), lambda i, j, k: (0, j)),
              outs=[out], deps=deps, side=side and side(grid[0] * grid[1]))
    return res if side else res[0]


def _rms_f(x, g):
    r = lax.rsqrt(jnp.mean(x * x, axis=-1, keepdims=True) + EPS)
    xh = x * r
    return xh * g, xh, r


def _rms_b(dy, g, xh, r):
    dxh = dy * g
    dx = r * (dxh - xh * jnp.mean(dxh * xh, axis=-1, keepdims=True))
    dg = jnp.sum(dy * xh, axis=0, keepdims=True)
    return dx, dg


def _rowwise(name, fn, row_ins, vec_ins, row_outs, vec_outs, deps=()):
    t = row_ins[0].shape[0]
    tr = min(ROW_TILE, t)
    nr, nv, no = len(row_ins), len(vec_ins), len(row_outs)
    first_out = nr + nv + len(deps)

    def body(*refs):
        ri, vi = refs[:nr], refs[nr:nr + nv]
        ro, vo = refs[first_out:first_out + no], refs[first_out + no:]
        routs, vouts = fn(*[r[...] for r in ri], *[v[...] for v in vi])
        for o, val in zip(ro, routs):
            o[...] = val.astype(o.dtype)
        i = pl.program_id(0)
        for o, val in zip(vo, vouts):
            @pl.when(i == 0)
            def _():
                o[...] = val

            @pl.when(i > 0)
            def _():
                o[...] += val

    res = pl.pallas_call(
        body, name=name,
        out_shape=[jax.ShapeDtypeStruct((t, w), d) for w, d in row_outs]
        + [jax.ShapeDtypeStruct((1, w), F32) for w in vec_outs],
        grid=(t // tr,),
        in_specs=[pl.BlockSpec((tr, a.shape[1]), lambda i: (i, 0)) for a in row_ins]
        + [pl.BlockSpec((1, a.shape[1]), lambda i: (0, 0)) for a in vec_ins] + [ANY_SPEC] * len(deps),
        out_specs=[pl.BlockSpec((tr, w), lambda i: (i, 0)) for w, _ in row_outs]
        + [pl.BlockSpec((1, w), lambda i: (0, 0)) for w in vec_outs],
        compiler_params=_params("arbitrary"),
    )(*row_ins, *vec_ins, *deps)
    return list(res)


def _norm_fwd(x, g, name, deps=()):
    d = x.shape[1]
    return _rowwise(name, lambda x, g: ((_rms_f(x, g)[0],), ()), [x], [g], [(d, BF16)], [], deps=deps)[0]


def _norm_bwd(x, dxn, dres, g, name):
    d = x.shape[1]

    def fn(x, dxn, dres, g):
        _, xh, r = _rms_f(x, g)
        dx, dg = _rms_b(dxn, g, xh, r)
        return (dres + dx,), (dg,)

    return _rowwise(name, fn, [x, dxn, dres], [g], [(d, F32)], [d])


def _block_fwd(h_prev, y, g_y, g_n, name, deps=()):
    d = h_prev.shape[1]

    def fn(h_prev, y, g_y, g_n):
        h = h_prev + _rms_f(y, g_y)[0]
        return (h, _rms_f(h, g_n)[0]), ()

    return _rowwise(name, fn, [h_prev, y], [g_y, g_n], [(d, F32), (d, BF16)], [], deps=deps)


def _block_bwd(h, y, d_up, d_n, g_y, g_n, name, deps=()):
    d = h.shape[1]

    def fn(h, y, d_up, d_n, g_y, g_n):
        _, hh, hr = _rms_f(h, g_n)
        dh_n, dg_n = _rms_b(d_n, g_n, hh, hr)
        dh = d_up + dh_n
        _, yh, yr = _rms_f(y, g_y)
        dy, dg_y = _rms_b(dh, g_y, yh, yr)
        return (dh, dy), (dg_y, dg_n)

    return _rowwise(name, fn, [h, y, d_up, d_n], [g_y, g_n], [(d, F32), (d, BF16)], [d, d], deps=deps)


def _loss_head(h2, z, e, target, g, name):
    d = h2.shape[1]

    def fn(h2, z, e, target, g):
        gate = jax.nn.sigmoid(z)
        y, xh, r = _rms_f(e * gate, g)
        diff = h2 + y - target
        loss = 0.5 * jnp.sum(jnp.mean(diff * diff, axis=-1, keepdims=True), axis=0, keepdims=True)
        dh3 = diff / d
        deg, dg = _rms_b(dh3, g, xh, r)
        dz = deg * e * gate * (1.0 - gate)
        de = deg * gate
        return (dh3, dz, de), (dg, jnp.broadcast_to(loss, (1, LANES)))

    return _rowwise(name, fn, [h2, z, e, target], [g], [(d, F32), (d, BF16), (d, BF16)], [d, LANES])


def _shift_down(x, s, rows):
    return jnp.where(rows >= s, pltpu.roll(x, s, axis=0), 0.0)


def _shift_up(x, s, rows):
    t = x.shape[0]
    return jnp.where(rows < t - s, pltpu.roll(x, t - s, axis=0), 0.0)


def _conv_pre(x, w, rows):
    pre = x * w[QK_CONV - 1:QK_CONV, :]
    for s in range(1, QK_CONV):
        pre = pre + _shift_down(x, s, rows) * w[QK_CONV - 1 - s:QK_CONV - s, :]
    return pre


def _conv_fwd(proj_a, w, start, name):
    t = proj_a.shape[0]
    width = w.shape[1]
    tc = _tile(width, 256)
    off = start // tc

    def body(x_ref, w_ref, y_ref):
        x = x_ref[...]
        rows = lax.broadcasted_iota(jnp.int32, x.shape, 0)
        pre = _conv_pre(x, w_ref[...], rows)
        y_ref[...] = pre * jax.nn.sigmoid(pre)

    return pl.pallas_call(
        body, name=name,
        out_shape=jax.ShapeDtypeStruct((t, width), F32),
        grid=(width // tc,),
        in_specs=[pl.BlockSpec((t, tc), lambda j: (0, off + j)), pl.BlockSpec((QK_CONV, tc), lambda j: (0, j))],
        out_specs=pl.BlockSpec((t, tc), lambda j: (0, j)),
        compiler_params=_params("parallel"),
    )(proj_a, w)


def _conv_bwd(proj_a, w, dy, start, name):
    t = proj_a.shape[0]
    width = w.shape[1]
    tc = _tile(width, 256)
    off = start // tc

    def body(x_ref, w_ref, dy_ref, dx_ref, dw_ref):
        x, w = x_ref[...], w_ref[...]
        rows = lax.broadcasted_iota(jnp.int32, x.shape, 0)
        pre = _conv_pre(x, w, rows)
        sig = jax.nn.sigmoid(pre)
        dpre = dy_ref[...] * (sig * (1.0 + pre * (1.0 - sig)))
        dx = dpre * w[QK_CONV - 1:QK_CONV, :]
        dws = [jnp.sum(dpre * x, axis=0, keepdims=True)]
        for s in range(1, QK_CONV):
            dx = dx + _shift_up(dpre, s, rows) * w[QK_CONV - 1 - s:QK_CONV - s, :]
            dws.append(jnp.sum(dpre * _shift_down(x, s, rows), axis=0, keepdims=True))
        dx_ref[...] = dx.astype(dx_ref.dtype)
        for s in range(QK_CONV):
            dw_ref[QK_CONV - 1 - s:QK_CONV - s, :] = dws[s]

    return pl.pallas_call(
        body, name=name,
        out_shape=[jax.ShapeDtypeStruct((t, width), BF16), jax.ShapeDtypeStruct((QK_CONV, width), F32)],
        grid=(width // tc,),
        in_specs=[pl.BlockSpec((t, tc), lambda j: (0, off + j)), pl.BlockSpec((QK_CONV, tc), lambda j: (0, j)),
                  pl.BlockSpec((t, tc), lambda j: (0, j))],
        out_specs=[pl.BlockSpec((t, tc), lambda j: (0, j)), pl.BlockSpec((QK_CONV, tc), lambda j: (0, j))],
        compiler_params=_params("parallel"),
    )(proj_a, w, dy)


def _dot(a, b, dims):
    return lax.dot_general(a.astype(BF16), b.astype(BF16), (dims, ((), ())), preferred_element_type=F32)


def _dot_split(ones, x):
    hi = x.astype(BF16)
    rest = x - hi.astype(F32)
    mid = rest.astype(BF16)
    lo = (rest - mid.astype(F32)).astype(BF16)
    dot = lambda part: lax.dot_general(ones, part, (NN, ((), ())), preferred_element_type=F32)
    return dot(hi) + dot(mid) + dot(lo)


def _chunk_iotas():
    rows = lax.broadcasted_iota(jnp.int32, (CHUNK, CHUNK), 0)
    cols = lax.broadcasted_iota(jnp.int32, (CHUNK, CHUNK), 1)
    return rows, cols


def _gates_fwd(proj_b, bias, col_block, heads, name):
    t = proj_b.shape[0]

    def body(x_ref, b_ref, gb_ref, gbt_ref):
        rows, cols = _chunk_iotas()
        tri = (cols <= rows).astype(BF16)
        for ci in range(t // CHUNK):
            sl = slice(ci * CHUNK, (ci + 1) * CHUNK)
            pre = x_ref[sl, :] + b_ref[...]
            log_f = jnp.minimum(pre, 0.0) - jnp.log(1.0 + jnp.exp(-jnp.abs(pre)))
            log_f = jnp.where((cols >= heads) & (cols < 2 * heads), log_f, 0.0)
            tile = jnp.where(cols < heads, pre, _dot_split(tri, log_f))
            gb_ref[sl, :] = tile
            gbt_ref[:, sl] = tile.T

    return pl.pallas_call(
        body, name=name,
        out_shape=[jax.ShapeDtypeStruct((t, LANES), F32), jax.ShapeDtypeStruct((LANES, t), F32)],
        grid=(1,),
        in_specs=[pl.BlockSpec((t, LANES), lambda i: (0, col_block)), pl.BlockSpec((1, LANES), lambda i: (0, 0))],
        out_specs=[pl.BlockSpec((t, LANES), lambda i: (0, 0)), pl.BlockSpec((LANES, t), lambda i: (0, 0))],
        compiler_params=_params("arbitrary"),
    )(proj_b, bias)


def _gate_columns(gb, heads, name):
    t = gb.shape[0]

    def body(gb_ref, out_ref):
        g = gb_ref[...]
        lane = lax.broadcasted_iota(jnp.int32, g.shape, 1)
        column = jnp.sum(jnp.where(lane == pl.program_id(0), g, 0.0), axis=-1, keepdims=True)
        out_ref[...] = jnp.broadcast_to(column, out_ref.shape)

    return pl.pallas_call(
        body, name=name,
        out_shape=jax.ShapeDtypeStruct((2 * heads, t, LANES), F32),
        grid=(2 * heads,),
        in_specs=[pl.BlockSpec((t, LANES), lambda h: (0, 0))],
        out_specs=pl.BlockSpec((None, t, LANES), lambda h: (h, 0, 0)),
        compiler_params=_params("parallel"),
    )(gb)


def _gates_bwd(proj_b, bias, dcol, drow, col_block, heads, name):
    t = proj_b.shape[0]

    def body(x_ref, b_ref, dc_ref, dr_ref, dx_ref, db_ref):
        rows, cols = _chunk_iotas()
        later = (cols >= rows).astype(BF16)
        total = jnp.zeros((1, LANES), F32)
        for ci in range(t // CHUNK):
            sl = slice(ci * CHUNK, (ci + 1) * CHUNK)
            by_row = dr_ref[:, sl].T
            d = dc_ref[sl, :] + jnp.where(cols < heads, by_row,
                                          jnp.where(cols < 2 * heads, -pltpu.roll(by_row, heads, axis=1), 0.0))
            d_log_f = _dot_split(later, jnp.where(cols >= heads, d, 0.0))
            pre = x_ref[sl, :] + b_ref[...]
            dx = jnp.where(cols < heads, d, jnp.where(cols < 2 * heads, d_log_f * jax.nn.sigmoid(-pre), 0.0))
            dx_ref[sl, :] = dx.astype(dx_ref.dtype)
            total = total + jnp.sum(dx, axis=0, keepdims=True)
        db_ref[...] = total

    return pl.pallas_call(
        body, name=name,
        out_shape=[jax.ShapeDtypeStruct((t, LANES), BF16), jax.ShapeDtypeStruct((1, LANES), F32)],
        grid=(1,),
        in_specs=[pl.BlockSpec((t, LANES), lambda i: (0, col_block)), pl.BlockSpec((1, LANES), lambda i: (0, 0)),
                  pl.BlockSpec((t, LANES), lambda i: (0, 0)), pl.BlockSpec((LANES, t), lambda i: (0, 0))],
        out_specs=[pl.BlockSpec((t, LANES), lambda i: (0, 0)), pl.BlockSpec((1, LANES), lambda i: (0, 0))],
        compiler_params=_params("arbitrary"),
    )(proj_b, bias, dcol, drow)


def _gate_views(li_ref, b_ref, gbt_ref, s, head, heads, sl):
    li_c, b_c = li_ref[s, sl, :], b_ref[s, sl, :]
    li_r = jnp.broadcast_to(gbt_ref[head, :, sl], (CHUNK, CHUNK))
    b_r = jnp.broadcast_to(gbt_ref[heads + head, :, sl], (CHUNK, CHUNK))
    return li_c, b_c, li_r, b_r


def _chunk_state(kt, k, v, li_c, b_c, ct, nrow, m_prev):
    b_tot = b_c[CHUNK - 1:CHUNK, 0:1]
    a = b_tot - b_c[:, 0:1] + li_c[:, 0:1]
    m_new = jnp.maximum(b_tot + m_prev, jnp.max(a, axis=0, keepdims=True))
    dec = jnp.exp(b_tot + m_prev - m_new)
    w = jnp.exp(a - m_new)
    wv = w * v
    ct_new = dec * ct + _dot(kt, wv, NN)
    n_new = dec * nrow + jnp.sum(w * k, axis=0, keepdims=True)
    return ct_new, n_new, m_new, dec, w, wv


def _chunk_scores(q, k, b_c, li_r, b_r, m_prev):
    rows, cols = _chunk_iotas()
    dlog = jnp.where(cols <= rows, b_c - b_r + li_r, -jnp.inf)
    inter = b_c[:, 0:1] + m_prev
    mt = jnp.maximum(inter, jnp.max(dlog, axis=-1, keepdims=True))
    dw = jnp.exp(dlog - mt)
    iw = jnp.exp(inter - mt)
    a = _dot(q, k, NT)
    return dw, iw, mt, a, a * dw


def _to_row(column):
    rows, cols = _chunk_iotas()
    return jnp.sum(jnp.where(rows == cols, jnp.broadcast_to(column, (CHUNK, CHUNK)), 0.0), axis=0, keepdims=True)


def _head_out(h, o, g):
    cell, hh, r = _rms_f(h, g)
    sig = jax.nn.sigmoid(o)
    return sig * cell, (cell, hh, r, sig)


HEADS_PER_STEP = 2


def _mlstm_fwd(qc, kc, kct, proj_a, gcols, gbt, g_mlstm, heads, d_model, name, dep):
    t = qc.shape[0]
    dh = g_mlstm.shape[1] // heads
    nc = t // CHUNK
    scale = dh ** -0.5
    hp = HEADS_PER_STEP

    def body(q_ref, k_ref, kt_ref, v_ref, o_ref, li_ref, b_ref, gbt_ref, g_ref, dep_ref, out_ref, ct_ref, n_ref, m_ref):
        del dep_ref
        step = pl.program_id(0)
        ct_ref[...] = jnp.zeros_like(ct_ref)
        n_ref[...] = jnp.zeros_like(n_ref)
        m_ref[...] = jnp.zeros_like(m_ref)

        def chunk(ci, carry):
            sl = pl.ds(pl.multiple_of(ci * CHUNK, CHUNK), CHUNK)
            for s in range(hp):
                cs = slice(s * dh, (s + 1) * dh)
                q, k, v, kt = q_ref[sl, cs], k_ref[sl, cs] * scale, v_ref[sl, cs], kt_ref[cs, sl] * scale
                li_c, b_c, li_r, b_r = _gate_views(li_ref, b_ref, gbt_ref, s, step * hp + s, heads, sl)
                ct, nrow, m_prev = ct_ref[s], n_ref[s, 0:1, :], m_ref[s, 0:1, 0:1]
                dw, iw, mt, _, sm = _chunk_scores(q, k, b_c, li_r, b_r, m_prev)
                num = _dot(jnp.concatenate([iw * q, sm], axis=1), jnp.concatenate([ct, v], axis=0), NN)
                den = iw * jnp.sum(q * nrow, axis=-1, keepdims=True) + jnp.sum(sm, axis=-1, keepdims=True)
                h = num / jnp.maximum(jnp.abs(den), jnp.exp(-mt))
                y, _ = _head_out(h, o_ref[sl, cs], g_ref[:, cs])
                out_ref[sl, cs] = y.astype(out_ref.dtype)
                ct_new, n_new, m_new, _, _, _ = _chunk_state(kt, k, v, li_c, b_c, ct, nrow, m_prev)
                ct_ref[s] = ct_new
                n_ref[s, 0:1, :] = n_new
                m_ref[s, 0:1, 0:1] = m_new
            return carry

        lax.fori_loop(0, nc, chunk, 0, unroll=2)

    col = lambda off: pl.BlockSpec((t, hp * dh), lambda h: (0, off // hp + h))
    slab = lambda off: pl.BlockSpec((hp, t, LANES), lambda h: (off // hp + h, 0, 0))
    return pl.pallas_call(
        body, name=name,
        out_shape=jax.ShapeDtypeStruct((t, d_model), BF16),
        grid=(heads // hp,),
        in_specs=[col(0), col(0), pl.BlockSpec((hp * dh, t), lambda h: (h, 0)), col(2 * heads), col(3 * heads),
                  slab(0), slab(heads), pl.BlockSpec((2 * heads, 1, t), lambda h: (0, 0, 0)),
                  pl.BlockSpec((1, hp * dh), lambda h: (0, h)), ANY_SPEC],
        out_specs=pl.BlockSpec((t, hp * dh), lambda h: (0, h)),
        scratch_shapes=[pltpu.VMEM((hp, dh, dh), F32), pltpu.VMEM((hp, 8, dh), F32), pltpu.VMEM((hp, 8, LANES), F32)],
        compiler_params=_params("arbitrary"),
    )(qc, kc, kct, proj_a, proj_a, gcols, gcols, gbt, g_mlstm, dep)


def _mlstm_bwd(qc, kc, qct, kct, proj_a, gcols, gbt, g_mlstm, dhcat, heads, name, dep):
    t = qc.shape[0]
    wm = g_mlstm.shape[1]
    dh = wm // heads
    nc = t // CHUNK
    scale = dh ** -0.5

    def body(q_ref, k_ref, qt_ref, kt_ref, v_ref, o_ref, li_ref, b_ref, gbt_ref, g_ref, dy_ref, dep_ref,
             dq_ref, dk_ref, dv_ref, do_ref, dcol_ref, drow_ref, dg_ref,
             ct_all, n_all, m_all, dct_ref, dn_ref):
        del dep_ref
        head = pl.program_id(0)
        g = g_ref[...]

        @pl.when(head == 0)
        def _():
            dcol_ref[...] = jnp.zeros_like(dcol_ref)
            drow_ref[...] = jnp.zeros_like(drow_ref)

        def load(ci):
            sl = pl.ds(pl.multiple_of(ci * CHUNK, CHUNK), CHUNK)
            return (sl, q_ref[sl, :], k_ref[sl, :] * scale, v_ref[sl, :], kt_ref[:, sl] * scale) \
                + _gate_views(li_ref, b_ref, gbt_ref, 0, head, heads, sl)

        def fwd(ci, carry):
            ct, nrow, m_prev = carry
            ct_all[ci] = ct
            n_all[ci, 0:1, :] = nrow
            m_all[ci, 0:1, 0:1] = m_prev
            _, _, k, v, kt, li_c, b_c, _, _ = load(ci)
            return _chunk_state(kt, k, v, li_c, b_c, ct, nrow, m_prev)[:3]

        lax.fori_loop(0, nc, fwd, (jnp.zeros((dh, dh), F32), jnp.zeros((1, dh), F32), jnp.zeros((1, 1), F32)))

        dct_ref[...] = jnp.zeros_like(dct_ref)
        dn_ref[...] = jnp.zeros_like(dn_ref)
        dg_ref[...] = jnp.zeros_like(dg_ref)
        rows, cols = _chunk_iotas()
        last = lax.broadcasted_iota(jnp.int32, (CHUNK, 1), 0) == CHUNK - 1
        lane = lax.broadcasted_iota(jnp.int32, (CHUNK, LANES), 1)

        def bwd(step, carry):
            ci = nc - 1 - step
            sl, q, k, v, kt, li_c, b_c, li_r, b_r = load(ci)
            qt = qt_ref[:, sl]
            ct, nrow, m_prev = ct_all[ci], n_all[ci, 0:1, :], m_all[ci, 0:1, 0:1]
            dw, iw, mt, a, sm = _chunk_scores(q, k, b_c, li_r, b_r, m_prev)
            qc_ = _dot(q, ct, NN)
            qn = jnp.sum(q * nrow, axis=-1, keepdims=True)
            num = iw * qc_ + _dot(sm, v, NN)
            den = iw * qn + jnp.sum(sm, axis=-1, keepdims=True)
            floor = jnp.exp(-mt)
            dn = jnp.maximum(jnp.abs(den), floor)
            h = num / dn
            _, (cell, hh, r, sig) = _head_out(h, o_ref[sl, :], g)
            dy = dy_ref[sl, :]
            dcell = dy * sig
            do_ref[sl, :] = (dy * cell * sig * (1.0 - sig)).astype(do_ref.dtype)
            dh_, dg = _rms_b(dcell, g, hh, r)
            dg_ref[...] += dg
            dnum = dh_ / dn
            ddn = -jnp.sum(dh_ * h, axis=-1, keepdims=True) / dn
            dden = jnp.where(jnp.abs(den) >= floor, ddn * jnp.sign(den), 0.0)
            idn = iw * dnum
            idd = iw * dden
            dw_t = jnp.exp(jnp.where(cols >= rows, b_r - b_c + li_c, -jnp.inf) - _to_row(mt))
            sm_t = _dot(k, q, NT) * dw_t
            dsm = _dot(dnum, v, NT) + dden
            da_ = dsm * dw
            da_t = (_dot(v, dnum, NT) + _to_row(dden)) * dw_t
            e = da_ * a
            dq = _dot(idn, ct, NT) + _dot(da_, k, NN) + idd * nrow
            dk = _dot(da_t, q, NN)
            dv = _dot(sm_t, dnum, NN)
            dct = _dot(qt, idn, NN)
            dnr = jnp.sum(q * idd, axis=0, keepdims=True)
            diw = jnp.sum(dnum * qc_, axis=-1, keepdims=True) + dden * qn
            db = jnp.sum(e, axis=-1, keepdims=True) + diw * iw
            _, _, _, dec, w, wv = _chunk_state(kt, k, v, li_c, b_c, ct, nrow, m_prev)
            dct_new, dn_new = dct_ref[...], dn_ref[0:1, :]
            ddec = jnp.sum(jnp.sum(dct_new * ct, axis=-1, keepdims=True), axis=0, keepdims=True) \
                + jnp.sum(dn_new * nrow, axis=-1, keepdims=True)
            dk = dk + _dot(wv, dct_new, NT) + w * dn_new
            dwv = _dot(k, dct_new, NN)
            dv = dv + w * dwv
            dwt = jnp.sum(dwv * v, axis=-1, keepdims=True) + jnp.sum(k * dn_new, axis=-1, keepdims=True)
            da = dwt * w
            dbtot = jnp.sum(da, axis=0, keepdims=True) + ddec * dec
            db = db - da + jnp.where(last, dbtot, 0.0)
            dct_ref[...] = dec * dct_new + dct
            dn_ref[0:1, :] = dec * dn_new + dnr
            dq_ref[sl, :] = dq
            dk_ref[sl, :] = dk * scale
            dv_ref[sl, :] = dv.astype(dv_ref.dtype)
            dcol_ref[sl, :] += jnp.where(lane == head, da, 0.0) + jnp.where(lane == heads + head, db, 0.0)
            drow_ref[head, :, sl] = jnp.sum(e, axis=0, keepdims=True)
            return carry

        lax.fori_loop(0, nc, bwd, 0, unroll=4)

    col = lambda off: pl.BlockSpec((t, dh), lambda h: (0, off + h))
    rowb = pl.BlockSpec((dh, t), lambda h: (h, 0))
    slab = lambda off: pl.BlockSpec((1, t, LANES), lambda h: (off + h, 0, 0))
    vec = pl.BlockSpec((1, dh), lambda h: (0, h))
    return pl.pallas_call(
        body, name=name,
        out_shape=[jax.ShapeDtypeStruct((t, wm), F32)] * 2 + [jax.ShapeDtypeStruct((t, wm), BF16)] * 2
        + [jax.ShapeDtypeStruct((t, LANES), F32), jax.ShapeDtypeStruct((heads, 1, t), F32),
           jax.ShapeDtypeStruct((1, wm), F32)],
        grid=(heads,),
        in_specs=[col(0), col(0), rowb, rowb, col(2 * heads), col(3 * heads), slab(0), slab(heads),
                  pl.BlockSpec((2 * heads, 1, t), lambda h: (0, 0, 0)), vec, col(0), ANY_SPEC],
        out_specs=[col(0), col(0), col(0), col(0), pl.BlockSpec((t, LANES), lambda h: (0, 0)),
                   pl.BlockSpec((heads, 1, t), lambda h: (0, 0, 0)), vec],
        scratch_shapes=[pltpu.VMEM((nc, dh, dh), F32), pltpu.VMEM((nc, 8, dh), F32), pltpu.VMEM((nc, 8, LANES), F32),
                        pltpu.VMEM((dh, dh), F32), pltpu.VMEM((8, dh), F32)],
        compiler_params=_params("arbitrary"),
    )(qc, kc, qct, kct, proj_a, proj_a, gcols, gcols, gbt, g_mlstm, dhcat, dep)


def _pool_window(gi):
    return jnp.where(gi == 0, 2.0, jnp.where(gi == 1, 4.0, jnp.where(gi == 2, 8.0, 16.0))).astype(F32)


def _pool_select(gi, levels):
    return jnp.where(gi == 0, levels[0], jnp.where(gi == 1, levels[1], jnp.where(gi == 2, levels[2], levels[3])))


def _pooled(u, gi, rows):
    s, levels = u, []
    for lvl in range(POOL_GROUPS):
        s = s + _shift_down(s, 1 << lvl, rows)
        levels.append(s)
    cnt = jnp.minimum((rows + 1).astype(F32), _pool_window(gi))
    return _pool_select(gi, levels) / cnt - u, cnt


def _pool_fwd(proj_b, w_pool, scale, hcat, name):
    t, d_model = hcat.shape
    g, cg, _ = w_pool.shape
    off = (d_model - g * cg) // cg

    def body(u_ref, w_ref, s_ref, hcat_ref, out_ref):
        del hcat_ref
        gi = pl.program_id(0)
        u = u_ref[...]
        rows = lax.broadcasted_iota(jnp.int32, u.shape, 0)
        pooled, _ = _pooled(u, gi, rows)
        out_ref[...] = (_dot(pooled, w_ref[...], NN) * s_ref[...]).astype(out_ref.dtype)

    return pl.pallas_call(
        body, name=name,
        out_shape=jax.ShapeDtypeStruct((t, d_model), BF16),
        grid=(g,),
        in_specs=[pl.BlockSpec((t, cg), lambda gi: (0, gi)), pl.BlockSpec((None, cg, cg), lambda gi: (gi, 0, 0)),
                  pl.BlockSpec((1, cg), lambda gi: (0, gi)), HBM_SPEC],
        out_specs=pl.BlockSpec((t, cg), lambda gi: (0, off + gi)),
        input_output_aliases={3: 0},
        compiler_params=_params("arbitrary"),
    )(proj_b, w_pool, scale, hcat)


def _pool_bwd(proj_b, w_pool, scale, dhcat, name):
    t, d_model = dhcat.shape
    g, cg, _ = w_pool.shape
    off = (d_model - g * cg) // cg

    def body(u_ref, w_ref, s_ref, dy_ref, du_ref, dw_ref, ds_ref):
        gi = pl.program_id(0)
        u, w, dy = u_ref[...], w_ref[...], dy_ref[...]
        rows = lax.broadcasted_iota(jnp.int32, u.shape, 0)
        pooled, cnt = _pooled(u, gi, rows)
        mixed = _dot(pooled, w, NN)
        ds_ref[...] = jnp.sum(dy * mixed, axis=0, keepdims=True)
        dmixed = dy * s_ref[...]
        dw_ref[...] = _dot(pooled, dmixed, TN).astype(dw_ref.dtype)
        dpooled = _dot(dmixed, w, NT)
        s, levels = dpooled / cnt, []
        for lvl in range(POOL_GROUPS):
            s = s + _shift_up(s, 1 << lvl, rows)
            levels.append(s)
        du_ref[...] = (_pool_select(gi, levels) - dpooled).astype(du_ref.dtype)

    return pl.pallas_call(
        body, name=name,
        out_shape=[jax.ShapeDtypeStruct((t, g * cg), BF16), jax.ShapeDtypeStruct((g, cg, cg), BF16),
                   jax.ShapeDtypeStruct((1, g * cg), F32)],
        grid=(g,),
        in_specs=[pl.BlockSpec((t, cg), lambda gi: (0, gi)), pl.BlockSpec((None, cg, cg), lambda gi: (gi, 0, 0)),
                  pl.BlockSpec((1, cg), lambda gi: (0, gi)), pl.BlockSpec((t, cg), lambda gi: (0, off + gi))],
        out_specs=[pl.BlockSpec((t, cg), lambda gi: (0, gi)), pl.BlockSpec((None, cg, cg), lambda gi: (gi, 0, 0)),
                   pl.BlockSpec((1, cg), lambda gi: (0, gi))],
        compiler_params=_params("arbitrary"),
    )(proj_b, w_pool, scale, dhcat)


def _pad_cols(a, width):
    return jnp.pad(a, ((0, 0), (0, width - a.shape[1])))


def kernel(x, p, w_in, b_gates, w_qk_conv, g_mlstm, w_pool, pool_scale, w_out, g_mix_pre, g_mix_post, w_ff1, w_ff2, g_ff_pre, g_ff_post, w_ple_proj, w_ple_gate, g_ple_gate, g_ple_post, loss_target, m_w_in, m_b_gates, m_w_qk_conv, m_g_mlstm, m_w_pool, m_pool_scale, m_w_out, m_g_mix_pre, m_g_mix_post, m_w_ff1, m_w_ff2, m_g_ff_pre, m_g_ff_post, m_w_ple_proj, m_w_ple_gate, m_g_ple_gate, m_g_ple_post, v_w_in, v_b_gates, v_w_qk_conv, v_g_mlstm, v_w_pool, v_pool_scale, v_w_out, v_g_mix_pre, v_g_mix_post, v_w_ff1, v_w_ff2, v_g_ff_pre, v_g_ff_post, v_w_ple_proj, v_w_ple_gate, v_g_ple_gate, v_g_ple_post):
    weights = dict(w_in=w_in, b_gates=b_gates, w_qk_conv=w_qk_conv, g_mlstm=g_mlstm, w_pool=w_pool,
                   pool_scale=pool_scale, w_out=w_out, g_mix_pre=g_mix_pre, g_mix_post=g_mix_post, w_ff1=w_ff1,
                   w_ff2=w_ff2, g_ff_pre=g_ff_pre, g_ff_post=g_ff_post, w_ple_proj=w_ple_proj,
                   w_ple_gate=w_ple_gate, g_ple_gate=g_ple_gate, g_ple_post=g_ple_post)
    mom1 = dict(w_in=m_w_in, b_gates=m_b_gates, w_qk_conv=m_w_qk_conv, g_mlstm=m_g_mlstm, w_pool=m_w_pool,
                pool_scale=m_pool_scale, w_out=m_w_out, g_mix_pre=m_g_mix_pre, g_mix_post=m_g_mix_post,
                w_ff1=m_w_ff1, w_ff2=m_w_ff2, g_ff_pre=m_g_ff_pre, g_ff_post=m_g_ff_post,
                w_ple_proj=m_w_ple_proj, w_ple_gate=m_w_ple_gate, g_ple_gate=m_g_ple_gate, g_ple_post=m_g_ple_post)
    mom2 = dict(w_in=v_w_in, b_gates=v_b_gates, w_qk_conv=v_w_qk_conv, g_mlstm=v_g_mlstm, w_pool=v_w_pool,
                pool_scale=v_pool_scale, w_out=v_w_out, g_mix_pre=v_g_mix_pre, g_mix_post=v_g_mix_post,
                w_ff1=v_w_ff1, w_ff2=v_w_ff2, g_ff_pre=v_g_ff_pre, g_ff_post=v_g_ff_post,
                w_ple_proj=v_w_ple_proj, w_ple_gate=v_w_ple_gate, g_ple_gate=v_g_ple_gate, g_ple_post=v_g_ple_post)
    order = list(weights)

    t, d = x.shape[1], x.shape[2]
    heads = MLSTM_HEADS
    wm = g_mlstm.shape[1]
    pw = pool_scale.shape[1]
    cg = pw // POOL_GROUPS
    in_cols = 4 * wm + 2 * heads + pw
    in_shard = w_in.shape[2]
    xs, ps, tgt = x[0], p[0, 0], loss_target[0]

    mx, my, mc = lax.axis_index("x"), lax.axis_index("y"), lax.axis_index("c")
    core = jnp.reshape(mc, (1,)).astype(jnp.int32)
    chip = jnp.reshape(2 * mx + my, (1,)).astype(jnp.int32)

    first = [w_in[0].T.astype(BF16), w_qk_conv[0], w_pool[0].reshape(POOL_GROUPS * w_pool.shape[2], cg).astype(BF16)]
    started, token = _gather_start([first], "gather_start_in")
    anchor = token[0, 0]
    later = lambda w: (w[0] + anchor).astype(BF16)
    rest, token = _gather_start([[later(w_out)], [later(w_ff1)], [later(w_ff2)], [later(w_ple_gate), later(w_ple_proj)]],
                                "gather_start_rest")
    started = started + rest
    bias = _pad_cols(b_gates, LANES)
    small = ["b_gates", "g_mlstm", "pool_scale", "g_mix_pre", "g_mix_post", "g_ff_pre", "g_ff_post", "g_ple_gate",
             "g_ple_post"]
    pack = lambda src: _pad_rows16(jnp.concatenate([_pad_cols(src[k] + anchor, d) for k in small], axis=0))
    small_state = [pack(weights), pack(mom1), pack(mom2)]
    in_state = [a[0].T + anchor for a in (w_in, m_w_in, v_w_in)]

    xn = _norm_fwd(xs, g_mix_pre, "norm_mix_pre", deps=[token])
    g_in, g_conv, g_pool = _forward_pair(
        _gather_wait(started[0], [xn] + small_state + in_state, "gather_wait_in"), "gather_pair_in")
    in_t_a = g_in.reshape(in_cols, d)
    in_t_b = jnp.concatenate([in_t_a[4 * wm + 2 * heads:], in_t_a[4 * wm:4 * wm + 2 * heads],
                              jnp.zeros((LANES - 2 * heads, d), BF16)], axis=0)
    conv_w = g_conv.transpose(1, 0, 2).reshape(QK_CONV, 2 * wm)
    pool_w = g_pool.reshape(N_DEV, POOL_GROUPS, cg // N_DEV, cg).transpose(1, 0, 2, 3).reshape(POOL_GROUPS, cg, cg)
    proj_a = _mm_nt("proj_qkvo", xn, in_t_a, n_rows=4 * wm)
    proj_b = _mm_nt("proj_pool_gates", xn, in_t_b)
    qc = _conv_fwd(proj_a, conv_w[:, :wm], 0, "q_conv")
    kc = _conv_fwd(proj_a, conv_w[:, wm:], wm, "k_conv")
    gb, gbt = _gates_fwd(proj_b, bias, pw // LANES, heads, "gates")
    gcols = _gate_columns(gb, heads, "gate_columns")
    gbt = gbt[:2 * heads].reshape(2 * heads, 1, t)
    qct, kct = qc.T, kc.T
    fwd_out = _forward_start(_gather_wait(started[1], gb, "gather_wait_out"), "gather_pair_start_out")
    hcat = _mlstm_fwd(qc, kc, kct, proj_a, gcols, gbt, g_mlstm, heads, d, "mlstm", fwd_out["token"])
    hcat = _pool_fwd(proj_b, pool_w, pool_scale, hcat, "pool")
    (g_out,) = _forward_wait(fwd_out, hcat, "gather_pair_wait_out")
    out_w = g_out.reshape(d, d)
    mix = _mm_nn("mix_out", hcat, out_w)[0]
    fwd_ff1 = _forward_start(_gather_wait(started[2], mix, "gather_wait_ff1"), "gather_pair_start_ff1")
    h1, hn = _block_fwd(xs, mix, g_mix_post, g_ff_pre, "residual_mix", deps=[fwd_ff1["token"]])
    (g_ff1,) = _forward_wait(fwd_ff1, hn, "gather_pair_wait_ff1")
    relu_a, act = _mm_nn_pieces("ff1", hn, g_ff1, out_dtype=[BF16, BF16],
                                epilogue=lambda acc: (jnp.maximum(acc, 0.0), jnp.square(jnp.maximum(acc, 0.0))))
    (g_ff2,) = _forward_pair(_gather_wait(started[3], act, "gather_wait_ff2"), "gather_pair_ff2")
    ff2_w = g_ff2.reshape(-1, d)
    ff = _mm_nn("ff2", act, ff2_w, tn=256, tk=4 * d)[0]
    fwd_ple = _forward_start(_gather_wait(started[4], ff, "gather_wait_ple"), "gather_pair_start_ple")
    h2, hg = _block_fwd(h1, ff, g_ff_post, g_ple_gate, "residual_ff", deps=[fwd_ple["token"]])
    g_pgate, g_pproj = _forward_wait(fwd_ple, hg, "gather_pair_wait_ple")
    pgate_w = g_pgate.reshape(d, d)
    z = _mm_nn("ple_gate", hg, pgate_w)[0]
    e = _mm_nn_pieces("ple_proj", ps, g_pproj)[0]
    dh3, dz, de, dg_ple_post, loss_part = _loss_head(h2, z, e, tgt, g_ple_post, "loss_head")
    loss = lax.psum(loss_part[0, 0], MESH_AXES)

    out = {}

    def state(k):
        if k == "w_in":
            return in_state
        return [a.reshape(-1, a.shape[-1]) for a in (weights[k], mom1[k], mom2[k])]

    def pair_begin(pieces, tag):
        keys = list(pieces)
        return keys, _pair_start([pieces[k] for k in keys], "reduce_pair_start_" + tag)

    def chip_begin(group, after, tag):
        keys, begun = group
        mine, got = _pair_wait(begun, after, "reduce_pair_wait_" + tag)
        sums = [_pair_sum(p_, g, core, "pair_sum_" + k) for k, p_, g in zip(keys, mine, got)]
        return keys, _chip_start(sums, "reduce_start_" + tag)

    def reduce_start(pieces, tag):
        keys = list(pieces)
        got = _pair_exchange([pieces[k] for k in keys], "reduce_pair_" + tag)
        sums = [_pair_sum(pieces[k], g, core, "pair_sum_" + k) for k, g in zip(keys, got)]
        return keys, _chip_start(sums, "reduce_start_" + tag)

    def reduce_finish(group, after, tag):
        keys, begun = group
        sums, got = _chip_wait(begun, after, "reduce_wait_" + tag)
        for k, s, g in zip(keys, sums, got):
            res = _adamw_shard(*state(k), s, g, chip, "adamw_" + k)
            out[k] = [r.T[None] if k == "w_in" else r.reshape(weights[k].shape) for r in res]

    token_of = lambda group: group[1]["token"]
    dhg = _mm_nt("d_ple_gate_in", dz, pgate_w)
    gw_pgate = _mm_tn("gw_ple_gate", hg, dz)
    gw_pproj = _mm_tn("gw_ple_proj", ps, de, pieces=N_DEV)
    pair_ple = pair_begin({"w_ple_gate": gw_pgate.reshape(N_DEV, d // N_DEV, d), "w_ple_proj": gw_pproj}, "ple")
    dh2, dff, dg_ff_post, dg_ple_gate = _block_bwd(h2, ff, dh3, dhg, g_ff_post, g_ple_gate, "d_residual_ff",
                                                   deps=[token_of(pair_ple)])
    red_ple = chip_begin(pair_ple, dff, "ple")
    da = _mm_nt("d_ff2_in", dff, ff2_w, out_dtype=BF16, extra=[relu_a],
                epilogue=lambda acc, r: (acc * (2.0 * r.astype(F32)),), deps=[token_of(red_ple)])
    gw_ff2 = _mm_tn("gw_ff2", act, dff)
    pair_ff2 = pair_begin({"w_ff2": gw_ff2.reshape(N_DEV, -1, d)}, "ff2")
    dhn = _mm_nt_pieces("d_ff1_in", da, g_ff1, deps=[token_of(pair_ff2)])
    place = jnp.concatenate([core, chip])
    mine_ff2, got_ff2 = _pair_wait(pair_ff2[1], dhn, "reduce_pair_wait_ff2")
    gw_ff1, sums_ff2 = _mm_tn("gw_ff1", hn, da, pieces=N_DEV, side=_side_pair_sum(mine_ff2[0], got_ff2[0], place))
    red_ff2 = (["w_ff2"], _chip_start([sums_ff2], "reduce_start_ff2"))
    reduce_finish(red_ple, token_of(red_ff2), "ple")
    pair_ff1 = pair_begin({"w_ff1": gw_ff1}, "ff1")
    dh1, dmix, dg_mix_post, dg_ff_pre = _block_bwd(h1, mix, dh2, dhn, g_mix_post, g_ff_pre, "d_residual_mix",
                                                   deps=[token_of(pair_ff1), token_of(red_ff2)])
    dhcat = _mm_nt("d_mix_in", dmix, out_w)
    red_ff1 = chip_begin(pair_ff1, dhcat, "ff1")
    gw_out = _mm_tn("gw_out", hcat, dmix, deps=[token_of(red_ff1)])
    du, gw_pool, dg_pool_scale = _pool_bwd(proj_b, pool_w, pool_scale, dhcat, "d_pool")
    pair_out = pair_begin({
        "w_out": gw_out.reshape(N_DEV, d // N_DEV, d),
        "w_pool": gw_pool.reshape(POOL_GROUPS, N_DEV, cg // N_DEV, cg).transpose(1, 0, 2, 3).reshape(N_DEV, -1, cg)},
        "out")
    dq, dk, dv, do, dcol, drow, dg_mlstm = _mlstm_bwd(qc, kc, qct, kct, proj_a, gcols, gbt, g_mlstm, dhcat, heads,
                                                      "d_mlstm", token_of(pair_out))
    red_out = chip_begin(pair_out, dq, "out")
    drow = jnp.pad(drow.reshape(heads, t), ((0, LANES - heads), (0, 0)))
    dgates, dbias = _gates_bwd(proj_b, bias, dcol, drow, pw // LANES, heads, "d_gates")
    dq_pre, gw_conv_q = _conv_bwd(proj_a, conv_w[:, :wm], dq, 0, "d_q_conv")
    dk_pre, gw_conv_k = _conv_bwd(proj_a, conv_w[:, wm:], dk, wm, "d_k_conv")
    gw_conv = jnp.concatenate([gw_conv_q, gw_conv_k], axis=1)
    dproj_a = jnp.concatenate([dq_pre, dk_pre, dv, do], axis=1)
    dproj_b = jnp.concatenate([du, dgates], axis=1)
    sums_ff2, got_ff2 = _chip_wait(red_ff2[1], token_of(red_out), "reduce_wait_ff2")
    gw_a, *update = _mm_tn("gw_in_qkvo", dproj_a, xn,
                           side=_side_adamw(*state("w_ff2"), sums_ff2[0], got_ff2[0], place))
    out["w_ff2"] = [r.reshape(weights["w_ff2"].shape) for r in update]
    gw_b = _mm_tn("gw_in_pool_gates", dproj_b, xn)
    gw_in_t = jnp.concatenate([gw_a, gw_b[pw:pw + 2 * heads], gw_b[:pw]], axis=0)
    red_in = reduce_start({
        "w_in": gw_in_t.reshape(N_DEV, in_shard, d),
        "w_qk_conv": gw_conv.reshape(QK_CONV, N_DEV, -1).transpose(1, 0, 2).astype(BF16)}, "in")
    reduce_finish(red_out, red_in[1]["token"], "out")
    sums_ff1, got_ff1 = _chip_wait(red_ff1[1], token_of(red_in), "reduce_wait_ff1")
    dxn_a, *update = _mm_nn("d_proj_qkvo_in", dproj_a, in_t_a, tn=512, tk=4 * wm, deps=[token_of(red_in)],
                            side=_side_adamw(*state("w_ff1"), sums_ff1[0], got_ff1[0], place))
    out["w_ff1"] = [r.reshape(weights["w_ff1"].shape) for r in update]
    dxn = _mm_nn("d_proj_in", dproj_b, in_t_b, extra=[dxn_a], epilogue=lambda acc, prev: (acc + prev,))[0]
    grad_x, dg_mix_pre = _norm_bwd(xs, dxn, dh1, g_mix_pre, "d_norm_mix_pre")

    small_grads = dict(b_gates=dbias[:, :2 * heads], g_mlstm=dg_mlstm, pool_scale=dg_pool_scale, g_mix_pre=dg_mix_pre,
                       g_mix_post=dg_mix_post, g_ff_pre=dg_ff_pre, g_ff_post=dg_ff_post, g_ple_gate=dg_ple_gate,
                       g_ple_post=dg_ple_post)
    small_begun = _everyone_start([pack(small_grads)], "small_grads_start")
    reduce_finish(red_in, small_begun["token"], "in")
    (small_parts,) = _everyone_wait(small_begun, out["w_in"][0], "small_grads_wait")
    res = _adamw_replicated(*small_state, small_parts, "adamw_replicated")
    for i, k in enumerate(small):
        width = weights[k].shape[1]
        out[k] = [r[i:i + 1, :width] for r in res]

    return (loss, grad_x[None], *[out[k][0] for k in order], *[out[k][1] for k in order],
            *[out[k][2] for k in order], *[out[k][3] for k in order])


def _pad_rows16(a):
    return jnp.pad(a, ((0, 16 - a.shape[0]), (0, 0)))
```

```python
import functools

import jax
import jax.numpy as jnp
from jax import lax
from jax.experimental import pallas as pl
from jax.experimental.pallas import tpu as pltpu

F32 = jnp.float32
BF16 = jnp.bfloat16
EPS = 1e-6
N_DEV = 8
N_CHIP = 4
LANES = 128
VMEM_LIMIT = 56 * 1024 * 1024
MLSTM_HEADS = 8
POOL_GROUPS = 4
QK_CONV = 4
CHUNK = 128
ROW_TILE = 256
ADAM_LR, ADAM_B1, ADAM_B2, ADAM_EPS, ADAM_WD, ADAM_STEP = 0.001, 0.9, 0.999, 1e-08, 0.01, 10
MESH_AXES = ("x", "y", "c")
MESH = pl.DeviceIdType.MESH
HBM_SPEC = pl.BlockSpec(memory_space=pltpu.HBM)
SEM_SPEC = pl.BlockSpec(memory_space=pltpu.SEMAPHORE)
DATAFLOW = pltpu.SideEffectType.DATAFLOW_SIDE_EFFECTING


def _params(*sem):
    if sem:
        return pltpu.CompilerParams(dimension_semantics=sem, vmem_limit_bytes=VMEM_LIMIT)
    return pltpu.CompilerParams(vmem_limit_bytes=VMEM_LIMIT)


def _tile(n, pref):
    if n <= pref:
        return n
    t = (pref // LANES) * LANES
    while t >= LANES:
        if n % t == 0:
            return t
        t -= LANES
    return n


def _all_gather(arrs, name):
    n = len(arrs)

    def body(*refs):
        ins, outs = refs[:n], refs[n:2 * n]
        send, recv, loc = refs[2 * n:]
        x, y, c = lax.axis_index("x"), lax.axis_index("y"), lax.axis_index("c")
        me = (x, y, c)
        sibling = (x, y, 1 - c)
        chips = [(1 - x, y), (x, 1 - y), (1 - x, 1 - y)]

        def row(dev):
            return 4 * dev[0] + 2 * dev[1] + dev[2]

        def copy(w, k, block_of, to, src=None):
            dst = outs[w].at[row(block_of)]
            return pltpu.make_async_remote_copy(
                src_ref=dst if src is None else src, dst_ref=dst,
                send_sem=send.at[7 * w + k], recv_sem=recv.at[7 * w + k],
                device_id=to, device_id_type=MESH)

        started = []
        local = []
        for w in range(n):
            cp = pltpu.make_async_copy(ins[w], outs[w].at[row(me)], loc.at[w])
            cp.start()
            local.append(cp)
            first = [copy(w, 0, me, sibling, src=ins[w])]
            first += [copy(w, 1 + j, me, (*chip, c), src=ins[w]) for j, chip in enumerate(chips)]
            for cp in first:
                cp.start()
            started += first
        for j, chip in enumerate(chips):
            for w in range(n):
                copy(w, 1 + j, (*chip, c), me).wait_recv()
                fwd = copy(w, 4 + j, (*chip, c), sibling)
                fwd.start()
                started.append(fwd)
        for w in range(n):
            copy(w, 0, sibling, me).wait_recv()
            for j, chip in enumerate(chips):
                copy(w, 4 + j, (*chip, 1 - c), me).wait_recv()
        for cp in started:
            cp.wait_send()
        for cp in local:
            cp.wait()

    outs = pl.pallas_call(
        body, name=name,
        out_shape=[jax.ShapeDtypeStruct((N_DEV,) + a.shape, a.dtype) for a in arrs],
        in_specs=[HBM_SPEC] * n, out_specs=[HBM_SPEC] * n,
        scratch_shapes=[pltpu.SemaphoreType.DMA((7 * n,)), pltpu.SemaphoreType.DMA((7 * n,)),
                        pltpu.SemaphoreType.DMA((n,))],
    )(*arrs)
    return list(outs)


def _pair_exchange(pieces, name):
    n = len(pieces)

    def body(*refs):
        ins, outs = refs[:n], refs[n:2 * n]
        send, recv = refs[2 * n:]
        x, y, c = lax.axis_index("x"), lax.axis_index("y"), lax.axis_index("c")
        sibling = (x, y, 1 - c)

        def copy(w, chip, src_core):
            return pltpu.make_async_remote_copy(
                src_ref=ins[w].at[2 * chip + src_core], dst_ref=outs[w].at[chip],
                send_sem=send.at[N_CHIP * w + chip], recv_sem=recv.at[N_CHIP * w + chip],
                device_id=sibling, device_id_type=MESH)

        cps = [copy(w, chip, 1 - c) for w in range(n) for chip in range(N_CHIP)]
        for cp in cps:
            cp.start()
        for cp in cps:
            cp.wait_recv()
        for cp in cps:
            cp.wait_send()

    outs = pl.pallas_call(
        body, name=name,
        out_shape=[jax.ShapeDtypeStruct((N_CHIP,) + a.shape[1:], a.dtype) for a in pieces],
        in_specs=[HBM_SPEC] * n, out_specs=[HBM_SPEC] * n,
        scratch_shapes=[pltpu.SemaphoreType.DMA((N_CHIP * n,)), pltpu.SemaphoreType.DMA((N_CHIP * n,))],
    )(*pieces)
    return list(outs)


def _hbm(a):
    return pltpu.with_memory_space_constraint(a, pltpu.HBM)


def _mesh_place():
    x, y, c = lax.axis_index("x"), lax.axis_index("y"), lax.axis_index("c")
    return x, y, c, [(1 - x, y), (x, 1 - y), (1 - x, 1 - y)]


def _chip_copies(srcs, lands, send, recv):
    x, y, c, chips = _mesh_place()
    return [pltpu.make_async_remote_copy(
        src_ref=srcs[w].at[2 * chip[0] + chip[1]], dst_ref=lands[w].at[k],
        send_sem=send.at[3 * w + k], recv_sem=recv.at[3 * w + k],
        device_id=(*chip, c), device_id_type=MESH)
        for k, chip in enumerate(chips) for w in range(len(srcs))]


def _chip_start(sums, name):
    return _split_start(_chip_copies, sums, [lax.empty((3,) + a.shape[1:], a.dtype) for a in sums], 3, name)


def _chip_wait(started, after, name):
    return _split_wait(_chip_copies, started, after, name)


def _pair_start(pieces, name):
    return _split_start(_pair_copies, pieces, [lax.empty((N_CHIP,) + a.shape[1:], a.dtype) for a in pieces], N_CHIP, name)


def _pair_wait(started, after, name):
    return _split_wait(_pair_copies, started, after, name)


def _pair_copies(srcs, lands, send, recv):
    x, y, c, _ = _mesh_place()
    return [pltpu.make_async_remote_copy(
        src_ref=srcs[w].at[2 * ch + 1 - c], dst_ref=lands[w].at[ch],
        send_sem=send.at[N_CHIP * w + ch], recv_sem=recv.at[N_CHIP * w + ch],
        device_id=(x, y, 1 - c), device_id_type=MESH)
        for w in range(len(srcs)) for ch in range(N_CHIP)]


def _split_start(copies_of, srcs, lands, per_array, name):
    ns, nb = len(srcs), len(srcs) + len(lands)
    n_copies = per_array * len(lands)

    def body(*refs):
        for cp in copies_of(refs[:ns], refs[ns:nb], refs[nb], refs[nb + 1]):
            cp.start()
        refs[-1][...] = jnp.zeros_like(refs[-1])

    res = pl.pallas_call(
        body, name=name,
        out_shape=[pltpu.SemaphoreType.DMA((n_copies,)), pltpu.SemaphoreType.DMA((n_copies,))]
        + [pltpu.HBM(a.shape, a.dtype) for a in list(srcs) + list(lands)] + [jax.ShapeDtypeStruct((8, LANES), F32)],
        in_specs=[HBM_SPEC] * nb,
        out_specs=[SEM_SPEC, SEM_SPEC] + [HBM_SPEC] * nb + [pl.BlockSpec(memory_space=pltpu.VMEM)],
        input_output_aliases={i: 2 + i for i in range(nb)},
        compiler_params=pltpu.CompilerParams(has_side_effects=DATAFLOW),
    )(*[_hbm(a) for a in srcs], *[_hbm(a) for a in lands])
    return dict(send=res[0], recv=res[1], srcs=list(res[2:2 + ns]), lands=list(res[2 + ns:2 + nb]), token=res[-1])


def _split_wait(copies_of, started, after, name):
    ns = len(started["srcs"])
    nb = ns + len(started["lands"])

    def body(*refs):
        for cp in copies_of(refs[:ns], refs[ns:nb], refs[nb], refs[nb + 1]):
            cp.wait_send()
            cp.wait_recv()

    res = pl.pallas_call(
        body, name=name,
        out_shape=[pltpu.HBM(a.shape, a.dtype) for a in started["srcs"] + started["lands"]],
        in_specs=[HBM_SPEC] * nb + [SEM_SPEC, SEM_SPEC, ANY_SPEC],
        out_specs=[HBM_SPEC] * nb,
        input_output_aliases={i: i for i in range(nb)},
        compiler_params=pltpu.CompilerParams(has_side_effects=DATAFLOW),
    )(*started["srcs"], *started["lands"], started["send"], started["recv"], after)
    return list(res[:ns]), list(res[ns:])


def _forward_copies(srcs, lands, send, recv):
    del srcs
    x, y, c, chips = _mesh_place()
    copies = []
    for w in range(len(lands)):
        for j, chip in enumerate(chips):
            rows = lands[w].at[4 * chip[0] + 2 * chip[1] + c]
            copies.append(pltpu.make_async_remote_copy(
                src_ref=rows, dst_ref=rows, send_sem=send.at[3 * w + j], recv_sem=recv.at[3 * w + j],
                device_id=(x, y, 1 - c), device_id_type=MESH))
    return copies


def _forward_start(lands, name):
    return _split_start(_forward_copies, [], lands, 3, name)


def _forward_wait(started, after, name):
    return _split_wait(_forward_copies, started, after, name)[1]


def _everyone_copies(srcs, lands, send, recv):
    x, y, c, _ = _mesh_place()
    me = 4 * x + 2 * y + c
    copies = []
    for w in range(len(srcs)):
        for k in range(N_DEV - 1):
            flip = k + 1
            peer = (1 - x if flip & 4 else x, 1 - y if flip & 2 else y, 1 - c if flip & 1 else c)
            copies.append(pltpu.make_async_remote_copy(
                src_ref=srcs[w], dst_ref=lands[w].at[me], send_sem=send.at[7 * w + k], recv_sem=recv.at[7 * w + k],
                device_id=peer, device_id_type=MESH))
    return copies


def _everyone_start(blocks, name):
    me = 4 * lax.axis_index("x") + 2 * lax.axis_index("y") + lax.axis_index("c")
    lands = [lax.dynamic_update_slice(lax.empty((N_DEV,) + a.shape, a.dtype), a[None], (me,) + (0,) * a.ndim)
             for a in blocks]
    return _split_start(_everyone_copies, blocks, lands, N_DEV - 1, name)


def _everyone_wait(started, after, name):
    return _split_wait(_everyone_copies, started, after, name)[1]


def _gather_copies(srcs, lands, send, recv):
    x, y, c, chips = _mesh_place()
    me = 4 * x + 2 * y + c
    targets = [(x, y, 1 - c)] + [(*chip, c) for chip in chips]
    return [pltpu.make_async_remote_copy(
        src_ref=srcs[w], dst_ref=lands[w].at[me], send_sem=send.at[4 * w + k], recv_sem=recv.at[4 * w + k],
        device_id=to, device_id_type=MESH)
        for w in range(len(srcs)) for k, to in enumerate(targets)]


def _gather_start(groups, name):
    sizes = [len(g) for g in groups]
    flat = [a for g in groups for a in g]
    n, ng = len(flat), len(groups)
    me = 4 * lax.axis_index("x") + 2 * lax.axis_index("y") + lax.axis_index("c")
    lands = [lax.dynamic_update_slice(lax.empty((N_DEV,) + a.shape, a.dtype), a[None], (me,) + (0,) * a.ndim)
             for a in flat]

    def body(*refs):
        srcs, zones = refs[:n], refs[n:2 * n]
        sems = refs[2 * n:2 * n + 2 * ng]
        token = refs[-1]
        lo = 0
        for gi, size in enumerate(sizes):
            for cp in _gather_copies(srcs[lo:lo + size], zones[lo:lo + size], sems[2 * gi], sems[2 * gi + 1]):
                cp.start()
            lo += size
        token[...] = jnp.zeros_like(token)

    sem_shapes = []
    for size in sizes:
        sem_shapes += [pltpu.SemaphoreType.DMA((4 * size,))] * 2
    res = pl.pallas_call(
        body, name=name,
        out_shape=sem_shapes + [pltpu.HBM(a.shape, a.dtype) for a in flat + lands] + [jax.ShapeDtypeStruct((8, LANES), F32)],
        in_specs=[HBM_SPEC] * (2 * n),
        out_specs=[SEM_SPEC] * (2 * ng) + [HBM_SPEC] * (2 * n) + [pl.BlockSpec(memory_space=pltpu.VMEM)],
        input_output_aliases={i: 2 * ng + i for i in range(2 * n)},
        compiler_params=pltpu.CompilerParams(has_side_effects=DATAFLOW),
    )(*[_hbm(a) for a in flat], *[_hbm(a) for a in lands])
    out, lo = [], 0
    for gi, size in enumerate(sizes):
        out.append(dict(send=res[2 * gi], recv=res[2 * gi + 1],
                        srcs=list(res[2 * ng + lo:2 * ng + lo + size]),
                        lands=list(res[2 * ng + n + lo:2 * ng + n + lo + size])))
        lo += size
    return out, res[-1]


def _gather_wait(started, after, name):
    n = len(started["srcs"])
    after = list(after) if isinstance(after, (list, tuple)) else [after]

    def body(*refs):
        srcs, zones = refs[:n], refs[n:2 * n]
        send, recv = refs[2 * n], refs[2 * n + 1]
        for cp in _gather_copies(srcs, zones, send, recv):
            cp.wait_send()
            cp.wait_recv()

    res = pl.pallas_call(
        body, name=name,
        out_shape=[pltpu.HBM(a.shape, a.dtype) for a in started["srcs"] + started["lands"]],
        in_specs=[HBM_SPEC] * (2 * n) + [SEM_SPEC, SEM_SPEC] + [ANY_SPEC] * len(after),
        out_specs=[HBM_SPEC] * (2 * n),
        input_output_aliases={i: i for i in range(2 * n)},
        compiler_params=pltpu.CompilerParams(has_side_effects=DATAFLOW),
    )(*started["srcs"], *started["lands"], started["send"], started["recv"], *after)
    return list(res[n:])


def _forward_pair(lands, name):
    n = len(lands)

    def body(*refs):
        zones = refs[:n]
        send, recv = refs[2 * n:]
        x, y, c, chips = _mesh_place()

        def copy(w, j, core):
            rows = zones[w].at[4 * chips[j][0] + 2 * chips[j][1] + core]
            return pltpu.make_async_remote_copy(
                src_ref=rows, dst_ref=rows, send_sem=send.at[3 * w + j], recv_sem=recv.at[3 * w + j],
                device_id=(x, y, 1 - c), device_id_type=MESH)

        sends = [copy(w, j, c) for w in range(n) for j in range(3)]
        for cp in sends:
            cp.start()
        for w in range(n):
            for j in range(3):
                copy(w, j, 1 - c).wait_recv()
        for cp in sends:
            cp.wait_send()

    return list(pl.pallas_call(
        body, name=name,
        out_shape=[jax.ShapeDtypeStruct(a.shape, a.dtype) for a in lands],
        in_specs=[HBM_SPEC] * n, out_specs=[HBM_SPEC] * n,
        input_output_aliases={i: i for i in range(n)},
        scratch_shapes=[pltpu.SemaphoreType.DMA((3 * n,)), pltpu.SemaphoreType.DMA((3 * n,))],
    )(*lands))


def _pair_sum(pieces, got, core, name):
    _, rows, cols = pieces.shape
    tr, tc = _tile_2d(rows, cols)

    def body(core_ref, mine_ref, got_ref, out_ref):
        del core_ref
        out_ref[...] = (mine_ref[...].astype(F32) + got_ref[...].astype(F32)).astype(out_ref.dtype)

    return pl.pallas_call(
        body, name=name,
        out_shape=jax.ShapeDtypeStruct((N_CHIP, rows, cols), pieces.dtype),
        grid_spec=pltpu.PrefetchScalarGridSpec(
            num_scalar_prefetch=1, grid=(N_CHIP, rows // tr, cols // tc),
            in_specs=[pl.BlockSpec((None, tr, tc), lambda ch, i, j, core: (2 * ch + core[0], i, j)),
                      pl.BlockSpec((None, tr, tc), lambda ch, i, j, core: (ch, i, j))],
            out_specs=pl.BlockSpec((None, tr, tc), lambda ch, i, j, core: (ch, i, j))),
        compiler_params=_params("parallel", "parallel", "parallel"),
    )(core, pieces, got)


def _tile_2d(rows, cols, budget=2 * 1024 * 1024):
    want = max(16, budget // (4 * cols))
    if rows <= want:
        return rows, cols
    t = (want // 16) * 16
    while t >= 16:
        if rows % t == 0:
            return t, cols
        t -= 16
    return rows, _tile(cols, max(LANES, budget // (4 * rows)))


def _adamw_math(w, g, m, v):
    m = ADAM_B1 * m + (1.0 - ADAM_B1) * g
    v = ADAM_B2 * v + (1.0 - ADAM_B2) * (g * g)
    m_hat = m / (1.0 - ADAM_B1 ** ADAM_STEP)
    v_hat = v / (1.0 - ADAM_B2 ** ADAM_STEP)
    delta = -ADAM_LR * (m_hat / (jnp.sqrt(v_hat) + ADAM_EPS) + ADAM_WD * w)
    return delta, m, v


def _adamw_shard(w, m, v, sums, got, chip, name):
    rows, cols = w.shape
    tr, tc = _tile_2d(rows, cols, budget=1024 * 1024)

    def body(chip_ref, w_ref, m_ref, v_ref, own_ref, got_ref, g_out, d_out, m_out, v_out):
        del chip_ref
        g = own_ref[...].astype(F32)
        for k in range(3):
            g = g + got_ref[k].astype(F32)
        delta, m_new, v_new = _adamw_math(w_ref[...], g, m_ref[...], v_ref[...])
        g_out[...] = g
        d_out[...] = delta
        m_out[...] = m_new
        v_out[...] = v_new

    blk = pl.BlockSpec((tr, tc), lambda i, j, chip: (i, j))
    return pl.pallas_call(
        body, name=name,
        out_shape=[jax.ShapeDtypeStruct((rows, cols), F32)] * 4,
        grid_spec=pltpu.PrefetchScalarGridSpec(
            num_scalar_prefetch=1, grid=(rows // tr, cols // tc),
            in_specs=[blk, blk, blk,
                      pl.BlockSpec((None, tr, tc), lambda i, j, chip: (chip[0], i, j)),
                      pl.BlockSpec((3, tr, tc), lambda i, j, chip: (0, i, j))],
            out_specs=[blk] * 4),
        compiler_params=_params("parallel", "parallel"),
    )(chip, w, m, v, sums, got)


def _adamw_replicated(w, m, v, parts, name):
    rows, cols = w.shape

    def body(w_ref, m_ref, v_ref, parts_ref, g_out, d_out, m_out, v_out):
        g = parts_ref[0]
        for d in range(1, N_DEV):
            g = g + parts_ref[d]
        delta, m_new, v_new = _adamw_math(w_ref[...], g, m_ref[...], v_ref[...])
        g_out[...] = g
        d_out[...] = delta
        m_out[...] = m_new
        v_out[...] = v_new

    return pl.pallas_call(
        body, name=name,
        out_shape=[jax.ShapeDtypeStruct((rows, cols), F32)] * 4,
        compiler_params=_params(),
    )(w, m, v, parts)


NN = ((1,), (0,))
NT = ((1,), (1,))
TN = ((0,), (0,))


ANY_SPEC = pl.BlockSpec(memory_space=pl.ANY)


def _mm(name, a, b, *, dims, grid, a_spec, b_spec, outs, extra=(), extra_specs=(), epilogue=None, acc_shape=None,
        deps=(), b_pieces=0, side=None):
    nk = grid[2]
    n_extra, n_out = len(extra), len(outs)
    side_ins = side["ins"] if side else []
    side_outs = side["outs"] if side else []
    first_side = 2 + n_extra + len(deps)
    first_out = first_side + len(side_ins)
    assert not side or nk == 1

    assert not side or (epilogue is None and n_out == 1 and not b_pieces)
    SIDE_SLICES = 4

    def body(*refs):
        dot = lambda lhs, rhs: lax.dot_general(lhs.astype(BF16), rhs.astype(BF16), (dims, ((), ())),
                                               preferred_element_type=F32)
        if side:
            refs = refs[1:]
            a_ref, b_ref, out_ref = refs[0], refs[1], refs[first_out]
            ins, sides = refs[first_side:first_out], refs[first_out + 1:first_out + 1 + len(side_outs)]
            rows = out_ref.shape[0] // SIDE_SLICES
            srows = sides[0].shape[-2] // SIDE_SLICES
            for part in range(SIDE_SLICES):
                rs = slice(part * srows, (part + 1) * srows)
                for val, o in zip(side["fn"](*[r[..., rs, :] for r in ins]), sides):
                    o[rs, :] = val.astype(o.dtype)
                ms = slice(part * rows, (part + 1) * rows)
                lhs = a_ref[:, ms] if dims == TN else a_ref[ms, :]
                out_ref[ms, :] = dot(lhs, b_ref[...]).astype(out_ref.dtype)
            return
        a_ref, b_ref = refs[0], refs[1]
        ex = refs[2:2 + n_extra]
        out_refs = refs[first_out:first_out + n_out]
        if b_pieces:
            ks = b_ref.shape[-1]
            part = dot(a_ref[:, 0:ks], b_ref[0])
            for piece in range(1, b_pieces):
                part = part + dot(a_ref[:, piece * ks:(piece + 1) * ks], b_ref[piece])
        else:
            part = dot(a_ref[...], b_ref[...])

        def finish(acc):
            res = epilogue(acc, *[e[...] for e in ex]) if epilogue else (acc,)
            for val, o in zip(res, out_refs):
                o[...] = val.astype(o.dtype)

        if nk == 1:
            finish(part)
        else:
            acc_ref = refs[-1]
            k = pl.program_id(2)

            @pl.when(k == 0)
            def _():
                acc_ref[...] = part

            @pl.when(k > 0)
            def _():
                acc_ref[...] += part

            @pl.when(k == nk - 1)
            def _():
                finish(acc_ref[...])

    scratch = [pltpu.VMEM(acc_shape, F32)] if nk > 1 else []
    out_shape = [jax.ShapeDtypeStruct(s, d) for s, d, _ in outs]
    in_specs = [a_spec, b_spec, *extra_specs] + [ANY_SPEC] * len(deps)
    out_specs = [sp for _, _, sp in outs]
    if not side:
        return list(pl.pallas_call(
            body, name=name, out_shape=out_shape, grid=grid, in_specs=in_specs, out_specs=out_specs,
            scratch_shapes=scratch, compiler_params=_params("parallel", "parallel", "arbitrary"),
        )(a, b, *extra, *deps))

    def with_scalars(spec):
        if spec.index_map is None:
            return spec
        return pl.BlockSpec(spec.block_shape, lambda i, j, k, s, f=spec.index_map: f(i, j, k))

    def by_step(block, index_map):
        return pl.BlockSpec(block, lambda i, j, k, s: index_map((i * grid[1] + j) * grid[2] + k, s))

    return list(pl.pallas_call(
        body, name=name,
        out_shape=out_shape + [jax.ShapeDtypeStruct(s, d) for s, d, _, _ in side_outs],
        grid_spec=pltpu.PrefetchScalarGridSpec(
            num_scalar_prefetch=1, grid=grid,
            in_specs=[with_scalars(sp) for sp in in_specs] + [by_step(blk, im) for _, blk, im in side_ins],
            out_specs=[with_scalars(sp) for sp in out_specs] + [by_step(blk, im) for _, _, blk, im in side_outs],
            scratch_shapes=scratch),
        compiler_params=_params("arbitrary", "arbitrary", "arbitrary"),
    )(side["scalars"], a, b, *extra, *deps, *[arr for arr, _, _ in side_ins]))


def _side_pair_sum(pieces, got, place):
    _, rows, cols = pieces.shape

    def build(steps):
        per_chip = steps // N_CHIP
        tr = rows // per_chip
        assert per_chip * N_CHIP == steps and tr * per_chip == rows and tr % 16 == 0
        blk = (None, tr, cols)
        here = lambda st, s: (st // per_chip, st % per_chip, 0)
        return dict(scalars=place,
                    ins=[(pieces, blk, lambda st, s: (2 * (st // per_chip) + s[0], st % per_chip, 0)), (got, blk, here)],
                    outs=[((N_CHIP, rows, cols), pieces.dtype, blk, here)],
                    fn=lambda mine, other: (mine.astype(F32) + other.astype(F32),))
    return build


def _side_adamw(w, m, v, sums, got, place):
    rows, cols = w.shape

    def build(steps):
        tr = rows // steps
        assert tr * steps == rows and tr % 16 == 0
        blk, here = (tr, cols), (lambda st, s: (st, 0))

        def fn(w, m, v, own, got):
            g = own.astype(F32)
            for k in range(3):
                g = g + got[k].astype(F32)
            delta, m_new, v_new = _adamw_math(w, g, m, v)
            return g, delta, m_new, v_new

        return dict(scalars=place,
                    ins=[(w, blk, here), (m, blk, here), (v, blk, here),
                         (sums, (None, tr, cols), lambda st, s: (s[1], st, 0)),
                         (got, (3, tr, cols), lambda st, s: (0, st, 0))],
                    outs=[((rows, cols), F32, blk, here)] * 4, fn=fn)
    return build


def _mm_nn(name, a, b, out_dtype=F32, tm=1024, tn=1024, tk=2048, epilogue=None, n_out=1, extra=(), deps=(), side=None):
    m, kd = a.shape
    n = b.shape[1]
    tm, tn, tk = _tile(m, tm), _tile(n, tn), _tile(kd, tk)
    dts = out_dtype if isinstance(out_dtype, (list, tuple)) else [out_dtype] * n_out
    o_spec = pl.BlockSpec((tm, tn), lambda i, j, k: (i, j))
    grid = (m // tm, n // tn, kd // tk)
    return _mm(name, a, b, dims=NN, grid=grid,
               a_spec=pl.BlockSpec((tm, tk), lambda i, j, k: (i, k)),
               b_spec=pl.BlockSpec((tk, tn), lambda i, j, k: (k, j)),
               outs=[((m, n), dt, o_spec) for dt in dts], extra=extra, extra_specs=[o_spec] * len(extra),
               epilogue=epilogue, acc_shape=(tm, tn), deps=deps, side=side and side(grid[0] * grid[1] * grid[2]))


def _mm_nn_pieces(name, a, b, out_dtype=F32, tm=1024, tk=2048, epilogue=None, n_out=1):
    m, kd = a.shape
    npc, _, ns = b.shape
    tm, tk = _tile(m, tm), _tile(kd, tk)
    tn = _tile(ns, 1024)
    r = ns // tn
    dts = out_dtype if isinstance(out_dtype, (list, tuple)) else [out_dtype] * n_out
    o_spec = pl.BlockSpec((tm, tn), lambda i, j, k: (i, j))
    return _mm(name, a, b, dims=NN, grid=(m // tm, npc * r, kd // tk),
               a_spec=pl.BlockSpec((tm, tk), lambda i, j, k: (i, k)),
               b_spec=pl.BlockSpec((None, tk, tn), lambda i, j, k: (j // r, k, j % r)),
               outs=[((m, npc * ns), dt, o_spec) for dt in dts], epilogue=epilogue, acc_shape=(tm, tn))


def _mm_nt(name, a, b, out_dtype=F32, tm=1024, tn=1024, tk=2048, epilogue=None, extra=(), n_rows=None, deps=(),
           side=None):
    m, kd = a.shape
    n = n_rows or b.shape[0]
    tm, tn, tk = _tile(m, tm), _tile(n, tn), _tile(kd, tk)
    o_spec = pl.BlockSpec((tm, tn), lambda i, j, k: (i, j))
    grid = (m // tm, n // tn, kd // tk)
    res = _mm(name, a, b, dims=NT, grid=grid,
              a_spec=pl.BlockSpec((tm, tk), lambda i, j, k: (i, k)),
              b_spec=pl.BlockSpec((tn, tk), lambda i, j, k: (j, k)),
              outs=[((m, n), out_dtype, o_spec)], extra=extra, extra_specs=[o_spec] * len(extra),
              epilogue=epilogue, acc_shape=(tm, tn), deps=deps, side=side and side(grid[0] * grid[1] * grid[2]))
    return res if side else res[0]


def _mm_nt_pieces(name, a, b, out_dtype=F32, tm=1024, tn=256, deps=()):
    m = a.shape[0]
    npc, n, ks = b.shape
    tm, tn = _tile(m, tm), _tile(n, tn)
    return _mm(name, a, b, dims=NT, grid=(m // tm, n // tn, 1),
               a_spec=pl.BlockSpec((tm, npc * ks), lambda i, j, k: (i, 0)),
               b_spec=pl.BlockSpec((npc, tn, ks), lambda i, j, k: (0, j, 0)),
               outs=[((m, n), out_dtype, pl.BlockSpec((tm, tn), lambda i, j, k: (i, j)))],
               deps=deps, b_pieces=npc)[0]


def _mm_tn(name, a, b, out_dtype=BF16, tm=1024, tn=1024, pieces=0, deps=(), side=None):
    t, m = a.shape
    n = b.shape[1]
    tm = _tile(m, tm)
    if pieces:
        ns = n // pieces
        tn = _tile(ns, tn)
        r = ns // tn
        out = ((pieces, m, ns), out_dtype, pl.BlockSpec((None, tm, tn), lambda i, j, k: (j // r, i, j % r)))
    else:
        tn = _tile(n, tn)
        out = ((m, n), out_dtype, pl.BlockSpec((tm, tn), lambda i, j, k: (i, j)))
    grid = (m // tm, n // tn, 1)
    res = _mm(name, a, b, dims=TN, grid=grid,
              a_spec=pl.BlockSpec((t, tm), lambda i, j, k: (0, i)),
              b_spec=pl.BlockSpec((t, tn), lambda i, j, k: (0, j)),
              outs=[out], deps=deps, side=side and side(grid[0] * grid[1]))
    return res if side else res[0]


def _rms_f(x, g):
    r = lax.rsqrt(jnp.mean(x * x, axis=-1, keepdims=True) + EPS)
    xh = x * r
    return xh * g, xh, r


def _rms_b(dy, g, xh, r):
    dxh = dy * g
    dx = r * (dxh - xh * jnp.mean(dxh * xh, axis=-1, keepdims=True))
    dg = jnp.sum(dy * xh, axis=0, keepdims=True)
    return dx, dg


def _rowwise(name, fn, row_ins, vec_ins, row_outs, vec_outs, deps=()):
    t = row_ins[0].shape[0]
    tr = min(ROW_TILE, t)
    nr, nv, no = len(row_ins), len(vec_ins), len(row_outs)
    first_out = nr + nv + len(deps)

    def body(*refs):
        ri, vi = refs[:nr], refs[nr:nr + nv]
        ro, vo = refs[first_out:first_out + no], refs[first_out + no:]
        routs, vouts = fn(*[r[...] for r in ri], *[v[...] for v in vi])
        for o, val in zip(ro, routs):
            o[...] = val.astype(o.dtype)
        i = pl.program_id(0)
        for o, val in zip(vo, vouts):
            @pl.when(i == 0)
            def _():
                o[...] = val

            @pl.when(i > 0)
            def _():
                o[...] += val

    res = pl.pallas_call(
        body, name=name,
        out_shape=[jax.ShapeDtypeStruct((t, w), d) for w, d in row_outs]
        + [jax.ShapeDtypeStruct((1, w), F32) for w in vec_outs],
        grid=(t // tr,),
        in_specs=[pl.BlockSpec((tr, a.shape[1]), lambda i: (i, 0)) for a in row_ins]
        + [pl.BlockSpec((1, a.shape[1]), lambda i: (0, 0)) for a in vec_ins] + [ANY_SPEC] * len(deps),
        out_specs=[pl.BlockSpec((tr, w), lambda i: (i, 0)) for w, _ in row_outs]
        + [pl.BlockSpec((1, w), lambda i: (0, 0)) for w in vec_outs],
        compiler_params=_params("arbitrary"),
    )(*row_ins, *vec_ins, *deps)
    return list(res)


def _norm_fwd(x, g, name, deps=()):
    d = x.shape[1]
    return _rowwise(name, lambda x, g: ((_rms_f(x, g)[0],), ()), [x], [g], [(d, BF16)], [], deps=deps)[0]


def _norm_bwd(x, dxn, dres, g, name):
    d = x.shape[1]

    def fn(x, dxn, dres, g):
        _, xh, r = _rms_f(x, g)
        dx, dg = _rms_b(dxn, g, xh, r)
        return (dres + dx,), (dg,)

    return _rowwise(name, fn, [x, dxn, dres], [g], [(d, F32)], [d])


def _block_fwd(h_prev, y, g_y, g_n, name, deps=()):
    d = h_prev.shape[1]

    def fn(h_prev, y, g_y, g_n):
        h = h_prev + _rms_f(y, g_y)[0]
        return (h, _rms_f(h, g_n)[0]), ()

    return _rowwise(name, fn, [h_prev, y], [g_y, g_n], [(d, F32), (d, BF16)], [], deps=deps)


def _block_bwd(h, y, d_up, d_n, g_y, g_n, name, deps=()):
    d = h.shape[1]

    def fn(h, y, d_up, d_n, g_y, g_n):
        _, hh, hr = _rms_f(h, g_n)
        dh_n, dg_n = _rms_b(d_n, g_n, hh, hr)
        dh = d_up + dh_n
        _, yh, yr = _rms_f(y, g_y)
        dy, dg_y = _rms_b(dh, g_y, yh, yr)
        return (dh, dy), (dg_y, dg_n)

    return _rowwise(name, fn, [h, y, d_up, d_n], [g_y, g_n], [(d, F32), (d, BF16)], [d, d], deps=deps)


def _loss_head(h2, z, e, target, g, name):
    d = h2.shape[1]

    def fn(h2, z, e, target, g):
        gate = jax.nn.sigmoid(z)
        y, xh, r = _rms_f(e * gate, g)
        diff = h2 + y - target
        loss = 0.5 * jnp.sum(jnp.mean(diff * diff, axis=-1, keepdims=True), axis=0, keepdims=True)
        dh3 = diff / d
        deg, dg = _rms_b(dh3, g, xh, r)
        dz = deg * e * gate * (1.0 - gate)
        de = deg * gate
        return (dh3, dz, de), (dg, jnp.broadcast_to(loss, (1, LANES)))

    return _rowwise(name, fn, [h2, z, e, target], [g], [(d, F32), (d, BF16), (d, BF16)], [d, LANES])


def _shift_down(x, s, rows):
    return jnp.where(rows >= s, pltpu.roll(x, s, axis=0), 0.0)


def _shift_up(x, s, rows):
    t = x.shape[0]
    return jnp.where(rows < t - s, pltpu.roll(x, t - s, axis=0), 0.0)


def _conv_pre(x, w, rows):
    pre = x * w[QK_CONV - 1:QK_CONV, :]
    for s in range(1, QK_CONV):
        pre = pre + _shift_down(x, s, rows) * w[QK_CONV - 1 - s:QK_CONV - s, :]
    return pre


def _conv_fwd(proj_a, w, start, name):
    t = proj_a.shape[0]
    width = w.shape[1]
    tc = _tile(width, 256)
    off = start // tc

    def body(x_ref, w_ref, y_ref):
        x = x_ref[...]
        rows = lax.broadcasted_iota(jnp.int32, x.shape, 0)
        pre = _conv_pre(x, w_ref[...], rows)
        y_ref[...] = pre * jax.nn.sigmoid(pre)

    return pl.pallas_call(
        body, name=name,
        out_shape=jax.ShapeDtypeStruct((t, width), F32),
        grid=(width // tc,),
        in_specs=[pl.BlockSpec((t, tc), lambda j: (0, off + j)), pl.BlockSpec((QK_CONV, tc), lambda j: (0, j))],
        out_specs=pl.BlockSpec((t, tc), lambda j: (0, j)),
        compiler_params=_params("parallel"),
    )(proj_a, w)


def _conv_bwd(proj_a, w, dy, start, name):
    t = proj_a.shape[0]
    width = w.shape[1]
    tc = _tile(width, 256)
    off = start // tc

    def body(x_ref, w_ref, dy_ref, dx_ref, dw_ref):
        x, w = x_ref[...], w_ref[...]
        rows = lax.broadcasted_iota(jnp.int32, x.shape, 0)
        pre = _conv_pre(x, w, rows)
        sig = jax.nn.sigmoid(pre)
        dpre = dy_ref[...] * (sig * (1.0 + pre * (1.0 - sig)))
        dx = dpre * w[QK_CONV - 1:QK_CONV, :]
        dws = [jnp.sum(dpre * x, axis=0, keepdims=True)]
        for s in range(1, QK_CONV):
            dx = dx + _shift_up(dpre, s, rows) * w[QK_CONV - 1 - s:QK_CONV - s, :]
            dws.append(jnp.sum(dpre * _shift_down(x, s, rows), axis=0, keepdims=True))
        dx_ref[...] = dx.astype(dx_ref.dtype)
        for s in range(QK_CONV):
            dw_ref[QK_CONV - 1 - s:QK_CONV - s, :] = dws[s]

    return pl.pallas_call(
        body, name=name,
        out_shape=[jax.ShapeDtypeStruct((t, width), BF16), jax.ShapeDtypeStruct((QK_CONV, width), F32)],
        grid=(width // tc,),
        in_specs=[pl.BlockSpec((t, tc), lambda j: (0, off + j)), pl.BlockSpec((QK_CONV, tc), lambda j: (0, j)),
                  pl.BlockSpec((t, tc), lambda j: (0, j))],
        out_specs=[pl.BlockSpec((t, tc), lambda j: (0, j)), pl.BlockSpec((QK_CONV, tc), lambda j: (0, j))],
        compiler_params=_params("parallel"),
    )(proj_a, w, dy)


def _dot(a, b, dims):
    return lax.dot_general(a.astype(BF16), b.astype(BF16), (dims, ((), ())), preferred_element_type=F32)


def _dot_split(ones, x):
    hi = x.astype(BF16)
    rest = x - hi.astype(F32)
    mid = rest.astype(BF16)
    lo = (rest - mid.astype(F32)).astype(BF16)
    dot = lambda part: lax.dot_general(ones, part, (NN, ((), ())), preferred_element_type=F32)
    return dot(hi) + dot(mid) + dot(lo)


def _chunk_iotas():
    rows = lax.broadcasted_iota(jnp.int32, (CHUNK, CHUNK), 0)
    cols = lax.broadcasted_iota(jnp.int32, (CHUNK, CHUNK), 1)
    return rows, cols


def _gates_fwd(proj_b, bias, col_block, heads, name):
    t = proj_b.shape[0]

    def body(x_ref, b_ref, gb_ref, gbt_ref):
        rows, cols = _chunk_iotas()
        tri = (cols <= rows).astype(BF16)
        for ci in range(t // CHUNK):
            sl = slice(ci * CHUNK, (ci + 1) * CHUNK)
            pre = x_ref[sl, :] + b_ref[...]
            log_f = jnp.minimum(pre, 0.0) - jnp.log(1.0 + jnp.exp(-jnp.abs(pre)))
            log_f = jnp.where((cols >= heads) & (cols < 2 * heads), log_f, 0.0)
            tile = jnp.where(cols < heads, pre, _dot_split(tri, log_f))
            gb_ref[sl, :] = tile
            gbt_ref[:, sl] = tile.T

    return pl.pallas_call(
        body, name=name,
        out_shape=[jax.ShapeDtypeStruct((t, LANES), F32), jax.ShapeDtypeStruct((LANES, t), F32)],
        grid=(1,),
        in_specs=[pl.BlockSpec((t, LANES), lambda i: (0, col_block)), pl.BlockSpec((1, LANES), lambda i: (0, 0))],
        out_specs=[pl.BlockSpec((t, LANES), lambda i: (0, 0)), pl.BlockSpec((LANES, t), lambda i: (0, 0))],
        compiler_params=_params("arbitrary"),
    )(proj_b, bias)


def _gate_columns(gb, heads, name):
    t = gb.shape[0]

    def body(gb_ref, out_ref):
        g = gb_ref[...]
        lane = lax.broadcasted_iota(jnp.int32, g.shape, 1)
        column = jnp.sum(jnp.where(lane == pl.program_id(0), g, 0.0), axis=-1, keepdims=True)
        out_ref[...] = jnp.broadcast_to(column, out_ref.shape)

    return pl.pallas_call(
        body, name=name,
        out_shape=jax.ShapeDtypeStruct((2 * heads, t, LANES), F32),
        grid=(2 * heads,),
        in_specs=[pl.BlockSpec((t, LANES), lambda h: (0, 0))],
        out_specs=pl.BlockSpec((None, t, LANES), lambda h: (h, 0, 0)),
        compiler_params=_params("parallel"),
    )(gb)


def _gates_bwd(proj_b, bias, dcol, drow, col_block, heads, name):
    t = proj_b.shape[0]

    def body(x_ref, b_ref, dc_ref, dr_ref, dx_ref, db_ref):
        rows, cols = _chunk_iotas()
        later = (cols >= rows).astype(BF16)
        total = jnp.zeros((1, LANES), F32)
        for ci in range(t // CHUNK):
            sl = slice(ci * CHUNK, (ci + 1) * CHUNK)
            by_row = dr_ref[:, sl].T
            d = dc_ref[sl, :] + jnp.where(cols < heads, by_row,
                                          jnp.where(cols < 2 * heads, -pltpu.roll(by_row, heads, axis=1), 0.0))
            d_log_f = _dot_split(later, jnp.where(cols >= heads, d, 0.0))
            pre = x_ref[sl, :] + b_ref[...]
            dx = jnp.where(cols < heads, d, jnp.where(cols < 2 * heads, d_log_f * jax.nn.sigmoid(-pre), 0.0))
            dx_ref[sl, :] = dx.astype(dx_ref.dtype)
            total = total + jnp.sum(dx, axis=0, keepdims=True)
        db_ref[...] = total

    return pl.pallas_call(
        body, name=name,
        out_shape=[jax.ShapeDtypeStruct((t, LANES), BF16), jax.ShapeDtypeStruct((1, LANES), F32)],
        grid=(1,),
        in_specs=[pl.BlockSpec((t, LANES), lambda i: (0, col_block)), pl.BlockSpec((1, LANES), lambda i: (0, 0)),
                  pl.BlockSpec((t, LANES), lambda i: (0, 0)), pl.BlockSpec((LANES, t), lambda i: (0, 0))],
        out_specs=[pl.BlockSpec((t, LANES), lambda i: (0, 0)), pl.BlockSpec((1, LANES), lambda i: (0, 0))],
        compiler_params=_params("arbitrary"),
    )(proj_b, bias, dcol, drow)


def _gate_views(li_ref, b_ref, gbt_ref, s, head, heads, sl):
    li_c, b_c = li_ref[s, sl, :], b_ref[s, sl, :]
    li_r = jnp.broadcast_to(gbt_ref[head, :, sl], (CHUNK, CHUNK))
    b_r = jnp.broadcast_to(gbt_ref[heads + head, :, sl], (CHUNK, CHUNK))
    return li_c, b_c, li_r, b_r


def _chunk_state(kt, k, v, li_c, b_c, ct, nrow, m_prev):
    b_tot = b_c[CHUNK - 1:CHUNK, 0:1]
    a = b_tot - b_c[:, 0:1] + li_c[:, 0:1]
    m_new = jnp.maximum(b_tot + m_prev, jnp.max(a, axis=0, keepdims=True))
    dec = jnp.exp(b_tot + m_prev - m_new)
    w = jnp.exp(a - m_new)
    wv = w * v
    ct_new = dec * ct + _dot(kt, wv, NN)
    n_new = dec * nrow + jnp.sum(w * k, axis=0, keepdims=True)
    return ct_new, n_new, m_new, dec, w, wv


def _chunk_scores(q, k, b_c, li_r, b_r, m_prev):
    rows, cols = _chunk_iotas()
    dlog = jnp.where(cols <= rows, b_c - b_r + li_r, -jnp.inf)
    inter = b_c[:, 0:1] + m_prev
    mt = jnp.maximum(inter, jnp.max(dlog, axis=-1, keepdims=True))
    dw = jnp.exp(dlog - mt)
    iw = jnp.exp(inter - mt)
    a = _dot(q, k, NT)
    return dw, iw, mt, a, a * dw


def _to_row(column):
    rows, cols = _chunk_iotas()
    return jnp.sum(jnp.where(rows == cols, jnp.broadcast_to(column, (CHUNK, CHUNK)), 0.0), axis=0, keepdims=True)


def _head_out(h, o, g):
    cell, hh, r = _rms_f(h, g)
    sig = jax.nn.sigmoid(o)
    return sig * cell, (cell, hh, r, sig)


HEADS_PER_STEP = 2


def _mlstm_fwd(qc, kc, kct, proj_a, gcols, gbt, g_mlstm, heads, d_model, name, dep):
    t = qc.shape[0]
    dh = g_mlstm.shape[1] // heads
    nc = t // CHUNK
    scale = dh ** -0.5
    hp = HEADS_PER_STEP

    def body(q_ref, k_ref, kt_ref, v_ref, o_ref, li_ref, b_ref, gbt_ref, g_ref, dep_ref, out_ref, ct_ref, n_ref, m_ref):
        del dep_ref
        step = pl.program_id(0)
        ct_ref[...] = jnp.zeros_like(ct_ref)
        n_ref[...] = jnp.zeros_like(n_ref)
        m_ref[...] = jnp.zeros_like(m_ref)

        def chunk(ci, carry):
            sl = pl.ds(pl.multiple_of(ci * CHUNK, CHUNK), CHUNK)
            for s in range(hp):
                cs = slice(s * dh, (s + 1) * dh)
                q, k, v, kt = q_ref[sl, cs], k_ref[sl, cs] * scale, v_ref[sl, cs], kt_ref[cs, sl]
                li_c, b_c, li_r, b_r = _gate_views(li_ref, b_ref, gbt_ref, s, step * hp + s, heads, sl)
                ct, nrow, m_prev = ct_ref[s], n_ref[s, 0:1, :], m_ref[s, 0:1, 0:1]
                dw, iw, mt, _, sm = _chunk_scores(q, k, b_c, li_r, b_r, m_prev)
                iq = iw * q
                num = _dot(jnp.concatenate([iq, sm], axis=1), jnp.concatenate([ct, v], axis=0), NN)
                den = jnp.sum(iq * nrow + sm, axis=-1, keepdims=True)
                h = num / jnp.maximum(jnp.abs(den), jnp.exp(-mt))
                y, _ = _head_out(h, o_ref[sl, cs], g_ref[:, cs])
                out_ref[sl, cs] = y.astype(out_ref.dtype)
                ct_new, n_new, m_new, _, _, _ = _chunk_state(kt, k, v, li_c, b_c, ct, nrow, m_prev)
                ct_ref[s] = ct_new
                n_ref[s, 0:1, :] = n_new
                m_ref[s, 0:1, 0:1] = m_new
            return carry

        lax.fori_loop(0, nc, chunk, 0, unroll=2)

    col = lambda off: pl.BlockSpec((t, hp * dh), lambda h: (0, off // hp + h))
    slab = lambda off: pl.BlockSpec((hp, t, LANES), lambda h: (off // hp + h, 0, 0))
    return pl.pallas_call(
        body, name=name,
        out_shape=jax.ShapeDtypeStruct((t, d_model), BF16),
        grid=(heads // hp,),
        in_specs=[col(0), col(0), pl.BlockSpec((hp * dh, t), lambda h: (h, 0)), col(2 * heads), col(3 * heads),
                  slab(0), slab(heads), pl.BlockSpec((2 * heads, 1, t), lambda h: (0, 0, 0)),
                  pl.BlockSpec((1, hp * dh), lambda h: (0, h)), ANY_SPEC],
        out_specs=pl.BlockSpec((t, hp * dh), lambda h: (0, h)),
        scratch_shapes=[pltpu.VMEM((hp, dh, dh), F32), pltpu.VMEM((hp, 8, dh), F32), pltpu.VMEM((hp, 8, LANES), F32)],
        compiler_params=_params("arbitrary"),
    )(qc, kc, kct, proj_a, proj_a, gcols, gcols, gbt, g_mlstm, dep)


def _mlstm_bwd(qc, kc, qct, kct, proj_a, gcols, gbt, g_mlstm, dhcat, heads, name, dep):
    t = qc.shape[0]
    wm = g_mlstm.shape[1]
    dh = wm // heads
    nc = t // CHUNK
    scale = dh ** -0.5

    def body(q_ref, k_ref, qt_ref, kt_ref, v_ref, o_ref, li_ref, b_ref, gbt_ref, g_ref, dy_ref, dep_ref,
             dq_ref, dk_ref, dv_ref, do_ref, dcol_ref, drow_ref, dg_ref,
             ct_all, n_all, m_all, dct_ref, dn_ref):
        del dep_ref
        head = pl.program_id(0)
        g = g_ref[...]

        @pl.when(head == 0)
        def _():
            dcol_ref[...] = jnp.zeros_like(dcol_ref)
            drow_ref[...] = jnp.zeros_like(drow_ref)

        def load(ci):
            sl = pl.ds(pl.multiple_of(ci * CHUNK, CHUNK), CHUNK)
            return (sl, q_ref[sl, :], k_ref[sl, :] * scale, v_ref[sl, :], kt_ref[:, sl]) \
                + _gate_views(li_ref, b_ref, gbt_ref, 0, head, heads, sl)

        def fwd(ci, carry):
            ct, nrow, m_prev = carry
            ct_all[ci] = ct
            n_all[ci, 0:1, :] = nrow
            m_all[ci, 0:1, 0:1] = m_prev
            _, _, k, v, kt, li_c, b_c, _, _ = load(ci)
            return _chunk_state(kt, k, v, li_c, b_c, ct, nrow, m_prev)[:3]

        lax.fori_loop(0, nc, fwd, (jnp.zeros((dh, dh), F32), jnp.zeros((1, dh), F32), jnp.zeros((1, 1), F32)))

        dct_ref[...] = jnp.zeros_like(dct_ref)
        dn_ref[...] = jnp.zeros_like(dn_ref)
        dg_ref[...] = jnp.zeros_like(dg_ref)
        rows, cols = _chunk_iotas()
        last = lax.broadcasted_iota(jnp.int32, (CHUNK, 1), 0) == CHUNK - 1
        lane = lax.broadcasted_iota(jnp.int32, (CHUNK, LANES), 1)

        def bwd(step, carry):
            ci = nc - 1 - step
            sl, q, k, v, kt, li_c, b_c, li_r, b_r = load(ci)
            qt = qt_ref[:, sl]
            ct, nrow, m_prev = ct_all[ci], n_all[ci, 0:1, :], m_all[ci, 0:1, 0:1]
            dw, iw, mt, a, sm = _chunk_scores(q, k, b_c, li_r, b_r, m_prev)
            qc_ = _dot(q, ct, NN)
            qn = jnp.sum(q * nrow, axis=-1, keepdims=True)
            num = iw * qc_ + _dot(sm, v, NN)
            den = iw * qn + jnp.sum(sm, axis=-1, keepdims=True)
            floor = jnp.exp(-mt)
            dn = jnp.maximum(jnp.abs(den), floor)
            h = num / dn
            _, (cell, hh, r, sig) = _head_out(h, o_ref[sl, :], g)
            dy = dy_ref[sl, :]
            dcell = dy * sig
            do_ref[sl, :] = (dy * cell * sig * (1.0 - sig)).astype(do_ref.dtype)
            dh_, dg = _rms_b(dcell, g, hh, r)
            dg_ref[...] += dg
            dnum = dh_ / dn
            ddn = -jnp.sum(dh_ * h, axis=-1, keepdims=True) / dn
            dden = jnp.where(jnp.abs(den) >= floor, ddn * jnp.sign(den), 0.0)
            idn = iw * dnum
            idd = iw * dden
            dw_t = jnp.exp(jnp.where(cols >= rows, b_r - b_c + li_c, -jnp.inf) - _to_row(mt))
            sm_t = _dot(k, q, NT) * dw_t
            dsm = _dot(dnum, v, NT) + dden
            da_ = dsm * dw
            da_t = (_dot(v, dnum, NT) + _to_row(dden)) * dw_t
            e = da_ * a
            dq = _dot(idn, ct, NT) + _dot(da_, k, NN) + idd * nrow
            dk = _dot(da_t, q, NN)
            dv = _dot(sm_t, dnum, NN)
            dct = _dot(qt, idn, NN)
            dnr = jnp.sum(q * idd, axis=0, keepdims=True)
            db = jnp.sum(e + idn * qc_, axis=-1, keepdims=True) + idd * qn
            _, _, _, dec, w, wv = _chunk_state(kt, k, v, li_c, b_c, ct, nrow, m_prev)
            dct_new, dn_new = dct_ref[...], dn_ref[0:1, :]
            ddec = jnp.sum(jnp.sum(dct_new * ct, axis=-1, keepdims=True), axis=0, keepdims=True) \
                + jnp.sum(dn_new * nrow, axis=-1, keepdims=True)
            dk = dk + _dot(wv, dct_new, NT) + w * dn_new
            dwv = _dot(k, dct_new, NN)
            dv = dv + w * dwv
            dwt = jnp.sum(dwv * v + k * dn_new, axis=-1, keepdims=True)
            da = dwt * w
            dbtot = jnp.sum(da, axis=0, keepdims=True) + ddec * dec
            db = db - da + jnp.where(last, dbtot, 0.0)
            dct_ref[...] = dec * dct_new + dct
            dn_ref[0:1, :] = dec * dn_new + dnr
            dq_ref[sl, :] = dq
            dk_ref[sl, :] = dk * scale
            dv_ref[sl, :] = dv.astype(dv_ref.dtype)
            dcol_ref[sl, :] += jnp.where(lane == head, da, 0.0) + jnp.where(lane == heads + head, db, 0.0)
            drow_ref[head, :, sl] = jnp.sum(e, axis=0, keepdims=True)
            return carry

        lax.fori_loop(0, nc, bwd, 0, unroll=4)

    col = lambda off: pl.BlockSpec((t, dh), lambda h: (0, off + h))
    rowb = pl.BlockSpec((dh, t), lambda h: (h, 0))
    slab = lambda off: pl.BlockSpec((1, t, LANES), lambda h: (off + h, 0, 0))
    vec = pl.BlockSpec((1, dh), lambda h: (0, h))
    return pl.pallas_call(
        body, name=name,
        out_shape=[jax.ShapeDtypeStruct((t, wm), F32)] * 2 + [jax.ShapeDtypeStruct((t, wm), BF16)] * 2
        + [jax.ShapeDtypeStruct((t, LANES), F32), jax.ShapeDtypeStruct((heads, 1, t), F32),
           jax.ShapeDtypeStruct((1, wm), F32)],
        grid=(heads,),
        in_specs=[col(0), col(0), rowb, rowb, col(2 * heads), col(3 * heads), slab(0), slab(heads),
                  pl.BlockSpec((2 * heads, 1, t), lambda h: (0, 0, 0)), vec, col(0), ANY_SPEC],
        out_specs=[col(0), col(0), col(0), col(0), pl.BlockSpec((t, LANES), lambda h: (0, 0)),
                   pl.BlockSpec((heads, 1, t), lambda h: (0, 0, 0)), vec],
        scratch_shapes=[pltpu.VMEM((nc, dh, dh), F32), pltpu.VMEM((nc, 8, dh), F32), pltpu.VMEM((nc, 8, LANES), F32),
                        pltpu.VMEM((dh, dh), F32), pltpu.VMEM((8, dh), F32)],
        compiler_params=_params("arbitrary"),
    )(qc, kc, qct, kct, proj_a, proj_a, gcols, gcols, gbt, g_mlstm, dhcat, dep)


def _pool_window(gi):
    return jnp.where(gi == 0, 2.0, jnp.where(gi == 1, 4.0, jnp.where(gi == 2, 8.0, 16.0))).astype(F32)


def _pool_select(gi, levels):
    return jnp.where(gi == 0, levels[0], jnp.where(gi == 1, levels[1], jnp.where(gi == 2, levels[2], levels[3])))


def _pooled(u, gi, rows):
    s, levels = u, []
    for lvl in range(POOL_GROUPS):
        s = s + _shift_down(s, 1 << lvl, rows)
        levels.append(s)
    cnt = jnp.minimum((rows + 1).astype(F32), _pool_window(gi))
    return _pool_select(gi, levels) / cnt - u, cnt


def _pool_fwd(proj_b, w_pool, scale, hcat, name):
    t, d_model = hcat.shape
    g, cg, _ = w_pool.shape
    off = (d_model - g * cg) // cg

    def body(u_ref, w_ref, s_ref, hcat_ref, out_ref):
        del hcat_ref
        gi = pl.program_id(0)
        u = u_ref[...]
        rows = lax.broadcasted_iota(jnp.int32, u.shape, 0)
        pooled, _ = _pooled(u, gi, rows)
        out_ref[...] = (_dot(pooled, w_ref[...], NN) * s_ref[...]).astype(out_ref.dtype)

    return pl.pallas_call(
        body, name=name,
        out_shape=jax.ShapeDtypeStruct((t, d_model), BF16),
        grid=(g,),
        in_specs=[pl.BlockSpec((t, cg), lambda gi: (0, gi)), pl.BlockSpec((None, cg, cg), lambda gi: (gi, 0, 0)),
                  pl.BlockSpec((1, cg), lambda gi: (0, gi)), HBM_SPEC],
        out_specs=pl.BlockSpec((t, cg), lambda gi: (0, off + gi)),
        input_output_aliases={3: 0},
        compiler_params=_params("arbitrary"),
    )(proj_b, w_pool, scale, hcat)


def _pool_bwd(proj_b, w_pool, scale, dhcat, name):
    t, d_model = dhcat.shape
    g, cg, _ = w_pool.shape
    off = (d_model - g * cg) // cg

    def body(u_ref, w_ref, s_ref, dy_ref, du_ref, dw_ref, ds_ref):
        gi = pl.program_id(0)
        u, w, dy = u_ref[...], w_ref[...], dy_ref[...]
        rows = lax.broadcasted_iota(jnp.int32, u.shape, 0)
        pooled, cnt = _pooled(u, gi, rows)
        mixed = _dot(pooled, w, NN)
        ds_ref[...] = jnp.sum(dy * mixed, axis=0, keepdims=True)
        dmixed = dy * s_ref[...]
        dw_ref[...] = _dot(pooled, dmixed, TN).astype(dw_ref.dtype)
        dpooled = _dot(dmixed, w, NT)
        s, levels = dpooled / cnt, []
        for lvl in range(POOL_GROUPS):
            s = s + _shift_up(s, 1 << lvl, rows)
            levels.append(s)
        du_ref[...] = (_pool_select(gi, levels) - dpooled).astype(du_ref.dtype)

    return pl.pallas_call(
        body, name=name,
        out_shape=[jax.ShapeDtypeStruct((t, g * cg), BF16), jax.ShapeDtypeStruct((g, cg, cg), BF16),
                   jax.ShapeDtypeStruct((1, g * cg), F32)],
        grid=(g,),
        in_specs=[pl.BlockSpec((t, cg), lambda gi: (0, gi)), pl.BlockSpec((None, cg, cg), lambda gi: (gi, 0, 0)),
                  pl.BlockSpec((1, cg), lambda gi: (0, gi)), pl.BlockSpec((t, cg), lambda gi: (0, off + gi))],
        out_specs=[pl.BlockSpec((t, cg), lambda gi: (0, gi)), pl.BlockSpec((None, cg, cg), lambda gi: (gi, 0, 0)),
                   pl.BlockSpec((1, cg), lambda gi: (0, gi))],
        compiler_params=_params("arbitrary"),
    )(proj_b, w_pool, scale, dhcat)


def _pad_cols(a, width):
    return jnp.pad(a, ((0, 0), (0, width - a.shape[1])))


def kernel(x, p, w_in, b_gates, w_qk_conv, g_mlstm, w_pool, pool_scale, w_out, g_mix_pre, g_mix_post, w_ff1, w_ff2, g_ff_pre, g_ff_post, w_ple_proj, w_ple_gate, g_ple_gate, g_ple_post, loss_target, m_w_in, m_b_gates, m_w_qk_conv, m_g_mlstm, m_w_pool, m_pool_scale, m_w_out, m_g_mix_pre, m_g_mix_post, m_w_ff1, m_w_ff2, m_g_ff_pre, m_g_ff_post, m_w_ple_proj, m_w_ple_gate, m_g_ple_gate, m_g_ple_post, v_w_in, v_b_gates, v_w_qk_conv, v_g_mlstm, v_w_pool, v_pool_scale, v_w_out, v_g_mix_pre, v_g_mix_post, v_w_ff1, v_w_ff2, v_g_ff_pre, v_g_ff_post, v_w_ple_proj, v_w_ple_gate, v_g_ple_gate, v_g_ple_post):
    weights = dict(w_in=w_in, b_gates=b_gates, w_qk_conv=w_qk_conv, g_mlstm=g_mlstm, w_pool=w_pool,
                   pool_scale=pool_scale, w_out=w_out, g_mix_pre=g_mix_pre, g_mix_post=g_mix_post, w_ff1=w_ff1,
                   w_ff2=w_ff2, g_ff_pre=g_ff_pre, g_ff_post=g_ff_post, w_ple_proj=w_ple_proj,
                   w_ple_gate=w_ple_gate, g_ple_gate=g_ple_gate, g_ple_post=g_ple_post)
    mom1 = dict(w_in=m_w_in, b_gates=m_b_gates, w_qk_conv=m_w_qk_conv, g_mlstm=m_g_mlstm, w_pool=m_w_pool,
                pool_scale=m_pool_scale, w_out=m_w_out, g_mix_pre=m_g_mix_pre, g_mix_post=m_g_mix_post,
                w_ff1=m_w_ff1, w_ff2=m_w_ff2, g_ff_pre=m_g_ff_pre, g_ff_post=m_g_ff_post,
                w_ple_proj=m_w_ple_proj, w_ple_gate=m_w_ple_gate, g_ple_gate=m_g_ple_gate, g_ple_post=m_g_ple_post)
    mom2 = dict(w_in=v_w_in, b_gates=v_b_gates, w_qk_conv=v_w_qk_conv, g_mlstm=v_g_mlstm, w_pool=v_w_pool,
                pool_scale=v_pool_scale, w_out=v_w_out, g_mix_pre=v_g_mix_pre, g_mix_post=v_g_mix_post,
                w_ff1=v_w_ff1, w_ff2=v_w_ff2, g_ff_pre=v_g_ff_pre, g_ff_post=v_g_ff_post,
                w_ple_proj=v_w_ple_proj, w_ple_gate=v_w_ple_gate, g_ple_gate=v_g_ple_gate, g_ple_post=v_g_ple_post)
    order = list(weights)

    t, d = x.shape[1], x.shape[2]
    heads = MLSTM_HEADS
    wm = g_mlstm.shape[1]
    pw = pool_scale.shape[1]
    cg = pw // POOL_GROUPS
    in_cols = 4 * wm + 2 * heads + pw
    in_shard = w_in.shape[2]
    xs, ps, tgt = x[0], p[0, 0], loss_target[0]

    mx, my, mc = lax.axis_index("x"), lax.axis_index("y"), lax.axis_index("c")
    core = jnp.reshape(mc, (1,)).astype(jnp.int32)
    chip = jnp.reshape(2 * mx + my, (1,)).astype(jnp.int32)

    first = [w_in[0].T.astype(BF16), w_qk_conv[0], w_pool[0].reshape(POOL_GROUPS * w_pool.shape[2], cg).astype(BF16)]
    started, token = _gather_start([first], "gather_start_in")
    anchor = token[0, 0]
    later = lambda w: (w[0] + anchor).astype(BF16)
    rest, token = _gather_start([[later(w_out)], [later(w_ff1)], [later(w_ff2)], [later(w_ple_gate), later(w_ple_proj)]],
                                "gather_start_rest")
    started = started + rest
    bias = _pad_cols(b_gates, LANES)
    small = ["b_gates", "g_mlstm", "pool_scale", "g_mix_pre", "g_mix_post", "g_ff_pre", "g_ff_post", "g_ple_gate",
             "g_ple_post"]
    pack = lambda src: _pad_rows16(jnp.concatenate([_pad_cols(src[k] + anchor, d) for k in small], axis=0))
    small_state = [pack(weights), pack(mom1), pack(mom2)]
    in_state = [a[0].T + anchor for a in (w_in, m_w_in, v_w_in)]

    xn = _norm_fwd(xs, g_mix_pre, "norm_mix_pre", deps=[token])
    g_in, g_conv, g_pool = _forward_pair(
        _gather_wait(started[0], [xn] + small_state + in_state, "gather_wait_in"), "gather_pair_in")
    in_t_a = g_in.reshape(in_cols, d)
    in_t_b = jnp.concatenate([in_t_a[4 * wm + 2 * heads:], in_t_a[4 * wm:4 * wm + 2 * heads],
                              jnp.zeros((LANES - 2 * heads, d), BF16)], axis=0)
    conv_w = g_conv.transpose(1, 0, 2).reshape(QK_CONV, 2 * wm)
    pool_w = g_pool.reshape(N_DEV, POOL_GROUPS, cg // N_DEV, cg).transpose(1, 0, 2, 3).reshape(POOL_GROUPS, cg, cg)
    proj_a = _mm_nt("proj_qkvo", xn, in_t_a, n_rows=4 * wm)
    proj_b = _mm_nt("proj_pool_gates", xn, in_t_b)
    qc = _conv_fwd(proj_a, conv_w[:, :wm], 0, "q_conv")
    kc = _conv_fwd(proj_a, conv_w[:, wm:], wm, "k_conv")
    gb, gbt = _gates_fwd(proj_b, bias, pw // LANES, heads, "gates")
    gcols = _gate_columns(gb, heads, "gate_columns")
    gbt = gbt[:2 * heads].reshape(2 * heads, 1, t)
    qct, kct = qc.T.astype(BF16), (kc * (wm // heads) ** -0.5).T.astype(BF16)
    fwd_out = _forward_start(_gather_wait(started[1], gb, "gather_wait_out"), "gather_pair_start_out")
    hcat = _mlstm_fwd(qc, kc, kct, proj_a, gcols, gbt, g_mlstm, heads, d, "mlstm", fwd_out["token"])
    hcat = _pool_fwd(proj_b, pool_w, pool_scale, hcat, "pool")
    (g_out,) = _forward_wait(fwd_out, hcat, "gather_pair_wait_out")
    out_w = g_out.reshape(d, d)
    mix = _mm_nn("mix_out", hcat, out_w)[0]
    fwd_ff1 = _forward_start(_gather_wait(started[2], mix, "gather_wait_ff1"), "gather_pair_start_ff1")
    h1, hn = _block_fwd(xs, mix, g_mix_post, g_ff_pre, "residual_mix", deps=[fwd_ff1["token"]])
    (g_ff1,) = _forward_wait(fwd_ff1, hn, "gather_pair_wait_ff1")
    relu_a, act = _mm_nn_pieces("ff1", hn, g_ff1, out_dtype=[BF16, BF16],
                                epilogue=lambda acc: (jnp.maximum(acc, 0.0), jnp.square(jnp.maximum(acc, 0.0))))
    (g_ff2,) = _forward_pair(_gather_wait(started[3], act, "gather_wait_ff2"), "gather_pair_ff2")
    ff2_w = g_ff2.reshape(-1, d)
    ff = _mm_nn("ff2", act, ff2_w, tn=256, tk=4 * d)[0]
    fwd_ple = _forward_start(_gather_wait(started[4], ff, "gather_wait_ple"), "gather_pair_start_ple")
    h2, hg = _block_fwd(h1, ff, g_ff_post, g_ple_gate, "residual_ff", deps=[fwd_ple["token"]])
    g_pgate, g_pproj = _forward_wait(fwd_ple, hg, "gather_pair_wait_ple")
    pgate_w = g_pgate.reshape(d, d)
    z = _mm_nn("ple_gate", hg, pgate_w)[0]
    e = _mm_nn_pieces("ple_proj", ps, g_pproj)[0]
    dh3, dz, de, dg_ple_post, loss_part = _loss_head(h2, z, e, tgt, g_ple_post, "loss_head")
    loss = lax.psum(loss_part[0, 0], MESH_AXES)

    out = {}

    def state(k):
        if k == "w_in":
            return in_state
        return [a.reshape(-1, a.shape[-1]) for a in (weights[k], mom1[k], mom2[k])]

    def pair_begin(pieces, tag):
        keys = list(pieces)
        return keys, _pair_start([pieces[k] for k in keys], "reduce_pair_start_" + tag)

    def chip_begin(group, after, tag):
        keys, begun = group
        mine, got = _pair_wait(begun, after, "reduce_pair_wait_" + tag)
        sums = [_pair_sum(p_, g, core, "pair_sum_" + k) for k, p_, g in zip(keys, mine, got)]
        return keys, _chip_start(sums, "reduce_start_" + tag)

    def reduce_start(pieces, tag):
        keys = list(pieces)
        got = _pair_exchange([pieces[k] for k in keys], "reduce_pair_" + tag)
        sums = [_pair_sum(pieces[k], g, core, "pair_sum_" + k) for k, g in zip(keys, got)]
        return keys, _chip_start(sums, "reduce_start_" + tag)

    def reduce_finish(group, after, tag):
        keys, begun = group
        sums, got = _chip_wait(begun, after, "reduce_wait_" + tag)
        for k, s, g in zip(keys, sums, got):
            res = _adamw_shard(*state(k), s, g, chip, "adamw_" + k)
            out[k] = [r.T[None] if k == "w_in" else r.reshape(weights[k].shape) for r in res]

    token_of = lambda group: group[1]["token"]
    dhg = _mm_nt("d_ple_gate_in", dz, pgate_w)
    gw_pgate = _mm_tn("gw_ple_gate", hg, dz)
    gw_pproj = _mm_tn("gw_ple_proj", ps, de, pieces=N_DEV)
    pair_ple = pair_begin({"w_ple_gate": gw_pgate.reshape(N_DEV, d // N_DEV, d), "w_ple_proj": gw_pproj}, "ple")
    dh2, dff, dg_ff_post, dg_ple_gate = _block_bwd(h2, ff, dh3, dhg, g_ff_post, g_ple_gate, "d_residual_ff",
                                                   deps=[token_of(pair_ple)])
    red_ple = chip_begin(pair_ple, dff, "ple")
    da = _mm_nt("d_ff2_in", dff, ff2_w, out_dtype=BF16, extra=[relu_a],
                epilogue=lambda acc, r: (acc * (2.0 * r.astype(F32)),), deps=[token_of(red_ple)])
    gw_ff2 = _mm_tn("gw_ff2", act, dff)
    pair_ff2 = pair_begin({"w_ff2": gw_ff2.reshape(N_DEV, -1, d)}, "ff2")
    dhn = _mm_nt_pieces("d_ff1_in", da, g_ff1, deps=[token_of(pair_ff2)])
    place = jnp.concatenate([core, chip])
    mine_ff2, got_ff2 = _pair_wait(pair_ff2[1], dhn, "reduce_pair_wait_ff2")
    gw_ff1, sums_ff2 = _mm_tn("gw_ff1", hn, da, pieces=N_DEV, side=_side_pair_sum(mine_ff2[0], got_ff2[0], place))
    red_ff2 = (["w_ff2"], _chip_start([sums_ff2], "reduce_start_ff2"))
    reduce_finish(red_ple, token_of(red_ff2), "ple")
    pair_ff1 = pair_begin({"w_ff1": gw_ff1}, "ff1")
    dh1, dmix, dg_mix_post, dg_ff_pre = _block_bwd(h1, mix, dh2, dhn, g_mix_post, g_ff_pre, "d_residual_mix",
                                                   deps=[token_of(pair_ff1), token_of(red_ff2)])
    mine_ff1, got_ff1 = _pair_wait(pair_ff1[1], dmix, "reduce_pair_wait_ff1")
    dhcat, sums_ff1 = _mm_nt("d_mix_in", dmix, out_w, tm=t // 2, tn=d // 2,
                             side=_side_pair_sum(mine_ff1[0], got_ff1[0], place))
    red_ff1 = (["w_ff1"], _chip_start([sums_ff1], "reduce_start_ff1"))
    gw_out = _mm_tn("gw_out", hcat, dmix, deps=[token_of(red_ff1)])
    du, gw_pool, dg_pool_scale = _pool_bwd(proj_b, pool_w, pool_scale, dhcat, "d_pool")
    pair_out = pair_begin({
        "w_out": gw_out.reshape(N_DEV, d // N_DEV, d),
        "w_pool": gw_pool.reshape(POOL_GROUPS, N_DEV, cg // N_DEV, cg).transpose(1, 0, 2, 3).reshape(N_DEV, -1, cg)},
        "out")
    dq, dk, dv, do, dcol, drow, dg_mlstm = _mlstm_bwd(qc, kc, qct, kct, proj_a, gcols, gbt, g_mlstm, dhcat, heads,
                                                      "d_mlstm", token_of(pair_out))
    red_out = chip_begin(pair_out, dq, "out")
    drow = jnp.pad(drow.reshape(heads, t), ((0, LANES - heads), (0, 0)))
    dgates, dbias = _gates_bwd(proj_b, bias, dcol, drow, pw // LANES, heads, "d_gates")
    dq_pre, gw_conv_q = _conv_bwd(proj_a, conv_w[:, :wm], dq, 0, "d_q_conv")
    dk_pre, gw_conv_k = _conv_bwd(proj_a, conv_w[:, wm:], dk, wm, "d_k_conv")
    gw_conv = jnp.concatenate([gw_conv_q, gw_conv_k], axis=1)
    dproj_a = jnp.concatenate([dq_pre, dk_pre, dv, do], axis=1)
    dproj_b = jnp.concatenate([du, dgates], axis=1)
    sums_ff2, got_ff2 = _chip_wait(red_ff2[1], token_of(red_out), "reduce_wait_ff2")
    gw_a, *update = _mm_tn("gw_in_qkvo", dproj_a, xn,
                           side=_side_adamw(*state("w_ff2"), sums_ff2[0], got_ff2[0], place))
    out["w_ff2"] = [r.reshape(weights["w_ff2"].shape) for r in update]
    gw_b = _mm_tn("gw_in_pool_gates", dproj_b, xn)
    gw_in_t = jnp.concatenate([gw_a, gw_b[pw:pw + 2 * heads], gw_b[:pw]], axis=0)
    red_in = reduce_start({
        "w_in": gw_in_t.reshape(N_DEV, in_shard, d),
        "w_qk_conv": gw_conv.reshape(QK_CONV, N_DEV, -1).transpose(1, 0, 2).astype(BF16)}, "in")
    reduce_finish(red_out, red_in[1]["token"], "out")
    sums_ff1, got_ff1 = _chip_wait(red_ff1[1], token_of(red_in), "reduce_wait_ff1")
    dxn_a, *update = _mm_nn("d_proj_qkvo_in", dproj_a, in_t_a, tn=512, tk=4 * wm, deps=[token_of(red_in)],
                            side=_side_adamw(*state("w_ff1"), sums_ff1[0], got_ff1[0], place))
    out["w_ff1"] = [r.reshape(weights["w_ff1"].shape) for r in update]
    dxn = _mm_nn("d_proj_in", dproj_b, in_t_b, extra=[dxn_a], epilogue=lambda acc, prev: (acc + prev,))[0]
    grad_x, dg_mix_pre = _norm_bwd(xs, dxn, dh1, g_mix_pre, "d_norm_mix_pre")

    small_grads = dict(b_gates=dbias[:, :2 * heads], g_mlstm=dg_mlstm, pool_scale=dg_pool_scale, g_mix_pre=dg_mix_pre,
                       g_mix_post=dg_mix_post, g_ff_pre=dg_ff_pre, g_ff_post=dg_ff_post, g_ple_gate=dg_ple_gate,
                       g_ple_post=dg_ple_post)
    small_begun = _everyone_start([pack(small_grads)], "small_grads_start")
    reduce_finish(red_in, small_begun["token"], "in")
    (small_parts,) = _everyone_wait(small_begun, out["w_in"][0], "small_grads_wait")
    res = _adamw_replicated(*small_state, small_parts, "adamw_replicated")
    for i, k in enumerate(small):
        width = weights[k].shape[1]
        out[k] = [r[i:i + 1, :width] for r in res]

    return (loss, grad_x[None], *[out[k][0] for k in order], *[out[k][1] for k in order],
            *[out[k][2] for k in order], *[out[k][3] for k in order])


def _pad_rows16(a):
    return jnp.pad(a, ((0, 16 - a.shape[0]), (0, 0)))
```

```python
import jax
import jax.numpy as jnp
from jax import lax
from jax.experimental import pallas as pl
from jax.experimental.pallas import tpu as pltpu

F32 = jnp.float32
BF16 = jnp.bfloat16
EPS = 1e-6
N_DEV = 8
N_CHIP = 4
LANES = 128
VMEM_LIMIT = 56 * 1024 * 1024
MLSTM_HEADS = 8
POOL_GROUPS = 4
QK_CONV = 4
CHUNK = 128
ROW_TILE = 256
ADAM_LR, ADAM_B1, ADAM_B2, ADAM_EPS, ADAM_WD, ADAM_STEP = 0.001, 0.9, 0.999, 1e-08, 0.01, 10
MESH_AXES = ("x", "y", "c")
MESH = pl.DeviceIdType.MESH
HBM_SPEC = pl.BlockSpec(memory_space=pltpu.HBM)
SEM_SPEC = pl.BlockSpec(memory_space=pltpu.SEMAPHORE)
DATAFLOW = pltpu.SideEffectType.DATAFLOW_SIDE_EFFECTING


def _params(*sem):
    if sem:
        return pltpu.CompilerParams(dimension_semantics=sem, vmem_limit_bytes=VMEM_LIMIT)
    return pltpu.CompilerParams(vmem_limit_bytes=VMEM_LIMIT)


def _tile(n, pref):
    if n <= pref:
        return n
    t = (pref // LANES) * LANES
    while t >= LANES:
        if n % t == 0:
            return t
        t -= LANES
    return n


def _pair_exchange(pieces, name):
    n = len(pieces)

    def body(*refs):
        ins, outs = refs[:n], refs[n:2 * n]
        send, recv = refs[2 * n:]
        x, y, c = lax.axis_index("x"), lax.axis_index("y"), lax.axis_index("c")
        sibling = (x, y, 1 - c)

        def copy(w, chip, src_core):
            return pltpu.make_async_remote_copy(
                src_ref=ins[w].at[2 * chip + src_core], dst_ref=outs[w].at[chip],
                send_sem=send.at[N_CHIP * w + chip], recv_sem=recv.at[N_CHIP * w + chip],
                device_id=sibling, device_id_type=MESH)

        cps = [copy(w, chip, 1 - c) for w in range(n) for chip in range(N_CHIP)]
        for cp in cps:
            cp.start()
        for cp in cps:
            cp.wait_recv()
        for cp in cps:
            cp.wait_send()

    outs = pl.pallas_call(
        body, name=name,
        out_shape=[jax.ShapeDtypeStruct((N_CHIP,) + a.shape[1:], a.dtype) for a in pieces],
        in_specs=[HBM_SPEC] * n, out_specs=[HBM_SPEC] * n,
        scratch_shapes=[pltpu.SemaphoreType.DMA((N_CHIP * n,)), pltpu.SemaphoreType.DMA((N_CHIP * n,))],
    )(*pieces)
    return list(outs)


def _hbm(a):
    return pltpu.with_memory_space_constraint(a, pltpu.HBM)


def _mesh_place():
    x, y, c = lax.axis_index("x"), lax.axis_index("y"), lax.axis_index("c")
    return x, y, c, [(1 - x, y), (x, 1 - y), (1 - x, 1 - y)]


def _chip_copies(srcs, lands, send, recv):
    x, y, c, chips = _mesh_place()
    return [pltpu.make_async_remote_copy(
        src_ref=srcs[w].at[2 * chip[0] + chip[1]], dst_ref=lands[w].at[k],
        send_sem=send.at[3 * w + k], recv_sem=recv.at[3 * w + k],
        device_id=(*chip, c), device_id_type=MESH)
        for k, chip in enumerate(chips) for w in range(len(srcs))]


def _chip_start(sums, name):
    return _split_start(_chip_copies, sums, [lax.empty((3,) + a.shape[1:], a.dtype) for a in sums], 3, name)


def _chip_wait(started, after, name):
    return _split_wait(_chip_copies, started, after, name)


def _pair_start(pieces, name):
    return _split_start(_pair_copies, pieces, [lax.empty((N_CHIP,) + a.shape[1:], a.dtype) for a in pieces], N_CHIP, name)


def _pair_wait(started, after, name):
    return _split_wait(_pair_copies, started, after, name)


def _pair_copies(srcs, lands, send, recv):
    x, y, c, _ = _mesh_place()
    return [pltpu.make_async_remote_copy(
        src_ref=srcs[w].at[2 * ch + 1 - c], dst_ref=lands[w].at[ch],
        send_sem=send.at[N_CHIP * w + ch], recv_sem=recv.at[N_CHIP * w + ch],
        device_id=(x, y, 1 - c), device_id_type=MESH)
        for w in range(len(srcs)) for ch in range(N_CHIP)]


def _split_start(copies_of, srcs, lands, per_array, name):
    ns, nb = len(srcs), len(srcs) + len(lands)
    n_copies = per_array * len(lands)

    def body(*refs):
        for cp in copies_of(refs[:ns], refs[ns:nb], refs[nb], refs[nb + 1]):
            cp.start()
        refs[-1][...] = jnp.zeros_like(refs[-1])

    res = pl.pallas_call(
        body, name=name,
        out_shape=[pltpu.SemaphoreType.DMA((n_copies,)), pltpu.SemaphoreType.DMA((n_copies,))]
        + [pltpu.HBM(a.shape, a.dtype) for a in list(srcs) + list(lands)] + [jax.ShapeDtypeStruct((8, LANES), F32)],
        in_specs=[HBM_SPEC] * nb,
        out_specs=[SEM_SPEC, SEM_SPEC] + [HBM_SPEC] * nb + [pl.BlockSpec(memory_space=pltpu.VMEM)],
        input_output_aliases={i: 2 + i for i in range(nb)},
        compiler_params=pltpu.CompilerParams(has_side_effects=DATAFLOW),
    )(*[_hbm(a) for a in srcs], *[_hbm(a) for a in lands])
    return dict(send=res[0], recv=res[1], srcs=list(res[2:2 + ns]), lands=list(res[2 + ns:2 + nb]), token=res[-1])


def _split_wait(copies_of, started, after, name):
    ns = len(started["srcs"])
    nb = ns + len(started["lands"])

    def body(*refs):
        for cp in copies_of(refs[:ns], refs[ns:nb], refs[nb], refs[nb + 1]):
            cp.wait_send()
            cp.wait_recv()

    res = pl.pallas_call(
        body, name=name,
        out_shape=[pltpu.HBM(a.shape, a.dtype) for a in started["srcs"] + started["lands"]],
        in_specs=[HBM_SPEC] * nb + [SEM_SPEC, SEM_SPEC, ANY_SPEC],
        out_specs=[HBM_SPEC] * nb,
        input_output_aliases={i: i for i in range(nb)},
        compiler_params=pltpu.CompilerParams(has_side_effects=DATAFLOW),
    )(*started["srcs"], *started["lands"], started["send"], started["recv"], after)
    return list(res[:ns]), list(res[ns:])


def _forward_copies(srcs, lands, send, recv):
    del srcs
    x, y, c, chips = _mesh_place()
    copies = []
    for w in range(len(lands)):
        for j, chip in enumerate(chips):
            rows = lands[w].at[4 * chip[0] + 2 * chip[1] + c]
            copies.append(pltpu.make_async_remote_copy(
                src_ref=rows, dst_ref=rows, send_sem=send.at[3 * w + j], recv_sem=recv.at[3 * w + j],
                device_id=(x, y, 1 - c), device_id_type=MESH))
    return copies


def _forward_start(lands, name):
    return _split_start(_forward_copies, [], lands, 3, name)


def _forward_wait(started, after, name):
    return _split_wait(_forward_copies, started, after, name)[1]


def _everyone_copies(srcs, lands, send, recv):
    x, y, c, _ = _mesh_place()
    me = 4 * x + 2 * y + c
    copies = []
    for w in range(len(srcs)):
        for k in range(N_DEV - 1):
            flip = k + 1
            peer = (1 - x if flip & 4 else x, 1 - y if flip & 2 else y, 1 - c if flip & 1 else c)
            copies.append(pltpu.make_async_remote_copy(
                src_ref=srcs[w], dst_ref=lands[w].at[me], send_sem=send.at[7 * w + k], recv_sem=recv.at[7 * w + k],
                device_id=peer, device_id_type=MESH))
    return copies


def _everyone_start(blocks, name):
    me = 4 * lax.axis_index("x") + 2 * lax.axis_index("y") + lax.axis_index("c")
    lands = [lax.dynamic_update_slice(lax.empty((N_DEV,) + a.shape, a.dtype), a[None], (me,) + (0,) * a.ndim)
             for a in blocks]
    return _split_start(_everyone_copies, blocks, lands, N_DEV - 1, name)


def _everyone_wait(started, after, name):
    return _split_wait(_everyone_copies, started, after, name)[1]


def _gather_copies(srcs, lands, send, recv):
    x, y, c, chips = _mesh_place()
    me = 4 * x + 2 * y + c
    targets = [(x, y, 1 - c)] + [(*chip, c) for chip in chips]
    return [pltpu.make_async_remote_copy(
        src_ref=srcs[w], dst_ref=lands[w].at[me], send_sem=send.at[4 * w + k], recv_sem=recv.at[4 * w + k],
        device_id=to, device_id_type=MESH)
        for w in range(len(srcs)) for k, to in enumerate(targets)]


def _gather_start(groups, name):
    sizes = [len(g) for g in groups]
    flat = [a for g in groups for a in g]
    n, ng = len(flat), len(groups)
    me = 4 * lax.axis_index("x") + 2 * lax.axis_index("y") + lax.axis_index("c")
    lands = [lax.dynamic_update_slice(lax.empty((N_DEV,) + a.shape, a.dtype), a[None], (me,) + (0,) * a.ndim)
             for a in flat]

    def body(*refs):
        srcs, zones = refs[:n], refs[n:2 * n]
        sems = refs[2 * n:2 * n + 2 * ng]
        token = refs[-1]
        lo = 0
        for gi, size in enumerate(sizes):
            for cp in _gather_copies(srcs[lo:lo + size], zones[lo:lo + size], sems[2 * gi], sems[2 * gi + 1]):
                cp.start()
            lo += size
        token[...] = jnp.zeros_like(token)

    sem_shapes = []
    for size in sizes:
        sem_shapes += [pltpu.SemaphoreType.DMA((4 * size,))] * 2
    res = pl.pallas_call(
        body, name=name,
        out_shape=sem_shapes + [pltpu.HBM(a.shape, a.dtype) for a in flat + lands] + [jax.ShapeDtypeStruct((8, LANES), F32)],
        in_specs=[HBM_SPEC] * (2 * n),
        out_specs=[SEM_SPEC] * (2 * ng) + [HBM_SPEC] * (2 * n) + [pl.BlockSpec(memory_space=pltpu.VMEM)],
        input_output_aliases={i: 2 * ng + i for i in range(2 * n)},
        compiler_params=pltpu.CompilerParams(has_side_effects=DATAFLOW),
    )(*[_hbm(a) for a in flat], *[_hbm(a) for a in lands])
    out, lo = [], 0
    for gi, size in enumerate(sizes):
        out.append(dict(send=res[2 * gi], recv=res[2 * gi + 1],
                        srcs=list(res[2 * ng + lo:2 * ng + lo + size]),
                        lands=list(res[2 * ng + n + lo:2 * ng + n + lo + size])))
        lo += size
    return out, res[-1]


def _gather_wait(started, after, name):
    n = len(started["srcs"])
    after = list(after) if isinstance(after, (list, tuple)) else [after]

    def body(*refs):
        srcs, zones = refs[:n], refs[n:2 * n]
        send, recv = refs[2 * n], refs[2 * n + 1]
        for cp in _gather_copies(srcs, zones, send, recv):
            cp.wait_send()
            cp.wait_recv()

    res = pl.pallas_call(
        body, name=name,
        out_shape=[pltpu.HBM(a.shape, a.dtype) for a in started["srcs"] + started["lands"]],
        in_specs=[HBM_SPEC] * (2 * n) + [SEM_SPEC, SEM_SPEC] + [ANY_SPEC] * len(after),
        out_specs=[HBM_SPEC] * (2 * n),
        input_output_aliases={i: i for i in range(2 * n)},
        compiler_params=pltpu.CompilerParams(has_side_effects=DATAFLOW),
    )(*started["srcs"], *started["lands"], started["send"], started["recv"], *after)
    return list(res[n:])


def _forward_pair(lands, name):
    n = len(lands)

    def body(*refs):
        zones = refs[:n]
        send, recv = refs[2 * n:]
        x, y, c, chips = _mesh_place()

        def copy(w, j, core):
            rows = zones[w].at[4 * chips[j][0] + 2 * chips[j][1] + core]
            return pltpu.make_async_remote_copy(
                src_ref=rows, dst_ref=rows, send_sem=send.at[3 * w + j], recv_sem=recv.at[3 * w + j],
                device_id=(x, y, 1 - c), device_id_type=MESH)

        sends = [copy(w, j, c) for w in range(n) for j in range(3)]
        for cp in sends:
            cp.start()
        for w in range(n):
            for j in range(3):
                copy(w, j, 1 - c).wait_recv()
        for cp in sends:
            cp.wait_send()

    return list(pl.pallas_call(
        body, name=name,
        out_shape=[jax.ShapeDtypeStruct(a.shape, a.dtype) for a in lands],
        in_specs=[HBM_SPEC] * n, out_specs=[HBM_SPEC] * n,
        input_output_aliases={i: i for i in range(n)},
        scratch_shapes=[pltpu.SemaphoreType.DMA((3 * n,)), pltpu.SemaphoreType.DMA((3 * n,))],
    )(*lands))


def _pair_sum(pieces, got, core, name):
    _, rows, cols = pieces.shape
    tr, tc = _tile_2d(rows, cols)

    def body(core_ref, mine_ref, got_ref, out_ref):
        del core_ref
        out_ref[...] = (mine_ref[...].astype(F32) + got_ref[...].astype(F32)).astype(out_ref.dtype)

    return pl.pallas_call(
        body, name=name,
        out_shape=jax.ShapeDtypeStruct((N_CHIP, rows, cols), pieces.dtype),
        grid_spec=pltpu.PrefetchScalarGridSpec(
            num_scalar_prefetch=1, grid=(N_CHIP, rows // tr, cols // tc),
            in_specs=[pl.BlockSpec((None, tr, tc), lambda ch, i, j, core: (2 * ch + core[0], i, j)),
                      pl.BlockSpec((None, tr, tc), lambda ch, i, j, core: (ch, i, j))],
            out_specs=pl.BlockSpec((None, tr, tc), lambda ch, i, j, core: (ch, i, j))),
        compiler_params=_params("parallel", "parallel", "parallel"),
    )(core, pieces, got)


def _tile_2d(rows, cols, budget=2 * 1024 * 1024):
    want = max(16, budget // (4 * cols))
    if rows <= want:
        return rows, cols
    t = (want // 16) * 16
    while t >= 16:
        if rows % t == 0:
            return t, cols
        t -= 16
    return rows, _tile(cols, max(LANES, budget // (4 * rows)))


def _adamw_math(w, g, m, v):
    m = ADAM_B1 * m + (1.0 - ADAM_B1) * g
    v = ADAM_B2 * v + (1.0 - ADAM_B2) * (g * g)
    m_hat = m / (1.0 - ADAM_B1 ** ADAM_STEP)
    v_hat = v / (1.0 - ADAM_B2 ** ADAM_STEP)
    delta = -ADAM_LR * (m_hat / (jnp.sqrt(v_hat) + ADAM_EPS) + ADAM_WD * w)
    return delta, m, v


def _adamw_shard(w, m, v, sums, got, chip, name):
    rows, cols = w.shape
    tr, tc = _tile_2d(rows, cols, budget=1024 * 1024)

    def body(chip_ref, w_ref, m_ref, v_ref, own_ref, got_ref, g_out, d_out, m_out, v_out):
        del chip_ref
        g = own_ref[...].astype(F32)
        for k in range(3):
            g = g + got_ref[k].astype(F32)
        delta, m_new, v_new = _adamw_math(w_ref[...], g, m_ref[...], v_ref[...])
        g_out[...] = g
        d_out[...] = delta
        m_out[...] = m_new
        v_out[...] = v_new

    blk = pl.BlockSpec((tr, tc), lambda i, j, chip: (i, j))
    return pl.pallas_call(
        body, name=name,
        out_shape=[jax.ShapeDtypeStruct((rows, cols), F32)] * 4,
        grid_spec=pltpu.PrefetchScalarGridSpec(
            num_scalar_prefetch=1, grid=(rows // tr, cols // tc),
            in_specs=[blk, blk, blk,
                      pl.BlockSpec((None, tr, tc), lambda i, j, chip: (chip[0], i, j)),
                      pl.BlockSpec((3, tr, tc), lambda i, j, chip: (0, i, j))],
            out_specs=[blk] * 4),
        compiler_params=_params("parallel", "parallel"),
    )(chip, w, m, v, sums, got)


def _adamw_replicated(w, m, v, parts, name):
    rows, cols = w.shape

    def body(w_ref, m_ref, v_ref, parts_ref, g_out, d_out, m_out, v_out):
        g = parts_ref[0]
        for d in range(1, N_DEV):
            g = g + parts_ref[d]
        delta, m_new, v_new = _adamw_math(w_ref[...], g, m_ref[...], v_ref[...])
        g_out[...] = g
        d_out[...] = delta
        m_out[...] = m_new
        v_out[...] = v_new

    return pl.pallas_call(
        body, name=name,
        out_shape=[jax.ShapeDtypeStruct((rows, cols), F32)] * 4,
        compiler_params=_params(),
    )(w, m, v, parts)


NN = ((1,), (0,))
NT = ((1,), (1,))
TN = ((0,), (0,))


ANY_SPEC = pl.BlockSpec(memory_space=pl.ANY)


def _mm(name, a, b, *, dims, grid, a_spec, b_spec, outs, extra=(), extra_specs=(), epilogue=None, acc_shape=None,
        deps=(), b_pieces=0, side=None):
    nk = grid[2]
    n_extra, n_out = len(extra), len(outs)
    side_ins = side["ins"] if side else []
    side_outs = side["outs"] if side else []
    first_side = 2 + n_extra + len(deps)
    first_out = first_side + len(side_ins)
    assert not side or nk == 1

    assert not side or (epilogue is None and n_out == 1 and not b_pieces)
    SIDE_SLICES = 4

    def body(*refs):
        dot = lambda lhs, rhs: lax.dot_general(lhs.astype(BF16), rhs.astype(BF16), (dims, ((), ())),
                                               preferred_element_type=F32)
        if side:
            refs = refs[1:]
            a_ref, b_ref, out_ref = refs[0], refs[1], refs[first_out]
            ins, sides = refs[first_side:first_out], refs[first_out + 1:first_out + 1 + len(side_outs)]
            rows = out_ref.shape[0] // SIDE_SLICES
            srows = sides[0].shape[-2] // SIDE_SLICES
            for part in range(SIDE_SLICES):
                rs = slice(part * srows, (part + 1) * srows)
                for val, o in zip(side["fn"](*[r[..., rs, :] for r in ins]), sides):
                    o[rs, :] = val.astype(o.dtype)
                ms = slice(part * rows, (part + 1) * rows)
                lhs = a_ref[:, ms] if dims == TN else a_ref[ms, :]
                out_ref[ms, :] = dot(lhs, b_ref[...]).astype(out_ref.dtype)
            return
        a_ref, b_ref = refs[0], refs[1]
        ex = refs[2:2 + n_extra]
        out_refs = refs[first_out:first_out + n_out]
        if b_pieces:
            ks = b_ref.shape[-1]
            part = dot(a_ref[:, 0:ks], b_ref[0])
            for piece in range(1, b_pieces):
                part = part + dot(a_ref[:, piece * ks:(piece + 1) * ks], b_ref[piece])
        else:
            part = dot(a_ref[...], b_ref[...])

        def finish(acc):
            res = epilogue(acc, *[e[...] for e in ex]) if epilogue else (acc,)
            for val, o in zip(res, out_refs):
                o[...] = val.astype(o.dtype)

        if nk == 1:
            finish(part)
        else:
            acc_ref = refs[-1]
            k = pl.program_id(2)

            @pl.when(k == 0)
            def _():
                acc_ref[...] = part

            @pl.when(k > 0)
            def _():
                acc_ref[...] += part

            @pl.when(k == nk - 1)
            def _():
                finish(acc_ref[...])

    scratch = [pltpu.VMEM(acc_shape, F32)] if nk > 1 else []
    out_shape = [jax.ShapeDtypeStruct(s, d) for s, d, _ in outs]
    in_specs = [a_spec, b_spec, *extra_specs] + [ANY_SPEC] * len(deps)
    out_specs = [sp for _, _, sp in outs]
    if not side:
        return list(pl.pallas_call(
            body, name=name, out_shape=out_shape, grid=grid, in_specs=in_specs, out_specs=out_specs,
            scratch_shapes=scratch, compiler_params=_params("parallel", "parallel", "arbitrary"),
        )(a, b, *extra, *deps))

    def with_scalars(spec):
        if spec.index_map is None:
            return spec
        return pl.BlockSpec(spec.block_shape, lambda i, j, k, s, f=spec.index_map: f(i, j, k))

    def by_step(block, index_map):
        return pl.BlockSpec(block, lambda i, j, k, s: index_map((i * grid[1] + j) * grid[2] + k, s))

    return list(pl.pallas_call(
        body, name=name,
        out_shape=out_shape + [jax.ShapeDtypeStruct(s, d) for s, d, _, _ in side_outs],
        grid_spec=pltpu.PrefetchScalarGridSpec(
            num_scalar_prefetch=1, grid=grid,
            in_specs=[with_scalars(sp) for sp in in_specs] + [by_step(blk, im) for _, blk, im in side_ins],
            out_specs=[with_scalars(sp) for sp in out_specs] + [by_step(blk, im) for _, _, blk, im in side_outs],
            scratch_shapes=scratch),
        compiler_params=_params("arbitrary", "arbitrary", "arbitrary"),
    )(side["scalars"], a, b, *extra, *deps, *[arr for arr, _, _ in side_ins]))


def _side_pair_sum(pieces, got, place):
    _, rows, cols = pieces.shape

    def build(steps):
        per_chip = steps // N_CHIP
        tr = rows // per_chip
        assert per_chip * N_CHIP == steps and tr * per_chip == rows and tr % 16 == 0
        blk = (None, tr, cols)
        here = lambda st, s: (st // per_chip, st % per_chip, 0)
        return dict(scalars=place,
                    ins=[(pieces, blk, lambda st, s: (2 * (st // per_chip) + s[0], st % per_chip, 0)), (got, blk, here)],
                    outs=[((N_CHIP, rows, cols), pieces.dtype, blk, here)],
                    fn=lambda mine, other: (mine.astype(F32) + other.astype(F32),))
    return build


def _side_adamw(w, m, v, sums, got, place):
    rows, cols = w.shape

    def build(steps):
        tr = rows // steps
        assert tr * steps == rows and tr % 16 == 0
        blk, here = (tr, cols), (lambda st, s: (st, 0))

        def fn(w, m, v, own, got):
            g = own.astype(F32)
            for k in range(3):
                g = g + got[k].astype(F32)
            delta, m_new, v_new = _adamw_math(w, g, m, v)
            return g, delta, m_new, v_new

        return dict(scalars=place,
                    ins=[(w, blk, here), (m, blk, here), (v, blk, here),
                         (sums, (None, tr, cols), lambda st, s: (s[1], st, 0)),
                         (got, (3, tr, cols), lambda st, s: (0, st, 0))],
                    outs=[((rows, cols), F32, blk, here)] * 4, fn=fn)
    return build


def _mm_nn(name, a, b, out_dtype=F32, tm=1024, tn=1024, tk=2048, epilogue=None, n_out=1, extra=(), deps=(), side=None):
    m, kd = a.shape
    n = b.shape[1]
    tm, tn, tk = _tile(m, tm), _tile(n, tn), _tile(kd, tk)
    dts = out_dtype if isinstance(out_dtype, (list, tuple)) else [out_dtype] * n_out
    o_spec = pl.BlockSpec((tm, tn), lambda i, j, k: (i, j))
    grid = (m // tm, n // tn, kd // tk)
    return _mm(name, a, b, dims=NN, grid=grid,
               a_spec=pl.BlockSpec((tm, tk), lambda i, j, k: (i, k)),
               b_spec=pl.BlockSpec((tk, tn), lambda i, j, k: (k, j)),
               outs=[((m, n), dt, o_spec) for dt in dts], extra=extra, extra_specs=[o_spec] * len(extra),
               epilogue=epilogue, acc_shape=(tm, tn), deps=deps, side=side and side(grid[0] * grid[1] * grid[2]))


def _mm_nn_pieces(name, a, b, out_dtype=F32, tm=1024, tk=2048, epilogue=None, n_out=1):
    m, kd = a.shape
    npc, _, ns = b.shape
    tm, tk = _tile(m, tm), _tile(kd, tk)
    tn = _tile(ns, 1024)
    r = ns // tn
    dts = out_dtype if isinstance(out_dtype, (list, tuple)) else [out_dtype] * n_out
    o_spec = pl.BlockSpec((tm, tn), lambda i, j, k: (i, j))
    return _mm(name, a, b, dims=NN, grid=(m // tm, npc * r, kd // tk),
               a_spec=pl.BlockSpec((tm, tk), lambda i, j, k: (i, k)),
               b_spec=pl.BlockSpec((None, tk, tn), lambda i, j, k: (j // r, k, j % r)),
               outs=[((m, npc * ns), dt, o_spec) for dt in dts], epilogue=epilogue, acc_shape=(tm, tn))


def _mm_nt(name, a, b, out_dtype=F32, tm=1024, tn=1024, tk=2048, epilogue=None, extra=(), n_rows=None, deps=(),
           side=None):
    m, kd = a.shape
    n = n_rows or b.shape[0]
    tm, tn, tk = _tile(m, tm), _tile(n, tn), _tile(kd, tk)
    o_spec = pl.BlockSpec((tm, tn), lambda i, j, k: (i, j))
    grid = (m // tm, n // tn, kd // tk)
    res = _mm(name, a, b, dims=NT, grid=grid,
              a_spec=pl.BlockSpec((tm, tk), lambda i, j, k: (i, k)),
              b_spec=pl.BlockSpec((tn, tk), lambda i, j, k: (j, k)),
              outs=[((m, n), out_dtype, o_spec)], extra=extra, extra_specs=[o_spec] * len(extra),
              epilogue=epilogue, acc_shape=(tm, tn), deps=deps, side=side and side(grid[0] * grid[1] * grid[2]))
    return res if side else res[0]


def _mm_nt_pieces(name, a, b, out_dtype=F32, tm=1024, tn=256, deps=()):
    m = a.shape[0]
    npc, n, ks = b.shape
    tm, tn = _tile(m, tm), _tile(n, tn)
    return _mm(name, a, b, dims=NT, grid=(m // tm, n // tn, 1),
               a_spec=pl.BlockSpec((tm, npc * ks), lambda i, j, k: (i, 0)),
               b_spec=pl.BlockSpec((npc, tn, ks), lambda i, j, k: (0, j, 0)),
               outs=[((m, n), out_dtype, pl.BlockSpec((tm, tn), lambda i, j, k: (i, j)))],
               deps=deps, b_pieces=npc)[0]


def _mm_tn(name, a, b, out_dtype=BF16, tm=1024, tn=1024, pieces=0, deps=(), side=None):
    t, m = a.shape
    n = b.shape[1]
    tm = _tile(m, tm)
    if pieces:
        ns = n // pieces
        tn = _tile(ns, tn)
        r = ns // tn
        out = ((pieces, m, ns), out_dtype, pl.BlockSpec((None, tm, tn), lambda i, j, k: (j // r, i, j % r)))
    else:
        tn = _tile(n, tn)
        out = ((m, n), out_dtype, pl.BlockSpec((tm, tn), lambda i, j, k: (i, j)))
    grid = (m // tm, n // tn, 1)
    res = _mm(name, a, b, dims=TN, grid=grid,
              a_spec=pl.BlockSpec((t, tm), lambda i, j, k: (0, i)),
              b_spec=pl.BlockSpec((t, tn), lambda i, j, k: (0, j)),
              outs=[out], deps=deps, side=side and side(grid[0] * grid[1]))
    return res if side else res[0]


def _rms_f(x, g):
    r = lax.rsqrt(jnp.mean(x * x, axis=-1, keepdims=True) + EPS)
    xh = x * r
    return xh * g, xh, r


def _rms_b(dy, g, xh, r):
    dxh = dy * g
    dx = r * (dxh - xh * jnp.mean(dxh * xh, axis=-1, keepdims=True))
    dg = jnp.sum(dy * xh, axis=0, keepdims=True)
    return dx, dg


def _rowwise(name, fn, row_ins, vec_ins, row_outs, vec_outs, deps=()):
    t = row_ins[0].shape[0]
    tr = min(ROW_TILE, t)
    nr, nv, no = len(row_ins), len(vec_ins), len(row_outs)
    first_out = nr + nv + len(deps)

    def body(*refs):
        ri, vi = refs[:nr], refs[nr:nr + nv]
        ro, vo = refs[first_out:first_out + no], refs[first_out + no:]
        routs, vouts = fn(*[r[...] for r in ri], *[v[...] for v in vi])
        for o, val in zip(ro, routs):
            o[...] = val.astype(o.dtype)
        i = pl.program_id(0)
        for o, val in zip(vo, vouts):
            @pl.when(i == 0)
            def _():
                o[...] = val

            @pl.when(i > 0)
            def _():
                o[...] += val

    res = pl.pallas_call(
        body, name=name,
        out_shape=[jax.ShapeDtypeStruct((t, w), d) for w, d in row_outs]
        + [jax.ShapeDtypeStruct((1, w), F32) for w in vec_outs],
        grid=(t // tr,),
        in_specs=[pl.BlockSpec((tr, a.shape[1]), lambda i: (i, 0)) for a in row_ins]
        + [pl.BlockSpec((1, a.shape[1]), lambda i: (0, 0)) for a in vec_ins] + [ANY_SPEC] * len(deps),
        out_specs=[pl.BlockSpec((tr, w), lambda i: (i, 0)) for w, _ in row_outs]
        + [pl.BlockSpec((1, w), lambda i: (0, 0)) for w in vec_outs],
        compiler_params=_params("arbitrary"),
    )(*row_ins, *vec_ins, *deps)
    return list(res)


def _norm_fwd(x, g, name, deps=()):
    d = x.shape[1]
    return _rowwise(name, lambda x, g: ((_rms_f(x, g)[0],), ()), [x], [g], [(d, BF16)], [], deps=deps)[0]


def _norm_bwd(x, dxn, dres, g, name):
    d = x.shape[1]

    def fn(x, dxn, dres, g):
        _, xh, r = _rms_f(x, g)
        dx, dg = _rms_b(dxn, g, xh, r)
        return (dres + dx,), (dg,)

    return _rowwise(name, fn, [x, dxn, dres], [g], [(d, F32)], [d])


def _block_fwd(h_prev, y, g_y, g_n, name, deps=()):
    d = h_prev.shape[1]

    def fn(h_prev, y, g_y, g_n):
        h = h_prev + _rms_f(y, g_y)[0]
        return (h, _rms_f(h, g_n)[0]), ()

    return _rowwise(name, fn, [h_prev, y], [g_y, g_n], [(d, F32), (d, BF16)], [], deps=deps)


def _block_bwd(h, y, d_up, d_n, g_y, g_n, name, deps=()):
    d = h.shape[1]

    def fn(h, y, d_up, d_n, g_y, g_n):
        _, hh, hr = _rms_f(h, g_n)
        dh_n, dg_n = _rms_b(d_n, g_n, hh, hr)
        dh = d_up + dh_n
        _, yh, yr = _rms_f(y, g_y)
        dy, dg_y = _rms_b(dh, g_y, yh, yr)
        return (dh, dy), (dg_y, dg_n)

    return _rowwise(name, fn, [h, y, d_up, d_n], [g_y, g_n], [(d, F32), (d, BF16)], [d, d], deps=deps)


def _loss_head(h2, z, e, target, g, name):
    d = h2.shape[1]

    def fn(h2, z, e, target, g):
        gate = jax.nn.sigmoid(z)
        y, xh, r = _rms_f(e * gate, g)
        diff = h2 + y - target
        loss = 0.5 * jnp.sum(jnp.mean(diff * diff, axis=-1, keepdims=True), axis=0, keepdims=True)
        dh3 = diff / d
        deg, dg = _rms_b(dh3, g, xh, r)
        dz = deg * e * gate * (1.0 - gate)
        de = deg * gate
        return (dh3, dz, de), (dg, jnp.broadcast_to(loss, (1, LANES)))

    return _rowwise(name, fn, [h2, z, e, target], [g], [(d, F32), (d, BF16), (d, BF16)], [d, LANES])


def _shift_down(x, s, rows):
    return jnp.where(rows >= s, pltpu.roll(x, s, axis=0), 0.0)


def _shift_up(x, s, rows):
    t = x.shape[0]
    return jnp.where(rows < t - s, pltpu.roll(x, t - s, axis=0), 0.0)


def _conv_pre(x, w, rows):
    pre = x * w[QK_CONV - 1:QK_CONV, :]
    for s in range(1, QK_CONV):
        pre = pre + _shift_down(x, s, rows) * w[QK_CONV - 1 - s:QK_CONV - s, :]
    return pre


def _conv_fwd(proj_a, w, start, name):
    t = proj_a.shape[0]
    width = w.shape[1]
    tc = _tile(width, 256)
    off = start // tc

    def body(x_ref, w_ref, y_ref):
        x = x_ref[...]
        rows = lax.broadcasted_iota(jnp.int32, x.shape, 0)
        pre = _conv_pre(x, w_ref[...], rows)
        y_ref[...] = pre * jax.nn.sigmoid(pre)

    return pl.pallas_call(
        body, name=name,
        out_shape=jax.ShapeDtypeStruct((t, width), F32),
        grid=(width // tc,),
        in_specs=[pl.BlockSpec((t, tc), lambda j: (0, off + j)), pl.BlockSpec((QK_CONV, tc), lambda j: (0, j))],
        out_specs=pl.BlockSpec((t, tc), lambda j: (0, j)),
        compiler_params=_params("parallel"),
    )(proj_a, w)


def _conv_bwd(proj_a, w, dy, start, name):
    t = proj_a.shape[0]
    width = w.shape[1]
    tc = _tile(width, 256)
    off = start // tc

    def body(x_ref, w_ref, dy_ref, dx_ref, dw_ref):
        x, w = x_ref[...], w_ref[...]
        rows = lax.broadcasted_iota(jnp.int32, x.shape, 0)
        pre = _conv_pre(x, w, rows)
        sig = jax.nn.sigmoid(pre)
        dpre = dy_ref[...] * (sig * (1.0 + pre * (1.0 - sig)))
        dx = dpre * w[QK_CONV - 1:QK_CONV, :]
        dws = [jnp.sum(dpre * x, axis=0, keepdims=True)]
        for s in range(1, QK_CONV):
            dx = dx + _shift_up(dpre, s, rows) * w[QK_CONV - 1 - s:QK_CONV - s, :]
            dws.append(jnp.sum(dpre * _shift_down(x, s, rows), axis=0, keepdims=True))
        dx_ref[...] = dx.astype(dx_ref.dtype)
        for s in range(QK_CONV):
            dw_ref[QK_CONV - 1 - s:QK_CONV - s, :] = dws[s]

    return pl.pallas_call(
        body, name=name,
        out_shape=[jax.ShapeDtypeStruct((t, width), BF16), jax.ShapeDtypeStruct((QK_CONV, width), F32)],
        grid=(width // tc,),
        in_specs=[pl.BlockSpec((t, tc), lambda j: (0, off + j)), pl.BlockSpec((QK_CONV, tc), lambda j: (0, j)),
                  pl.BlockSpec((t, tc), lambda j: (0, j))],
        out_specs=[pl.BlockSpec((t, tc), lambda j: (0, j)), pl.BlockSpec((QK_CONV, tc), lambda j: (0, j))],
        compiler_params=_params("parallel"),
    )(proj_a, w, dy)


def _dot(a, b, dims):
    return lax.dot_general(a.astype(BF16), b.astype(BF16), (dims, ((), ())), preferred_element_type=F32)


def _dot_split(ones, x):
    hi = x.astype(BF16)
    rest = x - hi.astype(F32)
    mid = rest.astype(BF16)
    lo = (rest - mid.astype(F32)).astype(BF16)
    dot = lambda part: lax.dot_general(ones, part, (NN, ((), ())), preferred_element_type=F32)
    return dot(hi) + dot(mid) + dot(lo)


def _chunk_iotas():
    rows = lax.broadcasted_iota(jnp.int32, (CHUNK, CHUNK), 0)
    cols = lax.broadcasted_iota(jnp.int32, (CHUNK, CHUNK), 1)
    return rows, cols


def _gates_fwd(proj_b, bias, col_block, heads, name):
    t = proj_b.shape[0]

    def body(x_ref, b_ref, gb_ref, gbt_ref):
        rows, cols = _chunk_iotas()
        tri = (cols <= rows).astype(BF16)
        for ci in range(t // CHUNK):
            sl = slice(ci * CHUNK, (ci + 1) * CHUNK)
            pre = x_ref[sl, :] + b_ref[...]
            log_f = jnp.minimum(pre, 0.0) - jnp.log(1.0 + jnp.exp(-jnp.abs(pre)))
            log_f = jnp.where((cols >= heads) & (cols < 2 * heads), log_f, 0.0)
            tile = jnp.where(cols < heads, pre, _dot_split(tri, log_f))
            gb_ref[sl, :] = tile
            gbt_ref[:, sl] = tile.T

    return pl.pallas_call(
        body, name=name,
        out_shape=[jax.ShapeDtypeStruct((t, LANES), F32), jax.ShapeDtypeStruct((LANES, t), F32)],
        grid=(1,),
        in_specs=[pl.BlockSpec((t, LANES), lambda i: (0, col_block)), pl.BlockSpec((1, LANES), lambda i: (0, 0))],
        out_specs=[pl.BlockSpec((t, LANES), lambda i: (0, 0)), pl.BlockSpec((LANES, t), lambda i: (0, 0))],
        compiler_params=_params("arbitrary"),
    )(proj_b, bias)


def _gate_columns(gb, heads, name):
    t = gb.shape[0]

    def body(gb_ref, out_ref):
        g = gb_ref[...]
        lane = lax.broadcasted_iota(jnp.int32, g.shape, 1)
        column = jnp.sum(jnp.where(lane == pl.program_id(0), g, 0.0), axis=-1, keepdims=True)
        out_ref[...] = jnp.broadcast_to(column, out_ref.shape)

    return pl.pallas_call(
        body, name=name,
        out_shape=jax.ShapeDtypeStruct((2 * heads, t, LANES), F32),
        grid=(2 * heads,),
        in_specs=[pl.BlockSpec((t, LANES), lambda h: (0, 0))],
        out_specs=pl.BlockSpec((None, t, LANES), lambda h: (h, 0, 0)),
        compiler_params=_params("parallel"),
    )(gb)


def _gates_bwd(proj_b, bias, dcol, drow, col_block, heads, name):
    t = proj_b.shape[0]

    def body(x_ref, b_ref, dc_ref, dr_ref, dx_ref, db_ref):
        rows, cols = _chunk_iotas()
        later = (cols >= rows).astype(BF16)
        total = jnp.zeros((1, LANES), F32)
        for ci in range(t // CHUNK):
            sl = slice(ci * CHUNK, (ci + 1) * CHUNK)
            by_row = dr_ref[:, sl].T
            d = dc_ref[sl, :] + jnp.where(cols < heads, by_row,
                                          jnp.where(cols < 2 * heads, -pltpu.roll(by_row, heads, axis=1), 0.0))
            d_log_f = _dot_split(later, jnp.where(cols >= heads, d, 0.0))
            pre = x_ref[sl, :] + b_ref[...]
            dx = jnp.where(cols < heads, d, jnp.where(cols < 2 * heads, d_log_f * jax.nn.sigmoid(-pre), 0.0))
            dx_ref[sl, :] = dx.astype(dx_ref.dtype)
            total = total + jnp.sum(dx, axis=0, keepdims=True)
        db_ref[...] = total

    return pl.pallas_call(
        body, name=name,
        out_shape=[jax.ShapeDtypeStruct((t, LANES), BF16), jax.ShapeDtypeStruct((1, LANES), F32)],
        grid=(1,),
        in_specs=[pl.BlockSpec((t, LANES), lambda i: (0, col_block)), pl.BlockSpec((1, LANES), lambda i: (0, 0)),
                  pl.BlockSpec((t, LANES), lambda i: (0, 0)), pl.BlockSpec((LANES, t), lambda i: (0, 0))],
        out_specs=[pl.BlockSpec((t, LANES), lambda i: (0, 0)), pl.BlockSpec((1, LANES), lambda i: (0, 0))],
        compiler_params=_params("arbitrary"),
    )(proj_b, bias, dcol, drow)


def _gate_views(li_ref, b_ref, gbt_ref, s, head, heads, sl):
    li_c, b_c = li_ref[s, sl, :], b_ref[s, sl, :]
    li_r = jnp.broadcast_to(gbt_ref[head, :, sl], (CHUNK, CHUNK))
    b_r = jnp.broadcast_to(gbt_ref[heads + head, :, sl], (CHUNK, CHUNK))
    return li_c, b_c, li_r, b_r


def _chunk_state(kt, k, v, li_c, b_c, ct, nrow, m_prev):
    b_tot = b_c[CHUNK - 1:CHUNK, 0:1]
    a = b_tot - b_c[:, 0:1] + li_c[:, 0:1]
    m_new = jnp.maximum(b_tot + m_prev, jnp.max(a, axis=0, keepdims=True))
    dec = jnp.exp(b_tot + m_prev - m_new)
    w = jnp.exp(a - m_new)
    wv = w * v
    ct_new = dec * ct + _dot(kt, wv, NN)
    n_new = dec * nrow + jnp.sum(w * k, axis=0, keepdims=True)
    return ct_new, n_new, m_new, dec, w, wv


def _chunk_scores(q, k, b_c, li_r, b_r, m_prev):
    rows, cols = _chunk_iotas()
    dlog = jnp.where(cols <= rows, b_c - b_r + li_r, -jnp.inf)
    inter = b_c[:, 0:1] + m_prev
    mt = jnp.maximum(inter, jnp.max(dlog, axis=-1, keepdims=True))
    dw = jnp.exp(dlog - mt)
    iw = jnp.exp(inter - mt)
    a = _dot(q, k, NT)
    return dw, iw, mt, a, a * dw


def _to_row(column):
    rows, cols = _chunk_iotas()
    return jnp.sum(jnp.where(rows == cols, jnp.broadcast_to(column, (CHUNK, CHUNK)), 0.0), axis=0, keepdims=True)


def _head_out(h, o, g):
    cell, hh, r = _rms_f(h, g)
    sig = jax.nn.sigmoid(o)
    return sig * cell, (cell, hh, r, sig)


HEADS_PER_STEP = 2


def _mlstm_fwd(qc, kc, kct, proj_a, gcols, gbt, g_mlstm, heads, d_model, name, dep):
    t = qc.shape[0]
    dh = g_mlstm.shape[1] // heads
    nc = t // CHUNK
    scale = dh ** -0.5
    hp = HEADS_PER_STEP

    def body(q_ref, k_ref, kt_ref, v_ref, o_ref, li_ref, b_ref, gbt_ref, g_ref, dep_ref, out_ref, ct_out, nm_out,
             ct_ref, n_ref, m_ref):
        del dep_ref
        step = pl.program_id(0)
        ct_ref[...] = jnp.zeros_like(ct_ref)
        n_ref[...] = jnp.zeros_like(n_ref)
        m_ref[...] = jnp.zeros_like(m_ref)
        tile_row = lax.broadcasted_iota(jnp.int32, (8, LANES), 0)

        def chunk(ci, carry):
            sl = pl.ds(pl.multiple_of(ci * CHUNK, CHUNK), CHUNK)
            for s in range(hp):
                cs = slice(s * dh, (s + 1) * dh)
                q, k, v, kt = q_ref[sl, cs], k_ref[sl, cs] * scale, v_ref[sl, cs], kt_ref[cs, sl]
                li_c, b_c, li_r, b_r = _gate_views(li_ref, b_ref, gbt_ref, s, step * hp + s, heads, sl)
                ct, nrow, m_prev = ct_ref[s], n_ref[s, 0:1, :], m_ref[s, 0:1, 0:1]
                ct_out[s, ci] = ct
                nm_out[s, ci] = jnp.where(tile_row == 0, nrow, jnp.where(tile_row == 1, m_prev, 0.0))
                dw, iw, mt, _, sm = _chunk_scores(q, k, b_c, li_r, b_r, m_prev)
                iq = iw * q
                num = _dot(jnp.concatenate([iq, sm], axis=1), jnp.concatenate([ct, v], axis=0), NN)
                den = jnp.sum(iq * nrow + sm, axis=-1, keepdims=True)
                h = num / jnp.maximum(jnp.abs(den), jnp.exp(-mt))
                y, _ = _head_out(h, o_ref[sl, cs], g_ref[:, cs])
                out_ref[sl, cs] = y.astype(out_ref.dtype)
                ct_new, n_new, m_new, _, _, _ = _chunk_state(kt, k, v, li_c, b_c, ct, nrow, m_prev)
                ct_ref[s] = ct_new
                n_ref[s, 0:1, :] = n_new
                m_ref[s, 0:1, 0:1] = m_new
            return carry

        lax.fori_loop(0, nc, chunk, 0, unroll=2)

    col = lambda off: pl.BlockSpec((t, hp * dh), lambda h: (0, off // hp + h))
    slab = lambda off: pl.BlockSpec((hp, t, LANES), lambda h: (off // hp + h, 0, 0))
    return pl.pallas_call(
        body, name=name,
        out_shape=[jax.ShapeDtypeStruct((t, d_model), BF16), jax.ShapeDtypeStruct((heads, nc, dh, dh), F32),
                   jax.ShapeDtypeStruct((heads, nc, 8, LANES), F32)],
        grid=(heads // hp,),
        in_specs=[col(0), col(0), pl.BlockSpec((hp * dh, t), lambda h: (h, 0)), col(2 * heads), col(3 * heads),
                  slab(0), slab(heads), pl.BlockSpec((2 * heads, 1, t), lambda h: (0, 0, 0)),
                  pl.BlockSpec((1, hp * dh), lambda h: (0, h)), ANY_SPEC],
        out_specs=[pl.BlockSpec((t, hp * dh), lambda h: (0, h)),
                   pl.BlockSpec((hp, nc, dh, dh), lambda h: (h, 0, 0, 0)),
                   pl.BlockSpec((hp, nc, 8, LANES), lambda h: (h, 0, 0, 0))],
        scratch_shapes=[pltpu.VMEM((hp, dh, dh), F32), pltpu.VMEM((hp, 8, dh), F32), pltpu.VMEM((hp, 8, LANES), F32)],
        compiler_params=_params("arbitrary"),
    )(qc, kc, kct, proj_a, proj_a, gcols, gcols, gbt, g_mlstm, dep)


def _mlstm_bwd(qc, kc, qct, kct, proj_a, gcols, gbt, g_mlstm, states, dhcat, heads, name, dep):
    t = qc.shape[0]
    wm = g_mlstm.shape[1]
    dh = wm // heads
    nc = t // CHUNK
    scale = dh ** -0.5

    def body(q_ref, k_ref, qt_ref, kt_ref, v_ref, o_ref, li_ref, b_ref, gbt_ref, g_ref, ct_all, nm_all, dy_ref, dep_ref,
             dq_ref, dk_ref, dv_ref, do_ref, dcol_ref, drow_ref, dg_ref, dct_ref, dn_ref):
        del dep_ref
        head = pl.program_id(0)
        g = g_ref[...]

        @pl.when(head == 0)
        def _():
            dcol_ref[...] = jnp.zeros_like(dcol_ref)
            drow_ref[...] = jnp.zeros_like(drow_ref)

        def load(ci):
            sl = pl.ds(pl.multiple_of(ci * CHUNK, CHUNK), CHUNK)
            return (sl, q_ref[sl, :], k_ref[sl, :] * scale, v_ref[sl, :], kt_ref[:, sl]) \
                + _gate_views(li_ref, b_ref, gbt_ref, 0, head, heads, sl)

        dct_ref[...] = jnp.zeros_like(dct_ref)
        dn_ref[...] = jnp.zeros_like(dn_ref)
        dg_ref[...] = jnp.zeros_like(dg_ref)
        rows, cols = _chunk_iotas()
        last = lax.broadcasted_iota(jnp.int32, (CHUNK, 1), 0) == CHUNK - 1
        lane = lax.broadcasted_iota(jnp.int32, (CHUNK, LANES), 1)

        def bwd(step, carry):
            ci = nc - 1 - step
            sl, q, k, v, kt, li_c, b_c, li_r, b_r = load(ci)
            qt = qt_ref[:, sl]
            ct, nrow, m_prev = ct_all[ci], nm_all[ci, 0:1, :], nm_all[ci, 1:2, 0:1]
            dw, iw, mt, a, sm = _chunk_scores(q, k, b_c, li_r, b_r, m_prev)
            qc_ = _dot(q, ct, NN)
            qn = jnp.sum(q * nrow, axis=-1, keepdims=True)
            num = iw * qc_ + _dot(sm, v, NN)
            den = iw * qn + jnp.sum(sm, axis=-1, keepdims=True)
            floor = jnp.exp(-mt)
            dn = jnp.maximum(jnp.abs(den), floor)
            h = num / dn
            _, (cell, hh, r, sig) = _head_out(h, o_ref[sl, :], g)
            dy = dy_ref[sl, :]
            dcell = dy * sig
            do_ref[sl, :] = (dy * cell * sig * (1.0 - sig)).astype(do_ref.dtype)
            dh_, dg = _rms_b(dcell, g, hh, r)
            dg_ref[...] += dg
            dnum = dh_ / dn
            ddn = -jnp.sum(dh_ * h, axis=-1, keepdims=True) / dn
            dden = jnp.where(jnp.abs(den) >= floor, ddn * jnp.sign(den), 0.0)
            idn = iw * dnum
            idd = iw * dden
            dw_t = jnp.exp(jnp.where(cols >= rows, b_r - b_c + li_c, -jnp.inf) - _to_row(mt))
            sm_t = _dot(k, q, NT) * dw_t
            dsm = _dot(dnum, v, NT) + dden
            da_ = dsm * dw
            da_t = (_dot(v, dnum, NT) + _to_row(dden)) * dw_t
            e = da_ * a
            dq = _dot(idn, ct, NT) + _dot(da_, k, NN) + idd * nrow
            dk = _dot(da_t, q, NN)
            dv = _dot(sm_t, dnum, NN)
            dct = _dot(qt, idn, NN)
            dnr = jnp.sum(q * idd, axis=0, keepdims=True)
            db = jnp.sum(e + idn * qc_, axis=-1, keepdims=True) + idd * qn
            _, _, _, dec, w, wv = _chunk_state(kt, k, v, li_c, b_c, ct, nrow, m_prev)
            dct_new, dn_new = dct_ref[...], dn_ref[0:1, :]
            ddec = jnp.sum(jnp.sum(dct_new * ct, axis=-1, keepdims=True), axis=0, keepdims=True) \
                + jnp.sum(dn_new * nrow, axis=-1, keepdims=True)
            dk = dk + _dot(wv, dct_new, NT) + w * dn_new
            dwv = _dot(k, dct_new, NN)
            dv = dv + w * dwv
            dwt = jnp.sum(dwv * v + k * dn_new, axis=-1, keepdims=True)
            da = dwt * w
            dbtot = jnp.sum(da, axis=0, keepdims=True) + ddec * dec
            db = db - da + jnp.where(last, dbtot, 0.0)
            dct_ref[...] = dec * dct_new + dct
            dn_ref[0:1, :] = dec * dn_new + dnr
            dq_ref[sl, :] = dq
            dk_ref[sl, :] = dk * scale
            dv_ref[sl, :] = dv.astype(dv_ref.dtype)
            dcol_ref[sl, :] += jnp.where(lane == head, da, 0.0) + jnp.where(lane == heads + head, db, 0.0)
            drow_ref[head, :, sl] = jnp.sum(e, axis=0, keepdims=True)
            return carry

        lax.fori_loop(0, nc, bwd, 0, unroll=4)

    col = lambda off: pl.BlockSpec((t, dh), lambda h: (0, off + h))
    rowb = pl.BlockSpec((dh, t), lambda h: (h, 0))
    slab = lambda off: pl.BlockSpec((1, t, LANES), lambda h: (off + h, 0, 0))
    vec = pl.BlockSpec((1, dh), lambda h: (0, h))
    return pl.pallas_call(
        body, name=name,
        out_shape=[jax.ShapeDtypeStruct((t, wm), F32)] * 2 + [jax.ShapeDtypeStruct((t, wm), BF16)] * 2
        + [jax.ShapeDtypeStruct((t, LANES), F32), jax.ShapeDtypeStruct((heads, 1, t), F32),
           jax.ShapeDtypeStruct((1, wm), F32)],
        grid=(heads,),
        in_specs=[col(0), col(0), rowb, rowb, col(2 * heads), col(3 * heads), slab(0), slab(heads),
                  pl.BlockSpec((2 * heads, 1, t), lambda h: (0, 0, 0)), vec,
                  pl.BlockSpec((None, nc, dh, dh), lambda h: (h, 0, 0, 0)),
                  pl.BlockSpec((None, nc, 8, LANES), lambda h: (h, 0, 0, 0)), col(0), ANY_SPEC],
        out_specs=[col(0), col(0), col(0), col(0), pl.BlockSpec((t, LANES), lambda h: (0, 0)),
                   pl.BlockSpec((heads, 1, t), lambda h: (0, 0, 0)), vec],
        scratch_shapes=[pltpu.VMEM((dh, dh), F32), pltpu.VMEM((8, dh), F32)],
        compiler_params=_params("arbitrary"),
    )(qc, kc, qct, kct, proj_a, proj_a, gcols, gcols, gbt, g_mlstm, *states, dhcat, dep)


def _pool_window(gi):
    return jnp.where(gi == 0, 2.0, jnp.where(gi == 1, 4.0, jnp.where(gi == 2, 8.0, 16.0))).astype(F32)


def _pool_select(gi, levels):
    return jnp.where(gi == 0, levels[0], jnp.where(gi == 1, levels[1], jnp.where(gi == 2, levels[2], levels[3])))


def _pooled(u, gi, rows):
    s, levels = u, []
    for lvl in range(POOL_GROUPS):
        s = s + _shift_down(s, 1 << lvl, rows)
        levels.append(s)
    cnt = jnp.minimum((rows + 1).astype(F32), _pool_window(gi))
    return _pool_select(gi, levels) / cnt - u, cnt


def _pool_fwd(proj_b, w_pool, scale, hcat, name):
    t, d_model = hcat.shape
    g, cg, _ = w_pool.shape
    off = (d_model - g * cg) // cg

    def body(u_ref, w_ref, s_ref, hcat_ref, out_ref):
        del hcat_ref
        gi = pl.program_id(0)
        u = u_ref[...]
        rows = lax.broadcasted_iota(jnp.int32, u.shape, 0)
        pooled, _ = _pooled(u, gi, rows)
        out_ref[...] = (_dot(pooled, w_ref[...], NN) * s_ref[...]).astype(out_ref.dtype)

    return pl.pallas_call(
        body, name=name,
        out_shape=jax.ShapeDtypeStruct((t, d_model), BF16),
        grid=(g,),
        in_specs=[pl.BlockSpec((t, cg), lambda gi: (0, gi)), pl.BlockSpec((None, cg, cg), lambda gi: (gi, 0, 0)),
                  pl.BlockSpec((1, cg), lambda gi: (0, gi)), HBM_SPEC],
        out_specs=pl.BlockSpec((t, cg), lambda gi: (0, off + gi)),
        input_output_aliases={3: 0},
        compiler_params=_params("arbitrary"),
    )(proj_b, w_pool, scale, hcat)


def _pool_bwd(proj_b, w_pool, scale, dhcat, name):
    t, d_model = dhcat.shape
    g, cg, _ = w_pool.shape
    off = (d_model - g * cg) // cg

    def body(u_ref, w_ref, s_ref, dy_ref, du_ref, dw_ref, ds_ref):
        gi = pl.program_id(0)
        u, w, dy = u_ref[...], w_ref[...], dy_ref[...]
        rows = lax.broadcasted_iota(jnp.int32, u.shape, 0)
        pooled, cnt = _pooled(u, gi, rows)
        mixed = _dot(pooled, w, NN)
        ds_ref[...] = jnp.sum(dy * mixed, axis=0, keepdims=True)
        dmixed = dy * s_ref[...]
        dw_ref[...] = _dot(pooled, dmixed, TN).astype(dw_ref.dtype)
        dpooled = _dot(dmixed, w, NT)
        s, levels = dpooled / cnt, []
        for lvl in range(POOL_GROUPS):
            s = s + _shift_up(s, 1 << lvl, rows)
            levels.append(s)
        du_ref[...] = (_pool_select(gi, levels) - dpooled).astype(du_ref.dtype)

    return pl.pallas_call(
        body, name=name,
        out_shape=[jax.ShapeDtypeStruct((t, g * cg), BF16), jax.ShapeDtypeStruct((g, cg, cg), BF16),
                   jax.ShapeDtypeStruct((1, g * cg), F32)],
        grid=(g,),
        in_specs=[pl.BlockSpec((t, cg), lambda gi: (0, gi)), pl.BlockSpec((None, cg, cg), lambda gi: (gi, 0, 0)),
                  pl.BlockSpec((1, cg), lambda gi: (0, gi)), pl.BlockSpec((t, cg), lambda gi: (0, off + gi))],
        out_specs=[pl.BlockSpec((t, cg), lambda gi: (0, gi)), pl.BlockSpec((None, cg, cg), lambda gi: (gi, 0, 0)),
                   pl.BlockSpec((1, cg), lambda gi: (0, gi))],
        compiler_params=_params("arbitrary"),
    )(proj_b, w_pool, scale, dhcat)


def _pad_cols(a, width):
    return jnp.pad(a, ((0, 0), (0, width - a.shape[1])))


def kernel(x, p, w_in, b_gates, w_qk_conv, g_mlstm, w_pool, pool_scale, w_out, g_mix_pre, g_mix_post, w_ff1, w_ff2, g_ff_pre, g_ff_post, w_ple_proj, w_ple_gate, g_ple_gate, g_ple_post, loss_target, m_w_in, m_b_gates, m_w_qk_conv, m_g_mlstm, m_w_pool, m_pool_scale, m_w_out, m_g_mix_pre, m_g_mix_post, m_w_ff1, m_w_ff2, m_g_ff_pre, m_g_ff_post, m_w_ple_proj, m_w_ple_gate, m_g_ple_gate, m_g_ple_post, v_w_in, v_b_gates, v_w_qk_conv, v_g_mlstm, v_w_pool, v_pool_scale, v_w_out, v_g_mix_pre, v_g_mix_post, v_w_ff1, v_w_ff2, v_g_ff_pre, v_g_ff_post, v_w_ple_proj, v_w_ple_gate, v_g_ple_gate, v_g_ple_post):
    weights = dict(w_in=w_in, b_gates=b_gates, w_qk_conv=w_qk_conv, g_mlstm=g_mlstm, w_pool=w_pool,
                   pool_scale=pool_scale, w_out=w_out, g_mix_pre=g_mix_pre, g_mix_post=g_mix_post, w_ff1=w_ff1,
                   w_ff2=w_ff2, g_ff_pre=g_ff_pre, g_ff_post=g_ff_post, w_ple_proj=w_ple_proj,
                   w_ple_gate=w_ple_gate, g_ple_gate=g_ple_gate, g_ple_post=g_ple_post)
    mom1 = dict(w_in=m_w_in, b_gates=m_b_gates, w_qk_conv=m_w_qk_conv, g_mlstm=m_g_mlstm, w_pool=m_w_pool,
                pool_scale=m_pool_scale, w_out=m_w_out, g_mix_pre=m_g_mix_pre, g_mix_post=m_g_mix_post,
                w_ff1=m_w_ff1, w_ff2=m_w_ff2, g_ff_pre=m_g_ff_pre, g_ff_post=m_g_ff_post,
                w_ple_proj=m_w_ple_proj, w_ple_gate=m_w_ple_gate, g_ple_gate=m_g_ple_gate, g_ple_post=m_g_ple_post)
    mom2 = dict(w_in=v_w_in, b_gates=v_b_gates, w_qk_conv=v_w_qk_conv, g_mlstm=v_g_mlstm, w_pool=v_w_pool,
                pool_scale=v_pool_scale, w_out=v_w_out, g_mix_pre=v_g_mix_pre, g_mix_post=v_g_mix_post,
                w_ff1=v_w_ff1, w_ff2=v_w_ff2, g_ff_pre=v_g_ff_pre, g_ff_post=v_g_ff_post,
                w_ple_proj=v_w_ple_proj, w_ple_gate=v_w_ple_gate, g_ple_gate=v_g_ple_gate, g_ple_post=v_g_ple_post)
    order = list(weights)

    t, d = x.shape[1], x.shape[2]
    heads = MLSTM_HEADS
    wm = g_mlstm.shape[1]
    pw = pool_scale.shape[1]
    cg = pw // POOL_GROUPS
    in_cols = 4 * wm + 2 * heads + pw
    in_shard = w_in.shape[2]
    xs, ps, tgt = x[0], p[0, 0], loss_target[0]

    mx, my, mc = lax.axis_index("x"), lax.axis_index("y"), lax.axis_index("c")
    core = jnp.reshape(mc, (1,)).astype(jnp.int32)
    chip = jnp.reshape(2 * mx + my, (1,)).astype(jnp.int32)

    first = [w_in[0].T.astype(BF16), w_qk_conv[0], w_pool[0].reshape(POOL_GROUPS * w_pool.shape[2], cg).astype(BF16)]
    started, token = _gather_start([first], "gather_start_in")
    anchor = token[0, 0]
    later = lambda w: (w[0] + anchor).astype(BF16)
    rest, token = _gather_start([[later(w_out)], [later(w_ff1)], [later(w_ff2)], [later(w_ple_gate), later(w_ple_proj)]],
                                "gather_start_rest")
    started = started + rest
    bias = _pad_cols(b_gates, LANES)
    small = ["b_gates", "g_mlstm", "pool_scale", "g_mix_pre", "g_mix_post", "g_ff_pre", "g_ff_post", "g_ple_gate",
             "g_ple_post"]
    pack = lambda src: _pad_rows16(jnp.concatenate([_pad_cols(src[k] + anchor, d) for k in small], axis=0))
    small_state = [pack(weights), pack(mom1), pack(mom2)]
    in_state = [a[0].T + anchor for a in (w_in, m_w_in, v_w_in)]

    xn = _norm_fwd(xs, g_mix_pre, "norm_mix_pre", deps=[token])
    g_in, g_conv, g_pool = _forward_pair(
        _gather_wait(started[0], [xn] + small_state + in_state, "gather_wait_in"), "gather_pair_in")
    in_t_a = g_in.reshape(in_cols, d)
    in_t_b = jnp.concatenate([in_t_a[4 * wm + 2 * heads:], in_t_a[4 * wm:4 * wm + 2 * heads],
                              jnp.zeros((LANES - 2 * heads, d), BF16)], axis=0)
    conv_w = g_conv.transpose(1, 0, 2).reshape(QK_CONV, 2 * wm)
    pool_w = g_pool.reshape(N_DEV, POOL_GROUPS, cg // N_DEV, cg).transpose(1, 0, 2, 3).reshape(POOL_GROUPS, cg, cg)
    proj_a = _mm_nt("proj_qkvo", xn, in_t_a, n_rows=4 * wm)
    proj_b = _mm_nt("proj_pool_gates", xn, in_t_b)
    qc = _conv_fwd(proj_a, conv_w[:, :wm], 0, "q_conv")
    kc = _conv_fwd(proj_a, conv_w[:, wm:], wm, "k_conv")
    gb, gbt = _gates_fwd(proj_b, bias, pw // LANES, heads, "gates")
    gcols = _gate_columns(gb, heads, "gate_columns")
    gbt = gbt[:2 * heads].reshape(2 * heads, 1, t)
    qct, kct = qc.T.astype(BF16), (kc * (wm // heads) ** -0.5).T.astype(BF16)
    fwd_out = _forward_start(_gather_wait(started[1], gb, "gather_wait_out"), "gather_pair_start_out")
    hcat, *mlstm_states = _mlstm_fwd(qc, kc, kct, proj_a, gcols, gbt, g_mlstm, heads, d, "mlstm", fwd_out["token"])
    hcat = _pool_fwd(proj_b, pool_w, pool_scale, hcat, "pool")
    (g_out,) = _forward_wait(fwd_out, hcat, "gather_pair_wait_out")
    out_w = g_out.reshape(d, d)
    mix = _mm_nn("mix_out", hcat, out_w, tn=512)[0]
    fwd_ff1 = _forward_start(_gather_wait(started[2], mix, "gather_wait_ff1"), "gather_pair_start_ff1")
    h1, hn = _block_fwd(xs, mix, g_mix_post, g_ff_pre, "residual_mix", deps=[fwd_ff1["token"]])
    (g_ff1,) = _forward_wait(fwd_ff1, hn, "gather_pair_wait_ff1")
    relu_a, act = _mm_nn_pieces("ff1", hn, g_ff1, out_dtype=[BF16, BF16],
                                epilogue=lambda acc: (jnp.maximum(acc, 0.0), jnp.square(jnp.maximum(acc, 0.0))))
    (g_ff2,) = _forward_pair(_gather_wait(started[3], act, "gather_wait_ff2"), "gather_pair_ff2")
    ff2_w = g_ff2.reshape(-1, d)
    ff = _mm_nn("ff2", act, ff2_w, tn=256, tk=4 * d)[0]
    fwd_ple = _forward_start(_gather_wait(started[4], ff, "gather_wait_ple"), "gather_pair_start_ple")
    h2, hg = _block_fwd(h1, ff, g_ff_post, g_ple_gate, "residual_ff", deps=[fwd_ple["token"]])
    g_pgate, g_pproj = _forward_wait(fwd_ple, hg, "gather_pair_wait_ple")
    pgate_w = g_pgate.reshape(d, d)
    z = _mm_nn("ple_gate", hg, pgate_w, tn=512)[0]
    e = _mm_nn_pieces("ple_proj", ps, g_pproj)[0]
    dh3, dz, de, dg_ple_post, loss_part = _loss_head(h2, z, e, tgt, g_ple_post, "loss_head")
    loss = lax.psum(loss_part[0, 0], MESH_AXES)

    out = {}

    def state(k):
        if k == "w_in":
            return in_state
        return [a.reshape(-1, a.shape[-1]) for a in (weights[k], mom1[k], mom2[k])]

    def pair_begin(pieces, tag):
        keys = list(pieces)
        return keys, _pair_start([pieces[k] for k in keys], "reduce_pair_start_" + tag)

    def chip_begin(group, after, tag):
        keys, begun = group
        mine, got = _pair_wait(begun, after, "reduce_pair_wait_" + tag)
        sums = [_pair_sum(p_, g, core, "pair_sum_" + k) for k, p_, g in zip(keys, mine, got)]
        return keys, _chip_start(sums, "reduce_start_" + tag)

    def reduce_start(pieces, tag):
        keys = list(pieces)
        got = _pair_exchange([pieces[k] for k in keys], "reduce_pair_" + tag)
        sums = [_pair_sum(pieces[k], g, core, "pair_sum_" + k) for k, g in zip(keys, got)]
        return keys, _chip_start(sums, "reduce_start_" + tag)

    def reduce_finish(group, after, tag):
        keys, begun = group
        sums, got = _chip_wait(begun, after, "reduce_wait_" + tag)
        for k, s, g in zip(keys, sums, got):
            res = _adamw_shard(*state(k), s, g, chip, "adamw_" + k)
            out[k] = [r.T[None] if k == "w_in" else r.reshape(weights[k].shape) for r in res]

    token_of = lambda group: group[1]["token"]
    dhg = _mm_nt("d_ple_gate_in", dz, pgate_w, tn=512)
    gw_pgate = _mm_tn("gw_ple_gate", hg, dz, tn=512)
    gw_pproj = _mm_tn("gw_ple_proj", ps, de, pieces=N_DEV)
    pair_ple = pair_begin({"w_ple_gate": gw_pgate.reshape(N_DEV, d // N_DEV, d), "w_ple_proj": gw_pproj}, "ple")
    dh2, dff, dg_ff_post, dg_ple_gate = _block_bwd(h2, ff, dh3, dhg, g_ff_post, g_ple_gate, "d_residual_ff",
                                                   deps=[token_of(pair_ple)])
    red_ple = chip_begin(pair_ple, dff, "ple")
    da = _mm_nt("d_ff2_in", dff, ff2_w, out_dtype=BF16, extra=[relu_a],
                epilogue=lambda acc, r: (acc * (2.0 * r.astype(F32)),), deps=[token_of(red_ple)])
    gw_ff2 = _mm_tn("gw_ff2", act, dff)
    pair_ff2 = pair_begin({"w_ff2": gw_ff2.reshape(N_DEV, -1, d)}, "ff2")
    dhn = _mm_nt_pieces("d_ff1_in", da, g_ff1, deps=[token_of(pair_ff2)])
    place = jnp.concatenate([core, chip])
    mine_ff2, got_ff2 = _pair_wait(pair_ff2[1], dhn, "reduce_pair_wait_ff2")
    gw_ff1, sums_ff2 = _mm_tn("gw_ff1", hn, da, pieces=N_DEV, side=_side_pair_sum(mine_ff2[0], got_ff2[0], place))
    red_ff2 = (["w_ff2"], _chip_start([sums_ff2], "reduce_start_ff2"))
    reduce_finish(red_ple, token_of(red_ff2), "ple")
    pair_ff1 = pair_begin({"w_ff1": gw_ff1}, "ff1")
    dh1, dmix, dg_mix_post, dg_ff_pre = _block_bwd(h1, mix, dh2, dhn, g_mix_post, g_ff_pre, "d_residual_mix",
                                                   deps=[token_of(pair_ff1), token_of(red_ff2)])
    mine_ff1, got_ff1 = _pair_wait(pair_ff1[1], dmix, "reduce_pair_wait_ff1")
    dhcat, sums_ff1 = _mm_nt("d_mix_in", dmix, out_w, tm=t // 2, tn=d // 2,
                             side=_side_pair_sum(mine_ff1[0], got_ff1[0], place))
    red_ff1 = (["w_ff1"], _chip_start([sums_ff1], "reduce_start_ff1"))
    gw_out = _mm_tn("gw_out", hcat, dmix, tn=512, deps=[token_of(red_ff1)])
    du, gw_pool, dg_pool_scale = _pool_bwd(proj_b, pool_w, pool_scale, dhcat, "d_pool")
    pair_out = pair_begin({
        "w_out": gw_out.reshape(N_DEV, d // N_DEV, d),
        "w_pool": gw_pool.reshape(POOL_GROUPS, N_DEV, cg // N_DEV, cg).transpose(1, 0, 2, 3).reshape(N_DEV, -1, cg)},
        "out")
    dq, dk, dv, do, dcol, drow, dg_mlstm = _mlstm_bwd(qc, kc, qct, kct, proj_a, gcols, gbt, g_mlstm, mlstm_states, dhcat,
                                                      heads, "d_mlstm", token_of(pair_out))
    red_out = chip_begin(pair_out, dq, "out")
    drow = jnp.pad(drow.reshape(heads, t), ((0, LANES - heads), (0, 0)))
    dgates, dbias = _gates_bwd(proj_b, bias, dcol, drow, pw // LANES, heads, "d_gates")
    dq_pre, gw_conv_q = _conv_bwd(proj_a, conv_w[:, :wm], dq, 0, "d_q_conv")
    dk_pre, gw_conv_k = _conv_bwd(proj_a, conv_w[:, wm:], dk, wm, "d_k_conv")
    gw_conv = jnp.concatenate([gw_conv_q, gw_conv_k], axis=1)
    dproj_a = jnp.concatenate([dq_pre, dk_pre, dv, do], axis=1)
    dproj_b = jnp.concatenate([du, dgates], axis=1)
    sums_ff2, got_ff2 = _chip_wait(red_ff2[1], token_of(red_out), "reduce_wait_ff2")
    gw_a, *update = _mm_tn("gw_in_qkvo", dproj_a, xn,
                           side=_side_adamw(*state("w_ff2"), sums_ff2[0], got_ff2[0], place))
    out["w_ff2"] = [r.reshape(weights["w_ff2"].shape) for r in update]
    gw_b = _mm_tn("gw_in_pool_gates", dproj_b, xn)
    gw_in_t = jnp.concatenate([gw_a, gw_b[pw:pw + 2 * heads], gw_b[:pw]], axis=0)
    red_in = reduce_start({
        "w_in": gw_in_t.reshape(N_DEV, in_shard, d),
        "w_qk_conv": gw_conv.reshape(QK_CONV, N_DEV, -1).transpose(1, 0, 2).astype(BF16)}, "in")
    reduce_finish(red_out, red_in[1]["token"], "out")
    sums_ff1, got_ff1 = _chip_wait(red_ff1[1], token_of(red_in), "reduce_wait_ff1")
    dxn_a, *update = _mm_nn("d_proj_qkvo_in", dproj_a, in_t_a, tn=512, tk=4 * wm, deps=[token_of(red_in)],
                            side=_side_adamw(*state("w_ff1"), sums_ff1[0], got_ff1[0], place))
    out["w_ff1"] = [r.reshape(weights["w_ff1"].shape) for r in update]
    dxn = _mm_nn("d_proj_in", dproj_b, in_t_b, extra=[dxn_a], epilogue=lambda acc, prev: (acc + prev,))[0]
    grad_x, dg_mix_pre = _norm_bwd(xs, dxn, dh1, g_mix_pre, "d_norm_mix_pre")

    small_grads = dict(b_gates=dbias[:, :2 * heads], g_mlstm=dg_mlstm, pool_scale=dg_pool_scale, g_mix_pre=dg_mix_pre,
                       g_mix_post=dg_mix_post, g_ff_pre=dg_ff_pre, g_ff_post=dg_ff_post, g_ple_gate=dg_ple_gate,
                       g_ple_post=dg_ple_post)
    small_begun = _everyone_start([pack(small_grads)], "small_grads_start")
    reduce_finish(red_in, small_begun["token"], "in")
    (small_parts,) = _everyone_wait(small_begun, out["w_in"][0], "small_grads_wait")
    res = _adamw_replicated(*small_state, small_parts, "adamw_replicated")
    for i, k in enumerate(small):
        width = weights[k].shape[1]
        out[k] = [r[i:i + 1, :width] for r in res]

    return (loss, grad_x[None], *[out[k][0] for k in order], *[out[k][1] for k in order],
            *[out[k][2] for k in order], *[out[k][3] for k in order])


def _pad_rows16(a):
    return jnp.pad(a, ((0, 16 - a.shape[0]), (0, 0)))
```

```python
import jax
import jax.numpy as jnp
from jax import lax
from jax.experimental import pallas as pl
from jax.experimental.pallas import tpu as pltpu

F32 = jnp.float32
BF16 = jnp.bfloat16
EPS = 1e-6
N_DEV = 8
N_CHIP = 4
LANES = 128
VMEM_LIMIT = 56 * 1024 * 1024
MLSTM_HEADS = 8
POOL_GROUPS = 4
QK_CONV = 4
CHUNK = 128
ROW_TILE = 256
ADAM_LR, ADAM_B1, ADAM_B2, ADAM_EPS, ADAM_WD, ADAM_STEP = 0.001, 0.9, 0.999, 1e-08, 0.01, 10
MESH_AXES = ("x", "y", "c")
MESH = pl.DeviceIdType.MESH
HBM_SPEC = pl.BlockSpec(memory_space=pltpu.HBM)
SEM_SPEC = pl.BlockSpec(memory_space=pltpu.SEMAPHORE)
DATAFLOW = pltpu.SideEffectType.DATAFLOW_SIDE_EFFECTING


def _params(*sem):
    if sem:
        return pltpu.CompilerParams(dimension_semantics=sem, vmem_limit_bytes=VMEM_LIMIT)
    return pltpu.CompilerParams(vmem_limit_bytes=VMEM_LIMIT)


def _tile(n, pref):
    if n <= pref:
        return n
    t = (pref // LANES) * LANES
    while t >= LANES:
        if n % t == 0:
            return t
        t -= LANES
    return n


def _hbm(a):
    return pltpu.with_memory_space_constraint(a, pltpu.HBM)


def _mesh_place():
    x, y, c = lax.axis_index("x"), lax.axis_index("y"), lax.axis_index("c")
    return x, y, c, [(1 - x, y), (x, 1 - y), (1 - x, 1 - y)]


def _chip_copies(srcs, lands, send, recv):
    x, y, c, chips = _mesh_place()
    return [pltpu.make_async_remote_copy(
        src_ref=srcs[w].at[2 * chip[0] + chip[1]], dst_ref=lands[w].at[k],
        send_sem=send.at[3 * w + k], recv_sem=recv.at[3 * w + k],
        device_id=(*chip, c), device_id_type=MESH)
        for k, chip in enumerate(chips) for w in range(len(srcs))]


def _chip_start(sums, name):
    return _split_start(_chip_copies, sums, [lax.empty((3,) + a.shape[1:], a.dtype) for a in sums], 3, name)


def _chip_wait(started, after, name):
    return _split_wait(_chip_copies, started, after, name)


def _pair_start(pieces, name):
    return _split_start(_pair_copies, pieces, [lax.empty((N_CHIP,) + a.shape[1:], a.dtype) for a in pieces], N_CHIP, name)


def _pair_wait(started, after, name):
    return _split_wait(_pair_copies, started, after, name)


def _pair_copies(srcs, lands, send, recv):
    x, y, c, _ = _mesh_place()
    return [pltpu.make_async_remote_copy(
        src_ref=srcs[w].at[2 * ch + 1 - c], dst_ref=lands[w].at[ch],
        send_sem=send.at[N_CHIP * w + ch], recv_sem=recv.at[N_CHIP * w + ch],
        device_id=(x, y, 1 - c), device_id_type=MESH)
        for w in range(len(srcs)) for ch in range(N_CHIP)]


def _split_start(copies_of, srcs, lands, per_array, name):
    ns, nb = len(srcs), len(srcs) + len(lands)
    n_copies = per_array * len(lands)

    def body(*refs):
        for cp in copies_of(refs[:ns], refs[ns:nb], refs[nb], refs[nb + 1]):
            cp.start()
        refs[-1][...] = jnp.zeros_like(refs[-1])

    res = pl.pallas_call(
        body, name=name,
        out_shape=[pltpu.SemaphoreType.DMA((n_copies,)), pltpu.SemaphoreType.DMA((n_copies,))]
        + [pltpu.HBM(a.shape, a.dtype) for a in list(srcs) + list(lands)] + [jax.ShapeDtypeStruct((8, LANES), F32)],
        in_specs=[HBM_SPEC] * nb,
        out_specs=[SEM_SPEC, SEM_SPEC] + [HBM_SPEC] * nb + [pl.BlockSpec(memory_space=pltpu.VMEM)],
        input_output_aliases={i: 2 + i for i in range(nb)},
        compiler_params=pltpu.CompilerParams(has_side_effects=DATAFLOW),
    )(*[_hbm(a) for a in srcs], *[_hbm(a) for a in lands])
    return dict(send=res[0], recv=res[1], srcs=list(res[2:2 + ns]), lands=list(res[2 + ns:2 + nb]), token=res[-1])


def _split_wait(copies_of, started, after, name):
    ns = len(started["srcs"])
    nb = ns + len(started["lands"])

    def body(*refs):
        for cp in copies_of(refs[:ns], refs[ns:nb], refs[nb], refs[nb + 1]):
            cp.wait_send()
            cp.wait_recv()

    res = pl.pallas_call(
        body, name=name,
        out_shape=[pltpu.HBM(a.shape, a.dtype) for a in started["srcs"] + started["lands"]],
        in_specs=[HBM_SPEC] * nb + [SEM_SPEC, SEM_SPEC, ANY_SPEC],
        out_specs=[HBM_SPEC] * nb,
        input_output_aliases={i: i for i in range(nb)},
        compiler_params=pltpu.CompilerParams(has_side_effects=DATAFLOW),
    )(*started["srcs"], *started["lands"], started["send"], started["recv"], after)
    return list(res[:ns]), list(res[ns:])


def _forward_copies(srcs, lands, send, recv):
    del srcs
    x, y, c, chips = _mesh_place()
    copies = []
    for w in range(len(lands)):
        for j, chip in enumerate(chips):
            rows = lands[w].at[4 * chip[0] + 2 * chip[1] + c]
            copies.append(pltpu.make_async_remote_copy(
                src_ref=rows, dst_ref=rows, send_sem=send.at[3 * w + j], recv_sem=recv.at[3 * w + j],
                device_id=(x, y, 1 - c), device_id_type=MESH))
    return copies


def _forward_start(lands, name):
    return _split_start(_forward_copies, [], lands, 3, name)


def _forward_wait(started, after, name):
    return _split_wait(_forward_copies, started, after, name)[1]


def _everyone_copies(srcs, lands, send, recv):
    x, y, c, _ = _mesh_place()
    me = 4 * x + 2 * y + c
    copies = []
    for w in range(len(srcs)):
        for k in range(N_DEV - 1):
            flip = k + 1
            peer = (1 - x if flip & 4 else x, 1 - y if flip & 2 else y, 1 - c if flip & 1 else c)
            copies.append(pltpu.make_async_remote_copy(
                src_ref=srcs[w], dst_ref=lands[w].at[me], send_sem=send.at[7 * w + k], recv_sem=recv.at[7 * w + k],
                device_id=peer, device_id_type=MESH))
    return copies


def _everyone_start(blocks, name):
    me = 4 * lax.axis_index("x") + 2 * lax.axis_index("y") + lax.axis_index("c")
    lands = [lax.dynamic_update_slice(lax.empty((N_DEV,) + a.shape, a.dtype), a[None], (me,) + (0,) * a.ndim)
             for a in blocks]
    return _split_start(_everyone_copies, blocks, lands, N_DEV - 1, name)


def _everyone_wait(started, after, name):
    return _split_wait(_everyone_copies, started, after, name)[1]


def _gather_copies(srcs, lands, send, recv):
    x, y, c, chips = _mesh_place()
    me = 4 * x + 2 * y + c
    targets = [(x, y, 1 - c)] + [(*chip, c) for chip in chips]
    return [pltpu.make_async_remote_copy(
        src_ref=srcs[w], dst_ref=lands[w].at[me], send_sem=send.at[4 * w + k], recv_sem=recv.at[4 * w + k],
        device_id=to, device_id_type=MESH)
        for w in range(len(srcs)) for k, to in enumerate(targets)]


def _gather_start(groups, name):
    sizes = [len(g) for g in groups]
    flat = [a for g in groups for a in g]
    n, ng = len(flat), len(groups)
    me = 4 * lax.axis_index("x") + 2 * lax.axis_index("y") + lax.axis_index("c")
    lands = [lax.dynamic_update_slice(lax.empty((N_DEV,) + a.shape, a.dtype), a[None], (me,) + (0,) * a.ndim)
             for a in flat]

    def body(*refs):
        srcs, zones = refs[:n], refs[n:2 * n]
        sems = refs[2 * n:2 * n + 2 * ng]
        token = refs[-1]
        lo = 0
        for gi, size in enumerate(sizes):
            for cp in _gather_copies(srcs[lo:lo + size], zones[lo:lo + size], sems[2 * gi], sems[2 * gi + 1]):
                cp.start()
            lo += size
        token[...] = jnp.zeros_like(token)

    sem_shapes = []
    for size in sizes:
        sem_shapes += [pltpu.SemaphoreType.DMA((4 * size,))] * 2
    res = pl.pallas_call(
        body, name=name,
        out_shape=sem_shapes + [pltpu.HBM(a.shape, a.dtype) for a in flat + lands] + [jax.ShapeDtypeStruct((8, LANES), F32)],
        in_specs=[HBM_SPEC] * (2 * n),
        out_specs=[SEM_SPEC] * (2 * ng) + [HBM_SPEC] * (2 * n) + [pl.BlockSpec(memory_space=pltpu.VMEM)],
        input_output_aliases={i: 2 * ng + i for i in range(2 * n)},
        compiler_params=pltpu.CompilerParams(has_side_effects=DATAFLOW),
    )(*[_hbm(a) for a in flat], *[_hbm(a) for a in lands])
    out, lo = [], 0
    for gi, size in enumerate(sizes):
        out.append(dict(send=res[2 * gi], recv=res[2 * gi + 1],
                        srcs=list(res[2 * ng + lo:2 * ng + lo + size]),
                        lands=list(res[2 * ng + n + lo:2 * ng + n + lo + size])))
        lo += size
    return out, res[-1]


def _gather_wait(started, after, name):
    n = len(started["srcs"])
    after = list(after) if isinstance(after, (list, tuple)) else [after]

    def body(*refs):
        srcs, zones = refs[:n], refs[n:2 * n]
        send, recv = refs[2 * n], refs[2 * n + 1]
        for cp in _gather_copies(srcs, zones, send, recv):
            cp.wait_send()
            cp.wait_recv()

    res = pl.pallas_call(
        body, name=name,
        out_shape=[pltpu.HBM(a.shape, a.dtype) for a in started["srcs"] + started["lands"]],
        in_specs=[HBM_SPEC] * (2 * n) + [SEM_SPEC, SEM_SPEC] + [ANY_SPEC] * len(after),
        out_specs=[HBM_SPEC] * (2 * n),
        input_output_aliases={i: i for i in range(2 * n)},
        compiler_params=pltpu.CompilerParams(has_side_effects=DATAFLOW),
    )(*started["srcs"], *started["lands"], started["send"], started["recv"], *after)
    return list(res[n:])


def _forward_pair(lands, name):
    n = len(lands)

    def body(*refs):
        zones = refs[:n]
        send, recv = refs[2 * n:]
        x, y, c, chips = _mesh_place()

        def copy(w, j, core):
            rows = zones[w].at[4 * chips[j][0] + 2 * chips[j][1] + core]
            return pltpu.make_async_remote_copy(
                src_ref=rows, dst_ref=rows, send_sem=send.at[3 * w + j], recv_sem=recv.at[3 * w + j],
                device_id=(x, y, 1 - c), device_id_type=MESH)

        sends = [copy(w, j, c) for w in range(n) for j in range(3)]
        for cp in sends:
            cp.start()
        for w in range(n):
            for j in range(3):
                copy(w, j, 1 - c).wait_recv()
        for cp in sends:
            cp.wait_send()

    return list(pl.pallas_call(
        body, name=name,
        out_shape=[jax.ShapeDtypeStruct(a.shape, a.dtype) for a in lands],
        in_specs=[HBM_SPEC] * n, out_specs=[HBM_SPEC] * n,
        input_output_aliases={i: i for i in range(n)},
        scratch_shapes=[pltpu.SemaphoreType.DMA((3 * n,)), pltpu.SemaphoreType.DMA((3 * n,))],
    )(*lands))


def _pair_sum(pieces, got, core, name):
    _, rows, cols = pieces.shape
    tr, tc = _tile_2d(rows, cols)

    def body(core_ref, mine_ref, got_ref, out_ref):
        del core_ref
        out_ref[...] = (mine_ref[...].astype(F32) + got_ref[...].astype(F32)).astype(out_ref.dtype)

    return pl.pallas_call(
        body, name=name,
        out_shape=jax.ShapeDtypeStruct((N_CHIP, rows, cols), pieces.dtype),
        grid_spec=pltpu.PrefetchScalarGridSpec(
            num_scalar_prefetch=1, grid=(N_CHIP, rows // tr, cols // tc),
            in_specs=[pl.BlockSpec((None, tr, tc), lambda ch, i, j, core: (2 * ch + core[0], i, j)),
                      pl.BlockSpec((None, tr, tc), lambda ch, i, j, core: (ch, i, j))],
            out_specs=pl.BlockSpec((None, tr, tc), lambda ch, i, j, core: (ch, i, j))),
        compiler_params=_params("parallel", "parallel", "parallel"),
    )(core, pieces, got)


def _tile_2d(rows, cols, budget=2 * 1024 * 1024):
    want = max(16, budget // (4 * cols))
    if rows <= want:
        return rows, cols
    t = (want // 16) * 16
    while t >= 16:
        if rows % t == 0:
            return t, cols
        t -= 16
    return rows, _tile(cols, max(LANES, budget // (4 * rows)))


def _adamw_math(w, g, m, v):
    m = ADAM_B1 * m + (1.0 - ADAM_B1) * g
    v = ADAM_B2 * v + (1.0 - ADAM_B2) * (g * g)
    m_hat = m / (1.0 - ADAM_B1 ** ADAM_STEP)
    v_hat = v / (1.0 - ADAM_B2 ** ADAM_STEP)
    delta = -ADAM_LR * (m_hat / (jnp.sqrt(v_hat) + ADAM_EPS) + ADAM_WD * w)
    return delta, m, v


def _adamw_shard(w, m, v, sums, got, chip, name):
    rows, cols = w.shape
    tr, tc = _tile_2d(rows, cols, budget=1024 * 1024)

    def body(chip_ref, w_ref, m_ref, v_ref, own_ref, got_ref, g_out, d_out, m_out, v_out):
        del chip_ref
        g = own_ref[...].astype(F32)
        for k in range(3):
            g = g + got_ref[k].astype(F32)
        delta, m_new, v_new = _adamw_math(w_ref[...], g, m_ref[...], v_ref[...])
        g_out[...] = g
        d_out[...] = delta
        m_out[...] = m_new
        v_out[...] = v_new

    blk = pl.BlockSpec((tr, tc), lambda i, j, chip: (i, j))
    return pl.pallas_call(
        body, name=name,
        out_shape=[jax.ShapeDtypeStruct((rows, cols), F32)] * 4,
        grid_spec=pltpu.PrefetchScalarGridSpec(
            num_scalar_prefetch=1, grid=(rows // tr, cols // tc),
            in_specs=[blk, blk, blk,
                      pl.BlockSpec((None, tr, tc), lambda i, j, chip: (chip[0], i, j)),
                      pl.BlockSpec((3, tr, tc), lambda i, j, chip: (0, i, j))],
            out_specs=[blk] * 4),
        compiler_params=_params("parallel", "parallel"),
    )(chip, w, m, v, sums, got)


def _adamw_replicated(w, m, v, parts, name):
    rows, cols = w.shape

    def body(w_ref, m_ref, v_ref, parts_ref, g_out, d_out, m_out, v_out):
        g = parts_ref[0]
        for d in range(1, N_DEV):
            g = g + parts_ref[d]
        delta, m_new, v_new = _adamw_math(w_ref[...], g, m_ref[...], v_ref[...])
        g_out[...] = g
        d_out[...] = delta
        m_out[...] = m_new
        v_out[...] = v_new

    return pl.pallas_call(
        body, name=name,
        out_shape=[jax.ShapeDtypeStruct((rows, cols), F32)] * 4,
        compiler_params=_params(),
    )(w, m, v, parts)


NN = ((1,), (0,))
NT = ((1,), (1,))
TN = ((0,), (0,))


ANY_SPEC = pl.BlockSpec(memory_space=pl.ANY)


def _mm(name, a, b, *, dims, grid, a_spec, b_spec, outs, extra=(), extra_specs=(), epilogue=None, acc_shape=None,
        deps=(), b_pieces=0, side=None):
    nk = grid[2]
    n_extra, n_out = len(extra), len(outs)
    side_ins = side["ins"] if side else []
    side_outs = side["outs"] if side else []
    first_side = 2 + n_extra + len(deps)
    first_out = first_side + len(side_ins)
    assert not side or nk == 1

    assert not side or (epilogue is None and n_out == 1 and not b_pieces)
    SIDE_SLICES = 4

    def body(*refs):
        dot = lambda lhs, rhs: lax.dot_general(lhs.astype(BF16), rhs.astype(BF16), (dims, ((), ())),
                                               preferred_element_type=F32)
        if side:
            refs = refs[1:]
            a_ref, b_ref, out_ref = refs[0], refs[1], refs[first_out]
            ins, sides = refs[first_side:first_out], refs[first_out + 1:first_out + 1 + len(side_outs)]
            rows = out_ref.shape[0] // SIDE_SLICES
            srows = sides[0].shape[-2] // SIDE_SLICES
            for part in range(SIDE_SLICES):
                rs = slice(part * srows, (part + 1) * srows)
                for val, o in zip(side["fn"](*[r[..., rs, :] for r in ins]), sides):
                    o[rs, :] = val.astype(o.dtype)
                ms = slice(part * rows, (part + 1) * rows)
                lhs = a_ref[:, ms] if dims == TN else a_ref[ms, :]
                out_ref[ms, :] = dot(lhs, b_ref[...]).astype(out_ref.dtype)
            return
        a_ref, b_ref = refs[0], refs[1]
        ex = refs[2:2 + n_extra]
        out_refs = refs[first_out:first_out + n_out]
        if b_pieces:
            ks = b_ref.shape[-1]
            part = dot(a_ref[:, 0:ks], b_ref[0])
            for piece in range(1, b_pieces):
                part = part + dot(a_ref[:, piece * ks:(piece + 1) * ks], b_ref[piece])
        else:
            part = dot(a_ref[...], b_ref[...])

        def finish(acc):
            res = epilogue(acc, *[e[...] for e in ex]) if epilogue else (acc,)
            for val, o in zip(res, out_refs):
                o[...] = val.astype(o.dtype)

        if nk == 1:
            finish(part)
        else:
            acc_ref = refs[-1]
            k = pl.program_id(2)

            @pl.when(k == 0)
            def _():
                acc_ref[...] = part

            @pl.when(k > 0)
            def _():
                acc_ref[...] += part

            @pl.when(k == nk - 1)
            def _():
                finish(acc_ref[...])

    scratch = [pltpu.VMEM(acc_shape, F32)] if nk > 1 else []
    out_shape = [jax.ShapeDtypeStruct(s, d) for s, d, _ in outs]
    in_specs = [a_spec, b_spec, *extra_specs] + [ANY_SPEC] * len(deps)
    out_specs = [sp for _, _, sp in outs]
    if not side:
        return list(pl.pallas_call(
            body, name=name, out_shape=out_shape, grid=grid, in_specs=in_specs, out_specs=out_specs,
            scratch_shapes=scratch, compiler_params=_params("parallel", "parallel", "arbitrary"),
        )(a, b, *extra, *deps))

    def with_scalars(spec):
        if spec.index_map is None:
            return spec
        return pl.BlockSpec(spec.block_shape, lambda i, j, k, s, f=spec.index_map: f(i, j, k))

    def by_step(block, index_map):
        return pl.BlockSpec(block, lambda i, j, k, s: index_map((i * grid[1] + j) * grid[2] + k, s))

    return list(pl.pallas_call(
        body, name=name,
        out_shape=out_shape + [jax.ShapeDtypeStruct(s, d) for s, d, _, _ in side_outs],
        grid_spec=pltpu.PrefetchScalarGridSpec(
            num_scalar_prefetch=1, grid=grid,
            in_specs=[with_scalars(sp) for sp in in_specs] + [by_step(blk, im) for _, blk, im in side_ins],
            out_specs=[with_scalars(sp) for sp in out_specs] + [by_step(blk, im) for _, _, blk, im in side_outs],
            scratch_shapes=scratch),
        compiler_params=_params("arbitrary", "arbitrary", "arbitrary"),
    )(side["scalars"], a, b, *extra, *deps, *[arr for arr, _, _ in side_ins]))


def _side_pair_sum(pieces, got, place):
    _, rows, cols = pieces.shape

    def build(steps):
        per_chip = steps // N_CHIP
        tr = rows // per_chip
        assert per_chip * N_CHIP == steps and tr * per_chip == rows and tr % 16 == 0
        blk = (None, tr, cols)
        here = lambda st, s: (st // per_chip, st % per_chip, 0)
        return dict(scalars=place,
                    ins=[(pieces, blk, lambda st, s: (2 * (st // per_chip) + s[0], st % per_chip, 0)), (got, blk, here)],
                    outs=[((N_CHIP, rows, cols), pieces.dtype, blk, here)],
                    fn=lambda mine, other: (mine.astype(F32) + other.astype(F32),))
    return build


def _side_adamw(w, m, v, sums, got, place):
    rows, cols = w.shape

    def build(steps):
        tr = rows // steps
        assert tr * steps == rows and tr % 16 == 0
        blk, here = (tr, cols), (lambda st, s: (st, 0))

        def fn(w, m, v, own, got):
            g = own.astype(F32)
            for k in range(3):
                g = g + got[k].astype(F32)
            delta, m_new, v_new = _adamw_math(w, g, m, v)
            return g, delta, m_new, v_new

        return dict(scalars=place,
                    ins=[(w, blk, here), (m, blk, here), (v, blk, here),
                         (sums, (None, tr, cols), lambda st, s: (s[1], st, 0)),
                         (got, (3, tr, cols), lambda st, s: (0, st, 0))],
                    outs=[((rows, cols), F32, blk, here)] * 4, fn=fn)
    return build


def _mm_nn(name, a, b, out_dtype=F32, tm=1024, tn=1024, tk=2048, epilogue=None, n_out=1, extra=(), deps=(), side=None):
    m, kd = a.shape
    n = b.shape[1]
    tm, tn, tk = _tile(m, tm), _tile(n, tn), _tile(kd, tk)
    dts = out_dtype if isinstance(out_dtype, (list, tuple)) else [out_dtype] * n_out
    o_spec = pl.BlockSpec((tm, tn), lambda i, j, k: (i, j))
    grid = (m // tm, n // tn, kd // tk)
    return _mm(name, a, b, dims=NN, grid=grid,
               a_spec=pl.BlockSpec((tm, tk), lambda i, j, k: (i, k)),
               b_spec=pl.BlockSpec((tk, tn), lambda i, j, k: (k, j)),
               outs=[((m, n), dt, o_spec) for dt in dts], extra=extra, extra_specs=[o_spec] * len(extra),
               epilogue=epilogue, acc_shape=(tm, tn), deps=deps, side=side and side(grid[0] * grid[1] * grid[2]))


def _mm_nn_pieces(name, a, b, out_dtype=F32, tm=1024, tk=2048, epilogue=None, n_out=1):
    m, kd = a.shape
    npc, _, ns = b.shape
    tm, tk = _tile(m, tm), _tile(kd, tk)
    tn = _tile(ns, 1024)
    r = ns // tn
    dts = out_dtype if isinstance(out_dtype, (list, tuple)) else [out_dtype] * n_out
    o_spec = pl.BlockSpec((tm, tn), lambda i, j, k: (i, j))
    return _mm(name, a, b, dims=NN, grid=(m // tm, npc * r, kd // tk),
               a_spec=pl.BlockSpec((tm, tk), lambda i, j, k: (i, k)),
               b_spec=pl.BlockSpec((None, tk, tn), lambda i, j, k: (j // r, k, j % r)),
               outs=[((m, npc * ns), dt, o_spec) for dt in dts], epilogue=epilogue, acc_shape=(tm, tn))


def _mm_nt(name, a, b, out_dtype=F32, tm=1024, tn=1024, tk=2048, epilogue=None, extra=(), n_rows=None, deps=(),
           side=None):
    m, kd = a.shape
    n = n_rows or b.shape[0]
    tm, tn, tk = _tile(m, tm), _tile(n, tn), _tile(kd, tk)
    o_spec = pl.BlockSpec((tm, tn), lambda i, j, k: (i, j))
    grid = (m // tm, n // tn, kd // tk)
    res = _mm(name, a, b, dims=NT, grid=grid,
              a_spec=pl.BlockSpec((tm, tk), lambda i, j, k: (i, k)),
              b_spec=pl.BlockSpec((tn, tk), lambda i, j, k: (j, k)),
              outs=[((m, n), out_dtype, o_spec)], extra=extra, extra_specs=[o_spec] * len(extra),
              epilogue=epilogue, acc_shape=(tm, tn), deps=deps, side=side and side(grid[0] * grid[1] * grid[2]))
    return res if side else res[0]


def _mm_nt_pieces(name, a, b, out_dtype=F32, tm=1024, tn=256, deps=()):
    m = a.shape[0]
    npc, n, ks = b.shape
    tm, tn = _tile(m, tm), _tile(n, tn)
    return _mm(name, a, b, dims=NT, grid=(m // tm, n // tn, 1),
               a_spec=pl.BlockSpec((tm, npc * ks), lambda i, j, k: (i, 0)),
               b_spec=pl.BlockSpec((npc, tn, ks), lambda i, j, k: (0, j, 0)),
               outs=[((m, n), out_dtype, pl.BlockSpec((tm, tn), lambda i, j, k: (i, j)))],
               deps=deps, b_pieces=npc)[0]


def _mm_tn(name, a, b, out_dtype=BF16, tm=1024, tn=1024, pieces=0, deps=(), side=None):
    t, m = a.shape
    n = b.shape[1]
    tm = _tile(m, tm)
    if pieces:
        ns = n // pieces
        tn = _tile(ns, tn)
        r = ns // tn
        out = ((pieces, m, ns), out_dtype, pl.BlockSpec((None, tm, tn), lambda i, j, k: (j // r, i, j % r)))
    else:
        tn = _tile(n, tn)
        out = ((m, n), out_dtype, pl.BlockSpec((tm, tn), lambda i, j, k: (i, j)))
    grid = (m // tm, n // tn, 1)
    res = _mm(name, a, b, dims=TN, grid=grid,
              a_spec=pl.BlockSpec((t, tm), lambda i, j, k: (0, i)),
              b_spec=pl.BlockSpec((t, tn), lambda i, j, k: (0, j)),
              outs=[out], deps=deps, side=side and side(grid[0] * grid[1]))
    return res if side else res[0]


def _rms_f(x, g):
    r = lax.rsqrt(jnp.mean(x * x, axis=-1, keepdims=True) + EPS)
    xh = x * r
    return xh * g, xh, r


def _rms_b(dy, g, xh, r):
    dxh = dy * g
    dx = r * (dxh - xh * jnp.mean(dxh * xh, axis=-1, keepdims=True))
    dg = jnp.sum(dy * xh, axis=0, keepdims=True)
    return dx, dg


def _rowwise(name, fn, row_ins, vec_ins, row_outs, vec_outs, deps=()):
    t = row_ins[0].shape[0]
    tr = min(ROW_TILE, t)
    nr, nv, no = len(row_ins), len(vec_ins), len(row_outs)
    first_out = nr + nv + len(deps)

    def body(*refs):
        ri, vi = refs[:nr], refs[nr:nr + nv]
        ro, vo = refs[first_out:first_out + no], refs[first_out + no:]
        routs, vouts = fn(*[r[...] for r in ri], *[v[...] for v in vi])
        for o, val in zip(ro, routs):
            o[...] = val.astype(o.dtype)
        i = pl.program_id(0)
        for o, val in zip(vo, vouts):
            @pl.when(i == 0)
            def _():
                o[...] = val

            @pl.when(i > 0)
            def _():
                o[...] += val

    res = pl.pallas_call(
        body, name=name,
        out_shape=[jax.ShapeDtypeStruct((t, w), d) for w, d in row_outs]
        + [jax.ShapeDtypeStruct((1, w), F32) for w in vec_outs],
        grid=(t // tr,),
        in_specs=[pl.BlockSpec((tr, a.shape[1]), lambda i: (i, 0)) for a in row_ins]
        + [pl.BlockSpec((1, a.shape[1]), lambda i: (0, 0)) for a in vec_ins] + [ANY_SPEC] * len(deps),
        out_specs=[pl.BlockSpec((tr, w), lambda i: (i, 0)) for w, _ in row_outs]
        + [pl.BlockSpec((1, w), lambda i: (0, 0)) for w in vec_outs],
        compiler_params=_params("arbitrary"),
    )(*row_ins, *vec_ins, *deps)
    return list(res)


def _norm_fwd(x, g, name, deps=()):
    d = x.shape[1]
    return _rowwise(name, lambda x, g: ((_rms_f(x, g)[0],), ()), [x], [g], [(d, BF16)], [], deps=deps)[0]


def _norm_bwd(x, dxn, dres, g, name):
    d = x.shape[1]

    def fn(x, dxn, dres, g):
        _, xh, r = _rms_f(x, g)
        dx, dg = _rms_b(dxn, g, xh, r)
        return (dres + dx,), (dg,)

    return _rowwise(name, fn, [x, dxn, dres], [g], [(d, F32)], [d])


def _block_fwd(h_prev, y, g_y, g_n, name, deps=()):
    d = h_prev.shape[1]

    def fn(h_prev, y, g_y, g_n):
        h = h_prev + _rms_f(y, g_y)[0]
        return (h, _rms_f(h, g_n)[0]), ()

    return _rowwise(name, fn, [h_prev, y], [g_y, g_n], [(d, F32), (d, BF16)], [], deps=deps)


def _block_bwd(h, y, d_up, d_n, g_y, g_n, name, deps=()):
    d = h.shape[1]

    def fn(h, y, d_up, d_n, g_y, g_n):
        _, hh, hr = _rms_f(h, g_n)
        dh_n, dg_n = _rms_b(d_n, g_n, hh, hr)
        dh = d_up + dh_n
        _, yh, yr = _rms_f(y, g_y)
        dy, dg_y = _rms_b(dh, g_y, yh, yr)
        return (dh, dy), (dg_y, dg_n)

    return _rowwise(name, fn, [h, y, d_up, d_n], [g_y, g_n], [(d, F32), (d, BF16)], [d, d], deps=deps)


def _loss_head(h2, z, e, target, g, name):
    d = h2.shape[1]

    def fn(h2, z, e, target, g):
        gate = jax.nn.sigmoid(z)
        y, xh, r = _rms_f(e * gate, g)
        diff = h2 + y - target
        loss = 0.5 * jnp.sum(jnp.mean(diff * diff, axis=-1, keepdims=True), axis=0, keepdims=True)
        dh3 = diff / d
        deg, dg = _rms_b(dh3, g, xh, r)
        dz = deg * e * gate * (1.0 - gate)
        de = deg * gate
        return (dh3, dz, de), (dg, jnp.broadcast_to(loss, (1, LANES)))

    return _rowwise(name, fn, [h2, z, e, target], [g], [(d, F32), (d, BF16), (d, BF16)], [d, LANES])


def _shift_down(x, s, rows):
    return jnp.where(rows >= s, pltpu.roll(x, s, axis=0), 0.0)


def _shift_up(x, s, rows):
    t = x.shape[0]
    return jnp.where(rows < t - s, pltpu.roll(x, t - s, axis=0), 0.0)


def _conv_pre(x, w, rows):
    pre = x * w[QK_CONV - 1:QK_CONV, :]
    for s in range(1, QK_CONV):
        pre = pre + _shift_down(x, s, rows) * w[QK_CONV - 1 - s:QK_CONV - s, :]
    return pre


def _conv_fwd(proj_a, w, start, name):
    t = proj_a.shape[0]
    width = w.shape[1]
    tc = _tile(width, 256)
    off = start // tc

    def body(x_ref, w_ref, y_ref):
        x = x_ref[...]
        rows = lax.broadcasted_iota(jnp.int32, x.shape, 0)
        pre = _conv_pre(x, w_ref[...], rows)
        y_ref[...] = pre * jax.nn.sigmoid(pre)

    return pl.pallas_call(
        body, name=name,
        out_shape=jax.ShapeDtypeStruct((t, width), F32),
        grid=(width // tc,),
        in_specs=[pl.BlockSpec((t, tc), lambda j: (0, off + j)), pl.BlockSpec((QK_CONV, tc), lambda j: (0, j))],
        out_specs=pl.BlockSpec((t, tc), lambda j: (0, j)),
        compiler_params=_params("parallel"),
    )(proj_a, w)


def _conv_bwd(proj_a, w, dy, start, name):
    t = proj_a.shape[0]
    width = w.shape[1]
    tc = _tile(width, 256)
    off = start // tc

    def body(x_ref, w_ref, dy_ref, dx_ref, dw_ref):
        x, w = x_ref[...], w_ref[...]
        rows = lax.broadcasted_iota(jnp.int32, x.shape, 0)
        pre = _conv_pre(x, w, rows)
        sig = jax.nn.sigmoid(pre)
        dpre = dy_ref[...] * (sig * (1.0 + pre * (1.0 - sig)))
        dx = dpre * w[QK_CONV - 1:QK_CONV, :]
        dws = [jnp.sum(dpre * x, axis=0, keepdims=True)]
        for s in range(1, QK_CONV):
            dx = dx + _shift_up(dpre, s, rows) * w[QK_CONV - 1 - s:QK_CONV - s, :]
            dws.append(jnp.sum(dpre * _shift_down(x, s, rows), axis=0, keepdims=True))
        dx_ref[...] = dx.astype(dx_ref.dtype)
        for s in range(QK_CONV):
            dw_ref[QK_CONV - 1 - s:QK_CONV - s, :] = dws[s]

    return pl.pallas_call(
        body, name=name,
        out_shape=[jax.ShapeDtypeStruct((t, width), BF16), jax.ShapeDtypeStruct((QK_CONV, width), F32)],
        grid=(width // tc,),
        in_specs=[pl.BlockSpec((t, tc), lambda j: (0, off + j)), pl.BlockSpec((QK_CONV, tc), lambda j: (0, j)),
                  pl.BlockSpec((t, tc), lambda j: (0, j))],
        out_specs=[pl.BlockSpec((t, tc), lambda j: (0, j)), pl.BlockSpec((QK_CONV, tc), lambda j: (0, j))],
        compiler_params=_params("parallel"),
    )(proj_a, w, dy)


def _dot(a, b, dims):
    return lax.dot_general(a.astype(BF16), b.astype(BF16), (dims, ((), ())), preferred_element_type=F32)


def _dot_split(ones, x):
    hi = x.astype(BF16)
    rest = x - hi.astype(F32)
    mid = rest.astype(BF16)
    lo = (rest - mid.astype(F32)).astype(BF16)
    dot = lambda part: lax.dot_general(ones, part, (NN, ((), ())), preferred_element_type=F32)
    return dot(hi) + dot(mid) + dot(lo)


def _chunk_iotas():
    rows = lax.broadcasted_iota(jnp.int32, (CHUNK, CHUNK), 0)
    cols = lax.broadcasted_iota(jnp.int32, (CHUNK, CHUNK), 1)
    return rows, cols


def _gates_fwd(proj_b, bias, col_block, heads, name):
    t = proj_b.shape[0]

    def body(x_ref, b_ref, gb_ref, gbt_ref, cols_ref):
        rows, cols = _chunk_iotas()
        tri = (cols <= rows).astype(BF16)
        for ci in range(t // CHUNK):
            sl = slice(ci * CHUNK, (ci + 1) * CHUNK)
            pre = x_ref[sl, :] + b_ref[...]
            log_f = jnp.minimum(pre, 0.0) - jnp.log(1.0 + jnp.exp(-jnp.abs(pre)))
            log_f = jnp.where((cols >= heads) & (cols < 2 * heads), log_f, 0.0)
            tile = jnp.where(cols < heads, pre, _dot_split(tri, log_f))
            gb_ref[sl, :] = tile
            gbt_ref[:, sl] = tile.T
            for h in range(2 * heads):
                cols_ref[h, sl, :] = jnp.broadcast_to(tile[:, h:h + 1], (CHUNK, LANES))

    return pl.pallas_call(
        body, name=name,
        out_shape=[jax.ShapeDtypeStruct((t, LANES), F32), jax.ShapeDtypeStruct((LANES, t), F32),
                   jax.ShapeDtypeStruct((2 * heads, t, LANES), F32)],
        grid=(1,),
        in_specs=[pl.BlockSpec((t, LANES), lambda i: (0, col_block)), pl.BlockSpec((1, LANES), lambda i: (0, 0))],
        out_specs=[pl.BlockSpec((t, LANES), lambda i: (0, 0)), pl.BlockSpec((LANES, t), lambda i: (0, 0)),
                   pl.BlockSpec((2 * heads, t, LANES), lambda i: (0, 0, 0))],
        compiler_params=_params("arbitrary"),
    )(proj_b, bias)


def _gates_bwd(proj_b, bias, dcol, drow, col_block, heads, name):
    t = proj_b.shape[0]

    def body(x_ref, b_ref, dc_ref, dr_ref, dx_ref, db_ref):
        rows, cols = _chunk_iotas()
        later = (cols >= rows).astype(BF16)
        total = jnp.zeros((1, LANES), F32)
        for ci in range(t // CHUNK):
            sl = slice(ci * CHUNK, (ci + 1) * CHUNK)
            by_row = dr_ref[:, sl].T
            d = dc_ref[sl, :] + jnp.where(cols < heads, by_row,
                                          jnp.where(cols < 2 * heads, -pltpu.roll(by_row, heads, axis=1), 0.0))
            d_log_f = _dot_split(later, jnp.where(cols >= heads, d, 0.0))
            pre = x_ref[sl, :] + b_ref[...]
            dx = jnp.where(cols < heads, d, jnp.where(cols < 2 * heads, d_log_f * jax.nn.sigmoid(-pre), 0.0))
            dx_ref[sl, :] = dx.astype(dx_ref.dtype)
            total = total + jnp.sum(dx, axis=0, keepdims=True)
        db_ref[...] = total

    return pl.pallas_call(
        body, name=name,
        out_shape=[jax.ShapeDtypeStruct((t, LANES), BF16), jax.ShapeDtypeStruct((1, LANES), F32)],
        grid=(1,),
        in_specs=[pl.BlockSpec((t, LANES), lambda i: (0, col_block)), pl.BlockSpec((1, LANES), lambda i: (0, 0)),
                  pl.BlockSpec((t, LANES), lambda i: (0, 0)), pl.BlockSpec((LANES, t), lambda i: (0, 0))],
        out_specs=[pl.BlockSpec((t, LANES), lambda i: (0, 0)), pl.BlockSpec((1, LANES), lambda i: (0, 0))],
        compiler_params=_params("arbitrary"),
    )(proj_b, bias, dcol, drow)


def _gate_views(li_ref, b_ref, gbt_ref, s, head, heads, sl):
    li_c, b_c = li_ref[s, sl, :], b_ref[s, sl, :]
    li_r = jnp.broadcast_to(gbt_ref[head, :, sl], (CHUNK, CHUNK))
    b_r = jnp.broadcast_to(gbt_ref[heads + head, :, sl], (CHUNK, CHUNK))
    return li_c, b_c, li_r, b_r


def _chunk_state(kt, k, v, li_c, b_c, ct, nrow, m_prev):
    b_tot = b_c[CHUNK - 1:CHUNK, 0:1]
    a = b_tot - b_c[:, 0:1] + li_c[:, 0:1]
    m_new = jnp.maximum(b_tot + m_prev, jnp.max(a, axis=0, keepdims=True))
    dec = jnp.exp(b_tot + m_prev - m_new)
    w = jnp.exp(a - m_new)
    wv = w * v
    ct_new = dec * ct + _dot(kt, wv, NN)
    n_new = dec * nrow + jnp.sum(w * k, axis=0, keepdims=True)
    return ct_new, n_new, m_new, dec, w, wv


def _chunk_scores(q, k, b_c, li_r, b_r, m_prev):
    rows, cols = _chunk_iotas()
    dlog = jnp.where(cols <= rows, b_c - b_r + li_r, -jnp.inf)
    inter = b_c[:, 0:1] + m_prev
    mt = jnp.maximum(inter, jnp.max(dlog, axis=-1, keepdims=True))
    dw = jnp.exp(dlog - mt)
    iw = jnp.exp(inter - mt)
    a = _dot(q, k, NT)
    return dw, iw, mt, a, a * dw


def _to_row(column):
    rows, cols = _chunk_iotas()
    return jnp.sum(jnp.where(rows == cols, jnp.broadcast_to(column, (CHUNK, CHUNK)), 0.0), axis=0, keepdims=True)


def _head_out(h, o, g):
    cell, hh, r = _rms_f(h, g)
    sig = jax.nn.sigmoid(o)
    return sig * cell, (cell, hh, r, sig)


HEADS_PER_STEP = 2


def _mlstm_fwd(qc, kc, kct, proj_a, gcols, gbt, g_mlstm, heads, d_model, name, dep):
    t = qc.shape[0]
    dh = g_mlstm.shape[1] // heads
    nc = t // CHUNK
    scale = dh ** -0.5
    hp = HEADS_PER_STEP

    def body(q_ref, k_ref, kt_ref, v_ref, o_ref, li_ref, b_ref, gbt_ref, g_ref, dep_ref, out_ref, ct_out, nm_out,
             ct_ref, n_ref, m_ref):
        del dep_ref
        step = pl.program_id(0)
        ct_ref[...] = jnp.zeros_like(ct_ref)
        n_ref[...] = jnp.zeros_like(n_ref)
        m_ref[...] = jnp.zeros_like(m_ref)
        tile_row = lax.broadcasted_iota(jnp.int32, (8, LANES), 0)

        def chunk(ci, carry):
            sl = pl.ds(pl.multiple_of(ci * CHUNK, CHUNK), CHUNK)
            for s in range(hp):
                cs = slice(s * dh, (s + 1) * dh)
                q, k, v, kt = q_ref[sl, cs], k_ref[sl, cs] * scale, v_ref[sl, cs], kt_ref[cs, sl]
                li_c, b_c, li_r, b_r = _gate_views(li_ref, b_ref, gbt_ref, s, step * hp + s, heads, sl)
                ct, nrow, m_prev = ct_ref[s], n_ref[s, 0:1, :], m_ref[s, 0:1, 0:1]
                ct_out[s, ci] = ct
                nm_out[s, ci] = jnp.where(tile_row == 0, nrow, jnp.where(tile_row == 1, m_prev, 0.0))
                dw, iw, mt, _, sm = _chunk_scores(q, k, b_c, li_r, b_r, m_prev)
                iq = iw * q
                num = _dot(jnp.concatenate([iq, sm], axis=1), jnp.concatenate([ct, v], axis=0), NN)
                den = jnp.sum(iq * nrow + sm, axis=-1, keepdims=True)
                h = num / jnp.maximum(jnp.abs(den), jnp.exp(-mt))
                y, _ = _head_out(h, o_ref[sl, cs], g_ref[:, cs])
                out_ref[sl, cs] = y.astype(out_ref.dtype)
                ct_new, n_new, m_new, _, _, _ = _chunk_state(kt, k, v, li_c, b_c, ct, nrow, m_prev)
                ct_ref[s] = ct_new
                n_ref[s, 0:1, :] = n_new
                m_ref[s, 0:1, 0:1] = m_new
            return carry

        lax.fori_loop(0, nc, chunk, 0, unroll=2)

    col = lambda off: pl.BlockSpec((t, hp * dh), lambda h: (0, off // hp + h))
    slab = lambda off: pl.BlockSpec((hp, t, LANES), lambda h: (off // hp + h, 0, 0))
    return pl.pallas_call(
        body, name=name,
        out_shape=[jax.ShapeDtypeStruct((t, d_model), BF16), jax.ShapeDtypeStruct((heads, nc, dh, dh), F32),
                   jax.ShapeDtypeStruct((heads, nc, 8, LANES), F32)],
        grid=(heads // hp,),
        in_specs=[col(0), col(0), pl.BlockSpec((hp * dh, t), lambda h: (h, 0)), col(2 * heads), col(3 * heads),
                  slab(0), slab(heads), pl.BlockSpec((2 * heads, 1, t), lambda h: (0, 0, 0)),
                  pl.BlockSpec((1, hp * dh), lambda h: (0, h)), ANY_SPEC],
        out_specs=[pl.BlockSpec((t, hp * dh), lambda h: (0, h)),
                   pl.BlockSpec((hp, nc, dh, dh), lambda h: (h, 0, 0, 0)),
                   pl.BlockSpec((hp, nc, 8, LANES), lambda h: (h, 0, 0, 0))],
        scratch_shapes=[pltpu.VMEM((hp, dh, dh), F32), pltpu.VMEM((hp, 8, dh), F32), pltpu.VMEM((hp, 8, LANES), F32)],
        compiler_params=_params("arbitrary"),
    )(qc, kc, kct, proj_a, proj_a, gcols, gcols, gbt, g_mlstm, dep)


def _mlstm_bwd(qc, kc, qct, kct, proj_a, gcols, gbt, g_mlstm, states, dhcat, heads, name, dep):
    t = qc.shape[0]
    wm = g_mlstm.shape[1]
    dh = wm // heads
    nc = t // CHUNK
    scale = dh ** -0.5

    def body(q_ref, k_ref, qt_ref, kt_ref, v_ref, o_ref, li_ref, b_ref, gbt_ref, g_ref, ct_all, nm_all, dy_ref, dep_ref,
             dq_ref, dk_ref, dv_ref, do_ref, dcol_ref, drow_ref, dg_ref, dct_ref, dn_ref):
        del dep_ref
        head = pl.program_id(0)
        g = g_ref[...]

        @pl.when(head == 0)
        def _():
            dcol_ref[...] = jnp.zeros_like(dcol_ref)
            drow_ref[...] = jnp.zeros_like(drow_ref)

        def load(ci):
            sl = pl.ds(pl.multiple_of(ci * CHUNK, CHUNK), CHUNK)
            return (sl, q_ref[sl, :], k_ref[sl, :] * scale, v_ref[sl, :], kt_ref[:, sl]) \
                + _gate_views(li_ref, b_ref, gbt_ref, 0, head, heads, sl)

        dct_ref[...] = jnp.zeros_like(dct_ref)
        dn_ref[...] = jnp.zeros_like(dn_ref)
        dg_ref[...] = jnp.zeros_like(dg_ref)
        rows, cols = _chunk_iotas()
        last = lax.broadcasted_iota(jnp.int32, (CHUNK, 1), 0) == CHUNK - 1
        lane = lax.broadcasted_iota(jnp.int32, (CHUNK, LANES), 1)

        def bwd(step, carry):
            ci = nc - 1 - step
            sl, q, k, v, kt, li_c, b_c, li_r, b_r = load(ci)
            qt = qt_ref[:, sl]
            ct, nrow, m_prev = ct_all[ci], nm_all[ci, 0:1, :], nm_all[ci, 1:2, 0:1]
            dw, iw, mt, a, sm = _chunk_scores(q, k, b_c, li_r, b_r, m_prev)
            qc_ = _dot(q, ct, NN)
            qn = jnp.sum(q * nrow, axis=-1, keepdims=True)
            num = iw * qc_ + _dot(sm, v, NN)
            den = iw * qn + jnp.sum(sm, axis=-1, keepdims=True)
            floor = jnp.exp(-mt)
            dn = jnp.maximum(jnp.abs(den), floor)
            h = num / dn
            _, (cell, hh, r, sig) = _head_out(h, o_ref[sl, :], g)
            dy = dy_ref[sl, :]
            dcell = dy * sig
            do_ref[sl, :] = (dy * cell * sig * (1.0 - sig)).astype(do_ref.dtype)
            dh_, dg = _rms_b(dcell, g, hh, r)
            dg_ref[...] += dg
            dnum = dh_ / dn
            ddn = -jnp.sum(dh_ * h, axis=-1, keepdims=True) / dn
            dden = jnp.where(jnp.abs(den) >= floor, ddn * jnp.sign(den), 0.0)
            idn = iw * dnum
            idd = iw * dden
            dw_t = jnp.exp(jnp.where(cols >= rows, b_r - b_c + li_c, -jnp.inf) - _to_row(mt))
            sm_t = _dot(k, q, NT) * dw_t
            dsm = _dot(dnum, v, NT) + dden
            da_ = dsm * dw
            da_t = (_dot(v, dnum, NT) + _to_row(dden)) * dw_t
            e = da_ * a
            dq = _dot(idn, ct, NT) + _dot(da_, k, NN) + idd * nrow
            dk = _dot(da_t, q, NN)
            dv = _dot(sm_t, dnum, NN)
            dct = _dot(qt, idn, NN)
            dnr = jnp.sum(q * idd, axis=0, keepdims=True)
            db = jnp.sum(e + idn * qc_, axis=-1, keepdims=True) + idd * qn
            _, _, _, dec, w, wv = _chunk_state(kt, k, v, li_c, b_c, ct, nrow, m_prev)
            dct_new, dn_new = dct_ref[...], dn_ref[0:1, :]
            ddec = jnp.sum(jnp.sum(dct_new * ct, axis=-1, keepdims=True), axis=0, keepdims=True) \
                + jnp.sum(dn_new * nrow, axis=-1, keepdims=True)
            dk = dk + _dot(wv, dct_new, NT) + w * dn_new
            dwv = _dot(k, dct_new, NN)
            dv = dv + w * dwv
            dwt = jnp.sum(dwv * v + k * dn_new, axis=-1, keepdims=True)
            da = dwt * w
            dbtot = jnp.sum(da, axis=0, keepdims=True) + ddec * dec
            db = db - da + jnp.where(last, dbtot, 0.0)
            dct_ref[...] = dec * dct_new + dct
            dn_ref[0:1, :] = dec * dn_new + dnr
            dq_ref[sl, :] = dq
            dk_ref[sl, :] = dk * scale
            dv_ref[sl, :] = dv.astype(dv_ref.dtype)
            dcol_ref[sl, :] += jnp.where(lane == head, da, 0.0) + jnp.where(lane == heads + head, db, 0.0)
            drow_ref[head, :, sl] = jnp.sum(e, axis=0, keepdims=True)
            return carry

        lax.fori_loop(0, nc, bwd, 0, unroll=4)

    col = lambda off: pl.BlockSpec((t, dh), lambda h: (0, off + h))
    rowb = pl.BlockSpec((dh, t), lambda h: (h, 0))
    slab = lambda off: pl.BlockSpec((1, t, LANES), lambda h: (off + h, 0, 0))
    vec = pl.BlockSpec((1, dh), lambda h: (0, h))
    return pl.pallas_call(
        body, name=name,
        out_shape=[jax.ShapeDtypeStruct((t, wm), F32)] * 2 + [jax.ShapeDtypeStruct((t, wm), BF16)] * 2
        + [jax.ShapeDtypeStruct((t, LANES), F32), jax.ShapeDtypeStruct((heads, 1, t), F32),
           jax.ShapeDtypeStruct((1, wm), F32)],
        grid=(heads,),
        in_specs=[col(0), col(0), rowb, rowb, col(2 * heads), col(3 * heads), slab(0), slab(heads),
                  pl.BlockSpec((2 * heads, 1, t), lambda h: (0, 0, 0)), vec,
                  pl.BlockSpec((None, nc, dh, dh), lambda h: (h, 0, 0, 0)),
                  pl.BlockSpec((None, nc, 8, LANES), lambda h: (h, 0, 0, 0)), col(0), ANY_SPEC],
        out_specs=[col(0), col(0), col(0), col(0), pl.BlockSpec((t, LANES), lambda h: (0, 0)),
                   pl.BlockSpec((heads, 1, t), lambda h: (0, 0, 0)), vec],
        scratch_shapes=[pltpu.VMEM((dh, dh), F32), pltpu.VMEM((8, dh), F32)],
        compiler_params=_params("arbitrary"),
    )(qc, kc, qct, kct, proj_a, proj_a, gcols, gcols, gbt, g_mlstm, *states, dhcat, dep)


def _pool_window(gi):
    return jnp.where(gi == 0, 2.0, jnp.where(gi == 1, 4.0, jnp.where(gi == 2, 8.0, 16.0))).astype(F32)


def _pool_select(gi, levels):
    return jnp.where(gi == 0, levels[0], jnp.where(gi == 1, levels[1], jnp.where(gi == 2, levels[2], levels[3])))


def _pooled(u, gi, rows):
    s, levels = u, []
    for lvl in range(POOL_GROUPS):
        s = s + _shift_down(s, 1 << lvl, rows)
        levels.append(s)
    cnt = jnp.minimum((rows + 1).astype(F32), _pool_window(gi))
    return _pool_select(gi, levels) / cnt - u, cnt


def _pool_fwd(proj_b, w_pool, scale, hcat, name):
    t, d_model = hcat.shape
    g, cg, _ = w_pool.shape
    off = (d_model - g * cg) // cg

    def body(u_ref, w_ref, s_ref, hcat_ref, out_ref):
        del hcat_ref
        gi = pl.program_id(0)
        u = u_ref[...]
        rows = lax.broadcasted_iota(jnp.int32, u.shape, 0)
        pooled, _ = _pooled(u, gi, rows)
        out_ref[...] = (_dot(pooled, w_ref[...], NN) * s_ref[...]).astype(out_ref.dtype)

    return pl.pallas_call(
        body, name=name,
        out_shape=jax.ShapeDtypeStruct((t, d_model), BF16),
        grid=(g,),
        in_specs=[pl.BlockSpec((t, cg), lambda gi: (0, gi)), pl.BlockSpec((None, cg, cg), lambda gi: (gi, 0, 0)),
                  pl.BlockSpec((1, cg), lambda gi: (0, gi)), HBM_SPEC],
        out_specs=pl.BlockSpec((t, cg), lambda gi: (0, off + gi)),
        input_output_aliases={3: 0},
        compiler_params=_params("arbitrary"),
    )(proj_b, w_pool, scale, hcat)


def _pool_bwd(proj_b, w_pool, scale, dhcat, name):
    t, d_model = dhcat.shape
    g, cg, _ = w_pool.shape
    off = (d_model - g * cg) // cg

    def body(u_ref, w_ref, s_ref, dy_ref, du_ref, dw_ref, ds_ref):
        gi = pl.program_id(0)
        u, w, dy = u_ref[...], w_ref[...], dy_ref[...]
        rows = lax.broadcasted_iota(jnp.int32, u.shape, 0)
        pooled, cnt = _pooled(u, gi, rows)
        mixed = _dot(pooled, w, NN)
        ds_ref[...] = jnp.sum(dy * mixed, axis=0, keepdims=True)
        dmixed = dy * s_ref[...]
        dw_ref[...] = _dot(pooled, dmixed, TN).astype(dw_ref.dtype)
        dpooled = _dot(dmixed, w, NT)
        s, levels = dpooled / cnt, []
        for lvl in range(POOL_GROUPS):
            s = s + _shift_up(s, 1 << lvl, rows)
            levels.append(s)
        du_ref[...] = (_pool_select(gi, levels) - dpooled).astype(du_ref.dtype)

    return pl.pallas_call(
        body, name=name,
        out_shape=[jax.ShapeDtypeStruct((t, g * cg), BF16), jax.ShapeDtypeStruct((g, cg, cg), BF16),
                   jax.ShapeDtypeStruct((1, g * cg), F32)],
        grid=(g,),
        in_specs=[pl.BlockSpec((t, cg), lambda gi: (0, gi)), pl.BlockSpec((None, cg, cg), lambda gi: (gi, 0, 0)),
                  pl.BlockSpec((1, cg), lambda gi: (0, gi)), pl.BlockSpec((t, cg), lambda gi: (0, off + gi))],
        out_specs=[pl.BlockSpec((t, cg), lambda gi: (0, gi)), pl.BlockSpec((None, cg, cg), lambda gi: (gi, 0, 0)),
                   pl.BlockSpec((1, cg), lambda gi: (0, gi))],
        compiler_params=_params("arbitrary"),
    )(proj_b, w_pool, scale, dhcat)


def _pad_cols(a, width):
    return jnp.pad(a, ((0, 0), (0, width - a.shape[1])))


def kernel(x, p, w_in, b_gates, w_qk_conv, g_mlstm, w_pool, pool_scale, w_out, g_mix_pre, g_mix_post, w_ff1, w_ff2, g_ff_pre, g_ff_post, w_ple_proj, w_ple_gate, g_ple_gate, g_ple_post, loss_target, m_w_in, m_b_gates, m_w_qk_conv, m_g_mlstm, m_w_pool, m_pool_scale, m_w_out, m_g_mix_pre, m_g_mix_post, m_w_ff1, m_w_ff2, m_g_ff_pre, m_g_ff_post, m_w_ple_proj, m_w_ple_gate, m_g_ple_gate, m_g_ple_post, v_w_in, v_b_gates, v_w_qk_conv, v_g_mlstm, v_w_pool, v_pool_scale, v_w_out, v_g_mix_pre, v_g_mix_post, v_w_ff1, v_w_ff2, v_g_ff_pre, v_g_ff_post, v_w_ple_proj, v_w_ple_gate, v_g_ple_gate, v_g_ple_post):
    weights = dict(w_in=w_in, b_gates=b_gates, w_qk_conv=w_qk_conv, g_mlstm=g_mlstm, w_pool=w_pool,
                   pool_scale=pool_scale, w_out=w_out, g_mix_pre=g_mix_pre, g_mix_post=g_mix_post, w_ff1=w_ff1,
                   w_ff2=w_ff2, g_ff_pre=g_ff_pre, g_ff_post=g_ff_post, w_ple_proj=w_ple_proj,
                   w_ple_gate=w_ple_gate, g_ple_gate=g_ple_gate, g_ple_post=g_ple_post)
    mom1 = dict(w_in=m_w_in, b_gates=m_b_gates, w_qk_conv=m_w_qk_conv, g_mlstm=m_g_mlstm, w_pool=m_w_pool,
                pool_scale=m_pool_scale, w_out=m_w_out, g_mix_pre=m_g_mix_pre, g_mix_post=m_g_mix_post,
                w_ff1=m_w_ff1, w_ff2=m_w_ff2, g_ff_pre=m_g_ff_pre, g_ff_post=m_g_ff_post,
                w_ple_proj=m_w_ple_proj, w_ple_gate=m_w_ple_gate, g_ple_gate=m_g_ple_gate, g_ple_post=m_g_ple_post)
    mom2 = dict(w_in=v_w_in, b_gates=v_b_gates, w_qk_conv=v_w_qk_conv, g_mlstm=v_g_mlstm, w_pool=v_w_pool,
                pool_scale=v_pool_scale, w_out=v_w_out, g_mix_pre=v_g_mix_pre, g_mix_post=v_g_mix_post,
                w_ff1=v_w_ff1, w_ff2=v_w_ff2, g_ff_pre=v_g_ff_pre, g_ff_post=v_g_ff_post,
                w_ple_proj=v_w_ple_proj, w_ple_gate=v_w_ple_gate, g_ple_gate=v_g_ple_gate, g_ple_post=v_g_ple_post)
    order = list(weights)

    t, d = x.shape[1], x.shape[2]
    heads = MLSTM_HEADS
    wm = g_mlstm.shape[1]
    pw = pool_scale.shape[1]
    cg = pw // POOL_GROUPS
    in_cols = 4 * wm + 2 * heads + pw
    in_shard = w_in.shape[2]
    xs, ps, tgt = x[0], p[0, 0], loss_target[0]

    mx, my, mc = lax.axis_index("x"), lax.axis_index("y"), lax.axis_index("c")
    core = jnp.reshape(mc, (1,)).astype(jnp.int32)
    chip = jnp.reshape(2 * mx + my, (1,)).astype(jnp.int32)

    first = [w_in[0].T.astype(BF16), w_qk_conv[0], w_pool[0].reshape(POOL_GROUPS * w_pool.shape[2], cg).astype(BF16)]
    started, token = _gather_start([first], "gather_start_in")
    anchor = token[0, 0]
    later = lambda w: (w[0] + anchor).astype(BF16)
    rest, token = _gather_start([[later(w_out)], [later(w_ff1)], [later(w_ff2)], [later(w_ple_gate), later(w_ple_proj)]],
                                "gather_start_rest")
    started = started + rest
    bias = _pad_cols(b_gates, LANES)
    small = ["b_gates", "g_mlstm", "pool_scale", "g_mix_pre", "g_mix_post", "g_ff_pre", "g_ff_post", "g_ple_gate",
             "g_ple_post"]
    pack = lambda src: _pad_rows16(jnp.concatenate([_pad_cols(src[k] + anchor, d) for k in small], axis=0))
    small_state = [pack(weights), pack(mom1), pack(mom2)]
    in_state = [a[0].T + anchor for a in (w_in, m_w_in, v_w_in)]

    xn = _norm_fwd(xs, g_mix_pre, "norm_mix_pre", deps=[token])
    g_in, g_conv, g_pool = _forward_pair(
        _gather_wait(started[0], [xn] + small_state + in_state, "gather_wait_in"), "gather_pair_in")
    in_t_a = g_in.reshape(in_cols, d)
    in_t_b = jnp.concatenate([in_t_a[4 * wm + 2 * heads:], in_t_a[4 * wm:4 * wm + 2 * heads],
                              jnp.zeros((LANES - 2 * heads, d), BF16)], axis=0)
    conv_w = g_conv.transpose(1, 0, 2).reshape(QK_CONV, 2 * wm)
    pool_w = g_pool.reshape(N_DEV, POOL_GROUPS, cg // N_DEV, cg).transpose(1, 0, 2, 3).reshape(POOL_GROUPS, cg, cg)
    proj_a = _mm_nt("proj_qkvo", xn, in_t_a, n_rows=4 * wm)
    proj_b = _mm_nt("proj_pool_gates", xn, in_t_b)
    qc = _conv_fwd(proj_a, conv_w[:, :wm], 0, "q_conv")
    kc = _conv_fwd(proj_a, conv_w[:, wm:], wm, "k_conv")
    gb, gbt, gcols = _gates_fwd(proj_b, bias, pw // LANES, heads, "gates")
    gbt = gbt[:2 * heads].reshape(2 * heads, 1, t)
    qct, kct = qc.T.astype(BF16), (kc * (wm // heads) ** -0.5).T.astype(BF16)
    fwd_out = _forward_start(_gather_wait(started[1], gb, "gather_wait_out"), "gather_pair_start_out")
    hcat, *mlstm_states = _mlstm_fwd(qc, kc, kct, proj_a, gcols, gbt, g_mlstm, heads, d, "mlstm", fwd_out["token"])
    hcat = _pool_fwd(proj_b, pool_w, pool_scale, hcat, "pool")
    (g_out,) = _forward_wait(fwd_out, hcat, "gather_pair_wait_out")
    out_w = g_out.reshape(d, d)
    mix = _mm_nn("mix_out", hcat, out_w, tn=512)[0]
    fwd_ff1 = _forward_start(_gather_wait(started[2], mix, "gather_wait_ff1"), "gather_pair_start_ff1")
    h1, hn = _block_fwd(xs, mix, g_mix_post, g_ff_pre, "residual_mix", deps=[fwd_ff1["token"]])
    (g_ff1,) = _forward_wait(fwd_ff1, hn, "gather_pair_wait_ff1")
    relu_a, act = _mm_nn_pieces("ff1", hn, g_ff1, out_dtype=[BF16, BF16],
                                epilogue=lambda acc: (jnp.maximum(acc, 0.0), jnp.square(jnp.maximum(acc, 0.0))))
    (g_ff2,) = _forward_pair(_gather_wait(started[3], act, "gather_wait_ff2"), "gather_pair_ff2")
    ff2_w = g_ff2.reshape(-1, d)
    ff = _mm_nn("ff2", act, ff2_w, tn=256, tk=4 * d)[0]
    fwd_ple = _forward_start(_gather_wait(started[4], ff, "gather_wait_ple"), "gather_pair_start_ple")
    h2, hg = _block_fwd(h1, ff, g_ff_post, g_ple_gate, "residual_ff", deps=[fwd_ple["token"]])
    g_pgate, g_pproj = _forward_wait(fwd_ple, hg, "gather_pair_wait_ple")
    pgate_w = g_pgate.reshape(d, d)
    z = _mm_nn("ple_gate", hg, pgate_w, tn=512)[0]
    e = _mm_nn_pieces("ple_proj", ps, g_pproj)[0]
    dh3, dz, de, dg_ple_post, loss_part = _loss_head(h2, z, e, tgt, g_ple_post, "loss_head")
    loss = lax.psum(loss_part[0, 0], MESH_AXES)

    out = {}

    def state(k):
        if k == "w_in":
            return in_state
        return [a.reshape(-1, a.shape[-1]) for a in (weights[k], mom1[k], mom2[k])]

    def pair_begin(pieces, tag):
        keys = list(pieces)
        return keys, _pair_start([pieces[k] for k in keys], "reduce_pair_start_" + tag)

    def chip_begin(group, after, tag):
        keys, begun = group
        mine, got = _pair_wait(begun, after, "reduce_pair_wait_" + tag)
        sums = [_pair_sum(p_, g, core, "pair_sum_" + k) for k, p_, g in zip(keys, mine, got)]
        return keys, _chip_start(sums, "reduce_start_" + tag)

    def reduce_finish(group, after, tag):
        keys, begun = group
        sums, got = _chip_wait(begun, after, "reduce_wait_" + tag)
        for k, s, g in zip(keys, sums, got):
            res = _adamw_shard(*state(k), s, g, chip, "adamw_" + k)
            out[k] = [r.T[None] if k == "w_in" else r.reshape(weights[k].shape) for r in res]

    token_of = lambda group: group[1]["token"]
    dhg = _mm_nt("d_ple_gate_in", dz, pgate_w, tn=512)
    gw_pgate = _mm_tn("gw_ple_gate", hg, dz, tn=512)
    gw_pproj = _mm_tn("gw_ple_proj", ps, de, pieces=N_DEV)
    pair_ple = pair_begin({"w_ple_gate": gw_pgate.reshape(N_DEV, d // N_DEV, d), "w_ple_proj": gw_pproj}, "ple")
    dh2, dff, dg_ff_post, dg_ple_gate = _block_bwd(h2, ff, dh3, dhg, g_ff_post, g_ple_gate, "d_residual_ff",
                                                   deps=[token_of(pair_ple)])
    red_ple = chip_begin(pair_ple, dff, "ple")
    da = _mm_nt("d_ff2_in", dff, ff2_w, out_dtype=BF16, extra=[relu_a],
                epilogue=lambda acc, r: (acc * (2.0 * r.astype(F32)),), deps=[token_of(red_ple)])
    gw_ff2 = _mm_tn("gw_ff2", act, dff)
    pair_ff2 = pair_begin({"w_ff2": gw_ff2.reshape(N_DEV, -1, d)}, "ff2")
    dhn = _mm_nt_pieces("d_ff1_in", da, g_ff1, deps=[token_of(pair_ff2)])
    place = jnp.concatenate([core, chip])
    mine_ff2, got_ff2 = _pair_wait(pair_ff2[1], dhn, "reduce_pair_wait_ff2")
    gw_ff1, sums_ff2 = _mm_tn("gw_ff1", hn, da, pieces=N_DEV, side=_side_pair_sum(mine_ff2[0], got_ff2[0], place))
    red_ff2 = (["w_ff2"], _chip_start([sums_ff2], "reduce_start_ff2"))
    reduce_finish(red_ple, token_of(red_ff2), "ple")
    pair_ff1 = pair_begin({"w_ff1": gw_ff1}, "ff1")
    dh1, dmix, dg_mix_post, dg_ff_pre = _block_bwd(h1, mix, dh2, dhn, g_mix_post, g_ff_pre, "d_residual_mix",
                                                   deps=[token_of(pair_ff1), token_of(red_ff2)])
    mine_ff1, got_ff1 = _pair_wait(pair_ff1[1], dmix, "reduce_pair_wait_ff1")
    dhcat, sums_ff1 = _mm_nt("d_mix_in", dmix, out_w, tm=t // 2, tn=d // 2,
                             side=_side_pair_sum(mine_ff1[0], got_ff1[0], place))
    red_ff1 = (["w_ff1"], _chip_start([sums_ff1], "reduce_start_ff1"))
    gw_out = _mm_tn("gw_out", hcat, dmix, tn=512, deps=[token_of(red_ff1)])
    du, gw_pool, dg_pool_scale = _pool_bwd(proj_b, pool_w, pool_scale, dhcat, "d_pool")
    pair_out = pair_begin({
        "w_out": gw_out.reshape(N_DEV, d // N_DEV, d),
        "w_pool": gw_pool.reshape(POOL_GROUPS, N_DEV, cg // N_DEV, cg).transpose(1, 0, 2, 3).reshape(N_DEV, -1, cg)},
        "out")
    dq, dk, dv, do, dcol, drow, dg_mlstm = _mlstm_bwd(qc, kc, qct, kct, proj_a, gcols, gbt, g_mlstm, mlstm_states, dhcat,
                                                      heads, "d_mlstm", token_of(pair_out))
    red_out = chip_begin(pair_out, dq, "out")
    drow = jnp.pad(drow.reshape(heads, t), ((0, LANES - heads), (0, 0)))
    dgates, dbias = _gates_bwd(proj_b, bias, dcol, drow, pw // LANES, heads, "d_gates")
    dq_pre, gw_conv_q = _conv_bwd(proj_a, conv_w[:, :wm], dq, 0, "d_q_conv")
    dk_pre, gw_conv_k = _conv_bwd(proj_a, conv_w[:, wm:], dk, wm, "d_k_conv")
    gw_conv = jnp.concatenate([gw_conv_q, gw_conv_k], axis=1)
    dproj_a = jnp.concatenate([dq_pre, dk_pre, dv, do], axis=1)
    dproj_b = jnp.concatenate([du, dgates], axis=1)
    sums_ff2, got_ff2 = _chip_wait(red_ff2[1], token_of(red_out), "reduce_wait_ff2")
    gw_a, *update = _mm_tn("gw_in_qkvo", dproj_a, xn,
                           side=_side_adamw(*state("w_ff2"), sums_ff2[0], got_ff2[0], place))
    out["w_ff2"] = [r.reshape(weights["w_ff2"].shape) for r in update]
    gw_b = _mm_tn("gw_in_pool_gates", dproj_b, xn)
    gw_in_t = jnp.concatenate([gw_a, gw_b[pw:pw + 2 * heads], gw_b[:pw]], axis=0)
    pair_in = pair_begin({
        "w_in": gw_in_t.reshape(N_DEV, in_shard, d),
        "w_qk_conv": gw_conv.reshape(QK_CONV, N_DEV, -1).transpose(1, 0, 2).astype(BF16)}, "in")
    reduce_finish(red_out, token_of(pair_in), "out")
    red_in = chip_begin(pair_in, out["w_out"][0], "in")
    sums_ff1, got_ff1 = _chip_wait(red_ff1[1], token_of(red_in), "reduce_wait_ff1")
    dxn_a, *update = _mm_nn("d_proj_qkvo_in", dproj_a, in_t_a, tn=512, tk=4 * wm, deps=[token_of(red_in)],
                            side=_side_adamw(*state("w_ff1"), sums_ff1[0], got_ff1[0], place))
    out["w_ff1"] = [r.reshape(weights["w_ff1"].shape) for r in update]
    dxn = _mm_nn("d_proj_in", dproj_b, in_t_b, extra=[dxn_a], epilogue=lambda acc, prev: (acc + prev,))[0]
    grad_x, dg_mix_pre = _norm_bwd(xs, dxn, dh1, g_mix_pre, "d_norm_mix_pre")

    small_grads = dict(b_gates=dbias[:, :2 * heads], g_mlstm=dg_mlstm, pool_scale=dg_pool_scale, g_mix_pre=dg_mix_pre,
                       g_mix_post=dg_mix_post, g_ff_pre=dg_ff_pre, g_ff_post=dg_ff_post, g_ple_gate=dg_ple_gate,
                       g_ple_post=dg_ple_post)
    small_begun = _everyone_start([pack(small_grads)], "small_grads_start")
    reduce_finish(red_in, small_begun["token"], "in")
    (small_parts,) = _everyone_wait(small_begun, out["w_in"][0], "small_grads_wait")
    res = _adamw_replicated(*small_state, small_parts, "adamw_replicated")
    for i, k in enumerate(small):
        width = weights[k].shape[1]
        out[k] = [r[i:i + 1, :width] for r in res]

    return (loss, grad_x[None], *[out[k][0] for k in order], *[out[k][1] for k in order],
            *[out[k][2] for k in order], *[out[k][3] for k in order])


def _pad_rows16(a):
    return jnp.pad(a, ((0, 16 - a.shape[0]), (0, 0)))
```

```python
import jax
import jax.numpy as jnp
from jax import lax
from jax.experimental import pallas as pl
from jax.experimental.pallas import tpu as pltpu

F32 = jnp.float32
BF16 = jnp.bfloat16
EPS = 1e-6
N_DEV = 8
N_CHIP = 4
LANES = 128
VMEM_LIMIT = 56 * 1024 * 1024
MLSTM_HEADS = 8
POOL_GROUPS = 4
QK_CONV = 4
CHUNK = 128
ROW_TILE = 256
ADAM_LR, ADAM_B1, ADAM_B2, ADAM_EPS, ADAM_WD, ADAM_STEP = 0.001, 0.9, 0.999, 1e-08, 0.01, 10
MESH_AXES = ("x", "y", "c")
MESH = pl.DeviceIdType.MESH
HBM_SPEC = pl.BlockSpec(memory_space=pltpu.HBM)
SEM_SPEC = pl.BlockSpec(memory_space=pltpu.SEMAPHORE)
DATAFLOW = pltpu.SideEffectType.DATAFLOW_SIDE_EFFECTING


def _params(*sem):
    if sem:
        return pltpu.CompilerParams(dimension_semantics=sem, vmem_limit_bytes=VMEM_LIMIT)
    return pltpu.CompilerParams(vmem_limit_bytes=VMEM_LIMIT)


def _tile(n, pref):
    if n <= pref:
        return n
    t = (pref // LANES) * LANES
    while t >= LANES:
        if n % t == 0:
            return t
        t -= LANES
    return n


def _hbm(a):
    return pltpu.with_memory_space_constraint(a, pltpu.HBM)


def _mesh_place():
    x, y, c = lax.axis_index("x"), lax.axis_index("y"), lax.axis_index("c")
    return x, y, c, [(1 - x, y), (x, 1 - y), (1 - x, 1 - y)]


def _chip_copies(srcs, lands, send, recv):
    x, y, c, chips = _mesh_place()
    return [pltpu.make_async_remote_copy(
        src_ref=srcs[w].at[2 * chip[0] + chip[1]], dst_ref=lands[w].at[k],
        send_sem=send.at[3 * w + k], recv_sem=recv.at[3 * w + k],
        device_id=(*chip, c), device_id_type=MESH)
        for k, chip in enumerate(chips) for w in range(len(srcs))]


def _chip_start(sums, name):
    return _split_start(_chip_copies, sums, [lax.empty((3,) + a.shape[1:], a.dtype) for a in sums], 3, name)


def _chip_wait(started, after, name):
    return _split_wait(_chip_copies, started, after, name)


def _pair_start(pieces, name):
    return _split_start(_pair_copies, pieces, [lax.empty((N_CHIP,) + a.shape[1:], a.dtype) for a in pieces], N_CHIP, name)


def _pair_wait(started, after, name):
    return _split_wait(_pair_copies, started, after, name)


def _pair_copies(srcs, lands, send, recv):
    x, y, c, _ = _mesh_place()
    return [pltpu.make_async_remote_copy(
        src_ref=srcs[w].at[2 * ch + 1 - c], dst_ref=lands[w].at[ch],
        send_sem=send.at[N_CHIP * w + ch], recv_sem=recv.at[N_CHIP * w + ch],
        device_id=(x, y, 1 - c), device_id_type=MESH)
        for w in range(len(srcs)) for ch in range(N_CHIP)]


def _split_start(copies_of, srcs, lands, per_array, name):
    ns, nb = len(srcs), len(srcs) + len(lands)
    n_copies = per_array * len(lands)

    def body(*refs):
        for cp in copies_of(refs[:ns], refs[ns:nb], refs[nb], refs[nb + 1]):
            cp.start()
        refs[-1][...] = jnp.zeros_like(refs[-1])

    res = pl.pallas_call(
        body, name=name,
        out_shape=[pltpu.SemaphoreType.DMA((n_copies,)), pltpu.SemaphoreType.DMA((n_copies,))]
        + [pltpu.HBM(a.shape, a.dtype) for a in list(srcs) + list(lands)] + [jax.ShapeDtypeStruct((8, LANES), F32)],
        in_specs=[HBM_SPEC] * nb,
        out_specs=[SEM_SPEC, SEM_SPEC] + [HBM_SPEC] * nb + [pl.BlockSpec(memory_space=pltpu.VMEM)],
        input_output_aliases={i: 2 + i for i in range(nb)},
        compiler_params=pltpu.CompilerParams(has_side_effects=DATAFLOW),
    )(*[_hbm(a) for a in srcs], *[_hbm(a) for a in lands])
    return dict(send=res[0], recv=res[1], srcs=list(res[2:2 + ns]), lands=list(res[2 + ns:2 + nb]), token=res[-1])


def _split_wait(copies_of, started, after, name):
    ns = len(started["srcs"])
    nb = ns + len(started["lands"])

    def body(*refs):
        for cp in copies_of(refs[:ns], refs[ns:nb], refs[nb], refs[nb + 1]):
            cp.wait_send()
            cp.wait_recv()

    res = pl.pallas_call(
        body, name=name,
        out_shape=[pltpu.HBM(a.shape, a.dtype) for a in started["srcs"] + started["lands"]],
        in_specs=[HBM_SPEC] * nb + [SEM_SPEC, SEM_SPEC, ANY_SPEC],
        out_specs=[HBM_SPEC] * nb,
        input_output_aliases={i: i for i in range(nb)},
        compiler_params=pltpu.CompilerParams(has_side_effects=DATAFLOW),
    )(*started["srcs"], *started["lands"], started["send"], started["recv"], after)
    return list(res[:ns]), list(res[ns:])


def _forward_copies(srcs, lands, send, recv):
    del srcs
    x, y, c, chips = _mesh_place()
    copies = []
    for w in range(len(lands)):
        for j, chip in enumerate(chips):
            rows = lands[w].at[4 * chip[0] + 2 * chip[1] + c]
            copies.append(pltpu.make_async_remote_copy(
                src_ref=rows, dst_ref=rows, send_sem=send.at[3 * w + j], recv_sem=recv.at[3 * w + j],
                device_id=(x, y, 1 - c), device_id_type=MESH))
    return copies


def _forward_start(lands, name):
    return _split_start(_forward_copies, [], lands, 3, name)


def _forward_wait(started, after, name):
    return _split_wait(_forward_copies, started, after, name)[1]


def _everyone_copies(srcs, lands, send, recv):
    x, y, c, _ = _mesh_place()
    me = 4 * x + 2 * y + c
    copies = []
    for w in range(len(srcs)):
        for k in range(N_DEV - 1):
            flip = k + 1
            peer = (1 - x if flip & 4 else x, 1 - y if flip & 2 else y, 1 - c if flip & 1 else c)
            copies.append(pltpu.make_async_remote_copy(
                src_ref=srcs[w], dst_ref=lands[w].at[me], send_sem=send.at[7 * w + k], recv_sem=recv.at[7 * w + k],
                device_id=peer, device_id_type=MESH))
    return copies


def _everyone_start(blocks, name):
    me = 4 * lax.axis_index("x") + 2 * lax.axis_index("y") + lax.axis_index("c")
    lands = [lax.dynamic_update_slice(lax.empty((N_DEV,) + a.shape, a.dtype), a[None], (me,) + (0,) * a.ndim)
             for a in blocks]
    return _split_start(_everyone_copies, blocks, lands, N_DEV - 1, name)


def _everyone_wait(started, after, name):
    return _split_wait(_everyone_copies, started, after, name)[1]


def _gather_copies(srcs, lands, send, recv):
    x, y, c, chips = _mesh_place()
    me = 4 * x + 2 * y + c
    targets = [(x, y, 1 - c)] + [(*chip, c) for chip in chips]
    return [pltpu.make_async_remote_copy(
        src_ref=srcs[w], dst_ref=lands[w].at[me], send_sem=send.at[4 * w + k], recv_sem=recv.at[4 * w + k],
        device_id=to, device_id_type=MESH)
        for w in range(len(srcs)) for k, to in enumerate(targets)]


def _gather_start(groups, name):
    sizes = [len(g) for g in groups]
    flat = [a for g in groups for a in g]
    n, ng = len(flat), len(groups)
    me = 4 * lax.axis_index("x") + 2 * lax.axis_index("y") + lax.axis_index("c")
    lands = [lax.dynamic_update_slice(lax.empty((N_DEV,) + a.shape, a.dtype), a[None], (me,) + (0,) * a.ndim)
             for a in flat]

    def body(*refs):
        srcs, zones = refs[:n], refs[n:2 * n]
        sems = refs[2 * n:2 * n + 2 * ng]
        token = refs[-1]
        lo = 0
        for gi, size in enumerate(sizes):
            for cp in _gather_copies(srcs[lo:lo + size], zones[lo:lo + size], sems[2 * gi], sems[2 * gi + 1]):
                cp.start()
            lo += size
        token[...] = jnp.zeros_like(token)

    sem_shapes = []
    for size in sizes:
        sem_shapes += [pltpu.SemaphoreType.DMA((4 * size,))] * 2
    res = pl.pallas_call(
        body, name=name,
        out_shape=sem_shapes + [pltpu.HBM(a.shape, a.dtype) for a in flat + lands] + [jax.ShapeDtypeStruct((8, LANES), F32)],
        in_specs=[HBM_SPEC] * (2 * n),
        out_specs=[SEM_SPEC] * (2 * ng) + [HBM_SPEC] * (2 * n) + [pl.BlockSpec(memory_space=pltpu.VMEM)],
        input_output_aliases={i: 2 * ng + i for i in range(2 * n)},
        compiler_params=pltpu.CompilerParams(has_side_effects=DATAFLOW),
    )(*[_hbm(a) for a in flat], *[_hbm(a) for a in lands])
    out, lo = [], 0
    for gi, size in enumerate(sizes):
        out.append(dict(send=res[2 * gi], recv=res[2 * gi + 1],
                        srcs=list(res[2 * ng + lo:2 * ng + lo + size]),
                        lands=list(res[2 * ng + n + lo:2 * ng + n + lo + size])))
        lo += size
    return out, res[-1]


def _gather_wait(started, after, name):
    n = len(started["srcs"])
    after = list(after) if isinstance(after, (list, tuple)) else [after]

    def body(*refs):
        srcs, zones = refs[:n], refs[n:2 * n]
        send, recv = refs[2 * n], refs[2 * n + 1]
        for cp in _gather_copies(srcs, zones, send, recv):
            cp.wait_send()
            cp.wait_recv()

    res = pl.pallas_call(
        body, name=name,
        out_shape=[pltpu.HBM(a.shape, a.dtype) for a in started["srcs"] + started["lands"]],
        in_specs=[HBM_SPEC] * (2 * n) + [SEM_SPEC, SEM_SPEC] + [ANY_SPEC] * len(after),
        out_specs=[HBM_SPEC] * (2 * n),
        input_output_aliases={i: i for i in range(2 * n)},
        compiler_params=pltpu.CompilerParams(has_side_effects=DATAFLOW),
    )(*started["srcs"], *started["lands"], started["send"], started["recv"], *after)
    return list(res[n:])


def _forward_pair(lands, name):
    n = len(lands)

    def body(*refs):
        zones = refs[:n]
        send, recv = refs[2 * n:]
        x, y, c, chips = _mesh_place()

        def copy(w, j, core):
            rows = zones[w].at[4 * chips[j][0] + 2 * chips[j][1] + core]
            return pltpu.make_async_remote_copy(
                src_ref=rows, dst_ref=rows, send_sem=send.at[3 * w + j], recv_sem=recv.at[3 * w + j],
                device_id=(x, y, 1 - c), device_id_type=MESH)

        sends = [copy(w, j, c) for w in range(n) for j in range(3)]
        for cp in sends:
            cp.start()
        for w in range(n):
            for j in range(3):
                copy(w, j, 1 - c).wait_recv()
        for cp in sends:
            cp.wait_send()

    return list(pl.pallas_call(
        body, name=name,
        out_shape=[jax.ShapeDtypeStruct(a.shape, a.dtype) for a in lands],
        in_specs=[HBM_SPEC] * n, out_specs=[HBM_SPEC] * n,
        input_output_aliases={i: i for i in range(n)},
        scratch_shapes=[pltpu.SemaphoreType.DMA((3 * n,)), pltpu.SemaphoreType.DMA((3 * n,))],
    )(*lands))


def _pair_sum(pieces, got, core, name):
    _, rows, cols = pieces.shape
    tr, tc = _tile_2d(rows, cols)

    def body(core_ref, mine_ref, got_ref, out_ref):
        del core_ref
        out_ref[...] = (mine_ref[...].astype(F32) + got_ref[...].astype(F32)).astype(out_ref.dtype)

    return pl.pallas_call(
        body, name=name,
        out_shape=jax.ShapeDtypeStruct((N_CHIP, rows, cols), pieces.dtype),
        grid_spec=pltpu.PrefetchScalarGridSpec(
            num_scalar_prefetch=1, grid=(N_CHIP, rows // tr, cols // tc),
            in_specs=[pl.BlockSpec((None, tr, tc), lambda ch, i, j, core: (2 * ch + core[0], i, j)),
                      pl.BlockSpec((None, tr, tc), lambda ch, i, j, core: (ch, i, j))],
            out_specs=pl.BlockSpec((None, tr, tc), lambda ch, i, j, core: (ch, i, j))),
        compiler_params=_params("parallel", "parallel", "parallel"),
    )(core, pieces, got)


def _tile_2d(rows, cols, budget=2 * 1024 * 1024):
    want = max(16, budget // (4 * cols))
    if rows <= want:
        return rows, cols
    t = (want // 16) * 16
    while t >= 16:
        if rows % t == 0:
            return t, cols
        t -= 16
    return rows, _tile(cols, max(LANES, budget // (4 * rows)))


def _adamw_math(w, g, m, v):
    m = ADAM_B1 * m + (1.0 - ADAM_B1) * g
    v = ADAM_B2 * v + (1.0 - ADAM_B2) * (g * g)
    m_hat = m / (1.0 - ADAM_B1 ** ADAM_STEP)
    v_hat = v / (1.0 - ADAM_B2 ** ADAM_STEP)
    delta = -ADAM_LR * (m_hat / (jnp.sqrt(v_hat) + ADAM_EPS) + ADAM_WD * w)
    return delta, m, v


def _adamw_shard(w, m, v, sums, got, chip, name):
    rows, cols = w.shape
    tr, tc = _tile_2d(rows, cols, budget=1024 * 1024)

    def body(chip_ref, w_ref, m_ref, v_ref, own_ref, got_ref, g_out, d_out, m_out, v_out):
        del chip_ref
        g = own_ref[...].astype(F32)
        for k in range(3):
            g = g + got_ref[k].astype(F32)
        delta, m_new, v_new = _adamw_math(w_ref[...], g, m_ref[...], v_ref[...])
        g_out[...] = g
        d_out[...] = delta
        m_out[...] = m_new
        v_out[...] = v_new

    blk = pl.BlockSpec((tr, tc), lambda i, j, chip: (i, j))
    return pl.pallas_call(
        body, name=name,
        out_shape=[jax.ShapeDtypeStruct((rows, cols), F32)] * 4,
        grid_spec=pltpu.PrefetchScalarGridSpec(
            num_scalar_prefetch=1, grid=(rows // tr, cols // tc),
            in_specs=[blk, blk, blk,
                      pl.BlockSpec((None, tr, tc), lambda i, j, chip: (chip[0], i, j)),
                      pl.BlockSpec((3, tr, tc), lambda i, j, chip: (0, i, j))],
            out_specs=[blk] * 4),
        compiler_params=_params("parallel", "parallel"),
    )(chip, w, m, v, sums, got)


def _adamw_replicated(w, m, v, parts, name):
    rows, cols = w.shape

    def body(w_ref, m_ref, v_ref, parts_ref, g_out, d_out, m_out, v_out):
        g = parts_ref[0]
        for d in range(1, N_DEV):
            g = g + parts_ref[d]
        delta, m_new, v_new = _adamw_math(w_ref[...], g, m_ref[...], v_ref[...])
        g_out[...] = g
        d_out[...] = delta
        m_out[...] = m_new
        v_out[...] = v_new

    return pl.pallas_call(
        body, name=name,
        out_shape=[jax.ShapeDtypeStruct((rows, cols), F32)] * 4,
        compiler_params=_params(),
    )(w, m, v, parts)


NN = ((1,), (0,))
NT = ((1,), (1,))
TN = ((0,), (0,))


ANY_SPEC = pl.BlockSpec(memory_space=pl.ANY)


def _mm(name, a, b, *, dims, grid, a_spec, b_spec, outs, extra=(), extra_specs=(), epilogue=None, acc_shape=None,
        deps=(), b_pieces=0, side=None):
    nk = grid[2]
    n_extra, n_out = len(extra), len(outs)
    side_ins = side["ins"] if side else []
    side_outs = side["outs"] if side else []
    first_side = 2 + n_extra + len(deps)
    first_out = first_side + len(side_ins)
    assert not side or nk == 1

    assert not side or (epilogue is None and n_out == 1 and not b_pieces)
    SIDE_SLICES = 4

    def body(*refs):
        dot = lambda lhs, rhs: lax.dot_general(lhs.astype(BF16), rhs.astype(BF16), (dims, ((), ())),
                                               preferred_element_type=F32)
        if side:
            refs = refs[1:]
            a_ref, b_ref, out_ref = refs[0], refs[1], refs[first_out]
            ins, sides = refs[first_side:first_out], refs[first_out + 1:first_out + 1 + len(side_outs)]
            rows = out_ref.shape[0] // SIDE_SLICES
            srows = sides[0].shape[-2] // SIDE_SLICES
            for part in range(SIDE_SLICES):
                rs = slice(part * srows, (part + 1) * srows)
                for val, o in zip(side["fn"](*[r[..., rs, :] for r in ins]), sides):
                    o[rs, :] = val.astype(o.dtype)
                ms = slice(part * rows, (part + 1) * rows)
                lhs = a_ref[:, ms] if dims == TN else a_ref[ms, :]
                out_ref[ms, :] = dot(lhs, b_ref[...]).astype(out_ref.dtype)
            return
        a_ref, b_ref = refs[0], refs[1]
        ex = refs[2:2 + n_extra]
        out_refs = refs[first_out:first_out + n_out]
        if b_pieces:
            ks = b_ref.shape[-1]
            part = dot(a_ref[:, 0:ks], b_ref[0])
            for piece in range(1, b_pieces):
                part = part + dot(a_ref[:, piece * ks:(piece + 1) * ks], b_ref[piece])
        else:
            part = dot(a_ref[...], b_ref[...])

        def finish(acc):
            res = epilogue(acc, *[e[...] for e in ex]) if epilogue else (acc,)
            for val, o in zip(res, out_refs):
                o[...] = val.astype(o.dtype)

        if nk == 1:
            finish(part)
        else:
            acc_ref = refs[-1]
            k = pl.program_id(2)

            @pl.when(k == 0)
            def _():
                acc_ref[...] = part

            @pl.when(k > 0)
            def _():
                acc_ref[...] += part

            @pl.when(k == nk - 1)
            def _():
                finish(acc_ref[...])

    scratch = [pltpu.VMEM(acc_shape, F32)] if nk > 1 else []
    out_shape = [jax.ShapeDtypeStruct(s, d) for s, d, _ in outs]
    in_specs = [a_spec, b_spec, *extra_specs] + [ANY_SPEC] * len(deps)
    out_specs = [sp for _, _, sp in outs]
    if not side:
        return list(pl.pallas_call(
            body, name=name, out_shape=out_shape, grid=grid, in_specs=in_specs, out_specs=out_specs,
            scratch_shapes=scratch, compiler_params=_params("parallel", "parallel", "arbitrary"),
        )(a, b, *extra, *deps))

    def with_scalars(spec):
        if spec.index_map is None:
            return spec
        return pl.BlockSpec(spec.block_shape, lambda i, j, k, s, f=spec.index_map: f(i, j, k))

    def by_step(block, index_map):
        return pl.BlockSpec(block, lambda i, j, k, s: index_map((i * grid[1] + j) * grid[2] + k, s))

    return list(pl.pallas_call(
        body, name=name,
        out_shape=out_shape + [jax.ShapeDtypeStruct(s, d) for s, d, _, _ in side_outs],
        grid_spec=pltpu.PrefetchScalarGridSpec(
            num_scalar_prefetch=1, grid=grid,
            in_specs=[with_scalars(sp) for sp in in_specs] + [by_step(blk, im) for _, blk, im in side_ins],
            out_specs=[with_scalars(sp) for sp in out_specs] + [by_step(blk, im) for _, _, blk, im in side_outs],
            scratch_shapes=scratch),
        compiler_params=_params("arbitrary", "arbitrary", "arbitrary"),
    )(side["scalars"], a, b, *extra, *deps, *[arr for arr, _, _ in side_ins]))


def _side_pair_sum(pieces, got, place):
    _, rows, cols = pieces.shape

    def build(steps):
        per_chip = steps // N_CHIP
        tr = rows // per_chip
        assert per_chip * N_CHIP == steps and tr * per_chip == rows and tr % 16 == 0
        blk = (None, tr, cols)
        here = lambda st, s: (st // per_chip, st % per_chip, 0)
        return dict(scalars=place,
                    ins=[(pieces, blk, lambda st, s: (2 * (st // per_chip) + s[0], st % per_chip, 0)), (got, blk, here)],
                    outs=[((N_CHIP, rows, cols), pieces.dtype, blk, here)],
                    fn=lambda mine, other: (mine.astype(F32) + other.astype(F32),))
    return build


def _side_adamw(w, m, v, sums, got, place):
    rows, cols = w.shape

    def build(steps):
        tr = rows // steps
        assert tr * steps == rows and tr % 16 == 0
        blk, here = (tr, cols), (lambda st, s: (st, 0))

        def fn(w, m, v, own, got):
            g = own.astype(F32)
            for k in range(3):
                g = g + got[k].astype(F32)
            delta, m_new, v_new = _adamw_math(w, g, m, v)
            return g, delta, m_new, v_new

        return dict(scalars=place,
                    ins=[(w, blk, here), (m, blk, here), (v, blk, here),
                         (sums, (None, tr, cols), lambda st, s: (s[1], st, 0)),
                         (got, (3, tr, cols), lambda st, s: (0, st, 0))],
                    outs=[((rows, cols), F32, blk, here)] * 4, fn=fn)
    return build


def _mm_nn(name, a, b, out_dtype=F32, tm=1024, tn=1024, tk=2048, epilogue=None, n_out=1, extra=(), deps=(), side=None):
    m, kd = a.shape
    n = b.shape[1]
    tm, tn, tk = _tile(m, tm), _tile(n, tn), _tile(kd, tk)
    dts = out_dtype if isinstance(out_dtype, (list, tuple)) else [out_dtype] * n_out
    o_spec = pl.BlockSpec((tm, tn), lambda i, j, k: (i, j))
    grid = (m // tm, n // tn, kd // tk)
    return _mm(name, a, b, dims=NN, grid=grid,
               a_spec=pl.BlockSpec((tm, tk), lambda i, j, k: (i, k)),
               b_spec=pl.BlockSpec((tk, tn), lambda i, j, k: (k, j)),
               outs=[((m, n), dt, o_spec) for dt in dts], extra=extra, extra_specs=[o_spec] * len(extra),
               epilogue=epilogue, acc_shape=(tm, tn), deps=deps, side=side and side(grid[0] * grid[1] * grid[2]))


def _mm_nn_pieces(name, a, b, out_dtype=F32, tm=1024, tk=2048, epilogue=None, n_out=1):
    m, kd = a.shape
    npc, _, ns = b.shape
    tm, tk = _tile(m, tm), _tile(kd, tk)
    tn = _tile(ns, 1024)
    r = ns // tn
    dts = out_dtype if isinstance(out_dtype, (list, tuple)) else [out_dtype] * n_out
    o_spec = pl.BlockSpec((tm, tn), lambda i, j, k: (i, j))
    return _mm(name, a, b, dims=NN, grid=(m // tm, npc * r, kd // tk),
               a_spec=pl.BlockSpec((tm, tk), lambda i, j, k: (i, k)),
               b_spec=pl.BlockSpec((None, tk, tn), lambda i, j, k: (j // r, k, j % r)),
               outs=[((m, npc * ns), dt, o_spec) for dt in dts], epilogue=epilogue, acc_shape=(tm, tn))


def _mm_nt(name, a, b, out_dtype=F32, tm=1024, tn=1024, tk=2048, epilogue=None, extra=(), n_rows=None, deps=(),
           side=None):
    m, kd = a.shape
    n = n_rows or b.shape[0]
    tm, tn, tk = _tile(m, tm), _tile(n, tn), _tile(kd, tk)
    o_spec = pl.BlockSpec((tm, tn), lambda i, j, k: (i, j))
    grid = (m // tm, n // tn, kd // tk)
    res = _mm(name, a, b, dims=NT, grid=grid,
              a_spec=pl.BlockSpec((tm, tk), lambda i, j, k: (i, k)),
              b_spec=pl.BlockSpec((tn, tk), lambda i, j, k: (j, k)),
              outs=[((m, n), out_dtype, o_spec)], extra=extra, extra_specs=[o_spec] * len(extra),
              epilogue=epilogue, acc_shape=(tm, tn), deps=deps, side=side and side(grid[0] * grid[1] * grid[2]))
    return res if side else res[0]


def _mm_nt_pieces(name, a, b, out_dtype=F32, tm=1024, tn=256, deps=()):
    m = a.shape[0]
    npc, n, ks = b.shape
    tm, tn = _tile(m, tm), _tile(n, tn)
    return _mm(name, a, b, dims=NT, grid=(m // tm, n // tn, 1),
               a_spec=pl.BlockSpec((tm, npc * ks), lambda i, j, k: (i, 0)),
               b_spec=pl.BlockSpec((npc, tn, ks), lambda i, j, k: (0, j, 0)),
               outs=[((m, n), out_dtype, pl.BlockSpec((tm, tn), lambda i, j, k: (i, j)))],
               deps=deps, b_pieces=npc)[0]


def _mm_tn(name, a, b, out_dtype=BF16, tm=1024, tn=1024, pieces=0, deps=(), side=None):
    t, m = a.shape
    n = b.shape[1]
    tm = _tile(m, tm)
    if pieces:
        ns = n // pieces
        tn = _tile(ns, tn)
        r = ns // tn
        out = ((pieces, m, ns), out_dtype, pl.BlockSpec((None, tm, tn), lambda i, j, k: (j // r, i, j % r)))
    else:
        tn = _tile(n, tn)
        out = ((m, n), out_dtype, pl.BlockSpec((tm, tn), lambda i, j, k: (i, j)))
    grid = (m // tm, n // tn, 1)
    res = _mm(name, a, b, dims=TN, grid=grid,
              a_spec=pl.BlockSpec((t, tm), lambda i, j, k: (0, i)),
              b_spec=pl.BlockSpec((t, tn), lambda i, j, k: (0, j)),
              outs=[out], deps=deps, side=side and side(grid[0] * grid[1]))
    return res if side else res[0]


def _rms_f(x, g):
    r = lax.rsqrt(jnp.mean(x * x, axis=-1, keepdims=True) + EPS)
    xh = x * r
    return xh * g, xh, r


def _rms_b(dy, g, xh, r):
    dxh = dy * g
    dx = r * (dxh - xh * jnp.mean(dxh * xh, axis=-1, keepdims=True))
    dg = jnp.sum(dy * xh, axis=0, keepdims=True)
    return dx, dg


def _rowwise(name, fn, row_ins, vec_ins, row_outs, vec_outs, deps=()):
    t = row_ins[0].shape[0]
    tr = min(ROW_TILE, t)
    nr, nv, no = len(row_ins), len(vec_ins), len(row_outs)
    first_out = nr + nv + len(deps)

    def body(*refs):
        ri, vi = refs[:nr], refs[nr:nr + nv]
        ro, vo = refs[first_out:first_out + no], refs[first_out + no:]
        routs, vouts = fn(*[r[...] for r in ri], *[v[...] for v in vi])
        for o, val in zip(ro, routs):
            o[...] = val.astype(o.dtype)
        i = pl.program_id(0)
        for o, val in zip(vo, vouts):
            @pl.when(i == 0)
            def _():
                o[...] = val

            @pl.when(i > 0)
            def _():
                o[...] += val

    res = pl.pallas_call(
        body, name=name,
        out_shape=[jax.ShapeDtypeStruct((t, w), d) for w, d in row_outs]
        + [jax.ShapeDtypeStruct((1, w), F32) for w in vec_outs],
        grid=(t // tr,),
        in_specs=[pl.BlockSpec((tr, a.shape[1]), lambda i: (i, 0)) for a in row_ins]
        + [pl.BlockSpec((1, a.shape[1]), lambda i: (0, 0)) for a in vec_ins] + [ANY_SPEC] * len(deps),
        out_specs=[pl.BlockSpec((tr, w), lambda i: (i, 0)) for w, _ in row_outs]
        + [pl.BlockSpec((1, w), lambda i: (0, 0)) for w in vec_outs],
        compiler_params=_params("arbitrary"),
    )(*row_ins, *vec_ins, *deps)
    return list(res)


def _norm_fwd(x, g, name, deps=()):
    d = x.shape[1]
    return _rowwise(name, lambda x, g: ((_rms_f(x, g)[0],), ()), [x], [g], [(d, BF16)], [], deps=deps)[0]


def _norm_bwd(x, dxn, dres, g, name):
    d = x.shape[1]

    def fn(x, dxn, dres, g):
        _, xh, r = _rms_f(x, g)
        dx, dg = _rms_b(dxn, g, xh, r)
        return (dres + dx,), (dg,)

    return _rowwise(name, fn, [x, dxn, dres], [g], [(d, F32)], [d])


def _block_fwd(h_prev, y, g_y, g_n, name, deps=()):
    d = h_prev.shape[1]

    def fn(h_prev, y, g_y, g_n):
        h = h_prev + _rms_f(y, g_y)[0]
        return (h, _rms_f(h, g_n)[0]), ()

    return _rowwise(name, fn, [h_prev, y], [g_y, g_n], [(d, F32), (d, BF16)], [], deps=deps)


def _block_bwd(h, y, d_up, d_n, g_y, g_n, name, deps=()):
    d = h.shape[1]

    def fn(h, y, d_up, d_n, g_y, g_n):
        _, hh, hr = _rms_f(h, g_n)
        dh_n, dg_n = _rms_b(d_n, g_n, hh, hr)
        dh = d_up + dh_n
        _, yh, yr = _rms_f(y, g_y)
        dy, dg_y = _rms_b(dh, g_y, yh, yr)
        return (dh, dy), (dg_y, dg_n)

    return _rowwise(name, fn, [h, y, d_up, d_n], [g_y, g_n], [(d, F32), (d, BF16)], [d, d], deps=deps)


def _loss_head(h2, z, e, target, g, name):
    d = h2.shape[1]

    def fn(h2, z, e, target, g):
        gate = jax.nn.sigmoid(z)
        y, xh, r = _rms_f(e * gate, g)
        diff = h2 + y - target
        loss = 0.5 * jnp.sum(jnp.mean(diff * diff, axis=-1, keepdims=True), axis=0, keepdims=True)
        dh3 = diff / d
        deg, dg = _rms_b(dh3, g, xh, r)
        dz = deg * e * gate * (1.0 - gate)
        de = deg * gate
        return (dh3, dz, de), (dg, jnp.broadcast_to(loss, (1, LANES)))

    return _rowwise(name, fn, [h2, z, e, target], [g], [(d, F32), (d, BF16), (d, BF16)], [d, LANES])


def _shift_down(x, s, rows):
    return jnp.where(rows >= s, pltpu.roll(x, s, axis=0), 0.0)


def _shift_up(x, s, rows):
    t = x.shape[0]
    return jnp.where(rows < t - s, pltpu.roll(x, t - s, axis=0), 0.0)


def _conv_pre(x, w, rows):
    pre = x * w[QK_CONV - 1:QK_CONV, :]
    for s in range(1, QK_CONV):
        pre = pre + _shift_down(x, s, rows) * w[QK_CONV - 1 - s:QK_CONV - s, :]
    return pre


def _conv_fwd(proj_a, w, start, name):
    t = proj_a.shape[0]
    width = w.shape[1]
    tc = _tile(width, 256)
    off = start // tc

    def body(x_ref, w_ref, y_ref):
        x = x_ref[...]
        rows = lax.broadcasted_iota(jnp.int32, x.shape, 0)
        pre = _conv_pre(x, w_ref[...], rows)
        y_ref[...] = pre * jax.nn.sigmoid(pre)

    return pl.pallas_call(
        body, name=name,
        out_shape=jax.ShapeDtypeStruct((t, width), F32),
        grid=(width // tc,),
        in_specs=[pl.BlockSpec((t, tc), lambda j: (0, off + j)), pl.BlockSpec((QK_CONV, tc), lambda j: (0, j))],
        out_specs=pl.BlockSpec((t, tc), lambda j: (0, j)),
        compiler_params=_params("parallel"),
    )(proj_a, w)


def _conv_bwd(proj_a, w, dy, start, name):
    t = proj_a.shape[0]
    width = w.shape[1]
    tc = _tile(width, 256)
    off = start // tc

    def body(x_ref, w_ref, dy_ref, dx_ref, dw_ref):
        x, w = x_ref[...], w_ref[...]
        rows = lax.broadcasted_iota(jnp.int32, x.shape, 0)
        pre = _conv_pre(x, w, rows)
        sig = jax.nn.sigmoid(pre)
        dpre = dy_ref[...] * (sig * (1.0 + pre * (1.0 - sig)))
        dx = dpre * w[QK_CONV - 1:QK_CONV, :]
        dws = [jnp.sum(dpre * x, axis=0, keepdims=True)]
        for s in range(1, QK_CONV):
            dx = dx + _shift_up(dpre, s, rows) * w[QK_CONV - 1 - s:QK_CONV - s, :]
            dws.append(jnp.sum(dpre * _shift_down(x, s, rows), axis=0, keepdims=True))
        dx_ref[...] = dx.astype(dx_ref.dtype)
        for s in range(QK_CONV):
            dw_ref[QK_CONV - 1 - s:QK_CONV - s, :] = dws[s]

    return pl.pallas_call(
        body, name=name,
        out_shape=[jax.ShapeDtypeStruct((t, width), BF16), jax.ShapeDtypeStruct((QK_CONV, width), F32)],
        grid=(width // tc,),
        in_specs=[pl.BlockSpec((t, tc), lambda j: (0, off + j)), pl.BlockSpec((QK_CONV, tc), lambda j: (0, j)),
                  pl.BlockSpec((t, tc), lambda j: (0, j))],
        out_specs=[pl.BlockSpec((t, tc), lambda j: (0, j)), pl.BlockSpec((QK_CONV, tc), lambda j: (0, j))],
        compiler_params=_params("parallel"),
    )(proj_a, w, dy)


def _dot(a, b, dims):
    return lax.dot_general(a.astype(BF16), b.astype(BF16), (dims, ((), ())), preferred_element_type=F32)


def _dot_split(ones, x):
    hi = x.astype(BF16)
    rest = x - hi.astype(F32)
    mid = rest.astype(BF16)
    lo = (rest - mid.astype(F32)).astype(BF16)
    dot = lambda part: lax.dot_general(ones, part, (NN, ((), ())), preferred_element_type=F32)
    return dot(hi) + dot(mid) + dot(lo)


def _chunk_iotas():
    rows = lax.broadcasted_iota(jnp.int32, (CHUNK, CHUNK), 0)
    cols = lax.broadcasted_iota(jnp.int32, (CHUNK, CHUNK), 1)
    return rows, cols


def _gates_fwd(proj_b, bias, col_block, heads, name):
    t = proj_b.shape[0]

    def body(x_ref, b_ref, gb_ref, gbt_ref, cols_ref):
        rows, cols = _chunk_iotas()
        tri = (cols <= rows).astype(BF16)
        for ci in range(t // CHUNK):
            sl = slice(ci * CHUNK, (ci + 1) * CHUNK)
            pre = x_ref[sl, :] + b_ref[...]
            log_f = jnp.minimum(pre, 0.0) - jnp.log(1.0 + jnp.exp(-jnp.abs(pre)))
            log_f = jnp.where((cols >= heads) & (cols < 2 * heads), log_f, 0.0)
            tile = jnp.where(cols < heads, pre, _dot_split(tri, log_f))
            gb_ref[sl, :] = tile
            gbt_ref[:, sl] = tile.T
            for h in range(2 * heads):
                cols_ref[h, sl, :] = jnp.broadcast_to(tile[:, h:h + 1], (CHUNK, LANES))

    return pl.pallas_call(
        body, name=name,
        out_shape=[jax.ShapeDtypeStruct((t, LANES), F32), jax.ShapeDtypeStruct((LANES, t), F32),
                   jax.ShapeDtypeStruct((2 * heads, t, LANES), F32)],
        grid=(1,),
        in_specs=[pl.BlockSpec((t, LANES), lambda i: (0, col_block)), pl.BlockSpec((1, LANES), lambda i: (0, 0))],
        out_specs=[pl.BlockSpec((t, LANES), lambda i: (0, 0)), pl.BlockSpec((LANES, t), lambda i: (0, 0)),
                   pl.BlockSpec((2 * heads, t, LANES), lambda i: (0, 0, 0))],
        compiler_params=_params("arbitrary"),
    )(proj_b, bias)


def _gates_bwd(proj_b, bias, dcol, drow, col_block, heads, name):
    t = proj_b.shape[0]

    def body(x_ref, b_ref, dc_ref, dr_ref, dx_ref, db_ref):
        rows, cols = _chunk_iotas()
        later = (cols >= rows).astype(BF16)
        total = jnp.zeros((1, LANES), F32)
        for ci in range(t // CHUNK):
            sl = slice(ci * CHUNK, (ci + 1) * CHUNK)
            by_row = dr_ref[:, sl].T
            d = dc_ref[sl, :] + jnp.where(cols < heads, by_row,
                                          jnp.where(cols < 2 * heads, -pltpu.roll(by_row, heads, axis=1), 0.0))
            d_log_f = _dot_split(later, jnp.where(cols >= heads, d, 0.0))
            pre = x_ref[sl, :] + b_ref[...]
            dx = jnp.where(cols < heads, d, jnp.where(cols < 2 * heads, d_log_f * jax.nn.sigmoid(-pre), 0.0))
            dx_ref[sl, :] = dx.astype(dx_ref.dtype)
            total = total + jnp.sum(dx, axis=0, keepdims=True)
        db_ref[...] = total

    return pl.pallas_call(
        body, name=name,
        out_shape=[jax.ShapeDtypeStruct((t, LANES), BF16), jax.ShapeDtypeStruct((1, LANES), F32)],
        grid=(1,),
        in_specs=[pl.BlockSpec((t, LANES), lambda i: (0, col_block)), pl.BlockSpec((1, LANES), lambda i: (0, 0)),
                  pl.BlockSpec((t, LANES), lambda i: (0, 0)), pl.BlockSpec((LANES, t), lambda i: (0, 0))],
        out_specs=[pl.BlockSpec((t, LANES), lambda i: (0, 0)), pl.BlockSpec((1, LANES), lambda i: (0, 0))],
        compiler_params=_params("arbitrary"),
    )(proj_b, bias, dcol, drow)


def _gate_views(li_ref, b_ref, gbt_ref, s, head, heads, sl):
    li_c, b_c = li_ref[s, sl, :], b_ref[s, sl, :]
    li_r = jnp.broadcast_to(gbt_ref[head, :, sl], (CHUNK, CHUNK))
    b_r = jnp.broadcast_to(gbt_ref[heads + head, :, sl], (CHUNK, CHUNK))
    return li_c, b_c, li_r, b_r


def _chunk_state(kt, k, v, li_c, b_c, ct, nrow, m_prev):
    b_tot = b_c[CHUNK - 1:CHUNK, 0:1]
    a = b_tot - b_c[:, 0:1] + li_c[:, 0:1]
    m_new = jnp.maximum(b_tot + m_prev, jnp.max(a, axis=0, keepdims=True))
    dec = jnp.exp(b_tot + m_prev - m_new)
    w = jnp.exp(a - m_new)
    wv = w * v
    ct_new = dec * ct + _dot(kt, wv, NN)
    n_new = dec * nrow + jnp.sum(w * k, axis=0, keepdims=True)
    return ct_new, n_new, m_new, dec, w, wv


def _chunk_scores(q, k, b_c, li_r, b_r, m_prev):
    rows, cols = _chunk_iotas()
    dlog = jnp.where(cols <= rows, b_c - b_r + li_r, -jnp.inf)
    inter = b_c[:, 0:1] + m_prev
    mt = jnp.maximum(inter, jnp.max(dlog, axis=-1, keepdims=True))
    dw = jnp.exp(dlog - mt)
    iw = jnp.exp(inter - mt)
    a = _dot(q, k, NT)
    return dw, iw, mt, a, a * dw


def _to_row(column):
    rows, cols = _chunk_iotas()
    return jnp.sum(jnp.where(rows == cols, jnp.broadcast_to(column, (CHUNK, CHUNK)), 0.0), axis=0, keepdims=True)


def _head_out(h, o, g):
    cell, hh, r = _rms_f(h, g)
    sig = jax.nn.sigmoid(o)
    return sig * cell, (cell, hh, r, sig)


HEADS_PER_STEP = 2


def _mlstm_fwd(qc, kc, kct, proj_a, gcols, gbt, g_mlstm, heads, d_model, name, dep):
    t = qc.shape[0]
    dh = g_mlstm.shape[1] // heads
    nc = t // CHUNK
    scale = dh ** -0.5
    hp = HEADS_PER_STEP

    def body(q_ref, k_ref, kt_ref, v_ref, o_ref, li_ref, b_ref, gbt_ref, g_ref, dep_ref, out_ref, ct_out, nm_out,
             ct_ref, n_ref, m_ref):
        del dep_ref
        step = pl.program_id(0)
        ct_ref[...] = jnp.zeros_like(ct_ref)
        n_ref[...] = jnp.zeros_like(n_ref)
        m_ref[...] = jnp.zeros_like(m_ref)
        tile_row = lax.broadcasted_iota(jnp.int32, (8, LANES), 0)

        def chunk(ci, carry):
            sl = pl.ds(pl.multiple_of(ci * CHUNK, CHUNK), CHUNK)
            for s in range(hp):
                cs = slice(s * dh, (s + 1) * dh)
                q, k, v, kt = q_ref[sl, cs], k_ref[sl, cs] * scale, v_ref[sl, cs], kt_ref[cs, sl]
                li_c, b_c, li_r, b_r = _gate_views(li_ref, b_ref, gbt_ref, s, step * hp + s, heads, sl)
                ct, nrow, m_prev = ct_ref[s], n_ref[s, 0:1, :], m_ref[s, 0:1, 0:1]
                ct_out[s, ci] = ct
                nm_out[s, ci] = jnp.where(tile_row == 0, nrow, jnp.where(tile_row == 1, m_prev, 0.0))
                dw, iw, mt, _, sm = _chunk_scores(q, k, b_c, li_r, b_r, m_prev)
                iq = iw * q
                num = _dot(jnp.concatenate([iq, sm], axis=1), jnp.concatenate([ct, v], axis=0), NN)
                den = jnp.sum(iq * nrow + sm, axis=-1, keepdims=True)
                h = num / jnp.maximum(jnp.abs(den), jnp.exp(-mt))
                y, _ = _head_out(h, o_ref[sl, cs], g_ref[:, cs])
                out_ref[sl, cs] = y.astype(out_ref.dtype)
                ct_new, n_new, m_new, _, _, _ = _chunk_state(kt, k, v, li_c, b_c, ct, nrow, m_prev)
                ct_ref[s] = ct_new
                n_ref[s, 0:1, :] = n_new
                m_ref[s, 0:1, 0:1] = m_new
            return carry

        lax.fori_loop(0, nc, chunk, 0, unroll=2)

    col = lambda off: pl.BlockSpec((t, hp * dh), lambda h: (0, off // hp + h))
    slab = lambda off: pl.BlockSpec((hp, t, LANES), lambda h: (off // hp + h, 0, 0))
    return pl.pallas_call(
        body, name=name,
        out_shape=[jax.ShapeDtypeStruct((t, d_model), BF16), jax.ShapeDtypeStruct((heads, nc, dh, dh), F32),
                   jax.ShapeDtypeStruct((heads, nc, 8, LANES), F32)],
        grid=(heads // hp,),
        in_specs=[col(0), col(0), pl.BlockSpec((hp * dh, t), lambda h: (h, 0)), col(2 * heads), col(3 * heads),
                  slab(0), slab(heads), pl.BlockSpec((2 * heads, 1, t), lambda h: (0, 0, 0)),
                  pl.BlockSpec((1, hp * dh), lambda h: (0, h)), ANY_SPEC],
        out_specs=[pl.BlockSpec((t, hp * dh), lambda h: (0, h)),
                   pl.BlockSpec((hp, nc, dh, dh), lambda h: (h, 0, 0, 0)),
                   pl.BlockSpec((hp, nc, 8, LANES), lambda h: (h, 0, 0, 0))],
        scratch_shapes=[pltpu.VMEM((hp, dh, dh), F32), pltpu.VMEM((hp, 8, dh), F32), pltpu.VMEM((hp, 8, LANES), F32)],
        compiler_params=_params("arbitrary"),
    )(qc, kc, kct, proj_a, proj_a, gcols, gcols, gbt, g_mlstm, dep)


def _mlstm_bwd(qc, kc, qct, kct, proj_a, gcols, gbt, g_mlstm, states, dhcat, heads, name, dep):
    t = qc.shape[0]
    wm = g_mlstm.shape[1]
    dh = wm // heads
    nc = t // CHUNK
    scale = dh ** -0.5

    def body(q_ref, k_ref, qt_ref, kt_ref, v_ref, o_ref, li_ref, b_ref, gbt_ref, g_ref, ct_all, nm_all, dy_ref, dep_ref,
             dq_ref, dk_ref, dv_ref, do_ref, dcol_ref, drow_ref, dg_ref, dct_ref, dn_ref):
        del dep_ref
        head = pl.program_id(0)
        g = g_ref[...]

        @pl.when(head == 0)
        def _():
            dcol_ref[...] = jnp.zeros_like(dcol_ref)
            drow_ref[...] = jnp.zeros_like(drow_ref)

        def load(ci):
            sl = pl.ds(pl.multiple_of(ci * CHUNK, CHUNK), CHUNK)
            return (sl, q_ref[sl, :], k_ref[sl, :] * scale, v_ref[sl, :], kt_ref[:, sl]) \
                + _gate_views(li_ref, b_ref, gbt_ref, 0, head, heads, sl)

        dct_ref[...] = jnp.zeros_like(dct_ref)
        dn_ref[...] = jnp.zeros_like(dn_ref)
        dg_ref[...] = jnp.zeros_like(dg_ref)
        rows, cols = _chunk_iotas()
        last = lax.broadcasted_iota(jnp.int32, (CHUNK, 1), 0) == CHUNK - 1
        lane = lax.broadcasted_iota(jnp.int32, (CHUNK, LANES), 1)

        def bwd(step, carry):
            ci = nc - 1 - step
            sl, q, k, v, kt, li_c, b_c, li_r, b_r = load(ci)
            qt = qt_ref[:, sl]
            ct, nrow, m_prev = ct_all[ci], nm_all[ci, 0:1, :], nm_all[ci, 1:2, 0:1]
            dw, iw, mt, a, sm = _chunk_scores(q, k, b_c, li_r, b_r, m_prev)
            qc_ = _dot(q, ct, NN)
            qn = jnp.sum(q * nrow, axis=-1, keepdims=True)
            num = iw * qc_ + _dot(sm, v, NN)
            den = iw * qn + jnp.sum(sm, axis=-1, keepdims=True)
            floor = jnp.exp(-mt)
            dn = jnp.maximum(jnp.abs(den), floor)
            h = num / dn
            _, (cell, hh, r, sig) = _head_out(h, o_ref[sl, :], g)
            dy = dy_ref[sl, :]
            dcell = dy * sig
            do_ref[sl, :] = (dy * cell * sig * (1.0 - sig)).astype(do_ref.dtype)
            dh_, dg = _rms_b(dcell, g, hh, r)
            dg_ref[...] += dg
            dnum = dh_ / dn
            ddn = -jnp.sum(dh_ * h, axis=-1, keepdims=True) / dn
            dden = jnp.where(jnp.abs(den) >= floor, ddn * jnp.sign(den), 0.0)
            idn = iw * dnum
            idd = iw * dden
            dw_t = jnp.exp(jnp.where(cols >= rows, b_r - b_c + li_c, -jnp.inf) - _to_row(mt))
            sm_t = _dot(k, q, NT) * dw_t
            dsm = _dot(dnum, v, NT) + dden
            da_ = dsm * dw
            da_t = (_dot(v, dnum, NT) + _to_row(dden)) * dw_t
            e = da_ * a
            dq = _dot(idn, ct, NT) + _dot(da_, k, NN) + idd * nrow
            dk = _dot(da_t, q, NN)
            dv = _dot(sm_t, dnum, NN)
            dct = _dot(qt, idn, NN)
            dnr = jnp.sum(q * idd, axis=0, keepdims=True)
            db = jnp.sum(e + idn * qc_, axis=-1, keepdims=True) + idd * qn
            _, _, _, dec, w, wv = _chunk_state(kt, k, v, li_c, b_c, ct, nrow, m_prev)
            dct_new, dn_new = dct_ref[...], dn_ref[0:1, :]
            ddec = jnp.sum(jnp.sum(dct_new * ct, axis=-1, keepdims=True), axis=0, keepdims=True) \
                + jnp.sum(dn_new * nrow, axis=-1, keepdims=True)
            dk = dk + _dot(wv, dct_new, NT) + w * dn_new
            dwv = _dot(k, dct_new, NN)
            dv = dv + w * dwv
            dwt = jnp.sum(dwv * v + k * dn_new, axis=-1, keepdims=True)
            da = dwt * w
            dbtot = jnp.sum(da, axis=0, keepdims=True) + ddec * dec
            db = db - da + jnp.where(last, dbtot, 0.0)
            dct_ref[...] = dec * dct_new + dct
            dn_ref[0:1, :] = dec * dn_new + dnr
            dq_ref[sl, :] = dq
            dk_ref[sl, :] = dk * scale
            dv_ref[sl, :] = dv.astype(dv_ref.dtype)
            dcol_ref[sl, :] += jnp.where(lane == head, da, 0.0) + jnp.where(lane == heads + head, db, 0.0)
            drow_ref[head, :, sl] = jnp.sum(e, axis=0, keepdims=True)
            return carry

        lax.fori_loop(0, nc, bwd, 0, unroll=8)

    col = lambda off: pl.BlockSpec((t, dh), lambda h: (0, off + h))
    rowb = pl.BlockSpec((dh, t), lambda h: (h, 0))
    slab = lambda off: pl.BlockSpec((1, t, LANES), lambda h: (off + h, 0, 0))
    vec = pl.BlockSpec((1, dh), lambda h: (0, h))
    return pl.pallas_call(
        body, name=name,
        out_shape=[jax.ShapeDtypeStruct((t, wm), F32)] * 2 + [jax.ShapeDtypeStruct((t, wm), BF16)] * 2
        + [jax.ShapeDtypeStruct((t, LANES), F32), jax.ShapeDtypeStruct((heads, 1, t), F32),
           jax.ShapeDtypeStruct((1, wm), F32)],
        grid=(heads,),
        in_specs=[col(0), col(0), rowb, rowb, col(2 * heads), col(3 * heads), slab(0), slab(heads),
                  pl.BlockSpec((2 * heads, 1, t), lambda h: (0, 0, 0)), vec,
                  pl.BlockSpec((None, nc, dh, dh), lambda h: (h, 0, 0, 0)),
                  pl.BlockSpec((None, nc, 8, LANES), lambda h: (h, 0, 0, 0)), col(0), ANY_SPEC],
        out_specs=[col(0), col(0), col(0), col(0), pl.BlockSpec((t, LANES), lambda h: (0, 0)),
                   pl.BlockSpec((heads, 1, t), lambda h: (0, 0, 0)), vec],
        scratch_shapes=[pltpu.VMEM((dh, dh), F32), pltpu.VMEM((8, dh), F32)],
        compiler_params=_params("arbitrary"),
    )(qc, kc, qct, kct, proj_a, proj_a, gcols, gcols, gbt, g_mlstm, *states, dhcat, dep)


def _pool_window(gi):
    return jnp.where(gi == 0, 2.0, jnp.where(gi == 1, 4.0, jnp.where(gi == 2, 8.0, 16.0))).astype(F32)


def _pool_select(gi, levels):
    return jnp.where(gi == 0, levels[0], jnp.where(gi == 1, levels[1], jnp.where(gi == 2, levels[2], levels[3])))


def _pooled(u, gi, rows):
    s, levels = u, []
    for lvl in range(POOL_GROUPS):
        s = s + _shift_down(s, 1 << lvl, rows)
        levels.append(s)
    cnt = jnp.minimum((rows + 1).astype(F32), _pool_window(gi))
    return _pool_select(gi, levels) / cnt - u, cnt


def _pool_fwd(proj_b, w_pool, scale, hcat, name):
    t, d_model = hcat.shape
    g, cg, _ = w_pool.shape
    off = (d_model - g * cg) // cg

    def body(u_ref, w_ref, s_ref, hcat_ref, out_ref):
        del hcat_ref
        gi = pl.program_id(0)
        u = u_ref[...]
        rows = lax.broadcasted_iota(jnp.int32, u.shape, 0)
        pooled, _ = _pooled(u, gi, rows)
        out_ref[...] = (_dot(pooled, w_ref[...], NN) * s_ref[...]).astype(out_ref.dtype)

    return pl.pallas_call(
        body, name=name,
        out_shape=jax.ShapeDtypeStruct((t, d_model), BF16),
        grid=(g,),
        in_specs=[pl.BlockSpec((t, cg), lambda gi: (0, gi)), pl.BlockSpec((None, cg, cg), lambda gi: (gi, 0, 0)),
                  pl.BlockSpec((1, cg), lambda gi: (0, gi)), HBM_SPEC],
        out_specs=pl.BlockSpec((t, cg), lambda gi: (0, off + gi)),
        input_output_aliases={3: 0},
        compiler_params=_params("arbitrary"),
    )(proj_b, w_pool, scale, hcat)


def _pool_bwd(proj_b, w_pool, scale, dhcat, name):
    t, d_model = dhcat.shape
    g, cg, _ = w_pool.shape
    off = (d_model - g * cg) // cg

    def body(u_ref, w_ref, s_ref, dy_ref, du_ref, dw_ref, ds_ref):
        gi = pl.program_id(0)
        u, w, dy = u_ref[...], w_ref[...], dy_ref[...]
        rows = lax.broadcasted_iota(jnp.int32, u.shape, 0)
        pooled, cnt = _pooled(u, gi, rows)
        mixed = _dot(pooled, w, NN)
        ds_ref[...] = jnp.sum(dy * mixed, axis=0, keepdims=True)
        dmixed = dy * s_ref[...]
        dw_ref[...] = _dot(pooled, dmixed, TN).astype(dw_ref.dtype)
        dpooled = _dot(dmixed, w, NT)
        s, levels = dpooled / cnt, []
        for lvl in range(POOL_GROUPS):
            s = s + _shift_up(s, 1 << lvl, rows)
            levels.append(s)
        du_ref[...] = (_pool_select(gi, levels) - dpooled).astype(du_ref.dtype)

    return pl.pallas_call(
        body, name=name,
        out_shape=[jax.ShapeDtypeStruct((t, g * cg), BF16), jax.ShapeDtypeStruct((g, cg, cg), BF16),
                   jax.ShapeDtypeStruct((1, g * cg), F32)],
        grid=(g,),
        in_specs=[pl.BlockSpec((t, cg), lambda gi: (0, gi)), pl.BlockSpec((None, cg, cg), lambda gi: (gi, 0, 0)),
                  pl.BlockSpec((1, cg), lambda gi: (0, gi)), pl.BlockSpec((t, cg), lambda gi: (0, off + gi))],
        out_specs=[pl.BlockSpec((t, cg), lambda gi: (0, gi)), pl.BlockSpec((None, cg, cg), lambda gi: (gi, 0, 0)),
                   pl.BlockSpec((1, cg), lambda gi: (0, gi))],
        compiler_params=_params("arbitrary"),
    )(proj_b, w_pool, scale, dhcat)


def _pad_cols(a, width):
    return jnp.pad(a, ((0, 0), (0, width - a.shape[1])))


def kernel(x, p, w_in, b_gates, w_qk_conv, g_mlstm, w_pool, pool_scale, w_out, g_mix_pre, g_mix_post, w_ff1, w_ff2, g_ff_pre, g_ff_post, w_ple_proj, w_ple_gate, g_ple_gate, g_ple_post, loss_target, m_w_in, m_b_gates, m_w_qk_conv, m_g_mlstm, m_w_pool, m_pool_scale, m_w_out, m_g_mix_pre, m_g_mix_post, m_w_ff1, m_w_ff2, m_g_ff_pre, m_g_ff_post, m_w_ple_proj, m_w_ple_gate, m_g_ple_gate, m_g_ple_post, v_w_in, v_b_gates, v_w_qk_conv, v_g_mlstm, v_w_pool, v_pool_scale, v_w_out, v_g_mix_pre, v_g_mix_post, v_w_ff1, v_w_ff2, v_g_ff_pre, v_g_ff_post, v_w_ple_proj, v_w_ple_gate, v_g_ple_gate, v_g_ple_post):
    weights = dict(w_in=w_in, b_gates=b_gates, w_qk_conv=w_qk_conv, g_mlstm=g_mlstm, w_pool=w_pool,
                   pool_scale=pool_scale, w_out=w_out, g_mix_pre=g_mix_pre, g_mix_post=g_mix_post, w_ff1=w_ff1,
                   w_ff2=w_ff2, g_ff_pre=g_ff_pre, g_ff_post=g_ff_post, w_ple_proj=w_ple_proj,
                   w_ple_gate=w_ple_gate, g_ple_gate=g_ple_gate, g_ple_post=g_ple_post)
    mom1 = dict(w_in=m_w_in, b_gates=m_b_gates, w_qk_conv=m_w_qk_conv, g_mlstm=m_g_mlstm, w_pool=m_w_pool,
                pool_scale=m_pool_scale, w_out=m_w_out, g_mix_pre=m_g_mix_pre, g_mix_post=m_g_mix_post,
                w_ff1=m_w_ff1, w_ff2=m_w_ff2, g_ff_pre=m_g_ff_pre, g_ff_post=m_g_ff_post,
                w_ple_proj=m_w_ple_proj, w_ple_gate=m_w_ple_gate, g_ple_gate=m_g_ple_gate, g_ple_post=m_g_ple_post)
    mom2 = dict(w_in=v_w_in, b_gates=v_b_gates, w_qk_conv=v_w_qk_conv, g_mlstm=v_g_mlstm, w_pool=v_w_pool,
                pool_scale=v_pool_scale, w_out=v_w_out, g_mix_pre=v_g_mix_pre, g_mix_post=v_g_mix_post,
                w_ff1=v_w_ff1, w_ff2=v_w_ff2, g_ff_pre=v_g_ff_pre, g_ff_post=v_g_ff_post,
                w_ple_proj=v_w_ple_proj, w_ple_gate=v_w_ple_gate, g_ple_gate=v_g_ple_gate, g_ple_post=v_g_ple_post)
    order = list(weights)

    t, d = x.shape[1], x.shape[2]
    heads = MLSTM_HEADS
    wm = g_mlstm.shape[1]
    pw = pool_scale.shape[1]
    cg = pw // POOL_GROUPS
    in_cols = 4 * wm + 2 * heads + pw
    in_shard = w_in.shape[2]
    xs, ps, tgt = x[0], p[0, 0], loss_target[0]

    mx, my, mc = lax.axis_index("x"), lax.axis_index("y"), lax.axis_index("c")
    core = jnp.reshape(mc, (1,)).astype(jnp.int32)
    chip = jnp.reshape(2 * mx + my, (1,)).astype(jnp.int32)

    first = [w_in[0].T.astype(BF16), w_qk_conv[0], w_pool[0].reshape(POOL_GROUPS * w_pool.shape[2], cg).astype(BF16)]
    started, token = _gather_start([first], "gather_start_in")
    anchor = token[0, 0]
    later = lambda w: (w[0] + anchor).astype(BF16)
    rest, token = _gather_start([[later(w_out)], [later(w_ff1)], [later(w_ff2)], [later(w_ple_gate), later(w_ple_proj)]],
                                "gather_start_rest")
    started = started + rest
    bias = _pad_cols(b_gates, LANES)
    small = ["b_gates", "g_mlstm", "pool_scale", "g_mix_pre", "g_mix_post", "g_ff_pre", "g_ff_post", "g_ple_gate",
             "g_ple_post"]
    pack = lambda src: _pad_rows16(jnp.concatenate([_pad_cols(src[k] + anchor, d) for k in small], axis=0))
    small_state = [pack(weights), pack(mom1), pack(mom2)]
    in_state = [a[0].T + anchor for a in (w_in, m_w_in, v_w_in)]

    xn = _norm_fwd(xs, g_mix_pre, "norm_mix_pre", deps=[token])
    g_in, g_conv, g_pool = _forward_pair(
        _gather_wait(started[0], [xn] + small_state + in_state, "gather_wait_in"), "gather_pair_in")
    in_t_a = g_in.reshape(in_cols, d)
    in_t_b = jnp.concatenate([in_t_a[4 * wm + 2 * heads:], in_t_a[4 * wm:4 * wm + 2 * heads],
                              jnp.zeros((LANES - 2 * heads, d), BF16)], axis=0)
    conv_w = g_conv.transpose(1, 0, 2).reshape(QK_CONV, 2 * wm)
    pool_w = g_pool.reshape(N_DEV, POOL_GROUPS, cg // N_DEV, cg).transpose(1, 0, 2, 3).reshape(POOL_GROUPS, cg, cg)
    proj_a = _mm_nt("proj_qkvo", xn, in_t_a, n_rows=4 * wm)
    proj_b = _mm_nt("proj_pool_gates", xn, in_t_b)
    qc = _conv_fwd(proj_a, conv_w[:, :wm], 0, "q_conv")
    kc = _conv_fwd(proj_a, conv_w[:, wm:], wm, "k_conv")
    gb, gbt, gcols = _gates_fwd(proj_b, bias, pw // LANES, heads, "gates")
    gbt = gbt[:2 * heads].reshape(2 * heads, 1, t)
    qct, kct = qc.T.astype(BF16), (kc * (wm // heads) ** -0.5).T.astype(BF16)
    fwd_out = _forward_start(_gather_wait(started[1], gb, "gather_wait_out"), "gather_pair_start_out")
    hcat, *mlstm_states = _mlstm_fwd(qc, kc, kct, proj_a, gcols, gbt, g_mlstm, heads, d, "mlstm", fwd_out["token"])
    hcat = _pool_fwd(proj_b, pool_w, pool_scale, hcat, "pool")
    (g_out,) = _forward_wait(fwd_out, hcat, "gather_pair_wait_out")
    out_w = g_out.reshape(d, d)
    mix = _mm_nn("mix_out", hcat, out_w)[0]
    fwd_ff1 = _forward_start(_gather_wait(started[2], mix, "gather_wait_ff1"), "gather_pair_start_ff1")
    h1, hn = _block_fwd(xs, mix, g_mix_post, g_ff_pre, "residual_mix", deps=[fwd_ff1["token"]])
    (g_ff1,) = _forward_wait(fwd_ff1, hn, "gather_pair_wait_ff1")
    relu_a, act = _mm_nn_pieces("ff1", hn, g_ff1, out_dtype=[BF16, BF16],
                                epilogue=lambda acc: (jnp.maximum(acc, 0.0), jnp.square(jnp.maximum(acc, 0.0))))
    (g_ff2,) = _forward_pair(_gather_wait(started[3], act, "gather_wait_ff2"), "gather_pair_ff2")
    ff2_w = g_ff2.reshape(-1, d)
    ff = _mm_nn("ff2", act, ff2_w, tn=256, tk=4 * d)[0]
    fwd_ple = _forward_start(_gather_wait(started[4], ff, "gather_wait_ple"), "gather_pair_start_ple")
    h2, hg = _block_fwd(h1, ff, g_ff_post, g_ple_gate, "residual_ff", deps=[fwd_ple["token"]])
    g_pgate, g_pproj = _forward_wait(fwd_ple, hg, "gather_pair_wait_ple")
    pgate_w = g_pgate.reshape(d, d)
    z = _mm_nn("ple_gate", hg, pgate_w)[0]
    e = _mm_nn_pieces("ple_proj", ps, g_pproj)[0]
    dh3, dz, de, dg_ple_post, loss_part = _loss_head(h2, z, e, tgt, g_ple_post, "loss_head")
    loss = lax.psum(loss_part[0, 0], MESH_AXES)

    out = {}

    def state(k):
        if k == "w_in":
            return in_state
        return [a.reshape(-1, a.shape[-1]) for a in (weights[k], mom1[k], mom2[k])]

    def pair_begin(pieces, tag):
        keys = list(pieces)
        return keys, _pair_start([pieces[k] for k in keys], "reduce_pair_start_" + tag)

    def chip_begin(group, after, tag):
        keys, begun = group
        mine, got = _pair_wait(begun, after, "reduce_pair_wait_" + tag)
        sums = [_pair_sum(p_, g, core, "pair_sum_" + k) for k, p_, g in zip(keys, mine, got)]
        return keys, _chip_start(sums, "reduce_start_" + tag)

    def reduce_finish(group, after, tag):
        keys, begun = group
        sums, got = _chip_wait(begun, after, "reduce_wait_" + tag)
        for k, s, g in zip(keys, sums, got):
            res = _adamw_shard(*state(k), s, g, chip, "adamw_" + k)
            out[k] = [r.T[None] if k == "w_in" else r.reshape(weights[k].shape) for r in res]

    token_of = lambda group: group[1]["token"]
    dhg = _mm_nt("d_ple_gate_in", dz, pgate_w)
    gw_pgate = _mm_tn("gw_ple_gate", hg, dz)
    gw_pproj = _mm_tn("gw_ple_proj", ps, de, pieces=N_DEV)
    dh2, dff, dg_ff_post, dg_ple_gate = _block_bwd(h2, ff, dh3, dhg, g_ff_post, g_ple_gate, "d_residual_ff")
    da = _mm_nt("d_ff2_in", dff, ff2_w, out_dtype=BF16, extra=[relu_a],
                epilogue=lambda acc, r: (acc * (2.0 * r.astype(F32)),))
    gw_ff2 = _mm_tn("gw_ff2", act, dff)
    ff2_keys = ["w_ff2", "w_ple_gate", "w_ple_proj"]
    pair_ff2 = pair_begin({"w_ff2": gw_ff2.reshape(N_DEV, -1, d),
                           "w_ple_gate": gw_pgate.reshape(N_DEV, d // N_DEV, d), "w_ple_proj": gw_pproj}, "ff2")
    dhn = _mm_nt_pieces("d_ff1_in", da, g_ff1, deps=[token_of(pair_ff2)])
    place = jnp.concatenate([core, chip])
    mine_ff2, got_ff2 = _pair_wait(pair_ff2[1], dhn, "reduce_pair_wait_ff2")
    gw_ff1, sums_ff2 = _mm_tn("gw_ff1", hn, da, pieces=N_DEV, side=_side_pair_sum(mine_ff2[0], got_ff2[0], place))
    sums_ple = [_pair_sum(p_, g, core, "pair_sum_" + k) for k, p_, g in zip(ff2_keys[1:], mine_ff2[1:], got_ff2[1:])]
    red_ff2 = (ff2_keys, _chip_start([sums_ff2] + sums_ple, "reduce_start_ff2"))
    pair_ff1 = pair_begin({"w_ff1": gw_ff1}, "ff1")
    dh1, dmix, dg_mix_post, dg_ff_pre = _block_bwd(h1, mix, dh2, dhn, g_mix_post, g_ff_pre, "d_residual_mix",
                                                   deps=[token_of(pair_ff1), token_of(red_ff2)])
    mine_ff1, got_ff1 = _pair_wait(pair_ff1[1], dmix, "reduce_pair_wait_ff1")
    dhcat, sums_ff1 = _mm_nt("d_mix_in", dmix, out_w, tm=t // 2, tn=d // 2,
                             side=_side_pair_sum(mine_ff1[0], got_ff1[0], place))
    red_ff1 = (["w_ff1"], _chip_start([sums_ff1], "reduce_start_ff1"))
    gw_out = _mm_tn("gw_out", hcat, dmix, deps=[token_of(red_ff1)])
    du, gw_pool, dg_pool_scale = _pool_bwd(proj_b, pool_w, pool_scale, dhcat, "d_pool")
    pair_out = pair_begin({
        "w_out": gw_out.reshape(N_DEV, d // N_DEV, d),
        "w_pool": gw_pool.reshape(POOL_GROUPS, N_DEV, cg // N_DEV, cg).transpose(1, 0, 2, 3).reshape(N_DEV, -1, cg)},
        "out")
    dq, dk, dv, do, dcol, drow, dg_mlstm = _mlstm_bwd(qc, kc, qct, kct, proj_a, gcols, gbt, g_mlstm, mlstm_states, dhcat,
                                                      heads, "d_mlstm", token_of(pair_out))
    red_out = chip_begin(pair_out, dq, "out")
    drow = jnp.pad(drow.reshape(heads, t), ((0, LANES - heads), (0, 0)))
    dgates, dbias = _gates_bwd(proj_b, bias, dcol, drow, pw // LANES, heads, "d_gates")
    dq_pre, gw_conv_q = _conv_bwd(proj_a, conv_w[:, :wm], dq, 0, "d_q_conv")
    dk_pre, gw_conv_k = _conv_bwd(proj_a, conv_w[:, wm:], dk, wm, "d_k_conv")
    gw_conv = jnp.concatenate([gw_conv_q, gw_conv_k], axis=1)
    dproj_a = jnp.concatenate([dq_pre, dk_pre, dv, do], axis=1)
    dproj_b = jnp.concatenate([du, dgates], axis=1)
    sums_ff2, got_ff2 = _chip_wait(red_ff2[1], token_of(red_out), "reduce_wait_ff2")
    gw_a, *update = _mm_tn("gw_in_qkvo", dproj_a, xn,
                           side=_side_adamw(*state("w_ff2"), sums_ff2[0], got_ff2[0], place))
    out["w_ff2"] = [r.reshape(weights["w_ff2"].shape) for r in update]
    for k, s, g in zip(ff2_keys[1:], sums_ff2[1:], got_ff2[1:]):
        out[k] = [r.reshape(weights[k].shape) for r in _adamw_shard(*state(k), s, g, chip, "adamw_" + k)]
    gw_b = _mm_tn("gw_in_pool_gates", dproj_b, xn)
    gw_in_t = jnp.concatenate([gw_a, gw_b[pw:pw + 2 * heads], gw_b[:pw]], axis=0)
    pair_in = pair_begin({
        "w_in": gw_in_t.reshape(N_DEV, in_shard, d),
        "w_qk_conv": gw_conv.reshape(QK_CONV, N_DEV, -1).transpose(1, 0, 2).astype(BF16)}, "in")
    reduce_finish(red_out, token_of(pair_in), "out")
    red_in = chip_begin(pair_in, out["w_out"][0], "in")
    sums_ff1, got_ff1 = _chip_wait(red_ff1[1], token_of(red_in), "reduce_wait_ff1")
    dxn_a, *update = _mm_nn("d_proj_qkvo_in", dproj_a, in_t_a, tn=512, tk=4 * wm, deps=[token_of(red_in)],
                            side=_side_adamw(*state("w_ff1"), sums_ff1[0], got_ff1[0], place))
    out["w_ff1"] = [r.reshape(weights["w_ff1"].shape) for r in update]
    dxn = _mm_nn("d_proj_in", dproj_b, in_t_b, extra=[dxn_a], epilogue=lambda acc, prev: (acc + prev,))[0]
    grad_x, dg_mix_pre = _norm_bwd(xs, dxn, dh1, g_mix_pre, "d_norm_mix_pre")

    small_grads = dict(b_gates=dbias[:, :2 * heads], g_mlstm=dg_mlstm, pool_scale=dg_pool_scale, g_mix_pre=dg_mix_pre,
                       g_mix_post=dg_mix_post, g_ff_pre=dg_ff_pre, g_ff_post=dg_ff_post, g_ple_gate=dg_ple_gate,
                       g_ple_post=dg_ple_post)
    small_begun = _everyone_start([pack(small_grads)], "small_grads_start")
    reduce_finish(red_in, small_begun["token"], "in")
    (small_parts,) = _everyone_wait(small_begun, out["w_in"][0], "small_grads_wait")
    res = _adamw_replicated(*small_state, small_parts, "adamw_replicated")
    for i, k in enumerate(small):
        width = weights[k].shape[1]
        out[k] = [r[i:i + 1, :width] for r in res]

    return (loss, grad_x[None], *[out[k][0] for k in order], *[out[k][1] for k in order],
            *[out[k][2] for k in order], *[out[k][3] for k in order])


def _pad_rows16(a):
    return jnp.pad(a, ((0, 16 - a.shape[0]), (0, 0)))
```

```python
import jax
import jax.numpy as jnp
from jax import lax
from jax.experimental import pallas as pl
from jax.experimental.pallas import tpu as pltpu

F32 = jnp.float32
BF16 = jnp.bfloat16
EPS = 1e-6
N_DEV = 8
N_CHIP = 4
LANES = 128
VMEM_LIMIT = 56 * 1024 * 1024
MLSTM_HEADS = 8
POOL_GROUPS = 4
QK_CONV = 4
CHUNK = 128
ROW_TILE = 256
ADAM_LR, ADAM_B1, ADAM_B2, ADAM_EPS, ADAM_WD, ADAM_STEP = 0.001, 0.9, 0.999, 1e-08, 0.01, 10
MESH_AXES = ("x", "y", "c")
MESH = pl.DeviceIdType.MESH
HBM_SPEC = pl.BlockSpec(memory_space=pltpu.HBM)
SEM_SPEC = pl.BlockSpec(memory_space=pltpu.SEMAPHORE)
DATAFLOW = pltpu.SideEffectType.DATAFLOW_SIDE_EFFECTING


def _params(*sem):
    if sem:
        return pltpu.CompilerParams(dimension_semantics=sem, vmem_limit_bytes=VMEM_LIMIT)
    return pltpu.CompilerParams(vmem_limit_bytes=VMEM_LIMIT)


def _tile(n, pref):
    if n <= pref:
        return n
    t = (pref // LANES) * LANES
    while t >= LANES:
        if n % t == 0:
            return t
        t -= LANES
    return n


def _hbm(a):
    return pltpu.with_memory_space_constraint(a, pltpu.HBM)


def _mesh_place():
    x, y, c = lax.axis_index("x"), lax.axis_index("y"), lax.axis_index("c")
    return x, y, c, [(1 - x, y), (x, 1 - y), (1 - x, 1 - y)]


def _chip_copies(srcs, lands, send, recv):
    x, y, c, chips = _mesh_place()
    return [pltpu.make_async_remote_copy(
        src_ref=srcs[w].at[2 * chip[0] + chip[1]], dst_ref=lands[w].at[k],
        send_sem=send.at[3 * w + k], recv_sem=recv.at[3 * w + k],
        device_id=(*chip, c), device_id_type=MESH)
        for k, chip in enumerate(chips) for w in range(len(srcs))]


def _chip_start(sums, name):
    return _split_start(_chip_copies, sums, [lax.empty((3,) + a.shape[1:], a.dtype) for a in sums], 3, name,
                        peers=_same_core_of_other_chips)


def _chip_wait(started, after, name):
    return _split_wait(_chip_copies, started, after, name)


def _pair_start(pieces, name):
    return _split_start(_pair_copies, pieces, [lax.empty((N_CHIP,) + a.shape[1:], a.dtype) for a in pieces], N_CHIP, name,
                        peers=_sibling)


def _pair_wait(started, after, name):
    return _split_wait(_pair_copies, started, after, name)


def _pair_copies(srcs, lands, send, recv):
    x, y, c, _ = _mesh_place()
    return [pltpu.make_async_remote_copy(
        src_ref=srcs[w].at[2 * ch + 1 - c], dst_ref=lands[w].at[ch],
        send_sem=send.at[N_CHIP * w + ch], recv_sem=recv.at[N_CHIP * w + ch],
        device_id=(x, y, 1 - c), device_id_type=MESH)
        for w in range(len(srcs)) for ch in range(N_CHIP)]


OWN_BARRIER_IDS = {name: i for i, name in enumerate([
    "reduce_pair_start_ff2", "reduce_pair_start_ff1", "reduce_pair_start_out", "reduce_pair_start_in",
    "gather_pair_start_out", "gather_pair_start_ff1", "gather_pair_start_ple",
    "reduce_start_ff2", "reduce_start_ff1", "reduce_start_out", "reduce_start_in"])}


def _sibling():
    x, y, c, _ = _mesh_place()
    return [(x, y, 1 - c)]


def _same_core_of_other_chips():
    x, y, c, chips = _mesh_place()
    return [(*chip, c) for chip in chips]


def _split_start(copies_of, srcs, lands, per_array, name, peers=None):
    ns, nb = len(srcs), len(srcs) + len(lands)
    n_copies = per_array * len(lands)

    def body(*refs):
        if peers is not None:
            barrier = pltpu.get_barrier_semaphore()
            others = peers()
            for peer in others:
                pl.semaphore_signal(barrier, inc=1, device_id=peer, device_id_type=MESH)
            pl.semaphore_wait(barrier, len(others))
        for cp in copies_of(refs[:ns], refs[ns:nb], refs[nb], refs[nb + 1]):
            cp.start()
        refs[-1][...] = jnp.zeros_like(refs[-1])

    params = dict(has_side_effects=DATAFLOW)
    if peers is not None:
        params["collective_id"] = OWN_BARRIER_IDS[name]
    res = pl.pallas_call(
        body, name=name,
        out_shape=[pltpu.SemaphoreType.DMA((n_copies,)), pltpu.SemaphoreType.DMA((n_copies,))]
        + [pltpu.HBM(a.shape, a.dtype) for a in list(srcs) + list(lands)] + [jax.ShapeDtypeStruct((8, LANES), F32)],
        in_specs=[HBM_SPEC] * nb,
        out_specs=[SEM_SPEC, SEM_SPEC] + [HBM_SPEC] * nb + [pl.BlockSpec(memory_space=pltpu.VMEM)],
        input_output_aliases={i: 2 + i for i in range(nb)},
        compiler_params=pltpu.CompilerParams(**params),
    )(*[_hbm(a) for a in srcs], *[_hbm(a) for a in lands])
    return dict(send=res[0], recv=res[1], srcs=list(res[2:2 + ns]), lands=list(res[2 + ns:2 + nb]), token=res[-1])


def _split_wait(copies_of, started, after, name):
    ns = len(started["srcs"])
    nb = ns + len(started["lands"])

    def body(*refs):
        for cp in copies_of(refs[:ns], refs[ns:nb], refs[nb], refs[nb + 1]):
            cp.wait_send()
            cp.wait_recv()

    res = pl.pallas_call(
        body, name=name,
        out_shape=[pltpu.HBM(a.shape, a.dtype) for a in started["srcs"] + started["lands"]],
        in_specs=[HBM_SPEC] * nb + [SEM_SPEC, SEM_SPEC, ANY_SPEC],
        out_specs=[HBM_SPEC] * nb,
        input_output_aliases={i: i for i in range(nb)},
        compiler_params=pltpu.CompilerParams(has_side_effects=DATAFLOW),
    )(*started["srcs"], *started["lands"], started["send"], started["recv"], after)
    return list(res[:ns]), list(res[ns:])


def _forward_copies(srcs, lands, send, recv):
    del srcs
    x, y, c, chips = _mesh_place()
    copies = []
    for w in range(len(lands)):
        for j, chip in enumerate(chips):
            rows = lands[w].at[4 * chip[0] + 2 * chip[1] + c]
            copies.append(pltpu.make_async_remote_copy(
                src_ref=rows, dst_ref=rows, send_sem=send.at[3 * w + j], recv_sem=recv.at[3 * w + j],
                device_id=(x, y, 1 - c), device_id_type=MESH))
    return copies


def _forward_start(lands, name):
    return _split_start(_forward_copies, [], lands, 3, name, peers=_sibling)


def _forward_wait(started, after, name):
    return _split_wait(_forward_copies, started, after, name)[1]


def _everyone_copies(srcs, lands, send, recv):
    x, y, c, _ = _mesh_place()
    me = 4 * x + 2 * y + c
    copies = []
    for w in range(len(srcs)):
        for k in range(N_DEV - 1):
            flip = k + 1
            peer = (1 - x if flip & 4 else x, 1 - y if flip & 2 else y, 1 - c if flip & 1 else c)
            copies.append(pltpu.make_async_remote_copy(
                src_ref=srcs[w], dst_ref=lands[w].at[me], send_sem=send.at[7 * w + k], recv_sem=recv.at[7 * w + k],
                device_id=peer, device_id_type=MESH))
    return copies


def _everyone_start(blocks, name):
    me = 4 * lax.axis_index("x") + 2 * lax.axis_index("y") + lax.axis_index("c")
    lands = [lax.dynamic_update_slice(lax.empty((N_DEV,) + a.shape, a.dtype), a[None], (me,) + (0,) * a.ndim)
             for a in blocks]
    return _split_start(_everyone_copies, blocks, lands, N_DEV - 1, name)


def _everyone_wait(started, after, name):
    return _split_wait(_everyone_copies, started, after, name)[1]


def _gather_copies(srcs, lands, send, recv):
    x, y, c, chips = _mesh_place()
    me = 4 * x + 2 * y + c
    targets = [(x, y, 1 - c)] + [(*chip, c) for chip in chips]
    return [pltpu.make_async_remote_copy(
        src_ref=srcs[w], dst_ref=lands[w].at[me], send_sem=send.at[4 * w + k], recv_sem=recv.at[4 * w + k],
        device_id=to, device_id_type=MESH)
        for w in range(len(srcs)) for k, to in enumerate(targets)]


def _gather_start(groups, name):
    sizes = [len(g) for g in groups]
    flat = [a for g in groups for a in g]
    n, ng = len(flat), len(groups)
    me = 4 * lax.axis_index("x") + 2 * lax.axis_index("y") + lax.axis_index("c")
    lands = [lax.dynamic_update_slice(lax.empty((N_DEV,) + a.shape, a.dtype), a[None], (me,) + (0,) * a.ndim)
             for a in flat]

    def body(*refs):
        srcs, zones = refs[:n], refs[n:2 * n]
        sems = refs[2 * n:2 * n + 2 * ng]
        token = refs[-1]
        lo = 0
        for gi, size in enumerate(sizes):
            for cp in _gather_copies(srcs[lo:lo + size], zones[lo:lo + size], sems[2 * gi], sems[2 * gi + 1]):
                cp.start()
            lo += size
        token[...] = jnp.zeros_like(token)

    sem_shapes = []
    for size in sizes:
        sem_shapes += [pltpu.SemaphoreType.DMA((4 * size,))] * 2
    res = pl.pallas_call(
        body, name=name,
        out_shape=sem_shapes + [pltpu.HBM(a.shape, a.dtype) for a in flat + lands] + [jax.ShapeDtypeStruct((8, LANES), F32)],
        in_specs=[HBM_SPEC] * (2 * n),
        out_specs=[SEM_SPEC] * (2 * ng) + [HBM_SPEC] * (2 * n) + [pl.BlockSpec(memory_space=pltpu.VMEM)],
        input_output_aliases={i: 2 * ng + i for i in range(2 * n)},
        compiler_params=pltpu.CompilerParams(has_side_effects=DATAFLOW),
    )(*[_hbm(a) for a in flat], *[_hbm(a) for a in lands])
    out, lo = [], 0
    for gi, size in enumerate(sizes):
        out.append(dict(send=res[2 * gi], recv=res[2 * gi + 1],
                        srcs=list(res[2 * ng + lo:2 * ng + lo + size]),
                        lands=list(res[2 * ng + n + lo:2 * ng + n + lo + size])))
        lo += size
    return out, res[-1]


def _gather_wait(started, after, name):
    n = len(started["srcs"])
    after = list(after) if isinstance(after, (list, tuple)) else [after]

    def body(*refs):
        srcs, zones = refs[:n], refs[n:2 * n]
        send, recv = refs[2 * n], refs[2 * n + 1]
        for cp in _gather_copies(srcs, zones, send, recv):
            cp.wait_send()
            cp.wait_recv()

    res = pl.pallas_call(
        body, name=name,
        out_shape=[pltpu.HBM(a.shape, a.dtype) for a in started["srcs"] + started["lands"]],
        in_specs=[HBM_SPEC] * (2 * n) + [SEM_SPEC, SEM_SPEC] + [ANY_SPEC] * len(after),
        out_specs=[HBM_SPEC] * (2 * n),
        input_output_aliases={i: i for i in range(2 * n)},
        compiler_params=pltpu.CompilerParams(has_side_effects=DATAFLOW),
    )(*started["srcs"], *started["lands"], started["send"], started["recv"], *after)
    return list(res[n:])


def _forward_pair(lands, name):
    n = len(lands)

    def body(*refs):
        zones = refs[:n]
        send, recv = refs[2 * n:]
        x, y, c, chips = _mesh_place()

        def copy(w, j, core):
            rows = zones[w].at[4 * chips[j][0] + 2 * chips[j][1] + core]
            return pltpu.make_async_remote_copy(
                src_ref=rows, dst_ref=rows, send_sem=send.at[3 * w + j], recv_sem=recv.at[3 * w + j],
                device_id=(x, y, 1 - c), device_id_type=MESH)

        sends = [copy(w, j, c) for w in range(n) for j in range(3)]
        for cp in sends:
            cp.start()
        for w in range(n):
            for j in range(3):
                copy(w, j, 1 - c).wait_recv()
        for cp in sends:
            cp.wait_send()

    return list(pl.pallas_call(
        body, name=name,
        out_shape=[jax.ShapeDtypeStruct(a.shape, a.dtype) for a in lands],
        in_specs=[HBM_SPEC] * n, out_specs=[HBM_SPEC] * n,
        input_output_aliases={i: i for i in range(n)},
        scratch_shapes=[pltpu.SemaphoreType.DMA((3 * n,)), pltpu.SemaphoreType.DMA((3 * n,))],
    )(*lands))


def _pair_sum(pieces, got, core, name):
    _, rows, cols = pieces.shape
    tr, tc = _tile_2d(rows, cols)

    def body(core_ref, mine_ref, got_ref, out_ref):
        del core_ref
        out_ref[...] = (mine_ref[...].astype(F32) + got_ref[...].astype(F32)).astype(out_ref.dtype)

    return pl.pallas_call(
        body, name=name,
        out_shape=jax.ShapeDtypeStruct((N_CHIP, rows, cols), pieces.dtype),
        grid_spec=pltpu.PrefetchScalarGridSpec(
            num_scalar_prefetch=1, grid=(N_CHIP, rows // tr, cols // tc),
            in_specs=[pl.BlockSpec((None, tr, tc), lambda ch, i, j, core: (2 * ch + core[0], i, j)),
                      pl.BlockSpec((None, tr, tc), lambda ch, i, j, core: (ch, i, j))],
            out_specs=pl.BlockSpec((None, tr, tc), lambda ch, i, j, core: (ch, i, j))),
        compiler_params=_params("parallel", "parallel", "parallel"),
    )(core, pieces, got)


def _tile_2d(rows, cols, budget=2 * 1024 * 1024):
    want = max(16, budget // (4 * cols))
    if rows <= want:
        return rows, cols
    t = (want // 16) * 16
    while t >= 16:
        if rows % t == 0:
            return t, cols
        t -= 16
    return rows, _tile(cols, max(LANES, budget // (4 * rows)))


def _adamw_math(w, g, m, v):
    m = ADAM_B1 * m + (1.0 - ADAM_B1) * g
    v = ADAM_B2 * v + (1.0 - ADAM_B2) * (g * g)
    m_hat = m / (1.0 - ADAM_B1 ** ADAM_STEP)
    v_hat = v / (1.0 - ADAM_B2 ** ADAM_STEP)
    delta = -ADAM_LR * (m_hat / (jnp.sqrt(v_hat) + ADAM_EPS) + ADAM_WD * w)
    return delta, m, v


def _adamw_shard(w, m, v, sums, got, chip, name):
    rows, cols = w.shape
    tr, tc = _tile_2d(rows, cols, budget=1024 * 1024)

    def body(chip_ref, w_ref, m_ref, v_ref, own_ref, got_ref, g_out, d_out, m_out, v_out):
        del chip_ref
        g = own_ref[...].astype(F32)
        for k in range(3):
            g = g + got_ref[k].astype(F32)
        delta, m_new, v_new = _adamw_math(w_ref[...], g, m_ref[...], v_ref[...])
        g_out[...] = g
        d_out[...] = delta
        m_out[...] = m_new
        v_out[...] = v_new

    blk = pl.BlockSpec((tr, tc), lambda i, j, chip: (i, j))
    return pl.pallas_call(
        body, name=name,
        out_shape=[jax.ShapeDtypeStruct((rows, cols), F32)] * 4,
        grid_spec=pltpu.PrefetchScalarGridSpec(
            num_scalar_prefetch=1, grid=(rows // tr, cols // tc),
            in_specs=[blk, blk, blk,
                      pl.BlockSpec((None, tr, tc), lambda i, j, chip: (chip[0], i, j)),
                      pl.BlockSpec((3, tr, tc), lambda i, j, chip: (0, i, j))],
            out_specs=[blk] * 4),
        compiler_params=_params("parallel", "parallel"),
    )(chip, w, m, v, sums, got)


def _adamw_replicated(w, m, v, parts, name):
    rows, cols = w.shape

    def body(w_ref, m_ref, v_ref, parts_ref, g_out, d_out, m_out, v_out):
        g = parts_ref[0]
        for d in range(1, N_DEV):
            g = g + parts_ref[d]
        delta, m_new, v_new = _adamw_math(w_ref[...], g, m_ref[...], v_ref[...])
        g_out[...] = g
        d_out[...] = delta
        m_out[...] = m_new
        v_out[...] = v_new

    return pl.pallas_call(
        body, name=name,
        out_shape=[jax.ShapeDtypeStruct((rows, cols), F32)] * 4,
        compiler_params=_params(),
    )(w, m, v, parts)


NN = ((1,), (0,))
NT = ((1,), (1,))
TN = ((0,), (0,))


ANY_SPEC = pl.BlockSpec(memory_space=pl.ANY)


def _mm(name, a, b, *, dims, grid, a_spec, b_spec, outs, extra=(), extra_specs=(), epilogue=None, acc_shape=None,
        deps=(), b_pieces=0, side=None):
    nk = grid[2]
    n_extra, n_out = len(extra), len(outs)
    side_ins = side["ins"] if side else []
    side_outs = side["outs"] if side else []
    first_side = 2 + n_extra + len(deps)
    first_out = first_side + len(side_ins)
    assert not side or nk == 1

    assert not side or (epilogue is None and n_out == 1 and not b_pieces)
    SIDE_SLICES = 4

    def body(*refs):
        dot = lambda lhs, rhs: lax.dot_general(lhs.astype(BF16), rhs.astype(BF16), (dims, ((), ())),
                                               preferred_element_type=F32)
        if side:
            refs = refs[1:]
            a_ref, b_ref, out_ref = refs[0], refs[1], refs[first_out]
            ins, sides = refs[first_side:first_out], refs[first_out + 1:first_out + 1 + len(side_outs)]
            rows = out_ref.shape[0] // SIDE_SLICES
            srows = sides[0].shape[-2] // SIDE_SLICES
            for part in range(SIDE_SLICES):
                rs = slice(part * srows, (part + 1) * srows)
                for val, o in zip(side["fn"](*[r[..., rs, :] for r in ins]), sides):
                    o[rs, :] = val.astype(o.dtype)
                ms = slice(part * rows, (part + 1) * rows)
                lhs = a_ref[:, ms] if dims == TN else a_ref[ms, :]
                out_ref[ms, :] = dot(lhs, b_ref[...]).astype(out_ref.dtype)
            return
        a_ref, b_ref = refs[0], refs[1]
        ex = refs[2:2 + n_extra]
        out_refs = refs[first_out:first_out + n_out]
        if b_pieces:
            ks = b_ref.shape[-1]
            part = dot(a_ref[:, 0:ks], b_ref[0])
            for piece in range(1, b_pieces):
                part = part + dot(a_ref[:, piece * ks:(piece + 1) * ks], b_ref[piece])
        else:
            part = dot(a_ref[...], b_ref[...])

        def finish(acc):
            res = epilogue(acc, *[e[...] for e in ex]) if epilogue else (acc,)
            for val, o in zip(res, out_refs):
                o[...] = val.astype(o.dtype)

        if nk == 1:
            finish(part)
        else:
            acc_ref = refs[-1]
            k = pl.program_id(2)

            @pl.when(k == 0)
            def _():
                acc_ref[...] = part

            @pl.when(k > 0)
            def _():
                acc_ref[...] += part

            @pl.when(k == nk - 1)
            def _():
                finish(acc_ref[...])

    scratch = [pltpu.VMEM(acc_shape, F32)] if nk > 1 else []
    out_shape = [jax.ShapeDtypeStruct(s, d) for s, d, _ in outs]
    in_specs = [a_spec, b_spec, *extra_specs] + [ANY_SPEC] * len(deps)
    out_specs = [sp for _, _, sp in outs]
    if not side:
        return list(pl.pallas_call(
            body, name=name, out_shape=out_shape, grid=grid, in_specs=in_specs, out_specs=out_specs,
            scratch_shapes=scratch, compiler_params=_params("parallel", "parallel", "arbitrary"),
        )(a, b, *extra, *deps))

    def with_scalars(spec):
        if spec.index_map is None:
            return spec
        return pl.BlockSpec(spec.block_shape, lambda i, j, k, s, f=spec.index_map: f(i, j, k))

    def by_step(block, index_map):
        return pl.BlockSpec(block, lambda i, j, k, s: index_map((i * grid[1] + j) * grid[2] + k, s))

    return list(pl.pallas_call(
        body, name=name,
        out_shape=out_shape + [jax.ShapeDtypeStruct(s, d) for s, d, _, _ in side_outs],
        grid_spec=pltpu.PrefetchScalarGridSpec(
            num_scalar_prefetch=1, grid=grid,
            in_specs=[with_scalars(sp) for sp in in_specs] + [by_step(blk, im) for _, blk, im in side_ins],
            out_specs=[with_scalars(sp) for sp in out_specs] + [by_step(blk, im) for _, _, blk, im in side_outs],
            scratch_shapes=scratch),
        compiler_params=_params("arbitrary", "arbitrary", "arbitrary"),
    )(side["scalars"], a, b, *extra, *deps, *[arr for arr, _, _ in side_ins]))


def _side_pair_sum(pieces, got, place):
    _, rows, cols = pieces.shape

    def build(steps):
        per_chip = steps // N_CHIP
        tr = rows // per_chip
        assert per_chip * N_CHIP == steps and tr * per_chip == rows and tr % 16 == 0
        blk = (None, tr, cols)
        here = lambda st, s: (st // per_chip, st % per_chip, 0)
        return dict(scalars=place,
                    ins=[(pieces, blk, lambda st, s: (2 * (st // per_chip) + s[0], st % per_chip, 0)), (got, blk, here)],
                    outs=[((N_CHIP, rows, cols), pieces.dtype, blk, here)],
                    fn=lambda mine, other: (mine.astype(F32) + other.astype(F32),))
    return build


def _side_adamw(w, m, v, sums, got, place):
    rows, cols = w.shape

    def build(steps):
        tr = rows // steps
        assert tr * steps == rows and tr % 16 == 0
        blk, here = (tr, cols), (lambda st, s: (st, 0))

        def fn(w, m, v, own, got):
            g = own.astype(F32)
            for k in range(3):
                g = g + got[k].astype(F32)
            delta, m_new, v_new = _adamw_math(w, g, m, v)
            return g, delta, m_new, v_new

        return dict(scalars=place,
                    ins=[(w, blk, here), (m, blk, here), (v, blk, here),
                         (sums, (None, tr, cols), lambda st, s: (s[1], st, 0)),
                         (got, (3, tr, cols), lambda st, s: (0, st, 0))],
                    outs=[((rows, cols), F32, blk, here)] * 4, fn=fn)
    return build


def _mm_nn(name, a, b, out_dtype=F32, tm=1024, tn=1024, tk=2048, epilogue=None, n_out=1, extra=(), deps=(), side=None):
    m, kd = a.shape
    n = b.shape[1]
    tm, tn, tk = _tile(m, tm), _tile(n, tn), _tile(kd, tk)
    dts = out_dtype if isinstance(out_dtype, (list, tuple)) else [out_dtype] * n_out
    o_spec = pl.BlockSpec((tm, tn), lambda i, j, k: (i, j))
    grid = (m // tm, n // tn, kd // tk)
    return _mm(name, a, b, dims=NN, grid=grid,
               a_spec=pl.BlockSpec((tm, tk), lambda i, j, k: (i, k)),
               b_spec=pl.BlockSpec((tk, tn), lambda i, j, k: (k, j)),
               outs=[((m, n), dt, o_spec) for dt in dts], extra=extra, extra_specs=[o_spec] * len(extra),
               epilogue=epilogue, acc_shape=(tm, tn), deps=deps, side=side and side(grid[0] * grid[1] * grid[2]))


def _mm_nn_pieces(name, a, b, out_dtype=F32, tm=1024, tk=2048, epilogue=None, n_out=1):
    m, kd = a.shape
    npc, _, ns = b.shape
    tm, tk = _tile(m, tm), _tile(kd, tk)
    tn = _tile(ns, 1024)
    r = ns // tn
    dts = out_dtype if isinstance(out_dtype, (list, tuple)) else [out_dtype] * n_out
    o_spec = pl.BlockSpec((tm, tn), lambda i, j, k: (i, j))
    return _mm(name, a, b, dims=NN, grid=(m // tm, npc * r, kd // tk),
               a_spec=pl.BlockSpec((tm, tk), lambda i, j, k: (i, k)),
               b_spec=pl.BlockSpec((None, tk, tn), lambda i, j, k: (j // r, k, j % r)),
               outs=[((m, npc * ns), dt, o_spec) for dt in dts], epilogue=epilogue, acc_shape=(tm, tn))


def _mm_nt(name, a, b, out_dtype=F32, tm=1024, tn=1024, tk=2048, epilogue=None, extra=(), n_rows=None, deps=(),
           side=None):
    m, kd = a.shape
    n = n_rows or b.shape[0]
    tm, tn, tk = _tile(m, tm), _tile(n, tn), _tile(kd, tk)
    o_spec = pl.BlockSpec((tm, tn), lambda i, j, k: (i, j))
    grid = (m // tm, n // tn, kd // tk)
    res = _mm(name, a, b, dims=NT, grid=grid,
              a_spec=pl.BlockSpec((tm, tk), lambda i, j, k: (i, k)),
              b_spec=pl.BlockSpec((tn, tk), lambda i, j, k: (j, k)),
              outs=[((m, n), out_dtype, o_spec)], extra=extra, extra_specs=[o_spec] * len(extra),
              epilogue=epilogue, acc_shape=(tm, tn), deps=deps, side=side and side(grid[0] * grid[1] * grid[2]))
    return res if side else res[0]


def _mm_nt_pieces(name, a, b, out_dtype=F32, tm=1024, tn=256, deps=()):
    m = a.shape[0]
    npc, n, ks = b.shape
    tm, tn = _tile(m, tm), _tile(n, tn)
    return _mm(name, a, b, dims=NT, grid=(m // tm, n // tn, 1),
               a_spec=pl.BlockSpec((tm, npc * ks), lambda i, j, k: (i, 0)),
               b_spec=pl.BlockSpec((npc, tn, ks), lambda i, j, k: (0, j, 0)),
               outs=[((m, n), out_dtype, pl.BlockSpec((tm, tn), lambda i, j, k: (i, j)))],
               deps=deps, b_pieces=npc)[0]


def _mm_tn(name, a, b, out_dtype=BF16, tm=1024, tn=1024, pieces=0, deps=(), side=None):
    t, m = a.shape
    n = b.shape[1]
    tm = _tile(m, tm)
    if pieces:
        ns = n // pieces
        tn = _tile(ns, tn)
        r = ns // tn
        out = ((pieces, m, ns), out_dtype, pl.BlockSpec((None, tm, tn), lambda i, j, k: (j // r, i, j % r)))
    else:
        tn = _tile(n, tn)
        out = ((m, n), out_dtype, pl.BlockSpec((tm, tn), lambda i, j, k: (i, j)))
    grid = (m // tm, n // tn, 1)
    res = _mm(name, a, b, dims=TN, grid=grid,
              a_spec=pl.BlockSpec((t, tm), lambda i, j, k: (0, i)),
              b_spec=pl.BlockSpec((t, tn), lambda i, j, k: (0, j)),
              outs=[out], deps=deps, side=side and side(grid[0] * grid[1]))
    return res if side else res[0]


def _rms_f(x, g):
    r = lax.rsqrt(jnp.mean(x * x, axis=-1, keepdims=True) + EPS)
    xh = x * r
    return xh * g, xh, r


def _rms_b(dy, g, xh, r):
    dxh = dy * g
    dx = r * (dxh - xh * jnp.mean(dxh * xh, axis=-1, keepdims=True))
    dg = jnp.sum(dy * xh, axis=0, keepdims=True)
    return dx, dg


def _rowwise(name, fn, row_ins, vec_ins, row_outs, vec_outs, deps=()):
    t = row_ins[0].shape[0]
    tr = min(ROW_TILE, t)
    nr, nv, no = len(row_ins), len(vec_ins), len(row_outs)
    first_out = nr + nv + len(deps)

    def body(*refs):
        ri, vi = refs[:nr], refs[nr:nr + nv]
        ro, vo = refs[first_out:first_out + no], refs[first_out + no:]
        routs, vouts = fn(*[r[...] for r in ri], *[v[...] for v in vi])
        for o, val in zip(ro, routs):
            o[...] = val.astype(o.dtype)
        i = pl.program_id(0)
        for o, val in zip(vo, vouts):
            @pl.when(i == 0)
            def _():
                o[...] = val

            @pl.when(i > 0)
            def _():
                o[...] += val

    res = pl.pallas_call(
        body, name=name,
        out_shape=[jax.ShapeDtypeStruct((t, w), d) for w, d in row_outs]
        + [jax.ShapeDtypeStruct((1, w), F32) for w in vec_outs],
        grid=(t // tr,),
        in_specs=[pl.BlockSpec((tr, a.shape[1]), lambda i: (i, 0)) for a in row_ins]
        + [pl.BlockSpec((1, a.shape[1]), lambda i: (0, 0)) for a in vec_ins] + [ANY_SPEC] * len(deps),
        out_specs=[pl.BlockSpec((tr, w), lambda i: (i, 0)) for w, _ in row_outs]
        + [pl.BlockSpec((1, w), lambda i: (0, 0)) for w in vec_outs],
        compiler_params=_params("arbitrary"),
    )(*row_ins, *vec_ins, *deps)
    return list(res)


def _norm_fwd(x, g, name, deps=()):
    d = x.shape[1]
    return _rowwise(name, lambda x, g: ((_rms_f(x, g)[0],), ()), [x], [g], [(d, BF16)], [], deps=deps)[0]


def _norm_bwd(x, dxn, dres, g, name):
    d = x.shape[1]

    def fn(x, dxn, dres, g):
        _, xh, r = _rms_f(x, g)
        dx, dg = _rms_b(dxn, g, xh, r)
        return (dres + dx,), (dg,)

    return _rowwise(name, fn, [x, dxn, dres], [g], [(d, F32)], [d])


def _block_fwd(h_prev, y, g_y, g_n, name, deps=()):
    d = h_prev.shape[1]

    def fn(h_prev, y, g_y, g_n):
        h = h_prev + _rms_f(y, g_y)[0]
        return (h, _rms_f(h, g_n)[0]), ()

    return _rowwise(name, fn, [h_prev, y], [g_y, g_n], [(d, F32), (d, BF16)], [], deps=deps)


def _block_bwd(h, y, d_up, d_n, g_y, g_n, name, deps=()):
    d = h.shape[1]

    def fn(h, y, d_up, d_n, g_y, g_n):
        _, hh, hr = _rms_f(h, g_n)
        dh_n, dg_n = _rms_b(d_n, g_n, hh, hr)
        dh = d_up + dh_n
        _, yh, yr = _rms_f(y, g_y)
        dy, dg_y = _rms_b(dh, g_y, yh, yr)
        return (dh, dy), (dg_y, dg_n)

    return _rowwise(name, fn, [h, y, d_up, d_n], [g_y, g_n], [(d, F32), (d, BF16)], [d, d], deps=deps)


def _loss_head(h2, z, e, target, g, name):
    d = h2.shape[1]

    def fn(h2, z, e, target, g):
        gate = jax.nn.sigmoid(z)
        y, xh, r = _rms_f(e * gate, g)
        diff = h2 + y - target
        loss = 0.5 * jnp.sum(jnp.mean(diff * diff, axis=-1, keepdims=True), axis=0, keepdims=True)
        dh3 = diff / d
        deg, dg = _rms_b(dh3, g, xh, r)
        dz = deg * e * gate * (1.0 - gate)
        de = deg * gate
        return (dh3, dz, de), (dg, jnp.broadcast_to(loss, (1, LANES)))

    return _rowwise(name, fn, [h2, z, e, target], [g], [(d, F32), (d, BF16), (d, BF16)], [d, LANES])


def _shift_down(x, s, rows):
    return jnp.where(rows >= s, pltpu.roll(x, s, axis=0), 0.0)


def _shift_up(x, s, rows):
    t = x.shape[0]
    return jnp.where(rows < t - s, pltpu.roll(x, t - s, axis=0), 0.0)


def _conv_pre(x, w, rows):
    pre = x * w[QK_CONV - 1:QK_CONV, :]
    for s in range(1, QK_CONV):
        pre = pre + _shift_down(x, s, rows) * w[QK_CONV - 1 - s:QK_CONV - s, :]
    return pre


def _conv_fwd(proj_a, w, start, name):
    t = proj_a.shape[0]
    width = w.shape[1]
    tc = _tile(width, 256)
    off = start // tc

    def body(x_ref, w_ref, y_ref):
        x = x_ref[...]
        rows = lax.broadcasted_iota(jnp.int32, x.shape, 0)
        pre = _conv_pre(x, w_ref[...], rows)
        y_ref[...] = pre * jax.nn.sigmoid(pre)

    return pl.pallas_call(
        body, name=name,
        out_shape=jax.ShapeDtypeStruct((t, width), F32),
        grid=(width // tc,),
        in_specs=[pl.BlockSpec((t, tc), lambda j: (0, off + j)), pl.BlockSpec((QK_CONV, tc), lambda j: (0, j))],
        out_specs=pl.BlockSpec((t, tc), lambda j: (0, j)),
        compiler_params=_params("parallel"),
    )(proj_a, w)


def _conv_bwd(proj_a, w, dy, start, name):
    t = proj_a.shape[0]
    width = w.shape[1]
    tc = _tile(width, 256)
    off = start // tc

    def body(x_ref, w_ref, dy_ref, dx_ref, dw_ref):
        x, w = x_ref[...], w_ref[...]
        rows = lax.broadcasted_iota(jnp.int32, x.shape, 0)
        pre = _conv_pre(x, w, rows)
        sig = jax.nn.sigmoid(pre)
        dpre = dy_ref[...] * (sig * (1.0 + pre * (1.0 - sig)))
        dx = dpre * w[QK_CONV - 1:QK_CONV, :]
        dws = [jnp.sum(dpre * x, axis=0, keepdims=True)]
        for s in range(1, QK_CONV):
            dx = dx + _shift_up(dpre, s, rows) * w[QK_CONV - 1 - s:QK_CONV - s, :]
            dws.append(jnp.sum(dpre * _shift_down(x, s, rows), axis=0, keepdims=True))
        dx_ref[...] = dx.astype(dx_ref.dtype)
        for s in range(QK_CONV):
            dw_ref[QK_CONV - 1 - s:QK_CONV - s, :] = dws[s]

    return pl.pallas_call(
        body, name=name,
        out_shape=[jax.ShapeDtypeStruct((t, width), BF16), jax.ShapeDtypeStruct((QK_CONV, width), F32)],
        grid=(width // tc,),
        in_specs=[pl.BlockSpec((t, tc), lambda j: (0, off + j)), pl.BlockSpec((QK_CONV, tc), lambda j: (0, j)),
                  pl.BlockSpec((t, tc), lambda j: (0, j))],
        out_specs=[pl.BlockSpec((t, tc), lambda j: (0, j)), pl.BlockSpec((QK_CONV, tc), lambda j: (0, j))],
        compiler_params=_params("parallel"),
    )(proj_a, w, dy)


def _dot(a, b, dims):
    return lax.dot_general(a.astype(BF16), b.astype(BF16), (dims, ((), ())), preferred_element_type=F32)


def _dot_split(ones, x):
    hi = x.astype(BF16)
    rest = x - hi.astype(F32)
    mid = rest.astype(BF16)
    lo = (rest - mid.astype(F32)).astype(BF16)
    dot = lambda part: lax.dot_general(ones, part, (NN, ((), ())), preferred_element_type=F32)
    return dot(hi) + dot(mid) + dot(lo)


def _chunk_iotas():
    rows = lax.broadcasted_iota(jnp.int32, (CHUNK, CHUNK), 0)
    cols = lax.broadcasted_iota(jnp.int32, (CHUNK, CHUNK), 1)
    return rows, cols


def _gates_fwd(proj_b, bias, col_block, heads, name):
    t = proj_b.shape[0]

    def body(x_ref, b_ref, gb_ref, gbt_ref, cols_ref):
        rows, cols = _chunk_iotas()
        tri = (cols <= rows).astype(BF16)
        for ci in range(t // CHUNK):
            sl = slice(ci * CHUNK, (ci + 1) * CHUNK)
            pre = x_ref[sl, :] + b_ref[...]
            log_f = jnp.minimum(pre, 0.0) - jnp.log(1.0 + jnp.exp(-jnp.abs(pre)))
            log_f = jnp.where((cols >= heads) & (cols < 2 * heads), log_f, 0.0)
            tile = jnp.where(cols < heads, pre, _dot_split(tri, log_f))
            gb_ref[sl, :] = tile
            gbt_ref[:, sl] = tile.T
            for h in range(2 * heads):
                cols_ref[h, sl, :] = jnp.broadcast_to(tile[:, h:h + 1], (CHUNK, LANES))

    return pl.pallas_call(
        body, name=name,
        out_shape=[jax.ShapeDtypeStruct((t, LANES), F32), jax.ShapeDtypeStruct((LANES, t), F32),
                   jax.ShapeDtypeStruct((2 * heads, t, LANES), F32)],
        grid=(1,),
        in_specs=[pl.BlockSpec((t, LANES), lambda i: (0, col_block)), pl.BlockSpec((1, LANES), lambda i: (0, 0))],
        out_specs=[pl.BlockSpec((t, LANES), lambda i: (0, 0)), pl.BlockSpec((LANES, t), lambda i: (0, 0)),
                   pl.BlockSpec((2 * heads, t, LANES), lambda i: (0, 0, 0))],
        compiler_params=_params("arbitrary"),
    )(proj_b, bias)


def _gates_bwd(proj_b, bias, dcol, drow, col_block, heads, name):
    t = proj_b.shape[0]

    def body(x_ref, b_ref, dc_ref, dr_ref, dx_ref, db_ref):
        rows, cols = _chunk_iotas()
        later = (cols >= rows).astype(BF16)
        total = jnp.zeros((1, LANES), F32)
        for ci in range(t // CHUNK):
            sl = slice(ci * CHUNK, (ci + 1) * CHUNK)
            by_row = dr_ref[:, sl].T
            d = dc_ref[sl, :] + jnp.where(cols < heads, by_row,
                                          jnp.where(cols < 2 * heads, -pltpu.roll(by_row, heads, axis=1), 0.0))
            d_log_f = _dot_split(later, jnp.where(cols >= heads, d, 0.0))
            pre = x_ref[sl, :] + b_ref[...]
            dx = jnp.where(cols < heads, d, jnp.where(cols < 2 * heads, d_log_f * jax.nn.sigmoid(-pre), 0.0))
            dx_ref[sl, :] = dx.astype(dx_ref.dtype)
            total = total + jnp.sum(dx, axis=0, keepdims=True)
        db_ref[...] = total

    return pl.pallas_call(
        body, name=name,
        out_shape=[jax.ShapeDtypeStruct((t, LANES), BF16), jax.ShapeDtypeStruct((1, LANES), F32)],
        grid=(1,),
        in_specs=[pl.BlockSpec((t, LANES), lambda i: (0, col_block)), pl.BlockSpec((1, LANES), lambda i: (0, 0)),
                  pl.BlockSpec((t, LANES), lambda i: (0, 0)), pl.BlockSpec((LANES, t), lambda i: (0, 0))],
        out_specs=[pl.BlockSpec((t, LANES), lambda i: (0, 0)), pl.BlockSpec((1, LANES), lambda i: (0, 0))],
        compiler_params=_params("arbitrary"),
    )(proj_b, bias, dcol, drow)


def _gate_views(li_ref, b_ref, gbt_ref, s, head, heads, sl):
    li_c, b_c = li_ref[s, sl, :], b_ref[s, sl, :]
    li_r = jnp.broadcast_to(gbt_ref[head, :, sl], (CHUNK, CHUNK))
    b_r = jnp.broadcast_to(gbt_ref[heads + head, :, sl], (CHUNK, CHUNK))
    return li_c, b_c, li_r, b_r


def _chunk_state(kt, k, v, li_c, b_c, ct, nrow, m_prev):
    b_tot = b_c[CHUNK - 1:CHUNK, 0:1]
    a = b_tot - b_c[:, 0:1] + li_c[:, 0:1]
    m_new = jnp.maximum(b_tot + m_prev, jnp.max(a, axis=0, keepdims=True))
    dec = jnp.exp(b_tot + m_prev - m_new)
    w = jnp.exp(a - m_new)
    wv = w * v
    ct_new = dec * ct + _dot(kt, wv, NN)
    n_new = dec * nrow + jnp.sum(w * k, axis=0, keepdims=True)
    return ct_new, n_new, m_new, dec, w, wv


def _chunk_scores(q, k, b_c, li_r, b_r, m_prev):
    rows, cols = _chunk_iotas()
    dlog = jnp.where(cols <= rows, b_c - b_r + li_r, -jnp.inf)
    inter = b_c[:, 0:1] + m_prev
    mt = jnp.maximum(inter, jnp.max(dlog, axis=-1, keepdims=True))
    dw = jnp.exp(dlog - mt)
    iw = jnp.exp(inter - mt)
    a = _dot(q, k, NT)
    return dw, iw, mt, a, a * dw


def _to_row(column):
    rows, cols = _chunk_iotas()
    return jnp.sum(jnp.where(rows == cols, jnp.broadcast_to(column, (CHUNK, CHUNK)), 0.0), axis=0, keepdims=True)


def _head_out(h, o, g):
    cell, hh, r = _rms_f(h, g)
    sig = jax.nn.sigmoid(o)
    return sig * cell, (cell, hh, r, sig)


HEADS_PER_STEP = 2


def _mlstm_fwd(qc, kc, kct, proj_a, gcols, gbt, g_mlstm, heads, d_model, name, dep):
    t = qc.shape[0]
    dh = g_mlstm.shape[1] // heads
    nc = t // CHUNK
    scale = dh ** -0.5
    hp = HEADS_PER_STEP

    def body(q_ref, k_ref, kt_ref, v_ref, o_ref, li_ref, b_ref, gbt_ref, g_ref, dep_ref, out_ref, ct_out, nm_out,
             ct_ref, n_ref, m_ref):
        del dep_ref
        step = pl.program_id(0)
        ct_ref[...] = jnp.zeros_like(ct_ref)
        n_ref[...] = jnp.zeros_like(n_ref)
        m_ref[...] = jnp.zeros_like(m_ref)
        tile_row = lax.broadcasted_iota(jnp.int32, (8, LANES), 0)

        def chunk(ci, carry):
            sl = pl.ds(pl.multiple_of(ci * CHUNK, CHUNK), CHUNK)
            for s in range(hp):
                cs = slice(s * dh, (s + 1) * dh)
                q, k, v, kt = q_ref[sl, cs], k_ref[sl, cs] * scale, v_ref[sl, cs], kt_ref[cs, sl]
                li_c, b_c, li_r, b_r = _gate_views(li_ref, b_ref, gbt_ref, s, step * hp + s, heads, sl)
                ct, nrow, m_prev = ct_ref[s], n_ref[s, 0:1, :], m_ref[s, 0:1, 0:1]
                ct_out[s, ci] = ct
                nm_out[s, ci] = jnp.where(tile_row == 0, nrow, jnp.where(tile_row == 1, m_prev, 0.0))
                dw, iw, mt, _, sm = _chunk_scores(q, k, b_c, li_r, b_r, m_prev)
                iq = iw * q
                num = _dot(jnp.concatenate([iq, sm], axis=1), jnp.concatenate([ct, v], axis=0), NN)
                den = jnp.sum(iq * nrow + sm, axis=-1, keepdims=True)
                h = num / jnp.maximum(jnp.abs(den), jnp.exp(-mt))
                y, _ = _head_out(h, o_ref[sl, cs], g_ref[:, cs])
                out_ref[sl, cs] = y.astype(out_ref.dtype)
                ct_new, n_new, m_new, _, _, _ = _chunk_state(kt, k, v, li_c, b_c, ct, nrow, m_prev)
                ct_ref[s] = ct_new
                n_ref[s, 0:1, :] = n_new
                m_ref[s, 0:1, 0:1] = m_new
            return carry

        lax.fori_loop(0, nc, chunk, 0, unroll=2)

    col = lambda off: pl.BlockSpec((t, hp * dh), lambda h: (0, off // hp + h))
    slab = lambda off: pl.BlockSpec((hp, t, LANES), lambda h: (off // hp + h, 0, 0))
    return pl.pallas_call(
        body, name=name,
        out_shape=[jax.ShapeDtypeStruct((t, d_model), BF16), jax.ShapeDtypeStruct((heads, nc, dh, dh), F32),
                   jax.ShapeDtypeStruct((heads, nc, 8, LANES), F32)],
        grid=(heads // hp,),
        in_specs=[col(0), col(0), pl.BlockSpec((hp * dh, t), lambda h: (h, 0)), col(2 * heads), col(3 * heads),
                  slab(0), slab(heads), pl.BlockSpec((2 * heads, 1, t), lambda h: (0, 0, 0)),
                  pl.BlockSpec((1, hp * dh), lambda h: (0, h)), ANY_SPEC],
        out_specs=[pl.BlockSpec((t, hp * dh), lambda h: (0, h)),
                   pl.BlockSpec((hp, nc, dh, dh), lambda h: (h, 0, 0, 0)),
                   pl.BlockSpec((hp, nc, 8, LANES), lambda h: (h, 0, 0, 0))],
        scratch_shapes=[pltpu.VMEM((hp, dh, dh), F32), pltpu.VMEM((hp, 8, dh), F32), pltpu.VMEM((hp, 8, LANES), F32)],
        compiler_params=_params("arbitrary"),
    )(qc, kc, kct, proj_a, proj_a, gcols, gcols, gbt, g_mlstm, dep)


def _mlstm_bwd(qc, kc, qct, kct, proj_a, gcols, gbt, g_mlstm, states, dhcat, heads, name, dep):
    t = qc.shape[0]
    wm = g_mlstm.shape[1]
    dh = wm // heads
    nc = t // CHUNK
    scale = dh ** -0.5

    def body(q_ref, k_ref, qt_ref, kt_ref, v_ref, o_ref, li_ref, b_ref, gbt_ref, g_ref, ct_all, nm_all, dy_ref, dep_ref,
             dq_ref, dk_ref, dv_ref, do_ref, dcol_ref, drow_ref, dg_ref, dct_ref, dn_ref):
        del dep_ref
        head = pl.program_id(0)
        g = g_ref[...]

        @pl.when(head == 0)
        def _():
            dcol_ref[...] = jnp.zeros_like(dcol_ref)
            drow_ref[...] = jnp.zeros_like(drow_ref)

        def load(ci):
            sl = pl.ds(pl.multiple_of(ci * CHUNK, CHUNK), CHUNK)
            return (sl, q_ref[sl, :], k_ref[sl, :] * scale, v_ref[sl, :], kt_ref[:, sl]) \
                + _gate_views(li_ref, b_ref, gbt_ref, 0, head, heads, sl)

        dct_ref[...] = jnp.zeros_like(dct_ref)
        dn_ref[...] = jnp.zeros_like(dn_ref)
        dg_ref[...] = jnp.zeros_like(dg_ref)
        rows, cols = _chunk_iotas()
        last = lax.broadcasted_iota(jnp.int32, (CHUNK, 1), 0) == CHUNK - 1
        lane = lax.broadcasted_iota(jnp.int32, (CHUNK, LANES), 1)

        def bwd(step, carry):
            ci = nc - 1 - step
            sl, q, k, v, kt, li_c, b_c, li_r, b_r = load(ci)
            qt = qt_ref[:, sl]
            ct, nrow, m_prev = ct_all[ci], nm_all[ci, 0:1, :], nm_all[ci, 1:2, 0:1]
            dw, iw, mt, a, sm = _chunk_scores(q, k, b_c, li_r, b_r, m_prev)
            qc_ = _dot(q, ct, NN)
            qn = jnp.sum(q * nrow, axis=-1, keepdims=True)
            num = iw * qc_ + _dot(sm, v, NN)
            den = iw * qn + jnp.sum(sm, axis=-1, keepdims=True)
            floor = jnp.exp(-mt)
            dn = jnp.maximum(jnp.abs(den), floor)
            h = num / dn
            _, (cell, hh, r, sig) = _head_out(h, o_ref[sl, :], g)
            dy = dy_ref[sl, :]
            dcell = dy * sig
            do_ref[sl, :] = (dy * cell * sig * (1.0 - sig)).astype(do_ref.dtype)
            dh_, dg = _rms_b(dcell, g, hh, r)
            dg_ref[...] += dg
            dnum = dh_ / dn
            ddn = -jnp.sum(dh_ * h, axis=-1, keepdims=True) / dn
            dden = jnp.where(jnp.abs(den) >= floor, ddn * jnp.sign(den), 0.0)
            idn = iw * dnum
            idd = iw * dden
            dw_t = jnp.exp(jnp.where(cols >= rows, b_r - b_c + li_c, -jnp.inf) - _to_row(mt))
            sm_t = _dot(k, q, NT) * dw_t
            dsm = _dot(dnum, v, NT) + dden
            da_ = dsm * dw
            da_t = (_dot(v, dnum, NT) + _to_row(dden)) * dw_t
            e = da_ * a
            dq = _dot(idn, ct, NT) + _dot(da_, k, NN) + idd * nrow
            dk = _dot(da_t, q, NN)
            dv = _dot(sm_t, dnum, NN)
            dct = _dot(qt, idn, NN)
            dnr = jnp.sum(q * idd, axis=0, keepdims=True)
            db = jnp.sum(e + idn * qc_, axis=-1, keepdims=True) + idd * qn
            _, _, _, dec, w, wv = _chunk_state(kt, k, v, li_c, b_c, ct, nrow, m_prev)
            dct_new, dn_new = dct_ref[...], dn_ref[0:1, :]
            ddec = jnp.sum(jnp.sum(dct_new * ct, axis=-1, keepdims=True), axis=0, keepdims=True) \
                + jnp.sum(dn_new * nrow, axis=-1, keepdims=True)
            dk = dk + _dot(wv, dct_new, NT) + w * dn_new
            dwv = _dot(k, dct_new, NN)
            dv = dv + w * dwv
            dwt = jnp.sum(dwv * v + k * dn_new, axis=-1, keepdims=True)
            da = dwt * w
            dbtot = jnp.sum(da, axis=0, keepdims=True) + ddec * dec
            db = db - da + jnp.where(last, dbtot, 0.0)
            dct_ref[...] = dec * dct_new + dct
            dn_ref[0:1, :] = dec * dn_new + dnr
            dq_ref[sl, :] = dq
            dk_ref[sl, :] = dk * scale
            dv_ref[sl, :] = dv.astype(dv_ref.dtype)
            dcol_ref[sl, :] += jnp.where(lane == head, da, 0.0) + jnp.where(lane == heads + head, db, 0.0)
            drow_ref[head, :, sl] = jnp.sum(e, axis=0, keepdims=True)
            return carry

        lax.fori_loop(0, nc, bwd, 0, unroll=8)

    col = lambda off: pl.BlockSpec((t, dh), lambda h: (0, off + h))
    rowb = pl.BlockSpec((dh, t), lambda h: (h, 0))
    slab = lambda off: pl.BlockSpec((1, t, LANES), lambda h: (off + h, 0, 0))
    vec = pl.BlockSpec((1, dh), lambda h: (0, h))
    return pl.pallas_call(
        body, name=name,
        out_shape=[jax.ShapeDtypeStruct((t, wm), F32)] * 2 + [jax.ShapeDtypeStruct((t, wm), BF16)] * 2
        + [jax.ShapeDtypeStruct((t, LANES), F32), jax.ShapeDtypeStruct((heads, 1, t), F32),
           jax.ShapeDtypeStruct((1, wm), F32)],
        grid=(heads,),
        in_specs=[col(0), col(0), rowb, rowb, col(2 * heads), col(3 * heads), slab(0), slab(heads),
                  pl.BlockSpec((2 * heads, 1, t), lambda h: (0, 0, 0)), vec,
                  pl.BlockSpec((None, nc, dh, dh), lambda h: (h, 0, 0, 0)),
                  pl.BlockSpec((None, nc, 8, LANES), lambda h: (h, 0, 0, 0)), col(0), ANY_SPEC],
        out_specs=[col(0), col(0), col(0), col(0), pl.BlockSpec((t, LANES), lambda h: (0, 0)),
                   pl.BlockSpec((heads, 1, t), lambda h: (0, 0, 0)), vec],
        scratch_shapes=[pltpu.VMEM((dh, dh), F32), pltpu.VMEM((8, dh), F32)],
        compiler_params=_params("arbitrary"),
    )(qc, kc, qct, kct, proj_a, proj_a, gcols, gcols, gbt, g_mlstm, *states, dhcat, dep)


def _pool_window(gi):
    return jnp.where(gi == 0, 2.0, jnp.where(gi == 1, 4.0, jnp.where(gi == 2, 8.0, 16.0))).astype(F32)


def _pool_select(gi, levels):
    return jnp.where(gi == 0, levels[0], jnp.where(gi == 1, levels[1], jnp.where(gi == 2, levels[2], levels[3])))


def _pooled(u, gi, rows):
    s, levels = u, []
    for lvl in range(POOL_GROUPS):
        s = s + _shift_down(s, 1 << lvl, rows)
        levels.append(s)
    cnt = jnp.minimum((rows + 1).astype(F32), _pool_window(gi))
    return _pool_select(gi, levels) / cnt - u, cnt


def _pool_fwd(proj_b, w_pool, scale, hcat, name):
    t, d_model = hcat.shape
    g, cg, _ = w_pool.shape
    off = (d_model - g * cg) // cg

    def body(u_ref, w_ref, s_ref, hcat_ref, out_ref):
        del hcat_ref
        gi = pl.program_id(0)
        u = u_ref[...]
        rows = lax.broadcasted_iota(jnp.int32, u.shape, 0)
        pooled, _ = _pooled(u, gi, rows)
        out_ref[...] = (_dot(pooled, w_ref[...], NN) * s_ref[...]).astype(out_ref.dtype)

    return pl.pallas_call(
        body, name=name,
        out_shape=jax.ShapeDtypeStruct((t, d_model), BF16),
        grid=(g,),
        in_specs=[pl.BlockSpec((t, cg), lambda gi: (0, gi)), pl.BlockSpec((None, cg, cg), lambda gi: (gi, 0, 0)),
                  pl.BlockSpec((1, cg), lambda gi: (0, gi)), HBM_SPEC],
        out_specs=pl.BlockSpec((t, cg), lambda gi: (0, off + gi)),
        input_output_aliases={3: 0},
        compiler_params=_params("arbitrary"),
    )(proj_b, w_pool, scale, hcat)


def _pool_bwd(proj_b, w_pool, scale, dhcat, name):
    t, d_model = dhcat.shape
    g, cg, _ = w_pool.shape
    off = (d_model - g * cg) // cg

    def body(u_ref, w_ref, s_ref, dy_ref, du_ref, dw_ref, ds_ref):
        gi = pl.program_id(0)
        u, w, dy = u_ref[...], w_ref[...], dy_ref[...]
        rows = lax.broadcasted_iota(jnp.int32, u.shape, 0)
        pooled, cnt = _pooled(u, gi, rows)
        mixed = _dot(pooled, w, NN)
        ds_ref[...] = jnp.sum(dy * mixed, axis=0, keepdims=True)
        dmixed = dy * s_ref[...]
        dw_ref[...] = _dot(pooled, dmixed, TN).astype(dw_ref.dtype)
        dpooled = _dot(dmixed, w, NT)
        s, levels = dpooled / cnt, []
        for lvl in range(POOL_GROUPS):
            s = s + _shift_up(s, 1 << lvl, rows)
            levels.append(s)
        du_ref[...] = (_pool_select(gi, levels) - dpooled).astype(du_ref.dtype)

    return pl.pallas_call(
        body, name=name,
        out_shape=[jax.ShapeDtypeStruct((t, g * cg), BF16), jax.ShapeDtypeStruct((g, cg, cg), BF16),
                   jax.ShapeDtypeStruct((1, g * cg), F32)],
        grid=(g,),
        in_specs=[pl.BlockSpec((t, cg), lambda gi: (0, gi)), pl.BlockSpec((None, cg, cg), lambda gi: (gi, 0, 0)),
                  pl.BlockSpec((1, cg), lambda gi: (0, gi)), pl.BlockSpec((t, cg), lambda gi: (0, off + gi))],
        out_specs=[pl.BlockSpec((t, cg), lambda gi: (0, gi)), pl.BlockSpec((None, cg, cg), lambda gi: (gi, 0, 0)),
                   pl.BlockSpec((1, cg), lambda gi: (0, gi))],
        compiler_params=_params("arbitrary"),
    )(proj_b, w_pool, scale, dhcat)


def _pad_cols(a, width):
    return jnp.pad(a, ((0, 0), (0, width - a.shape[1])))


def kernel(x, p, w_in, b_gates, w_qk_conv, g_mlstm, w_pool, pool_scale, w_out, g_mix_pre, g_mix_post, w_ff1, w_ff2, g_ff_pre, g_ff_post, w_ple_proj, w_ple_gate, g_ple_gate, g_ple_post, loss_target, m_w_in, m_b_gates, m_w_qk_conv, m_g_mlstm, m_w_pool, m_pool_scale, m_w_out, m_g_mix_pre, m_g_mix_post, m_w_ff1, m_w_ff2, m_g_ff_pre, m_g_ff_post, m_w_ple_proj, m_w_ple_gate, m_g_ple_gate, m_g_ple_post, v_w_in, v_b_gates, v_w_qk_conv, v_g_mlstm, v_w_pool, v_pool_scale, v_w_out, v_g_mix_pre, v_g_mix_post, v_w_ff1, v_w_ff2, v_g_ff_pre, v_g_ff_post, v_w_ple_proj, v_w_ple_gate, v_g_ple_gate, v_g_ple_post):
    weights = dict(w_in=w_in, b_gates=b_gates, w_qk_conv=w_qk_conv, g_mlstm=g_mlstm, w_pool=w_pool,
                   pool_scale=pool_scale, w_out=w_out, g_mix_pre=g_mix_pre, g_mix_post=g_mix_post, w_ff1=w_ff1,
                   w_ff2=w_ff2, g_ff_pre=g_ff_pre, g_ff_post=g_ff_post, w_ple_proj=w_ple_proj,
                   w_ple_gate=w_ple_gate, g_ple_gate=g_ple_gate, g_ple_post=g_ple_post)
    mom1 = dict(w_in=m_w_in, b_gates=m_b_gates, w_qk_conv=m_w_qk_conv, g_mlstm=m_g_mlstm, w_pool=m_w_pool,
                pool_scale=m_pool_scale, w_out=m_w_out, g_mix_pre=m_g_mix_pre, g_mix_post=m_g_mix_post,
                w_ff1=m_w_ff1, w_ff2=m_w_ff2, g_ff_pre=m_g_ff_pre, g_ff_post=m_g_ff_post,
                w_ple_proj=m_w_ple_proj, w_ple_gate=m_w_ple_gate, g_ple_gate=m_g_ple_gate, g_ple_post=m_g_ple_post)
    mom2 = dict(w_in=v_w_in, b_gates=v_b_gates, w_qk_conv=v_w_qk_conv, g_mlstm=v_g_mlstm, w_pool=v_w_pool,
                pool_scale=v_pool_scale, w_out=v_w_out, g_mix_pre=v_g_mix_pre, g_mix_post=v_g_mix_post,
                w_ff1=v_w_ff1, w_ff2=v_w_ff2, g_ff_pre=v_g_ff_pre, g_ff_post=v_g_ff_post,
                w_ple_proj=v_w_ple_proj, w_ple_gate=v_w_ple_gate, g_ple_gate=v_g_ple_gate, g_ple_post=v_g_ple_post)
    order = list(weights)

    t, d = x.shape[1], x.shape[2]
    heads = MLSTM_HEADS
    wm = g_mlstm.shape[1]
    pw = pool_scale.shape[1]
    cg = pw // POOL_GROUPS
    in_cols = 4 * wm + 2 * heads + pw
    in_shard = w_in.shape[2]
    xs, ps, tgt = x[0], p[0, 0], loss_target[0]

    mx, my, mc = lax.axis_index("x"), lax.axis_index("y"), lax.axis_index("c")
    core = jnp.reshape(mc, (1,)).astype(jnp.int32)
    chip = jnp.reshape(2 * mx + my, (1,)).astype(jnp.int32)

    first = [w_in[0].T.astype(BF16), w_qk_conv[0], w_pool[0].reshape(POOL_GROUPS * w_pool.shape[2], cg).astype(BF16)]
    started, token = _gather_start([first], "gather_start_in")
    anchor = token[0, 0]
    later = lambda w: (w[0] + anchor).astype(BF16)
    rest, token = _gather_start([[later(w_out)], [later(w_ff1)], [later(w_ff2)], [later(w_ple_gate), later(w_ple_proj)]],
                                "gather_start_rest")
    started = started + rest
    bias = _pad_cols(b_gates, LANES)
    small = ["b_gates", "g_mlstm", "pool_scale", "g_mix_pre", "g_mix_post", "g_ff_pre", "g_ff_post", "g_ple_gate",
             "g_ple_post"]
    pack = lambda src: _pad_rows16(jnp.concatenate([_pad_cols(src[k] + anchor, d) for k in small], axis=0))
    small_state = [pack(weights), pack(mom1), pack(mom2)]
    in_state = [a[0].T + anchor for a in (w_in, m_w_in, v_w_in)]

    xn = _norm_fwd(xs, g_mix_pre, "norm_mix_pre", deps=[token])
    g_in, g_conv, g_pool = _forward_pair(
        _gather_wait(started[0], [xn] + small_state + in_state, "gather_wait_in"), "gather_pair_in")
    in_t_a = g_in.reshape(in_cols, d)
    in_t_b = jnp.concatenate([in_t_a[4 * wm + 2 * heads:], in_t_a[4 * wm:4 * wm + 2 * heads],
                              jnp.zeros((LANES - 2 * heads, d), BF16)], axis=0)
    conv_w = g_conv.transpose(1, 0, 2).reshape(QK_CONV, 2 * wm)
    pool_w = g_pool.reshape(N_DEV, POOL_GROUPS, cg // N_DEV, cg).transpose(1, 0, 2, 3).reshape(POOL_GROUPS, cg, cg)
    proj_a = _mm_nt("proj_qkvo", xn, in_t_a, n_rows=4 * wm)
    proj_b = _mm_nt("proj_pool_gates", xn, in_t_b)
    qc = _conv_fwd(proj_a, conv_w[:, :wm], 0, "q_conv")
    kc = _conv_fwd(proj_a, conv_w[:, wm:], wm, "k_conv")
    gb, gbt, gcols = _gates_fwd(proj_b, bias, pw // LANES, heads, "gates")
    gbt = gbt[:2 * heads].reshape(2 * heads, 1, t)
    qct, kct = qc.T.astype(BF16), (kc * (wm // heads) ** -0.5).T.astype(BF16)
    fwd_out = _forward_start(_gather_wait(started[1], gb, "gather_wait_out"), "gather_pair_start_out")
    hcat, *mlstm_states = _mlstm_fwd(qc, kc, kct, proj_a, gcols, gbt, g_mlstm, heads, d, "mlstm", fwd_out["token"])
    hcat = _pool_fwd(proj_b, pool_w, pool_scale, hcat, "pool")
    (g_out,) = _forward_wait(fwd_out, hcat, "gather_pair_wait_out")
    out_w = g_out.reshape(d, d)
    mix = _mm_nn("mix_out", hcat, out_w)[0]
    fwd_ff1 = _forward_start(_gather_wait(started[2], mix, "gather_wait_ff1"), "gather_pair_start_ff1")
    h1, hn = _block_fwd(xs, mix, g_mix_post, g_ff_pre, "residual_mix", deps=[fwd_ff1["token"]])
    (g_ff1,) = _forward_wait(fwd_ff1, hn, "gather_pair_wait_ff1")
    relu_a, act = _mm_nn_pieces("ff1", hn, g_ff1, out_dtype=[BF16, BF16],
                                epilogue=lambda acc: (jnp.maximum(acc, 0.0), jnp.square(jnp.maximum(acc, 0.0))))
    (g_ff2,) = _forward_pair(_gather_wait(started[3], act, "gather_wait_ff2"), "gather_pair_ff2")
    ff2_w = g_ff2.reshape(-1, d)
    ff = _mm_nn("ff2", act, ff2_w, tn=256, tk=4 * d)[0]
    fwd_ple = _forward_start(_gather_wait(started[4], ff, "gather_wait_ple"), "gather_pair_start_ple")
    h2, hg = _block_fwd(h1, ff, g_ff_post, g_ple_gate, "residual_ff", deps=[fwd_ple["token"]])
    g_pgate, g_pproj = _forward_wait(fwd_ple, hg, "gather_pair_wait_ple")
    pgate_w = g_pgate.reshape(d, d)
    z = _mm_nn("ple_gate", hg, pgate_w)[0]
    e = _mm_nn_pieces("ple_proj", ps, g_pproj)[0]
    dh3, dz, de, dg_ple_post, loss_part = _loss_head(h2, z, e, tgt, g_ple_post, "loss_head")
    loss = lax.psum(loss_part[0, 0], MESH_AXES)

    out = {}

    def state(k):
        if k == "w_in":
            return in_state
        return [a.reshape(-1, a.shape[-1]) for a in (weights[k], mom1[k], mom2[k])]

    def pair_begin(pieces, tag):
        keys = list(pieces)
        return keys, _pair_start([pieces[k] for k in keys], "reduce_pair_start_" + tag)

    def chip_begin(group, after, tag):
        keys, begun = group
        mine, got = _pair_wait(begun, after, "reduce_pair_wait_" + tag)
        sums = [_pair_sum(p_, g, core, "pair_sum_" + k) for k, p_, g in zip(keys, mine, got)]
        return keys, _chip_start(sums, "reduce_start_" + tag)

    def reduce_finish(group, after, tag):
        keys, begun = group
        sums, got = _chip_wait(begun, after, "reduce_wait_" + tag)
        for k, s, g in zip(keys, sums, got):
            res = _adamw_shard(*state(k), s, g, chip, "adamw_" + k)
            out[k] = [r.T[None] if k == "w_in" else r.reshape(weights[k].shape) for r in res]

    token_of = lambda group: group[1]["token"]
    dhg = _mm_nt("d_ple_gate_in", dz, pgate_w)
    gw_pgate = _mm_tn("gw_ple_gate", hg, dz)
    gw_pproj = _mm_tn("gw_ple_proj", ps, de, pieces=N_DEV)
    dh2, dff, dg_ff_post, dg_ple_gate = _block_bwd(h2, ff, dh3, dhg, g_ff_post, g_ple_gate, "d_residual_ff")
    da = _mm_nt("d_ff2_in", dff, ff2_w, out_dtype=BF16, extra=[relu_a],
                epilogue=lambda acc, r: (acc * (2.0 * r.astype(F32)),))
    gw_ff2 = _mm_tn("gw_ff2", act, dff)
    ff2_keys = ["w_ff2", "w_ple_gate", "w_ple_proj"]
    pair_ff2 = pair_begin({"w_ff2": gw_ff2.reshape(N_DEV, -1, d),
                           "w_ple_gate": gw_pgate.reshape(N_DEV, d // N_DEV, d), "w_ple_proj": gw_pproj}, "ff2")
    dhn = _mm_nt_pieces("d_ff1_in", da, g_ff1, deps=[token_of(pair_ff2)])
    place = jnp.concatenate([core, chip])
    mine_ff2, got_ff2 = _pair_wait(pair_ff2[1], dhn, "reduce_pair_wait_ff2")
    gw_ff1, sums_ff2 = _mm_tn("gw_ff1", hn, da, pieces=N_DEV, side=_side_pair_sum(mine_ff2[0], got_ff2[0], place))
    sums_ple = [_pair_sum(p_, g, core, "pair_sum_" + k) for k, p_, g in zip(ff2_keys[1:], mine_ff2[1:], got_ff2[1:])]
    red_ff2 = (ff2_keys, _chip_start([sums_ff2] + sums_ple, "reduce_start_ff2"))
    pair_ff1 = pair_begin({"w_ff1": gw_ff1}, "ff1")
    dh1, dmix, dg_mix_post, dg_ff_pre = _block_bwd(h1, mix, dh2, dhn, g_mix_post, g_ff_pre, "d_residual_mix",
                                                   deps=[token_of(pair_ff1), token_of(red_ff2)])
    mine_ff1, got_ff1 = _pair_wait(pair_ff1[1], dmix, "reduce_pair_wait_ff1")
    dhcat, sums_ff1 = _mm_nt("d_mix_in", dmix, out_w, tm=t // 2, tn=d // 2,
                             side=_side_pair_sum(mine_ff1[0], got_ff1[0], place))
    red_ff1 = (["w_ff1"], _chip_start([sums_ff1], "reduce_start_ff1"))
    gw_out = _mm_tn("gw_out", hcat, dmix, deps=[token_of(red_ff1)])
    du, gw_pool, dg_pool_scale = _pool_bwd(proj_b, pool_w, pool_scale, dhcat, "d_pool")
    pair_out = pair_begin({
        "w_out": gw_out.reshape(N_DEV, d // N_DEV, d),
        "w_pool": gw_pool.reshape(POOL_GROUPS, N_DEV, cg // N_DEV, cg).transpose(1, 0, 2, 3).reshape(N_DEV, -1, cg)},
        "out")
    dq, dk, dv, do, dcol, drow, dg_mlstm = _mlstm_bwd(qc, kc, qct, kct, proj_a, gcols, gbt, g_mlstm, mlstm_states, dhcat,
                                                      heads, "d_mlstm", token_of(pair_out))
    red_out = chip_begin(pair_out, dq, "out")
    drow = jnp.pad(drow.reshape(heads, t), ((0, LANES - heads), (0, 0)))
    dgates, dbias = _gates_bwd(proj_b, bias, dcol, drow, pw // LANES, heads, "d_gates")
    dq_pre, gw_conv_q = _conv_bwd(proj_a, conv_w[:, :wm], dq, 0, "d_q_conv")
    dk_pre, gw_conv_k = _conv_bwd(proj_a, conv_w[:, wm:], dk, wm, "d_k_conv")
    gw_conv = jnp.concatenate([gw_conv_q, gw_conv_k], axis=1)
    dproj_a = jnp.concatenate([dq_pre, dk_pre, dv, do], axis=1)
    dproj_b = jnp.concatenate([du, dgates], axis=1)
    sums_ff2, got_ff2 = _chip_wait(red_ff2[1], token_of(red_out), "reduce_wait_ff2")
    gw_a, *update = _mm_tn("gw_in_qkvo", dproj_a, xn,
                           side=_side_adamw(*state("w_ff2"), sums_ff2[0], got_ff2[0], place))
    out["w_ff2"] = [r.reshape(weights["w_ff2"].shape) for r in update]
    for k, s, g in zip(ff2_keys[1:], sums_ff2[1:], got_ff2[1:]):
        out[k] = [r.reshape(weights[k].shape) for r in _adamw_shard(*state(k), s, g, chip, "adamw_" + k)]
    gw_b = _mm_tn("gw_in_pool_gates", dproj_b, xn)
    gw_in_t = jnp.concatenate([gw_a, gw_b[pw:pw + 2 * heads], gw_b[:pw]], axis=0)
    pair_in = pair_begin({
        "w_in": gw_in_t.reshape(N_DEV, in_shard, d),
        "w_qk_conv": gw_conv.reshape(QK_CONV, N_DEV, -1).transpose(1, 0, 2).astype(BF16)}, "in")
    reduce_finish(red_out, token_of(pair_in), "out")
    red_in = chip_begin(pair_in, out["w_out"][0], "in")
    sums_ff1, got_ff1 = _chip_wait(red_ff1[1], token_of(red_in), "reduce_wait_ff1")
    dxn_a, *update = _mm_nn("d_proj_qkvo_in", dproj_a, in_t_a, tn=512, tk=4 * wm, deps=[token_of(red_in)],
                            side=_side_adamw(*state("w_ff1"), sums_ff1[0], got_ff1[0], place))
    out["w_ff1"] = [r.reshape(weights["w_ff1"].shape) for r in update]
    dxn = _mm_nn("d_proj_in", dproj_b, in_t_b, extra=[dxn_a], epilogue=lambda acc, prev: (acc + prev,))[0]
    grad_x, dg_mix_pre = _norm_bwd(xs, dxn, dh1, g_mix_pre, "d_norm_mix_pre")

    small_grads = dict(b_gates=dbias[:, :2 * heads], g_mlstm=dg_mlstm, pool_scale=dg_pool_scale, g_mix_pre=dg_mix_pre,
                       g_mix_post=dg_mix_post, g_ff_pre=dg_ff_pre, g_ff_post=dg_ff_post, g_ple_gate=dg_ple_gate,
                       g_ple_post=dg_ple_post)
    small_begun = _everyone_start([pack(small_grads)], "small_grads_start")
    reduce_finish(red_in, small_begun["token"], "in")
    (small_parts,) = _everyone_wait(small_begun, out["w_in"][0], "small_grads_wait")
    res = _adamw_replicated(*small_state, small_parts, "adamw_replicated")
    for i, k in enumerate(small):
        width = weights[k].shape[1]
        out[k] = [r[i:i + 1, :width] for r in res]

    return (loss, grad_x[None], *[out[k][0] for k in order], *[out[k][1] for k in order],
            *[out[k][2] for k in order], *[out[k][3] for k in order])


def _pad_rows16(a):
    return jnp.pad(a, ((0, 16 - a.shape[0]), (0, 0)))
```

```python
import jax
import jax.numpy as jnp
from jax import lax
from jax.experimental import pallas as pl
from jax.experimental.pallas import tpu as pltpu

F32 = jnp.float32
BF16 = jnp.bfloat16
EPS = 1e-6
N_DEV = 8
N_CHIP = 4
LANES = 128
VMEM_LIMIT = 56 * 1024 * 1024
MLSTM_HEADS = 8
POOL_GROUPS = 4
QK_CONV = 4
CHUNK = 128
ROW_TILE = 256
ADAM_LR, ADAM_B1, ADAM_B2, ADAM_EPS, ADAM_WD, ADAM_STEP = 0.001, 0.9, 0.999, 1e-08, 0.01, 10
MESH_AXES = ("x", "y", "c")
MESH = pl.DeviceIdType.MESH
HBM_SPEC = pl.BlockSpec(memory_space=pltpu.HBM)
SEM_SPEC = pl.BlockSpec(memory_space=pltpu.SEMAPHORE)
DATAFLOW = pltpu.SideEffectType.DATAFLOW_SIDE_EFFECTING


def _params(*sem):
    if sem:
        return pltpu.CompilerParams(dimension_semantics=sem, vmem_limit_bytes=VMEM_LIMIT)
    return pltpu.CompilerParams(vmem_limit_bytes=VMEM_LIMIT)


def _tile(n, pref):
    if n <= pref:
        return n
    t = (pref // LANES) * LANES
    while t >= LANES:
        if n % t == 0:
            return t
        t -= LANES
    return n


def _hbm(a):
    return pltpu.with_memory_space_constraint(a, pltpu.HBM)


def _mesh_place():
    x, y, c = lax.axis_index("x"), lax.axis_index("y"), lax.axis_index("c")
    return x, y, c, [(1 - x, y), (x, 1 - y), (1 - x, 1 - y)]


def _chip_copies(srcs, lands, send, recv):
    x, y, c, chips = _mesh_place()
    return [pltpu.make_async_remote_copy(
        src_ref=srcs[w].at[2 * chip[0] + chip[1]], dst_ref=lands[w].at[k],
        send_sem=send.at[3 * w + k], recv_sem=recv.at[3 * w + k],
        device_id=(*chip, c), device_id_type=MESH)
        for k, chip in enumerate(chips) for w in range(len(srcs))]


def _chip_start(sums, name):
    return _split_start(_chip_copies, sums, [lax.empty((3,) + a.shape[1:], a.dtype) for a in sums], 3, name,
                        peers=_same_core_of_other_chips)


def _chip_wait(started, after, name):
    return _split_wait(_chip_copies, started, after, name)


def _pair_start(pieces, name):
    return _split_start(_pair_copies, pieces, [lax.empty((N_CHIP,) + a.shape[1:], a.dtype) for a in pieces], N_CHIP, name,
                        peers=_sibling)


def _pair_wait(started, after, name):
    return _split_wait(_pair_copies, started, after, name)


def _pair_copies(srcs, lands, send, recv):
    x, y, c, _ = _mesh_place()
    return [pltpu.make_async_remote_copy(
        src_ref=srcs[w].at[2 * ch + 1 - c], dst_ref=lands[w].at[ch],
        send_sem=send.at[N_CHIP * w + ch], recv_sem=recv.at[N_CHIP * w + ch],
        device_id=(x, y, 1 - c), device_id_type=MESH)
        for w in range(len(srcs)) for ch in range(N_CHIP)]


OWN_BARRIER_IDS = {name: i for i, name in enumerate([
    "reduce_pair_start_ff2", "reduce_pair_start_ff1", "reduce_pair_start_out", "reduce_pair_start_in",
    "gather_pair_start_out", "gather_pair_start_ff1", "gather_pair_start_ple",
    "reduce_start_ff2", "reduce_start_ff1", "reduce_start_out", "reduce_start_in",
    "gather_start_in", "gather_start_rest", "gather_pair_in", "gather_pair_ff2"])}


def _shake_hands(peers):
    barrier = pltpu.get_barrier_semaphore()
    for peer in peers:
        pl.semaphore_signal(barrier, inc=1, device_id=peer, device_id_type=MESH)
    pl.semaphore_wait(barrier, len(peers))


def _sibling():
    x, y, c, _ = _mesh_place()
    return [(x, y, 1 - c)]


def _same_core_of_other_chips():
    x, y, c, chips = _mesh_place()
    return [(*chip, c) for chip in chips]


def _split_start(copies_of, srcs, lands, per_array, name, peers=None):
    ns, nb = len(srcs), len(srcs) + len(lands)
    n_copies = per_array * len(lands)

    def body(*refs):
        if peers is not None:
            _shake_hands(peers())
        for cp in copies_of(refs[:ns], refs[ns:nb], refs[nb], refs[nb + 1]):
            cp.start()
        refs[-1][...] = jnp.zeros_like(refs[-1])

    params = dict(has_side_effects=DATAFLOW)
    if peers is not None:
        params["collective_id"] = OWN_BARRIER_IDS[name]
    res = pl.pallas_call(
        body, name=name,
        out_shape=[pltpu.SemaphoreType.DMA((n_copies,)), pltpu.SemaphoreType.DMA((n_copies,))]
        + [pltpu.HBM(a.shape, a.dtype) for a in list(srcs) + list(lands)] + [jax.ShapeDtypeStruct((8, LANES), F32)],
        in_specs=[HBM_SPEC] * nb,
        out_specs=[SEM_SPEC, SEM_SPEC] + [HBM_SPEC] * nb + [pl.BlockSpec(memory_space=pltpu.VMEM)],
        input_output_aliases={i: 2 + i for i in range(nb)},
        compiler_params=pltpu.CompilerParams(**params),
    )(*[_hbm(a) for a in srcs], *[_hbm(a) for a in lands])
    return dict(send=res[0], recv=res[1], srcs=list(res[2:2 + ns]), lands=list(res[2 + ns:2 + nb]), token=res[-1])


def _split_wait(copies_of, started, after, name):
    ns = len(started["srcs"])
    nb = ns + len(started["lands"])

    def body(*refs):
        for cp in copies_of(refs[:ns], refs[ns:nb], refs[nb], refs[nb + 1]):
            cp.wait_send()
            cp.wait_recv()

    res = pl.pallas_call(
        body, name=name,
        out_shape=[pltpu.HBM(a.shape, a.dtype) for a in started["srcs"] + started["lands"]],
        in_specs=[HBM_SPEC] * nb + [SEM_SPEC, SEM_SPEC, ANY_SPEC],
        out_specs=[HBM_SPEC] * nb,
        input_output_aliases={i: i for i in range(nb)},
        compiler_params=pltpu.CompilerParams(has_side_effects=DATAFLOW),
    )(*started["srcs"], *started["lands"], started["send"], started["recv"], after)
    return list(res[:ns]), list(res[ns:])


def _forward_copies(srcs, lands, send, recv):
    del srcs
    x, y, c, chips = _mesh_place()
    copies = []
    for w in range(len(lands)):
        for j, chip in enumerate(chips):
            rows = lands[w].at[4 * chip[0] + 2 * chip[1] + c]
            copies.append(pltpu.make_async_remote_copy(
                src_ref=rows, dst_ref=rows, send_sem=send.at[3 * w + j], recv_sem=recv.at[3 * w + j],
                device_id=(x, y, 1 - c), device_id_type=MESH))
    return copies


def _forward_start(lands, name):
    return _split_start(_forward_copies, [], lands, 3, name, peers=_sibling)


def _forward_wait(started, after, name):
    return _split_wait(_forward_copies, started, after, name)[1]


def _everyone_copies(srcs, lands, send, recv):
    x, y, c, _ = _mesh_place()
    me = 4 * x + 2 * y + c
    copies = []
    for w in range(len(srcs)):
        for k in range(N_DEV - 1):
            flip = k + 1
            peer = (1 - x if flip & 4 else x, 1 - y if flip & 2 else y, 1 - c if flip & 1 else c)
            copies.append(pltpu.make_async_remote_copy(
                src_ref=srcs[w], dst_ref=lands[w].at[me], send_sem=send.at[7 * w + k], recv_sem=recv.at[7 * w + k],
                device_id=peer, device_id_type=MESH))
    return copies


def _everyone_start(blocks, name):
    me = 4 * lax.axis_index("x") + 2 * lax.axis_index("y") + lax.axis_index("c")
    lands = [lax.dynamic_update_slice(lax.empty((N_DEV,) + a.shape, a.dtype), a[None], (me,) + (0,) * a.ndim)
             for a in blocks]
    return _split_start(_everyone_copies, blocks, lands, N_DEV - 1, name)


def _everyone_wait(started, after, name):
    return _split_wait(_everyone_copies, started, after, name)[1]


def _gather_copies(srcs, lands, send, recv):
    x, y, c, chips = _mesh_place()
    me = 4 * x + 2 * y + c
    targets = [(x, y, 1 - c)] + [(*chip, c) for chip in chips]
    return [pltpu.make_async_remote_copy(
        src_ref=srcs[w], dst_ref=lands[w].at[me], send_sem=send.at[4 * w + k], recv_sem=recv.at[4 * w + k],
        device_id=to, device_id_type=MESH)
        for w in range(len(srcs)) for k, to in enumerate(targets)]


def _gather_start(groups, name):
    sizes = [len(g) for g in groups]
    flat = [a for g in groups for a in g]
    n, ng = len(flat), len(groups)
    me = 4 * lax.axis_index("x") + 2 * lax.axis_index("y") + lax.axis_index("c")
    lands = [lax.dynamic_update_slice(lax.empty((N_DEV,) + a.shape, a.dtype), a[None], (me,) + (0,) * a.ndim)
             for a in flat]

    def body(*refs):
        _shake_hands(_sibling() + _same_core_of_other_chips())
        srcs, zones = refs[:n], refs[n:2 * n]
        sems = refs[2 * n:2 * n + 2 * ng]
        token = refs[-1]
        lo = 0
        for gi, size in enumerate(sizes):
            for cp in _gather_copies(srcs[lo:lo + size], zones[lo:lo + size], sems[2 * gi], sems[2 * gi + 1]):
                cp.start()
            lo += size
        token[...] = jnp.zeros_like(token)

    sem_shapes = []
    for size in sizes:
        sem_shapes += [pltpu.SemaphoreType.DMA((4 * size,))] * 2
    res = pl.pallas_call(
        body, name=name,
        out_shape=sem_shapes + [pltpu.HBM(a.shape, a.dtype) for a in flat + lands] + [jax.ShapeDtypeStruct((8, LANES), F32)],
        in_specs=[HBM_SPEC] * (2 * n),
        out_specs=[SEM_SPEC] * (2 * ng) + [HBM_SPEC] * (2 * n) + [pl.BlockSpec(memory_space=pltpu.VMEM)],
        input_output_aliases={i: 2 * ng + i for i in range(2 * n)},
        compiler_params=pltpu.CompilerParams(has_side_effects=DATAFLOW, collective_id=OWN_BARRIER_IDS[name]),
    )(*[_hbm(a) for a in flat], *[_hbm(a) for a in lands])
    out, lo = [], 0
    for gi, size in enumerate(sizes):
        out.append(dict(send=res[2 * gi], recv=res[2 * gi + 1],
                        srcs=list(res[2 * ng + lo:2 * ng + lo + size]),
                        lands=list(res[2 * ng + n + lo:2 * ng + n + lo + size])))
        lo += size
    return out, res[-1]


def _gather_wait(started, after, name):
    n = len(started["srcs"])
    after = list(after) if isinstance(after, (list, tuple)) else [after]

    def body(*refs):
        srcs, zones = refs[:n], refs[n:2 * n]
        send, recv = refs[2 * n], refs[2 * n + 1]
        for cp in _gather_copies(srcs, zones, send, recv):
            cp.wait_send()
            cp.wait_recv()

    res = pl.pallas_call(
        body, name=name,
        out_shape=[pltpu.HBM(a.shape, a.dtype) for a in started["srcs"] + started["lands"]],
        in_specs=[HBM_SPEC] * (2 * n) + [SEM_SPEC, SEM_SPEC] + [ANY_SPEC] * len(after),
        out_specs=[HBM_SPEC] * (2 * n),
        input_output_aliases={i: i for i in range(2 * n)},
        compiler_params=pltpu.CompilerParams(has_side_effects=DATAFLOW),
    )(*started["srcs"], *started["lands"], started["send"], started["recv"], *after)
    return list(res[n:])


def _forward_pair(lands, name):
    n = len(lands)

    def body(*refs):
        _shake_hands(_sibling())
        zones = refs[:n]
        send, recv = refs[2 * n:]
        x, y, c, chips = _mesh_place()

        def copy(w, j, core):
            rows = zones[w].at[4 * chips[j][0] + 2 * chips[j][1] + core]
            return pltpu.make_async_remote_copy(
                src_ref=rows, dst_ref=rows, send_sem=send.at[3 * w + j], recv_sem=recv.at[3 * w + j],
                device_id=(x, y, 1 - c), device_id_type=MESH)

        sends = [copy(w, j, c) for w in range(n) for j in range(3)]
        for cp in sends:
            cp.start()
        for w in range(n):
            for j in range(3):
                copy(w, j, 1 - c).wait_recv()
        for cp in sends:
            cp.wait_send()

    return list(pl.pallas_call(
        body, name=name,
        out_shape=[jax.ShapeDtypeStruct(a.shape, a.dtype) for a in lands],
        in_specs=[HBM_SPEC] * n, out_specs=[HBM_SPEC] * n,
        input_output_aliases={i: i for i in range(n)},
        scratch_shapes=[pltpu.SemaphoreType.DMA((3 * n,)), pltpu.SemaphoreType.DMA((3 * n,))],
        compiler_params=pltpu.CompilerParams(collective_id=OWN_BARRIER_IDS[name]),
    )(*lands))


def _pair_sum(pieces, got, core, name):
    _, rows, cols = pieces.shape
    tr, tc = _tile_2d(rows, cols)

    def body(core_ref, mine_ref, got_ref, out_ref):
        del core_ref
        out_ref[...] = (mine_ref[...].astype(F32) + got_ref[...].astype(F32)).astype(out_ref.dtype)

    return pl.pallas_call(
        body, name=name,
        out_shape=jax.ShapeDtypeStruct((N_CHIP, rows, cols), pieces.dtype),
        grid_spec=pltpu.PrefetchScalarGridSpec(
            num_scalar_prefetch=1, grid=(N_CHIP, rows // tr, cols // tc),
            in_specs=[pl.BlockSpec((None, tr, tc), lambda ch, i, j, core: (2 * ch + core[0], i, j)),
                      pl.BlockSpec((None, tr, tc), lambda ch, i, j, core: (ch, i, j))],
            out_specs=pl.BlockSpec((None, tr, tc), lambda ch, i, j, core: (ch, i, j))),
        compiler_params=_params("parallel", "parallel", "parallel"),
    )(core, pieces, got)


def _tile_2d(rows, cols, budget=2 * 1024 * 1024):
    want = max(16, budget // (4 * cols))
    if rows <= want:
        return rows, cols
    t = (want // 16) * 16
    while t >= 16:
        if rows % t == 0:
            return t, cols
        t -= 16
    return rows, _tile(cols, max(LANES, budget // (4 * rows)))


def _adamw_math(w, g, m, v):
    m = ADAM_B1 * m + (1.0 - ADAM_B1) * g
    v = ADAM_B2 * v + (1.0 - ADAM_B2) * (g * g)
    m_hat = m / (1.0 - ADAM_B1 ** ADAM_STEP)
    v_hat = v / (1.0 - ADAM_B2 ** ADAM_STEP)
    delta = -ADAM_LR * (m_hat / (jnp.sqrt(v_hat) + ADAM_EPS) + ADAM_WD * w)
    return delta, m, v


def _adamw_shard(w, m, v, sums, got, chip, name):
    rows, cols = w.shape
    tr, tc = _tile_2d(rows, cols, budget=1024 * 1024)

    def body(chip_ref, w_ref, m_ref, v_ref, own_ref, got_ref, g_out, d_out, m_out, v_out):
        del chip_ref
        g = own_ref[...].astype(F32)
        for k in range(3):
            g = g + got_ref[k].astype(F32)
        delta, m_new, v_new = _adamw_math(w_ref[...], g, m_ref[...], v_ref[...])
        g_out[...] = g
        d_out[...] = delta
        m_out[...] = m_new
        v_out[...] = v_new

    blk = pl.BlockSpec((tr, tc), lambda i, j, chip: (i, j))
    return pl.pallas_call(
        body, name=name,
        out_shape=[jax.ShapeDtypeStruct((rows, cols), F32)] * 4,
        grid_spec=pltpu.PrefetchScalarGridSpec(
            num_scalar_prefetch=1, grid=(rows // tr, cols // tc),
            in_specs=[blk, blk, blk,
                      pl.BlockSpec((None, tr, tc), lambda i, j, chip: (chip[0], i, j)),
                      pl.BlockSpec((3, tr, tc), lambda i, j, chip: (0, i, j))],
            out_specs=[blk] * 4),
        compiler_params=_params("parallel", "parallel"),
    )(chip, w, m, v, sums, got)


def _adamw_replicated(w, m, v, parts, name):
    rows, cols = w.shape

    def body(w_ref, m_ref, v_ref, parts_ref, g_out, d_out, m_out, v_out):
        g = parts_ref[0]
        for d in range(1, N_DEV):
            g = g + parts_ref[d]
        delta, m_new, v_new = _adamw_math(w_ref[...], g, m_ref[...], v_ref[...])
        g_out[...] = g
        d_out[...] = delta
        m_out[...] = m_new
        v_out[...] = v_new

    return pl.pallas_call(
        body, name=name,
        out_shape=[jax.ShapeDtypeStruct((rows, cols), F32)] * 4,
        compiler_params=_params(),
    )(w, m, v, parts)


NN = ((1,), (0,))
NT = ((1,), (1,))
TN = ((0,), (0,))


ANY_SPEC = pl.BlockSpec(memory_space=pl.ANY)


def _mm(name, a, b, *, dims, grid, a_spec, b_spec, outs, extra=(), extra_specs=(), epilogue=None, acc_shape=None,
        deps=(), b_pieces=0, side=None):
    nk = grid[2]
    n_extra, n_out = len(extra), len(outs)
    side_ins = side["ins"] if side else []
    side_outs = side["outs"] if side else []
    first_side = 2 + n_extra + len(deps)
    first_out = first_side + len(side_ins)
    assert not side or nk == 1

    assert not side or (epilogue is None and n_out == 1 and not b_pieces)
    SIDE_SLICES = 4

    def body(*refs):
        dot = lambda lhs, rhs: lax.dot_general(lhs.astype(BF16), rhs.astype(BF16), (dims, ((), ())),
                                               preferred_element_type=F32)
        if side:
            refs = refs[1:]
            a_ref, b_ref, out_ref = refs[0], refs[1], refs[first_out]
            ins, sides = refs[first_side:first_out], refs[first_out + 1:first_out + 1 + len(side_outs)]
            rows = out_ref.shape[0] // SIDE_SLICES
            srows = sides[0].shape[-2] // SIDE_SLICES
            for part in range(SIDE_SLICES):
                rs = slice(part * srows, (part + 1) * srows)
                for val, o in zip(side["fn"](*[r[..., rs, :] for r in ins]), sides):
                    o[rs, :] = val.astype(o.dtype)
                ms = slice(part * rows, (part + 1) * rows)
                lhs = a_ref[:, ms] if dims == TN else a_ref[ms, :]
                out_ref[ms, :] = dot(lhs, b_ref[...]).astype(out_ref.dtype)
            return
        a_ref, b_ref = refs[0], refs[1]
        ex = refs[2:2 + n_extra]
        out_refs = refs[first_out:first_out + n_out]
        if b_pieces:
            ks = b_ref.shape[-1]
            part = dot(a_ref[:, 0:ks], b_ref[0])
            for piece in range(1, b_pieces):
                part = part + dot(a_ref[:, piece * ks:(piece + 1) * ks], b_ref[piece])
        else:
            part = dot(a_ref[...], b_ref[...])

        def finish(acc):
            res = epilogue(acc, *[e[...] for e in ex]) if epilogue else (acc,)
            for val, o in zip(res, out_refs):
                o[...] = val.astype(o.dtype)

        if nk == 1:
            finish(part)
        else:
            acc_ref = refs[-1]
            k = pl.program_id(2)

            @pl.when(k == 0)
            def _():
                acc_ref[...] = part

            @pl.when(k > 0)
            def _():
                acc_ref[...] += part

            @pl.when(k == nk - 1)
            def _():
                finish(acc_ref[...])

    scratch = [pltpu.VMEM(acc_shape, F32)] if nk > 1 else []
    out_shape = [jax.ShapeDtypeStruct(s, d) for s, d, _ in outs]
    in_specs = [a_spec, b_spec, *extra_specs] + [ANY_SPEC] * len(deps)
    out_specs = [sp for _, _, sp in outs]
    if not side:
        return list(pl.pallas_call(
            body, name=name, out_shape=out_shape, grid=grid, in_specs=in_specs, out_specs=out_specs,
            scratch_shapes=scratch, compiler_params=_params("parallel", "parallel", "arbitrary"),
        )(a, b, *extra, *deps))

    def with_scalars(spec):
        if spec.index_map is None:
            return spec
        return pl.BlockSpec(spec.block_shape, lambda i, j, k, s, f=spec.index_map: f(i, j, k))

    def by_step(block, index_map):
        return pl.BlockSpec(block, lambda i, j, k, s: index_map((i * grid[1] + j) * grid[2] + k, s))

    return list(pl.pallas_call(
        body, name=name,
        out_shape=out_shape + [jax.ShapeDtypeStruct(s, d) for s, d, _, _ in side_outs],
        grid_spec=pltpu.PrefetchScalarGridSpec(
            num_scalar_prefetch=1, grid=grid,
            in_specs=[with_scalars(sp) for sp in in_specs] + [by_step(blk, im) for _, blk, im in side_ins],
            out_specs=[with_scalars(sp) for sp in out_specs] + [by_step(blk, im) for _, _, blk, im in side_outs],
            scratch_shapes=scratch),
        compiler_params=_params("arbitrary", "arbitrary", "arbitrary"),
    )(side["scalars"], a, b, *extra, *deps, *[arr for arr, _, _ in side_ins]))


def _side_pair_sum(pieces, got, place):
    _, rows, cols = pieces.shape

    def build(steps):
        per_chip = steps // N_CHIP
        tr = rows // per_chip
        assert per_chip * N_CHIP == steps and tr * per_chip == rows and tr % 16 == 0
        blk = (None, tr, cols)
        here = lambda st, s: (st // per_chip, st % per_chip, 0)
        return dict(scalars=place,
                    ins=[(pieces, blk, lambda st, s: (2 * (st // per_chip) + s[0], st % per_chip, 0)), (got, blk, here)],
                    outs=[((N_CHIP, rows, cols), pieces.dtype, blk, here)],
                    fn=lambda mine, other: (mine.astype(F32) + other.astype(F32),))
    return build


def _side_adamw(w, m, v, sums, got, place):
    rows, cols = w.shape

    def build(steps):
        tr = rows // steps
        assert tr * steps == rows and tr % 16 == 0
        blk, here = (tr, cols), (lambda st, s: (st, 0))

        def fn(w, m, v, own, got):
            g = own.astype(F32)
            for k in range(3):
                g = g + got[k].astype(F32)
            delta, m_new, v_new = _adamw_math(w, g, m, v)
            return g, delta, m_new, v_new

        return dict(scalars=place,
                    ins=[(w, blk, here), (m, blk, here), (v, blk, here),
                         (sums, (None, tr, cols), lambda st, s: (s[1], st, 0)),
                         (got, (3, tr, cols), lambda st, s: (0, st, 0))],
                    outs=[((rows, cols), F32, blk, here)] * 4, fn=fn)
    return build


def _mm_nn(name, a, b, out_dtype=F32, tm=1024, tn=1024, tk=2048, epilogue=None, n_out=1, extra=(), deps=(), side=None):
    m, kd = a.shape
    n = b.shape[1]
    tm, tn, tk = _tile(m, tm), _tile(n, tn), _tile(kd, tk)
    dts = out_dtype if isinstance(out_dtype, (list, tuple)) else [out_dtype] * n_out
    o_spec = pl.BlockSpec((tm, tn), lambda i, j, k: (i, j))
    grid = (m // tm, n // tn, kd // tk)
    return _mm(name, a, b, dims=NN, grid=grid,
               a_spec=pl.BlockSpec((tm, tk), lambda i, j, k: (i, k)),
               b_spec=pl.BlockSpec((tk, tn), lambda i, j, k: (k, j)),
               outs=[((m, n), dt, o_spec) for dt in dts], extra=extra, extra_specs=[o_spec] * len(extra),
               epilogue=epilogue, acc_shape=(tm, tn), deps=deps, side=side and side(grid[0] * grid[1] * grid[2]))


def _mm_nn_pieces(name, a, b, out_dtype=F32, tm=1024, tk=2048, epilogue=None, n_out=1):
    m, kd = a.shape
    npc, _, ns = b.shape
    tm, tk = _tile(m, tm), _tile(kd, tk)
    tn = _tile(ns, 1024)
    r = ns // tn
    dts = out_dtype if isinstance(out_dtype, (list, tuple)) else [out_dtype] * n_out
    o_spec = pl.BlockSpec((tm, tn), lambda i, j, k: (i, j))
    return _mm(name, a, b, dims=NN, grid=(m // tm, npc * r, kd // tk),
               a_spec=pl.BlockSpec((tm, tk), lambda i, j, k: (i, k)),
               b_spec=pl.BlockSpec((None, tk, tn), lambda i, j, k: (j // r, k, j % r)),
               outs=[((m, npc * ns), dt, o_spec) for dt in dts], epilogue=epilogue, acc_shape=(tm, tn))


def _mm_nt(name, a, b, out_dtype=F32, tm=1024, tn=1024, tk=2048, epilogue=None, extra=(), n_rows=None, deps=(),
           side=None):
    m, kd = a.shape
    n = n_rows or b.shape[0]
    tm, tn, tk = _tile(m, tm), _tile(n, tn), _tile(kd, tk)
    o_spec = pl.BlockSpec((tm, tn), lambda i, j, k: (i, j))
    grid = (m // tm, n // tn, kd // tk)
    res = _mm(name, a, b, dims=NT, grid=grid,
              a_spec=pl.BlockSpec((tm, tk), lambda i, j, k: (i, k)),
              b_spec=pl.BlockSpec((tn, tk), lambda i, j, k: (j, k)),
              outs=[((m, n), out_dtype, o_spec)], extra=extra, extra_specs=[o_spec] * len(extra),
              epilogue=epilogue, acc_shape=(tm, tn), deps=deps, side=side and side(grid[0] * grid[1] * grid[2]))
    return res if side else res[0]


def _mm_nt_pieces(name, a, b, out_dtype=F32, tm=1024, tn=256, deps=()):
    m = a.shape[0]
    npc, n, ks = b.shape
    tm, tn = _tile(m, tm), _tile(n, tn)
    return _mm(name, a, b, dims=NT, grid=(m // tm, n // tn, 1),
               a_spec=pl.BlockSpec((tm, npc * ks), lambda i, j, k: (i, 0)),
               b_spec=pl.BlockSpec((npc, tn, ks), lambda i, j, k: (0, j, 0)),
               outs=[((m, n), out_dtype, pl.BlockSpec((tm, tn), lambda i, j, k: (i, j)))],
               deps=deps, b_pieces=npc)[0]


def _mm_tn(name, a, b, out_dtype=BF16, tm=1024, tn=1024, pieces=0, deps=(), side=None):
    t, m = a.shape
    n = b.shape[1]
    tm = _tile(m, tm)
    if pieces:
        ns = n // pieces
        tn = _tile(ns, tn)
        r = ns // tn
        out = ((pieces, m, ns), out_dtype, pl.BlockSpec((None, tm, tn), lambda i, j, k: (j // r, i, j % r)))
    else:
        tn = _tile(n, tn)
        out = ((m, n), out_dtype, pl.BlockSpec((tm, tn), lambda i, j, k: (i, j)))
    grid = (m // tm, n // tn, 1)
    res = _mm(name, a, b, dims=TN, grid=grid,
              a_spec=pl.BlockSpec((t, tm), lambda i, j, k: (0, i)),
              b_spec=pl.BlockSpec((t, tn), lambda i, j, k: (0, j)),
              outs=[out], deps=deps, side=side and side(grid[0] * grid[1]))
    return res if side else res[0]


def _rms_f(x, g):
    r = lax.rsqrt(jnp.mean(x * x, axis=-1, keepdims=True) + EPS)
    xh = x * r
    return xh * g, xh, r


def _rms_b(dy, g, xh, r):
    dxh = dy * g
    dx = r * (dxh - xh * jnp.mean(dxh * xh, axis=-1, keepdims=True))
    dg = jnp.sum(dy * xh, axis=0, keepdims=True)
    return dx, dg


def _rowwise(name, fn, row_ins, vec_ins, row_outs, vec_outs, deps=()):
    t = row_ins[0].shape[0]
    tr = min(ROW_TILE, t)
    nr, nv, no = len(row_ins), len(vec_ins), len(row_outs)
    first_out = nr + nv + len(deps)

    def body(*refs):
        ri, vi = refs[:nr], refs[nr:nr + nv]
        ro, vo = refs[first_out:first_out + no], refs[first_out + no:]
        routs, vouts = fn(*[r[...] for r in ri], *[v[...] for v in vi])
        for o, val in zip(ro, routs):
            o[...] = val.astype(o.dtype)
        i = pl.program_id(0)
        for o, val in zip(vo, vouts):
            @pl.when(i == 0)
            def _():
                o[...] = val

            @pl.when(i > 0)
            def _():
                o[...] += val

    res = pl.pallas_call(
        body, name=name,
        out_shape=[jax.ShapeDtypeStruct((t, w), d) for w, d in row_outs]
        + [jax.ShapeDtypeStruct((1, w), F32) for w in vec_outs],
        grid=(t // tr,),
        in_specs=[pl.BlockSpec((tr, a.shape[1]), lambda i: (i, 0)) for a in row_ins]
        + [pl.BlockSpec((1, a.shape[1]), lambda i: (0, 0)) for a in vec_ins] + [ANY_SPEC] * len(deps),
        out_specs=[pl.BlockSpec((tr, w), lambda i: (i, 0)) for w, _ in row_outs]
        + [pl.BlockSpec((1, w), lambda i: (0, 0)) for w in vec_outs],
        compiler_params=_params("arbitrary"),
    )(*row_ins, *vec_ins, *deps)
    return list(res)


def _norm_fwd(x, g, name, deps=()):
    d = x.shape[1]
    return _rowwise(name, lambda x, g: ((_rms_f(x, g)[0],), ()), [x], [g], [(d, BF16)], [], deps=deps)[0]


def _norm_bwd(x, dxn, dres, g, name):
    d = x.shape[1]

    def fn(x, dxn, dres, g):
        _, xh, r = _rms_f(x, g)
        dx, dg = _rms_b(dxn, g, xh, r)
        return (dres + dx,), (dg,)

    return _rowwise(name, fn, [x, dxn, dres], [g], [(d, F32)], [d])


def _block_fwd(h_prev, y, g_y, g_n, name, deps=()):
    d = h_prev.shape[1]

    def fn(h_prev, y, g_y, g_n):
        h = h_prev + _rms_f(y, g_y)[0]
        return (h, _rms_f(h, g_n)[0]), ()

    return _rowwise(name, fn, [h_prev, y], [g_y, g_n], [(d, F32), (d, BF16)], [], deps=deps)


def _block_bwd(h, y, d_up, d_n, g_y, g_n, name, deps=()):
    d = h.shape[1]

    def fn(h, y, d_up, d_n, g_y, g_n):
        _, hh, hr = _rms_f(h, g_n)
        dh_n, dg_n = _rms_b(d_n, g_n, hh, hr)
        dh = d_up + dh_n
        _, yh, yr = _rms_f(y, g_y)
        dy, dg_y = _rms_b(dh, g_y, yh, yr)
        return (dh, dy), (dg_y, dg_n)

    return _rowwise(name, fn, [h, y, d_up, d_n], [g_y, g_n], [(d, F32), (d, BF16)], [d, d], deps=deps)


def _loss_head(h2, z, e, target, g, name):
    d = h2.shape[1]

    def fn(h2, z, e, target, g):
        gate = jax.nn.sigmoid(z)
        y, xh, r = _rms_f(e * gate, g)
        diff = h2 + y - target
        loss = 0.5 * jnp.sum(jnp.mean(diff * diff, axis=-1, keepdims=True), axis=0, keepdims=True)
        dh3 = diff / d
        deg, dg = _rms_b(dh3, g, xh, r)
        dz = deg * e * gate * (1.0 - gate)
        de = deg * gate
        return (dh3, dz, de), (dg, jnp.broadcast_to(loss, (1, LANES)))

    return _rowwise(name, fn, [h2, z, e, target], [g], [(d, F32), (d, BF16), (d, BF16)], [d, LANES])


def _shift_down(x, s, rows):
    return jnp.where(rows >= s, pltpu.roll(x, s, axis=0), 0.0)


def _shift_up(x, s, rows):
    t = x.shape[0]
    return jnp.where(rows < t - s, pltpu.roll(x, t - s, axis=0), 0.0)


def _conv_pre(x, w, rows):
    pre = x * w[QK_CONV - 1:QK_CONV, :]
    for s in range(1, QK_CONV):
        pre = pre + _shift_down(x, s, rows) * w[QK_CONV - 1 - s:QK_CONV - s, :]
    return pre


def _conv_fwd(proj_a, w, start, name):
    t = proj_a.shape[0]
    width = w.shape[1]
    tc = _tile(width, 256)
    off = start // tc

    def body(x_ref, w_ref, y_ref):
        x = x_ref[...]
        rows = lax.broadcasted_iota(jnp.int32, x.shape, 0)
        pre = _conv_pre(x, w_ref[...], rows)
        y_ref[...] = pre * jax.nn.sigmoid(pre)

    return pl.pallas_call(
        body, name=name,
        out_shape=jax.ShapeDtypeStruct((t, width), F32),
        grid=(width // tc,),
        in_specs=[pl.BlockSpec((t, tc), lambda j: (0, off + j)), pl.BlockSpec((QK_CONV, tc), lambda j: (0, j))],
        out_specs=pl.BlockSpec((t, tc), lambda j: (0, j)),
        compiler_params=_params("parallel"),
    )(proj_a, w)


def _conv_bwd(proj_a, w, dy, start, name):
    t = proj_a.shape[0]
    width = w.shape[1]
    tc = _tile(width, 256)
    off = start // tc

    def body(x_ref, w_ref, dy_ref, dx_ref, dw_ref):
        x, w = x_ref[...], w_ref[...]
        rows = lax.broadcasted_iota(jnp.int32, x.shape, 0)
        pre = _conv_pre(x, w, rows)
        sig = jax.nn.sigmoid(pre)
        dpre = dy_ref[...] * (sig * (1.0 + pre * (1.0 - sig)))
        dx = dpre * w[QK_CONV - 1:QK_CONV, :]
        dws = [jnp.sum(dpre * x, axis=0, keepdims=True)]
        for s in range(1, QK_CONV):
            dx = dx + _shift_up(dpre, s, rows) * w[QK_CONV - 1 - s:QK_CONV - s, :]
            dws.append(jnp.sum(dpre * _shift_down(x, s, rows), axis=0, keepdims=True))
        dx_ref[...] = dx.astype(dx_ref.dtype)
        for s in range(QK_CONV):
            dw_ref[QK_CONV - 1 - s:QK_CONV - s, :] = dws[s]

    return pl.pallas_call(
        body, name=name,
        out_shape=[jax.ShapeDtypeStruct((t, width), BF16), jax.ShapeDtypeStruct((QK_CONV, width), F32)],
        grid=(width // tc,),
        in_specs=[pl.BlockSpec((t, tc), lambda j: (0, off + j)), pl.BlockSpec((QK_CONV, tc), lambda j: (0, j)),
                  pl.BlockSpec((t, tc), lambda j: (0, j))],
        out_specs=[pl.BlockSpec((t, tc), lambda j: (0, j)), pl.BlockSpec((QK_CONV, tc), lambda j: (0, j))],
        compiler_params=_params("parallel"),
    )(proj_a, w, dy)


def _dot(a, b, dims):
    return lax.dot_general(a.astype(BF16), b.astype(BF16), (dims, ((), ())), preferred_element_type=F32)


def _dot_split(ones, x):
    hi = x.astype(BF16)
    rest = x - hi.astype(F32)
    mid = rest.astype(BF16)
    lo = (rest - mid.astype(F32)).astype(BF16)
    dot = lambda part: lax.dot_general(ones, part, (NN, ((), ())), preferred_element_type=F32)
    return dot(hi) + dot(mid) + dot(lo)


def _chunk_iotas():
    rows = lax.broadcasted_iota(jnp.int32, (CHUNK, CHUNK), 0)
    cols = lax.broadcasted_iota(jnp.int32, (CHUNK, CHUNK), 1)
    return rows, cols


def _gates_fwd(proj_b, bias, col_block, heads, name):
    t = proj_b.shape[0]

    def body(x_ref, b_ref, gb_ref, gbt_ref, cols_ref):
        rows, cols = _chunk_iotas()
        tri = (cols <= rows).astype(BF16)
        for ci in range(t // CHUNK):
            sl = slice(ci * CHUNK, (ci + 1) * CHUNK)
            pre = x_ref[sl, :] + b_ref[...]
            log_f = jnp.minimum(pre, 0.0) - jnp.log(1.0 + jnp.exp(-jnp.abs(pre)))
            log_f = jnp.where((cols >= heads) & (cols < 2 * heads), log_f, 0.0)
            tile = jnp.where(cols < heads, pre, _dot_split(tri, log_f))
            gb_ref[sl, :] = tile
            gbt_ref[:, sl] = tile.T
            for h in range(2 * heads):
                cols_ref[h, sl, :] = jnp.broadcast_to(tile[:, h:h + 1], (CHUNK, LANES))

    return pl.pallas_call(
        body, name=name,
        out_shape=[jax.ShapeDtypeStruct((t, LANES), F32), jax.ShapeDtypeStruct((LANES, t), F32),
                   jax.ShapeDtypeStruct((2 * heads, t, LANES), F32)],
        grid=(1,),
        in_specs=[pl.BlockSpec((t, LANES), lambda i: (0, col_block)), pl.BlockSpec((1, LANES), lambda i: (0, 0))],
        out_specs=[pl.BlockSpec((t, LANES), lambda i: (0, 0)), pl.BlockSpec((LANES, t), lambda i: (0, 0)),
                   pl.BlockSpec((2 * heads, t, LANES), lambda i: (0, 0, 0))],
        compiler_params=_params("arbitrary"),
    )(proj_b, bias)


def _gates_bwd(proj_b, bias, dcol, drow, col_block, heads, name):
    t = proj_b.shape[0]

    def body(x_ref, b_ref, dc_ref, dr_ref, dx_ref, db_ref):
        rows, cols = _chunk_iotas()
        later = (cols >= rows).astype(BF16)
        total = jnp.zeros((1, LANES), F32)
        for ci in range(t // CHUNK):
            sl = slice(ci * CHUNK, (ci + 1) * CHUNK)
            by_row = dr_ref[:, sl].T
            d = dc_ref[sl, :] + jnp.where(cols < heads, by_row,
                                          jnp.where(cols < 2 * heads, -pltpu.roll(by_row, heads, axis=1), 0.0))
            d_log_f = _dot_split(later, jnp.where(cols >= heads, d, 0.0))
            pre = x_ref[sl, :] + b_ref[...]
            dx = jnp.where(cols < heads, d, jnp.where(cols < 2 * heads, d_log_f * jax.nn.sigmoid(-pre), 0.0))
            dx_ref[sl, :] = dx.astype(dx_ref.dtype)
            total = total + jnp.sum(dx, axis=0, keepdims=True)
        db_ref[...] = total

    return pl.pallas_call(
        body, name=name,
        out_shape=[jax.ShapeDtypeStruct((t, LANES), BF16), jax.ShapeDtypeStruct((1, LANES), F32)],
        grid=(1,),
        in_specs=[pl.BlockSpec((t, LANES), lambda i: (0, col_block)), pl.BlockSpec((1, LANES), lambda i: (0, 0)),
                  pl.BlockSpec((t, LANES), lambda i: (0, 0)), pl.BlockSpec((LANES, t), lambda i: (0, 0))],
        out_specs=[pl.BlockSpec((t, LANES), lambda i: (0, 0)), pl.BlockSpec((1, LANES), lambda i: (0, 0))],
        compiler_params=_params("arbitrary"),
    )(proj_b, bias, dcol, drow)


def _gate_views(li_ref, b_ref, gbt_ref, s, head, heads, sl):
    li_c, b_c = li_ref[s, sl, :], b_ref[s, sl, :]
    li_r = jnp.broadcast_to(gbt_ref[head, :, sl], (CHUNK, CHUNK))
    b_r = jnp.broadcast_to(gbt_ref[heads + head, :, sl], (CHUNK, CHUNK))
    return li_c, b_c, li_r, b_r


def _chunk_state(kt, k, v, li_c, b_c, ct, nrow, m_prev):
    b_tot = b_c[CHUNK - 1:CHUNK, 0:1]
    a = b_tot - b_c[:, 0:1] + li_c[:, 0:1]
    m_new = jnp.maximum(b_tot + m_prev, jnp.max(a, axis=0, keepdims=True))
    dec = jnp.exp(b_tot + m_prev - m_new)
    w = jnp.exp(a - m_new)
    wv = w * v
    ct_new = dec * ct + _dot(kt, wv, NN)
    n_new = dec * nrow + jnp.sum(w * k, axis=0, keepdims=True)
    return ct_new, n_new, m_new, dec, w, wv


def _chunk_scores(q, k, b_c, li_r, b_r, m_prev):
    rows, cols = _chunk_iotas()
    dlog = jnp.where(cols <= rows, b_c - b_r + li_r, -jnp.inf)
    inter = b_c[:, 0:1] + m_prev
    mt = jnp.maximum(inter, jnp.max(dlog, axis=-1, keepdims=True))
    dw = jnp.exp(dlog - mt)
    iw = jnp.exp(inter - mt)
    a = _dot(q, k, NT)
    return dw, iw, mt, a, a * dw


def _to_row(column):
    rows, cols = _chunk_iotas()
    return jnp.sum(jnp.where(rows == cols, jnp.broadcast_to(column, (CHUNK, CHUNK)), 0.0), axis=0, keepdims=True)


def _head_out(h, o, g):
    cell, hh, r = _rms_f(h, g)
    sig = jax.nn.sigmoid(o)
    return sig * cell, (cell, hh, r, sig)


HEADS_PER_STEP = 2


def _mlstm_fwd(qc, kc, kct, proj_a, gcols, gbt, g_mlstm, heads, d_model, name, dep):
    t = qc.shape[0]
    dh = g_mlstm.shape[1] // heads
    nc = t // CHUNK
    scale = dh ** -0.5
    hp = HEADS_PER_STEP

    def body(q_ref, k_ref, kt_ref, v_ref, o_ref, li_ref, b_ref, gbt_ref, g_ref, dep_ref, out_ref, ct_out, nm_out,
             ct_ref, n_ref, m_ref):
        del dep_ref
        step = pl.program_id(0)
        ct_ref[...] = jnp.zeros_like(ct_ref)
        n_ref[...] = jnp.zeros_like(n_ref)
        m_ref[...] = jnp.zeros_like(m_ref)
        tile_row = lax.broadcasted_iota(jnp.int32, (8, LANES), 0)

        def chunk(ci, carry):
            sl = pl.ds(pl.multiple_of(ci * CHUNK, CHUNK), CHUNK)
            for s in range(hp):
                cs = slice(s * dh, (s + 1) * dh)
                q, k, v, kt = q_ref[sl, cs], k_ref[sl, cs] * scale, v_ref[sl, cs], kt_ref[cs, sl]
                li_c, b_c, li_r, b_r = _gate_views(li_ref, b_ref, gbt_ref, s, step * hp + s, heads, sl)
                ct, nrow, m_prev = ct_ref[s], n_ref[s, 0:1, :], m_ref[s, 0:1, 0:1]
                ct_out[s, ci] = ct
                nm_out[s, ci] = jnp.where(tile_row == 0, nrow, jnp.where(tile_row == 1, m_prev, 0.0))
                dw, iw, mt, _, sm = _chunk_scores(q, k, b_c, li_r, b_r, m_prev)
                iq = iw * q
                num = _dot(jnp.concatenate([iq, sm], axis=1), jnp.concatenate([ct, v], axis=0), NN)
                den = jnp.sum(iq * nrow + sm, axis=-1, keepdims=True)
                h = num / jnp.maximum(jnp.abs(den), jnp.exp(-mt))
                y, _ = _head_out(h, o_ref[sl, cs], g_ref[:, cs])
                out_ref[sl, cs] = y.astype(out_ref.dtype)
                ct_new, n_new, m_new, _, _, _ = _chunk_state(kt, k, v, li_c, b_c, ct, nrow, m_prev)
                ct_ref[s] = ct_new
                n_ref[s, 0:1, :] = n_new
                m_ref[s, 0:1, 0:1] = m_new
            return carry

        lax.fori_loop(0, nc, chunk, 0, unroll=2)

    col = lambda off: pl.BlockSpec((t, hp * dh), lambda h: (0, off // hp + h))
    slab = lambda off: pl.BlockSpec((hp, t, LANES), lambda h: (off // hp + h, 0, 0))
    return pl.pallas_call(
        body, name=name,
        out_shape=[jax.ShapeDtypeStruct((t, d_model), BF16), jax.ShapeDtypeStruct((heads, nc, dh, dh), F32),
                   jax.ShapeDtypeStruct((heads, nc, 8, LANES), F32)],
        grid=(heads // hp,),
        in_specs=[col(0), col(0), pl.BlockSpec((hp * dh, t), lambda h: (h, 0)), col(2 * heads), col(3 * heads),
                  slab(0), slab(heads), pl.BlockSpec((2 * heads, 1, t), lambda h: (0, 0, 0)),
                  pl.BlockSpec((1, hp * dh), lambda h: (0, h)), ANY_SPEC],
        out_specs=[pl.BlockSpec((t, hp * dh), lambda h: (0, h)),
                   pl.BlockSpec((hp, nc, dh, dh), lambda h: (h, 0, 0, 0)),
                   pl.BlockSpec((hp, nc, 8, LANES), lambda h: (h, 0, 0, 0))],
        scratch_shapes=[pltpu.VMEM((hp, dh, dh), F32), pltpu.VMEM((hp, 8, dh), F32), pltpu.VMEM((hp, 8, LANES), F32)],
        compiler_params=_params("arbitrary"),
    )(qc, kc, kct, proj_a, proj_a, gcols, gcols, gbt, g_mlstm, dep)


def _mlstm_bwd(qc, kc, qct, kct, proj_a, gcols, gbt, g_mlstm, states, dhcat, heads, name, dep):
    t = qc.shape[0]
    wm = g_mlstm.shape[1]
    dh = wm // heads
    nc = t // CHUNK
    scale = dh ** -0.5

    def body(q_ref, k_ref, qt_ref, kt_ref, v_ref, o_ref, li_ref, b_ref, gbt_ref, g_ref, ct_all, nm_all, dy_ref, dep_ref,
             dq_ref, dk_ref, dv_ref, do_ref, dcol_ref, drow_ref, dg_ref, dct_ref, dn_ref):
        del dep_ref
        head = pl.program_id(0)
        g = g_ref[...]

        @pl.when(head == 0)
        def _():
            dcol_ref[...] = jnp.zeros_like(dcol_ref)
            drow_ref[...] = jnp.zeros_like(drow_ref)

        def load(ci):
            sl = pl.ds(pl.multiple_of(ci * CHUNK, CHUNK), CHUNK)
            return (sl, q_ref[sl, :], k_ref[sl, :] * scale, v_ref[sl, :], kt_ref[:, sl]) \
                + _gate_views(li_ref, b_ref, gbt_ref, 0, head, heads, sl)

        dct_ref[...] = jnp.zeros_like(dct_ref)
        dn_ref[...] = jnp.zeros_like(dn_ref)
        dg_ref[...] = jnp.zeros_like(dg_ref)
        rows, cols = _chunk_iotas()
        last = lax.broadcasted_iota(jnp.int32, (CHUNK, 1), 0) == CHUNK - 1
        lane = lax.broadcasted_iota(jnp.int32, (CHUNK, LANES), 1)

        def bwd(step, carry):
            ci = nc - 1 - step
            sl, q, k, v, kt, li_c, b_c, li_r, b_r = load(ci)
            qt = qt_ref[:, sl]
            ct, nrow, m_prev = ct_all[ci], nm_all[ci, 0:1, :], nm_all[ci, 1:2, 0:1]
            dw, iw, mt, a, sm = _chunk_scores(q, k, b_c, li_r, b_r, m_prev)
            qc_ = _dot(q, ct, NN)
            qn = jnp.sum(q * nrow, axis=-1, keepdims=True)
            num = iw * qc_ + _dot(sm, v, NN)
            den = iw * qn + jnp.sum(sm, axis=-1, keepdims=True)
            floor = jnp.exp(-mt)
            dn = jnp.maximum(jnp.abs(den), floor)
            h = num / dn
            _, (cell, hh, r, sig) = _head_out(h, o_ref[sl, :], g)
            dy = dy_ref[sl, :]
            dcell = dy * sig
            do_ref[sl, :] = (dy * cell * sig * (1.0 - sig)).astype(do_ref.dtype)
            dh_, dg = _rms_b(dcell, g, hh, r)
            dg_ref[...] += dg
            dnum = dh_ / dn
            ddn = -jnp.sum(dh_ * h, axis=-1, keepdims=True) / dn
            dden = jnp.where(jnp.abs(den) >= floor, ddn * jnp.sign(den), 0.0)
            idn = iw * dnum
            idd = iw * dden
            dw_t = jnp.exp(jnp.where(cols >= rows, b_r - b_c + li_c, -jnp.inf) - _to_row(mt))
            sm_t = _dot(k, q, NT) * dw_t
            dsm = _dot(dnum, v, NT) + dden
            da_ = dsm * dw
            da_t = (_dot(v, dnum, NT) + _to_row(dden)) * dw_t
            e = da_ * a
            dq = _dot(idn, ct, NT) + _dot(da_, k, NN) + idd * nrow
            dk = _dot(da_t, q, NN)
            dv = _dot(sm_t, dnum, NN)
            dct = _dot(qt, idn, NN)
            dnr = jnp.sum(q * idd, axis=0, keepdims=True)
            db = jnp.sum(e + idn * qc_, axis=-1, keepdims=True) + idd * qn
            _, _, _, dec, w, wv = _chunk_state(kt, k, v, li_c, b_c, ct, nrow, m_prev)
            dct_new, dn_new = dct_ref[...], dn_ref[0:1, :]
            ddec = jnp.sum(jnp.sum(dct_new * ct, axis=-1, keepdims=True), axis=0, keepdims=True) \
                + jnp.sum(dn_new * nrow, axis=-1, keepdims=True)
            dk = dk + _dot(wv, dct_new, NT) + w * dn_new
            dwv = _dot(k, dct_new, NN)
            dv = dv + w * dwv
            dwt = jnp.sum(dwv * v + k * dn_new, axis=-1, keepdims=True)
            da = dwt * w
            dbtot = jnp.sum(da, axis=0, keepdims=True) + ddec * dec
            db = db - da + jnp.where(last, dbtot, 0.0)
            dct_ref[...] = dec * dct_new + dct
            dn_ref[0:1, :] = dec * dn_new + dnr
            dq_ref[sl, :] = dq
            dk_ref[sl, :] = dk * scale
            dv_ref[sl, :] = dv.astype(dv_ref.dtype)
            dcol_ref[sl, :] += jnp.where(lane == head, da, 0.0) + jnp.where(lane == heads + head, db, 0.0)
            drow_ref[head, :, sl] = jnp.sum(e, axis=0, keepdims=True)
            return carry

        lax.fori_loop(0, nc, bwd, 0, unroll=8)

    col = lambda off: pl.BlockSpec((t, dh), lambda h: (0, off + h))
    rowb = pl.BlockSpec((dh, t), lambda h: (h, 0))
    slab = lambda off: pl.BlockSpec((1, t, LANES), lambda h: (off + h, 0, 0))
    vec = pl.BlockSpec((1, dh), lambda h: (0, h))
    return pl.pallas_call(
        body, name=name,
        out_shape=[jax.ShapeDtypeStruct((t, wm), F32)] * 2 + [jax.ShapeDtypeStruct((t, wm), BF16)] * 2
        + [jax.ShapeDtypeStruct((t, LANES), F32), jax.ShapeDtypeStruct((heads, 1, t), F32),
           jax.ShapeDtypeStruct((1, wm), F32)],
        grid=(heads,),
        in_specs=[col(0), col(0), rowb, rowb, col(2 * heads), col(3 * heads), slab(0), slab(heads),
                  pl.BlockSpec((2 * heads, 1, t), lambda h: (0, 0, 0)), vec,
                  pl.BlockSpec((None, nc, dh, dh), lambda h: (h, 0, 0, 0)),
                  pl.BlockSpec((None, nc, 8, LANES), lambda h: (h, 0, 0, 0)), col(0), ANY_SPEC],
        out_specs=[col(0), col(0), col(0), col(0), pl.BlockSpec((t, LANES), lambda h: (0, 0)),
                   pl.BlockSpec((heads, 1, t), lambda h: (0, 0, 0)), vec],
        scratch_shapes=[pltpu.VMEM((dh, dh), F32), pltpu.VMEM((8, dh), F32)],
        compiler_params=_params("arbitrary"),
    )(qc, kc, qct, kct, proj_a, proj_a, gcols, gcols, gbt, g_mlstm, *states, dhcat, dep)


def _pool_window(gi):
    return jnp.where(gi == 0, 2.0, jnp.where(gi == 1, 4.0, jnp.where(gi == 2, 8.0, 16.0))).astype(F32)


def _pool_select(gi, levels):
    return jnp.where(gi == 0, levels[0], jnp.where(gi == 1, levels[1], jnp.where(gi == 2, levels[2], levels[3])))


def _pooled(u, gi, rows):
    s, levels = u, []
    for lvl in range(POOL_GROUPS):
        s = s + _shift_down(s, 1 << lvl, rows)
        levels.append(s)
    cnt = jnp.minimum((rows + 1).astype(F32), _pool_window(gi))
    return _pool_select(gi, levels) / cnt - u, cnt


def _pool_fwd(proj_b, w_pool, scale, hcat, name):
    t, d_model = hcat.shape
    g, cg, _ = w_pool.shape
    off = (d_model - g * cg) // cg

    def body(u_ref, w_ref, s_ref, hcat_ref, out_ref):
        del hcat_ref
        gi = pl.program_id(0)
        u = u_ref[...]
        rows = lax.broadcasted_iota(jnp.int32, u.shape, 0)
        pooled, _ = _pooled(u, gi, rows)
        out_ref[...] = (_dot(pooled, w_ref[...], NN) * s_ref[...]).astype(out_ref.dtype)

    return pl.pallas_call(
        body, name=name,
        out_shape=jax.ShapeDtypeStruct((t, d_model), BF16),
        grid=(g,),
        in_specs=[pl.BlockSpec((t, cg), lambda gi: (0, gi)), pl.BlockSpec((None, cg, cg), lambda gi: (gi, 0, 0)),
                  pl.BlockSpec((1, cg), lambda gi: (0, gi)), HBM_SPEC],
        out_specs=pl.BlockSpec((t, cg), lambda gi: (0, off + gi)),
        input_output_aliases={3: 0},
        compiler_params=_params("arbitrary"),
    )(proj_b, w_pool, scale, hcat)


def _pool_bwd(proj_b, w_pool, scale, dhcat, name):
    t, d_model = dhcat.shape
    g, cg, _ = w_pool.shape
    off = (d_model - g * cg) // cg

    def body(u_ref, w_ref, s_ref, dy_ref, du_ref, dw_ref, ds_ref):
        gi = pl.program_id(0)
        u, w, dy = u_ref[...], w_ref[...], dy_ref[...]
        rows = lax.broadcasted_iota(jnp.int32, u.shape, 0)
        pooled, cnt = _pooled(u, gi, rows)
        mixed = _dot(pooled, w, NN)
        ds_ref[...] = jnp.sum(dy * mixed, axis=0, keepdims=True)
        dmixed = dy * s_ref[...]
        dw_ref[...] = _dot(pooled, dmixed, TN).astype(dw_ref.dtype)
        dpooled = _dot(dmixed, w, NT)
        s, levels = dpooled / cnt, []
        for lvl in range(POOL_GROUPS):
            s = s + _shift_up(s, 1 << lvl, rows)
            levels.append(s)
        du_ref[...] = (_pool_select(gi, levels) - dpooled).astype(du_ref.dtype)

    return pl.pallas_call(
        body, name=name,
        out_shape=[jax.ShapeDtypeStruct((t, g * cg), BF16), jax.ShapeDtypeStruct((g, cg, cg), BF16),
                   jax.ShapeDtypeStruct((1, g * cg), F32)],
        grid=(g,),
        in_specs=[pl.BlockSpec((t, cg), lambda gi: (0, gi)), pl.BlockSpec((None, cg, cg), lambda gi: (gi, 0, 0)),
                  pl.BlockSpec((1, cg), lambda gi: (0, gi)), pl.BlockSpec((t, cg), lambda gi: (0, off + gi))],
        out_specs=[pl.BlockSpec((t, cg), lambda gi: (0, gi)), pl.BlockSpec((None, cg, cg), lambda gi: (gi, 0, 0)),
                   pl.BlockSpec((1, cg), lambda gi: (0, gi))],
        compiler_params=_params("arbitrary"),
    )(proj_b, w_pool, scale, dhcat)


def _pad_cols(a, width):
    return jnp.pad(a, ((0, 0), (0, width - a.shape[1])))


def kernel(x, p, w_in, b_gates, w_qk_conv, g_mlstm, w_pool, pool_scale, w_out, g_mix_pre, g_mix_post, w_ff1, w_ff2, g_ff_pre, g_ff_post, w_ple_proj, w_ple_gate, g_ple_gate, g_ple_post, loss_target, m_w_in, m_b_gates, m_w_qk_conv, m_g_mlstm, m_w_pool, m_pool_scale, m_w_out, m_g_mix_pre, m_g_mix_post, m_w_ff1, m_w_ff2, m_g_ff_pre, m_g_ff_post, m_w_ple_proj, m_w_ple_gate, m_g_ple_gate, m_g_ple_post, v_w_in, v_b_gates, v_w_qk_conv, v_g_mlstm, v_w_pool, v_pool_scale, v_w_out, v_g_mix_pre, v_g_mix_post, v_w_ff1, v_w_ff2, v_g_ff_pre, v_g_ff_post, v_w_ple_proj, v_w_ple_gate, v_g_ple_gate, v_g_ple_post):
    weights = dict(w_in=w_in, b_gates=b_gates, w_qk_conv=w_qk_conv, g_mlstm=g_mlstm, w_pool=w_pool,
                   pool_scale=pool_scale, w_out=w_out, g_mix_pre=g_mix_pre, g_mix_post=g_mix_post, w_ff1=w_ff1,
                   w_ff2=w_ff2, g_ff_pre=g_ff_pre, g_ff_post=g_ff_post, w_ple_proj=w_ple_proj,
                   w_ple_gate=w_ple_gate, g_ple_gate=g_ple_gate, g_ple_post=g_ple_post)
    mom1 = dict(w_in=m_w_in, b_gates=m_b_gates, w_qk_conv=m_w_qk_conv, g_mlstm=m_g_mlstm, w_pool=m_w_pool,
                pool_scale=m_pool_scale, w_out=m_w_out, g_mix_pre=m_g_mix_pre, g_mix_post=m_g_mix_post,
                w_ff1=m_w_ff1, w_ff2=m_w_ff2, g_ff_pre=m_g_ff_pre, g_ff_post=m_g_ff_post,
                w_ple_proj=m_w_ple_proj, w_ple_gate=m_w_ple_gate, g_ple_gate=m_g_ple_gate, g_ple_post=m_g_ple_post)
    mom2 = dict(w_in=v_w_in, b_gates=v_b_gates, w_qk_conv=v_w_qk_conv, g_mlstm=v_g_mlstm, w_pool=v_w_pool,
                pool_scale=v_pool_scale, w_out=v_w_out, g_mix_pre=v_g_mix_pre, g_mix_post=v_g_mix_post,
                w_ff1=v_w_ff1, w_ff2=v_w_ff2, g_ff_pre=v_g_ff_pre, g_ff_post=v_g_ff_post,
                w_ple_proj=v_w_ple_proj, w_ple_gate=v_w_ple_gate, g_ple_gate=v_g_ple_gate, g_ple_post=v_g_ple_post)
    order = list(weights)

    t, d = x.shape[1], x.shape[2]
    heads = MLSTM_HEADS
    wm = g_mlstm.shape[1]
    pw = pool_scale.shape[1]
    cg = pw // POOL_GROUPS
    in_cols = 4 * wm + 2 * heads + pw
    in_shard = w_in.shape[2]
    xs, ps, tgt = x[0], p[0, 0], loss_target[0]

    mx, my, mc = lax.axis_index("x"), lax.axis_index("y"), lax.axis_index("c")
    core = jnp.reshape(mc, (1,)).astype(jnp.int32)
    chip = jnp.reshape(2 * mx + my, (1,)).astype(jnp.int32)

    first = [w_in[0].T.astype(BF16), w_qk_conv[0], w_pool[0].reshape(POOL_GROUPS * w_pool.shape[2], cg).astype(BF16)]
    started, token = _gather_start([first], "gather_start_in")
    anchor = token[0, 0]
    later = lambda w: (w[0] + anchor).astype(BF16)
    rest, token = _gather_start([[later(w_out)], [later(w_ff1)], [later(w_ff2)], [later(w_ple_gate), later(w_ple_proj)]],
                                "gather_start_rest")
    started = started + rest
    bias = _pad_cols(b_gates, LANES)
    small = ["b_gates", "g_mlstm", "pool_scale", "g_mix_pre", "g_mix_post", "g_ff_pre", "g_ff_post", "g_ple_gate",
             "g_ple_post"]
    pack = lambda src: _pad_rows16(jnp.concatenate([_pad_cols(src[k] + anchor, d) for k in small], axis=0))
    small_state = [pack(weights), pack(mom1), pack(mom2)]
    in_state = [a[0].T + anchor for a in (w_in, m_w_in, v_w_in)]

    xn = _norm_fwd(xs, g_mix_pre, "norm_mix_pre", deps=[token])
    g_in, g_conv, g_pool = _forward_pair(
        _gather_wait(started[0], [xn] + small_state + in_state, "gather_wait_in"), "gather_pair_in")
    in_t_a = g_in.reshape(in_cols, d)
    in_t_b = jnp.concatenate([in_t_a[4 * wm + 2 * heads:], in_t_a[4 * wm:4 * wm + 2 * heads],
                              jnp.zeros((LANES - 2 * heads, d), BF16)], axis=0)
    conv_w = g_conv.transpose(1, 0, 2).reshape(QK_CONV, 2 * wm)
    pool_w = g_pool.reshape(N_DEV, POOL_GROUPS, cg // N_DEV, cg).transpose(1, 0, 2, 3).reshape(POOL_GROUPS, cg, cg)
    proj_a = _mm_nt("proj_qkvo", xn, in_t_a, n_rows=4 * wm)
    proj_b = _mm_nt("proj_pool_gates", xn, in_t_b)
    qc = _conv_fwd(proj_a, conv_w[:, :wm], 0, "q_conv")
    kc = _conv_fwd(proj_a, conv_w[:, wm:], wm, "k_conv")
    gb, gbt, gcols = _gates_fwd(proj_b, bias, pw // LANES, heads, "gates")
    gbt = gbt[:2 * heads].reshape(2 * heads, 1, t)
    qct, kct = qc.T.astype(BF16), (kc * (wm // heads) ** -0.5).T.astype(BF16)
    fwd_out = _forward_start(_gather_wait(started[1], gb, "gather_wait_out"), "gather_pair_start_out")
    hcat, *mlstm_states = _mlstm_fwd(qc, kc, kct, proj_a, gcols, gbt, g_mlstm, heads, d, "mlstm", fwd_out["token"])
    hcat = _pool_fwd(proj_b, pool_w, pool_scale, hcat, "pool")
    (g_out,) = _forward_wait(fwd_out, hcat, "gather_pair_wait_out")
    out_w = g_out.reshape(d, d)
    mix = _mm_nn("mix_out", hcat, out_w)[0]
    fwd_ff1 = _forward_start(_gather_wait(started[2], mix, "gather_wait_ff1"), "gather_pair_start_ff1")
    h1, hn = _block_fwd(xs, mix, g_mix_post, g_ff_pre, "residual_mix", deps=[fwd_ff1["token"]])
    (g_ff1,) = _forward_wait(fwd_ff1, hn, "gather_pair_wait_ff1")
    relu_a, act = _mm_nn_pieces("ff1", hn, g_ff1, out_dtype=[BF16, BF16],
                                epilogue=lambda acc: (jnp.maximum(acc, 0.0), jnp.square(jnp.maximum(acc, 0.0))))
    (g_ff2,) = _forward_pair(_gather_wait(started[3], act, "gather_wait_ff2"), "gather_pair_ff2")
    ff2_w = g_ff2.reshape(-1, d)
    ff = _mm_nn("ff2", act, ff2_w, tn=256, tk=4 * d)[0]
    fwd_ple = _forward_start(_gather_wait(started[4], ff, "gather_wait_ple"), "gather_pair_start_ple")
    h2, hg = _block_fwd(h1, ff, g_ff_post, g_ple_gate, "residual_ff", deps=[fwd_ple["token"]])
    g_pgate, g_pproj = _forward_wait(fwd_ple, hg, "gather_pair_wait_ple")
    pgate_w = g_pgate.reshape(d, d)
    z = _mm_nn("ple_gate", hg, pgate_w)[0]
    e = _mm_nn_pieces("ple_proj", ps, g_pproj)[0]
    dh3, dz, de, dg_ple_post, loss_part = _loss_head(h2, z, e, tgt, g_ple_post, "loss_head")
    loss = lax.psum(loss_part[0, 0], MESH_AXES)

    out = {}

    def state(k):
        if k == "w_in":
            return in_state
        return [a.reshape(-1, a.shape[-1]) for a in (weights[k], mom1[k], mom2[k])]

    def pair_begin(pieces, tag):
        keys = list(pieces)
        return keys, _pair_start([pieces[k] for k in keys], "reduce_pair_start_" + tag)

    def chip_begin(group, after, tag):
        keys, begun = group
        mine, got = _pair_wait(begun, after, "reduce_pair_wait_" + tag)
        sums = [_pair_sum(p_, g, core, "pair_sum_" + k) for k, p_, g in zip(keys, mine, got)]
        return keys, _chip_start(sums, "reduce_start_" + tag)

    def reduce_finish(group, after, tag):
        keys, begun = group
        sums, got = _chip_wait(begun, after, "reduce_wait_" + tag)
        for k, s, g in zip(keys, sums, got):
            res = _adamw_shard(*state(k), s, g, chip, "adamw_" + k)
            out[k] = [r.T[None] if k == "w_in" else r.reshape(weights[k].shape) for r in res]

    token_of = lambda group: group[1]["token"]
    dhg = _mm_nt("d_ple_gate_in", dz, pgate_w)
    gw_pgate = _mm_tn("gw_ple_gate", hg, dz)
    gw_pproj = _mm_tn("gw_ple_proj", ps, de, pieces=N_DEV)
    dh2, dff, dg_ff_post, dg_ple_gate = _block_bwd(h2, ff, dh3, dhg, g_ff_post, g_ple_gate, "d_residual_ff")
    da = _mm_nt("d_ff2_in", dff, ff2_w, out_dtype=BF16, extra=[relu_a],
                epilogue=lambda acc, r: (acc * (2.0 * r.astype(F32)),))
    gw_ff2 = _mm_tn("gw_ff2", act, dff)
    ff2_keys = ["w_ff2", "w_ple_gate", "w_ple_proj"]
    pair_ff2 = pair_begin({"w_ff2": gw_ff2.reshape(N_DEV, -1, d),
                           "w_ple_gate": gw_pgate.reshape(N_DEV, d // N_DEV, d), "w_ple_proj": gw_pproj}, "ff2")
    dhn = _mm_nt_pieces("d_ff1_in", da, g_ff1, deps=[token_of(pair_ff2)])
    place = jnp.concatenate([core, chip])
    mine_ff2, got_ff2 = _pair_wait(pair_ff2[1], dhn, "reduce_pair_wait_ff2")
    gw_ff1, sums_ff2 = _mm_tn("gw_ff1", hn, da, pieces=N_DEV, side=_side_pair_sum(mine_ff2[0], got_ff2[0], place))
    sums_ple = [_pair_sum(p_, g, core, "pair_sum_" + k) for k, p_, g in zip(ff2_keys[1:], mine_ff2[1:], got_ff2[1:])]
    red_ff2 = (ff2_keys, _chip_start([sums_ff2] + sums_ple, "reduce_start_ff2"))
    pair_ff1 = pair_begin({"w_ff1": gw_ff1}, "ff1")
    dh1, dmix, dg_mix_post, dg_ff_pre = _block_bwd(h1, mix, dh2, dhn, g_mix_post, g_ff_pre, "d_residual_mix",
                                                   deps=[token_of(pair_ff1), token_of(red_ff2)])
    mine_ff1, got_ff1 = _pair_wait(pair_ff1[1], dmix, "reduce_pair_wait_ff1")
    dhcat, sums_ff1 = _mm_nt("d_mix_in", dmix, out_w, tm=t // 2, tn=d // 2,
                             side=_side_pair_sum(mine_ff1[0], got_ff1[0], place))
    red_ff1 = (["w_ff1"], _chip_start([sums_ff1], "reduce_start_ff1"))
    gw_out = _mm_tn("gw_out", hcat, dmix, deps=[token_of(red_ff1)])
    du, gw_pool, dg_pool_scale = _pool_bwd(proj_b, pool_w, pool_scale, dhcat, "d_pool")
    pair_out = pair_begin({
        "w_out": gw_out.reshape(N_DEV, d // N_DEV, d),
        "w_pool": gw_pool.reshape(POOL_GROUPS, N_DEV, cg // N_DEV, cg).transpose(1, 0, 2, 3).reshape(N_DEV, -1, cg)},
        "out")
    dq, dk, dv, do, dcol, drow, dg_mlstm = _mlstm_bwd(qc, kc, qct, kct, proj_a, gcols, gbt, g_mlstm, mlstm_states, dhcat,
                                                      heads, "d_mlstm", token_of(pair_out))
    red_out = chip_begin(pair_out, dq, "out")
    drow = jnp.pad(drow.reshape(heads, t), ((0, LANES - heads), (0, 0)))
    dgates, dbias = _gates_bwd(proj_b, bias, dcol, drow, pw // LANES, heads, "d_gates")
    dq_pre, gw_conv_q = _conv_bwd(proj_a, conv_w[:, :wm], dq, 0, "d_q_conv")
    dk_pre, gw_conv_k = _conv_bwd(proj_a, conv_w[:, wm:], dk, wm, "d_k_conv")
    gw_conv = jnp.concatenate([gw_conv_q, gw_conv_k], axis=1)
    dproj_a = jnp.concatenate([dq_pre, dk_pre, dv, do], axis=1)
    dproj_b = jnp.concatenate([du, dgates], axis=1)
    sums_ff2, got_ff2 = _chip_wait(red_ff2[1], token_of(red_out), "reduce_wait_ff2")
    gw_a, *update = _mm_tn("gw_in_qkvo", dproj_a, xn,
                           side=_side_adamw(*state("w_ff2"), sums_ff2[0], got_ff2[0], place))
    out["w_ff2"] = [r.reshape(weights["w_ff2"].shape) for r in update]
    for k, s, g in zip(ff2_keys[1:], sums_ff2[1:], got_ff2[1:]):
        out[k] = [r.reshape(weights[k].shape) for r in _adamw_shard(*state(k), s, g, chip, "adamw_" + k)]
    gw_b = _mm_tn("gw_in_pool_gates", dproj_b, xn)
    gw_in_t = jnp.concatenate([gw_a, gw_b[pw:pw + 2 * heads], gw_b[:pw]], axis=0)
    pair_in = pair_begin({
        "w_in": gw_in_t.reshape(N_DEV, in_shard, d),
        "w_qk_conv": gw_conv.reshape(QK_CONV, N_DEV, -1).transpose(1, 0, 2).astype(BF16)}, "in")
    reduce_finish(red_out, token_of(pair_in), "out")
    red_in = chip_begin(pair_in, out["w_out"][0], "in")
    sums_ff1, got_ff1 = _chip_wait(red_ff1[1], token_of(red_in), "reduce_wait_ff1")
    dxn_a, *update = _mm_nn("d_proj_qkvo_in", dproj_a, in_t_a, tn=512, tk=4 * wm, deps=[token_of(red_in)],
                            side=_side_adamw(*state("w_ff1"), sums_ff1[0], got_ff1[0], place))
    out["w_ff1"] = [r.reshape(weights["w_ff1"].shape) for r in update]
    dxn = _mm_nn("d_proj_in", dproj_b, in_t_b, extra=[dxn_a], epilogue=lambda acc, prev: (acc + prev,))[0]
    grad_x, dg_mix_pre = _norm_bwd(xs, dxn, dh1, g_mix_pre, "d_norm_mix_pre")

    small_grads = dict(b_gates=dbias[:, :2 * heads], g_mlstm=dg_mlstm, pool_scale=dg_pool_scale, g_mix_pre=dg_mix_pre,
                       g_mix_post=dg_mix_post, g_ff_pre=dg_ff_pre, g_ff_post=dg_ff_post, g_ple_gate=dg_ple_gate,
                       g_ple_post=dg_ple_post)
    small_begun = _everyone_start([pack(small_grads)], "small_grads_start")
    reduce_finish(red_in, small_begun["token"], "in")
    (small_parts,) = _everyone_wait(small_begun, out["w_in"][0], "small_grads_wait")
    res = _adamw_replicated(*small_state, small_parts, "adamw_replicated")
    for i, k in enumerate(small):
        width = weights[k].shape[1]
        out[k] = [r[i:i + 1, :width] for r in res]

    return (loss, grad_x[None], *[out[k][0] for k in order], *[out[k][1] for k in order],
            *[out[k][2] for k in order], *[out[k][3] for k in order])


def _pad_rows16(a):
    return jnp.pad(a, ((0, 16 - a.shape[0]), (0, 0)))
```

```python
import jax
import jax.numpy as jnp
from jax import lax
from jax.experimental import pallas as pl
from jax.experimental.pallas import tpu as pltpu

F32 = jnp.float32
BF16 = jnp.bfloat16
EPS = 1e-6
N_DEV = 8
N_CHIP = 4
LANES = 128
VMEM_LIMIT = 56 * 1024 * 1024
MLSTM_HEADS = 8
POOL_GROUPS = 4
QK_CONV = 4
CHUNK = 128
ROW_TILE = 256
ADAM_LR, ADAM_B1, ADAM_B2, ADAM_EPS, ADAM_WD, ADAM_STEP = 0.001, 0.9, 0.999, 1e-08, 0.01, 10
MESH_AXES = ("x", "y", "c")
MESH = pl.DeviceIdType.MESH
HBM_SPEC = pl.BlockSpec(memory_space=pltpu.HBM)
SEM_SPEC = pl.BlockSpec(memory_space=pltpu.SEMAPHORE)
DATAFLOW = pltpu.SideEffectType.DATAFLOW_SIDE_EFFECTING


def _params(*sem):
    if sem:
        return pltpu.CompilerParams(dimension_semantics=sem, vmem_limit_bytes=VMEM_LIMIT)
    return pltpu.CompilerParams(vmem_limit_bytes=VMEM_LIMIT)


def _tile(n, pref):
    if n <= pref:
        return n
    t = (pref // LANES) * LANES
    while t >= LANES:
        if n % t == 0:
            return t
        t -= LANES
    return n


def _hbm(a):
    return pltpu.with_memory_space_constraint(a, pltpu.HBM)


def _mesh_place():
    x, y, c = lax.axis_index("x"), lax.axis_index("y"), lax.axis_index("c")
    return x, y, c, [(1 - x, y), (x, 1 - y), (1 - x, 1 - y)]


def _chip_copies(srcs, lands, send, recv):
    x, y, c, chips = _mesh_place()
    return [pltpu.make_async_remote_copy(
        src_ref=srcs[w].at[2 * chip[0] + chip[1]], dst_ref=lands[w].at[k],
        send_sem=send.at[3 * w + k], recv_sem=recv.at[3 * w + k],
        device_id=(*chip, c), device_id_type=MESH)
        for k, chip in enumerate(chips) for w in range(len(srcs))]


def _chip_start(sums, name):
    return _split_start(_chip_copies, sums, [lax.empty((3,) + a.shape[1:], a.dtype) for a in sums], 3, name,
                        peers=_same_core_of_other_chips)


def _chip_wait(started, after, name):
    return _split_wait(_chip_copies, started, after, name)


def _pair_start(pieces, name):
    return _split_start(_pair_copies, pieces, [lax.empty((N_CHIP,) + a.shape[1:], a.dtype) for a in pieces], N_CHIP, name,
                        peers=_sibling)


def _pair_wait(started, after, name):
    return _split_wait(_pair_copies, started, after, name)


def _pair_copies(srcs, lands, send, recv):
    x, y, c, _ = _mesh_place()
    return [pltpu.make_async_remote_copy(
        src_ref=srcs[w].at[2 * ch + 1 - c], dst_ref=lands[w].at[ch],
        send_sem=send.at[N_CHIP * w + ch], recv_sem=recv.at[N_CHIP * w + ch],
        device_id=(x, y, 1 - c), device_id_type=MESH)
        for w in range(len(srcs)) for ch in range(N_CHIP)]


OWN_BARRIER_IDS = {name: i for i, name in enumerate([
    "reduce_pair_start_ff2", "reduce_pair_start_ff1", "reduce_pair_start_out", "reduce_pair_start_in",
    "gather_pair_start_out", "gather_pair_start_ff1", "gather_pair_start_ple",
    "reduce_start_ff2", "reduce_start_ff1", "reduce_start_out", "reduce_start_in",
    "gather_start_in", "gather_start_rest", "gather_pair_in", "gather_pair_ff2_lo", "gather_pair_ff2_hi"])}


def _shake_hands(peers):
    barrier = pltpu.get_barrier_semaphore()
    for peer in peers:
        pl.semaphore_signal(barrier, inc=1, device_id=peer, device_id_type=MESH)
    pl.semaphore_wait(barrier, len(peers))


def _sibling():
    x, y, c, _ = _mesh_place()
    return [(x, y, 1 - c)]


def _same_core_of_other_chips():
    x, y, c, chips = _mesh_place()
    return [(*chip, c) for chip in chips]


def _split_start(copies_of, srcs, lands, per_array, name, peers=None):
    ns, nb = len(srcs), len(srcs) + len(lands)
    n_copies = per_array * len(lands)

    def body(*refs):
        if peers is not None:
            _shake_hands(peers())
        for cp in copies_of(refs[:ns], refs[ns:nb], refs[nb], refs[nb + 1]):
            cp.start()
        refs[-1][...] = jnp.zeros_like(refs[-1])

    params = dict(has_side_effects=DATAFLOW)
    if peers is not None:
        params["collective_id"] = OWN_BARRIER_IDS[name]
    res = pl.pallas_call(
        body, name=name,
        out_shape=[pltpu.SemaphoreType.DMA((n_copies,)), pltpu.SemaphoreType.DMA((n_copies,))]
        + [pltpu.HBM(a.shape, a.dtype) for a in list(srcs) + list(lands)] + [jax.ShapeDtypeStruct((8, LANES), F32)],
        in_specs=[HBM_SPEC] * nb,
        out_specs=[SEM_SPEC, SEM_SPEC] + [HBM_SPEC] * nb + [pl.BlockSpec(memory_space=pltpu.VMEM)],
        input_output_aliases={i: 2 + i for i in range(nb)},
        compiler_params=pltpu.CompilerParams(**params),
    )(*[_hbm(a) for a in srcs], *[_hbm(a) for a in lands])
    return dict(send=res[0], recv=res[1], srcs=list(res[2:2 + ns]), lands=list(res[2 + ns:2 + nb]), token=res[-1])


def _split_wait(copies_of, started, after, name):
    ns = len(started["srcs"])
    nb = ns + len(started["lands"])

    def body(*refs):
        for cp in copies_of(refs[:ns], refs[ns:nb], refs[nb], refs[nb + 1]):
            cp.wait_send()
            cp.wait_recv()

    res = pl.pallas_call(
        body, name=name,
        out_shape=[pltpu.HBM(a.shape, a.dtype) for a in started["srcs"] + started["lands"]],
        in_specs=[HBM_SPEC] * nb + [SEM_SPEC, SEM_SPEC, ANY_SPEC],
        out_specs=[HBM_SPEC] * nb,
        input_output_aliases={i: i for i in range(nb)},
        compiler_params=pltpu.CompilerParams(has_side_effects=DATAFLOW),
    )(*started["srcs"], *started["lands"], started["send"], started["recv"], after)
    return list(res[:ns]), list(res[ns:])


def _forward_copies(srcs, lands, send, recv):
    del srcs
    x, y, c, chips = _mesh_place()
    copies = []
    for w in range(len(lands)):
        for j, chip in enumerate(chips):
            rows = lands[w].at[4 * chip[0] + 2 * chip[1] + c]
            copies.append(pltpu.make_async_remote_copy(
                src_ref=rows, dst_ref=rows, send_sem=send.at[3 * w + j], recv_sem=recv.at[3 * w + j],
                device_id=(x, y, 1 - c), device_id_type=MESH))
    return copies


def _forward_start(lands, name):
    return _split_start(_forward_copies, [], lands, 3, name, peers=_sibling)


def _forward_wait(started, after, name):
    return _split_wait(_forward_copies, started, after, name)[1]


def _everyone_copies(srcs, lands, send, recv):
    x, y, c, _ = _mesh_place()
    me = 4 * x + 2 * y + c
    copies = []
    for w in range(len(srcs)):
        for k in range(N_DEV - 1):
            flip = k + 1
            peer = (1 - x if flip & 4 else x, 1 - y if flip & 2 else y, 1 - c if flip & 1 else c)
            copies.append(pltpu.make_async_remote_copy(
                src_ref=srcs[w], dst_ref=lands[w].at[me], send_sem=send.at[7 * w + k], recv_sem=recv.at[7 * w + k],
                device_id=peer, device_id_type=MESH))
    return copies


def _everyone_start(blocks, name):
    me = 4 * lax.axis_index("x") + 2 * lax.axis_index("y") + lax.axis_index("c")
    lands = [lax.dynamic_update_slice(lax.empty((N_DEV,) + a.shape, a.dtype), a[None], (me,) + (0,) * a.ndim)
             for a in blocks]
    return _split_start(_everyone_copies, blocks, lands, N_DEV - 1, name)


def _everyone_wait(started, after, name):
    return _split_wait(_everyone_copies, started, after, name)[1]


def _gather_copies(srcs, lands, send, recv):
    x, y, c, chips = _mesh_place()
    me = 4 * x + 2 * y + c
    targets = [(x, y, 1 - c)] + [(*chip, c) for chip in chips]
    return [pltpu.make_async_remote_copy(
        src_ref=srcs[w], dst_ref=lands[w].at[me], send_sem=send.at[4 * w + k], recv_sem=recv.at[4 * w + k],
        device_id=to, device_id_type=MESH)
        for w in range(len(srcs)) for k, to in enumerate(targets)]


def _gather_start(groups, name):
    sizes = [len(g) for g in groups]
    flat = [a for g in groups for a in g]
    n, ng = len(flat), len(groups)
    me = 4 * lax.axis_index("x") + 2 * lax.axis_index("y") + lax.axis_index("c")
    lands = [lax.dynamic_update_slice(lax.empty((N_DEV,) + a.shape, a.dtype), a[None], (me,) + (0,) * a.ndim)
             for a in flat]

    def body(*refs):
        _shake_hands(_sibling() + _same_core_of_other_chips())
        srcs, zones = refs[:n], refs[n:2 * n]
        sems = refs[2 * n:2 * n + 2 * ng]
        token = refs[-1]
        lo = 0
        for gi, size in enumerate(sizes):
            for cp in _gather_copies(srcs[lo:lo + size], zones[lo:lo + size], sems[2 * gi], sems[2 * gi + 1]):
                cp.start()
            lo += size
        token[...] = jnp.zeros_like(token)

    sem_shapes = []
    for size in sizes:
        sem_shapes += [pltpu.SemaphoreType.DMA((4 * size,))] * 2
    res = pl.pallas_call(
        body, name=name,
        out_shape=sem_shapes + [pltpu.HBM(a.shape, a.dtype) for a in flat + lands] + [jax.ShapeDtypeStruct((8, LANES), F32)],
        in_specs=[HBM_SPEC] * (2 * n),
        out_specs=[SEM_SPEC] * (2 * ng) + [HBM_SPEC] * (2 * n) + [pl.BlockSpec(memory_space=pltpu.VMEM)],
        input_output_aliases={i: 2 * ng + i for i in range(2 * n)},
        compiler_params=pltpu.CompilerParams(has_side_effects=DATAFLOW, collective_id=OWN_BARRIER_IDS[name]),
    )(*[_hbm(a) for a in flat], *[_hbm(a) for a in lands])
    out, lo = [], 0
    for gi, size in enumerate(sizes):
        out.append(dict(send=res[2 * gi], recv=res[2 * gi + 1],
                        srcs=list(res[2 * ng + lo:2 * ng + lo + size]),
                        lands=list(res[2 * ng + n + lo:2 * ng + n + lo + size])))
        lo += size
    return out, res[-1]


def _gather_wait(started, after, name):
    n = len(started["srcs"])
    after = list(after) if isinstance(after, (list, tuple)) else [after]

    def body(*refs):
        srcs, zones = refs[:n], refs[n:2 * n]
        send, recv = refs[2 * n], refs[2 * n + 1]
        for cp in _gather_copies(srcs, zones, send, recv):
            cp.wait_send()
            cp.wait_recv()

    res = pl.pallas_call(
        body, name=name,
        out_shape=[pltpu.HBM(a.shape, a.dtype) for a in started["srcs"] + started["lands"]],
        in_specs=[HBM_SPEC] * (2 * n) + [SEM_SPEC, SEM_SPEC] + [ANY_SPEC] * len(after),
        out_specs=[HBM_SPEC] * (2 * n),
        input_output_aliases={i: i for i in range(2 * n)},
        compiler_params=pltpu.CompilerParams(has_side_effects=DATAFLOW),
    )(*started["srcs"], *started["lands"], started["send"], started["recv"], *after)
    return list(res[n:])


def _forward_pair(lands, name):
    n = len(lands)

    def body(*refs):
        _shake_hands(_sibling())
        zones = refs[:n]
        send, recv = refs[2 * n:]
        x, y, c, chips = _mesh_place()

        def copy(w, j, core):
            rows = zones[w].at[4 * chips[j][0] + 2 * chips[j][1] + core]
            return pltpu.make_async_remote_copy(
                src_ref=rows, dst_ref=rows, send_sem=send.at[3 * w + j], recv_sem=recv.at[3 * w + j],
                device_id=(x, y, 1 - c), device_id_type=MESH)

        sends = [copy(w, j, c) for w in range(n) for j in range(3)]
        for cp in sends:
            cp.start()
        for w in range(n):
            for j in range(3):
                copy(w, j, 1 - c).wait_recv()
        for cp in sends:
            cp.wait_send()

    return list(pl.pallas_call(
        body, name=name,
        out_shape=[jax.ShapeDtypeStruct(a.shape, a.dtype) for a in lands],
        in_specs=[HBM_SPEC] * n, out_specs=[HBM_SPEC] * n,
        input_output_aliases={i: i for i in range(n)},
        scratch_shapes=[pltpu.SemaphoreType.DMA((3 * n,)), pltpu.SemaphoreType.DMA((3 * n,))],
        compiler_params=pltpu.CompilerParams(collective_id=OWN_BARRIER_IDS[name]),
    )(*lands))


def _pair_sum(pieces, got, core, name):
    _, rows, cols = pieces.shape
    tr, tc = _tile_2d(rows, cols)

    def body(core_ref, mine_ref, got_ref, out_ref):
        del core_ref
        out_ref[...] = (mine_ref[...].astype(F32) + got_ref[...].astype(F32)).astype(out_ref.dtype)

    return pl.pallas_call(
        body, name=name,
        out_shape=jax.ShapeDtypeStruct((N_CHIP, rows, cols), pieces.dtype),
        grid_spec=pltpu.PrefetchScalarGridSpec(
            num_scalar_prefetch=1, grid=(N_CHIP, rows // tr, cols // tc),
            in_specs=[pl.BlockSpec((None, tr, tc), lambda ch, i, j, core: (2 * ch + core[0], i, j)),
                      pl.BlockSpec((None, tr, tc), lambda ch, i, j, core: (ch, i, j))],
            out_specs=pl.BlockSpec((None, tr, tc), lambda ch, i, j, core: (ch, i, j))),
        compiler_params=_params("parallel", "parallel", "parallel"),
    )(core, pieces, got)


def _tile_2d(rows, cols, budget=2 * 1024 * 1024):
    want = max(16, budget // (4 * cols))
    if rows <= want:
        return rows, cols
    t = (want // 16) * 16
    while t >= 16:
        if rows % t == 0:
            return t, cols
        t -= 16
    return rows, _tile(cols, max(LANES, budget // (4 * rows)))


def _adamw_math(w, g, m, v):
    m = ADAM_B1 * m + (1.0 - ADAM_B1) * g
    v = ADAM_B2 * v + (1.0 - ADAM_B2) * (g * g)
    m_hat = m / (1.0 - ADAM_B1 ** ADAM_STEP)
    v_hat = v / (1.0 - ADAM_B2 ** ADAM_STEP)
    delta = -ADAM_LR * (m_hat / (jnp.sqrt(v_hat) + ADAM_EPS) + ADAM_WD * w)
    return delta, m, v


def _adamw_shard(w, m, v, sums, got, chip, name):
    rows, cols = w.shape
    tr, tc = _tile_2d(rows, cols, budget=1024 * 1024)

    def body(chip_ref, w_ref, m_ref, v_ref, own_ref, got_ref, g_out, d_out, m_out, v_out):
        del chip_ref
        g = own_ref[...].astype(F32)
        for k in range(3):
            g = g + got_ref[k].astype(F32)
        delta, m_new, v_new = _adamw_math(w_ref[...], g, m_ref[...], v_ref[...])
        g_out[...] = g
        d_out[...] = delta
        m_out[...] = m_new
        v_out[...] = v_new

    blk = pl.BlockSpec((tr, tc), lambda i, j, chip: (i, j))
    return pl.pallas_call(
        body, name=name,
        out_shape=[jax.ShapeDtypeStruct((rows, cols), F32)] * 4,
        grid_spec=pltpu.PrefetchScalarGridSpec(
            num_scalar_prefetch=1, grid=(rows // tr, cols // tc),
            in_specs=[blk, blk, blk,
                      pl.BlockSpec((None, tr, tc), lambda i, j, chip: (chip[0], i, j)),
                      pl.BlockSpec((3, tr, tc), lambda i, j, chip: (0, i, j))],
            out_specs=[blk] * 4),
        compiler_params=_params("parallel", "parallel"),
    )(chip, w, m, v, sums, got)


def _adamw_replicated(w, m, v, parts, name):
    rows, cols = w.shape

    def body(w_ref, m_ref, v_ref, parts_ref, g_out, d_out, m_out, v_out):
        g = parts_ref[0]
        for d in range(1, N_DEV):
            g = g + parts_ref[d]
        delta, m_new, v_new = _adamw_math(w_ref[...], g, m_ref[...], v_ref[...])
        g_out[...] = g
        d_out[...] = delta
        m_out[...] = m_new
        v_out[...] = v_new

    return pl.pallas_call(
        body, name=name,
        out_shape=[jax.ShapeDtypeStruct((rows, cols), F32)] * 4,
        compiler_params=_params(),
    )(w, m, v, parts)


NN = ((1,), (0,))
NT = ((1,), (1,))
TN = ((0,), (0,))


ANY_SPEC = pl.BlockSpec(memory_space=pl.ANY)


def _mm(name, a, b, *, dims, grid, a_spec, b_spec, outs, extra=(), extra_specs=(), epilogue=None, acc_shape=None,
        deps=(), b_pieces=0, side=None, b2=None, into=None):
    nk = grid[2]
    n_extra, n_out = len(extra), len(outs)
    side_ins = side["ins"] if side else []
    side_outs = side["outs"] if side else []
    first_side = 2 + n_extra + len(deps)
    tail = ([b2] if b2 is not None else []) + ([into] if into is not None else [])
    first_out = first_side + len(side_ins) + len(tail)
    assert not side or nk == 1

    assert not side or (epilogue is None and n_out == 1 and not b_pieces and not tail)
    SIDE_SLICES = 4

    def body(*refs):
        dot = lambda lhs, rhs: lax.dot_general(lhs.astype(BF16), rhs.astype(BF16), (dims, ((), ())),
                                               preferred_element_type=F32)
        if side:
            refs = refs[1:]
            a_ref, b_ref, out_ref = refs[0], refs[1], refs[first_out]
            ins, sides = refs[first_side:first_out], refs[first_out + 1:first_out + 1 + len(side_outs)]
            rows = out_ref.shape[0] // SIDE_SLICES
            srows = sides[0].shape[-2] // SIDE_SLICES
            for part in range(SIDE_SLICES):
                rs = slice(part * srows, (part + 1) * srows)
                for val, o in zip(side["fn"](*[r[..., rs, :] for r in ins]), sides):
                    o[rs, :] = val.astype(o.dtype)
                ms = slice(part * rows, (part + 1) * rows)
                lhs = a_ref[:, ms] if dims == TN else a_ref[ms, :]
                out_ref[ms, :] = dot(lhs, b_ref[...]).astype(out_ref.dtype)
            return
        a_ref, b_ref = refs[0], refs[1]
        ex = refs[2:2 + n_extra]
        out_refs = refs[first_out:first_out + n_out]
        if b_pieces:
            ks = b_ref.shape[-1]
            part = dot(a_ref[:, 0:ks], b_ref[0])
            for piece in range(1, b_pieces):
                part = part + dot(a_ref[:, piece * ks:(piece + 1) * ks], b_ref[piece])
        elif b2 is not None:
            ks = b_ref.shape[-1]
            part = dot(a_ref[:, 0:ks], b_ref[...]) + dot(a_ref[:, ks:2 * ks], refs[first_side][...])
        else:
            part = dot(a_ref[...], b_ref[...])

        def finish(acc):
            res = epilogue(acc, *[e[...] for e in ex]) if epilogue else (acc,)
            for val, o in zip(res, out_refs):
                o[...] = val.astype(o.dtype)

        if nk == 1:
            finish(part)
        else:
            acc_ref = refs[-1]
            k = pl.program_id(2)

            @pl.when(k == 0)
            def _():
                acc_ref[...] = part

            @pl.when(k > 0)
            def _():
                acc_ref[...] += part

            @pl.when(k == nk - 1)
            def _():
                finish(acc_ref[...])

    scratch = [pltpu.VMEM(acc_shape, F32)] if nk > 1 else []
    out_shape = [jax.ShapeDtypeStruct(s, d) for s, d, _ in outs]
    in_specs = [a_spec, b_spec, *extra_specs] + [ANY_SPEC] * len(deps)
    out_specs = [sp for _, _, sp in outs]
    if not side:
        in_specs += ([b_spec] if b2 is not None else []) + ([ANY_SPEC] if into is not None else [])
        return list(pl.pallas_call(
            body, name=name, out_shape=out_shape, grid=grid, in_specs=in_specs, out_specs=out_specs,
            scratch_shapes=scratch, compiler_params=_params("parallel", "parallel", "arbitrary"),
            input_output_aliases={len(in_specs) - 1: 0} if into is not None else {},
        )(a, b, *extra, *deps, *tail))

    def with_scalars(spec):
        if spec.index_map is None:
            return spec
        return pl.BlockSpec(spec.block_shape, lambda i, j, k, s, f=spec.index_map: f(i, j, k))

    def by_step(block, index_map):
        return pl.BlockSpec(block, lambda i, j, k, s: index_map((i * grid[1] + j) * grid[2] + k, s))

    return list(pl.pallas_call(
        body, name=name,
        out_shape=out_shape + [jax.ShapeDtypeStruct(s, d) for s, d, _, _ in side_outs],
        grid_spec=pltpu.PrefetchScalarGridSpec(
            num_scalar_prefetch=1, grid=grid,
            in_specs=[with_scalars(sp) for sp in in_specs] + [by_step(blk, im) for _, blk, im in side_ins],
            out_specs=[with_scalars(sp) for sp in out_specs] + [by_step(blk, im) for _, _, blk, im in side_outs],
            scratch_shapes=scratch),
        compiler_params=_params("arbitrary", "arbitrary", "arbitrary"),
    )(side["scalars"], a, b, *extra, *deps, *[arr for arr, _, _ in side_ins]))


def _side_pair_sum(pieces, got, place):
    _, rows, cols = pieces.shape

    def build(steps):
        per_chip = steps // N_CHIP
        tr = rows // per_chip
        assert per_chip * N_CHIP == steps and tr * per_chip == rows and tr % 16 == 0
        blk = (None, tr, cols)
        here = lambda st, s: (st // per_chip, st % per_chip, 0)
        return dict(scalars=place,
                    ins=[(pieces, blk, lambda st, s: (2 * (st // per_chip) + s[0], st % per_chip, 0)), (got, blk, here)],
                    outs=[((N_CHIP, rows, cols), pieces.dtype, blk, here)],
                    fn=lambda mine, other: (mine.astype(F32) + other.astype(F32),))
    return build


def _side_adamw(w, m, v, sums, got, place):
    rows, cols = w.shape

    def build(steps):
        tr = rows // steps
        assert tr * steps == rows and tr % 16 == 0
        blk, here = (tr, cols), (lambda st, s: (st, 0))

        def fn(w, m, v, own, got):
            g = own.astype(F32)
            for k in range(3):
                g = g + got[k].astype(F32)
            delta, m_new, v_new = _adamw_math(w, g, m, v)
            return g, delta, m_new, v_new

        return dict(scalars=place,
                    ins=[(w, blk, here), (m, blk, here), (v, blk, here),
                         (sums, (None, tr, cols), lambda st, s: (s[1], st, 0)),
                         (got, (3, tr, cols), lambda st, s: (0, st, 0))],
                    outs=[((rows, cols), F32, blk, here)] * 4, fn=fn)
    return build


def _mm_nn(name, a, b, out_dtype=F32, tm=1024, tn=1024, tk=2048, epilogue=None, n_out=1, extra=(), deps=(), side=None,
           out_cols=None, out_at=0, into=None):
    m, kd = a.shape
    n = b.shape[1]
    tm, tn, tk = _tile(m, tm), _tile(n, tn), _tile(kd, tk)
    dts = out_dtype if isinstance(out_dtype, (list, tuple)) else [out_dtype] * n_out
    shift = out_at // tn
    o_spec = pl.BlockSpec((tm, tn), lambda i, j, k: (i, j + shift))
    grid = (m // tm, n // tn, kd // tk)
    return _mm(name, a, b, dims=NN, grid=grid,
               a_spec=pl.BlockSpec((tm, tk), lambda i, j, k: (i, k)),
               b_spec=pl.BlockSpec((tk, tn), lambda i, j, k: (k, j)),
               outs=[((m, out_cols or n), dt, o_spec) for dt in dts], extra=extra, extra_specs=[o_spec] * len(extra),
               epilogue=epilogue, acc_shape=(tm, tn), deps=deps, side=side and side(grid[0] * grid[1] * grid[2]),
               into=into)


def _mm_nn_pieces(name, a, b, out_dtype=F32, tm=1024, tk=2048, epilogue=None, n_out=1):
    m, kd = a.shape
    npc, _, ns = b.shape
    tm, tk = _tile(m, tm), _tile(kd, tk)
    tn = _tile(ns, 1024)
    r = ns // tn
    dts = out_dtype if isinstance(out_dtype, (list, tuple)) else [out_dtype] * n_out
    o_spec = pl.BlockSpec((tm, tn), lambda i, j, k: (i, j))
    return _mm(name, a, b, dims=NN, grid=(m // tm, npc * r, kd // tk),
               a_spec=pl.BlockSpec((tm, tk), lambda i, j, k: (i, k)),
               b_spec=pl.BlockSpec((None, tk, tn), lambda i, j, k: (j // r, k, j % r)),
               outs=[((m, npc * ns), dt, o_spec) for dt in dts], epilogue=epilogue, acc_shape=(tm, tn))


def _mm_nt(name, a, b, out_dtype=F32, tm=1024, tn=1024, tk=2048, epilogue=None, extra=(), n_rows=None, deps=(),
           side=None, b2=None):
    m, kd = a.shape
    n = n_rows or b.shape[0]
    tm, tn, tk = _tile(m, tm), _tile(n, tn), _tile(kd, tk)
    o_spec = pl.BlockSpec((tm, tn), lambda i, j, k: (i, j))
    if b2 is not None:
        tk = kd
        b_spec = pl.BlockSpec((tn, kd // 2), lambda i, j, k: (j, 0))
    else:
        b_spec = pl.BlockSpec((tn, tk), lambda i, j, k: (j, k))
    grid = (m // tm, n // tn, kd // tk)
    res = _mm(name, a, b, dims=NT, grid=grid,
              a_spec=pl.BlockSpec((tm, tk), lambda i, j, k: (i, k)), b_spec=b_spec,
              outs=[((m, n), out_dtype, o_spec)], extra=extra, extra_specs=[o_spec] * len(extra),
              epilogue=epilogue, acc_shape=(tm, tn), deps=deps, side=side and side(grid[0] * grid[1] * grid[2]), b2=b2)
    return res if side else res[0]


def _mm_nt_pieces(name, a, b, out_dtype=F32, tm=1024, tn=256, deps=()):
    m = a.shape[0]
    npc, n, ks = b.shape
    tm, tn = _tile(m, tm), _tile(n, tn)
    return _mm(name, a, b, dims=NT, grid=(m // tm, n // tn, 1),
               a_spec=pl.BlockSpec((tm, npc * ks), lambda i, j, k: (i, 0)),
               b_spec=pl.BlockSpec((npc, tn, ks), lambda i, j, k: (0, j, 0)),
               outs=[((m, n), out_dtype, pl.BlockSpec((tm, tn), lambda i, j, k: (i, j)))],
               deps=deps, b_pieces=npc)[0]


def _mm_tn(name, a, b, out_dtype=BF16, tm=1024, tn=1024, pieces=0, deps=(), side=None):
    t, m = a.shape
    n = b.shape[1]
    tm = _tile(m, tm)
    if pieces:
        ns = n // pieces
        tn = _tile(ns, tn)
        r = ns // tn
        out = ((pieces, m, ns), out_dtype, pl.BlockSpec((None, tm, tn), lambda i, j, k: (j // r, i, j % r)))
    else:
        tn = _tile(n, tn)
        out = ((m, n), out_dtype, pl.BlockSpec((tm, tn), lambda i, j, k: (i, j)))
    grid = (m // tm, n // tn, 1)
    res = _mm(name, a, b, dims=TN, grid=grid,
              a_spec=pl.BlockSpec((t, tm), lambda i, j, k: (0, i)),
              b_spec=pl.BlockSpec((t, tn), lambda i, j, k: (0, j)),
              outs=[out], deps=deps, side=side and side(grid[0] * grid[1]))
    return res if side else res[0]


def _rms_f(x, g):
    r = lax.rsqrt(jnp.mean(x * x, axis=-1, keepdims=True) + EPS)
    xh = x * r
    return xh * g, xh, r


def _rms_b(dy, g, xh, r):
    dxh = dy * g
    dx = r * (dxh - xh * jnp.mean(dxh * xh, axis=-1, keepdims=True))
    dg = jnp.sum(dy * xh, axis=0, keepdims=True)
    return dx, dg


def _rowwise(name, fn, row_ins, vec_ins, row_outs, vec_outs, deps=()):
    t = row_ins[0].shape[0]
    tr = min(ROW_TILE, t)
    nr, nv, no = len(row_ins), len(vec_ins), len(row_outs)
    first_out = nr + nv + len(deps)

    def body(*refs):
        ri, vi = refs[:nr], refs[nr:nr + nv]
        ro, vo = refs[first_out:first_out + no], refs[first_out + no:]
        routs, vouts = fn(*[r[...] for r in ri], *[v[...] for v in vi])
        for o, val in zip(ro, routs):
            o[...] = val.astype(o.dtype)
        i = pl.program_id(0)
        for o, val in zip(vo, vouts):
            @pl.when(i == 0)
            def _():
                o[...] = val

            @pl.when(i > 0)
            def _():
                o[...] += val

    res = pl.pallas_call(
        body, name=name,
        out_shape=[jax.ShapeDtypeStruct((t, w), d) for w, d in row_outs]
        + [jax.ShapeDtypeStruct((1, w), F32) for w in vec_outs],
        grid=(t // tr,),
        in_specs=[pl.BlockSpec((tr, a.shape[1]), lambda i: (i, 0)) for a in row_ins]
        + [pl.BlockSpec((1, a.shape[1]), lambda i: (0, 0)) for a in vec_ins] + [ANY_SPEC] * len(deps),
        out_specs=[pl.BlockSpec((tr, w), lambda i: (i, 0)) for w, _ in row_outs]
        + [pl.BlockSpec((1, w), lambda i: (0, 0)) for w in vec_outs],
        compiler_params=_params("arbitrary"),
    )(*row_ins, *vec_ins, *deps)
    return list(res)


def _norm_fwd(x, g, name, deps=()):
    d = x.shape[1]
    return _rowwise(name, lambda x, g: ((_rms_f(x, g)[0],), ()), [x], [g], [(d, BF16)], [], deps=deps)[0]


def _norm_bwd(x, dxn, dres, g, name):
    d = x.shape[1]

    def fn(x, dxn, dres, g):
        _, xh, r = _rms_f(x, g)
        dx, dg = _rms_b(dxn, g, xh, r)
        return (dres + dx,), (dg,)

    return _rowwise(name, fn, [x, dxn, dres], [g], [(d, F32)], [d])


def _block_fwd(h_prev, y, g_y, g_n, name, deps=()):
    d = h_prev.shape[1]

    def fn(h_prev, y, g_y, g_n):
        h = h_prev + _rms_f(y, g_y)[0]
        return (h, _rms_f(h, g_n)[0]), ()

    return _rowwise(name, fn, [h_prev, y], [g_y, g_n], [(d, F32), (d, BF16)], [], deps=deps)


def _block_bwd(h, y, d_up, d_n, g_y, g_n, name, deps=()):
    d = h.shape[1]

    def fn(h, y, d_up, d_n, g_y, g_n):
        _, hh, hr = _rms_f(h, g_n)
        dh_n, dg_n = _rms_b(d_n, g_n, hh, hr)
        dh = d_up + dh_n
        _, yh, yr = _rms_f(y, g_y)
        dy, dg_y = _rms_b(dh, g_y, yh, yr)
        return (dh, dy), (dg_y, dg_n)

    return _rowwise(name, fn, [h, y, d_up, d_n], [g_y, g_n], [(d, F32), (d, BF16)], [d, d], deps=deps)


def _loss_head(h2, z, e, target, g, name):
    d = h2.shape[1]

    def fn(h2, z, e, target, g):
        gate = jax.nn.sigmoid(z)
        y, xh, r = _rms_f(e * gate, g)
        diff = h2 + y - target
        loss = 0.5 * jnp.sum(jnp.mean(diff * diff, axis=-1, keepdims=True), axis=0, keepdims=True)
        dh3 = diff / d
        deg, dg = _rms_b(dh3, g, xh, r)
        dz = deg * e * gate * (1.0 - gate)
        de = deg * gate
        return (dh3, dz, de), (dg, jnp.broadcast_to(loss, (1, LANES)))

    return _rowwise(name, fn, [h2, z, e, target], [g], [(d, F32), (d, BF16), (d, BF16)], [d, LANES])


def _shift_down(x, s, rows):
    return jnp.where(rows >= s, pltpu.roll(x, s, axis=0), 0.0)


def _shift_up(x, s, rows):
    t = x.shape[0]
    return jnp.where(rows < t - s, pltpu.roll(x, t - s, axis=0), 0.0)


def _conv_pre(x, w, rows):
    pre = x * w[QK_CONV - 1:QK_CONV, :]
    for s in range(1, QK_CONV):
        pre = pre + _shift_down(x, s, rows) * w[QK_CONV - 1 - s:QK_CONV - s, :]
    return pre


def _conv_fwd(proj_a, w, start, name):
    t = proj_a.shape[0]
    width = w.shape[1]
    tc = _tile(width, 256)
    off = start // tc

    def body(x_ref, w_ref, y_ref):
        x = x_ref[...]
        rows = lax.broadcasted_iota(jnp.int32, x.shape, 0)
        pre = _conv_pre(x, w_ref[...], rows)
        y_ref[...] = pre * jax.nn.sigmoid(pre)

    return pl.pallas_call(
        body, name=name,
        out_shape=jax.ShapeDtypeStruct((t, width), F32),
        grid=(width // tc,),
        in_specs=[pl.BlockSpec((t, tc), lambda j: (0, off + j)), pl.BlockSpec((QK_CONV, tc), lambda j: (0, j))],
        out_specs=pl.BlockSpec((t, tc), lambda j: (0, j)),
        compiler_params=_params("parallel"),
    )(proj_a, w)


def _conv_bwd(proj_a, w, dy, start, name):
    t = proj_a.shape[0]
    width = w.shape[1]
    tc = _tile(width, 256)
    off = start // tc

    def body(x_ref, w_ref, dy_ref, dx_ref, dw_ref):
        x, w = x_ref[...], w_ref[...]
        rows = lax.broadcasted_iota(jnp.int32, x.shape, 0)
        pre = _conv_pre(x, w, rows)
        sig = jax.nn.sigmoid(pre)
        dpre = dy_ref[...] * (sig * (1.0 + pre * (1.0 - sig)))
        dx = dpre * w[QK_CONV - 1:QK_CONV, :]
        dws = [jnp.sum(dpre * x, axis=0, keepdims=True)]
        for s in range(1, QK_CONV):
            dx = dx + _shift_up(dpre, s, rows) * w[QK_CONV - 1 - s:QK_CONV - s, :]
            dws.append(jnp.sum(dpre * _shift_down(x, s, rows), axis=0, keepdims=True))
        dx_ref[...] = dx.astype(dx_ref.dtype)
        for s in range(QK_CONV):
            dw_ref[QK_CONV - 1 - s:QK_CONV - s, :] = dws[s]

    return pl.pallas_call(
        body, name=name,
        out_shape=[jax.ShapeDtypeStruct((t, width), BF16), jax.ShapeDtypeStruct((QK_CONV, width), F32)],
        grid=(width // tc,),
        in_specs=[pl.BlockSpec((t, tc), lambda j: (0, off + j)), pl.BlockSpec((QK_CONV, tc), lambda j: (0, j)),
                  pl.BlockSpec((t, tc), lambda j: (0, j))],
        out_specs=[pl.BlockSpec((t, tc), lambda j: (0, j)), pl.BlockSpec((QK_CONV, tc), lambda j: (0, j))],
        compiler_params=_params("parallel"),
    )(proj_a, w, dy)


def _dot(a, b, dims):
    return lax.dot_general(a.astype(BF16), b.astype(BF16), (dims, ((), ())), preferred_element_type=F32)


def _dot_split(ones, x):
    hi = x.astype(BF16)
    rest = x - hi.astype(F32)
    mid = rest.astype(BF16)
    lo = (rest - mid.astype(F32)).astype(BF16)
    dot = lambda part: lax.dot_general(ones, part, (NN, ((), ())), preferred_element_type=F32)
    return dot(hi) + dot(mid) + dot(lo)


def _chunk_iotas():
    rows = lax.broadcasted_iota(jnp.int32, (CHUNK, CHUNK), 0)
    cols = lax.broadcasted_iota(jnp.int32, (CHUNK, CHUNK), 1)
    return rows, cols


def _gates_fwd(proj_b, bias, col_block, heads, name):
    t = proj_b.shape[0]

    def body(x_ref, b_ref, gb_ref, gbt_ref, cols_ref):
        rows, cols = _chunk_iotas()
        tri = (cols <= rows).astype(BF16)
        for ci in range(t // CHUNK):
            sl = slice(ci * CHUNK, (ci + 1) * CHUNK)
            pre = x_ref[sl, :] + b_ref[...]
            log_f = jnp.minimum(pre, 0.0) - jnp.log(1.0 + jnp.exp(-jnp.abs(pre)))
            log_f = jnp.where((cols >= heads) & (cols < 2 * heads), log_f, 0.0)
            tile = jnp.where(cols < heads, pre, _dot_split(tri, log_f))
            gb_ref[sl, :] = tile
            gbt_ref[:, sl] = tile.T
            for h in range(2 * heads):
                cols_ref[h, sl, :] = jnp.broadcast_to(tile[:, h:h + 1], (CHUNK, LANES))

    return pl.pallas_call(
        body, name=name,
        out_shape=[jax.ShapeDtypeStruct((t, LANES), F32), jax.ShapeDtypeStruct((LANES, t), F32),
                   jax.ShapeDtypeStruct((2 * heads, t, LANES), F32)],
        grid=(1,),
        in_specs=[pl.BlockSpec((t, LANES), lambda i: (0, col_block)), pl.BlockSpec((1, LANES), lambda i: (0, 0))],
        out_specs=[pl.BlockSpec((t, LANES), lambda i: (0, 0)), pl.BlockSpec((LANES, t), lambda i: (0, 0)),
                   pl.BlockSpec((2 * heads, t, LANES), lambda i: (0, 0, 0))],
        compiler_params=_params("arbitrary"),
    )(proj_b, bias)


def _gates_bwd(proj_b, bias, dcol, drow, col_block, heads, name):
    t = proj_b.shape[0]

    def body(x_ref, b_ref, dc_ref, dr_ref, dx_ref, db_ref):
        rows, cols = _chunk_iotas()
        later = (cols >= rows).astype(BF16)
        total = jnp.zeros((1, LANES), F32)
        for ci in range(t // CHUNK):
            sl = slice(ci * CHUNK, (ci + 1) * CHUNK)
            by_row = dr_ref[:, sl].T
            d = dc_ref[sl, :] + jnp.where(cols < heads, by_row,
                                          jnp.where(cols < 2 * heads, -pltpu.roll(by_row, heads, axis=1), 0.0))
            d_log_f = _dot_split(later, jnp.where(cols >= heads, d, 0.0))
            pre = x_ref[sl, :] + b_ref[...]
            dx = jnp.where(cols < heads, d, jnp.where(cols < 2 * heads, d_log_f * jax.nn.sigmoid(-pre), 0.0))
            dx_ref[sl, :] = dx.astype(dx_ref.dtype)
            total = total + jnp.sum(dx, axis=0, keepdims=True)
        db_ref[...] = total

    return pl.pallas_call(
        body, name=name,
        out_shape=[jax.ShapeDtypeStruct((t, LANES), BF16), jax.ShapeDtypeStruct((1, LANES), F32)],
        grid=(1,),
        in_specs=[pl.BlockSpec((t, LANES), lambda i: (0, col_block)), pl.BlockSpec((1, LANES), lambda i: (0, 0)),
                  pl.BlockSpec((t, LANES), lambda i: (0, 0)), pl.BlockSpec((LANES, t), lambda i: (0, 0))],
        out_specs=[pl.BlockSpec((t, LANES), lambda i: (0, 0)), pl.BlockSpec((1, LANES), lambda i: (0, 0))],
        compiler_params=_params("arbitrary"),
    )(proj_b, bias, dcol, drow)


def _gate_views(li_ref, b_ref, gbt_ref, s, head, heads, sl):
    li_c, b_c = li_ref[s, sl, :], b_ref[s, sl, :]
    li_r = jnp.broadcast_to(gbt_ref[head, :, sl], (CHUNK, CHUNK))
    b_r = jnp.broadcast_to(gbt_ref[heads + head, :, sl], (CHUNK, CHUNK))
    return li_c, b_c, li_r, b_r


def _chunk_state(kt, k, v, li_c, b_c, ct, nrow, m_prev):
    b_tot = b_c[CHUNK - 1:CHUNK, 0:1]
    a = b_tot - b_c[:, 0:1] + li_c[:, 0:1]
    m_new = jnp.maximum(b_tot + m_prev, jnp.max(a, axis=0, keepdims=True))
    dec = jnp.exp(b_tot + m_prev - m_new)
    w = jnp.exp(a - m_new)
    wv = w * v
    ct_new = dec * ct + _dot(kt, wv, NN)
    n_new = dec * nrow + jnp.sum(w * k, axis=0, keepdims=True)
    return ct_new, n_new, m_new, dec, w, wv


def _chunk_scores(q, k, b_c, li_r, b_r, m_prev):
    rows, cols = _chunk_iotas()
    dlog = jnp.where(cols <= rows, b_c - b_r + li_r, -jnp.inf)
    inter = b_c[:, 0:1] + m_prev
    mt = jnp.maximum(inter, jnp.max(dlog, axis=-1, keepdims=True))
    dw = jnp.exp(dlog - mt)
    iw = jnp.exp(inter - mt)
    a = _dot(q, k, NT)
    return dw, iw, mt, a, a * dw


def _to_row(column):
    rows, cols = _chunk_iotas()
    return jnp.sum(jnp.where(rows == cols, jnp.broadcast_to(column, (CHUNK, CHUNK)), 0.0), axis=0, keepdims=True)


def _head_out(h, o, g):
    cell, hh, r = _rms_f(h, g)
    sig = jax.nn.sigmoid(o)
    return sig * cell, (cell, hh, r, sig)


HEADS_PER_STEP = 2


def _mlstm_fwd(qc, kc, kct, proj_a, gcols, gbt, g_mlstm, heads, d_model, name, dep):
    t = qc.shape[0]
    dh = g_mlstm.shape[1] // heads
    nc = t // CHUNK
    scale = dh ** -0.5
    hp = HEADS_PER_STEP

    def body(q_ref, k_ref, kt_ref, v_ref, o_ref, li_ref, b_ref, gbt_ref, g_ref, dep_ref, out_ref, ct_out, nm_out,
             ct_ref, n_ref, m_ref):
        del dep_ref
        step = pl.program_id(0)
        ct_ref[...] = jnp.zeros_like(ct_ref)
        n_ref[...] = jnp.zeros_like(n_ref)
        m_ref[...] = jnp.zeros_like(m_ref)
        tile_row = lax.broadcasted_iota(jnp.int32, (8, LANES), 0)

        def chunk(ci, carry):
            sl = pl.ds(pl.multiple_of(ci * CHUNK, CHUNK), CHUNK)
            for s in range(hp):
                cs = slice(s * dh, (s + 1) * dh)
                q, k, v, kt = q_ref[sl, cs], k_ref[sl, cs] * scale, v_ref[sl, cs], kt_ref[cs, sl]
                li_c, b_c, li_r, b_r = _gate_views(li_ref, b_ref, gbt_ref, s, step * hp + s, heads, sl)
                ct, nrow, m_prev = ct_ref[s], n_ref[s, 0:1, :], m_ref[s, 0:1, 0:1]
                ct_out[s, ci] = ct
                nm_out[s, ci] = jnp.where(tile_row == 0, nrow, jnp.where(tile_row == 1, m_prev, 0.0))
                dw, iw, mt, _, sm = _chunk_scores(q, k, b_c, li_r, b_r, m_prev)
                iq = iw * q
                num = _dot(jnp.concatenate([iq, sm], axis=1), jnp.concatenate([ct, v], axis=0), NN)
                den = jnp.sum(iq * nrow + sm, axis=-1, keepdims=True)
                h = num / jnp.maximum(jnp.abs(den), jnp.exp(-mt))
                y, _ = _head_out(h, o_ref[sl, cs], g_ref[:, cs])
                out_ref[sl, cs] = y.astype(out_ref.dtype)
                ct_new, n_new, m_new, _, _, _ = _chunk_state(kt, k, v, li_c, b_c, ct, nrow, m_prev)
                ct_ref[s] = ct_new
                n_ref[s, 0:1, :] = n_new
                m_ref[s, 0:1, 0:1] = m_new
            return carry

        lax.fori_loop(0, nc, chunk, 0, unroll=2)

    col = lambda off: pl.BlockSpec((t, hp * dh), lambda h: (0, off // hp + h))
    slab = lambda off: pl.BlockSpec((hp, t, LANES), lambda h: (off // hp + h, 0, 0))
    return pl.pallas_call(
        body, name=name,
        out_shape=[jax.ShapeDtypeStruct((t, d_model), BF16), jax.ShapeDtypeStruct((heads, nc, dh, dh), F32),
                   jax.ShapeDtypeStruct((heads, nc, 8, LANES), F32)],
        grid=(heads // hp,),
        in_specs=[col(0), col(0), pl.BlockSpec((hp * dh, t), lambda h: (h, 0)), col(2 * heads), col(3 * heads),
                  slab(0), slab(heads), pl.BlockSpec((2 * heads, 1, t), lambda h: (0, 0, 0)),
                  pl.BlockSpec((1, hp * dh), lambda h: (0, h)), ANY_SPEC],
        out_specs=[pl.BlockSpec((t, hp * dh), lambda h: (0, h)),
                   pl.BlockSpec((hp, nc, dh, dh), lambda h: (h, 0, 0, 0)),
                   pl.BlockSpec((hp, nc, 8, LANES), lambda h: (h, 0, 0, 0))],
        scratch_shapes=[pltpu.VMEM((hp, dh, dh), F32), pltpu.VMEM((hp, 8, dh), F32), pltpu.VMEM((hp, 8, LANES), F32)],
        compiler_params=_params("arbitrary"),
    )(qc, kc, kct, proj_a, proj_a, gcols, gcols, gbt, g_mlstm, dep)


def _mlstm_bwd(qc, kc, qct, kct, proj_a, gcols, gbt, g_mlstm, states, dhcat, heads, name, dep):
    t = qc.shape[0]
    wm = g_mlstm.shape[1]
    dh = wm // heads
    nc = t // CHUNK
    scale = dh ** -0.5

    def body(q_ref, k_ref, qt_ref, kt_ref, v_ref, o_ref, li_ref, b_ref, gbt_ref, g_ref, ct_all, nm_all, dy_ref, dep_ref,
             dq_ref, dk_ref, dv_ref, do_ref, dcol_ref, drow_ref, dg_ref, dct_ref, dn_ref):
        del dep_ref
        head = pl.program_id(0)
        g = g_ref[...]

        @pl.when(head == 0)
        def _():
            dcol_ref[...] = jnp.zeros_like(dcol_ref)
            drow_ref[...] = jnp.zeros_like(drow_ref)

        def load(ci):
            sl = pl.ds(pl.multiple_of(ci * CHUNK, CHUNK), CHUNK)
            return (sl, q_ref[sl, :], k_ref[sl, :] * scale, v_ref[sl, :], kt_ref[:, sl]) \
                + _gate_views(li_ref, b_ref, gbt_ref, 0, head, heads, sl)

        dct_ref[...] = jnp.zeros_like(dct_ref)
        dn_ref[...] = jnp.zeros_like(dn_ref)
        dg_ref[...] = jnp.zeros_like(dg_ref)
        rows, cols = _chunk_iotas()
        last = lax.broadcasted_iota(jnp.int32, (CHUNK, 1), 0) == CHUNK - 1
        lane = lax.broadcasted_iota(jnp.int32, (CHUNK, LANES), 1)

        def bwd(step, carry):
            ci = nc - 1 - step
            sl, q, k, v, kt, li_c, b_c, li_r, b_r = load(ci)
            qt = qt_ref[:, sl]
            ct, nrow, m_prev = ct_all[ci], nm_all[ci, 0:1, :], nm_all[ci, 1:2, 0:1]
            dw, iw, mt, a, sm = _chunk_scores(q, k, b_c, li_r, b_r, m_prev)
            qc_ = _dot(q, ct, NN)
            qn = jnp.sum(q * nrow, axis=-1, keepdims=True)
            num = iw * qc_ + _dot(sm, v, NN)
            den = iw * qn + jnp.sum(sm, axis=-1, keepdims=True)
            floor = jnp.exp(-mt)
            dn = jnp.maximum(jnp.abs(den), floor)
            h = num / dn
            _, (cell, hh, r, sig) = _head_out(h, o_ref[sl, :], g)
            dy = dy_ref[sl, :]
            dcell = dy * sig
            do_ref[sl, :] = (dy * cell * sig * (1.0 - sig)).astype(do_ref.dtype)
            dh_, dg = _rms_b(dcell, g, hh, r)
            dg_ref[...] += dg
            dnum = dh_ / dn
            ddn = -jnp.sum(dh_ * h, axis=-1, keepdims=True) / dn
            dden = jnp.where(jnp.abs(den) >= floor, ddn * jnp.sign(den), 0.0)
            idn = iw * dnum
            idd = iw * dden
            dw_t = jnp.exp(jnp.where(cols >= rows, b_r - b_c + li_c, -jnp.inf) - _to_row(mt))
            sm_t = _dot(k, q, NT) * dw_t
            dsm = _dot(dnum, v, NT) + dden
            da_ = dsm * dw
            da_t = (_dot(v, dnum, NT) + _to_row(dden)) * dw_t
            e = da_ * a
            dq = _dot(idn, ct, NT) + _dot(da_, k, NN) + idd * nrow
            dk = _dot(da_t, q, NN)
            dv = _dot(sm_t, dnum, NN)
            dct = _dot(qt, idn, NN)
            dnr = jnp.sum(q * idd, axis=0, keepdims=True)
            db = jnp.sum(e + idn * qc_, axis=-1, keepdims=True) + idd * qn
            _, _, _, dec, w, wv = _chunk_state(kt, k, v, li_c, b_c, ct, nrow, m_prev)
            dct_new, dn_new = dct_ref[...], dn_ref[0:1, :]
            ddec = jnp.sum(jnp.sum(dct_new * ct, axis=-1, keepdims=True), axis=0, keepdims=True) \
                + jnp.sum(dn_new * nrow, axis=-1, keepdims=True)
            dk = dk + _dot(wv, dct_new, NT) + w * dn_new
            dwv = _dot(k, dct_new, NN)
            dv = dv + w * dwv
            dwt = jnp.sum(dwv * v + k * dn_new, axis=-1, keepdims=True)
            da = dwt * w
            dbtot = jnp.sum(da, axis=0, keepdims=True) + ddec * dec
            db = db - da + jnp.where(last, dbtot, 0.0)
            dct_ref[...] = dec * dct_new + dct
            dn_ref[0:1, :] = dec * dn_new + dnr
            dq_ref[sl, :] = dq
            dk_ref[sl, :] = dk * scale
            dv_ref[sl, :] = dv.astype(dv_ref.dtype)
            dcol_ref[sl, :] += jnp.where(lane == head, da, 0.0) + jnp.where(lane == heads + head, db, 0.0)
            drow_ref[head, :, sl] = jnp.sum(e, axis=0, keepdims=True)
            return carry

        lax.fori_loop(0, nc, bwd, 0, unroll=8)

    col = lambda off: pl.BlockSpec((t, dh), lambda h: (0, off + h))
    rowb = pl.BlockSpec((dh, t), lambda h: (h, 0))
    slab = lambda off: pl.BlockSpec((1, t, LANES), lambda h: (off + h, 0, 0))
    vec = pl.BlockSpec((1, dh), lambda h: (0, h))
    return pl.pallas_call(
        body, name=name,
        out_shape=[jax.ShapeDtypeStruct((t, wm), F32)] * 2 + [jax.ShapeDtypeStruct((t, wm), BF16)] * 2
        + [jax.ShapeDtypeStruct((t, LANES), F32), jax.ShapeDtypeStruct((heads, 1, t), F32),
           jax.ShapeDtypeStruct((1, wm), F32)],
        grid=(heads,),
        in_specs=[col(0), col(0), rowb, rowb, col(2 * heads), col(3 * heads), slab(0), slab(heads),
                  pl.BlockSpec((2 * heads, 1, t), lambda h: (0, 0, 0)), vec,
                  pl.BlockSpec((None, nc, dh, dh), lambda h: (h, 0, 0, 0)),
                  pl.BlockSpec((None, nc, 8, LANES), lambda h: (h, 0, 0, 0)), col(0), ANY_SPEC],
        out_specs=[col(0), col(0), col(0), col(0), pl.BlockSpec((t, LANES), lambda h: (0, 0)),
                   pl.BlockSpec((heads, 1, t), lambda h: (0, 0, 0)), vec],
        scratch_shapes=[pltpu.VMEM((dh, dh), F32), pltpu.VMEM((8, dh), F32)],
        compiler_params=_params("arbitrary"),
    )(qc, kc, qct, kct, proj_a, proj_a, gcols, gcols, gbt, g_mlstm, *states, dhcat, dep)


def _pool_window(gi):
    return jnp.where(gi == 0, 2.0, jnp.where(gi == 1, 4.0, jnp.where(gi == 2, 8.0, 16.0))).astype(F32)


def _pool_select(gi, levels):
    return jnp.where(gi == 0, levels[0], jnp.where(gi == 1, levels[1], jnp.where(gi == 2, levels[2], levels[3])))


def _pooled(u, gi, rows):
    s, levels = u, []
    for lvl in range(POOL_GROUPS):
        s = s + _shift_down(s, 1 << lvl, rows)
        levels.append(s)
    cnt = jnp.minimum((rows + 1).astype(F32), _pool_window(gi))
    return _pool_select(gi, levels) / cnt - u, cnt


def _pool_fwd(proj_b, w_pool, scale, hcat, name):
    t, d_model = hcat.shape
    g, cg, _ = w_pool.shape
    off = (d_model - g * cg) // cg

    def body(u_ref, w_ref, s_ref, hcat_ref, out_ref):
        del hcat_ref
        gi = pl.program_id(0)
        u = u_ref[...]
        rows = lax.broadcasted_iota(jnp.int32, u.shape, 0)
        pooled, _ = _pooled(u, gi, rows)
        out_ref[...] = (_dot(pooled, w_ref[...], NN) * s_ref[...]).astype(out_ref.dtype)

    return pl.pallas_call(
        body, name=name,
        out_shape=jax.ShapeDtypeStruct((t, d_model), BF16),
        grid=(g,),
        in_specs=[pl.BlockSpec((t, cg), lambda gi: (0, gi)), pl.BlockSpec((None, cg, cg), lambda gi: (gi, 0, 0)),
                  pl.BlockSpec((1, cg), lambda gi: (0, gi)), HBM_SPEC],
        out_specs=pl.BlockSpec((t, cg), lambda gi: (0, off + gi)),
        input_output_aliases={3: 0},
        compiler_params=_params("arbitrary"),
    )(proj_b, w_pool, scale, hcat)


def _pool_bwd(proj_b, w_pool, scale, dhcat, name):
    t, d_model = dhcat.shape
    g, cg, _ = w_pool.shape
    off = (d_model - g * cg) // cg

    def body(u_ref, w_ref, s_ref, dy_ref, du_ref, dw_ref, ds_ref):
        gi = pl.program_id(0)
        u, w, dy = u_ref[...], w_ref[...], dy_ref[...]
        rows = lax.broadcasted_iota(jnp.int32, u.shape, 0)
        pooled, cnt = _pooled(u, gi, rows)
        mixed = _dot(pooled, w, NN)
        ds_ref[...] = jnp.sum(dy * mixed, axis=0, keepdims=True)
        dmixed = dy * s_ref[...]
        dw_ref[...] = _dot(pooled, dmixed, TN).astype(dw_ref.dtype)
        dpooled = _dot(dmixed, w, NT)
        s, levels = dpooled / cnt, []
        for lvl in range(POOL_GROUPS):
            s = s + _shift_up(s, 1 << lvl, rows)
            levels.append(s)
        du_ref[...] = (_pool_select(gi, levels) - dpooled).astype(du_ref.dtype)

    return pl.pallas_call(
        body, name=name,
        out_shape=[jax.ShapeDtypeStruct((t, g * cg), BF16), jax.ShapeDtypeStruct((g, cg, cg), BF16),
                   jax.ShapeDtypeStruct((1, g * cg), F32)],
        grid=(g,),
        in_specs=[pl.BlockSpec((t, cg), lambda gi: (0, gi)), pl.BlockSpec((None, cg, cg), lambda gi: (gi, 0, 0)),
                  pl.BlockSpec((1, cg), lambda gi: (0, gi)), pl.BlockSpec((t, cg), lambda gi: (0, off + gi))],
        out_specs=[pl.BlockSpec((t, cg), lambda gi: (0, gi)), pl.BlockSpec((None, cg, cg), lambda gi: (gi, 0, 0)),
                   pl.BlockSpec((1, cg), lambda gi: (0, gi))],
        compiler_params=_params("arbitrary"),
    )(proj_b, w_pool, scale, dhcat)


def _pad_cols(a, width):
    return jnp.pad(a, ((0, 0), (0, width - a.shape[1])))


def kernel(x, p, w_in, b_gates, w_qk_conv, g_mlstm, w_pool, pool_scale, w_out, g_mix_pre, g_mix_post, w_ff1, w_ff2, g_ff_pre, g_ff_post, w_ple_proj, w_ple_gate, g_ple_gate, g_ple_post, loss_target, m_w_in, m_b_gates, m_w_qk_conv, m_g_mlstm, m_w_pool, m_pool_scale, m_w_out, m_g_mix_pre, m_g_mix_post, m_w_ff1, m_w_ff2, m_g_ff_pre, m_g_ff_post, m_w_ple_proj, m_w_ple_gate, m_g_ple_gate, m_g_ple_post, v_w_in, v_b_gates, v_w_qk_conv, v_g_mlstm, v_w_pool, v_pool_scale, v_w_out, v_g_mix_pre, v_g_mix_post, v_w_ff1, v_w_ff2, v_g_ff_pre, v_g_ff_post, v_w_ple_proj, v_w_ple_gate, v_g_ple_gate, v_g_ple_post):
    weights = dict(w_in=w_in, b_gates=b_gates, w_qk_conv=w_qk_conv, g_mlstm=g_mlstm, w_pool=w_pool,
                   pool_scale=pool_scale, w_out=w_out, g_mix_pre=g_mix_pre, g_mix_post=g_mix_post, w_ff1=w_ff1,
                   w_ff2=w_ff2, g_ff_pre=g_ff_pre, g_ff_post=g_ff_post, w_ple_proj=w_ple_proj,
                   w_ple_gate=w_ple_gate, g_ple_gate=g_ple_gate, g_ple_post=g_ple_post)
    mom1 = dict(w_in=m_w_in, b_gates=m_b_gates, w_qk_conv=m_w_qk_conv, g_mlstm=m_g_mlstm, w_pool=m_w_pool,
                pool_scale=m_pool_scale, w_out=m_w_out, g_mix_pre=m_g_mix_pre, g_mix_post=m_g_mix_post,
                w_ff1=m_w_ff1, w_ff2=m_w_ff2, g_ff_pre=m_g_ff_pre, g_ff_post=m_g_ff_post,
                w_ple_proj=m_w_ple_proj, w_ple_gate=m_w_ple_gate, g_ple_gate=m_g_ple_gate, g_ple_post=m_g_ple_post)
    mom2 = dict(w_in=v_w_in, b_gates=v_b_gates, w_qk_conv=v_w_qk_conv, g_mlstm=v_g_mlstm, w_pool=v_w_pool,
                pool_scale=v_pool_scale, w_out=v_w_out, g_mix_pre=v_g_mix_pre, g_mix_post=v_g_mix_post,
                w_ff1=v_w_ff1, w_ff2=v_w_ff2, g_ff_pre=v_g_ff_pre, g_ff_post=v_g_ff_post,
                w_ple_proj=v_w_ple_proj, w_ple_gate=v_w_ple_gate, g_ple_gate=v_g_ple_gate, g_ple_post=v_g_ple_post)
    order = list(weights)

    t, d = x.shape[1], x.shape[2]
    heads = MLSTM_HEADS
    wm = g_mlstm.shape[1]
    pw = pool_scale.shape[1]
    cg = pw // POOL_GROUPS
    in_cols = 4 * wm + 2 * heads + pw
    in_shard = w_in.shape[2]
    xs, ps, tgt = x[0], p[0, 0], loss_target[0]

    mx, my, mc = lax.axis_index("x"), lax.axis_index("y"), lax.axis_index("c")
    core = jnp.reshape(mc, (1,)).astype(jnp.int32)
    chip = jnp.reshape(2 * mx + my, (1,)).astype(jnp.int32)

    first = [w_in[0].T.astype(BF16), w_qk_conv[0], w_pool[0].reshape(POOL_GROUPS * w_pool.shape[2], cg).astype(BF16)]
    started, token = _gather_start([first], "gather_start_in")
    anchor = token[0, 0]
    later = lambda w: (w[0] + anchor).astype(BF16)
    ff2_shard = later(w_ff2)
    rest, token = _gather_start([[later(w_out)], [later(w_ff1)], [ff2_shard[:, :d // 2]], [ff2_shard[:, d // 2:]],
                                 [later(w_ple_gate), later(w_ple_proj)]], "gather_start_rest")
    started = started + rest
    bias = _pad_cols(b_gates, LANES)
    small = ["b_gates", "g_mlstm", "pool_scale", "g_mix_pre", "g_mix_post", "g_ff_pre", "g_ff_post", "g_ple_gate",
             "g_ple_post"]
    pack = lambda src: _pad_rows16(jnp.concatenate([_pad_cols(src[k] + anchor, d) for k in small], axis=0))
    small_state = [pack(weights), pack(mom1), pack(mom2)]
    in_state = [a[0].T + anchor for a in (w_in, m_w_in, v_w_in)]

    xn = _norm_fwd(xs, g_mix_pre, "norm_mix_pre", deps=[token])
    g_in, g_conv, g_pool = _forward_pair(
        _gather_wait(started[0], [xn] + small_state + in_state, "gather_wait_in"), "gather_pair_in")
    in_t_a = g_in.reshape(in_cols, d)
    in_t_b = jnp.concatenate([in_t_a[4 * wm + 2 * heads:], in_t_a[4 * wm:4 * wm + 2 * heads],
                              jnp.zeros((LANES - 2 * heads, d), BF16)], axis=0)
    conv_w = g_conv.transpose(1, 0, 2).reshape(QK_CONV, 2 * wm)
    pool_w = g_pool.reshape(N_DEV, POOL_GROUPS, cg // N_DEV, cg).transpose(1, 0, 2, 3).reshape(POOL_GROUPS, cg, cg)
    proj_a = _mm_nt("proj_qkvo", xn, in_t_a, n_rows=4 * wm)
    proj_b = _mm_nt("proj_pool_gates", xn, in_t_b)
    qc = _conv_fwd(proj_a, conv_w[:, :wm], 0, "q_conv")
    kc = _conv_fwd(proj_a, conv_w[:, wm:], wm, "k_conv")
    gb, gbt, gcols = _gates_fwd(proj_b, bias, pw // LANES, heads, "gates")
    gbt = gbt[:2 * heads].reshape(2 * heads, 1, t)
    qct, kct = qc.T.astype(BF16), (kc * (wm // heads) ** -0.5).T.astype(BF16)
    fwd_out = _forward_start(_gather_wait(started[1], gb, "gather_wait_out"), "gather_pair_start_out")
    hcat, *mlstm_states = _mlstm_fwd(qc, kc, kct, proj_a, gcols, gbt, g_mlstm, heads, d, "mlstm", fwd_out["token"])
    hcat = _pool_fwd(proj_b, pool_w, pool_scale, hcat, "pool")
    (g_out,) = _forward_wait(fwd_out, hcat, "gather_pair_wait_out")
    out_w = g_out.reshape(d, d)
    mix = _mm_nn("mix_out", hcat, out_w)[0]
    fwd_ff1 = _forward_start(_gather_wait(started[2], mix, "gather_wait_ff1"), "gather_pair_start_ff1")
    h1, hn = _block_fwd(xs, mix, g_mix_post, g_ff_pre, "residual_mix", deps=[fwd_ff1["token"]])
    (g_ff1,) = _forward_wait(fwd_ff1, hn, "gather_pair_wait_ff1")
    relu_a, act = _mm_nn_pieces("ff1", hn, g_ff1, out_dtype=[BF16, BF16],
                                epilogue=lambda acc: (jnp.maximum(acc, 0.0), jnp.square(jnp.maximum(acc, 0.0))))
    (g_ff2_lo,) = _forward_pair(_gather_wait(started[3], act, "gather_wait_ff2_lo"), "gather_pair_ff2_lo")
    ff2_lo = g_ff2_lo.reshape(-1, d // 2)
    ff = _mm_nn("ff2_lo", act, ff2_lo, tn=256, tk=4 * d, out_cols=d)[0]
    (g_ff2_hi,) = _forward_pair(_gather_wait(started[4], ff, "gather_wait_ff2_hi"), "gather_pair_ff2_hi")
    ff2_hi = g_ff2_hi.reshape(-1, d // 2)
    ff = _mm_nn("ff2_hi", act, ff2_hi, tn=256, tk=4 * d, out_cols=d, out_at=d // 2, into=ff)[0]
    fwd_ple = _forward_start(_gather_wait(started[5], ff, "gather_wait_ple"), "gather_pair_start_ple")
    h2, hg = _block_fwd(h1, ff, g_ff_post, g_ple_gate, "residual_ff", deps=[fwd_ple["token"]])
    g_pgate, g_pproj = _forward_wait(fwd_ple, hg, "gather_pair_wait_ple")
    pgate_w = g_pgate.reshape(d, d)
    z = _mm_nn("ple_gate", hg, pgate_w)[0]
    e = _mm_nn_pieces("ple_proj", ps, g_pproj)[0]
    dh3, dz, de, dg_ple_post, loss_part = _loss_head(h2, z, e, tgt, g_ple_post, "loss_head")
    loss = lax.psum(loss_part[0, 0], MESH_AXES)

    out = {}

    def state(k):
        if k == "w_in":
            return in_state
        return [a.reshape(-1, a.shape[-1]) for a in (weights[k], mom1[k], mom2[k])]

    def pair_begin(pieces, tag):
        keys = list(pieces)
        return keys, _pair_start([pieces[k] for k in keys], "reduce_pair_start_" + tag)

    def chip_begin(group, after, tag):
        keys, begun = group
        mine, got = _pair_wait(begun, after, "reduce_pair_wait_" + tag)
        sums = [_pair_sum(p_, g, core, "pair_sum_" + k) for k, p_, g in zip(keys, mine, got)]
        return keys, _chip_start(sums, "reduce_start_" + tag)

    def reduce_finish(group, after, tag):
        keys, begun = group
        sums, got = _chip_wait(begun, after, "reduce_wait_" + tag)
        for k, s, g in zip(keys, sums, got):
            res = _adamw_shard(*state(k), s, g, chip, "adamw_" + k)
            out[k] = [r.T[None] if k == "w_in" else r.reshape(weights[k].shape) for r in res]

    token_of = lambda group: group[1]["token"]
    dhg = _mm_nt("d_ple_gate_in", dz, pgate_w)
    gw_pgate = _mm_tn("gw_ple_gate", hg, dz)
    gw_pproj = _mm_tn("gw_ple_proj", ps, de, pieces=N_DEV)
    dh2, dff, dg_ff_post, dg_ple_gate = _block_bwd(h2, ff, dh3, dhg, g_ff_post, g_ple_gate, "d_residual_ff")
    da = _mm_nt("d_ff2_in", dff, ff2_lo, b2=ff2_hi, out_dtype=BF16, extra=[relu_a],
                epilogue=lambda acc, r: (acc * (2.0 * r.astype(F32)),))
    gw_ff2 = _mm_tn("gw_ff2", act, dff)
    ff2_keys = ["w_ff2", "w_ple_gate", "w_ple_proj"]
    pair_ff2 = pair_begin({"w_ff2": gw_ff2.reshape(N_DEV, -1, d),
                           "w_ple_gate": gw_pgate.reshape(N_DEV, d // N_DEV, d), "w_ple_proj": gw_pproj}, "ff2")
    dhn = _mm_nt_pieces("d_ff1_in", da, g_ff1, deps=[token_of(pair_ff2)])
    place = jnp.concatenate([core, chip])
    mine_ff2, got_ff2 = _pair_wait(pair_ff2[1], dhn, "reduce_pair_wait_ff2")
    gw_ff1, sums_ff2 = _mm_tn("gw_ff1", hn, da, pieces=N_DEV, side=_side_pair_sum(mine_ff2[0], got_ff2[0], place))
    sums_ple = [_pair_sum(p_, g, core, "pair_sum_" + k) for k, p_, g in zip(ff2_keys[1:], mine_ff2[1:], got_ff2[1:])]
    red_ff2 = (ff2_keys, _chip_start([sums_ff2] + sums_ple, "reduce_start_ff2"))
    pair_ff1 = pair_begin({"w_ff1": gw_ff1}, "ff1")
    dh1, dmix, dg_mix_post, dg_ff_pre = _block_bwd(h1, mix, dh2, dhn, g_mix_post, g_ff_pre, "d_residual_mix",
                                                   deps=[token_of(pair_ff1), token_of(red_ff2)])
    mine_ff1, got_ff1 = _pair_wait(pair_ff1[1], dmix, "reduce_pair_wait_ff1")
    dhcat, sums_ff1 = _mm_nt("d_mix_in", dmix, out_w, tm=t // 2, tn=d // 2,
                             side=_side_pair_sum(mine_ff1[0], got_ff1[0], place))
    red_ff1 = (["w_ff1"], _chip_start([sums_ff1], "reduce_start_ff1"))
    gw_out = _mm_tn("gw_out", hcat, dmix, deps=[token_of(red_ff1)])
    du, gw_pool, dg_pool_scale = _pool_bwd(proj_b, pool_w, pool_scale, dhcat, "d_pool")
    pair_out = pair_begin({
        "w_out": gw_out.reshape(N_DEV, d // N_DEV, d),
        "w_pool": gw_pool.reshape(POOL_GROUPS, N_DEV, cg // N_DEV, cg).transpose(1, 0, 2, 3).reshape(N_DEV, -1, cg)},
        "out")
    dq, dk, dv, do, dcol, drow, dg_mlstm = _mlstm_bwd(qc, kc, qct, kct, proj_a, gcols, gbt, g_mlstm, mlstm_states, dhcat,
                                                      heads, "d_mlstm", token_of(pair_out))
    red_out = chip_begin(pair_out, dq, "out")
    drow = jnp.pad(drow.reshape(heads, t), ((0, LANES - heads), (0, 0)))
    dgates, dbias = _gates_bwd(proj_b, bias, dcol, drow, pw // LANES, heads, "d_gates")
    dq_pre, gw_conv_q = _conv_bwd(proj_a, conv_w[:, :wm], dq, 0, "d_q_conv")
    dk_pre, gw_conv_k = _conv_bwd(proj_a, conv_w[:, wm:], dk, wm, "d_k_conv")
    gw_conv = jnp.concatenate([gw_conv_q, gw_conv_k], axis=1)
    dproj_a = jnp.concatenate([dq_pre, dk_pre, dv, do], axis=1)
    dproj_b = jnp.concatenate([du, dgates], axis=1)
    sums_ff2, got_ff2 = _chip_wait(red_ff2[1], token_of(red_out), "reduce_wait_ff2")
    gw_a, *update = _mm_tn("gw_in_qkvo", dproj_a, xn,
                           side=_side_adamw(*state("w_ff2"), sums_ff2[0], got_ff2[0], place))
    out["w_ff2"] = [r.reshape(weights["w_ff2"].shape) for r in update]
    for k, s, g in zip(ff2_keys[1:], sums_ff2[1:], got_ff2[1:]):
        out[k] = [r.reshape(weights[k].shape) for r in _adamw_shard(*state(k), s, g, chip, "adamw_" + k)]
    gw_b = _mm_tn("gw_in_pool_gates", dproj_b, xn)
    gw_in_t = jnp.concatenate([gw_a, gw_b[pw:pw + 2 * heads], gw_b[:pw]], axis=0)
    pair_in = pair_begin({
        "w_in": gw_in_t.reshape(N_DEV, in_shard, d),
        "w_qk_conv": gw_conv.reshape(QK_CONV, N_DEV, -1).transpose(1, 0, 2).astype(BF16)}, "in")
    reduce_finish(red_out, token_of(pair_in), "out")
    red_in = chip_begin(pair_in, out["w_out"][0], "in")
    sums_ff1, got_ff1 = _chip_wait(red_ff1[1], token_of(red_in), "reduce_wait_ff1")
    dxn_a, *update = _mm_nn("d_proj_qkvo_in", dproj_a, in_t_a, tn=512, tk=4 * wm, deps=[token_of(red_in)],
                            side=_side_adamw(*state("w_ff1"), sums_ff1[0], got_ff1[0], place))
    out["w_ff1"] = [r.reshape(weights["w_ff1"].shape) for r in update]
    dxn = _mm_nn("d_proj_in", dproj_b, in_t_b, extra=[dxn_a], epilogue=lambda acc, prev: (acc + prev,))[0]
    grad_x, dg_mix_pre = _norm_bwd(xs, dxn, dh1, g_mix_pre, "d_norm_mix_pre")

    small_grads = dict(b_gates=dbias[:, :2 * heads], g_mlstm=dg_mlstm, pool_scale=dg_pool_scale, g_mix_pre=dg_mix_pre,
                       g_mix_post=dg_mix_post, g_ff_pre=dg_ff_pre, g_ff_post=dg_ff_post, g_ple_gate=dg_ple_gate,
                       g_ple_post=dg_ple_post)
    small_begun = _everyone_start([pack(small_grads)], "small_grads_start")
    reduce_finish(red_in, small_begun["token"], "in")
    (small_parts,) = _everyone_wait(small_begun, out["w_in"][0], "small_grads_wait")
    res = _adamw_replicated(*small_state, small_parts, "adamw_replicated")
    for i, k in enumerate(small):
        width = weights[k].shape[1]
        out[k] = [r[i:i + 1, :width] for r in res]

    return (loss, grad_x[None], *[out[k][0] for k in order], *[out[k][1] for k in order],
            *[out[k][2] for k in order], *[out[k][3] for k in order])


def _pad_rows16(a):
    return jnp.pad(a, ((0, 16 - a.shape[0]), (0, 0)))
```

```python
import jax
import jax.numpy as jnp
from jax import lax
from jax.experimental import pallas as pl
from jax.experimental.pallas import tpu as pltpu

F32 = jnp.float32
BF16 = jnp.bfloat16
EPS = 1e-6
N_DEV = 8
N_CHIP = 4
LANES = 128
VMEM_LIMIT = 56 * 1024 * 1024
MLSTM_HEADS = 8
POOL_GROUPS = 4
QK_CONV = 4
CHUNK = 128
ROW_TILE = 256
ADAM_LR, ADAM_B1, ADAM_B2, ADAM_EPS, ADAM_WD, ADAM_STEP = 0.001, 0.9, 0.999, 1e-08, 0.01, 10
MESH_AXES = ("x", "y", "c")
MESH = pl.DeviceIdType.MESH
HBM_SPEC = pl.BlockSpec(memory_space=pltpu.HBM)
SEM_SPEC = pl.BlockSpec(memory_space=pltpu.SEMAPHORE)
DATAFLOW = pltpu.SideEffectType.DATAFLOW_SIDE_EFFECTING


def _params(*sem):
    if sem:
        return pltpu.CompilerParams(dimension_semantics=sem, vmem_limit_bytes=VMEM_LIMIT)
    return pltpu.CompilerParams(vmem_limit_bytes=VMEM_LIMIT)


def _tile(n, pref):
    if n <= pref:
        return n
    t = (pref // LANES) * LANES
    while t >= LANES:
        if n % t == 0:
            return t
        t -= LANES
    return n


def _hbm(a):
    return pltpu.with_memory_space_constraint(a, pltpu.HBM)


def _mesh_place():
    x, y, c = lax.axis_index("x"), lax.axis_index("y"), lax.axis_index("c")
    return x, y, c, [(1 - x, y), (x, 1 - y), (1 - x, 1 - y)]


def _chip_copies(srcs, lands, send, recv):
    x, y, c, chips = _mesh_place()
    return [pltpu.make_async_remote_copy(
        src_ref=srcs[w].at[2 * chip[0] + chip[1]], dst_ref=lands[w].at[k],
        send_sem=send.at[3 * w + k], recv_sem=recv.at[3 * w + k],
        device_id=(*chip, c), device_id_type=MESH)
        for k, chip in enumerate(chips) for w in range(len(srcs))]


def _chip_start(sums, name):
    return _split_start(_chip_copies, sums, [lax.empty((3,) + a.shape[1:], a.dtype) for a in sums], 3, name,
                        peers=_same_core_of_other_chips)


def _chip_wait(started, after, name):
    return _split_wait(_chip_copies, started, after, name)


def _pair_start(pieces, name):
    return _split_start(_pair_copies, pieces, [lax.empty((N_CHIP,) + a.shape[1:], a.dtype) for a in pieces], N_CHIP, name,
                        peers=_sibling)


def _pair_wait(started, after, name):
    return _split_wait(_pair_copies, started, after, name)


def _pair_copies(srcs, lands, send, recv):
    x, y, c, _ = _mesh_place()
    return [pltpu.make_async_remote_copy(
        src_ref=srcs[w].at[2 * ch + 1 - c], dst_ref=lands[w].at[ch],
        send_sem=send.at[N_CHIP * w + ch], recv_sem=recv.at[N_CHIP * w + ch],
        device_id=(x, y, 1 - c), device_id_type=MESH)
        for w in range(len(srcs)) for ch in range(N_CHIP)]


OWN_BARRIER_IDS = {name: i for i, name in enumerate([
    "reduce_pair_start_ff2", "reduce_pair_start_ff1", "reduce_pair_start_out", "reduce_pair_start_in",
    "gather_pair_start_out", "gather_pair_start_ff1", "gather_pair_start_ple",
    "reduce_start_ff2", "reduce_start_ff1", "reduce_start_out", "reduce_start_in",
    "gather_start_in", "gather_start_rest", "gather_pair_in", "gather_pair_ff2_lo", "gather_pair_ff2_hi",
    "gather_pair_ple_proj"])}


def _shake_hands(peers):
    barrier = pltpu.get_barrier_semaphore()
    for peer in peers:
        pl.semaphore_signal(barrier, inc=1, device_id=peer, device_id_type=MESH)
    pl.semaphore_wait(barrier, len(peers))


def _sibling():
    x, y, c, _ = _mesh_place()
    return [(x, y, 1 - c)]


def _same_core_of_other_chips():
    x, y, c, chips = _mesh_place()
    return [(*chip, c) for chip in chips]


def _split_start(copies_of, srcs, lands, per_array, name, peers=None):
    ns, nb = len(srcs), len(srcs) + len(lands)
    n_copies = per_array * len(lands)

    def body(*refs):
        if peers is not None:
            _shake_hands(peers())
        for cp in copies_of(refs[:ns], refs[ns:nb], refs[nb], refs[nb + 1]):
            cp.start()
        refs[-1][...] = jnp.zeros_like(refs[-1])

    params = dict(has_side_effects=DATAFLOW)
    if peers is not None:
        params["collective_id"] = OWN_BARRIER_IDS[name]
    res = pl.pallas_call(
        body, name=name,
        out_shape=[pltpu.SemaphoreType.DMA((n_copies,)), pltpu.SemaphoreType.DMA((n_copies,))]
        + [pltpu.HBM(a.shape, a.dtype) for a in list(srcs) + list(lands)] + [jax.ShapeDtypeStruct((8, LANES), F32)],
        in_specs=[HBM_SPEC] * nb,
        out_specs=[SEM_SPEC, SEM_SPEC] + [HBM_SPEC] * nb + [pl.BlockSpec(memory_space=pltpu.VMEM)],
        input_output_aliases={i: 2 + i for i in range(nb)},
        compiler_params=pltpu.CompilerParams(**params),
    )(*[_hbm(a) for a in srcs], *[_hbm(a) for a in lands])
    return dict(send=res[0], recv=res[1], srcs=list(res[2:2 + ns]), lands=list(res[2 + ns:2 + nb]), token=res[-1])


def _split_wait(copies_of, started, after, name):
    ns = len(started["srcs"])
    nb = ns + len(started["lands"])

    def body(*refs):
        for cp in copies_of(refs[:ns], refs[ns:nb], refs[nb], refs[nb + 1]):
            cp.wait_send()
            cp.wait_recv()

    res = pl.pallas_call(
        body, name=name,
        out_shape=[pltpu.HBM(a.shape, a.dtype) for a in started["srcs"] + started["lands"]],
        in_specs=[HBM_SPEC] * nb + [SEM_SPEC, SEM_SPEC, ANY_SPEC],
        out_specs=[HBM_SPEC] * nb,
        input_output_aliases={i: i for i in range(nb)},
        compiler_params=pltpu.CompilerParams(has_side_effects=DATAFLOW),
    )(*started["srcs"], *started["lands"], started["send"], started["recv"], after)
    return list(res[:ns]), list(res[ns:])


def _forward_copies(srcs, lands, send, recv):
    del srcs
    x, y, c, chips = _mesh_place()
    copies = []
    for w in range(len(lands)):
        for j, chip in enumerate(chips):
            rows = lands[w].at[4 * chip[0] + 2 * chip[1] + c]
            copies.append(pltpu.make_async_remote_copy(
                src_ref=rows, dst_ref=rows, send_sem=send.at[3 * w + j], recv_sem=recv.at[3 * w + j],
                device_id=(x, y, 1 - c), device_id_type=MESH))
    return copies


def _forward_start(lands, name):
    return _split_start(_forward_copies, [], lands, 3, name, peers=_sibling)


def _forward_wait(started, after, name):
    return _split_wait(_forward_copies, started, after, name)[1]


def _everyone_copies(srcs, lands, send, recv):
    x, y, c, _ = _mesh_place()
    me = 4 * x + 2 * y + c
    copies = []
    for w in range(len(srcs)):
        for k in range(N_DEV - 1):
            flip = k + 1
            peer = (1 - x if flip & 4 else x, 1 - y if flip & 2 else y, 1 - c if flip & 1 else c)
            copies.append(pltpu.make_async_remote_copy(
                src_ref=srcs[w], dst_ref=lands[w].at[me], send_sem=send.at[7 * w + k], recv_sem=recv.at[7 * w + k],
                device_id=peer, device_id_type=MESH))
    return copies


def _everyone_start(blocks, name):
    me = 4 * lax.axis_index("x") + 2 * lax.axis_index("y") + lax.axis_index("c")
    lands = [lax.dynamic_update_slice(lax.empty((N_DEV,) + a.shape, a.dtype), a[None], (me,) + (0,) * a.ndim)
             for a in blocks]
    return _split_start(_everyone_copies, blocks, lands, N_DEV - 1, name)


def _everyone_wait(started, after, name):
    return _split_wait(_everyone_copies, started, after, name)[1]


def _gather_copies(srcs, lands, send, recv):
    x, y, c, chips = _mesh_place()
    me = 4 * x + 2 * y + c
    targets = [(x, y, 1 - c)] + [(*chip, c) for chip in chips]
    return [pltpu.make_async_remote_copy(
        src_ref=srcs[w], dst_ref=lands[w].at[me], send_sem=send.at[4 * w + k], recv_sem=recv.at[4 * w + k],
        device_id=to, device_id_type=MESH)
        for w in range(len(srcs)) for k, to in enumerate(targets)]


def _gather_start(groups, name):
    sizes = [len(g) for g in groups]
    flat = [a for g in groups for a in g]
    n, ng = len(flat), len(groups)
    me = 4 * lax.axis_index("x") + 2 * lax.axis_index("y") + lax.axis_index("c")
    lands = [lax.dynamic_update_slice(lax.empty((N_DEV,) + a.shape, a.dtype), a[None], (me,) + (0,) * a.ndim)
             for a in flat]

    def body(*refs):
        _shake_hands(_sibling() + _same_core_of_other_chips())
        srcs, zones = refs[:n], refs[n:2 * n]
        sems = refs[2 * n:2 * n + 2 * ng]
        token = refs[-1]
        lo = 0
        for gi, size in enumerate(sizes):
            for cp in _gather_copies(srcs[lo:lo + size], zones[lo:lo + size], sems[2 * gi], sems[2 * gi + 1]):
                cp.start()
            lo += size
        token[...] = jnp.zeros_like(token)

    sem_shapes = []
    for size in sizes:
        sem_shapes += [pltpu.SemaphoreType.DMA((4 * size,))] * 2
    res = pl.pallas_call(
        body, name=name,
        out_shape=sem_shapes + [pltpu.HBM(a.shape, a.dtype) for a in flat + lands] + [jax.ShapeDtypeStruct((8, LANES), F32)],
        in_specs=[HBM_SPEC] * (2 * n),
        out_specs=[SEM_SPEC] * (2 * ng) + [HBM_SPEC] * (2 * n) + [pl.BlockSpec(memory_space=pltpu.VMEM)],
        input_output_aliases={i: 2 * ng + i for i in range(2 * n)},
        compiler_params=pltpu.CompilerParams(has_side_effects=DATAFLOW, collective_id=OWN_BARRIER_IDS[name]),
    )(*[_hbm(a) for a in flat], *[_hbm(a) for a in lands])
    out, lo = [], 0
    for gi, size in enumerate(sizes):
        out.append(dict(send=res[2 * gi], recv=res[2 * gi + 1],
                        srcs=list(res[2 * ng + lo:2 * ng + lo + size]),
                        lands=list(res[2 * ng + n + lo:2 * ng + n + lo + size])))
        lo += size
    return out, res[-1]


def _gather_wait(started, after, name):
    n = len(started["srcs"])
    after = list(after) if isinstance(after, (list, tuple)) else [after]

    def body(*refs):
        srcs, zones = refs[:n], refs[n:2 * n]
        send, recv = refs[2 * n], refs[2 * n + 1]
        for cp in _gather_copies(srcs, zones, send, recv):
            cp.wait_send()
            cp.wait_recv()

    res = pl.pallas_call(
        body, name=name,
        out_shape=[pltpu.HBM(a.shape, a.dtype) for a in started["srcs"] + started["lands"]],
        in_specs=[HBM_SPEC] * (2 * n) + [SEM_SPEC, SEM_SPEC] + [ANY_SPEC] * len(after),
        out_specs=[HBM_SPEC] * (2 * n),
        input_output_aliases={i: i for i in range(2 * n)},
        compiler_params=pltpu.CompilerParams(has_side_effects=DATAFLOW),
    )(*started["srcs"], *started["lands"], started["send"], started["recv"], *after)
    return list(res[n:])


def _forward_pair(lands, name):
    n = len(lands)

    def body(*refs):
        _shake_hands(_sibling())
        zones = refs[:n]
        send, recv = refs[2 * n:]
        x, y, c, chips = _mesh_place()

        def copy(w, j, core):
            rows = zones[w].at[4 * chips[j][0] + 2 * chips[j][1] + core]
            return pltpu.make_async_remote_copy(
                src_ref=rows, dst_ref=rows, send_sem=send.at[3 * w + j], recv_sem=recv.at[3 * w + j],
                device_id=(x, y, 1 - c), device_id_type=MESH)

        sends = [copy(w, j, c) for w in range(n) for j in range(3)]
        for cp in sends:
            cp.start()
        for w in range(n):
            for j in range(3):
                copy(w, j, 1 - c).wait_recv()
        for cp in sends:
            cp.wait_send()

    return list(pl.pallas_call(
        body, name=name,
        out_shape=[jax.ShapeDtypeStruct(a.shape, a.dtype) for a in lands],
        in_specs=[HBM_SPEC] * n, out_specs=[HBM_SPEC] * n,
        input_output_aliases={i: i for i in range(n)},
        scratch_shapes=[pltpu.SemaphoreType.DMA((3 * n,)), pltpu.SemaphoreType.DMA((3 * n,))],
        compiler_params=pltpu.CompilerParams(collective_id=OWN_BARRIER_IDS[name]),
    )(*lands))


def _pair_sum(pieces, got, core, name):
    _, rows, cols = pieces.shape
    tr, tc = _tile_2d(rows, cols)

    def body(core_ref, mine_ref, got_ref, out_ref):
        del core_ref
        out_ref[...] = (mine_ref[...].astype(F32) + got_ref[...].astype(F32)).astype(out_ref.dtype)

    return pl.pallas_call(
        body, name=name,
        out_shape=jax.ShapeDtypeStruct((N_CHIP, rows, cols), pieces.dtype),
        grid_spec=pltpu.PrefetchScalarGridSpec(
            num_scalar_prefetch=1, grid=(N_CHIP, rows // tr, cols // tc),
            in_specs=[pl.BlockSpec((None, tr, tc), lambda ch, i, j, core: (2 * ch + core[0], i, j)),
                      pl.BlockSpec((None, tr, tc), lambda ch, i, j, core: (ch, i, j))],
            out_specs=pl.BlockSpec((None, tr, tc), lambda ch, i, j, core: (ch, i, j))),
        compiler_params=_params("parallel", "parallel", "parallel"),
    )(core, pieces, got)


def _tile_2d(rows, cols, budget=2 * 1024 * 1024):
    want = max(16, budget // (4 * cols))
    if rows <= want:
        return rows, cols
    t = (want // 16) * 16
    while t >= 16:
        if rows % t == 0:
            return t, cols
        t -= 16
    return rows, _tile(cols, max(LANES, budget // (4 * rows)))


def _adamw_math(w, g, m, v):
    m = ADAM_B1 * m + (1.0 - ADAM_B1) * g
    v = ADAM_B2 * v + (1.0 - ADAM_B2) * (g * g)
    m_hat = m / (1.0 - ADAM_B1 ** ADAM_STEP)
    v_hat = v / (1.0 - ADAM_B2 ** ADAM_STEP)
    delta = -ADAM_LR * (m_hat / (jnp.sqrt(v_hat) + ADAM_EPS) + ADAM_WD * w)
    return delta, m, v


def _adamw_shard(w, m, v, sums, got, chip, name):
    rows, cols = w.shape
    tr, tc = _tile_2d(rows, cols, budget=1024 * 1024)

    def body(chip_ref, w_ref, m_ref, v_ref, own_ref, got_ref, g_out, d_out, m_out, v_out):
        del chip_ref
        g = own_ref[...].astype(F32)
        for k in range(3):
            g = g + got_ref[k].astype(F32)
        delta, m_new, v_new = _adamw_math(w_ref[...], g, m_ref[...], v_ref[...])
        g_out[...] = g
        d_out[...] = delta
        m_out[...] = m_new
        v_out[...] = v_new

    blk = pl.BlockSpec((tr, tc), lambda i, j, chip: (i, j))
    return pl.pallas_call(
        body, name=name,
        out_shape=[jax.ShapeDtypeStruct((rows, cols), F32)] * 4,
        grid_spec=pltpu.PrefetchScalarGridSpec(
            num_scalar_prefetch=1, grid=(rows // tr, cols // tc),
            in_specs=[blk, blk, blk,
                      pl.BlockSpec((None, tr, tc), lambda i, j, chip: (chip[0], i, j)),
                      pl.BlockSpec((3, tr, tc), lambda i, j, chip: (0, i, j))],
            out_specs=[blk] * 4),
        compiler_params=_params("parallel", "parallel"),
    )(chip, w, m, v, sums, got)


def _adamw_replicated(w, m, v, parts, name):
    rows, cols = w.shape

    def body(w_ref, m_ref, v_ref, parts_ref, g_out, d_out, m_out, v_out):
        g = parts_ref[0]
        for d in range(1, N_DEV):
            g = g + parts_ref[d]
        delta, m_new, v_new = _adamw_math(w_ref[...], g, m_ref[...], v_ref[...])
        g_out[...] = g
        d_out[...] = delta
        m_out[...] = m_new
        v_out[...] = v_new

    return pl.pallas_call(
        body, name=name,
        out_shape=[jax.ShapeDtypeStruct((rows, cols), F32)] * 4,
        compiler_params=_params(),
    )(w, m, v, parts)


NN = ((1,), (0,))
NT = ((1,), (1,))
TN = ((0,), (0,))


ANY_SPEC = pl.BlockSpec(memory_space=pl.ANY)


def _mm(name, a, b, *, dims, grid, a_spec, b_spec, outs, extra=(), extra_specs=(), epilogue=None, acc_shape=None,
        deps=(), b_pieces=0, side=None, b2=None, into=None):
    nk = grid[2]
    n_extra, n_out = len(extra), len(outs)
    side_ins = side["ins"] if side else []
    side_outs = side["outs"] if side else []
    first_side = 2 + n_extra + len(deps)
    tail = ([b2] if b2 is not None else []) + ([into] if into is not None else [])
    first_out = first_side + len(side_ins) + len(tail)
    assert not side or nk == 1

    assert not side or (epilogue is None and n_out == 1 and not b_pieces and not tail)
    SIDE_SLICES = 4

    def body(*refs):
        dot = lambda lhs, rhs: lax.dot_general(lhs.astype(BF16), rhs.astype(BF16), (dims, ((), ())),
                                               preferred_element_type=F32)
        if side:
            refs = refs[1:]
            a_ref, b_ref, out_ref = refs[0], refs[1], refs[first_out]
            ins, sides = refs[first_side:first_out], refs[first_out + 1:first_out + 1 + len(side_outs)]
            rows = out_ref.shape[0] // SIDE_SLICES
            srows = sides[0].shape[-2] // SIDE_SLICES
            for part in range(SIDE_SLICES):
                rs = slice(part * srows, (part + 1) * srows)
                for val, o in zip(side["fn"](*[r[..., rs, :] for r in ins]), sides):
                    o[rs, :] = val.astype(o.dtype)
                ms = slice(part * rows, (part + 1) * rows)
                lhs = a_ref[:, ms] if dims == TN else a_ref[ms, :]
                out_ref[ms, :] = dot(lhs, b_ref[...]).astype(out_ref.dtype)
            return
        a_ref, b_ref = refs[0], refs[1]
        ex = refs[2:2 + n_extra]
        out_refs = refs[first_out:first_out + n_out]
        if b_pieces:
            ks = b_ref.shape[-1]
            part = dot(a_ref[:, 0:ks], b_ref[0])
            for piece in range(1, b_pieces):
                part = part + dot(a_ref[:, piece * ks:(piece + 1) * ks], b_ref[piece])
        elif b2 is not None:
            ks = b_ref.shape[-1]
            part = dot(a_ref[:, 0:ks], b_ref[...]) + dot(a_ref[:, ks:2 * ks], refs[first_side][...])
        else:
            part = dot(a_ref[...], b_ref[...])

        def finish(acc):
            res = epilogue(acc, *[e[...] for e in ex]) if epilogue else (acc,)
            for val, o in zip(res, out_refs):
                o[...] = val.astype(o.dtype)

        if nk == 1:
            finish(part)
        else:
            acc_ref = refs[-1]
            k = pl.program_id(2)

            @pl.when(k == 0)
            def _():
                acc_ref[...] = part

            @pl.when(k > 0)
            def _():
                acc_ref[...] += part

            @pl.when(k == nk - 1)
            def _():
                finish(acc_ref[...])

    scratch = [pltpu.VMEM(acc_shape, F32)] if nk > 1 else []
    out_shape = [jax.ShapeDtypeStruct(s, d) for s, d, _ in outs]
    in_specs = [a_spec, b_spec, *extra_specs] + [ANY_SPEC] * len(deps)
    out_specs = [sp for _, _, sp in outs]
    if not side:
        in_specs += ([b_spec] if b2 is not None else []) + ([ANY_SPEC] if into is not None else [])
        return list(pl.pallas_call(
            body, name=name, out_shape=out_shape, grid=grid, in_specs=in_specs, out_specs=out_specs,
            scratch_shapes=scratch, compiler_params=_params("parallel", "parallel", "arbitrary"),
            input_output_aliases={len(in_specs) - 1: 0} if into is not None else {},
        )(a, b, *extra, *deps, *tail))

    def with_scalars(spec):
        if spec.index_map is None:
            return spec
        return pl.BlockSpec(spec.block_shape, lambda i, j, k, s, f=spec.index_map: f(i, j, k))

    def by_step(block, index_map):
        return pl.BlockSpec(block, lambda i, j, k, s: index_map((i * grid[1] + j) * grid[2] + k, s))

    return list(pl.pallas_call(
        body, name=name,
        out_shape=out_shape + [jax.ShapeDtypeStruct(s, d) for s, d, _, _ in side_outs],
        grid_spec=pltpu.PrefetchScalarGridSpec(
            num_scalar_prefetch=1, grid=grid,
            in_specs=[with_scalars(sp) for sp in in_specs] + [by_step(blk, im) for _, blk, im in side_ins],
            out_specs=[with_scalars(sp) for sp in out_specs] + [by_step(blk, im) for _, _, blk, im in side_outs],
            scratch_shapes=scratch),
        compiler_params=_params("arbitrary", "arbitrary", "arbitrary"),
    )(side["scalars"], a, b, *extra, *deps, *[arr for arr, _, _ in side_ins]))


def _side_pair_sum(pieces, got, place):
    _, rows, cols = pieces.shape

    def build(steps):
        per_chip = steps // N_CHIP
        tr = rows // per_chip
        assert per_chip * N_CHIP == steps and tr * per_chip == rows and tr % 16 == 0
        blk = (None, tr, cols)
        here = lambda st, s: (st // per_chip, st % per_chip, 0)
        return dict(scalars=place,
                    ins=[(pieces, blk, lambda st, s: (2 * (st // per_chip) + s[0], st % per_chip, 0)), (got, blk, here)],
                    outs=[((N_CHIP, rows, cols), pieces.dtype, blk, here)],
                    fn=lambda mine, other: (mine.astype(F32) + other.astype(F32),))
    return build


def _side_adamw(w, m, v, sums, got, place):
    rows, cols = w.shape

    def build(steps):
        tr = rows // steps
        assert tr * steps == rows and tr % 16 == 0
        blk, here = (tr, cols), (lambda st, s: (st, 0))

        def fn(w, m, v, own, got):
            g = own.astype(F32)
            for k in range(3):
                g = g + got[k].astype(F32)
            delta, m_new, v_new = _adamw_math(w, g, m, v)
            return g, delta, m_new, v_new

        return dict(scalars=place,
                    ins=[(w, blk, here), (m, blk, here), (v, blk, here),
                         (sums, (None, tr, cols), lambda st, s: (s[1], st, 0)),
                         (got, (3, tr, cols), lambda st, s: (0, st, 0))],
                    outs=[((rows, cols), F32, blk, here)] * 4, fn=fn)
    return build


def _mm_nn(name, a, b, out_dtype=F32, tm=1024, tn=1024, tk=2048, epilogue=None, n_out=1, extra=(), deps=(), side=None,
           out_cols=None, out_at=0, into=None):
    m, kd = a.shape
    n = b.shape[1]
    tm, tn, tk = _tile(m, tm), _tile(n, tn), _tile(kd, tk)
    dts = out_dtype if isinstance(out_dtype, (list, tuple)) else [out_dtype] * n_out
    shift = out_at // tn
    o_spec = pl.BlockSpec((tm, tn), lambda i, j, k: (i, j + shift))
    grid = (m // tm, n // tn, kd // tk)
    return _mm(name, a, b, dims=NN, grid=grid,
               a_spec=pl.BlockSpec((tm, tk), lambda i, j, k: (i, k)),
               b_spec=pl.BlockSpec((tk, tn), lambda i, j, k: (k, j)),
               outs=[((m, out_cols or n), dt, o_spec) for dt in dts], extra=extra, extra_specs=[o_spec] * len(extra),
               epilogue=epilogue, acc_shape=(tm, tn), deps=deps, side=side and side(grid[0] * grid[1] * grid[2]),
               into=into)


def _mm_nn_pieces(name, a, b, out_dtype=F32, tm=1024, tk=2048, epilogue=None, n_out=1):
    m, kd = a.shape
    npc, _, ns = b.shape
    tm, tk = _tile(m, tm), _tile(kd, tk)
    tn = _tile(ns, 1024)
    r = ns // tn
    dts = out_dtype if isinstance(out_dtype, (list, tuple)) else [out_dtype] * n_out
    o_spec = pl.BlockSpec((tm, tn), lambda i, j, k: (i, j))
    return _mm(name, a, b, dims=NN, grid=(m // tm, npc * r, kd // tk),
               a_spec=pl.BlockSpec((tm, tk), lambda i, j, k: (i, k)),
               b_spec=pl.BlockSpec((None, tk, tn), lambda i, j, k: (j // r, k, j % r)),
               outs=[((m, npc * ns), dt, o_spec) for dt in dts], epilogue=epilogue, acc_shape=(tm, tn))


def _mm_nt(name, a, b, out_dtype=F32, tm=1024, tn=1024, tk=2048, epilogue=None, extra=(), n_rows=None, deps=(),
           side=None, b2=None):
    m, kd = a.shape
    n = n_rows or b.shape[0]
    tm, tn, tk = _tile(m, tm), _tile(n, tn), _tile(kd, tk)
    o_spec = pl.BlockSpec((tm, tn), lambda i, j, k: (i, j))
    if b2 is not None:
        tk = kd
        b_spec = pl.BlockSpec((tn, kd // 2), lambda i, j, k: (j, 0))
    else:
        b_spec = pl.BlockSpec((tn, tk), lambda i, j, k: (j, k))
    grid = (m // tm, n // tn, kd // tk)
    res = _mm(name, a, b, dims=NT, grid=grid,
              a_spec=pl.BlockSpec((tm, tk), lambda i, j, k: (i, k)), b_spec=b_spec,
              outs=[((m, n), out_dtype, o_spec)], extra=extra, extra_specs=[o_spec] * len(extra),
              epilogue=epilogue, acc_shape=(tm, tn), deps=deps, side=side and side(grid[0] * grid[1] * grid[2]), b2=b2)
    return res if side else res[0]


def _mm_nt_pieces(name, a, b, out_dtype=F32, tm=1024, tn=256, deps=()):
    m = a.shape[0]
    npc, n, ks = b.shape
    tm, tn = _tile(m, tm), _tile(n, tn)
    return _mm(name, a, b, dims=NT, grid=(m // tm, n // tn, 1),
               a_spec=pl.BlockSpec((tm, npc * ks), lambda i, j, k: (i, 0)),
               b_spec=pl.BlockSpec((npc, tn, ks), lambda i, j, k: (0, j, 0)),
               outs=[((m, n), out_dtype, pl.BlockSpec((tm, tn), lambda i, j, k: (i, j)))],
               deps=deps, b_pieces=npc)[0]


def _mm_tn(name, a, b, out_dtype=BF16, tm=1024, tn=1024, pieces=0, deps=(), side=None):
    t, m = a.shape
    n = b.shape[1]
    tm = _tile(m, tm)
    if pieces:
        ns = n // pieces
        tn = _tile(ns, tn)
        r = ns // tn
        out = ((pieces, m, ns), out_dtype, pl.BlockSpec((None, tm, tn), lambda i, j, k: (j // r, i, j % r)))
    else:
        tn = _tile(n, tn)
        out = ((m, n), out_dtype, pl.BlockSpec((tm, tn), lambda i, j, k: (i, j)))
    grid = (m // tm, n // tn, 1)
    res = _mm(name, a, b, dims=TN, grid=grid,
              a_spec=pl.BlockSpec((t, tm), lambda i, j, k: (0, i)),
              b_spec=pl.BlockSpec((t, tn), lambda i, j, k: (0, j)),
              outs=[out], deps=deps, side=side and side(grid[0] * grid[1]))
    return res if side else res[0]


def _rms_f(x, g):
    r = lax.rsqrt(jnp.mean(x * x, axis=-1, keepdims=True) + EPS)
    xh = x * r
    return xh * g, xh, r


def _rms_b(dy, g, xh, r):
    dxh = dy * g
    dx = r * (dxh - xh * jnp.mean(dxh * xh, axis=-1, keepdims=True))
    dg = jnp.sum(dy * xh, axis=0, keepdims=True)
    return dx, dg


def _rowwise(name, fn, row_ins, vec_ins, row_outs, vec_outs, deps=()):
    t = row_ins[0].shape[0]
    tr = min(ROW_TILE, t)
    nr, nv, no = len(row_ins), len(vec_ins), len(row_outs)
    first_out = nr + nv + len(deps)

    def body(*refs):
        ri, vi = refs[:nr], refs[nr:nr + nv]
        ro, vo = refs[first_out:first_out + no], refs[first_out + no:]
        routs, vouts = fn(*[r[...] for r in ri], *[v[...] for v in vi])
        for o, val in zip(ro, routs):
            o[...] = val.astype(o.dtype)
        i = pl.program_id(0)
        for o, val in zip(vo, vouts):
            @pl.when(i == 0)
            def _():
                o[...] = val

            @pl.when(i > 0)
            def _():
                o[...] += val

    res = pl.pallas_call(
        body, name=name,
        out_shape=[jax.ShapeDtypeStruct((t, w), d) for w, d in row_outs]
        + [jax.ShapeDtypeStruct((1, w), F32) for w in vec_outs],
        grid=(t // tr,),
        in_specs=[pl.BlockSpec((tr, a.shape[1]), lambda i: (i, 0)) for a in row_ins]
        + [pl.BlockSpec((1, a.shape[1]), lambda i: (0, 0)) for a in vec_ins] + [ANY_SPEC] * len(deps),
        out_specs=[pl.BlockSpec((tr, w), lambda i: (i, 0)) for w, _ in row_outs]
        + [pl.BlockSpec((1, w), lambda i: (0, 0)) for w in vec_outs],
        compiler_params=_params("arbitrary"),
    )(*row_ins, *vec_ins, *deps)
    return list(res)


def _norm_fwd(x, g, name, deps=()):
    d = x.shape[1]
    return _rowwise(name, lambda x, g: ((_rms_f(x, g)[0],), ()), [x], [g], [(d, BF16)], [], deps=deps)[0]


def _norm_bwd(x, dxn, dres, g, name):
    d = x.shape[1]

    def fn(x, dxn, dres, g):
        _, xh, r = _rms_f(x, g)
        dx, dg = _rms_b(dxn, g, xh, r)
        return (dres + dx,), (dg,)

    return _rowwise(name, fn, [x, dxn, dres], [g], [(d, F32)], [d])


def _block_fwd(h_prev, y, g_y, g_n, name, deps=()):
    d = h_prev.shape[1]

    def fn(h_prev, y, g_y, g_n):
        h = h_prev + _rms_f(y, g_y)[0]
        return (h, _rms_f(h, g_n)[0]), ()

    return _rowwise(name, fn, [h_prev, y], [g_y, g_n], [(d, F32), (d, BF16)], [], deps=deps)


def _block_bwd(h, y, d_up, d_n, g_y, g_n, name, deps=()):
    d = h.shape[1]

    def fn(h, y, d_up, d_n, g_y, g_n):
        _, hh, hr = _rms_f(h, g_n)
        dh_n, dg_n = _rms_b(d_n, g_n, hh, hr)
        dh = d_up + dh_n
        _, yh, yr = _rms_f(y, g_y)
        dy, dg_y = _rms_b(dh, g_y, yh, yr)
        return (dh, dy), (dg_y, dg_n)

    return _rowwise(name, fn, [h, y, d_up, d_n], [g_y, g_n], [(d, F32), (d, BF16)], [d, d], deps=deps)


def _loss_head(h2, z, e, target, g, name):
    d = h2.shape[1]

    def fn(h2, z, e, target, g):
        gate = jax.nn.sigmoid(z)
        y, xh, r = _rms_f(e * gate, g)
        diff = h2 + y - target
        loss = 0.5 * jnp.sum(jnp.mean(diff * diff, axis=-1, keepdims=True), axis=0, keepdims=True)
        dh3 = diff / d
        deg, dg = _rms_b(dh3, g, xh, r)
        dz = deg * e * gate * (1.0 - gate)
        de = deg * gate
        return (dh3, dz, de), (dg, jnp.broadcast_to(loss, (1, LANES)))

    return _rowwise(name, fn, [h2, z, e, target], [g], [(d, F32), (d, BF16), (d, BF16)], [d, LANES])


def _shift_down(x, s, rows):
    return jnp.where(rows >= s, pltpu.roll(x, s, axis=0), 0.0)


def _shift_up(x, s, rows):
    t = x.shape[0]
    return jnp.where(rows < t - s, pltpu.roll(x, t - s, axis=0), 0.0)


def _conv_pre(x, w, rows):
    pre = x * w[QK_CONV - 1:QK_CONV, :]
    for s in range(1, QK_CONV):
        pre = pre + _shift_down(x, s, rows) * w[QK_CONV - 1 - s:QK_CONV - s, :]
    return pre


def _conv_fwd(proj_a, w, start, name):
    t = proj_a.shape[0]
    width = w.shape[1]
    tc = _tile(width, 256)
    off = start // tc

    def body(x_ref, w_ref, y_ref):
        x = x_ref[...]
        rows = lax.broadcasted_iota(jnp.int32, x.shape, 0)
        pre = _conv_pre(x, w_ref[...], rows)
        y_ref[...] = pre * jax.nn.sigmoid(pre)

    return pl.pallas_call(
        body, name=name,
        out_shape=jax.ShapeDtypeStruct((t, width), F32),
        grid=(width // tc,),
        in_specs=[pl.BlockSpec((t, tc), lambda j: (0, off + j)), pl.BlockSpec((QK_CONV, tc), lambda j: (0, j))],
        out_specs=pl.BlockSpec((t, tc), lambda j: (0, j)),
        compiler_params=_params("parallel"),
    )(proj_a, w)


def _conv_bwd(proj_a, w, dy, start, name):
    t = proj_a.shape[0]
    width = w.shape[1]
    tc = _tile(width, 256)
    off = start // tc

    def body(x_ref, w_ref, dy_ref, dx_ref, dw_ref):
        x, w = x_ref[...], w_ref[...]
        rows = lax.broadcasted_iota(jnp.int32, x.shape, 0)
        pre = _conv_pre(x, w, rows)
        sig = jax.nn.sigmoid(pre)
        dpre = dy_ref[...] * (sig * (1.0 + pre * (1.0 - sig)))
        dx = dpre * w[QK_CONV - 1:QK_CONV, :]
        dws = [jnp.sum(dpre * x, axis=0, keepdims=True)]
        for s in range(1, QK_CONV):
            dx = dx + _shift_up(dpre, s, rows) * w[QK_CONV - 1 - s:QK_CONV - s, :]
            dws.append(jnp.sum(dpre * _shift_down(x, s, rows), axis=0, keepdims=True))
        dx_ref[...] = dx.astype(dx_ref.dtype)
        for s in range(QK_CONV):
            dw_ref[QK_CONV - 1 - s:QK_CONV - s, :] = dws[s]

    return pl.pallas_call(
        body, name=name,
        out_shape=[jax.ShapeDtypeStruct((t, width), BF16), jax.ShapeDtypeStruct((QK_CONV, width), F32)],
        grid=(width // tc,),
        in_specs=[pl.BlockSpec((t, tc), lambda j: (0, off + j)), pl.BlockSpec((QK_CONV, tc), lambda j: (0, j)),
                  pl.BlockSpec((t, tc), lambda j: (0, j))],
        out_specs=[pl.BlockSpec((t, tc), lambda j: (0, j)), pl.BlockSpec((QK_CONV, tc), lambda j: (0, j))],
        compiler_params=_params("parallel"),
    )(proj_a, w, dy)


def _dot(a, b, dims):
    return lax.dot_general(a.astype(BF16), b.astype(BF16), (dims, ((), ())), preferred_element_type=F32)


def _dot_split(ones, x):
    hi = x.astype(BF16)
    rest = x - hi.astype(F32)
    mid = rest.astype(BF16)
    lo = (rest - mid.astype(F32)).astype(BF16)
    dot = lambda part: lax.dot_general(ones, part, (NN, ((), ())), preferred_element_type=F32)
    return dot(hi) + dot(mid) + dot(lo)


def _chunk_iotas():
    rows = lax.broadcasted_iota(jnp.int32, (CHUNK, CHUNK), 0)
    cols = lax.broadcasted_iota(jnp.int32, (CHUNK, CHUNK), 1)
    return rows, cols


def _gates_fwd(proj_b, bias, col_block, heads, name):
    t = proj_b.shape[0]

    def body(x_ref, b_ref, gb_ref, gbt_ref, cols_ref):
        rows, cols = _chunk_iotas()
        tri = (cols <= rows).astype(BF16)
        for ci in range(t // CHUNK):
            sl = slice(ci * CHUNK, (ci + 1) * CHUNK)
            pre = x_ref[sl, :] + b_ref[...]
            log_f = jnp.minimum(pre, 0.0) - jnp.log(1.0 + jnp.exp(-jnp.abs(pre)))
            log_f = jnp.where((cols >= heads) & (cols < 2 * heads), log_f, 0.0)
            tile = jnp.where(cols < heads, pre, _dot_split(tri, log_f))
            gb_ref[sl, :] = tile
            gbt_ref[:, sl] = tile.T
            for h in range(2 * heads):
                cols_ref[h, sl, :] = jnp.broadcast_to(tile[:, h:h + 1], (CHUNK, LANES))

    return pl.pallas_call(
        body, name=name,
        out_shape=[jax.ShapeDtypeStruct((t, LANES), F32), jax.ShapeDtypeStruct((LANES, t), F32),
                   jax.ShapeDtypeStruct((2 * heads, t, LANES), F32)],
        grid=(1,),
        in_specs=[pl.BlockSpec((t, LANES), lambda i: (0, col_block)), pl.BlockSpec((1, LANES), lambda i: (0, 0))],
        out_specs=[pl.BlockSpec((t, LANES), lambda i: (0, 0)), pl.BlockSpec((LANES, t), lambda i: (0, 0)),
                   pl.BlockSpec((2 * heads, t, LANES), lambda i: (0, 0, 0))],
        compiler_params=_params("arbitrary"),
    )(proj_b, bias)


def _gates_bwd(proj_b, bias, dcol, drow, col_block, heads, name):
    t = proj_b.shape[0]

    def body(x_ref, b_ref, dc_ref, dr_ref, dx_ref, db_ref):
        rows, cols = _chunk_iotas()
        later = (cols >= rows).astype(BF16)
        total = jnp.zeros((1, LANES), F32)
        for ci in range(t // CHUNK):
            sl = slice(ci * CHUNK, (ci + 1) * CHUNK)
            by_row = dr_ref[:, sl].T
            d = dc_ref[sl, :] + jnp.where(cols < heads, by_row,
                                          jnp.where(cols < 2 * heads, -pltpu.roll(by_row, heads, axis=1), 0.0))
            d_log_f = _dot_split(later, jnp.where(cols >= heads, d, 0.0))
            pre = x_ref[sl, :] + b_ref[...]
            dx = jnp.where(cols < heads, d, jnp.where(cols < 2 * heads, d_log_f * jax.nn.sigmoid(-pre), 0.0))
            dx_ref[sl, :] = dx.astype(dx_ref.dtype)
            total = total + jnp.sum(dx, axis=0, keepdims=True)
        db_ref[...] = total

    return pl.pallas_call(
        body, name=name,
        out_shape=[jax.ShapeDtypeStruct((t, LANES), BF16), jax.ShapeDtypeStruct((1, LANES), F32)],
        grid=(1,),
        in_specs=[pl.BlockSpec((t, LANES), lambda i: (0, col_block)), pl.BlockSpec((1, LANES), lambda i: (0, 0)),
                  pl.BlockSpec((t, LANES), lambda i: (0, 0)), pl.BlockSpec((LANES, t), lambda i: (0, 0))],
        out_specs=[pl.BlockSpec((t, LANES), lambda i: (0, 0)), pl.BlockSpec((1, LANES), lambda i: (0, 0))],
        compiler_params=_params("arbitrary"),
    )(proj_b, bias, dcol, drow)


def _gate_views(li_ref, b_ref, gbt_ref, s, head, heads, sl):
    li_c, b_c = li_ref[s, sl, :], b_ref[s, sl, :]
    li_r = jnp.broadcast_to(gbt_ref[head, :, sl], (CHUNK, CHUNK))
    b_r = jnp.broadcast_to(gbt_ref[heads + head, :, sl], (CHUNK, CHUNK))
    return li_c, b_c, li_r, b_r


def _chunk_state(kt, k, v, li_c, b_c, ct, nrow, m_prev):
    b_tot = b_c[CHUNK - 1:CHUNK, 0:1]
    a = b_tot - b_c[:, 0:1] + li_c[:, 0:1]
    m_new = jnp.maximum(b_tot + m_prev, jnp.max(a, axis=0, keepdims=True))
    dec = jnp.exp(b_tot + m_prev - m_new)
    w = jnp.exp(a - m_new)
    wv = w * v
    ct_new = dec * ct + _dot(kt, wv, NN)
    n_new = dec * nrow + jnp.sum(w * k, axis=0, keepdims=True)
    return ct_new, n_new, m_new, dec, w, wv


def _chunk_scores(q, k, b_c, li_r, b_r, m_prev):
    rows, cols = _chunk_iotas()
    dlog = jnp.where(cols <= rows, b_c - b_r + li_r, -jnp.inf)
    inter = b_c[:, 0:1] + m_prev
    mt = jnp.maximum(inter, jnp.max(dlog, axis=-1, keepdims=True))
    dw = jnp.exp(dlog - mt)
    iw = jnp.exp(inter - mt)
    a = _dot(q, k, NT)
    return dw, iw, mt, a, a * dw


def _to_row(column):
    rows, cols = _chunk_iotas()
    return jnp.sum(jnp.where(rows == cols, jnp.broadcast_to(column, (CHUNK, CHUNK)), 0.0), axis=0, keepdims=True)


def _head_out(h, o, g):
    cell, hh, r = _rms_f(h, g)
    sig = jax.nn.sigmoid(o)
    return sig * cell, (cell, hh, r, sig)


HEADS_PER_STEP = 2


def _mlstm_fwd(qc, kc, kct, proj_a, gcols, gbt, g_mlstm, heads, d_model, name, dep):
    t = qc.shape[0]
    dh = g_mlstm.shape[1] // heads
    nc = t // CHUNK
    scale = dh ** -0.5
    hp = HEADS_PER_STEP

    def body(q_ref, k_ref, kt_ref, v_ref, o_ref, li_ref, b_ref, gbt_ref, g_ref, dep_ref, out_ref, ct_out, nm_out,
             ct_ref, n_ref, m_ref):
        del dep_ref
        step = pl.program_id(0)
        ct_ref[...] = jnp.zeros_like(ct_ref)
        n_ref[...] = jnp.zeros_like(n_ref)
        m_ref[...] = jnp.zeros_like(m_ref)
        tile_row = lax.broadcasted_iota(jnp.int32, (8, LANES), 0)

        def chunk(ci, carry):
            sl = pl.ds(pl.multiple_of(ci * CHUNK, CHUNK), CHUNK)
            for s in range(hp):
                cs = slice(s * dh, (s + 1) * dh)
                q, k, v, kt = q_ref[sl, cs], k_ref[sl, cs] * scale, v_ref[sl, cs], kt_ref[cs, sl]
                li_c, b_c, li_r, b_r = _gate_views(li_ref, b_ref, gbt_ref, s, step * hp + s, heads, sl)
                ct, nrow, m_prev = ct_ref[s], n_ref[s, 0:1, :], m_ref[s, 0:1, 0:1]
                ct_out[s, ci] = ct
                nm_out[s, ci] = jnp.where(tile_row == 0, nrow, jnp.where(tile_row == 1, m_prev, 0.0))
                dw, iw, mt, _, sm = _chunk_scores(q, k, b_c, li_r, b_r, m_prev)
                iq = iw * q
                num = _dot(jnp.concatenate([iq, sm], axis=1), jnp.concatenate([ct, v], axis=0), NN)
                den = jnp.sum(iq * nrow + sm, axis=-1, keepdims=True)
                h = num / jnp.maximum(jnp.abs(den), jnp.exp(-mt))
                y, _ = _head_out(h, o_ref[sl, cs], g_ref[:, cs])
                out_ref[sl, cs] = y.astype(out_ref.dtype)
                ct_new, n_new, m_new, _, _, _ = _chunk_state(kt, k, v, li_c, b_c, ct, nrow, m_prev)
                ct_ref[s] = ct_new
                n_ref[s, 0:1, :] = n_new
                m_ref[s, 0:1, 0:1] = m_new
            return carry

        lax.fori_loop(0, nc, chunk, 0, unroll=2)

    col = lambda off: pl.BlockSpec((t, hp * dh), lambda h: (0, off // hp + h))
    slab = lambda off: pl.BlockSpec((hp, t, LANES), lambda h: (off // hp + h, 0, 0))
    return pl.pallas_call(
        body, name=name,
        out_shape=[jax.ShapeDtypeStruct((t, d_model), BF16), jax.ShapeDtypeStruct((heads, nc, dh, dh), F32),
                   jax.ShapeDtypeStruct((heads, nc, 8, LANES), F32)],
        grid=(heads // hp,),
        in_specs=[col(0), col(0), pl.BlockSpec((hp * dh, t), lambda h: (h, 0)), col(2 * heads), col(3 * heads),
                  slab(0), slab(heads), pl.BlockSpec((2 * heads, 1, t), lambda h: (0, 0, 0)),
                  pl.BlockSpec((1, hp * dh), lambda h: (0, h)), ANY_SPEC],
        out_specs=[pl.BlockSpec((t, hp * dh), lambda h: (0, h)),
                   pl.BlockSpec((hp, nc, dh, dh), lambda h: (h, 0, 0, 0)),
                   pl.BlockSpec((hp, nc, 8, LANES), lambda h: (h, 0, 0, 0))],
        scratch_shapes=[pltpu.VMEM((hp, dh, dh), F32), pltpu.VMEM((hp, 8, dh), F32), pltpu.VMEM((hp, 8, LANES), F32)],
        compiler_params=_params("arbitrary"),
    )(qc, kc, kct, proj_a, proj_a, gcols, gcols, gbt, g_mlstm, dep)


def _mlstm_bwd(qc, kc, qct, kct, proj_a, gcols, gbt, g_mlstm, states, dhcat, heads, name, dep):
    t = qc.shape[0]
    wm = g_mlstm.shape[1]
    dh = wm // heads
    nc = t // CHUNK
    scale = dh ** -0.5

    def body(q_ref, k_ref, qt_ref, kt_ref, v_ref, o_ref, li_ref, b_ref, gbt_ref, g_ref, ct_all, nm_all, dy_ref, dep_ref,
             dq_ref, dk_ref, dv_ref, do_ref, dcol_ref, drow_ref, dg_ref, dct_ref, dn_ref):
        del dep_ref
        head = pl.program_id(0)
        g = g_ref[...]

        @pl.when(head == 0)
        def _():
            dcol_ref[...] = jnp.zeros_like(dcol_ref)
            drow_ref[...] = jnp.zeros_like(drow_ref)

        def load(ci):
            sl = pl.ds(pl.multiple_of(ci * CHUNK, CHUNK), CHUNK)
            return (sl, q_ref[sl, :], k_ref[sl, :] * scale, v_ref[sl, :], kt_ref[:, sl]) \
                + _gate_views(li_ref, b_ref, gbt_ref, 0, head, heads, sl)

        dct_ref[...] = jnp.zeros_like(dct_ref)
        dn_ref[...] = jnp.zeros_like(dn_ref)
        dg_ref[...] = jnp.zeros_like(dg_ref)
        rows, cols = _chunk_iotas()
        last = lax.broadcasted_iota(jnp.int32, (CHUNK, 1), 0) == CHUNK - 1
        lane = lax.broadcasted_iota(jnp.int32, (CHUNK, LANES), 1)

        def bwd(step, carry):
            ci = nc - 1 - step
            sl, q, k, v, kt, li_c, b_c, li_r, b_r = load(ci)
            qt = qt_ref[:, sl]
            ct, nrow, m_prev = ct_all[ci], nm_all[ci, 0:1, :], nm_all[ci, 1:2, 0:1]
            dw, iw, mt, a, sm = _chunk_scores(q, k, b_c, li_r, b_r, m_prev)
            qc_ = _dot(q, ct, NN)
            qn = jnp.sum(q * nrow, axis=-1, keepdims=True)
            num = iw * qc_ + _dot(sm, v, NN)
            den = iw * qn + jnp.sum(sm, axis=-1, keepdims=True)
            floor = jnp.exp(-mt)
            dn = jnp.maximum(jnp.abs(den), floor)
            h = num / dn
            _, (cell, hh, r, sig) = _head_out(h, o_ref[sl, :], g)
            dy = dy_ref[sl, :]
            dcell = dy * sig
            do_ref[sl, :] = (dy * cell * sig * (1.0 - sig)).astype(do_ref.dtype)
            dh_, dg = _rms_b(dcell, g, hh, r)
            dg_ref[...] += dg
            dnum = dh_ / dn
            ddn = -jnp.sum(dh_ * h, axis=-1, keepdims=True) / dn
            dden = jnp.where(jnp.abs(den) >= floor, ddn * jnp.sign(den), 0.0)
            idn = iw * dnum
            idd = iw * dden
            dw_t = jnp.exp(jnp.where(cols >= rows, b_r - b_c + li_c, -jnp.inf) - _to_row(mt))
            sm_t = _dot(k, q, NT) * dw_t
            dsm = _dot(dnum, v, NT) + dden
            da_ = dsm * dw
            da_t = (_dot(v, dnum, NT) + _to_row(dden)) * dw_t
            e = da_ * a
            dq = _dot(idn, ct, NT) + _dot(da_, k, NN) + idd * nrow
            dk = _dot(da_t, q, NN)
            dv = _dot(sm_t, dnum, NN)
            dct = _dot(qt, idn, NN)
            dnr = jnp.sum(q * idd, axis=0, keepdims=True)
            db = jnp.sum(e + idn * qc_, axis=-1, keepdims=True) + idd * qn
            _, _, _, dec, w, wv = _chunk_state(kt, k, v, li_c, b_c, ct, nrow, m_prev)
            dct_new, dn_new = dct_ref[...], dn_ref[0:1, :]
            ddec = jnp.sum(jnp.sum(dct_new * ct, axis=-1, keepdims=True), axis=0, keepdims=True) \
                + jnp.sum(dn_new * nrow, axis=-1, keepdims=True)
            dk = dk + _dot(wv, dct_new, NT) + w * dn_new
            dwv = _dot(k, dct_new, NN)
            dv = dv + w * dwv
            dwt = jnp.sum(dwv * v + k * dn_new, axis=-1, keepdims=True)
            da = dwt * w
            dbtot = jnp.sum(da, axis=0, keepdims=True) + ddec * dec
            db = db - da + jnp.where(last, dbtot, 0.0)
            dct_ref[...] = dec * dct_new + dct
            dn_ref[0:1, :] = dec * dn_new + dnr
            dq_ref[sl, :] = dq
            dk_ref[sl, :] = dk * scale
            dv_ref[sl, :] = dv.astype(dv_ref.dtype)
            dcol_ref[sl, :] += jnp.where(lane == head, da, 0.0) + jnp.where(lane == heads + head, db, 0.0)
            drow_ref[head, :, sl] = jnp.sum(e, axis=0, keepdims=True)
            return carry

        lax.fori_loop(0, nc, bwd, 0, unroll=8)

    col = lambda off: pl.BlockSpec((t, dh), lambda h: (0, off + h))
    rowb = pl.BlockSpec((dh, t), lambda h: (h, 0))
    slab = lambda off: pl.BlockSpec((1, t, LANES), lambda h: (off + h, 0, 0))
    vec = pl.BlockSpec((1, dh), lambda h: (0, h))
    return pl.pallas_call(
        body, name=name,
        out_shape=[jax.ShapeDtypeStruct((t, wm), F32)] * 2 + [jax.ShapeDtypeStruct((t, wm), BF16)] * 2
        + [jax.ShapeDtypeStruct((t, LANES), F32), jax.ShapeDtypeStruct((heads, 1, t), F32),
           jax.ShapeDtypeStruct((1, wm), F32)],
        grid=(heads,),
        in_specs=[col(0), col(0), rowb, rowb, col(2 * heads), col(3 * heads), slab(0), slab(heads),
                  pl.BlockSpec((2 * heads, 1, t), lambda h: (0, 0, 0)), vec,
                  pl.BlockSpec((None, nc, dh, dh), lambda h: (h, 0, 0, 0)),
                  pl.BlockSpec((None, nc, 8, LANES), lambda h: (h, 0, 0, 0)), col(0), ANY_SPEC],
        out_specs=[col(0), col(0), col(0), col(0), pl.BlockSpec((t, LANES), lambda h: (0, 0)),
                   pl.BlockSpec((heads, 1, t), lambda h: (0, 0, 0)), vec],
        scratch_shapes=[pltpu.VMEM((dh, dh), F32), pltpu.VMEM((8, dh), F32)],
        compiler_params=_params("arbitrary"),
    )(qc, kc, qct, kct, proj_a, proj_a, gcols, gcols, gbt, g_mlstm, *states, dhcat, dep)


def _pool_window(gi):
    return jnp.where(gi == 0, 2.0, jnp.where(gi == 1, 4.0, jnp.where(gi == 2, 8.0, 16.0))).astype(F32)


def _pool_select(gi, levels):
    return jnp.where(gi == 0, levels[0], jnp.where(gi == 1, levels[1], jnp.where(gi == 2, levels[2], levels[3])))


def _pooled(u, gi, rows):
    s, levels = u, []
    for lvl in range(POOL_GROUPS):
        s = s + _shift_down(s, 1 << lvl, rows)
        levels.append(s)
    cnt = jnp.minimum((rows + 1).astype(F32), _pool_window(gi))
    return _pool_select(gi, levels) / cnt - u, cnt


def _pool_fwd(proj_b, w_pool, scale, hcat, name):
    t, d_model = hcat.shape
    g, cg, _ = w_pool.shape
    off = (d_model - g * cg) // cg

    def body(u_ref, w_ref, s_ref, hcat_ref, out_ref):
        del hcat_ref
        gi = pl.program_id(0)
        u = u_ref[...]
        rows = lax.broadcasted_iota(jnp.int32, u.shape, 0)
        pooled, _ = _pooled(u, gi, rows)
        out_ref[...] = (_dot(pooled, w_ref[...], NN) * s_ref[...]).astype(out_ref.dtype)

    return pl.pallas_call(
        body, name=name,
        out_shape=jax.ShapeDtypeStruct((t, d_model), BF16),
        grid=(g,),
        in_specs=[pl.BlockSpec((t, cg), lambda gi: (0, gi)), pl.BlockSpec((None, cg, cg), lambda gi: (gi, 0, 0)),
                  pl.BlockSpec((1, cg), lambda gi: (0, gi)), HBM_SPEC],
        out_specs=pl.BlockSpec((t, cg), lambda gi: (0, off + gi)),
        input_output_aliases={3: 0},
        compiler_params=_params("arbitrary"),
    )(proj_b, w_pool, scale, hcat)


def _pool_bwd(proj_b, w_pool, scale, dhcat, name):
    t, d_model = dhcat.shape
    g, cg, _ = w_pool.shape
    off = (d_model - g * cg) // cg

    def body(u_ref, w_ref, s_ref, dy_ref, du_ref, dw_ref, ds_ref):
        gi = pl.program_id(0)
        u, w, dy = u_ref[...], w_ref[...], dy_ref[...]
        rows = lax.broadcasted_iota(jnp.int32, u.shape, 0)
        pooled, cnt = _pooled(u, gi, rows)
        mixed = _dot(pooled, w, NN)
        ds_ref[...] = jnp.sum(dy * mixed, axis=0, keepdims=True)
        dmixed = dy * s_ref[...]
        dw_ref[...] = _dot(pooled, dmixed, TN).astype(dw_ref.dtype)
        dpooled = _dot(dmixed, w, NT)
        s, levels = dpooled / cnt, []
        for lvl in range(POOL_GROUPS):
            s = s + _shift_up(s, 1 << lvl, rows)
            levels.append(s)
        du_ref[...] = (_pool_select(gi, levels) - dpooled).astype(du_ref.dtype)

    return pl.pallas_call(
        body, name=name,
        out_shape=[jax.ShapeDtypeStruct((t, g * cg), BF16), jax.ShapeDtypeStruct((g, cg, cg), BF16),
                   jax.ShapeDtypeStruct((1, g * cg), F32)],
        grid=(g,),
        in_specs=[pl.BlockSpec((t, cg), lambda gi: (0, gi)), pl.BlockSpec((None, cg, cg), lambda gi: (gi, 0, 0)),
                  pl.BlockSpec((1, cg), lambda gi: (0, gi)), pl.BlockSpec((t, cg), lambda gi: (0, off + gi))],
        out_specs=[pl.BlockSpec((t, cg), lambda gi: (0, gi)), pl.BlockSpec((None, cg, cg), lambda gi: (gi, 0, 0)),
                   pl.BlockSpec((1, cg), lambda gi: (0, gi))],
        compiler_params=_params("arbitrary"),
    )(proj_b, w_pool, scale, dhcat)


def _pad_cols(a, width):
    return jnp.pad(a, ((0, 0), (0, width - a.shape[1])))


def kernel(x, p, w_in, b_gates, w_qk_conv, g_mlstm, w_pool, pool_scale, w_out, g_mix_pre, g_mix_post, w_ff1, w_ff2, g_ff_pre, g_ff_post, w_ple_proj, w_ple_gate, g_ple_gate, g_ple_post, loss_target, m_w_in, m_b_gates, m_w_qk_conv, m_g_mlstm, m_w_pool, m_pool_scale, m_w_out, m_g_mix_pre, m_g_mix_post, m_w_ff1, m_w_ff2, m_g_ff_pre, m_g_ff_post, m_w_ple_proj, m_w_ple_gate, m_g_ple_gate, m_g_ple_post, v_w_in, v_b_gates, v_w_qk_conv, v_g_mlstm, v_w_pool, v_pool_scale, v_w_out, v_g_mix_pre, v_g_mix_post, v_w_ff1, v_w_ff2, v_g_ff_pre, v_g_ff_post, v_w_ple_proj, v_w_ple_gate, v_g_ple_gate, v_g_ple_post):
    weights = dict(w_in=w_in, b_gates=b_gates, w_qk_conv=w_qk_conv, g_mlstm=g_mlstm, w_pool=w_pool,
                   pool_scale=pool_scale, w_out=w_out, g_mix_pre=g_mix_pre, g_mix_post=g_mix_post, w_ff1=w_ff1,
                   w_ff2=w_ff2, g_ff_pre=g_ff_pre, g_ff_post=g_ff_post, w_ple_proj=w_ple_proj,
                   w_ple_gate=w_ple_gate, g_ple_gate=g_ple_gate, g_ple_post=g_ple_post)
    mom1 = dict(w_in=m_w_in, b_gates=m_b_gates, w_qk_conv=m_w_qk_conv, g_mlstm=m_g_mlstm, w_pool=m_w_pool,
                pool_scale=m_pool_scale, w_out=m_w_out, g_mix_pre=m_g_mix_pre, g_mix_post=m_g_mix_post,
                w_ff1=m_w_ff1, w_ff2=m_w_ff2, g_ff_pre=m_g_ff_pre, g_ff_post=m_g_ff_post,
                w_ple_proj=m_w_ple_proj, w_ple_gate=m_w_ple_gate, g_ple_gate=m_g_ple_gate, g_ple_post=m_g_ple_post)
    mom2 = dict(w_in=v_w_in, b_gates=v_b_gates, w_qk_conv=v_w_qk_conv, g_mlstm=v_g_mlstm, w_pool=v_w_pool,
                pool_scale=v_pool_scale, w_out=v_w_out, g_mix_pre=v_g_mix_pre, g_mix_post=v_g_mix_post,
                w_ff1=v_w_ff1, w_ff2=v_w_ff2, g_ff_pre=v_g_ff_pre, g_ff_post=v_g_ff_post,
                w_ple_proj=v_w_ple_proj, w_ple_gate=v_w_ple_gate, g_ple_gate=v_g_ple_gate, g_ple_post=v_g_ple_post)
    order = list(weights)

    t, d = x.shape[1], x.shape[2]
    heads = MLSTM_HEADS
    wm = g_mlstm.shape[1]
    pw = pool_scale.shape[1]
    cg = pw // POOL_GROUPS
    in_cols = 4 * wm + 2 * heads + pw
    in_shard = w_in.shape[2]
    xs, ps, tgt = x[0], p[0, 0], loss_target[0]

    mx, my, mc = lax.axis_index("x"), lax.axis_index("y"), lax.axis_index("c")
    core = jnp.reshape(mc, (1,)).astype(jnp.int32)
    chip = jnp.reshape(2 * mx + my, (1,)).astype(jnp.int32)

    first = [w_in[0].T.astype(BF16), w_qk_conv[0], w_pool[0].reshape(POOL_GROUPS * w_pool.shape[2], cg).astype(BF16)]
    started, token = _gather_start([[w_ple_proj[0].astype(BF16)], first], "gather_start_in")
    anchor = token[0, 0]
    later = lambda w: (w[0] + anchor).astype(BF16)
    ff2_shard = later(w_ff2)
    rest, token = _gather_start([[later(w_out)], [later(w_ff1)], [ff2_shard[:, :d // 2]], [ff2_shard[:, d // 2:]],
                                 [later(w_ple_gate)]], "gather_start_rest")
    started = dict(zip(["ple_proj", "in", "out", "ff1", "ff2_lo", "ff2_hi", "ple_gate"], started + rest))
    bias = _pad_cols(b_gates, LANES)
    small = ["b_gates", "g_mlstm", "pool_scale", "g_mix_pre", "g_mix_post", "g_ff_pre", "g_ff_post", "g_ple_gate",
             "g_ple_post"]
    pack = lambda src: _pad_rows16(jnp.concatenate([_pad_cols(src[k] + anchor, d) for k in small], axis=0))
    small_state = [pack(weights), pack(mom1), pack(mom2)]
    in_state = [a[0].T + anchor for a in (w_in, m_w_in, v_w_in)]

    xn = _norm_fwd(xs, g_mix_pre, "norm_mix_pre", deps=[token])
    (g_pproj,) = _forward_pair(_gather_wait(started["ple_proj"], xn, "gather_wait_ple_proj"), "gather_pair_ple_proj")
    e = _mm_nn_pieces("ple_proj", ps, g_pproj)[0]
    g_in, g_conv, g_pool = _forward_pair(
        _gather_wait(started["in"], [e] + small_state + in_state, "gather_wait_in"), "gather_pair_in")
    in_t_a = g_in.reshape(in_cols, d)
    in_t_b = jnp.concatenate([in_t_a[4 * wm + 2 * heads:], in_t_a[4 * wm:4 * wm + 2 * heads],
                              jnp.zeros((LANES - 2 * heads, d), BF16)], axis=0)
    conv_w = g_conv.transpose(1, 0, 2).reshape(QK_CONV, 2 * wm)
    pool_w = g_pool.reshape(N_DEV, POOL_GROUPS, cg // N_DEV, cg).transpose(1, 0, 2, 3).reshape(POOL_GROUPS, cg, cg)
    proj_a = _mm_nt("proj_qkvo", xn, in_t_a, n_rows=4 * wm)
    proj_b = _mm_nt("proj_pool_gates", xn, in_t_b)
    qc = _conv_fwd(proj_a, conv_w[:, :wm], 0, "q_conv")
    kc = _conv_fwd(proj_a, conv_w[:, wm:], wm, "k_conv")
    gb, gbt, gcols = _gates_fwd(proj_b, bias, pw // LANES, heads, "gates")
    gbt = gbt[:2 * heads].reshape(2 * heads, 1, t)
    qct, kct = qc.T.astype(BF16), (kc * (wm // heads) ** -0.5).T.astype(BF16)
    fwd_out = _forward_start(_gather_wait(started["out"], gb, "gather_wait_out"), "gather_pair_start_out")
    hcat, *mlstm_states = _mlstm_fwd(qc, kc, kct, proj_a, gcols, gbt, g_mlstm, heads, d, "mlstm", fwd_out["token"])
    hcat = _pool_fwd(proj_b, pool_w, pool_scale, hcat, "pool")
    (g_out,) = _forward_wait(fwd_out, hcat, "gather_pair_wait_out")
    out_w = g_out.reshape(d, d)
    mix = _mm_nn("mix_out", hcat, out_w)[0]
    fwd_ff1 = _forward_start(_gather_wait(started["ff1"], mix, "gather_wait_ff1"), "gather_pair_start_ff1")
    h1, hn = _block_fwd(xs, mix, g_mix_post, g_ff_pre, "residual_mix", deps=[fwd_ff1["token"]])
    (g_ff1,) = _forward_wait(fwd_ff1, hn, "gather_pair_wait_ff1")
    relu_a, act = _mm_nn_pieces("ff1", hn, g_ff1, out_dtype=[BF16, BF16],
                                epilogue=lambda acc: (jnp.maximum(acc, 0.0), jnp.square(jnp.maximum(acc, 0.0))))
    (g_ff2_lo,) = _forward_pair(_gather_wait(started["ff2_lo"], act, "gather_wait_ff2_lo"), "gather_pair_ff2_lo")
    ff2_lo = g_ff2_lo.reshape(-1, d // 2)
    ff = _mm_nn("ff2_lo", act, ff2_lo, tn=256, tk=4 * d, out_cols=d)[0]
    (g_ff2_hi,) = _forward_pair(_gather_wait(started["ff2_hi"], ff, "gather_wait_ff2_hi"), "gather_pair_ff2_hi")
    ff2_hi = g_ff2_hi.reshape(-1, d // 2)
    ff = _mm_nn("ff2_hi", act, ff2_hi, tn=256, tk=4 * d, out_cols=d, out_at=d // 2, into=ff)[0]
    fwd_ple = _forward_start(_gather_wait(started["ple_gate"], ff, "gather_wait_ple"), "gather_pair_start_ple")
    h2, hg = _block_fwd(h1, ff, g_ff_post, g_ple_gate, "residual_ff", deps=[fwd_ple["token"]])
    (g_pgate,) = _forward_wait(fwd_ple, hg, "gather_pair_wait_ple")
    pgate_w = g_pgate.reshape(d, d)
    z = _mm_nn("ple_gate", hg, pgate_w)[0]
    dh3, dz, de, dg_ple_post, loss_part = _loss_head(h2, z, e, tgt, g_ple_post, "loss_head")
    loss = lax.psum(loss_part[0, 0], MESH_AXES)

    out = {}

    def state(k):
        if k == "w_in":
            return in_state
        return [a.reshape(-1, a.shape[-1]) for a in (weights[k], mom1[k], mom2[k])]

    def pair_begin(pieces, tag):
        keys = list(pieces)
        return keys, _pair_start([pieces[k] for k in keys], "reduce_pair_start_" + tag)

    def chip_begin(group, after, tag):
        keys, begun = group
        mine, got = _pair_wait(begun, after, "reduce_pair_wait_" + tag)
        sums = [_pair_sum(p_, g, core, "pair_sum_" + k) for k, p_, g in zip(keys, mine, got)]
        return keys, _chip_start(sums, "reduce_start_" + tag)

    def reduce_finish(group, after, tag):
        keys, begun = group
        sums, got = _chip_wait(begun, after, "reduce_wait_" + tag)
        for k, s, g in zip(keys, sums, got):
            res = _adamw_shard(*state(k), s, g, chip, "adamw_" + k)
            out[k] = [r.T[None] if k == "w_in" else r.reshape(weights[k].shape) for r in res]

    token_of = lambda group: group[1]["token"]
    dhg = _mm_nt("d_ple_gate_in", dz, pgate_w)
    gw_pgate = _mm_tn("gw_ple_gate", hg, dz)
    gw_pproj = _mm_tn("gw_ple_proj", ps, de, pieces=N_DEV)
    dh2, dff, dg_ff_post, dg_ple_gate = _block_bwd(h2, ff, dh3, dhg, g_ff_post, g_ple_gate, "d_residual_ff")
    da = _mm_nt("d_ff2_in", dff, ff2_lo, b2=ff2_hi, out_dtype=BF16, extra=[relu_a],
                epilogue=lambda acc, r: (acc * (2.0 * r.astype(F32)),))
    gw_ff2 = _mm_tn("gw_ff2", act, dff)
    ff2_keys = ["w_ff2", "w_ple_gate", "w_ple_proj"]
    pair_ff2 = pair_begin({"w_ff2": gw_ff2.reshape(N_DEV, -1, d),
                           "w_ple_gate": gw_pgate.reshape(N_DEV, d // N_DEV, d), "w_ple_proj": gw_pproj}, "ff2")
    dhn = _mm_nt_pieces("d_ff1_in", da, g_ff1, deps=[token_of(pair_ff2)])
    place = jnp.concatenate([core, chip])
    mine_ff2, got_ff2 = _pair_wait(pair_ff2[1], dhn, "reduce_pair_wait_ff2")
    gw_ff1, sums_ff2 = _mm_tn("gw_ff1", hn, da, pieces=N_DEV, side=_side_pair_sum(mine_ff2[0], got_ff2[0], place))
    sums_ple = [_pair_sum(p_, g, core, "pair_sum_" + k) for k, p_, g in zip(ff2_keys[1:], mine_ff2[1:], got_ff2[1:])]
    red_ff2 = (ff2_keys, _chip_start([sums_ff2] + sums_ple, "reduce_start_ff2"))
    pair_ff1 = pair_begin({"w_ff1": gw_ff1}, "ff1")
    dh1, dmix, dg_mix_post, dg_ff_pre = _block_bwd(h1, mix, dh2, dhn, g_mix_post, g_ff_pre, "d_residual_mix",
                                                   deps=[token_of(pair_ff1), token_of(red_ff2)])
    mine_ff1, got_ff1 = _pair_wait(pair_ff1[1], dmix, "reduce_pair_wait_ff1")
    dhcat, sums_ff1 = _mm_nt("d_mix_in", dmix, out_w, tm=t // 2, tn=d // 2,
                             side=_side_pair_sum(mine_ff1[0], got_ff1[0], place))
    red_ff1 = (["w_ff1"], _chip_start([sums_ff1], "reduce_start_ff1"))
    gw_out = _mm_tn("gw_out", hcat, dmix, deps=[token_of(red_ff1)])
    du, gw_pool, dg_pool_scale = _pool_bwd(proj_b, pool_w, pool_scale, dhcat, "d_pool")
    pair_out = pair_begin({
        "w_out": gw_out.reshape(N_DEV, d // N_DEV, d),
        "w_pool": gw_pool.reshape(POOL_GROUPS, N_DEV, cg // N_DEV, cg).transpose(1, 0, 2, 3).reshape(N_DEV, -1, cg)},
        "out")
    dq, dk, dv, do, dcol, drow, dg_mlstm = _mlstm_bwd(qc, kc, qct, kct, proj_a, gcols, gbt, g_mlstm, mlstm_states, dhcat,
                                                      heads, "d_mlstm", token_of(pair_out))
    red_out = chip_begin(pair_out, dq, "out")
    drow = jnp.pad(drow.reshape(heads, t), ((0, LANES - heads), (0, 0)))
    dgates, dbias = _gates_bwd(proj_b, bias, dcol, drow, pw // LANES, heads, "d_gates")
    dq_pre, gw_conv_q = _conv_bwd(proj_a, conv_w[:, :wm], dq, 0, "d_q_conv")
    dk_pre, gw_conv_k = _conv_bwd(proj_a, conv_w[:, wm:], dk, wm, "d_k_conv")
    gw_conv = jnp.concatenate([gw_conv_q, gw_conv_k], axis=1)
    dproj_a = jnp.concatenate([dq_pre, dk_pre, dv, do], axis=1)
    dproj_b = jnp.concatenate([du, dgates], axis=1)
    sums_ff2, got_ff2 = _chip_wait(red_ff2[1], token_of(red_out), "reduce_wait_ff2")
    gw_a, *update = _mm_tn("gw_in_qkvo", dproj_a, xn,
                           side=_side_adamw(*state("w_ff2"), sums_ff2[0], got_ff2[0], place))
    out["w_ff2"] = [r.reshape(weights["w_ff2"].shape) for r in update]
    for k, s, g in zip(ff2_keys[1:], sums_ff2[1:], got_ff2[1:]):
        out[k] = [r.reshape(weights[k].shape) for r in _adamw_shard(*state(k), s, g, chip, "adamw_" + k)]
    gw_b = _mm_tn("gw_in_pool_gates", dproj_b, xn)
    gw_in_t = jnp.concatenate([gw_a, gw_b[pw:pw + 2 * heads], gw_b[:pw]], axis=0)
    pair_in = pair_begin({
        "w_in": gw_in_t.reshape(N_DEV, in_shard, d),
        "w_qk_conv": gw_conv.reshape(QK_CONV, N_DEV, -1).transpose(1, 0, 2).astype(BF16)}, "in")
    reduce_finish(red_out, token_of(pair_in), "out")
    red_in = chip_begin(pair_in, out["w_out"][0], "in")
    sums_ff1, got_ff1 = _chip_wait(red_ff1[1], token_of(red_in), "reduce_wait_ff1")
    dxn_a, *update = _mm_nn("d_proj_qkvo_in", dproj_a, in_t_a, tn=512, tk=4 * wm, deps=[token_of(red_in)],
                            side=_side_adamw(*state("w_ff1"), sums_ff1[0], got_ff1[0], place))
    out["w_ff1"] = [r.reshape(weights["w_ff1"].shape) for r in update]
    dxn = _mm_nn("d_proj_in", dproj_b, in_t_b, extra=[dxn_a], epilogue=lambda acc, prev: (acc + prev,))[0]
    grad_x, dg_mix_pre = _norm_bwd(xs, dxn, dh1, g_mix_pre, "d_norm_mix_pre")

    small_grads = dict(b_gates=dbias[:, :2 * heads], g_mlstm=dg_mlstm, pool_scale=dg_pool_scale, g_mix_pre=dg_mix_pre,
                       g_mix_post=dg_mix_post, g_ff_pre=dg_ff_pre, g_ff_post=dg_ff_post, g_ple_gate=dg_ple_gate,
                       g_ple_post=dg_ple_post)
    small_begun = _everyone_start([pack(small_grads)], "small_grads_start")
    reduce_finish(red_in, small_begun["token"], "in")
    (small_parts,) = _everyone_wait(small_begun, out["w_in"][0], "small_grads_wait")
    res = _adamw_replicated(*small_state, small_parts, "adamw_replicated")
    for i, k in enumerate(small):
        width = weights[k].shape[1]
        out[k] = [r[i:i + 1, :width] for r in res]

    return (loss, grad_x[None], *[out[k][0] for k in order], *[out[k][1] for k in order],
            *[out[k][2] for k in order], *[out[k][3] for k in order])


def _pad_rows16(a):
    return jnp.pad(a, ((0, 16 - a.shape[0]), (0, 0)))
```

```python
import jax
import jax.numpy as jnp
from jax import lax
from jax.experimental import pallas as pl
from jax.experimental.pallas import tpu as pltpu

F32 = jnp.float32
BF16 = jnp.bfloat16
EPS = 1e-6
N_DEV = 8
N_CHIP = 4
LANES = 128
VMEM_LIMIT = 56 * 1024 * 1024
MLSTM_HEADS = 8
POOL_GROUPS = 4
QK_CONV = 4
CHUNK = 128
ROW_TILE = 256
ADAM_LR, ADAM_B1, ADAM_B2, ADAM_EPS, ADAM_WD, ADAM_STEP = 0.001, 0.9, 0.999, 1e-08, 0.01, 10
MESH_AXES = ("x", "y", "c")
MESH = pl.DeviceIdType.MESH
HBM_SPEC = pl.BlockSpec(memory_space=pltpu.HBM)
SEM_SPEC = pl.BlockSpec(memory_space=pltpu.SEMAPHORE)
DATAFLOW = pltpu.SideEffectType.DATAFLOW_SIDE_EFFECTING


def _params(*sem):
    if sem:
        return pltpu.CompilerParams(dimension_semantics=sem, vmem_limit_bytes=VMEM_LIMIT)
    return pltpu.CompilerParams(vmem_limit_bytes=VMEM_LIMIT)


def _tile(n, pref):
    if n <= pref:
        return n
    t = (pref // LANES) * LANES
    while t >= LANES:
        if n % t == 0:
            return t
        t -= LANES
    return n


def _hbm(a):
    return pltpu.with_memory_space_constraint(a, pltpu.HBM)


def _mesh_place():
    x, y, c = lax.axis_index("x"), lax.axis_index("y"), lax.axis_index("c")
    return x, y, c, [(1 - x, y), (x, 1 - y), (1 - x, 1 - y)]


def _chip_copies(srcs, lands, send, recv):
    x, y, c, chips = _mesh_place()
    return [pltpu.make_async_remote_copy(
        src_ref=srcs[w].at[2 * chip[0] + chip[1]], dst_ref=lands[w].at[k],
        send_sem=send.at[3 * w + k], recv_sem=recv.at[3 * w + k],
        device_id=(*chip, c), device_id_type=MESH)
        for k, chip in enumerate(chips) for w in range(len(srcs))]


def _chip_start(sums, name):
    return _split_start(_chip_copies, sums, [lax.empty((3,) + a.shape[1:], a.dtype) for a in sums], 3, name,
                        peers=_same_core_of_other_chips)


def _chip_wait(started, after, name):
    return _split_wait(_chip_copies, started, after, name)


def _pair_start(pieces, name):
    return _split_start(_pair_copies, pieces, [lax.empty((N_CHIP,) + a.shape[1:], a.dtype) for a in pieces], N_CHIP, name,
                        peers=_sibling)


def _pair_wait(started, after, name):
    return _split_wait(_pair_copies, started, after, name)


def _pair_copies(srcs, lands, send, recv):
    x, y, c, _ = _mesh_place()
    return [pltpu.make_async_remote_copy(
        src_ref=srcs[w].at[2 * ch + 1 - c], dst_ref=lands[w].at[ch],
        send_sem=send.at[N_CHIP * w + ch], recv_sem=recv.at[N_CHIP * w + ch],
        device_id=(x, y, 1 - c), device_id_type=MESH)
        for w in range(len(srcs)) for ch in range(N_CHIP)]


OWN_BARRIER_IDS = {name: i for i, name in enumerate([
    "reduce_pair_start_ff2", "reduce_pair_start_ff1", "reduce_pair_start_out", "reduce_pair_start_in",
    "gather_pair_start_out", "gather_pair_start_ff1", "gather_pair_start_ple",
    "reduce_start_ff2", "reduce_start_ff1", "reduce_start_out", "reduce_start_in",
    "gather_start_in", "gather_start_rest", "gather_pair_in", "gather_pair_ff2_lo", "gather_pair_ff2_hi",
    "gather_pair_ple_proj"])}


def _shake_hands(peers):
    barrier = pltpu.get_barrier_semaphore()
    for peer in peers:
        pl.semaphore_signal(barrier, inc=1, device_id=peer, device_id_type=MESH)
    pl.semaphore_wait(barrier, len(peers))


def _sibling():
    x, y, c, _ = _mesh_place()
    return [(x, y, 1 - c)]


def _same_core_of_other_chips():
    x, y, c, chips = _mesh_place()
    return [(*chip, c) for chip in chips]


def _split_start(copies_of, srcs, lands, per_array, name, peers=None):
    ns, nb = len(srcs), len(srcs) + len(lands)
    n_copies = per_array * len(lands)

    def body(*refs):
        if peers is not None:
            _shake_hands(peers())
        for cp in copies_of(refs[:ns], refs[ns:nb], refs[nb], refs[nb + 1]):
            cp.start()
        refs[-1][...] = jnp.zeros_like(refs[-1])

    params = dict(has_side_effects=DATAFLOW)
    if peers is not None:
        params["collective_id"] = OWN_BARRIER_IDS[name]
    res = pl.pallas_call(
        body, name=name,
        out_shape=[pltpu.SemaphoreType.DMA((n_copies,)), pltpu.SemaphoreType.DMA((n_copies,))]
        + [pltpu.HBM(a.shape, a.dtype) for a in list(srcs) + list(lands)] + [jax.ShapeDtypeStruct((8, LANES), F32)],
        in_specs=[HBM_SPEC] * nb,
        out_specs=[SEM_SPEC, SEM_SPEC] + [HBM_SPEC] * nb + [pl.BlockSpec(memory_space=pltpu.VMEM)],
        input_output_aliases={i: 2 + i for i in range(nb)},
        compiler_params=pltpu.CompilerParams(**params),
    )(*[_hbm(a) for a in srcs], *[_hbm(a) for a in lands])
    return dict(send=res[0], recv=res[1], srcs=list(res[2:2 + ns]), lands=list(res[2 + ns:2 + nb]), token=res[-1])


def _split_wait(copies_of, started, after, name):
    ns = len(started["srcs"])
    nb = ns + len(started["lands"])

    def body(*refs):
        for cp in copies_of(refs[:ns], refs[ns:nb], refs[nb], refs[nb + 1]):
            cp.wait_send()
            cp.wait_recv()

    res = pl.pallas_call(
        body, name=name,
        out_shape=[pltpu.HBM(a.shape, a.dtype) for a in started["srcs"] + started["lands"]],
        in_specs=[HBM_SPEC] * nb + [SEM_SPEC, SEM_SPEC, ANY_SPEC],
        out_specs=[HBM_SPEC] * nb,
        input_output_aliases={i: i for i in range(nb)},
        compiler_params=pltpu.CompilerParams(has_side_effects=DATAFLOW),
    )(*started["srcs"], *started["lands"], started["send"], started["recv"], after)
    return list(res[:ns]), list(res[ns:])


def _forward_copies(srcs, lands, send, recv):
    del srcs
    x, y, c, chips = _mesh_place()
    copies = []
    for w in range(len(lands)):
        for j, chip in enumerate(chips):
            rows = lands[w].at[4 * chip[0] + 2 * chip[1] + c]
            copies.append(pltpu.make_async_remote_copy(
                src_ref=rows, dst_ref=rows, send_sem=send.at[3 * w + j], recv_sem=recv.at[3 * w + j],
                device_id=(x, y, 1 - c), device_id_type=MESH))
    return copies


def _forward_start(lands, name):
    return _split_start(_forward_copies, [], lands, 3, name, peers=_sibling)


def _forward_wait(started, after, name):
    return _split_wait(_forward_copies, started, after, name)[1]


def _everyone_copies(srcs, lands, send, recv):
    x, y, c, _ = _mesh_place()
    me = 4 * x + 2 * y + c
    copies = []
    for w in range(len(srcs)):
        for k in range(N_DEV - 1):
            flip = k + 1
            peer = (1 - x if flip & 4 else x, 1 - y if flip & 2 else y, 1 - c if flip & 1 else c)
            copies.append(pltpu.make_async_remote_copy(
                src_ref=srcs[w], dst_ref=lands[w].at[me], send_sem=send.at[7 * w + k], recv_sem=recv.at[7 * w + k],
                device_id=peer, device_id_type=MESH))
    return copies


def _everyone_start(blocks, name):
    me = 4 * lax.axis_index("x") + 2 * lax.axis_index("y") + lax.axis_index("c")
    lands = [lax.dynamic_update_slice(lax.empty((N_DEV,) + a.shape, a.dtype), a[None], (me,) + (0,) * a.ndim)
             for a in blocks]
    return _split_start(_everyone_copies, blocks, lands, N_DEV - 1, name)


def _everyone_wait(started, after, name):
    return _split_wait(_everyone_copies, started, after, name)[1]


def _gather_copies(srcs, lands, send, recv):
    x, y, c, chips = _mesh_place()
    me = 4 * x + 2 * y + c
    targets = [(x, y, 1 - c)] + [(*chip, c) for chip in chips]
    return [pltpu.make_async_remote_copy(
        src_ref=srcs[w], dst_ref=lands[w].at[me], send_sem=send.at[4 * w + k], recv_sem=recv.at[4 * w + k],
        device_id=to, device_id_type=MESH)
        for w in range(len(srcs)) for k, to in enumerate(targets)]


def _gather_start(groups, name):
    sizes = [len(g) for g in groups]
    flat = [a for g in groups for a in g]
    n, ng = len(flat), len(groups)
    me = 4 * lax.axis_index("x") + 2 * lax.axis_index("y") + lax.axis_index("c")
    lands = [lax.dynamic_update_slice(lax.empty((N_DEV,) + a.shape, a.dtype), a[None], (me,) + (0,) * a.ndim)
             for a in flat]

    def body(*refs):
        _shake_hands(_sibling() + _same_core_of_other_chips())
        srcs, zones = refs[:n], refs[n:2 * n]
        sems = refs[2 * n:2 * n + 2 * ng]
        token = refs[-1]
        lo = 0
        for gi, size in enumerate(sizes):
            for cp in _gather_copies(srcs[lo:lo + size], zones[lo:lo + size], sems[2 * gi], sems[2 * gi + 1]):
                cp.start()
            lo += size
        token[...] = jnp.zeros_like(token)

    sem_shapes = []
    for size in sizes:
        sem_shapes += [pltpu.SemaphoreType.DMA((4 * size,))] * 2
    res = pl.pallas_call(
        body, name=name,
        out_shape=sem_shapes + [pltpu.HBM(a.shape, a.dtype) for a in flat + lands] + [jax.ShapeDtypeStruct((8, LANES), F32)],
        in_specs=[HBM_SPEC] * (2 * n),
        out_specs=[SEM_SPEC] * (2 * ng) + [HBM_SPEC] * (2 * n) + [pl.BlockSpec(memory_space=pltpu.VMEM)],
        input_output_aliases={i: 2 * ng + i for i in range(2 * n)},
        compiler_params=pltpu.CompilerParams(has_side_effects=DATAFLOW, collective_id=OWN_BARRIER_IDS[name]),
    )(*[_hbm(a) for a in flat], *[_hbm(a) for a in lands])
    out, lo = [], 0
    for gi, size in enumerate(sizes):
        out.append(dict(send=res[2 * gi], recv=res[2 * gi + 1],
                        srcs=list(res[2 * ng + lo:2 * ng + lo + size]),
                        lands=list(res[2 * ng + n + lo:2 * ng + n + lo + size])))
        lo += size
    return out, res[-1]


def _gather_wait(started, after, name):
    n = len(started["srcs"])
    after = list(after) if isinstance(after, (list, tuple)) else [after]

    def body(*refs):
        srcs, zones = refs[:n], refs[n:2 * n]
        send, recv = refs[2 * n], refs[2 * n + 1]
        for cp in _gather_copies(srcs, zones, send, recv):
            cp.wait_send()
            cp.wait_recv()

    res = pl.pallas_call(
        body, name=name,
        out_shape=[pltpu.HBM(a.shape, a.dtype) for a in started["srcs"] + started["lands"]],
        in_specs=[HBM_SPEC] * (2 * n) + [SEM_SPEC, SEM_SPEC] + [ANY_SPEC] * len(after),
        out_specs=[HBM_SPEC] * (2 * n),
        input_output_aliases={i: i for i in range(2 * n)},
        compiler_params=pltpu.CompilerParams(has_side_effects=DATAFLOW),
    )(*started["srcs"], *started["lands"], started["send"], started["recv"], *after)
    return list(res[n:])


def _forward_pair(lands, name):
    n = len(lands)

    def body(*refs):
        _shake_hands(_sibling())
        zones = refs[:n]
        send, recv = refs[2 * n:]
        x, y, c, chips = _mesh_place()

        def copy(w, j, core):
            rows = zones[w].at[4 * chips[j][0] + 2 * chips[j][1] + core]
            return pltpu.make_async_remote_copy(
                src_ref=rows, dst_ref=rows, send_sem=send.at[3 * w + j], recv_sem=recv.at[3 * w + j],
                device_id=(x, y, 1 - c), device_id_type=MESH)

        sends = [copy(w, j, c) for w in range(n) for j in range(3)]
        for cp in sends:
            cp.start()
        for w in range(n):
            for j in range(3):
                copy(w, j, 1 - c).wait_recv()
        for cp in sends:
            cp.wait_send()

    return list(pl.pallas_call(
        body, name=name,
        out_shape=[jax.ShapeDtypeStruct(a.shape, a.dtype) for a in lands],
        in_specs=[HBM_SPEC] * n, out_specs=[HBM_SPEC] * n,
        input_output_aliases={i: i for i in range(n)},
        scratch_shapes=[pltpu.SemaphoreType.DMA((3 * n,)), pltpu.SemaphoreType.DMA((3 * n,))],
        compiler_params=pltpu.CompilerParams(collective_id=OWN_BARRIER_IDS[name]),
    )(*lands))


def _pair_sum(pieces, got, core, name):
    _, rows, cols = pieces.shape
    tr, tc = _tile_2d(rows, cols)

    def body(core_ref, mine_ref, got_ref, out_ref):
        del core_ref
        out_ref[...] = (mine_ref[...].astype(F32) + got_ref[...].astype(F32)).astype(out_ref.dtype)

    return pl.pallas_call(
        body, name=name,
        out_shape=jax.ShapeDtypeStruct((N_CHIP, rows, cols), pieces.dtype),
        grid_spec=pltpu.PrefetchScalarGridSpec(
            num_scalar_prefetch=1, grid=(N_CHIP, rows // tr, cols // tc),
            in_specs=[pl.BlockSpec((None, tr, tc), lambda ch, i, j, core: (2 * ch + core[0], i, j)),
                      pl.BlockSpec((None, tr, tc), lambda ch, i, j, core: (ch, i, j))],
            out_specs=pl.BlockSpec((None, tr, tc), lambda ch, i, j, core: (ch, i, j))),
        compiler_params=_params("parallel", "parallel", "parallel"),
    )(core, pieces, got)


def _tile_2d(rows, cols, budget=2 * 1024 * 1024):
    want = max(16, budget // (4 * cols))
    if rows <= want:
        return rows, cols
    t = (want // 16) * 16
    while t >= 16:
        if rows % t == 0:
            return t, cols
        t -= 16
    return rows, _tile(cols, max(LANES, budget // (4 * rows)))


def _adamw_math(w, g, m, v):
    m = ADAM_B1 * m + (1.0 - ADAM_B1) * g
    v = ADAM_B2 * v + (1.0 - ADAM_B2) * (g * g)
    m_hat = m / (1.0 - ADAM_B1 ** ADAM_STEP)
    v_hat = v / (1.0 - ADAM_B2 ** ADAM_STEP)
    delta = -ADAM_LR * (m_hat / (jnp.sqrt(v_hat) + ADAM_EPS) + ADAM_WD * w)
    return delta, m, v


def _adamw_shard(w, m, v, sums, got, chip, name):
    rows, cols = w.shape
    tr, tc = _tile_2d(rows, cols, budget=1024 * 1024)

    def body(chip_ref, w_ref, m_ref, v_ref, own_ref, got_ref, g_out, d_out, m_out, v_out):
        del chip_ref
        g = own_ref[...].astype(F32)
        for k in range(3):
            g = g + got_ref[k].astype(F32)
        delta, m_new, v_new = _adamw_math(w_ref[...], g, m_ref[...], v_ref[...])
        g_out[...] = g
        d_out[...] = delta
        m_out[...] = m_new
        v_out[...] = v_new

    blk = pl.BlockSpec((tr, tc), lambda i, j, chip: (i, j))
    return pl.pallas_call(
        body, name=name,
        out_shape=[jax.ShapeDtypeStruct((rows, cols), F32)] * 4,
        grid_spec=pltpu.PrefetchScalarGridSpec(
            num_scalar_prefetch=1, grid=(rows // tr, cols // tc),
            in_specs=[blk, blk, blk,
                      pl.BlockSpec((None, tr, tc), lambda i, j, chip: (chip[0], i, j)),
                      pl.BlockSpec((3, tr, tc), lambda i, j, chip: (0, i, j))],
            out_specs=[blk] * 4),
        compiler_params=_params("parallel", "parallel"),
    )(chip, w, m, v, sums, got)


def _adamw_replicated(w, m, v, parts, name):
    rows, cols = w.shape

    def body(w_ref, m_ref, v_ref, parts_ref, g_out, d_out, m_out, v_out):
        g = parts_ref[0]
        for d in range(1, N_DEV):
            g = g + parts_ref[d]
        delta, m_new, v_new = _adamw_math(w_ref[...], g, m_ref[...], v_ref[...])
        g_out[...] = g
        d_out[...] = delta
        m_out[...] = m_new
        v_out[...] = v_new

    return pl.pallas_call(
        body, name=name,
        out_shape=[jax.ShapeDtypeStruct((rows, cols), F32)] * 4,
        compiler_params=_params(),
    )(w, m, v, parts)


NN = ((1,), (0,))
NT = ((1,), (1,))
TN = ((0,), (0,))


ANY_SPEC = pl.BlockSpec(memory_space=pl.ANY)


def _mm(name, a, b, *, dims, grid, a_spec, b_spec, outs, extra=(), extra_specs=(), epilogue=None, acc_shape=None,
        deps=(), b_pieces=0, side=None, b2=None, into=None):
    nk = grid[2]
    n_extra, n_out = len(extra), len(outs)
    side_ins = side["ins"] if side else []
    side_outs = side["outs"] if side else []
    first_side = 2 + n_extra + len(deps)
    tail = ([b2] if b2 is not None else []) + ([into] if into is not None else [])
    first_out = first_side + len(side_ins) + len(tail)
    assert not side or nk == 1

    assert not side or (epilogue is None and n_out == 1 and not b_pieces and not tail)
    SIDE_SLICES = 4

    def body(*refs):
        dot = lambda lhs, rhs: lax.dot_general(lhs.astype(BF16), rhs.astype(BF16), (dims, ((), ())),
                                               preferred_element_type=F32)
        if side:
            refs = refs[1:]
            a_ref, b_ref, out_ref = refs[0], refs[1], refs[first_out]
            ins, sides = refs[first_side:first_out], refs[first_out + 1:first_out + 1 + len(side_outs)]
            rows = out_ref.shape[0] // SIDE_SLICES
            srows = sides[0].shape[-2] // SIDE_SLICES
            for part in range(SIDE_SLICES):
                rs = slice(part * srows, (part + 1) * srows)
                for val, o in zip(side["fn"](*[r[..., rs, :] for r in ins]), sides):
                    o[rs, :] = val.astype(o.dtype)
                ms = slice(part * rows, (part + 1) * rows)
                lhs = a_ref[:, ms] if dims == TN else a_ref[ms, :]
                out_ref[ms, :] = dot(lhs, b_ref[...]).astype(out_ref.dtype)
            return
        a_ref, b_ref = refs[0], refs[1]
        ex = refs[2:2 + n_extra]
        out_refs = refs[first_out:first_out + n_out]
        if b_pieces:
            ks = b_ref.shape[-1]
            part = dot(a_ref[:, 0:ks], b_ref[0])
            for piece in range(1, b_pieces):
                part = part + dot(a_ref[:, piece * ks:(piece + 1) * ks], b_ref[piece])
        elif b2 is not None:
            ks = b_ref.shape[-1]
            part = dot(a_ref[:, 0:ks], b_ref[...]) + dot(a_ref[:, ks:2 * ks], refs[first_side][...])
        else:
            part = dot(a_ref[...], b_ref[...])

        def finish(acc):
            res = epilogue(acc, *[e[...] for e in ex]) if epilogue else (acc,)
            for val, o in zip(res, out_refs):
                o[...] = val.astype(o.dtype)

        if nk == 1:
            finish(part)
        else:
            acc_ref = refs[-1]
            k = pl.program_id(2)

            @pl.when(k == 0)
            def _():
                acc_ref[...] = part

            @pl.when(k > 0)
            def _():
                acc_ref[...] += part

            @pl.when(k == nk - 1)
            def _():
                finish(acc_ref[...])

    scratch = [pltpu.VMEM(acc_shape, F32)] if nk > 1 else []
    out_shape = [jax.ShapeDtypeStruct(s, d) for s, d, _ in outs]
    in_specs = [a_spec, b_spec, *extra_specs] + [ANY_SPEC] * len(deps)
    out_specs = [sp for _, _, sp in outs]
    if not side:
        in_specs += ([b_spec] if b2 is not None else []) + ([ANY_SPEC] if into is not None else [])
        return list(pl.pallas_call(
            body, name=name, out_shape=out_shape, grid=grid, in_specs=in_specs, out_specs=out_specs,
            scratch_shapes=scratch, compiler_params=_params("parallel", "parallel", "arbitrary"),
            input_output_aliases={len(in_specs) - 1: 0} if into is not None else {},
        )(a, b, *extra, *deps, *tail))

    def with_scalars(spec):
        if spec.index_map is None:
            return spec
        return pl.BlockSpec(spec.block_shape, lambda i, j, k, s, f=spec.index_map: f(i, j, k))

    def by_step(block, index_map):
        return pl.BlockSpec(block, lambda i, j, k, s: index_map((i * grid[1] + j) * grid[2] + k, s))

    return list(pl.pallas_call(
        body, name=name,
        out_shape=out_shape + [jax.ShapeDtypeStruct(s, d) for s, d, _, _ in side_outs],
        grid_spec=pltpu.PrefetchScalarGridSpec(
            num_scalar_prefetch=1, grid=grid,
            in_specs=[with_scalars(sp) for sp in in_specs] + [by_step(blk, im) for _, blk, im in side_ins],
            out_specs=[with_scalars(sp) for sp in out_specs] + [by_step(blk, im) for _, _, blk, im in side_outs],
            scratch_shapes=scratch),
        compiler_params=_params("arbitrary", "arbitrary", "arbitrary"),
    )(side["scalars"], a, b, *extra, *deps, *[arr for arr, _, _ in side_ins]))


def _side_pair_sum(pieces, got, place):
    _, rows, cols = pieces.shape

    def build(steps):
        per_chip = steps // N_CHIP
        tr = rows // per_chip
        assert per_chip * N_CHIP == steps and tr * per_chip == rows and tr % 16 == 0
        blk = (None, tr, cols)
        here = lambda st, s: (st // per_chip, st % per_chip, 0)
        return dict(scalars=place,
                    ins=[(pieces, blk, lambda st, s: (2 * (st // per_chip) + s[0], st % per_chip, 0)), (got, blk, here)],
                    outs=[((N_CHIP, rows, cols), pieces.dtype, blk, here)],
                    fn=lambda mine, other: (mine.astype(F32) + other.astype(F32),))
    return build


def _side_adamw(w, m, v, sums, got, place):
    rows, cols = w.shape

    def build(steps):
        tr = rows // steps
        assert tr * steps == rows and tr % 16 == 0
        blk, here = (tr, cols), (lambda st, s: (st, 0))

        def fn(w, m, v, own, got):
            g = own.astype(F32)
            for k in range(3):
                g = g + got[k].astype(F32)
            delta, m_new, v_new = _adamw_math(w, g, m, v)
            return g, delta, m_new, v_new

        return dict(scalars=place,
                    ins=[(w, blk, here), (m, blk, here), (v, blk, here),
                         (sums, (None, tr, cols), lambda st, s: (s[1], st, 0)),
                         (got, (3, tr, cols), lambda st, s: (0, st, 0))],
                    outs=[((rows, cols), F32, blk, here)] * 4, fn=fn)
    return build


def _mm_nn(name, a, b, out_dtype=F32, tm=1024, tn=1024, tk=2048, epilogue=None, n_out=1, extra=(), deps=(), side=None,
           out_cols=None, out_at=0, into=None):
    m, kd = a.shape
    n = b.shape[1]
    tm, tn, tk = _tile(m, tm), _tile(n, tn), _tile(kd, tk)
    dts = out_dtype if isinstance(out_dtype, (list, tuple)) else [out_dtype] * n_out
    shift = out_at // tn
    o_spec = pl.BlockSpec((tm, tn), lambda i, j, k: (i, j + shift))
    grid = (m // tm, n // tn, kd // tk)
    return _mm(name, a, b, dims=NN, grid=grid,
               a_spec=pl.BlockSpec((tm, tk), lambda i, j, k: (i, k)),
               b_spec=pl.BlockSpec((tk, tn), lambda i, j, k: (k, j)),
               outs=[((m, out_cols or n), dt, o_spec) for dt in dts], extra=extra, extra_specs=[o_spec] * len(extra),
               epilogue=epilogue, acc_shape=(tm, tn), deps=deps, side=side and side(grid[0] * grid[1] * grid[2]),
               into=into)


def _mm_nn_pieces(name, a, b, out_dtype=F32, tm=1024, tk=2048, epilogue=None, n_out=1):
    m, kd = a.shape
    npc, _, ns = b.shape
    tm, tk = _tile(m, tm), _tile(kd, tk)
    tn = _tile(ns, 1024)
    r = ns // tn
    dts = out_dtype if isinstance(out_dtype, (list, tuple)) else [out_dtype] * n_out
    o_spec = pl.BlockSpec((tm, tn), lambda i, j, k: (i, j))
    return _mm(name, a, b, dims=NN, grid=(m // tm, npc * r, kd // tk),
               a_spec=pl.BlockSpec((tm, tk), lambda i, j, k: (i, k)),
               b_spec=pl.BlockSpec((None, tk, tn), lambda i, j, k: (j // r, k, j % r)),
               outs=[((m, npc * ns), dt, o_spec) for dt in dts], epilogue=epilogue, acc_shape=(tm, tn))


def _mm_nt(name, a, b, out_dtype=F32, tm=1024, tn=1024, tk=2048, epilogue=None, extra=(), n_rows=None, deps=(),
           side=None, b2=None):
    m, kd = a.shape
    n = n_rows or b.shape[0]
    tm, tn, tk = _tile(m, tm), _tile(n, tn), _tile(kd, tk)
    o_spec = pl.BlockSpec((tm, tn), lambda i, j, k: (i, j))
    if b2 is not None:
        tk = kd
        b_spec = pl.BlockSpec((tn, kd // 2), lambda i, j, k: (j, 0))
    else:
        b_spec = pl.BlockSpec((tn, tk), lambda i, j, k: (j, k))
    grid = (m // tm, n // tn, kd // tk)
    res = _mm(name, a, b, dims=NT, grid=grid,
              a_spec=pl.BlockSpec((tm, tk), lambda i, j, k: (i, k)), b_spec=b_spec,
              outs=[((m, n), out_dtype, o_spec)], extra=extra, extra_specs=[o_spec] * len(extra),
              epilogue=epilogue, acc_shape=(tm, tn), deps=deps, side=side and side(grid[0] * grid[1] * grid[2]), b2=b2)
    return res if side else res[0]


def _mm_nt_pieces(name, a, b, out_dtype=F32, tm=1024, tn=256, deps=()):
    m = a.shape[0]
    npc, n, ks = b.shape
    tm, tn = _tile(m, tm), _tile(n, tn)
    return _mm(name, a, b, dims=NT, grid=(m // tm, n // tn, 1),
               a_spec=pl.BlockSpec((tm, npc * ks), lambda i, j, k: (i, 0)),
               b_spec=pl.BlockSpec((npc, tn, ks), lambda i, j, k: (0, j, 0)),
               outs=[((m, n), out_dtype, pl.BlockSpec((tm, tn), lambda i, j, k: (i, j)))],
               deps=deps, b_pieces=npc)[0]


def _mm_tn(name, a, b, out_dtype=BF16, tm=1024, tn=1024, pieces=0, deps=(), side=None):
    t, m = a.shape
    n = b.shape[1]
    tm = _tile(m, tm)
    if pieces:
        ns = n // pieces
        tn = _tile(ns, tn)
        r = ns // tn
        out = ((pieces, m, ns), out_dtype, pl.BlockSpec((None, tm, tn), lambda i, j, k: (j // r, i, j % r)))
    else:
        tn = _tile(n, tn)
        out = ((m, n), out_dtype, pl.BlockSpec((tm, tn), lambda i, j, k: (i, j)))
    grid = (m // tm, n // tn, 1)
    res = _mm(name, a, b, dims=TN, grid=grid,
              a_spec=pl.BlockSpec((t, tm), lambda i, j, k: (0, i)),
              b_spec=pl.BlockSpec((t, tn), lambda i, j, k: (0, j)),
              outs=[out], deps=deps, side=side and side(grid[0] * grid[1]))
    return res if side else res[0]


def _rms_f(x, g):
    r = lax.rsqrt(jnp.mean(x * x, axis=-1, keepdims=True) + EPS)
    xh = x * r
    return xh * g, xh, r


def _rms_b(dy, g, xh, r):
    dxh = dy * g
    dx = r * (dxh - xh * jnp.mean(dxh * xh, axis=-1, keepdims=True))
    dg = jnp.sum(dy * xh, axis=0, keepdims=True)
    return dx, dg


def _rowwise(name, fn, row_ins, vec_ins, row_outs, vec_outs, deps=()):
    t = row_ins[0].shape[0]
    tr = min(ROW_TILE, t)
    nr, nv, no = len(row_ins), len(vec_ins), len(row_outs)
    first_out = nr + nv + len(deps)

    def body(*refs):
        ri, vi = refs[:nr], refs[nr:nr + nv]
        ro, vo = refs[first_out:first_out + no], refs[first_out + no:]
        routs, vouts = fn(*[r[...] for r in ri], *[v[...] for v in vi])
        for o, val in zip(ro, routs):
            o[...] = val.astype(o.dtype)
        i = pl.program_id(0)
        for o, val in zip(vo, vouts):
            @pl.when(i == 0)
            def _():
                o[...] = val

            @pl.when(i > 0)
            def _():
                o[...] += val

    res = pl.pallas_call(
        body, name=name,
        out_shape=[jax.ShapeDtypeStruct((t, w), d) for w, d in row_outs]
        + [jax.ShapeDtypeStruct((1, w), F32) for w in vec_outs],
        grid=(t // tr,),
        in_specs=[pl.BlockSpec((tr, a.shape[1]), lambda i: (i, 0)) for a in row_ins]
        + [pl.BlockSpec((1, a.shape[1]), lambda i: (0, 0)) for a in vec_ins] + [ANY_SPEC] * len(deps),
        out_specs=[pl.BlockSpec((tr, w), lambda i: (i, 0)) for w, _ in row_outs]
        + [pl.BlockSpec((1, w), lambda i: (0, 0)) for w in vec_outs],
        compiler_params=_params("arbitrary"),
    )(*row_ins, *vec_ins, *deps)
    return list(res)


def _norm_fwd(x, g, name, deps=()):
    d = x.shape[1]
    return _rowwise(name, lambda x, g: ((_rms_f(x, g)[0],), ()), [x], [g], [(d, BF16)], [], deps=deps)[0]


def _norm_bwd(x, dxn, dres, g, name):
    d = x.shape[1]

    def fn(x, dxn, dres, g):
        _, xh, r = _rms_f(x, g)
        dx, dg = _rms_b(dxn, g, xh, r)
        return (dres + dx,), (dg,)

    return _rowwise(name, fn, [x, dxn, dres], [g], [(d, F32)], [d])


def _block_fwd(h_prev, y, g_y, g_n, name, deps=()):
    d = h_prev.shape[1]

    def fn(h_prev, y, g_y, g_n):
        h = h_prev + _rms_f(y, g_y)[0]
        return (h, _rms_f(h, g_n)[0]), ()

    return _rowwise(name, fn, [h_prev, y], [g_y, g_n], [(d, F32), (d, BF16)], [], deps=deps)


def _block_bwd(h, y, d_up, d_n, g_y, g_n, name, deps=()):
    d = h.shape[1]

    def fn(h, y, d_up, d_n, g_y, g_n):
        _, hh, hr = _rms_f(h, g_n)
        dh_n, dg_n = _rms_b(d_n, g_n, hh, hr)
        dh = d_up + dh_n
        _, yh, yr = _rms_f(y, g_y)
        dy, dg_y = _rms_b(dh, g_y, yh, yr)
        return (dh, dy), (dg_y, dg_n)

    return _rowwise(name, fn, [h, y, d_up, d_n], [g_y, g_n], [(d, F32), (d, BF16)], [d, d], deps=deps)


def _loss_head(h2, z, e, target, g, name):
    d = h2.shape[1]

    def fn(h2, z, e, target, g):
        gate = jax.nn.sigmoid(z)
        y, xh, r = _rms_f(e * gate, g)
        diff = h2 + y - target
        loss = 0.5 * jnp.sum(jnp.mean(diff * diff, axis=-1, keepdims=True), axis=0, keepdims=True)
        dh3 = diff / d
        deg, dg = _rms_b(dh3, g, xh, r)
        dz = deg * e * gate * (1.0 - gate)
        de = deg * gate
        return (dh3, dz, de), (dg, jnp.broadcast_to(loss, (1, LANES)))

    return _rowwise(name, fn, [h2, z, e, target], [g], [(d, F32), (d, BF16), (d, BF16)], [d, LANES])


def _shift_down(x, s, rows):
    return jnp.where(rows >= s, pltpu.roll(x, s, axis=0), 0.0)


def _shift_up(x, s, rows):
    t = x.shape[0]
    return jnp.where(rows < t - s, pltpu.roll(x, t - s, axis=0), 0.0)


def _conv_pre(x, w, rows):
    pre = x * w[QK_CONV - 1:QK_CONV, :]
    for s in range(1, QK_CONV):
        pre = pre + _shift_down(x, s, rows) * w[QK_CONV - 1 - s:QK_CONV - s, :]
    return pre


def _conv_fwd(proj_a, w, start, name, scale_t=1.0):
    t = proj_a.shape[0]
    width = w.shape[1]
    tc = _tile(width, 256)
    off = start // tc

    def body(x_ref, w_ref, y_ref, yt_ref):
        x = x_ref[...]
        rows = lax.broadcasted_iota(jnp.int32, x.shape, 0)
        pre = _conv_pre(x, w_ref[...], rows)
        y = pre * jax.nn.sigmoid(pre)
        y_ref[...] = y
        yt_ref[...] = (y * scale_t).T.astype(yt_ref.dtype)

    return pl.pallas_call(
        body, name=name,
        out_shape=[jax.ShapeDtypeStruct((t, width), F32), jax.ShapeDtypeStruct((width, t), BF16)],
        grid=(width // tc,),
        in_specs=[pl.BlockSpec((t, tc), lambda j: (0, off + j)), pl.BlockSpec((QK_CONV, tc), lambda j: (0, j))],
        out_specs=[pl.BlockSpec((t, tc), lambda j: (0, j)), pl.BlockSpec((tc, t), lambda j: (j, 0))],
        compiler_params=_params("parallel"),
    )(proj_a, w)


def _conv_bwd(proj_a, w, dy, start, name):
    t = proj_a.shape[0]
    width = w.shape[1]
    tc = _tile(width, 256)
    off = start // tc

    def body(x_ref, w_ref, dy_ref, dx_ref, dw_ref):
        x, w = x_ref[...], w_ref[...]
        rows = lax.broadcasted_iota(jnp.int32, x.shape, 0)
        pre = _conv_pre(x, w, rows)
        sig = jax.nn.sigmoid(pre)
        dpre = dy_ref[...] * (sig * (1.0 + pre * (1.0 - sig)))
        dx = dpre * w[QK_CONV - 1:QK_CONV, :]
        dws = [jnp.sum(dpre * x, axis=0, keepdims=True)]
        for s in range(1, QK_CONV):
            dx = dx + _shift_up(dpre, s, rows) * w[QK_CONV - 1 - s:QK_CONV - s, :]
            dws.append(jnp.sum(dpre * _shift_down(x, s, rows), axis=0, keepdims=True))
        dx_ref[...] = dx.astype(dx_ref.dtype)
        for s in range(QK_CONV):
            dw_ref[QK_CONV - 1 - s:QK_CONV - s, :] = dws[s]

    return pl.pallas_call(
        body, name=name,
        out_shape=[jax.ShapeDtypeStruct((t, width), BF16), jax.ShapeDtypeStruct((QK_CONV, width), F32)],
        grid=(width // tc,),
        in_specs=[pl.BlockSpec((t, tc), lambda j: (0, off + j)), pl.BlockSpec((QK_CONV, tc), lambda j: (0, j)),
                  pl.BlockSpec((t, tc), lambda j: (0, j))],
        out_specs=[pl.BlockSpec((t, tc), lambda j: (0, j)), pl.BlockSpec((QK_CONV, tc), lambda j: (0, j))],
        compiler_params=_params("parallel"),
    )(proj_a, w, dy)


def _dot(a, b, dims):
    return lax.dot_general(a.astype(BF16), b.astype(BF16), (dims, ((), ())), preferred_element_type=F32)


def _dot_split(ones, x):
    hi = x.astype(BF16)
    rest = x - hi.astype(F32)
    mid = rest.astype(BF16)
    lo = (rest - mid.astype(F32)).astype(BF16)
    dot = lambda part: lax.dot_general(ones, part, (NN, ((), ())), preferred_element_type=F32)
    return dot(hi) + dot(mid) + dot(lo)


def _chunk_iotas():
    rows = lax.broadcasted_iota(jnp.int32, (CHUNK, CHUNK), 0)
    cols = lax.broadcasted_iota(jnp.int32, (CHUNK, CHUNK), 1)
    return rows, cols


def _gates_fwd(proj_b, bias, col_block, heads, name):
    t = proj_b.shape[0]

    def body(x_ref, b_ref, gb_ref, gbt_ref, cols_ref):
        rows, cols = _chunk_iotas()
        tri = (cols <= rows).astype(BF16)
        for ci in range(t // CHUNK):
            sl = slice(ci * CHUNK, (ci + 1) * CHUNK)
            pre = x_ref[sl, :] + b_ref[...]
            log_f = jnp.minimum(pre, 0.0) - jnp.log(1.0 + jnp.exp(-jnp.abs(pre)))
            log_f = jnp.where((cols >= heads) & (cols < 2 * heads), log_f, 0.0)
            tile = jnp.where(cols < heads, pre, _dot_split(tri, log_f))
            gb_ref[sl, :] = tile
            gbt_ref[:, sl] = tile.T
            for h in range(2 * heads):
                cols_ref[h, sl, :] = jnp.broadcast_to(tile[:, h:h + 1], (CHUNK, LANES))

    return pl.pallas_call(
        body, name=name,
        out_shape=[jax.ShapeDtypeStruct((t, LANES), F32), jax.ShapeDtypeStruct((LANES, t), F32),
                   jax.ShapeDtypeStruct((2 * heads, t, LANES), F32)],
        grid=(1,),
        in_specs=[pl.BlockSpec((t, LANES), lambda i: (0, col_block)), pl.BlockSpec((1, LANES), lambda i: (0, 0))],
        out_specs=[pl.BlockSpec((t, LANES), lambda i: (0, 0)), pl.BlockSpec((LANES, t), lambda i: (0, 0)),
                   pl.BlockSpec((2 * heads, t, LANES), lambda i: (0, 0, 0))],
        compiler_params=_params("arbitrary"),
    )(proj_b, bias)


def _gates_bwd(proj_b, bias, dcol, drow, col_block, heads, name):
    t = proj_b.shape[0]

    def body(x_ref, b_ref, dc_ref, dr_ref, dx_ref, db_ref):
        rows, cols = _chunk_iotas()
        later = (cols >= rows).astype(BF16)
        total = jnp.zeros((1, LANES), F32)
        for ci in range(t // CHUNK):
            sl = slice(ci * CHUNK, (ci + 1) * CHUNK)
            by_row = dr_ref[:, sl].T
            d = dc_ref[sl, :] + jnp.where(cols < heads, by_row,
                                          jnp.where(cols < 2 * heads, -pltpu.roll(by_row, heads, axis=1), 0.0))
            d_log_f = _dot_split(later, jnp.where(cols >= heads, d, 0.0))
            pre = x_ref[sl, :] + b_ref[...]
            dx = jnp.where(cols < heads, d, jnp.where(cols < 2 * heads, d_log_f * jax.nn.sigmoid(-pre), 0.0))
            dx_ref[sl, :] = dx.astype(dx_ref.dtype)
            total = total + jnp.sum(dx, axis=0, keepdims=True)
        db_ref[...] = total

    return pl.pallas_call(
        body, name=name,
        out_shape=[jax.ShapeDtypeStruct((t, LANES), BF16), jax.ShapeDtypeStruct((1, LANES), F32)],
        grid=(1,),
        in_specs=[pl.BlockSpec((t, LANES), lambda i: (0, col_block)), pl.BlockSpec((1, LANES), lambda i: (0, 0)),
                  pl.BlockSpec((t, LANES), lambda i: (0, 0)), pl.BlockSpec((LANES, t), lambda i: (0, 0))],
        out_specs=[pl.BlockSpec((t, LANES), lambda i: (0, 0)), pl.BlockSpec((1, LANES), lambda i: (0, 0))],
        compiler_params=_params("arbitrary"),
    )(proj_b, bias, dcol, drow)


def _gate_views(li_ref, b_ref, gbt_ref, s, head, heads, sl):
    li_c, b_c = li_ref[s, sl, :], b_ref[s, sl, :]
    li_r = jnp.broadcast_to(gbt_ref[head, :, sl], (CHUNK, CHUNK))
    b_r = jnp.broadcast_to(gbt_ref[heads + head, :, sl], (CHUNK, CHUNK))
    return li_c, b_c, li_r, b_r


def _chunk_state(kt, k, v, li_c, b_c, ct, nrow, m_prev):
    b_tot = b_c[CHUNK - 1:CHUNK, 0:1]
    a = b_tot - b_c[:, 0:1] + li_c[:, 0:1]
    m_new = jnp.maximum(b_tot + m_prev, jnp.max(a, axis=0, keepdims=True))
    dec = jnp.exp(b_tot + m_prev - m_new)
    w = jnp.exp(a - m_new)
    wv = w * v
    ct_new = dec * ct + _dot(kt, wv, NN)
    n_new = dec * nrow + jnp.sum(w * k, axis=0, keepdims=True)
    return ct_new, n_new, m_new, dec, w, wv


def _chunk_scores(q, k, b_c, li_r, b_r, m_prev):
    rows, cols = _chunk_iotas()
    dlog = jnp.where(cols <= rows, b_c - b_r + li_r, -jnp.inf)
    inter = b_c[:, 0:1] + m_prev
    mt = jnp.maximum(inter, jnp.max(dlog, axis=-1, keepdims=True))
    dw = jnp.exp(dlog - mt)
    iw = jnp.exp(inter - mt)
    a = _dot(q, k, NT)
    return dw, iw, mt, a, a * dw


def _to_row(column):
    rows, cols = _chunk_iotas()
    return jnp.sum(jnp.where(rows == cols, jnp.broadcast_to(column, (CHUNK, CHUNK)), 0.0), axis=0, keepdims=True)


def _head_out(h, o, g):
    cell, hh, r = _rms_f(h, g)
    sig = jax.nn.sigmoid(o)
    return sig * cell, (cell, hh, r, sig)


HEADS_PER_STEP = 2


def _mlstm_fwd(qc, kc, kct, proj_a, gcols, gbt, g_mlstm, heads, d_model, name, dep):
    t = qc.shape[0]
    dh = g_mlstm.shape[1] // heads
    nc = t // CHUNK
    scale = dh ** -0.5
    hp = HEADS_PER_STEP

    def body(q_ref, k_ref, kt_ref, v_ref, o_ref, li_ref, b_ref, gbt_ref, g_ref, dep_ref, out_ref, ct_out, nm_out,
             ct_ref, n_ref, m_ref):
        del dep_ref
        step = pl.program_id(0)
        ct_ref[...] = jnp.zeros_like(ct_ref)
        n_ref[...] = jnp.zeros_like(n_ref)
        m_ref[...] = jnp.zeros_like(m_ref)
        tile_row = lax.broadcasted_iota(jnp.int32, (8, LANES), 0)

        def chunk(ci, carry):
            sl = pl.ds(pl.multiple_of(ci * CHUNK, CHUNK), CHUNK)
            for s in range(hp):
                cs = slice(s * dh, (s + 1) * dh)
                q, k, v, kt = q_ref[sl, cs], k_ref[sl, cs] * scale, v_ref[sl, cs], kt_ref[cs, sl]
                li_c, b_c, li_r, b_r = _gate_views(li_ref, b_ref, gbt_ref, s, step * hp + s, heads, sl)
                ct, nrow, m_prev = ct_ref[s], n_ref[s, 0:1, :], m_ref[s, 0:1, 0:1]
                ct_out[s, ci] = ct
                nm_out[s, ci] = jnp.where(tile_row == 0, nrow, jnp.where(tile_row == 1, m_prev, 0.0))
                dw, iw, mt, _, sm = _chunk_scores(q, k, b_c, li_r, b_r, m_prev)
                iq = iw * q
                num = _dot(jnp.concatenate([iq, sm], axis=1), jnp.concatenate([ct, v], axis=0), NN)
                den = jnp.sum(iq * nrow + sm, axis=-1, keepdims=True)
                h = num / jnp.maximum(jnp.abs(den), jnp.exp(-mt))
                y, _ = _head_out(h, o_ref[sl, cs], g_ref[:, cs])
                out_ref[sl, cs] = y.astype(out_ref.dtype)
                ct_new, n_new, m_new, _, _, _ = _chunk_state(kt, k, v, li_c, b_c, ct, nrow, m_prev)
                ct_ref[s] = ct_new
                n_ref[s, 0:1, :] = n_new
                m_ref[s, 0:1, 0:1] = m_new
            return carry

        lax.fori_loop(0, nc, chunk, 0, unroll=2)

    col = lambda off: pl.BlockSpec((t, hp * dh), lambda h: (0, off // hp + h))
    slab = lambda off: pl.BlockSpec((hp, t, LANES), lambda h: (off // hp + h, 0, 0))
    return pl.pallas_call(
        body, name=name,
        out_shape=[jax.ShapeDtypeStruct((t, d_model), BF16), jax.ShapeDtypeStruct((heads, nc, dh, dh), F32),
                   jax.ShapeDtypeStruct((heads, nc, 8, LANES), F32)],
        grid=(heads // hp,),
        in_specs=[col(0), col(0), pl.BlockSpec((hp * dh, t), lambda h: (h, 0)), col(2 * heads), col(3 * heads),
                  slab(0), slab(heads), pl.BlockSpec((2 * heads, 1, t), lambda h: (0, 0, 0)),
                  pl.BlockSpec((1, hp * dh), lambda h: (0, h)), ANY_SPEC],
        out_specs=[pl.BlockSpec((t, hp * dh), lambda h: (0, h)),
                   pl.BlockSpec((hp, nc, dh, dh), lambda h: (h, 0, 0, 0)),
                   pl.BlockSpec((hp, nc, 8, LANES), lambda h: (h, 0, 0, 0))],
        scratch_shapes=[pltpu.VMEM((hp, dh, dh), F32), pltpu.VMEM((hp, 8, dh), F32), pltpu.VMEM((hp, 8, LANES), F32)],
        compiler_params=_params("arbitrary"),
    )(qc, kc, kct, proj_a, proj_a, gcols, gcols, gbt, g_mlstm, dep)


def _mlstm_bwd(qc, kc, qct, kct, proj_a, gcols, gbt, g_mlstm, states, dhcat, heads, name, dep):
    t = qc.shape[0]
    wm = g_mlstm.shape[1]
    dh = wm // heads
    nc = t // CHUNK
    scale = dh ** -0.5

    def body(q_ref, k_ref, qt_ref, kt_ref, v_ref, o_ref, li_ref, b_ref, gbt_ref, g_ref, ct_all, nm_all, dy_ref, dep_ref,
             dq_ref, dk_ref, dv_ref, do_ref, dcol_ref, drow_ref, dg_ref, dct_ref, dn_ref):
        del dep_ref
        head = pl.program_id(0)
        g = g_ref[...]

        @pl.when(head == 0)
        def _():
            dcol_ref[...] = jnp.zeros_like(dcol_ref)
            drow_ref[...] = jnp.zeros_like(drow_ref)

        def load(ci):
            sl = pl.ds(pl.multiple_of(ci * CHUNK, CHUNK), CHUNK)
            return (sl, q_ref[sl, :], k_ref[sl, :] * scale, v_ref[sl, :], kt_ref[:, sl]) \
                + _gate_views(li_ref, b_ref, gbt_ref, 0, head, heads, sl)

        dct_ref[...] = jnp.zeros_like(dct_ref)
        dn_ref[...] = jnp.zeros_like(dn_ref)
        dg_ref[...] = jnp.zeros_like(dg_ref)
        rows, cols = _chunk_iotas()
        last = lax.broadcasted_iota(jnp.int32, (CHUNK, 1), 0) == CHUNK - 1
        lane = lax.broadcasted_iota(jnp.int32, (CHUNK, LANES), 1)

        def bwd(step, carry):
            ci = nc - 1 - step
            sl, q, k, v, kt, li_c, b_c, li_r, b_r = load(ci)
            qt = qt_ref[:, sl]
            ct, nrow, m_prev = ct_all[ci], nm_all[ci, 0:1, :], nm_all[ci, 1:2, 0:1]
            dw, iw, mt, a, sm = _chunk_scores(q, k, b_c, li_r, b_r, m_prev)
            qc_ = _dot(q, ct, NN)
            qn = jnp.sum(q * nrow, axis=-1, keepdims=True)
            num = iw * qc_ + _dot(sm, v, NN)
            den = iw * qn + jnp.sum(sm, axis=-1, keepdims=True)
            floor = jnp.exp(-mt)
            dn = jnp.maximum(jnp.abs(den), floor)
            h = num / dn
            _, (cell, hh, r, sig) = _head_out(h, o_ref[sl, :], g)
            dy = dy_ref[sl, :]
            dcell = dy * sig
            do_ref[sl, :] = (dy * cell * sig * (1.0 - sig)).astype(do_ref.dtype)
            dh_, dg = _rms_b(dcell, g, hh, r)
            dg_ref[...] += dg
            dnum = dh_ / dn
            ddn = -jnp.sum(dh_ * h, axis=-1, keepdims=True) / dn
            dden = jnp.where(jnp.abs(den) >= floor, ddn * jnp.sign(den), 0.0)
            idn = iw * dnum
            idd = iw * dden
            dw_t = jnp.exp(jnp.where(cols >= rows, b_r - b_c + li_c, -jnp.inf) - _to_row(mt))
            sm_t = _dot(k, q, NT) * dw_t
            dsm = _dot(dnum, v, NT) + dden
            da_ = dsm * dw
            da_t = (_dot(v, dnum, NT) + _to_row(dden)) * dw_t
            e = da_ * a
            dq = _dot(idn, ct, NT) + _dot(da_, k, NN) + idd * nrow
            dk = _dot(da_t, q, NN)
            dv = _dot(sm_t, dnum, NN)
            dct = _dot(qt, idn, NN)
            dnr = jnp.sum(q * idd, axis=0, keepdims=True)
            db = jnp.sum(e + idn * qc_, axis=-1, keepdims=True) + idd * qn
            _, _, _, dec, w, wv = _chunk_state(kt, k, v, li_c, b_c, ct, nrow, m_prev)
            dct_new, dn_new = dct_ref[...], dn_ref[0:1, :]
            ddec = jnp.sum(jnp.sum(dct_new * ct, axis=-1, keepdims=True), axis=0, keepdims=True) \
                + jnp.sum(dn_new * nrow, axis=-1, keepdims=True)
            dk = dk + _dot(wv, dct_new, NT) + w * dn_new
            dwv = _dot(k, dct_new, NN)
            dv = dv + w * dwv
            dwt = jnp.sum(dwv * v + k * dn_new, axis=-1, keepdims=True)
            da = dwt * w
            dbtot = jnp.sum(da, axis=0, keepdims=True) + ddec * dec
            db = db - da + jnp.where(last, dbtot, 0.0)
            dct_ref[...] = dec * dct_new + dct
            dn_ref[0:1, :] = dec * dn_new + dnr
            dq_ref[sl, :] = dq
            dk_ref[sl, :] = dk * scale
            dv_ref[sl, :] = dv.astype(dv_ref.dtype)
            dcol_ref[sl, :] += jnp.where(lane == head, da, 0.0) + jnp.where(lane == heads + head, db, 0.0)
            drow_ref[head, :, sl] = jnp.sum(e, axis=0, keepdims=True)
            return carry

        lax.fori_loop(0, nc, bwd, 0, unroll=8)

    col = lambda off: pl.BlockSpec((t, dh), lambda h: (0, off + h))
    rowb = pl.BlockSpec((dh, t), lambda h: (h, 0))
    slab = lambda off: pl.BlockSpec((1, t, LANES), lambda h: (off + h, 0, 0))
    vec = pl.BlockSpec((1, dh), lambda h: (0, h))
    return pl.pallas_call(
        body, name=name,
        out_shape=[jax.ShapeDtypeStruct((t, wm), F32)] * 2 + [jax.ShapeDtypeStruct((t, wm), BF16)] * 2
        + [jax.ShapeDtypeStruct((t, LANES), F32), jax.ShapeDtypeStruct((heads, 1, t), F32),
           jax.ShapeDtypeStruct((1, wm), F32)],
        grid=(heads,),
        in_specs=[col(0), col(0), rowb, rowb, col(2 * heads), col(3 * heads), slab(0), slab(heads),
                  pl.BlockSpec((2 * heads, 1, t), lambda h: (0, 0, 0)), vec,
                  pl.BlockSpec((None, nc, dh, dh), lambda h: (h, 0, 0, 0)),
                  pl.BlockSpec((None, nc, 8, LANES), lambda h: (h, 0, 0, 0)), col(0), ANY_SPEC],
        out_specs=[col(0), col(0), col(0), col(0), pl.BlockSpec((t, LANES), lambda h: (0, 0)),
                   pl.BlockSpec((heads, 1, t), lambda h: (0, 0, 0)), vec],
        scratch_shapes=[pltpu.VMEM((dh, dh), F32), pltpu.VMEM((8, dh), F32)],
        compiler_params=_params("arbitrary"),
    )(qc, kc, qct, kct, proj_a, proj_a, gcols, gcols, gbt, g_mlstm, *states, dhcat, dep)


def _pool_window(gi):
    return jnp.where(gi == 0, 2.0, jnp.where(gi == 1, 4.0, jnp.where(gi == 2, 8.0, 16.0))).astype(F32)


def _pool_select(gi, levels):
    return jnp.where(gi == 0, levels[0], jnp.where(gi == 1, levels[1], jnp.where(gi == 2, levels[2], levels[3])))


def _pooled(u, gi, rows):
    s, levels = u, []
    for lvl in range(POOL_GROUPS):
        s = s + _shift_down(s, 1 << lvl, rows)
        levels.append(s)
    cnt = jnp.minimum((rows + 1).astype(F32), _pool_window(gi))
    return _pool_select(gi, levels) / cnt - u, cnt


def _pool_fwd(proj_b, w_pool, scale, hcat, name):
    t, d_model = hcat.shape
    g, cg, _ = w_pool.shape
    off = (d_model - g * cg) // cg

    def body(u_ref, w_ref, s_ref, hcat_ref, out_ref):
        del hcat_ref
        gi = pl.program_id(0)
        u = u_ref[...]
        rows = lax.broadcasted_iota(jnp.int32, u.shape, 0)
        pooled, _ = _pooled(u, gi, rows)
        out_ref[...] = (_dot(pooled, w_ref[...], NN) * s_ref[...]).astype(out_ref.dtype)

    return pl.pallas_call(
        body, name=name,
        out_shape=jax.ShapeDtypeStruct((t, d_model), BF16),
        grid=(g,),
        in_specs=[pl.BlockSpec((t, cg), lambda gi: (0, gi)), pl.BlockSpec((None, cg, cg), lambda gi: (gi, 0, 0)),
                  pl.BlockSpec((1, cg), lambda gi: (0, gi)), HBM_SPEC],
        out_specs=pl.BlockSpec((t, cg), lambda gi: (0, off + gi)),
        input_output_aliases={3: 0},
        compiler_params=_params("arbitrary"),
    )(proj_b, w_pool, scale, hcat)


def _pool_bwd(proj_b, w_pool, scale, dhcat, name):
    t, d_model = dhcat.shape
    g, cg, _ = w_pool.shape
    off = (d_model - g * cg) // cg

    def body(u_ref, w_ref, s_ref, dy_ref, du_ref, dw_ref, ds_ref):
        gi = pl.program_id(0)
        u, w, dy = u_ref[...], w_ref[...], dy_ref[...]
        rows = lax.broadcasted_iota(jnp.int32, u.shape, 0)
        pooled, cnt = _pooled(u, gi, rows)
        mixed = _dot(pooled, w, NN)
        ds_ref[...] = jnp.sum(dy * mixed, axis=0, keepdims=True)
        dmixed = dy * s_ref[...]
        dw_ref[...] = _dot(pooled, dmixed, TN).astype(dw_ref.dtype)
        dpooled = _dot(dmixed, w, NT)
        s, levels = dpooled / cnt, []
        for lvl in range(POOL_GROUPS):
            s = s + _shift_up(s, 1 << lvl, rows)
            levels.append(s)
        du_ref[...] = (_pool_select(gi, levels) - dpooled).astype(du_ref.dtype)

    return pl.pallas_call(
        body, name=name,
        out_shape=[jax.ShapeDtypeStruct((t, g * cg), BF16), jax.ShapeDtypeStruct((g, cg, cg), BF16),
                   jax.ShapeDtypeStruct((1, g * cg), F32)],
        grid=(g,),
        in_specs=[pl.BlockSpec((t, cg), lambda gi: (0, gi)), pl.BlockSpec((None, cg, cg), lambda gi: (gi, 0, 0)),
                  pl.BlockSpec((1, cg), lambda gi: (0, gi)), pl.BlockSpec((t, cg), lambda gi: (0, off + gi))],
        out_specs=[pl.BlockSpec((t, cg), lambda gi: (0, gi)), pl.BlockSpec((None, cg, cg), lambda gi: (gi, 0, 0)),
                   pl.BlockSpec((1, cg), lambda gi: (0, gi))],
        compiler_params=_params("arbitrary"),
    )(proj_b, w_pool, scale, dhcat)


def _pad_cols(a, width):
    return jnp.pad(a, ((0, 0), (0, width - a.shape[1])))


def kernel(x, p, w_in, b_gates, w_qk_conv, g_mlstm, w_pool, pool_scale, w_out, g_mix_pre, g_mix_post, w_ff1, w_ff2, g_ff_pre, g_ff_post, w_ple_proj, w_ple_gate, g_ple_gate, g_ple_post, loss_target, m_w_in, m_b_gates, m_w_qk_conv, m_g_mlstm, m_w_pool, m_pool_scale, m_w_out, m_g_mix_pre, m_g_mix_post, m_w_ff1, m_w_ff2, m_g_ff_pre, m_g_ff_post, m_w_ple_proj, m_w_ple_gate, m_g_ple_gate, m_g_ple_post, v_w_in, v_b_gates, v_w_qk_conv, v_g_mlstm, v_w_pool, v_pool_scale, v_w_out, v_g_mix_pre, v_g_mix_post, v_w_ff1, v_w_ff2, v_g_ff_pre, v_g_ff_post, v_w_ple_proj, v_w_ple_gate, v_g_ple_gate, v_g_ple_post):
    weights = dict(w_in=w_in, b_gates=b_gates, w_qk_conv=w_qk_conv, g_mlstm=g_mlstm, w_pool=w_pool,
                   pool_scale=pool_scale, w_out=w_out, g_mix_pre=g_mix_pre, g_mix_post=g_mix_post, w_ff1=w_ff1,
                   w_ff2=w_ff2, g_ff_pre=g_ff_pre, g_ff_post=g_ff_post, w_ple_proj=w_ple_proj,
                   w_ple_gate=w_ple_gate, g_ple_gate=g_ple_gate, g_ple_post=g_ple_post)
    mom1 = dict(w_in=m_w_in, b_gates=m_b_gates, w_qk_conv=m_w_qk_conv, g_mlstm=m_g_mlstm, w_pool=m_w_pool,
                pool_scale=m_pool_scale, w_out=m_w_out, g_mix_pre=m_g_mix_pre, g_mix_post=m_g_mix_post,
                w_ff1=m_w_ff1, w_ff2=m_w_ff2, g_ff_pre=m_g_ff_pre, g_ff_post=m_g_ff_post,
                w_ple_proj=m_w_ple_proj, w_ple_gate=m_w_ple_gate, g_ple_gate=m_g_ple_gate, g_ple_post=m_g_ple_post)
    mom2 = dict(w_in=v_w_in, b_gates=v_b_gates, w_qk_conv=v_w_qk_conv, g_mlstm=v_g_mlstm, w_pool=v_w_pool,
                pool_scale=v_pool_scale, w_out=v_w_out, g_mix_pre=v_g_mix_pre, g_mix_post=v_g_mix_post,
                w_ff1=v_w_ff1, w_ff2=v_w_ff2, g_ff_pre=v_g_ff_pre, g_ff_post=v_g_ff_post,
                w_ple_proj=v_w_ple_proj, w_ple_gate=v_w_ple_gate, g_ple_gate=v_g_ple_gate, g_ple_post=v_g_ple_post)
    order = list(weights)

    t, d = x.shape[1], x.shape[2]
    heads = MLSTM_HEADS
    wm = g_mlstm.shape[1]
    pw = pool_scale.shape[1]
    cg = pw // POOL_GROUPS
    in_cols = 4 * wm + 2 * heads + pw
    in_shard = w_in.shape[2]
    xs, ps, tgt = x[0], p[0, 0], loss_target[0]

    mx, my, mc = lax.axis_index("x"), lax.axis_index("y"), lax.axis_index("c")
    core = jnp.reshape(mc, (1,)).astype(jnp.int32)
    chip = jnp.reshape(2 * mx + my, (1,)).astype(jnp.int32)

    first = [w_in[0].T.astype(BF16), w_qk_conv[0], w_pool[0].reshape(POOL_GROUPS * w_pool.shape[2], cg).astype(BF16)]
    started, token = _gather_start([[w_ple_proj[0].astype(BF16)], first], "gather_start_in")
    anchor = token[0, 0]
    later = lambda w: (w[0] + anchor).astype(BF16)
    ff2_shard = later(w_ff2)
    rest, token = _gather_start([[later(w_out)], [later(w_ff1)], [ff2_shard[:, :d // 2]], [ff2_shard[:, d // 2:]],
                                 [later(w_ple_gate)]], "gather_start_rest")
    started = dict(zip(["ple_proj", "in", "out", "ff1", "ff2_lo", "ff2_hi", "ple_gate"], started + rest))
    bias = _pad_cols(b_gates, LANES)
    small = ["b_gates", "g_mlstm", "pool_scale", "g_mix_pre", "g_mix_post", "g_ff_pre", "g_ff_post", "g_ple_gate",
             "g_ple_post"]
    pack = lambda src: _pad_rows16(jnp.concatenate([_pad_cols(src[k] + anchor, d) for k in small], axis=0))
    small_state = [pack(weights), pack(mom1), pack(mom2)]
    in_state = [a[0].T + anchor for a in (w_in, m_w_in, v_w_in)]

    xn = _norm_fwd(xs, g_mix_pre, "norm_mix_pre", deps=[token])
    (g_pproj,) = _forward_pair(_gather_wait(started["ple_proj"], xn, "gather_wait_ple_proj"), "gather_pair_ple_proj")
    e = _mm_nn_pieces("ple_proj", ps, g_pproj)[0]
    g_in, g_conv, g_pool = _forward_pair(
        _gather_wait(started["in"], [e] + small_state + in_state, "gather_wait_in"), "gather_pair_in")
    in_t_a = g_in.reshape(in_cols, d)
    in_t_b = jnp.concatenate([in_t_a[4 * wm + 2 * heads:], in_t_a[4 * wm:4 * wm + 2 * heads],
                              jnp.zeros((LANES - 2 * heads, d), BF16)], axis=0)
    conv_w = g_conv.transpose(1, 0, 2).reshape(QK_CONV, 2 * wm)
    pool_w = g_pool.reshape(N_DEV, POOL_GROUPS, cg // N_DEV, cg).transpose(1, 0, 2, 3).reshape(POOL_GROUPS, cg, cg)
    proj_a = _mm_nt("proj_qkvo", xn, in_t_a, n_rows=4 * wm)
    proj_b = _mm_nt("proj_pool_gates", xn, in_t_b)
    qc, qct = _conv_fwd(proj_a, conv_w[:, :wm], 0, "q_conv")
    kc, kct = _conv_fwd(proj_a, conv_w[:, wm:], wm, "k_conv", scale_t=(wm // heads) ** -0.5)
    gb, gbt, gcols = _gates_fwd(proj_b, bias, pw // LANES, heads, "gates")
    gbt = gbt[:2 * heads].reshape(2 * heads, 1, t)
    fwd_out = _forward_start(_gather_wait(started["out"], gb, "gather_wait_out"), "gather_pair_start_out")
    hcat, *mlstm_states = _mlstm_fwd(qc, kc, kct, proj_a, gcols, gbt, g_mlstm, heads, d, "mlstm", fwd_out["token"])
    hcat = _pool_fwd(proj_b, pool_w, pool_scale, hcat, "pool")
    (g_out,) = _forward_wait(fwd_out, hcat, "gather_pair_wait_out")
    out_w = g_out.reshape(d, d)
    mix = _mm_nn("mix_out", hcat, out_w)[0]
    fwd_ff1 = _forward_start(_gather_wait(started["ff1"], mix, "gather_wait_ff1"), "gather_pair_start_ff1")
    h1, hn = _block_fwd(xs, mix, g_mix_post, g_ff_pre, "residual_mix", deps=[fwd_ff1["token"]])
    (g_ff1,) = _forward_wait(fwd_ff1, hn, "gather_pair_wait_ff1")
    relu_a, act = _mm_nn_pieces("ff1", hn, g_ff1, out_dtype=[BF16, BF16],
                                epilogue=lambda acc: (jnp.maximum(acc, 0.0), jnp.square(jnp.maximum(acc, 0.0))))
    (g_ff2_lo,) = _forward_pair(_gather_wait(started["ff2_lo"], act, "gather_wait_ff2_lo"), "gather_pair_ff2_lo")
    ff2_lo = g_ff2_lo.reshape(-1, d // 2)
    ff = _mm_nn("ff2_lo", act, ff2_lo, tn=256, tk=4 * d, out_cols=d)[0]
    (g_ff2_hi,) = _forward_pair(_gather_wait(started["ff2_hi"], ff, "gather_wait_ff2_hi"), "gather_pair_ff2_hi")
    ff2_hi = g_ff2_hi.reshape(-1, d // 2)
    ff = _mm_nn("ff2_hi", act, ff2_hi, tn=256, tk=4 * d, out_cols=d, out_at=d // 2, into=ff)[0]
    fwd_ple = _forward_start(_gather_wait(started["ple_gate"], ff, "gather_wait_ple"), "gather_pair_start_ple")
    h2, hg = _block_fwd(h1, ff, g_ff_post, g_ple_gate, "residual_ff", deps=[fwd_ple["token"]])
    (g_pgate,) = _forward_wait(fwd_ple, hg, "gather_pair_wait_ple")
    pgate_w = g_pgate.reshape(d, d)
    z = _mm_nn("ple_gate", hg, pgate_w)[0]
    dh3, dz, de, dg_ple_post, loss_part = _loss_head(h2, z, e, tgt, g_ple_post, "loss_head")
    loss = lax.psum(loss_part[0, 0], MESH_AXES)

    out = {}

    def state(k):
        if k == "w_in":
            return in_state
        return [a.reshape(-1, a.shape[-1]) for a in (weights[k], mom1[k], mom2[k])]

    def pair_begin(pieces, tag):
        keys = list(pieces)
        return keys, _pair_start([pieces[k] for k in keys], "reduce_pair_start_" + tag)

    def chip_begin(group, after, tag):
        keys, begun = group
        mine, got = _pair_wait(begun, after, "reduce_pair_wait_" + tag)
        sums = [_pair_sum(p_, g, core, "pair_sum_" + k) for k, p_, g in zip(keys, mine, got)]
        return keys, _chip_start(sums, "reduce_start_" + tag)

    def reduce_finish(group, after, tag):
        keys, begun = group
        sums, got = _chip_wait(begun, after, "reduce_wait_" + tag)
        for k, s, g in zip(keys, sums, got):
            res = _adamw_shard(*state(k), s, g, chip, "adamw_" + k)
            out[k] = [r.T[None] if k == "w_in" else r.reshape(weights[k].shape) for r in res]

    token_of = lambda group: group[1]["token"]
    dhg = _mm_nt("d_ple_gate_in", dz, pgate_w)
    gw_pgate = _mm_tn("gw_ple_gate", hg, dz)
    gw_pproj = _mm_tn("gw_ple_proj", ps, de, pieces=N_DEV)
    dh2, dff, dg_ff_post, dg_ple_gate = _block_bwd(h2, ff, dh3, dhg, g_ff_post, g_ple_gate, "d_residual_ff")
    da = _mm_nt("d_ff2_in", dff, ff2_lo, b2=ff2_hi, out_dtype=BF16, extra=[relu_a],
                epilogue=lambda acc, r: (acc * (2.0 * r.astype(F32)),))
    gw_ff2 = _mm_tn("gw_ff2", act, dff)
    ff2_keys = ["w_ff2", "w_ple_gate", "w_ple_proj"]
    pair_ff2 = pair_begin({"w_ff2": gw_ff2.reshape(N_DEV, -1, d),
                           "w_ple_gate": gw_pgate.reshape(N_DEV, d // N_DEV, d), "w_ple_proj": gw_pproj}, "ff2")
    dhn = _mm_nt_pieces("d_ff1_in", da, g_ff1, deps=[token_of(pair_ff2)])
    place = jnp.concatenate([core, chip])
    mine_ff2, got_ff2 = _pair_wait(pair_ff2[1], dhn, "reduce_pair_wait_ff2")
    gw_ff1, sums_ff2 = _mm_tn("gw_ff1", hn, da, pieces=N_DEV, side=_side_pair_sum(mine_ff2[0], got_ff2[0], place))
    sums_ple = [_pair_sum(p_, g, core, "pair_sum_" + k) for k, p_, g in zip(ff2_keys[1:], mine_ff2[1:], got_ff2[1:])]
    red_ff2 = (ff2_keys, _chip_start([sums_ff2] + sums_ple, "reduce_start_ff2"))
    pair_ff1 = pair_begin({"w_ff1": gw_ff1}, "ff1")
    dh1, dmix, dg_mix_post, dg_ff_pre = _block_bwd(h1, mix, dh2, dhn, g_mix_post, g_ff_pre, "d_residual_mix",
                                                   deps=[token_of(pair_ff1), token_of(red_ff2)])
    mine_ff1, got_ff1 = _pair_wait(pair_ff1[1], dmix, "reduce_pair_wait_ff1")
    dhcat, sums_ff1 = _mm_nt("d_mix_in", dmix, out_w, tm=t // 2, tn=d // 2,
                             side=_side_pair_sum(mine_ff1[0], got_ff1[0], place))
    red_ff1 = (["w_ff1"], _chip_start([sums_ff1], "reduce_start_ff1"))
    gw_out = _mm_tn("gw_out", hcat, dmix, deps=[token_of(red_ff1)])
    du, gw_pool, dg_pool_scale = _pool_bwd(proj_b, pool_w, pool_scale, dhcat, "d_pool")
    pair_out = pair_begin({
        "w_out": gw_out.reshape(N_DEV, d // N_DEV, d),
        "w_pool": gw_pool.reshape(POOL_GROUPS, N_DEV, cg // N_DEV, cg).transpose(1, 0, 2, 3).reshape(N_DEV, -1, cg)},
        "out")
    dq, dk, dv, do, dcol, drow, dg_mlstm = _mlstm_bwd(qc, kc, qct, kct, proj_a, gcols, gbt, g_mlstm, mlstm_states, dhcat,
                                                      heads, "d_mlstm", token_of(pair_out))
    red_out = chip_begin(pair_out, dq, "out")
    drow = jnp.pad(drow.reshape(heads, t), ((0, LANES - heads), (0, 0)))
    dgates, dbias = _gates_bwd(proj_b, bias, dcol, drow, pw // LANES, heads, "d_gates")
    dq_pre, gw_conv_q = _conv_bwd(proj_a, conv_w[:, :wm], dq, 0, "d_q_conv")
    dk_pre, gw_conv_k = _conv_bwd(proj_a, conv_w[:, wm:], dk, wm, "d_k_conv")
    gw_conv = jnp.concatenate([gw_conv_q, gw_conv_k], axis=1)
    dproj_a = jnp.concatenate([dq_pre, dk_pre, dv, do], axis=1)
    dproj_b = jnp.concatenate([du, dgates], axis=1)
    sums_ff2, got_ff2 = _chip_wait(red_ff2[1], token_of(red_out), "reduce_wait_ff2")
    gw_a, *update = _mm_tn("gw_in_qkvo", dproj_a, xn,
                           side=_side_adamw(*state("w_ff2"), sums_ff2[0], got_ff2[0], place))
    out["w_ff2"] = [r.reshape(weights["w_ff2"].shape) for r in update]
    for k, s, g in zip(ff2_keys[1:], sums_ff2[1:], got_ff2[1:]):
        out[k] = [r.reshape(weights[k].shape) for r in _adamw_shard(*state(k), s, g, chip, "adamw_" + k)]
    gw_b = _mm_tn("gw_in_pool_gates", dproj_b, xn)
    gw_in_t = jnp.concatenate([gw_a, gw_b[pw:pw + 2 * heads], gw_b[:pw]], axis=0)
    pair_in = pair_begin({
        "w_in": gw_in_t.reshape(N_DEV, in_shard, d),
        "w_qk_conv": gw_conv.reshape(QK_CONV, N_DEV, -1).transpose(1, 0, 2).astype(BF16)}, "in")
    reduce_finish(red_out, token_of(pair_in), "out")
    red_in = chip_begin(pair_in, out["w_out"][0], "in")
    sums_ff1, got_ff1 = _chip_wait(red_ff1[1], token_of(red_in), "reduce_wait_ff1")
    dxn_a, *update = _mm_nn("d_proj_qkvo_in", dproj_a, in_t_a, tn=512, tk=4 * wm, deps=[token_of(red_in)],
                            side=_side_adamw(*state("w_ff1"), sums_ff1[0], got_ff1[0], place))
    out["w_ff1"] = [r.reshape(weights["w_ff1"].shape) for r in update]
    dxn = _mm_nn("d_proj_in", dproj_b, in_t_b, extra=[dxn_a], epilogue=lambda acc, prev: (acc + prev,))[0]
    grad_x, dg_mix_pre = _norm_bwd(xs, dxn, dh1, g_mix_pre, "d_norm_mix_pre")

    small_grads = dict(b_gates=dbias[:, :2 * heads], g_mlstm=dg_mlstm, pool_scale=dg_pool_scale, g_mix_pre=dg_mix_pre,
                       g_mix_post=dg_mix_post, g_ff_pre=dg_ff_pre, g_ff_post=dg_ff_post, g_ple_gate=dg_ple_gate,
                       g_ple_post=dg_ple_post)
    small_begun = _everyone_start([pack(small_grads)], "small_grads_start")
    reduce_finish(red_in, small_begun["token"], "in")
    (small_parts,) = _everyone_wait(small_begun, out["w_in"][0], "small_grads_wait")
    res = _adamw_replicated(*small_state, small_parts, "adamw_replicated")
    for i, k in enumerate(small):
        width = weights[k].shape[1]
        out[k] = [r[i:i + 1, :width] for r in res]

    return (loss, grad_x[None], *[out[k][0] for k in order], *[out[k][1] for k in order],
            *[out[k][2] for k in order], *[out[k][3] for k in order])


def _pad_rows16(a):
    return jnp.pad(a, ((0, 16 - a.shape[0]), (0, 0)))
```

```python
import jax
import jax.numpy as jnp
from jax import lax
from jax.experimental import pallas as pl
from jax.experimental.pallas import tpu as pltpu

F32 = jnp.float32
BF16 = jnp.bfloat16
EPS = 1e-6
N_DEV = 8
N_CHIP = 4
LANES = 128
VMEM_LIMIT = 56 * 1024 * 1024
MLSTM_HEADS = 8
POOL_GROUPS = 4
QK_CONV = 4
CHUNK = 128
ROW_TILE = 256
ADAM_LR, ADAM_B1, ADAM_B2, ADAM_EPS, ADAM_WD, ADAM_STEP = 0.001, 0.9, 0.999, 1e-08, 0.01, 10
MESH_AXES = ("x", "y", "c")
MESH = pl.DeviceIdType.MESH
HBM_SPEC = pl.BlockSpec(memory_space=pltpu.HBM)
SEM_SPEC = pl.BlockSpec(memory_space=pltpu.SEMAPHORE)
DATAFLOW = pltpu.SideEffectType.DATAFLOW_SIDE_EFFECTING


def _params(*sem):
    if sem:
        return pltpu.CompilerParams(dimension_semantics=sem, vmem_limit_bytes=VMEM_LIMIT)
    return pltpu.CompilerParams(vmem_limit_bytes=VMEM_LIMIT)


def _tile(n, pref):
    if n <= pref:
        return n
    t = (pref // LANES) * LANES
    while t >= LANES:
        if n % t == 0:
            return t
        t -= LANES
    return n


def _hbm(a):
    return pltpu.with_memory_space_constraint(a, pltpu.HBM)


def _mesh_place():
    x, y, c = lax.axis_index("x"), lax.axis_index("y"), lax.axis_index("c")
    return x, y, c, [(1 - x, y), (x, 1 - y), (1 - x, 1 - y)]


def _chip_copies(srcs, lands, send, recv):
    x, y, c, chips = _mesh_place()
    return [pltpu.make_async_remote_copy(
        src_ref=srcs[w].at[2 * chip[0] + chip[1]], dst_ref=lands[w].at[k],
        send_sem=send.at[3 * w + k], recv_sem=recv.at[3 * w + k],
        device_id=(*chip, c), device_id_type=MESH)
        for k, chip in enumerate(chips) for w in range(len(srcs))]


def _chip_start(sums, name):
    return _split_start(_chip_copies, sums, [lax.empty((3,) + a.shape[1:], a.dtype) for a in sums], 3, name,
                        peers=_same_core_of_other_chips)


def _chip_wait(started, after, name):
    return _split_wait(_chip_copies, started, after, name)


def _pair_start(pieces, name):
    return _split_start(_pair_copies, pieces, [lax.empty((N_CHIP,) + a.shape[1:], a.dtype) for a in pieces], N_CHIP, name,
                        peers=_sibling)


def _pair_wait(started, after, name):
    return _split_wait(_pair_copies, started, after, name)


def _pair_copies(srcs, lands, send, recv):
    x, y, c, _ = _mesh_place()
    return [pltpu.make_async_remote_copy(
        src_ref=srcs[w].at[2 * ch + 1 - c], dst_ref=lands[w].at[ch],
        send_sem=send.at[N_CHIP * w + ch], recv_sem=recv.at[N_CHIP * w + ch],
        device_id=(x, y, 1 - c), device_id_type=MESH)
        for w in range(len(srcs)) for ch in range(N_CHIP)]


OWN_BARRIER_IDS = {name: i for i, name in enumerate([
    "reduce_pair_start_ff2", "reduce_pair_start_ff1", "reduce_pair_start_out", "reduce_pair_start_in",
    "gather_pair_start_out", "gather_pair_start_ff1", "gather_pair_start_ple",
    "reduce_start_ff2", "reduce_start_ff1", "reduce_start_out", "reduce_start_in",
    "gather_start_in", "gather_start_rest", "gather_pair_in", "gather_pair_ff2_lo", "gather_pair_ff2_hi",
    "gather_pair_ple_proj"])}


def _shake_hands(peers):
    barrier = pltpu.get_barrier_semaphore()
    for peer in peers:
        pl.semaphore_signal(barrier, inc=1, device_id=peer, device_id_type=MESH)
    pl.semaphore_wait(barrier, len(peers))


def _sibling():
    x, y, c, _ = _mesh_place()
    return [(x, y, 1 - c)]


def _same_core_of_other_chips():
    x, y, c, chips = _mesh_place()
    return [(*chip, c) for chip in chips]


def _split_start(copies_of, srcs, lands, per_array, name, peers=None):
    ns, nb = len(srcs), len(srcs) + len(lands)
    n_copies = per_array * len(lands)

    def body(*refs):
        if peers is not None:
            _shake_hands(peers())
        for cp in copies_of(refs[:ns], refs[ns:nb], refs[nb], refs[nb + 1]):
            cp.start()
        refs[-1][...] = jnp.zeros_like(refs[-1])

    params = dict(has_side_effects=DATAFLOW)
    if peers is not None:
        params["collective_id"] = OWN_BARRIER_IDS[name]
    res = pl.pallas_call(
        body, name=name,
        out_shape=[pltpu.SemaphoreType.DMA((n_copies,)), pltpu.SemaphoreType.DMA((n_copies,))]
        + [pltpu.HBM(a.shape, a.dtype) for a in list(srcs) + list(lands)] + [jax.ShapeDtypeStruct((8, LANES), F32)],
        in_specs=[HBM_SPEC] * nb,
        out_specs=[SEM_SPEC, SEM_SPEC] + [HBM_SPEC] * nb + [pl.BlockSpec(memory_space=pltpu.VMEM)],
        input_output_aliases={i: 2 + i for i in range(nb)},
        compiler_params=pltpu.CompilerParams(**params),
    )(*[_hbm(a) for a in srcs], *[_hbm(a) for a in lands])
    return dict(send=res[0], recv=res[1], srcs=list(res[2:2 + ns]), lands=list(res[2 + ns:2 + nb]), token=res[-1])


def _split_wait(copies_of, started, after, name):
    ns = len(started["srcs"])
    nb = ns + len(started["lands"])

    def body(*refs):
        for cp in copies_of(refs[:ns], refs[ns:nb], refs[nb], refs[nb + 1]):
            cp.wait_send()
            cp.wait_recv()

    res = pl.pallas_call(
        body, name=name,
        out_shape=[pltpu.HBM(a.shape, a.dtype) for a in started["srcs"] + started["lands"]],
        in_specs=[HBM_SPEC] * nb + [SEM_SPEC, SEM_SPEC, ANY_SPEC],
        out_specs=[HBM_SPEC] * nb,
        input_output_aliases={i: i for i in range(nb)},
        compiler_params=pltpu.CompilerParams(has_side_effects=DATAFLOW),
    )(*started["srcs"], *started["lands"], started["send"], started["recv"], after)
    return list(res[:ns]), list(res[ns:])


def _forward_copies(srcs, lands, send, recv):
    del srcs
    x, y, c, chips = _mesh_place()
    copies = []
    for w in range(len(lands)):
        for j, chip in enumerate(chips):
            rows = lands[w].at[4 * chip[0] + 2 * chip[1] + c]
            copies.append(pltpu.make_async_remote_copy(
                src_ref=rows, dst_ref=rows, send_sem=send.at[3 * w + j], recv_sem=recv.at[3 * w + j],
                device_id=(x, y, 1 - c), device_id_type=MESH))
    return copies


def _forward_start(lands, name):
    return _split_start(_forward_copies, [], lands, 3, name, peers=_sibling)


def _forward_wait(started, after, name):
    return _split_wait(_forward_copies, started, after, name)[1]


def _everyone_copies(srcs, lands, send, recv):
    x, y, c, _ = _mesh_place()
    me = 4 * x + 2 * y + c
    copies = []
    for w in range(len(srcs)):
        for k in range(N_DEV - 1):
            flip = k + 1
            peer = (1 - x if flip & 4 else x, 1 - y if flip & 2 else y, 1 - c if flip & 1 else c)
            copies.append(pltpu.make_async_remote_copy(
                src_ref=srcs[w], dst_ref=lands[w].at[me], send_sem=send.at[7 * w + k], recv_sem=recv.at[7 * w + k],
                device_id=peer, device_id_type=MESH))
    return copies


def _everyone_start(blocks, name):
    me = 4 * lax.axis_index("x") + 2 * lax.axis_index("y") + lax.axis_index("c")
    lands = [lax.dynamic_update_slice(lax.empty((N_DEV,) + a.shape, a.dtype), a[None], (me,) + (0,) * a.ndim)
             for a in blocks]
    return _split_start(_everyone_copies, blocks, lands, N_DEV - 1, name)


def _everyone_wait(started, after, name):
    return _split_wait(_everyone_copies, started, after, name)[1]


def _gather_copies(srcs, lands, send, recv):
    x, y, c, chips = _mesh_place()
    me = 4 * x + 2 * y + c
    targets = [(x, y, 1 - c)] + [(*chip, c) for chip in chips]
    return [pltpu.make_async_remote_copy(
        src_ref=srcs[w], dst_ref=lands[w].at[me], send_sem=send.at[4 * w + k], recv_sem=recv.at[4 * w + k],
        device_id=to, device_id_type=MESH)
        for w in range(len(srcs)) for k, to in enumerate(targets)]


def _gather_start(groups, name):
    sizes = [len(g) for g in groups]
    flat = [a for g in groups for a in g]
    n, ng = len(flat), len(groups)
    me = 4 * lax.axis_index("x") + 2 * lax.axis_index("y") + lax.axis_index("c")
    lands = [lax.dynamic_update_slice(lax.empty((N_DEV,) + a.shape, a.dtype), a[None], (me,) + (0,) * a.ndim)
             for a in flat]

    def body(*refs):
        _shake_hands(_sibling() + _same_core_of_other_chips())
        srcs, zones = refs[:n], refs[n:2 * n]
        sems = refs[2 * n:2 * n + 2 * ng]
        token = refs[-1]
        lo = 0
        for gi, size in enumerate(sizes):
            for cp in _gather_copies(srcs[lo:lo + size], zones[lo:lo + size], sems[2 * gi], sems[2 * gi + 1]):
                cp.start()
            lo += size
        token[...] = jnp.zeros_like(token)

    sem_shapes = []
    for size in sizes:
        sem_shapes += [pltpu.SemaphoreType.DMA((4 * size,))] * 2
    res = pl.pallas_call(
        body, name=name,
        out_shape=sem_shapes + [pltpu.HBM(a.shape, a.dtype) for a in flat + lands] + [jax.ShapeDtypeStruct((8, LANES), F32)],
        in_specs=[HBM_SPEC] * (2 * n),
        out_specs=[SEM_SPEC] * (2 * ng) + [HBM_SPEC] * (2 * n) + [pl.BlockSpec(memory_space=pltpu.VMEM)],
        input_output_aliases={i: 2 * ng + i for i in range(2 * n)},
        compiler_params=pltpu.CompilerParams(has_side_effects=DATAFLOW, collective_id=OWN_BARRIER_IDS[name]),
    )(*[_hbm(a) for a in flat], *[_hbm(a) for a in lands])
    out, lo = [], 0
    for gi, size in enumerate(sizes):
        out.append(dict(send=res[2 * gi], recv=res[2 * gi + 1],
                        srcs=list(res[2 * ng + lo:2 * ng + lo + size]),
                        lands=list(res[2 * ng + n + lo:2 * ng + n + lo + size])))
        lo += size
    return out, res[-1]


def _gather_wait(started, after, name):
    n = len(started["srcs"])
    after = list(after) if isinstance(after, (list, tuple)) else [after]

    def body(*refs):
        srcs, zones = refs[:n], refs[n:2 * n]
        send, recv = refs[2 * n], refs[2 * n + 1]
        for cp in _gather_copies(srcs, zones, send, recv):
            cp.wait_send()
            cp.wait_recv()

    res = pl.pallas_call(
        body, name=name,
        out_shape=[pltpu.HBM(a.shape, a.dtype) for a in started["srcs"] + started["lands"]],
        in_specs=[HBM_SPEC] * (2 * n) + [SEM_SPEC, SEM_SPEC] + [ANY_SPEC] * len(after),
        out_specs=[HBM_SPEC] * (2 * n),
        input_output_aliases={i: i for i in range(2 * n)},
        compiler_params=pltpu.CompilerParams(has_side_effects=DATAFLOW),
    )(*started["srcs"], *started["lands"], started["send"], started["recv"], *after)
    return list(res[n:])


def _forward_pair(lands, name):
    n = len(lands)

    def body(*refs):
        _shake_hands(_sibling())
        zones = refs[:n]
        send, recv = refs[2 * n:]
        x, y, c, chips = _mesh_place()

        def copy(w, j, core):
            rows = zones[w].at[4 * chips[j][0] + 2 * chips[j][1] + core]
            return pltpu.make_async_remote_copy(
                src_ref=rows, dst_ref=rows, send_sem=send.at[3 * w + j], recv_sem=recv.at[3 * w + j],
                device_id=(x, y, 1 - c), device_id_type=MESH)

        sends = [copy(w, j, c) for w in range(n) for j in range(3)]
        for cp in sends:
            cp.start()
        for w in range(n):
            for j in range(3):
                copy(w, j, 1 - c).wait_recv()
        for cp in sends:
            cp.wait_send()

    return list(pl.pallas_call(
        body, name=name,
        out_shape=[jax.ShapeDtypeStruct(a.shape, a.dtype) for a in lands],
        in_specs=[HBM_SPEC] * n, out_specs=[HBM_SPEC] * n,
        input_output_aliases={i: i for i in range(n)},
        scratch_shapes=[pltpu.SemaphoreType.DMA((3 * n,)), pltpu.SemaphoreType.DMA((3 * n,))],
        compiler_params=pltpu.CompilerParams(collective_id=OWN_BARRIER_IDS[name]),
    )(*lands))


def _pair_sum(pieces, got, core, name):
    _, rows, cols = pieces.shape
    tr, tc = _tile_2d(rows, cols)

    def body(core_ref, mine_ref, got_ref, out_ref):
        del core_ref
        out_ref[...] = (mine_ref[...].astype(F32) + got_ref[...].astype(F32)).astype(out_ref.dtype)

    return pl.pallas_call(
        body, name=name,
        out_shape=jax.ShapeDtypeStruct((N_CHIP, rows, cols), pieces.dtype),
        grid_spec=pltpu.PrefetchScalarGridSpec(
            num_scalar_prefetch=1, grid=(N_CHIP, rows // tr, cols // tc),
            in_specs=[pl.BlockSpec((None, tr, tc), lambda ch, i, j, core: (2 * ch + core[0], i, j)),
                      pl.BlockSpec((None, tr, tc), lambda ch, i, j, core: (ch, i, j))],
            out_specs=pl.BlockSpec((None, tr, tc), lambda ch, i, j, core: (ch, i, j))),
        compiler_params=_params("parallel", "parallel", "parallel"),
    )(core, pieces, got)


def _tile_2d(rows, cols, budget=2 * 1024 * 1024):
    want = max(16, budget // (4 * cols))
    if rows <= want:
        return rows, cols
    t = (want // 16) * 16
    while t >= 16:
        if rows % t == 0:
            return t, cols
        t -= 16
    return rows, _tile(cols, max(LANES, budget // (4 * rows)))


def _adamw_math(w, g, m, v):
    m = ADAM_B1 * m + (1.0 - ADAM_B1) * g
    v = ADAM_B2 * v + (1.0 - ADAM_B2) * (g * g)
    m_hat = m / (1.0 - ADAM_B1 ** ADAM_STEP)
    v_hat = v / (1.0 - ADAM_B2 ** ADAM_STEP)
    delta = -ADAM_LR * (m_hat / (jnp.sqrt(v_hat) + ADAM_EPS) + ADAM_WD * w)
    return delta, m, v


def _adamw_shard(w, m, v, sums, got, chip, name):
    rows, cols = w.shape
    tr, tc = _tile_2d(rows, cols, budget=1024 * 1024)

    def body(chip_ref, w_ref, m_ref, v_ref, own_ref, got_ref, g_out, d_out, m_out, v_out):
        del chip_ref
        g = own_ref[...].astype(F32)
        for k in range(3):
            g = g + got_ref[k].astype(F32)
        delta, m_new, v_new = _adamw_math(w_ref[...], g, m_ref[...], v_ref[...])
        g_out[...] = g
        d_out[...] = delta
        m_out[...] = m_new
        v_out[...] = v_new

    blk = pl.BlockSpec((tr, tc), lambda i, j, chip: (i, j))
    return pl.pallas_call(
        body, name=name,
        out_shape=[jax.ShapeDtypeStruct((rows, cols), F32)] * 4,
        grid_spec=pltpu.PrefetchScalarGridSpec(
            num_scalar_prefetch=1, grid=(rows // tr, cols // tc),
            in_specs=[blk, blk, blk,
                      pl.BlockSpec((None, tr, tc), lambda i, j, chip: (chip[0], i, j)),
                      pl.BlockSpec((3, tr, tc), lambda i, j, chip: (0, i, j))],
            out_specs=[blk] * 4),
        compiler_params=_params("parallel", "parallel"),
    )(chip, w, m, v, sums, got)


def _adamw_replicated(w, m, v, parts, name):
    rows, cols = w.shape

    def body(w_ref, m_ref, v_ref, parts_ref, g_out, d_out, m_out, v_out):
        g = parts_ref[0]
        for d in range(1, N_DEV):
            g = g + parts_ref[d]
        delta, m_new, v_new = _adamw_math(w_ref[...], g, m_ref[...], v_ref[...])
        g_out[...] = g
        d_out[...] = delta
        m_out[...] = m_new
        v_out[...] = v_new

    return pl.pallas_call(
        body, name=name,
        out_shape=[jax.ShapeDtypeStruct((rows, cols), F32)] * 4,
        compiler_params=_params(),
    )(w, m, v, parts)


NN = ((1,), (0,))
NT = ((1,), (1,))
TN = ((0,), (0,))


ANY_SPEC = pl.BlockSpec(memory_space=pl.ANY)


def _mm(name, a, b, *, dims, grid, a_spec, b_spec, outs, extra=(), extra_specs=(), epilogue=None, acc_shape=None,
        deps=(), b_pieces=0, side=None, b2=None, into=None):
    nk = grid[2]
    n_extra, n_out = len(extra), len(outs)
    side_ins = side["ins"] if side else []
    side_outs = side["outs"] if side else []
    first_side = 2 + n_extra + len(deps)
    tail = ([b2] if b2 is not None else []) + ([into] if into is not None else [])
    first_out = first_side + len(side_ins) + len(tail)
    assert not side or nk == 1

    assert not side or (epilogue is None and n_out == 1 and not b_pieces and not tail)
    SIDE_SLICES = 4

    def body(*refs):
        dot = lambda lhs, rhs: lax.dot_general(lhs.astype(BF16), rhs.astype(BF16), (dims, ((), ())),
                                               preferred_element_type=F32)
        if side:
            refs = refs[1:]
            a_ref, b_ref, out_ref = refs[0], refs[1], refs[first_out]
            ins, sides = refs[first_side:first_out], refs[first_out + 1:first_out + 1 + len(side_outs)]
            rows = out_ref.shape[0] // SIDE_SLICES
            srows = sides[0].shape[-2] // SIDE_SLICES
            for part in range(SIDE_SLICES):
                rs = slice(part * srows, (part + 1) * srows)
                for val, o in zip(side["fn"](*[r[..., rs, :] for r in ins]), sides):
                    o[rs, :] = val.astype(o.dtype)
                ms = slice(part * rows, (part + 1) * rows)
                lhs = a_ref[:, ms] if dims == TN else a_ref[ms, :]
                out_ref[ms, :] = dot(lhs, b_ref[...]).astype(out_ref.dtype)
            return
        a_ref, b_ref = refs[0], refs[1]
        ex = refs[2:2 + n_extra]
        out_refs = refs[first_out:first_out + n_out]
        if b_pieces:
            ks = b_ref.shape[-1]
            part = dot(a_ref[:, 0:ks], b_ref[0])
            for piece in range(1, b_pieces):
                part = part + dot(a_ref[:, piece * ks:(piece + 1) * ks], b_ref[piece])
        elif b2 is not None:
            ks = b_ref.shape[-1]
            part = dot(a_ref[:, 0:ks], b_ref[...]) + dot(a_ref[:, ks:2 * ks], refs[first_side][...])
        else:
            part = dot(a_ref[...], b_ref[...])

        def finish(acc):
            res = epilogue(acc, *[e[...] for e in ex]) if epilogue else (acc,)
            for val, o in zip(res, out_refs):
                o[...] = val.astype(o.dtype)

        if nk == 1:
            finish(part)
        else:
            acc_ref = refs[-1]
            k = pl.program_id(2)

            @pl.when(k == 0)
            def _():
                acc_ref[...] = part

            @pl.when(k > 0)
            def _():
                acc_ref[...] += part

            @pl.when(k == nk - 1)
            def _():
                finish(acc_ref[...])

    scratch = [pltpu.VMEM(acc_shape, F32)] if nk > 1 else []
    out_shape = [jax.ShapeDtypeStruct(s, d) for s, d, _ in outs]
    in_specs = [a_spec, b_spec, *extra_specs] + [ANY_SPEC] * len(deps)
    out_specs = [sp for _, _, sp in outs]
    if not side:
        in_specs += ([b_spec] if b2 is not None else []) + ([ANY_SPEC] if into is not None else [])
        return list(pl.pallas_call(
            body, name=name, out_shape=out_shape, grid=grid, in_specs=in_specs, out_specs=out_specs,
            scratch_shapes=scratch, compiler_params=_params("parallel", "parallel", "arbitrary"),
            input_output_aliases={len(in_specs) - 1: 0} if into is not None else {},
        )(a, b, *extra, *deps, *tail))

    def with_scalars(spec):
        if spec.index_map is None:
            return spec
        return pl.BlockSpec(spec.block_shape, lambda i, j, k, s, f=spec.index_map: f(i, j, k))

    def by_step(block, index_map):
        return pl.BlockSpec(block, lambda i, j, k, s: index_map((i * grid[1] + j) * grid[2] + k, s))

    return list(pl.pallas_call(
        body, name=name,
        out_shape=out_shape + [jax.ShapeDtypeStruct(s, d) for s, d, _, _ in side_outs],
        grid_spec=pltpu.PrefetchScalarGridSpec(
            num_scalar_prefetch=1, grid=grid,
            in_specs=[with_scalars(sp) for sp in in_specs] + [by_step(blk, im) for _, blk, im in side_ins],
            out_specs=[with_scalars(sp) for sp in out_specs] + [by_step(blk, im) for _, _, blk, im in side_outs],
            scratch_shapes=scratch),
        compiler_params=_params("arbitrary", "arbitrary", "arbitrary"),
    )(side["scalars"], a, b, *extra, *deps, *[arr for arr, _, _ in side_ins]))


def _side_pair_sum(pieces, got, place):
    _, rows, cols = pieces.shape

    def build(steps):
        per_chip = steps // N_CHIP
        tr = rows // per_chip
        assert per_chip * N_CHIP == steps and tr * per_chip == rows and tr % 16 == 0
        blk = (None, tr, cols)
        here = lambda st, s: (st // per_chip, st % per_chip, 0)
        return dict(scalars=place,
                    ins=[(pieces, blk, lambda st, s: (2 * (st // per_chip) + s[0], st % per_chip, 0)), (got, blk, here)],
                    outs=[((N_CHIP, rows, cols), pieces.dtype, blk, here)],
                    fn=lambda mine, other: (mine.astype(F32) + other.astype(F32),))
    return build


def _side_adamw(w, m, v, sums, got, place):
    rows, cols = w.shape

    def build(steps):
        tr = rows // steps
        assert tr * steps == rows and tr % 16 == 0
        blk, here = (tr, cols), (lambda st, s: (st, 0))

        def fn(w, m, v, own, got):
            g = own.astype(F32)
            for k in range(3):
                g = g + got[k].astype(F32)
            delta, m_new, v_new = _adamw_math(w, g, m, v)
            return g, delta, m_new, v_new

        return dict(scalars=place,
                    ins=[(w, blk, here), (m, blk, here), (v, blk, here),
                         (sums, (None, tr, cols), lambda st, s: (s[1], st, 0)),
                         (got, (3, tr, cols), lambda st, s: (0, st, 0))],
                    outs=[((rows, cols), F32, blk, here)] * 4, fn=fn)
    return build


def _mm_nn(name, a, b, out_dtype=F32, tm=1024, tn=1024, tk=2048, epilogue=None, n_out=1, extra=(), deps=(), side=None,
           out_cols=None, out_at=0, into=None):
    m, kd = a.shape
    n = b.shape[1]
    tm, tn, tk = _tile(m, tm), _tile(n, tn), _tile(kd, tk)
    dts = out_dtype if isinstance(out_dtype, (list, tuple)) else [out_dtype] * n_out
    shift = out_at // tn
    o_spec = pl.BlockSpec((tm, tn), lambda i, j, k: (i, j + shift))
    grid = (m // tm, n // tn, kd // tk)
    return _mm(name, a, b, dims=NN, grid=grid,
               a_spec=pl.BlockSpec((tm, tk), lambda i, j, k: (i, k)),
               b_spec=pl.BlockSpec((tk, tn), lambda i, j, k: (k, j)),
               outs=[((m, out_cols or n), dt, o_spec) for dt in dts], extra=extra, extra_specs=[o_spec] * len(extra),
               epilogue=epilogue, acc_shape=(tm, tn), deps=deps, side=side and side(grid[0] * grid[1] * grid[2]),
               into=into)


def _mm_nn_pieces(name, a, b, out_dtype=F32, tm=1024, tk=2048, epilogue=None, n_out=1):
    m, kd = a.shape
    npc, _, ns = b.shape
    tm, tk = _tile(m, tm), _tile(kd, tk)
    tn = _tile(ns, 1024)
    r = ns // tn
    dts = out_dtype if isinstance(out_dtype, (list, tuple)) else [out_dtype] * n_out
    o_spec = pl.BlockSpec((tm, tn), lambda i, j, k: (i, j))
    return _mm(name, a, b, dims=NN, grid=(m // tm, npc * r, kd // tk),
               a_spec=pl.BlockSpec((tm, tk), lambda i, j, k: (i, k)),
               b_spec=pl.BlockSpec((None, tk, tn), lambda i, j, k: (j // r, k, j % r)),
               outs=[((m, npc * ns), dt, o_spec) for dt in dts], epilogue=epilogue, acc_shape=(tm, tn))


def _mm_nt(name, a, b, out_dtype=F32, tm=1024, tn=1024, tk=2048, epilogue=None, extra=(), n_rows=None, deps=(),
           side=None, b2=None):
    m, kd = a.shape
    n = n_rows or b.shape[0]
    tm, tn, tk = _tile(m, tm), _tile(n, tn), _tile(kd, tk)
    o_spec = pl.BlockSpec((tm, tn), lambda i, j, k: (i, j))
    if b2 is not None:
        tk = kd
        b_spec = pl.BlockSpec((tn, kd // 2), lambda i, j, k: (j, 0))
    else:
        b_spec = pl.BlockSpec((tn, tk), lambda i, j, k: (j, k))
    grid = (m // tm, n // tn, kd // tk)
    res = _mm(name, a, b, dims=NT, grid=grid,
              a_spec=pl.BlockSpec((tm, tk), lambda i, j, k: (i, k)), b_spec=b_spec,
              outs=[((m, n), out_dtype, o_spec)], extra=extra, extra_specs=[o_spec] * len(extra),
              epilogue=epilogue, acc_shape=(tm, tn), deps=deps, side=side and side(grid[0] * grid[1] * grid[2]), b2=b2)
    return res if side else res[0]


def _mm_nt_pieces(name, a, b, out_dtype=F32, tm=1024, tn=256, deps=()):
    m = a.shape[0]
    npc, n, ks = b.shape
    tm, tn = _tile(m, tm), _tile(n, tn)
    return _mm(name, a, b, dims=NT, grid=(m // tm, n // tn, 1),
               a_spec=pl.BlockSpec((tm, npc * ks), lambda i, j, k: (i, 0)),
               b_spec=pl.BlockSpec((npc, tn, ks), lambda i, j, k: (0, j, 0)),
               outs=[((m, n), out_dtype, pl.BlockSpec((tm, tn), lambda i, j, k: (i, j)))],
               deps=deps, b_pieces=npc)[0]


def _mm_tn(name, a, b, out_dtype=BF16, tm=1024, tn=1024, pieces=0, deps=(), side=None):
    t, m = a.shape
    n = b.shape[1]
    tm = _tile(m, tm)
    if pieces:
        ns = n // pieces
        tn = _tile(ns, tn)
        r = ns // tn
        out = ((pieces, m, ns), out_dtype, pl.BlockSpec((None, tm, tn), lambda i, j, k: (j // r, i, j % r)))
    else:
        tn = _tile(n, tn)
        out = ((m, n), out_dtype, pl.BlockSpec((tm, tn), lambda i, j, k: (i, j)))
    grid = (m // tm, n // tn, 1)
    res = _mm(name, a, b, dims=TN, grid=grid,
              a_spec=pl.BlockSpec((t, tm), lambda i, j, k: (0, i)),
              b_spec=pl.BlockSpec((t, tn), lambda i, j, k: (0, j)),
              outs=[out], deps=deps, side=side and side(grid[0] * grid[1]))
    return res if side else res[0]


def _rms_f(x, g):
    r = lax.rsqrt(jnp.mean(x * x, axis=-1, keepdims=True) + EPS)
    xh = x * r
    return xh * g, xh, r


def _rms_b(dy, g, xh, r):
    dxh = dy * g
    dx = r * (dxh - xh * jnp.mean(dxh * xh, axis=-1, keepdims=True))
    dg = jnp.sum(dy * xh, axis=0, keepdims=True)
    return dx, dg


def _rowwise(name, fn, row_ins, vec_ins, row_outs, vec_outs, deps=()):
    t = row_ins[0].shape[0]
    tr = min(ROW_TILE, t)
    nr, nv, no = len(row_ins), len(vec_ins), len(row_outs)
    first_out = nr + nv + len(deps)

    def body(*refs):
        ri, vi = refs[:nr], refs[nr:nr + nv]
        ro, vo = refs[first_out:first_out + no], refs[first_out + no:]
        routs, vouts = fn(*[r[...] for r in ri], *[v[...] for v in vi])
        for o, val in zip(ro, routs):
            o[...] = val.astype(o.dtype)
        i = pl.program_id(0)
        for o, val in zip(vo, vouts):
            @pl.when(i == 0)
            def _():
                o[...] = val

            @pl.when(i > 0)
            def _():
                o[...] += val

    res = pl.pallas_call(
        body, name=name,
        out_shape=[jax.ShapeDtypeStruct((t, w), d) for w, d in row_outs]
        + [jax.ShapeDtypeStruct((1, w), F32) for w in vec_outs],
        grid=(t // tr,),
        in_specs=[pl.BlockSpec((tr, a.shape[1]), lambda i: (i, 0)) for a in row_ins]
        + [pl.BlockSpec((1, a.shape[1]), lambda i: (0, 0)) for a in vec_ins] + [ANY_SPEC] * len(deps),
        out_specs=[pl.BlockSpec((tr, w), lambda i: (i, 0)) for w, _ in row_outs]
        + [pl.BlockSpec((1, w), lambda i: (0, 0)) for w in vec_outs],
        compiler_params=_params("arbitrary"),
    )(*row_ins, *vec_ins, *deps)
    return list(res)


def _norm_fwd(x, g, name, deps=()):
    d = x.shape[1]
    return _rowwise(name, lambda x, g: ((_rms_f(x, g)[0],), ()), [x], [g], [(d, BF16)], [], deps=deps)[0]


def _norm_bwd(x, dxn, dres, g, name):
    d = x.shape[1]

    def fn(x, dxn, dres, g):
        _, xh, r = _rms_f(x, g)
        dx, dg = _rms_b(dxn, g, xh, r)
        return (dres + dx,), (dg,)

    return _rowwise(name, fn, [x, dxn, dres], [g], [(d, F32)], [d])


def _block_fwd(h_prev, y, g_y, g_n, name, deps=()):
    d = h_prev.shape[1]

    def fn(h_prev, y, g_y, g_n):
        h = h_prev + _rms_f(y, g_y)[0]
        return (h, _rms_f(h, g_n)[0]), ()

    return _rowwise(name, fn, [h_prev, y], [g_y, g_n], [(d, F32), (d, BF16)], [], deps=deps)


def _block_bwd(h, y, d_up, d_n, g_y, g_n, name, deps=()):
    d = h.shape[1]

    def fn(h, y, d_up, d_n, g_y, g_n):
        _, hh, hr = _rms_f(h, g_n)
        dh_n, dg_n = _rms_b(d_n, g_n, hh, hr)
        dh = d_up + dh_n
        _, yh, yr = _rms_f(y, g_y)
        dy, dg_y = _rms_b(dh, g_y, yh, yr)
        return (dh, dy), (dg_y, dg_n)

    return _rowwise(name, fn, [h, y, d_up, d_n], [g_y, g_n], [(d, F32), (d, BF16)], [d, d], deps=deps)


def _loss_head(h2, z, e, target, g, name):
    d = h2.shape[1]

    def fn(h2, z, e, target, g):
        gate = jax.nn.sigmoid(z)
        y, xh, r = _rms_f(e * gate, g)
        diff = h2 + y - target
        loss = 0.5 * jnp.sum(jnp.mean(diff * diff, axis=-1, keepdims=True), axis=0, keepdims=True)
        dh3 = diff / d
        deg, dg = _rms_b(dh3, g, xh, r)
        dz = deg * e * gate * (1.0 - gate)
        de = deg * gate
        return (dh3, dz, de), (dg, jnp.broadcast_to(loss, (1, LANES)))

    return _rowwise(name, fn, [h2, z, e, target], [g], [(d, F32), (d, BF16), (d, BF16)], [d, LANES])


def _shift_down(x, s, rows):
    return jnp.where(rows >= s, pltpu.roll(x, s, axis=0), 0.0)


def _shift_up(x, s, rows):
    t = x.shape[0]
    return jnp.where(rows < t - s, pltpu.roll(x, t - s, axis=0), 0.0)


def _conv_pre(x, w, rows):
    pre = x * w[QK_CONV - 1:QK_CONV, :]
    for s in range(1, QK_CONV):
        pre = pre + _shift_down(x, s, rows) * w[QK_CONV - 1 - s:QK_CONV - s, :]
    return pre


def _conv_fwd(proj_a, w, start, name, scale_t=1.0):
    t = proj_a.shape[0]
    width = w.shape[1]
    tc = _tile(width, 256)
    off = start // tc

    def body(x_ref, w_ref, y_ref, yt_ref):
        x = x_ref[...]
        rows = lax.broadcasted_iota(jnp.int32, x.shape, 0)
        pre = _conv_pre(x, w_ref[...], rows)
        y = pre * jax.nn.sigmoid(pre)
        y_ref[...] = y
        yt_ref[...] = (y * scale_t).T.astype(yt_ref.dtype)

    return pl.pallas_call(
        body, name=name,
        out_shape=[jax.ShapeDtypeStruct((t, width), F32), jax.ShapeDtypeStruct((width, t), BF16)],
        grid=(width // tc,),
        in_specs=[pl.BlockSpec((t, tc), lambda j: (0, off + j)), pl.BlockSpec((QK_CONV, tc), lambda j: (0, j))],
        out_specs=[pl.BlockSpec((t, tc), lambda j: (0, j)), pl.BlockSpec((tc, t), lambda j: (j, 0))],
        compiler_params=_params("parallel"),
    )(proj_a, w)


def _conv_bwd(proj_a, w, dy, start, name):
    t = proj_a.shape[0]
    width = w.shape[1]
    tc = _tile(width, 256)
    off = start // tc

    def body(x_ref, w_ref, dy_ref, dx_ref, dw_ref):
        x, w = x_ref[...], w_ref[...]
        rows = lax.broadcasted_iota(jnp.int32, x.shape, 0)
        pre = _conv_pre(x, w, rows)
        sig = jax.nn.sigmoid(pre)
        dpre = dy_ref[...] * (sig * (1.0 + pre * (1.0 - sig)))
        dx = dpre * w[QK_CONV - 1:QK_CONV, :]
        dws = [jnp.sum(dpre * x, axis=0, keepdims=True)]
        for s in range(1, QK_CONV):
            dx = dx + _shift_up(dpre, s, rows) * w[QK_CONV - 1 - s:QK_CONV - s, :]
            dws.append(jnp.sum(dpre * _shift_down(x, s, rows), axis=0, keepdims=True))
        dx_ref[...] = dx.astype(dx_ref.dtype)
        for s in range(QK_CONV):
            dw_ref[QK_CONV - 1 - s:QK_CONV - s, :] = dws[s]

    return pl.pallas_call(
        body, name=name,
        out_shape=[jax.ShapeDtypeStruct((t, width), BF16), jax.ShapeDtypeStruct((QK_CONV, width), F32)],
        grid=(width // tc,),
        in_specs=[pl.BlockSpec((t, tc), lambda j: (0, off + j)), pl.BlockSpec((QK_CONV, tc), lambda j: (0, j)),
                  pl.BlockSpec((t, tc), lambda j: (0, j))],
        out_specs=[pl.BlockSpec((t, tc), lambda j: (0, j)), pl.BlockSpec((QK_CONV, tc), lambda j: (0, j))],
        compiler_params=_params("parallel"),
    )(proj_a, w, dy)


def _dot(a, b, dims):
    return lax.dot_general(a.astype(BF16), b.astype(BF16), (dims, ((), ())), preferred_element_type=F32)


def _dot_split(ones, x):
    hi = x.astype(BF16)
    rest = x - hi.astype(F32)
    mid = rest.astype(BF16)
    lo = (rest - mid.astype(F32)).astype(BF16)
    dot = lambda part: lax.dot_general(ones, part, (NN, ((), ())), preferred_element_type=F32)
    return dot(hi) + dot(mid) + dot(lo)


def _chunk_iotas():
    rows = lax.broadcasted_iota(jnp.int32, (CHUNK, CHUNK), 0)
    cols = lax.broadcasted_iota(jnp.int32, (CHUNK, CHUNK), 1)
    return rows, cols


def _gates_fwd(proj_b, bias, col_block, heads, name):
    t = proj_b.shape[0]

    def body(x_ref, b_ref, gb_ref, gbt_ref, cols_ref):
        rows, cols = _chunk_iotas()
        tri = (cols <= rows).astype(BF16)
        for ci in range(t // CHUNK):
            sl = slice(ci * CHUNK, (ci + 1) * CHUNK)
            pre = x_ref[sl, :] + b_ref[...]
            log_f = jnp.minimum(pre, 0.0) - jnp.log(1.0 + jnp.exp(-jnp.abs(pre)))
            log_f = jnp.where((cols >= heads) & (cols < 2 * heads), log_f, 0.0)
            tile = jnp.where(cols < heads, pre, _dot_split(tri, log_f))
            gb_ref[sl, :] = tile
            gbt_ref[:, sl] = tile.T
            for h in range(2 * heads):
                cols_ref[h, sl, :] = jnp.broadcast_to(tile[:, h:h + 1], (CHUNK, LANES))

    return pl.pallas_call(
        body, name=name,
        out_shape=[jax.ShapeDtypeStruct((t, LANES), F32), jax.ShapeDtypeStruct((LANES, t), F32),
                   jax.ShapeDtypeStruct((2 * heads, t, LANES), F32)],
        grid=(1,),
        in_specs=[pl.BlockSpec((t, LANES), lambda i: (0, col_block)), pl.BlockSpec((1, LANES), lambda i: (0, 0))],
        out_specs=[pl.BlockSpec((t, LANES), lambda i: (0, 0)), pl.BlockSpec((LANES, t), lambda i: (0, 0)),
                   pl.BlockSpec((2 * heads, t, LANES), lambda i: (0, 0, 0))],
        compiler_params=_params("arbitrary"),
    )(proj_b, bias)


def _gates_bwd(proj_b, bias, dcol, drow, col_block, heads, name):
    t = proj_b.shape[0]

    def body(x_ref, b_ref, dc_ref, dr_ref, dx_ref, db_ref):
        rows, cols = _chunk_iotas()
        later = (cols >= rows).astype(BF16)
        total = jnp.zeros((1, LANES), F32)
        for ci in range(t // CHUNK):
            sl = slice(ci * CHUNK, (ci + 1) * CHUNK)
            by_row = dr_ref[:, sl].T
            d = dc_ref[sl, :] + jnp.where(cols < heads, by_row,
                                          jnp.where(cols < 2 * heads, -pltpu.roll(by_row, heads, axis=1), 0.0))
            d_log_f = _dot_split(later, jnp.where(cols >= heads, d, 0.0))
            pre = x_ref[sl, :] + b_ref[...]
            dx = jnp.where(cols < heads, d, jnp.where(cols < 2 * heads, d_log_f * jax.nn.sigmoid(-pre), 0.0))
            dx_ref[sl, :] = dx.astype(dx_ref.dtype)
            total = total + jnp.sum(dx, axis=0, keepdims=True)
        db_ref[...] = total

    return pl.pallas_call(
        body, name=name,
        out_shape=[jax.ShapeDtypeStruct((t, LANES), BF16), jax.ShapeDtypeStruct((1, LANES), F32)],
        grid=(1,),
        in_specs=[pl.BlockSpec((t, LANES), lambda i: (0, col_block)), pl.BlockSpec((1, LANES), lambda i: (0, 0)),
                  pl.BlockSpec((t, LANES), lambda i: (0, 0)), pl.BlockSpec((LANES, t), lambda i: (0, 0))],
        out_specs=[pl.BlockSpec((t, LANES), lambda i: (0, 0)), pl.BlockSpec((1, LANES), lambda i: (0, 0))],
        compiler_params=_params("arbitrary"),
    )(proj_b, bias, dcol, drow)


def _gate_views(li_ref, b_ref, gbt_ref, s, head, heads, sl):
    li_c, b_c = li_ref[s, sl, :], b_ref[s, sl, :]
    li_r = jnp.broadcast_to(gbt_ref[head, :, sl], (CHUNK, CHUNK))
    b_r = jnp.broadcast_to(gbt_ref[heads + head, :, sl], (CHUNK, CHUNK))
    return li_c, b_c, li_r, b_r


def _chunk_state(kt, k, v, li_c, b_c, ct, nrow, m_prev):
    b_tot = b_c[CHUNK - 1:CHUNK, 0:1]
    a = b_tot - b_c[:, 0:1] + li_c[:, 0:1]
    m_new = jnp.maximum(b_tot + m_prev, jnp.max(a, axis=0, keepdims=True))
    dec = jnp.exp(b_tot + m_prev - m_new)
    w = jnp.exp(a - m_new)
    wv = w * v
    ct_new = dec * ct + _dot(kt, wv, NN)
    n_new = dec * nrow + jnp.sum(w * k, axis=0, keepdims=True)
    return ct_new, n_new, m_new, dec, w, wv


def _chunk_scores(q, k, b_c, li_r, b_r, m_prev):
    rows, cols = _chunk_iotas()
    dlog = jnp.where(cols <= rows, b_c - b_r + li_r, -jnp.inf)
    inter = b_c[:, 0:1] + m_prev
    mt = jnp.maximum(inter, jnp.max(dlog, axis=-1, keepdims=True))
    dw = jnp.exp(dlog - mt)
    iw = jnp.exp(inter - mt)
    a = _dot(q, k, NT)
    return dw, iw, mt, a, a * dw


def _to_row(column):
    rows, cols = _chunk_iotas()
    return jnp.sum(jnp.where(rows == cols, jnp.broadcast_to(column, (CHUNK, CHUNK)), 0.0), axis=0, keepdims=True)


def _head_out(h, o, g):
    cell, hh, r = _rms_f(h, g)
    sig = jax.nn.sigmoid(o)
    return sig * cell, (cell, hh, r, sig)


HEADS_PER_STEP = 2


def _mlstm_fwd(qc, kc, kct, proj_a, gcols, gbt, g_mlstm, heads, d_model, name, dep):
    t = qc.shape[0]
    dh = g_mlstm.shape[1] // heads
    nc = t // CHUNK
    scale = dh ** -0.5
    hp = HEADS_PER_STEP

    def body(q_ref, k_ref, kt_ref, v_ref, o_ref, li_ref, b_ref, gbt_ref, g_ref, dep_ref, out_ref, ct_out, nm_out,
             ct_ref, n_ref, m_ref):
        del dep_ref
        step = pl.program_id(0)
        ct_ref[...] = jnp.zeros_like(ct_ref)
        n_ref[...] = jnp.zeros_like(n_ref)
        m_ref[...] = jnp.zeros_like(m_ref)
        tile_row = lax.broadcasted_iota(jnp.int32, (8, LANES), 0)

        def chunk(ci, carry):
            sl = pl.ds(pl.multiple_of(ci * CHUNK, CHUNK), CHUNK)
            for s in range(hp):
                cs = slice(s * dh, (s + 1) * dh)
                q, k, v, kt = q_ref[sl, cs], k_ref[sl, cs] * scale, v_ref[sl, cs], kt_ref[cs, sl]
                li_c, b_c, li_r, b_r = _gate_views(li_ref, b_ref, gbt_ref, s, step * hp + s, heads, sl)
                ct, nrow, m_prev = ct_ref[s], n_ref[s, 0:1, :], m_ref[s, 0:1, 0:1]
                ct_out[s, ci] = ct
                nm_out[s, ci] = jnp.where(tile_row == 0, nrow, jnp.where(tile_row == 1, m_prev, 0.0))
                dw, iw, mt, _, sm = _chunk_scores(q, k, b_c, li_r, b_r, m_prev)
                iq = iw * q
                num = _dot(jnp.concatenate([iq, sm], axis=1), jnp.concatenate([ct, v], axis=0), NN)
                den = jnp.sum(iq * nrow + sm, axis=-1, keepdims=True)
                h = num / jnp.maximum(jnp.abs(den), jnp.exp(-mt))
                y, _ = _head_out(h, o_ref[sl, cs], g_ref[:, cs])
                out_ref[sl, cs] = y.astype(out_ref.dtype)
                ct_new, n_new, m_new, _, _, _ = _chunk_state(kt, k, v, li_c, b_c, ct, nrow, m_prev)
                ct_ref[s] = ct_new
                n_ref[s, 0:1, :] = n_new
                m_ref[s, 0:1, 0:1] = m_new
            return carry

        lax.fori_loop(0, nc, chunk, 0, unroll=2)

    col = lambda off: pl.BlockSpec((t, hp * dh), lambda h: (0, off // hp + h))
    slab = lambda off: pl.BlockSpec((hp, t, LANES), lambda h: (off // hp + h, 0, 0))
    return pl.pallas_call(
        body, name=name,
        out_shape=[jax.ShapeDtypeStruct((t, d_model), BF16), jax.ShapeDtypeStruct((heads, nc, dh, dh), F32),
                   jax.ShapeDtypeStruct((heads, nc, 8, LANES), F32)],
        grid=(heads // hp,),
        in_specs=[col(0), col(0), pl.BlockSpec((hp * dh, t), lambda h: (h, 0)), col(2 * heads), col(3 * heads),
                  slab(0), slab(heads), pl.BlockSpec((2 * heads, 1, t), lambda h: (0, 0, 0)),
                  pl.BlockSpec((1, hp * dh), lambda h: (0, h)), ANY_SPEC],
        out_specs=[pl.BlockSpec((t, hp * dh), lambda h: (0, h)),
                   pl.BlockSpec((hp, nc, dh, dh), lambda h: (h, 0, 0, 0)),
                   pl.BlockSpec((hp, nc, 8, LANES), lambda h: (h, 0, 0, 0))],
        scratch_shapes=[pltpu.VMEM((hp, dh, dh), F32), pltpu.VMEM((hp, 8, dh), F32), pltpu.VMEM((hp, 8, LANES), F32)],
        compiler_params=_params("arbitrary"),
    )(qc, kc, kct, proj_a, proj_a, gcols, gcols, gbt, g_mlstm, dep)


def _mlstm_bwd(qc, kc, qct, kct, proj_a, gcols, gbt, g_mlstm, states, dhcat, heads, name, dep):
    t = qc.shape[0]
    wm = g_mlstm.shape[1]
    dh = wm // heads
    nc = t // CHUNK
    scale = dh ** -0.5

    def body(q_ref, k_ref, qt_ref, kt_ref, v_ref, o_ref, li_ref, b_ref, gbt_ref, g_ref, ct_all, nm_all, dy_ref, dep_ref,
             dq_ref, dk_ref, dv_ref, do_ref, dcol_ref, drow_ref, dg_ref, dct_ref, dn_ref):
        del dep_ref
        head = pl.program_id(0)
        g = g_ref[...]

        @pl.when(head == 0)
        def _():
            dcol_ref[...] = jnp.zeros_like(dcol_ref)
            drow_ref[...] = jnp.zeros_like(drow_ref)

        def load(ci):
            sl = pl.ds(pl.multiple_of(ci * CHUNK, CHUNK), CHUNK)
            return (sl, q_ref[sl, :], k_ref[sl, :] * scale, v_ref[sl, :], kt_ref[:, sl]) \
                + _gate_views(li_ref, b_ref, gbt_ref, 0, head, heads, sl)

        dct_ref[...] = jnp.zeros_like(dct_ref)
        dn_ref[...] = jnp.zeros_like(dn_ref)
        dg_ref[...] = jnp.zeros_like(dg_ref)
        rows, cols = _chunk_iotas()
        last = lax.broadcasted_iota(jnp.int32, (CHUNK, 1), 0) == CHUNK - 1
        lane = lax.broadcasted_iota(jnp.int32, (CHUNK, LANES), 1)

        def bwd(step, carry):
            ci = nc - 1 - step
            sl, q, k, v, kt, li_c, b_c, li_r, b_r = load(ci)
            qt = qt_ref[:, sl]
            ct, nrow, m_prev = ct_all[ci], nm_all[ci, 0:1, :], nm_all[ci, 1:2, 0:1]
            dw, iw, mt, a, sm = _chunk_scores(q, k, b_c, li_r, b_r, m_prev)
            qc_ = _dot(q, ct, NN)
            qn = jnp.sum(q * nrow, axis=-1, keepdims=True)
            num = iw * qc_ + _dot(sm, v, NN)
            den = iw * qn + jnp.sum(sm, axis=-1, keepdims=True)
            floor = jnp.exp(-mt)
            dn = jnp.maximum(jnp.abs(den), floor)
            h = num / dn
            _, (cell, hh, r, sig) = _head_out(h, o_ref[sl, :], g)
            dy = dy_ref[sl, :]
            dcell = dy * sig
            do_ref[sl, :] = (dy * cell * sig * (1.0 - sig)).astype(do_ref.dtype)
            dh_, dg = _rms_b(dcell, g, hh, r)
            dg_ref[...] += dg
            dnum = dh_ / dn
            ddn = -jnp.sum(dh_ * h, axis=-1, keepdims=True) / dn
            dden = jnp.where(jnp.abs(den) >= floor, ddn * jnp.sign(den), 0.0)
            idn = iw * dnum
            idd = iw * dden
            dw_t = jnp.exp(jnp.where(cols >= rows, b_r - b_c + li_c, -jnp.inf) - _to_row(mt))
            sm_t = _dot(k, q, NT) * dw_t
            dsm = _dot(dnum, v, NT) + dden
            da_ = dsm * dw
            da_t = (_dot(v, dnum, NT) + _to_row(dden)) * dw_t
            e = da_ * a
            dq = _dot(idn, ct, NT) + _dot(da_, k, NN) + idd * nrow
            dk = _dot(da_t, q, NN)
            dv = _dot(sm_t, dnum, NN)
            dct = _dot(qt, idn, NN)
            dnr = jnp.sum(q * idd, axis=0, keepdims=True)
            db = jnp.sum(e + idn * qc_, axis=-1, keepdims=True) + idd * qn
            _, _, _, dec, w, wv = _chunk_state(kt, k, v, li_c, b_c, ct, nrow, m_prev)
            dct_new, dn_new = dct_ref[...], dn_ref[0:1, :]
            ddec = jnp.sum(jnp.sum(dct_new * ct, axis=-1, keepdims=True), axis=0, keepdims=True) \
                + jnp.sum(dn_new * nrow, axis=-1, keepdims=True)
            dk = dk + _dot(wv, dct_new, NT) + w * dn_new
            dwv = _dot(k, dct_new, NN)
            dv = dv + w * dwv
            dwt = jnp.sum(dwv * v + k * dn_new, axis=-1, keepdims=True)
            da = dwt * w
            dbtot = jnp.sum(da, axis=0, keepdims=True) + ddec * dec
            db = db - da + jnp.where(last, dbtot, 0.0)
            dct_ref[...] = dec * dct_new + dct
            dn_ref[0:1, :] = dec * dn_new + dnr
            dq_ref[sl, :] = dq
            dk_ref[sl, :] = dk * scale
            dv_ref[sl, :] = dv.astype(dv_ref.dtype)
            dcol_ref[sl, :] += jnp.where(lane == head, da, 0.0) + jnp.where(lane == heads + head, db, 0.0)
            drow_ref[head, :, sl] = jnp.sum(e, axis=0, keepdims=True)
            return carry

        lax.fori_loop(0, nc, bwd, 0, unroll=8)

    col = lambda off: pl.BlockSpec((t, dh), lambda h: (0, off + h))
    rowb = pl.BlockSpec((dh, t), lambda h: (h, 0))
    slab = lambda off: pl.BlockSpec((1, t, LANES), lambda h: (off + h, 0, 0))
    vec = pl.BlockSpec((1, dh), lambda h: (0, h))
    return pl.pallas_call(
        body, name=name,
        out_shape=[jax.ShapeDtypeStruct((t, wm), F32)] * 2 + [jax.ShapeDtypeStruct((t, wm), BF16)] * 2
        + [jax.ShapeDtypeStruct((t, LANES), F32), jax.ShapeDtypeStruct((heads, 1, t), F32),
           jax.ShapeDtypeStruct((1, wm), F32)],
        grid=(heads,),
        in_specs=[col(0), col(0), rowb, rowb, col(2 * heads), col(3 * heads), slab(0), slab(heads),
                  pl.BlockSpec((2 * heads, 1, t), lambda h: (0, 0, 0)), vec,
                  pl.BlockSpec((None, nc, dh, dh), lambda h: (h, 0, 0, 0)),
                  pl.BlockSpec((None, nc, 8, LANES), lambda h: (h, 0, 0, 0)), col(0), ANY_SPEC],
        out_specs=[col(0), col(0), col(0), col(0), pl.BlockSpec((t, LANES), lambda h: (0, 0)),
                   pl.BlockSpec((heads, 1, t), lambda h: (0, 0, 0)), vec],
        scratch_shapes=[pltpu.VMEM((dh, dh), F32), pltpu.VMEM((8, dh), F32)],
        compiler_params=_params("arbitrary"),
    )(qc, kc, qct, kct, proj_a, proj_a, gcols, gcols, gbt, g_mlstm, *states, dhcat, dep)


def _pool_window(gi):
    return jnp.where(gi == 0, 2.0, jnp.where(gi == 1, 4.0, jnp.where(gi == 2, 8.0, 16.0))).astype(F32)


def _pool_select(gi, levels):
    return jnp.where(gi == 0, levels[0], jnp.where(gi == 1, levels[1], jnp.where(gi == 2, levels[2], levels[3])))


def _pooled(u, gi, rows):
    s, levels = u, []
    for lvl in range(POOL_GROUPS):
        s = s + _shift_down(s, 1 << lvl, rows)
        levels.append(s)
    cnt = jnp.minimum((rows + 1).astype(F32), _pool_window(gi))
    return _pool_select(gi, levels) / cnt - u, cnt


def _pool_fwd(proj_b, w_pool, scale, hcat, name):
    t, d_model = hcat.shape
    g, cg, _ = w_pool.shape
    off = (d_model - g * cg) // cg

    def body(u_ref, w_ref, s_ref, hcat_ref, out_ref):
        del hcat_ref
        gi = pl.program_id(0)
        u = u_ref[...]
        rows = lax.broadcasted_iota(jnp.int32, u.shape, 0)
        pooled, _ = _pooled(u, gi, rows)
        out_ref[...] = (_dot(pooled, w_ref[...], NN) * s_ref[...]).astype(out_ref.dtype)

    return pl.pallas_call(
        body, name=name,
        out_shape=jax.ShapeDtypeStruct((t, d_model), BF16),
        grid=(g,),
        in_specs=[pl.BlockSpec((t, cg), lambda gi: (0, gi)), pl.BlockSpec((None, cg, cg), lambda gi: (gi, 0, 0)),
                  pl.BlockSpec((1, cg), lambda gi: (0, gi)), HBM_SPEC],
        out_specs=pl.BlockSpec((t, cg), lambda gi: (0, off + gi)),
        input_output_aliases={3: 0},
        compiler_params=_params("arbitrary"),
    )(proj_b, w_pool, scale, hcat)


def _pool_bwd(proj_b, w_pool, scale, dhcat, name):
    t, d_model = dhcat.shape
    g, cg, _ = w_pool.shape
    off = (d_model - g * cg) // cg

    def body(u_ref, w_ref, s_ref, dy_ref, du_ref, dw_ref, ds_ref):
        gi = pl.program_id(0)
        u, w, dy = u_ref[...], w_ref[...], dy_ref[...]
        rows = lax.broadcasted_iota(jnp.int32, u.shape, 0)
        pooled, cnt = _pooled(u, gi, rows)
        mixed = _dot(pooled, w, NN)
        ds_ref[...] = jnp.sum(dy * mixed, axis=0, keepdims=True)
        dmixed = dy * s_ref[...]
        dw_ref[...] = _dot(pooled, dmixed, TN).astype(dw_ref.dtype)
        dpooled = _dot(dmixed, w, NT)
        s, levels = dpooled / cnt, []
        for lvl in range(POOL_GROUPS):
            s = s + _shift_up(s, 1 << lvl, rows)
            levels.append(s)
        du_ref[...] = (_pool_select(gi, levels) - dpooled).astype(du_ref.dtype)

    return pl.pallas_call(
        body, name=name,
        out_shape=[jax.ShapeDtypeStruct((t, g * cg), BF16), jax.ShapeDtypeStruct((g, cg, cg), BF16),
                   jax.ShapeDtypeStruct((1, g * cg), F32)],
        grid=(g,),
        in_specs=[pl.BlockSpec((t, cg), lambda gi: (0, gi)), pl.BlockSpec((None, cg, cg), lambda gi: (gi, 0, 0)),
                  pl.BlockSpec((1, cg), lambda gi: (0, gi)), pl.BlockSpec((t, cg), lambda gi: (0, off + gi))],
        out_specs=[pl.BlockSpec((t, cg), lambda gi: (0, gi)), pl.BlockSpec((None, cg, cg), lambda gi: (gi, 0, 0)),
                   pl.BlockSpec((1, cg), lambda gi: (0, gi))],
        compiler_params=_params("arbitrary"),
    )(proj_b, w_pool, scale, dhcat)


def _pad_cols(a, width):
    return jnp.pad(a, ((0, 0), (0, width - a.shape[1])))


W_IN_COL_TILE = 512


def _w_in_rows(pieces, main, gates, name):
    n, shard, d = pieces.shape
    pool = n * shard - main - gates
    tc = _tile(d, W_IN_COL_TILE)

    def body(x_ref, a_ref, b_ref, flat):
        for j in range(n):
            flat[j * shard:(j + 1) * shard, :] = x_ref[j]
        a_ref[...] = flat[:main, :]
        b_ref[:pool, :] = flat[main + gates:, :]
        b_ref[pool:pool + gates, :] = flat[main:main + gates, :]
        b_ref[pool + gates:, :] = jnp.zeros((LANES - gates, tc), b_ref.dtype)

    return pl.pallas_call(
        body, name=name,
        out_shape=[jax.ShapeDtypeStruct((main, d), pieces.dtype), jax.ShapeDtypeStruct((pool + LANES, d), pieces.dtype)],
        grid=(d // tc,),
        in_specs=[pl.BlockSpec((n, shard, tc), lambda j: (0, 0, j))],
        out_specs=[pl.BlockSpec((main, tc), lambda j: (0, j)), pl.BlockSpec((pool + LANES, tc), lambda j: (0, j))],
        scratch_shapes=[pltpu.VMEM((n * shard, tc), pieces.dtype)],
        compiler_params=_params("parallel"),
    )(pieces)


def _w_in_pieces(rows_main, rows_rest, shard, gates, name):
    main, d = rows_main.shape
    pool = rows_rest.shape[0] - LANES
    n = (main + gates + pool) // shard
    tc = _tile(d, W_IN_COL_TILE)

    def body(a_ref, b_ref, o_ref, flat):
        flat[:main, :] = a_ref[...]
        flat[main:main + gates, :] = b_ref[pool:pool + gates, :]
        flat[main + gates:, :] = b_ref[:pool, :]
        for j in range(n):
            o_ref[j] = flat[j * shard:(j + 1) * shard, :]

    return pl.pallas_call(
        body, name=name,
        out_shape=jax.ShapeDtypeStruct((n, shard, d), rows_main.dtype),
        grid=(d // tc,),
        in_specs=[pl.BlockSpec((main, tc), lambda j: (0, j)), pl.BlockSpec((pool + LANES, tc), lambda j: (0, j))],
        out_specs=pl.BlockSpec((n, shard, tc), lambda j: (0, 0, j)),
        scratch_shapes=[pltpu.VMEM((n * shard, tc), rows_main.dtype)],
        compiler_params=_params("parallel"),
    )(rows_main, rows_rest)


def kernel(x, p, w_in, b_gates, w_qk_conv, g_mlstm, w_pool, pool_scale, w_out, g_mix_pre, g_mix_post, w_ff1, w_ff2, g_ff_pre, g_ff_post, w_ple_proj, w_ple_gate, g_ple_gate, g_ple_post, loss_target, m_w_in, m_b_gates, m_w_qk_conv, m_g_mlstm, m_w_pool, m_pool_scale, m_w_out, m_g_mix_pre, m_g_mix_post, m_w_ff1, m_w_ff2, m_g_ff_pre, m_g_ff_post, m_w_ple_proj, m_w_ple_gate, m_g_ple_gate, m_g_ple_post, v_w_in, v_b_gates, v_w_qk_conv, v_g_mlstm, v_w_pool, v_pool_scale, v_w_out, v_g_mix_pre, v_g_mix_post, v_w_ff1, v_w_ff2, v_g_ff_pre, v_g_ff_post, v_w_ple_proj, v_w_ple_gate, v_g_ple_gate, v_g_ple_post):
    weights = dict(w_in=w_in, b_gates=b_gates, w_qk_conv=w_qk_conv, g_mlstm=g_mlstm, w_pool=w_pool,
                   pool_scale=pool_scale, w_out=w_out, g_mix_pre=g_mix_pre, g_mix_post=g_mix_post, w_ff1=w_ff1,
                   w_ff2=w_ff2, g_ff_pre=g_ff_pre, g_ff_post=g_ff_post, w_ple_proj=w_ple_proj,
                   w_ple_gate=w_ple_gate, g_ple_gate=g_ple_gate, g_ple_post=g_ple_post)
    mom1 = dict(w_in=m_w_in, b_gates=m_b_gates, w_qk_conv=m_w_qk_conv, g_mlstm=m_g_mlstm, w_pool=m_w_pool,
                pool_scale=m_pool_scale, w_out=m_w_out, g_mix_pre=m_g_mix_pre, g_mix_post=m_g_mix_post,
                w_ff1=m_w_ff1, w_ff2=m_w_ff2, g_ff_pre=m_g_ff_pre, g_ff_post=m_g_ff_post,
                w_ple_proj=m_w_ple_proj, w_ple_gate=m_w_ple_gate, g_ple_gate=m_g_ple_gate, g_ple_post=m_g_ple_post)
    mom2 = dict(w_in=v_w_in, b_gates=v_b_gates, w_qk_conv=v_w_qk_conv, g_mlstm=v_g_mlstm, w_pool=v_w_pool,
                pool_scale=v_pool_scale, w_out=v_w_out, g_mix_pre=v_g_mix_pre, g_mix_post=v_g_mix_post,
                w_ff1=v_w_ff1, w_ff2=v_w_ff2, g_ff_pre=v_g_ff_pre, g_ff_post=v_g_ff_post,
                w_ple_proj=v_w_ple_proj, w_ple_gate=v_w_ple_gate, g_ple_gate=v_g_ple_gate, g_ple_post=v_g_ple_post)
    order = list(weights)

    t, d = x.shape[1], x.shape[2]
    heads = MLSTM_HEADS
    wm = g_mlstm.shape[1]
    pw = pool_scale.shape[1]
    cg = pw // POOL_GROUPS
    in_shard = w_in.shape[2]
    xs, ps, tgt = x[0], p[0, 0], loss_target[0]

    mx, my, mc = lax.axis_index("x"), lax.axis_index("y"), lax.axis_index("c")
    core = jnp.reshape(mc, (1,)).astype(jnp.int32)
    chip = jnp.reshape(2 * mx + my, (1,)).astype(jnp.int32)

    first = [w_in[0].T.astype(BF16), w_qk_conv[0], w_pool[0].reshape(POOL_GROUPS * w_pool.shape[2], cg).astype(BF16)]
    started, token = _gather_start([[w_ple_proj[0].astype(BF16)], first], "gather_start_in")
    anchor = token[0, 0]
    later = lambda w: (w[0] + anchor).astype(BF16)
    ff2_shard = later(w_ff2)
    rest, token = _gather_start([[later(w_out)], [later(w_ff1)], [ff2_shard[:, :d // 2]], [ff2_shard[:, d // 2:]],
                                 [later(w_ple_gate)]], "gather_start_rest")
    started = dict(zip(["ple_proj", "in", "out", "ff1", "ff2_lo", "ff2_hi", "ple_gate"], started + rest))
    bias = _pad_cols(b_gates, LANES)
    small = ["b_gates", "g_mlstm", "pool_scale", "g_mix_pre", "g_mix_post", "g_ff_pre", "g_ff_post", "g_ple_gate",
             "g_ple_post"]
    pack = lambda src: _pad_rows16(jnp.concatenate([_pad_cols(src[k] + anchor, d) for k in small], axis=0))
    small_state = [pack(weights), pack(mom1), pack(mom2)]
    in_state = [a[0].T + anchor for a in (w_in, m_w_in, v_w_in)]

    xn = _norm_fwd(xs, g_mix_pre, "norm_mix_pre", deps=[token])
    (g_pproj,) = _forward_pair(_gather_wait(started["ple_proj"], xn, "gather_wait_ple_proj"), "gather_pair_ple_proj")
    e = _mm_nn_pieces("ple_proj", ps, g_pproj)[0]
    g_in, g_conv, g_pool = _forward_pair(
        _gather_wait(started["in"], [e] + small_state + in_state, "gather_wait_in"), "gather_pair_in")
    in_t_a, in_t_b = _w_in_rows(g_in, 4 * wm, 2 * heads, "w_in_rows")
    conv_w = g_conv.transpose(1, 0, 2).reshape(QK_CONV, 2 * wm)
    pool_w = g_pool.reshape(N_DEV, POOL_GROUPS, cg // N_DEV, cg).transpose(1, 0, 2, 3).reshape(POOL_GROUPS, cg, cg)
    proj_a = _mm_nt("proj_qkvo", xn, in_t_a, n_rows=4 * wm)
    proj_b = _mm_nt("proj_pool_gates", xn, in_t_b)
    qc, qct = _conv_fwd(proj_a, conv_w[:, :wm], 0, "q_conv")
    kc, kct = _conv_fwd(proj_a, conv_w[:, wm:], wm, "k_conv", scale_t=(wm // heads) ** -0.5)
    gb, gbt, gcols = _gates_fwd(proj_b, bias, pw // LANES, heads, "gates")
    gbt = gbt[:2 * heads].reshape(2 * heads, 1, t)
    fwd_out = _forward_start(_gather_wait(started["out"], gb, "gather_wait_out"), "gather_pair_start_out")
    hcat, *mlstm_states = _mlstm_fwd(qc, kc, kct, proj_a, gcols, gbt, g_mlstm, heads, d, "mlstm", fwd_out["token"])
    hcat = _pool_fwd(proj_b, pool_w, pool_scale, hcat, "pool")
    (g_out,) = _forward_wait(fwd_out, hcat, "gather_pair_wait_out")
    out_w = g_out.reshape(d, d)
    mix = _mm_nn("mix_out", hcat, out_w)[0]
    fwd_ff1 = _forward_start(_gather_wait(started["ff1"], mix, "gather_wait_ff1"), "gather_pair_start_ff1")
    h1, hn = _block_fwd(xs, mix, g_mix_post, g_ff_pre, "residual_mix", deps=[fwd_ff1["token"]])
    (g_ff1,) = _forward_wait(fwd_ff1, hn, "gather_pair_wait_ff1")
    relu_a, act = _mm_nn_pieces("ff1", hn, g_ff1, out_dtype=[BF16, BF16],
                                epilogue=lambda acc: (jnp.maximum(acc, 0.0), jnp.square(jnp.maximum(acc, 0.0))))
    (g_ff2_lo,) = _forward_pair(_gather_wait(started["ff2_lo"], act, "gather_wait_ff2_lo"), "gather_pair_ff2_lo")
    ff2_lo = g_ff2_lo.reshape(-1, d // 2)
    ff = _mm_nn("ff2_lo", act, ff2_lo, tn=256, tk=4 * d, out_cols=d)[0]
    (g_ff2_hi,) = _forward_pair(_gather_wait(started["ff2_hi"], ff, "gather_wait_ff2_hi"), "gather_pair_ff2_hi")
    ff2_hi = g_ff2_hi.reshape(-1, d // 2)
    ff = _mm_nn("ff2_hi", act, ff2_hi, tn=256, tk=4 * d, out_cols=d, out_at=d // 2, into=ff)[0]
    fwd_ple = _forward_start(_gather_wait(started["ple_gate"], ff, "gather_wait_ple"), "gather_pair_start_ple")
    h2, hg = _block_fwd(h1, ff, g_ff_post, g_ple_gate, "residual_ff", deps=[fwd_ple["token"]])
    (g_pgate,) = _forward_wait(fwd_ple, hg, "gather_pair_wait_ple")
    pgate_w = g_pgate.reshape(d, d)
    z = _mm_nn("ple_gate", hg, pgate_w)[0]
    dh3, dz, de, dg_ple_post, loss_part = _loss_head(h2, z, e, tgt, g_ple_post, "loss_head")
    loss = lax.psum(loss_part[0, 0], MESH_AXES)

    out = {}

    def state(k):
        if k == "w_in":
            return in_state
        return [a.reshape(-1, a.shape[-1]) for a in (weights[k], mom1[k], mom2[k])]

    def pair_begin(pieces, tag):
        keys = list(pieces)
        return keys, _pair_start([pieces[k] for k in keys], "reduce_pair_start_" + tag)

    def chip_begin(group, after, tag):
        keys, begun = group
        mine, got = _pair_wait(begun, after, "reduce_pair_wait_" + tag)
        sums = [_pair_sum(p_, g, core, "pair_sum_" + k) for k, p_, g in zip(keys, mine, got)]
        return keys, _chip_start(sums, "reduce_start_" + tag)

    def reduce_finish(group, after, tag):
        keys, begun = group
        sums, got = _chip_wait(begun, after, "reduce_wait_" + tag)
        for k, s, g in zip(keys, sums, got):
            res = _adamw_shard(*state(k), s, g, chip, "adamw_" + k)
            out[k] = [r.T[None] if k == "w_in" else r.reshape(weights[k].shape) for r in res]

    token_of = lambda group: group[1]["token"]
    dhg = _mm_nt("d_ple_gate_in", dz, pgate_w)
    gw_pgate = _mm_tn("gw_ple_gate", hg, dz)
    gw_pproj = _mm_tn("gw_ple_proj", ps, de, pieces=N_DEV)
    dh2, dff, dg_ff_post, dg_ple_gate = _block_bwd(h2, ff, dh3, dhg, g_ff_post, g_ple_gate, "d_residual_ff")
    da = _mm_nt("d_ff2_in", dff, ff2_lo, b2=ff2_hi, out_dtype=BF16, extra=[relu_a],
                epilogue=lambda acc, r: (acc * (2.0 * r.astype(F32)),))
    gw_ff2 = _mm_tn("gw_ff2", act, dff)
    ff2_keys = ["w_ff2", "w_ple_gate", "w_ple_proj"]
    pair_ff2 = pair_begin({"w_ff2": gw_ff2.reshape(N_DEV, -1, d),
                           "w_ple_gate": gw_pgate.reshape(N_DEV, d // N_DEV, d), "w_ple_proj": gw_pproj}, "ff2")
    dhn = _mm_nt_pieces("d_ff1_in", da, g_ff1, deps=[token_of(pair_ff2)])
    place = jnp.concatenate([core, chip])
    mine_ff2, got_ff2 = _pair_wait(pair_ff2[1], dhn, "reduce_pair_wait_ff2")
    gw_ff1, sums_ff2 = _mm_tn("gw_ff1", hn, da, pieces=N_DEV, side=_side_pair_sum(mine_ff2[0], got_ff2[0], place))
    sums_ple = [_pair_sum(p_, g, core, "pair_sum_" + k) for k, p_, g in zip(ff2_keys[1:], mine_ff2[1:], got_ff2[1:])]
    red_ff2 = (ff2_keys, _chip_start([sums_ff2] + sums_ple, "reduce_start_ff2"))
    pair_ff1 = pair_begin({"w_ff1": gw_ff1}, "ff1")
    dh1, dmix, dg_mix_post, dg_ff_pre = _block_bwd(h1, mix, dh2, dhn, g_mix_post, g_ff_pre, "d_residual_mix",
                                                   deps=[token_of(pair_ff1), token_of(red_ff2)])
    mine_ff1, got_ff1 = _pair_wait(pair_ff1[1], dmix, "reduce_pair_wait_ff1")
    dhcat, sums_ff1 = _mm_nt("d_mix_in", dmix, out_w, tm=t // 2, tn=d // 2,
                             side=_side_pair_sum(mine_ff1[0], got_ff1[0], place))
    red_ff1 = (["w_ff1"], _chip_start([sums_ff1], "reduce_start_ff1"))
    gw_out = _mm_tn("gw_out", hcat, dmix, deps=[token_of(red_ff1)])
    du, gw_pool, dg_pool_scale = _pool_bwd(proj_b, pool_w, pool_scale, dhcat, "d_pool")
    pair_out = pair_begin({
        "w_out": gw_out.reshape(N_DEV, d // N_DEV, d),
        "w_pool": gw_pool.reshape(POOL_GROUPS, N_DEV, cg // N_DEV, cg).transpose(1, 0, 2, 3).reshape(N_DEV, -1, cg)},
        "out")
    dq, dk, dv, do, dcol, drow, dg_mlstm = _mlstm_bwd(qc, kc, qct, kct, proj_a, gcols, gbt, g_mlstm, mlstm_states, dhcat,
                                                      heads, "d_mlstm", token_of(pair_out))
    red_out = chip_begin(pair_out, dq, "out")
    drow = jnp.pad(drow.reshape(heads, t), ((0, LANES - heads), (0, 0)))
    dgates, dbias = _gates_bwd(proj_b, bias, dcol, drow, pw // LANES, heads, "d_gates")
    dq_pre, gw_conv_q = _conv_bwd(proj_a, conv_w[:, :wm], dq, 0, "d_q_conv")
    dk_pre, gw_conv_k = _conv_bwd(proj_a, conv_w[:, wm:], dk, wm, "d_k_conv")
    gw_conv = jnp.concatenate([gw_conv_q, gw_conv_k], axis=1)
    dproj_a = jnp.concatenate([dq_pre, dk_pre, dv, do], axis=1)
    dproj_b = jnp.concatenate([du, dgates], axis=1)
    sums_ff2, got_ff2 = _chip_wait(red_ff2[1], token_of(red_out), "reduce_wait_ff2")
    gw_a, *update = _mm_tn("gw_in_qkvo", dproj_a, xn,
                           side=_side_adamw(*state("w_ff2"), sums_ff2[0], got_ff2[0], place))
    out["w_ff2"] = [r.reshape(weights["w_ff2"].shape) for r in update]
    for k, s, g in zip(ff2_keys[1:], sums_ff2[1:], got_ff2[1:]):
        out[k] = [r.reshape(weights[k].shape) for r in _adamw_shard(*state(k), s, g, chip, "adamw_" + k)]
    gw_b = _mm_tn("gw_in_pool_gates", dproj_b, xn)
    pair_in = pair_begin({
        "w_in": _w_in_pieces(gw_a, gw_b, in_shard, 2 * heads, "w_in_pieces"),
        "w_qk_conv": gw_conv.reshape(QK_CONV, N_DEV, -1).transpose(1, 0, 2).astype(BF16)}, "in")
    reduce_finish(red_out, token_of(pair_in), "out")
    red_in = chip_begin(pair_in, out["w_out"][0], "in")
    sums_ff1, got_ff1 = _chip_wait(red_ff1[1], token_of(red_in), "reduce_wait_ff1")
    dxn_a, *update = _mm_nn("d_proj_qkvo_in", dproj_a, in_t_a, tn=512, tk=4 * wm, deps=[token_of(red_in)],
                            side=_side_adamw(*state("w_ff1"), sums_ff1[0], got_ff1[0], place))
    out["w_ff1"] = [r.reshape(weights["w_ff1"].shape) for r in update]
    dxn = _mm_nn("d_proj_in", dproj_b, in_t_b, extra=[dxn_a], epilogue=lambda acc, prev: (acc + prev,))[0]
    grad_x, dg_mix_pre = _norm_bwd(xs, dxn, dh1, g_mix_pre, "d_norm_mix_pre")

    small_grads = dict(b_gates=dbias[:, :2 * heads], g_mlstm=dg_mlstm, pool_scale=dg_pool_scale, g_mix_pre=dg_mix_pre,
                       g_mix_post=dg_mix_post, g_ff_pre=dg_ff_pre, g_ff_post=dg_ff_post, g_ple_gate=dg_ple_gate,
                       g_ple_post=dg_ple_post)
    small_begun = _everyone_start([pack(small_grads)], "small_grads_start")
    reduce_finish(red_in, small_begun["token"], "in")
    (small_parts,) = _everyone_wait(small_begun, out["w_in"][0], "small_grads_wait")
    res = _adamw_replicated(*small_state, small_parts, "adamw_replicated")
    for i, k in enumerate(small):
        width = weights[k].shape[1]
        out[k] = [r[i:i + 1, :width] for r in res]

    return (loss, grad_x[None], *[out[k][0] for k in order], *[out[k][1] for k in order],
            *[out[k][2] for k in order], *[out[k][3] for k in order])


def _pad_rows16(a):
    return jnp.pad(a, ((0, 16 - a.shape[0]), (0, 0)))
```

```python
import jax
import jax.numpy as jnp
from jax import lax
from jax.experimental import pallas as pl
from jax.experimental.pallas import tpu as pltpu

F32 = jnp.float32
BF16 = jnp.bfloat16
EPS = 1e-6
N_DEV = 8
N_CHIP = 4
LANES = 128
VMEM_LIMIT = 56 * 1024 * 1024
MLSTM_HEADS = 8
POOL_GROUPS = 4
QK_CONV = 4
CHUNK = 128
ROW_TILE = 256
ADAM_LR, ADAM_B1, ADAM_B2, ADAM_EPS, ADAM_WD, ADAM_STEP = 0.001, 0.9, 0.999, 1e-08, 0.01, 10
MESH_AXES = ("x", "y", "c")
MESH = pl.DeviceIdType.MESH
HBM_SPEC = pl.BlockSpec(memory_space=pltpu.HBM)
SEM_SPEC = pl.BlockSpec(memory_space=pltpu.SEMAPHORE)
DATAFLOW = pltpu.SideEffectType.DATAFLOW_SIDE_EFFECTING


def _params(*sem):
    if sem:
        return pltpu.CompilerParams(dimension_semantics=sem, vmem_limit_bytes=VMEM_LIMIT)
    return pltpu.CompilerParams(vmem_limit_bytes=VMEM_LIMIT)


def _tile(n, pref):
    if n <= pref:
        return n
    t = (pref // LANES) * LANES
    while t >= LANES:
        if n % t == 0:
            return t
        t -= LANES
    return n


def _hbm(a):
    return pltpu.with_memory_space_constraint(a, pltpu.HBM)


def _mesh_place():
    x, y, c = lax.axis_index("x"), lax.axis_index("y"), lax.axis_index("c")
    return x, y, c, [(1 - x, y), (x, 1 - y), (1 - x, 1 - y)]


def _chip_copies(srcs, lands, send, recv):
    x, y, c, chips = _mesh_place()
    return [pltpu.make_async_remote_copy(
        src_ref=srcs[w].at[2 * chip[0] + chip[1]], dst_ref=lands[w].at[k],
        send_sem=send.at[3 * w + k], recv_sem=recv.at[3 * w + k],
        device_id=(*chip, c), device_id_type=MESH)
        for k, chip in enumerate(chips) for w in range(len(srcs))]


def _chip_start(sums, name):
    return _split_start(_chip_copies, sums, [lax.empty((3,) + a.shape[1:], a.dtype) for a in sums], 3, name,
                        peers=_same_core_of_other_chips)


def _chip_wait(started, after, name):
    return _split_wait(_chip_copies, started, after, name)


def _pair_start(pieces, name):
    return _split_start(_pair_copies, pieces, [lax.empty((N_CHIP,) + a.shape[1:], a.dtype) for a in pieces], N_CHIP, name,
                        peers=_sibling)


def _pair_wait(started, after, name):
    return _split_wait(_pair_copies, started, after, name)


def _pair_copies(srcs, lands, send, recv):
    x, y, c, _ = _mesh_place()
    return [pltpu.make_async_remote_copy(
        src_ref=srcs[w].at[2 * ch + 1 - c], dst_ref=lands[w].at[ch],
        send_sem=send.at[N_CHIP * w + ch], recv_sem=recv.at[N_CHIP * w + ch],
        device_id=(x, y, 1 - c), device_id_type=MESH)
        for w in range(len(srcs)) for ch in range(N_CHIP)]


OWN_BARRIER_IDS = {name: i for i, name in enumerate([
    "reduce_pair_start_ff2", "reduce_pair_start_ff1", "reduce_pair_start_out", "reduce_pair_start_in",
    "gather_pair_start_out", "gather_pair_start_ff1", "gather_pair_start_ple",
    "reduce_start_ff2", "reduce_start_ff1", "reduce_start_out", "reduce_start_in",
    "gather_start_in", "gather_start_rest", "gather_pair_in", "gather_pair_ff2_lo", "gather_pair_ff2_hi",
    "gather_pair_ple_proj"])}


def _shake_hands(peers):
    barrier = pltpu.get_barrier_semaphore()
    for peer in peers:
        pl.semaphore_signal(barrier, inc=1, device_id=peer, device_id_type=MESH)
    pl.semaphore_wait(barrier, len(peers))


def _sibling():
    x, y, c, _ = _mesh_place()
    return [(x, y, 1 - c)]


def _same_core_of_other_chips():
    x, y, c, chips = _mesh_place()
    return [(*chip, c) for chip in chips]


def _split_start(copies_of, srcs, lands, per_array, name, peers=None):
    ns, nb = len(srcs), len(srcs) + len(lands)
    n_copies = per_array * len(lands)

    def body(*refs):
        if peers is not None:
            _shake_hands(peers())
        for cp in copies_of(refs[:ns], refs[ns:nb], refs[nb], refs[nb + 1]):
            cp.start()
        refs[-1][...] = jnp.zeros_like(refs[-1])

    params = dict(has_side_effects=DATAFLOW)
    if peers is not None:
        params["collective_id"] = OWN_BARRIER_IDS[name]
    res = pl.pallas_call(
        body, name=name,
        out_shape=[pltpu.SemaphoreType.DMA((n_copies,)), pltpu.SemaphoreType.DMA((n_copies,))]
        + [pltpu.HBM(a.shape, a.dtype) for a in list(srcs) + list(lands)] + [jax.ShapeDtypeStruct((8, LANES), F32)],
        in_specs=[HBM_SPEC] * nb,
        out_specs=[SEM_SPEC, SEM_SPEC] + [HBM_SPEC] * nb + [pl.BlockSpec(memory_space=pltpu.VMEM)],
        input_output_aliases={i: 2 + i for i in range(nb)},
        compiler_params=pltpu.CompilerParams(**params),
    )(*[_hbm(a) for a in srcs], *[_hbm(a) for a in lands])
    return dict(send=res[0], recv=res[1], srcs=list(res[2:2 + ns]), lands=list(res[2 + ns:2 + nb]), token=res[-1])


def _split_wait(copies_of, started, after, name):
    ns = len(started["srcs"])
    nb = ns + len(started["lands"])

    def body(*refs):
        for cp in copies_of(refs[:ns], refs[ns:nb], refs[nb], refs[nb + 1]):
            cp.wait_send()
            cp.wait_recv()

    res = pl.pallas_call(
        body, name=name,
        out_shape=[pltpu.HBM(a.shape, a.dtype) for a in started["srcs"] + started["lands"]],
        in_specs=[HBM_SPEC] * nb + [SEM_SPEC, SEM_SPEC, ANY_SPEC],
        out_specs=[HBM_SPEC] * nb,
        input_output_aliases={i: i for i in range(nb)},
        compiler_params=pltpu.CompilerParams(has_side_effects=DATAFLOW),
    )(*started["srcs"], *started["lands"], started["send"], started["recv"], after)
    return list(res[:ns]), list(res[ns:])


def _forward_copies(srcs, lands, send, recv):
    del srcs
    x, y, c, chips = _mesh_place()
    copies = []
    for w in range(len(lands)):
        for j, chip in enumerate(chips):
            rows = lands[w].at[4 * chip[0] + 2 * chip[1] + c]
            copies.append(pltpu.make_async_remote_copy(
                src_ref=rows, dst_ref=rows, send_sem=send.at[3 * w + j], recv_sem=recv.at[3 * w + j],
                device_id=(x, y, 1 - c), device_id_type=MESH))
    return copies


def _forward_start(lands, name):
    return _split_start(_forward_copies, [], lands, 3, name, peers=_sibling)


def _forward_wait(started, after, name):
    return _split_wait(_forward_copies, started, after, name)[1]


def _everyone_copies(srcs, lands, send, recv):
    x, y, c, _ = _mesh_place()
    me = 4 * x + 2 * y + c
    copies = []
    for w in range(len(srcs)):
        for k in range(N_DEV - 1):
            flip = k + 1
            peer = (1 - x if flip & 4 else x, 1 - y if flip & 2 else y, 1 - c if flip & 1 else c)
            copies.append(pltpu.make_async_remote_copy(
                src_ref=srcs[w], dst_ref=lands[w].at[me], send_sem=send.at[7 * w + k], recv_sem=recv.at[7 * w + k],
                device_id=peer, device_id_type=MESH))
    return copies


def _everyone_start(blocks, name):
    me = 4 * lax.axis_index("x") + 2 * lax.axis_index("y") + lax.axis_index("c")
    lands = [lax.dynamic_update_slice(lax.empty((N_DEV,) + a.shape, a.dtype), a[None], (me,) + (0,) * a.ndim)
             for a in blocks]
    return _split_start(_everyone_copies, blocks, lands, N_DEV - 1, name)


def _everyone_wait(started, after, name):
    return _split_wait(_everyone_copies, started, after, name)[1]


def _gather_copies(srcs, lands, send, recv):
    x, y, c, chips = _mesh_place()
    me = 4 * x + 2 * y + c
    targets = [(x, y, 1 - c)] + [(*chip, c) for chip in chips]
    return [pltpu.make_async_remote_copy(
        src_ref=srcs[w], dst_ref=lands[w].at[me], send_sem=send.at[4 * w + k], recv_sem=recv.at[4 * w + k],
        device_id=to, device_id_type=MESH)
        for w in range(len(srcs)) for k, to in enumerate(targets)]


def _gather_start(groups, name):
    sizes = [len(g) for g in groups]
    flat = [a for g in groups for a in g]
    n, ng = len(flat), len(groups)
    me = 4 * lax.axis_index("x") + 2 * lax.axis_index("y") + lax.axis_index("c")
    lands = [lax.dynamic_update_slice(lax.empty((N_DEV,) + a.shape, a.dtype), a[None], (me,) + (0,) * a.ndim)
             for a in flat]

    def body(*refs):
        _shake_hands(_sibling() + _same_core_of_other_chips())
        srcs, zones = refs[:n], refs[n:2 * n]
        sems = refs[2 * n:2 * n + 2 * ng]
        token = refs[-1]
        lo = 0
        for gi, size in enumerate(sizes):
            for cp in _gather_copies(srcs[lo:lo + size], zones[lo:lo + size], sems[2 * gi], sems[2 * gi + 1]):
                cp.start()
            lo += size
        token[...] = jnp.zeros_like(token)

    sem_shapes = []
    for size in sizes:
        sem_shapes += [pltpu.SemaphoreType.DMA((4 * size,))] * 2
    res = pl.pallas_call(
        body, name=name,
        out_shape=sem_shapes + [pltpu.HBM(a.shape, a.dtype) for a in flat + lands] + [jax.ShapeDtypeStruct((8, LANES), F32)],
        in_specs=[HBM_SPEC] * (2 * n),
        out_specs=[SEM_SPEC] * (2 * ng) + [HBM_SPEC] * (2 * n) + [pl.BlockSpec(memory_space=pltpu.VMEM)],
        input_output_aliases={i: 2 * ng + i for i in range(2 * n)},
        compiler_params=pltpu.CompilerParams(has_side_effects=DATAFLOW, collective_id=OWN_BARRIER_IDS[name]),
    )(*[_hbm(a) for a in flat], *[_hbm(a) for a in lands])
    out, lo = [], 0
    for gi, size in enumerate(sizes):
        out.append(dict(send=res[2 * gi], recv=res[2 * gi + 1],
                        srcs=list(res[2 * ng + lo:2 * ng + lo + size]),
                        lands=list(res[2 * ng + n + lo:2 * ng + n + lo + size])))
        lo += size
    return out, res[-1]


def _gather_wait(started, after, name):
    n = len(started["srcs"])
    after = list(after) if isinstance(after, (list, tuple)) else [after]

    def body(*refs):
        srcs, zones = refs[:n], refs[n:2 * n]
        send, recv = refs[2 * n], refs[2 * n + 1]
        for cp in _gather_copies(srcs, zones, send, recv):
            cp.wait_send()
            cp.wait_recv()

    res = pl.pallas_call(
        body, name=name,
        out_shape=[pltpu.HBM(a.shape, a.dtype) for a in started["srcs"] + started["lands"]],
        in_specs=[HBM_SPEC] * (2 * n) + [SEM_SPEC, SEM_SPEC] + [ANY_SPEC] * len(after),
        out_specs=[HBM_SPEC] * (2 * n),
        input_output_aliases={i: i for i in range(2 * n)},
        compiler_params=pltpu.CompilerParams(has_side_effects=DATAFLOW),
    )(*started["srcs"], *started["lands"], started["send"], started["recv"], *after)
    return list(res[n:])


def _forward_pair(lands, name):
    n = len(lands)

    def body(*refs):
        _shake_hands(_sibling())
        zones = refs[:n]
        send, recv = refs[2 * n:]
        x, y, c, chips = _mesh_place()

        def copy(w, j, core):
            rows = zones[w].at[4 * chips[j][0] + 2 * chips[j][1] + core]
            return pltpu.make_async_remote_copy(
                src_ref=rows, dst_ref=rows, send_sem=send.at[3 * w + j], recv_sem=recv.at[3 * w + j],
                device_id=(x, y, 1 - c), device_id_type=MESH)

        sends = [copy(w, j, c) for w in range(n) for j in range(3)]
        for cp in sends:
            cp.start()
        for w in range(n):
            for j in range(3):
                copy(w, j, 1 - c).wait_recv()
        for cp in sends:
            cp.wait_send()

    return list(pl.pallas_call(
        body, name=name,
        out_shape=[jax.ShapeDtypeStruct(a.shape, a.dtype) for a in lands],
        in_specs=[HBM_SPEC] * n, out_specs=[HBM_SPEC] * n,
        input_output_aliases={i: i for i in range(n)},
        scratch_shapes=[pltpu.SemaphoreType.DMA((3 * n,)), pltpu.SemaphoreType.DMA((3 * n,))],
        compiler_params=pltpu.CompilerParams(collective_id=OWN_BARRIER_IDS[name]),
    )(*lands))


def _pair_sum(pieces, got, core, name):
    _, rows, cols = pieces.shape
    tr, tc = _tile_2d(rows, cols)

    def body(core_ref, mine_ref, got_ref, out_ref):
        del core_ref
        out_ref[...] = (mine_ref[...].astype(F32) + got_ref[...].astype(F32)).astype(out_ref.dtype)

    return pl.pallas_call(
        body, name=name,
        out_shape=jax.ShapeDtypeStruct((N_CHIP, rows, cols), pieces.dtype),
        grid_spec=pltpu.PrefetchScalarGridSpec(
            num_scalar_prefetch=1, grid=(N_CHIP, rows // tr, cols // tc),
            in_specs=[pl.BlockSpec((None, tr, tc), lambda ch, i, j, core: (2 * ch + core[0], i, j)),
                      pl.BlockSpec((None, tr, tc), lambda ch, i, j, core: (ch, i, j))],
            out_specs=pl.BlockSpec((None, tr, tc), lambda ch, i, j, core: (ch, i, j))),
        compiler_params=_params("parallel", "parallel", "parallel"),
    )(core, pieces, got)


def _tile_2d(rows, cols, budget=2 * 1024 * 1024):
    want = max(16, budget // (4 * cols))
    if rows <= want:
        return rows, cols
    t = (want // 16) * 16
    while t >= 16:
        if rows % t == 0:
            return t, cols
        t -= 16
    return rows, _tile(cols, max(LANES, budget // (4 * rows)))


def _adamw_math(w, g, m, v):
    m = ADAM_B1 * m + (1.0 - ADAM_B1) * g
    v = ADAM_B2 * v + (1.0 - ADAM_B2) * (g * g)
    m_hat = m / (1.0 - ADAM_B1 ** ADAM_STEP)
    v_hat = v / (1.0 - ADAM_B2 ** ADAM_STEP)
    delta = -ADAM_LR * (m_hat / (jnp.sqrt(v_hat) + ADAM_EPS) + ADAM_WD * w)
    return delta, m, v


def _adamw_shard(w, m, v, sums, got, chip, name):
    rows, cols = w.shape
    tr, tc = _tile_2d(rows, cols, budget=1024 * 1024)

    def body(chip_ref, w_ref, m_ref, v_ref, own_ref, got_ref, g_out, d_out, m_out, v_out):
        del chip_ref
        g = own_ref[...].astype(F32)
        for k in range(3):
            g = g + got_ref[k].astype(F32)
        delta, m_new, v_new = _adamw_math(w_ref[...], g, m_ref[...], v_ref[...])
        g_out[...] = g
        d_out[...] = delta
        m_out[...] = m_new
        v_out[...] = v_new

    blk = pl.BlockSpec((tr, tc), lambda i, j, chip: (i, j))
    return pl.pallas_call(
        body, name=name,
        out_shape=[jax.ShapeDtypeStruct((rows, cols), F32)] * 4,
        grid_spec=pltpu.PrefetchScalarGridSpec(
            num_scalar_prefetch=1, grid=(rows // tr, cols // tc),
            in_specs=[blk, blk, blk,
                      pl.BlockSpec((None, tr, tc), lambda i, j, chip: (chip[0], i, j)),
                      pl.BlockSpec((3, tr, tc), lambda i, j, chip: (0, i, j))],
            out_specs=[blk] * 4),
        compiler_params=_params("parallel", "parallel"),
    )(chip, w, m, v, sums, got)


def _adamw_replicated(w, m, v, parts, name):
    rows, cols = w.shape

    def body(w_ref, m_ref, v_ref, parts_ref, g_out, d_out, m_out, v_out):
        g = parts_ref[0]
        for d in range(1, N_DEV):
            g = g + parts_ref[d]
        delta, m_new, v_new = _adamw_math(w_ref[...], g, m_ref[...], v_ref[...])
        g_out[...] = g
        d_out[...] = delta
        m_out[...] = m_new
        v_out[...] = v_new

    return pl.pallas_call(
        body, name=name,
        out_shape=[jax.ShapeDtypeStruct((rows, cols), F32)] * 4,
        compiler_params=_params(),
    )(w, m, v, parts)


NN = ((1,), (0,))
NT = ((1,), (1,))
TN = ((0,), (0,))


ANY_SPEC = pl.BlockSpec(memory_space=pl.ANY)


def _mm(name, a, b, *, dims, grid, a_spec, b_spec, outs, extra=(), extra_specs=(), epilogue=None, acc_shape=None,
        deps=(), b_pieces=0, side=None, b2=None, into=None):
    nk = grid[2]
    n_extra, n_out = len(extra), len(outs)
    side_ins = side["ins"] if side else []
    side_outs = side["outs"] if side else []
    first_side = 2 + n_extra + len(deps)
    tail = ([b2] if b2 is not None else []) + ([into] if into is not None else [])
    first_out = first_side + len(side_ins) + len(tail)
    assert not side or nk == 1

    assert not side or (n_out == 1 and not b_pieces and not tail)
    SIDE_SLICES = 4

    def body(*refs):
        dot = lambda lhs, rhs: lax.dot_general(lhs.astype(BF16), rhs.astype(BF16), (dims, ((), ())),
                                               preferred_element_type=F32)
        if side:
            refs = refs[1:]
            a_ref, b_ref, out_ref = refs[0], refs[1], refs[first_out]
            ins, sides = refs[first_side:first_out], refs[first_out + 1:first_out + 1 + len(side_outs)]
            rows = out_ref.shape[0] // SIDE_SLICES
            srows = sides[0].shape[-2] // SIDE_SLICES
            for part in range(SIDE_SLICES):
                rs = slice(part * srows, (part + 1) * srows)
                for val, o in zip(side["fn"](*[r[..., rs, :] for r in ins]), sides):
                    o[rs, :] = val.astype(o.dtype)
                ms = slice(part * rows, (part + 1) * rows)
                lhs = a_ref[:, ms] if dims == TN else a_ref[ms, :]
                acc = dot(lhs, b_ref[...])
                if epilogue:
                    (acc,) = epilogue(acc, *[e[ms, :] for e in refs[2:2 + n_extra]])
                out_ref[ms, :] = acc.astype(out_ref.dtype)
            return
        a_ref, b_ref = refs[0], refs[1]
        ex = refs[2:2 + n_extra]
        out_refs = refs[first_out:first_out + n_out]
        if b_pieces:
            ks = b_ref.shape[-1]
            part = dot(a_ref[:, 0:ks], b_ref[0])
            for piece in range(1, b_pieces):
                part = part + dot(a_ref[:, piece * ks:(piece + 1) * ks], b_ref[piece])
        elif b2 is not None:
            ks = b_ref.shape[-1]
            part = dot(a_ref[:, 0:ks], b_ref[...]) + dot(a_ref[:, ks:2 * ks], refs[first_side][...])
        else:
            part = dot(a_ref[...], b_ref[...])

        def finish(acc):
            res = epilogue(acc, *[e[...] for e in ex]) if epilogue else (acc,)
            for val, o in zip(res, out_refs):
                o[...] = val.astype(o.dtype)

        if nk == 1:
            finish(part)
        else:
            acc_ref = refs[-1]
            k = pl.program_id(2)

            @pl.when(k == 0)
            def _():
                acc_ref[...] = part

            @pl.when(k > 0)
            def _():
                acc_ref[...] += part

            @pl.when(k == nk - 1)
            def _():
                finish(acc_ref[...])

    scratch = [pltpu.VMEM(acc_shape, F32)] if nk > 1 else []
    out_shape = [jax.ShapeDtypeStruct(s, d) for s, d, _ in outs]
    in_specs = [a_spec, b_spec, *extra_specs] + [ANY_SPEC] * len(deps)
    out_specs = [sp for _, _, sp in outs]
    if not side:
        in_specs += ([b_spec] if b2 is not None else []) + ([ANY_SPEC] if into is not None else [])
        return list(pl.pallas_call(
            body, name=name, out_shape=out_shape, grid=grid, in_specs=in_specs, out_specs=out_specs,
            scratch_shapes=scratch, compiler_params=_params("parallel", "parallel", "arbitrary"),
            input_output_aliases={len(in_specs) - 1: 0} if into is not None else {},
        )(a, b, *extra, *deps, *tail))

    def with_scalars(spec):
        if spec.index_map is None:
            return spec
        return pl.BlockSpec(spec.block_shape, lambda i, j, k, s, f=spec.index_map: f(i, j, k))

    def by_step(block, index_map):
        return pl.BlockSpec(block, lambda i, j, k, s: index_map((i * grid[1] + j) * grid[2] + k, s))

    return list(pl.pallas_call(
        body, name=name,
        out_shape=out_shape + [jax.ShapeDtypeStruct(s, d) for s, d, _, _ in side_outs],
        grid_spec=pltpu.PrefetchScalarGridSpec(
            num_scalar_prefetch=1, grid=grid,
            in_specs=[with_scalars(sp) for sp in in_specs] + [by_step(blk, im) for _, blk, im in side_ins],
            out_specs=[with_scalars(sp) for sp in out_specs] + [by_step(blk, im) for _, _, blk, im in side_outs],
            scratch_shapes=scratch),
        compiler_params=_params("arbitrary", "arbitrary", "arbitrary"),
    )(side["scalars"], a, b, *extra, *deps, *[arr for arr, _, _ in side_ins]))


def _side_pair_sum(pieces, got, place):
    _, rows, cols = pieces.shape

    def build(steps):
        per_chip = steps // N_CHIP
        tr = rows // per_chip
        assert per_chip * N_CHIP == steps and tr * per_chip == rows and tr % 16 == 0
        blk = (None, tr, cols)
        here = lambda st, s: (st // per_chip, st % per_chip, 0)
        return dict(scalars=place,
                    ins=[(pieces, blk, lambda st, s: (2 * (st // per_chip) + s[0], st % per_chip, 0)), (got, blk, here)],
                    outs=[((N_CHIP, rows, cols), pieces.dtype, blk, here)],
                    fn=lambda mine, other: (mine.astype(F32) + other.astype(F32),))
    return build


def _side_adamw(w, m, v, sums, got, place):
    rows, cols = w.shape

    def build(steps):
        tr = rows // steps
        assert tr * steps == rows and tr % 16 == 0
        blk, here = (tr, cols), (lambda st, s: (st, 0))

        def fn(w, m, v, own, got):
            g = own.astype(F32)
            for k in range(3):
                g = g + got[k].astype(F32)
            delta, m_new, v_new = _adamw_math(w, g, m, v)
            return g, delta, m_new, v_new

        return dict(scalars=place,
                    ins=[(w, blk, here), (m, blk, here), (v, blk, here),
                         (sums, (None, tr, cols), lambda st, s: (s[1], st, 0)),
                         (got, (3, tr, cols), lambda st, s: (0, st, 0))],
                    outs=[((rows, cols), F32, blk, here)] * 4, fn=fn)
    return build


def _mm_nn(name, a, b, out_dtype=F32, tm=1024, tn=1024, tk=2048, epilogue=None, n_out=1, extra=(), deps=(), side=None,
           out_cols=None, out_at=0, into=None):
    m, kd = a.shape
    n = b.shape[1]
    tm, tn, tk = _tile(m, tm), _tile(n, tn), _tile(kd, tk)
    dts = out_dtype if isinstance(out_dtype, (list, tuple)) else [out_dtype] * n_out
    shift = out_at // tn
    o_spec = pl.BlockSpec((tm, tn), lambda i, j, k: (i, j + shift))
    grid = (m // tm, n // tn, kd // tk)
    return _mm(name, a, b, dims=NN, grid=grid,
               a_spec=pl.BlockSpec((tm, tk), lambda i, j, k: (i, k)),
               b_spec=pl.BlockSpec((tk, tn), lambda i, j, k: (k, j)),
               outs=[((m, out_cols or n), dt, o_spec) for dt in dts], extra=extra, extra_specs=[o_spec] * len(extra),
               epilogue=epilogue, acc_shape=(tm, tn), deps=deps, side=side and side(grid[0] * grid[1] * grid[2]),
               into=into)


def _mm_nn_pieces(name, a, b, out_dtype=F32, tm=1024, tk=2048, epilogue=None, n_out=1):
    m, kd = a.shape
    npc, _, ns = b.shape
    tm, tk = _tile(m, tm), _tile(kd, tk)
    tn = _tile(ns, 1024)
    r = ns // tn
    dts = out_dtype if isinstance(out_dtype, (list, tuple)) else [out_dtype] * n_out
    o_spec = pl.BlockSpec((tm, tn), lambda i, j, k: (i, j))
    return _mm(name, a, b, dims=NN, grid=(m // tm, npc * r, kd // tk),
               a_spec=pl.BlockSpec((tm, tk), lambda i, j, k: (i, k)),
               b_spec=pl.BlockSpec((None, tk, tn), lambda i, j, k: (j // r, k, j % r)),
               outs=[((m, npc * ns), dt, o_spec) for dt in dts], epilogue=epilogue, acc_shape=(tm, tn))


def _mm_nt(name, a, b, out_dtype=F32, tm=1024, tn=1024, tk=2048, epilogue=None, extra=(), n_rows=None, deps=(),
           side=None, b2=None):
    m, kd = a.shape
    n = n_rows or b.shape[0]
    tm, tn, tk = _tile(m, tm), _tile(n, tn), _tile(kd, tk)
    o_spec = pl.BlockSpec((tm, tn), lambda i, j, k: (i, j))
    if b2 is not None:
        tk = kd
        b_spec = pl.BlockSpec((tn, kd // 2), lambda i, j, k: (j, 0))
    else:
        b_spec = pl.BlockSpec((tn, tk), lambda i, j, k: (j, k))
    grid = (m // tm, n // tn, kd // tk)
    res = _mm(name, a, b, dims=NT, grid=grid,
              a_spec=pl.BlockSpec((tm, tk), lambda i, j, k: (i, k)), b_spec=b_spec,
              outs=[((m, n), out_dtype, o_spec)], extra=extra, extra_specs=[o_spec] * len(extra),
              epilogue=epilogue, acc_shape=(tm, tn), deps=deps, side=side and side(grid[0] * grid[1] * grid[2]), b2=b2)
    return res if side else res[0]


def _mm_nt_pieces(name, a, b, out_dtype=F32, tm=1024, tn=256, deps=()):
    m = a.shape[0]
    npc, n, ks = b.shape
    tm, tn = _tile(m, tm), _tile(n, tn)
    return _mm(name, a, b, dims=NT, grid=(m // tm, n // tn, 1),
               a_spec=pl.BlockSpec((tm, npc * ks), lambda i, j, k: (i, 0)),
               b_spec=pl.BlockSpec((npc, tn, ks), lambda i, j, k: (0, j, 0)),
               outs=[((m, n), out_dtype, pl.BlockSpec((tm, tn), lambda i, j, k: (i, j)))],
               deps=deps, b_pieces=npc)[0]


def _mm_tn(name, a, b, out_dtype=BF16, tm=1024, tn=1024, pieces=0, deps=(), side=None):
    t, m = a.shape
    n = b.shape[1]
    tm = _tile(m, tm)
    if pieces:
        ns = n // pieces
        tn = _tile(ns, tn)
        r = ns // tn
        out = ((pieces, m, ns), out_dtype, pl.BlockSpec((None, tm, tn), lambda i, j, k: (j // r, i, j % r)))
    else:
        tn = _tile(n, tn)
        out = ((m, n), out_dtype, pl.BlockSpec((tm, tn), lambda i, j, k: (i, j)))
    grid = (m // tm, n // tn, 1)
    res = _mm(name, a, b, dims=TN, grid=grid,
              a_spec=pl.BlockSpec((t, tm), lambda i, j, k: (0, i)),
              b_spec=pl.BlockSpec((t, tn), lambda i, j, k: (0, j)),
              outs=[out], deps=deps, side=side and side(grid[0] * grid[1]))
    return res if side else res[0]


def _rms_f(x, g):
    r = lax.rsqrt(jnp.mean(x * x, axis=-1, keepdims=True) + EPS)
    xh = x * r
    return xh * g, xh, r


def _rms_b(dy, g, xh, r):
    dxh = dy * g
    dx = r * (dxh - xh * jnp.mean(dxh * xh, axis=-1, keepdims=True))
    dg = jnp.sum(dy * xh, axis=0, keepdims=True)
    return dx, dg


def _rowwise(name, fn, row_ins, vec_ins, row_outs, vec_outs, deps=()):
    t = row_ins[0].shape[0]
    tr = min(ROW_TILE, t)
    nr, nv, no = len(row_ins), len(vec_ins), len(row_outs)
    first_out = nr + nv + len(deps)

    def body(*refs):
        ri, vi = refs[:nr], refs[nr:nr + nv]
        ro, vo = refs[first_out:first_out + no], refs[first_out + no:]
        routs, vouts = fn(*[r[...] for r in ri], *[v[...] for v in vi])
        for o, val in zip(ro, routs):
            o[...] = val.astype(o.dtype)
        i = pl.program_id(0)
        for o, val in zip(vo, vouts):
            @pl.when(i == 0)
            def _():
                o[...] = val

            @pl.when(i > 0)
            def _():
                o[...] += val

    res = pl.pallas_call(
        body, name=name,
        out_shape=[jax.ShapeDtypeStruct((t, w), d) for w, d in row_outs]
        + [jax.ShapeDtypeStruct((1, w), F32) for w in vec_outs],
        grid=(t // tr,),
        in_specs=[pl.BlockSpec((tr, a.shape[1]), lambda i: (i, 0)) for a in row_ins]
        + [pl.BlockSpec((1, a.shape[1]), lambda i: (0, 0)) for a in vec_ins] + [ANY_SPEC] * len(deps),
        out_specs=[pl.BlockSpec((tr, w), lambda i: (i, 0)) for w, _ in row_outs]
        + [pl.BlockSpec((1, w), lambda i: (0, 0)) for w in vec_outs],
        compiler_params=_params("arbitrary"),
    )(*row_ins, *vec_ins, *deps)
    return list(res)


def _norm_fwd(x, g, name, deps=()):
    d = x.shape[1]
    return _rowwise(name, lambda x, g: ((_rms_f(x, g)[0],), ()), [x], [g], [(d, BF16)], [], deps=deps)[0]


def _norm_bwd(x, dxn, dres, g, name):
    d = x.shape[1]

    def fn(x, dxn, dres, g):
        _, xh, r = _rms_f(x, g)
        dx, dg = _rms_b(dxn, g, xh, r)
        return (dres + dx,), (dg,)

    return _rowwise(name, fn, [x, dxn, dres], [g], [(d, F32)], [d])


def _block_fwd(h_prev, y, g_y, g_n, name, deps=()):
    d = h_prev.shape[1]

    def fn(h_prev, y, g_y, g_n):
        h = h_prev + _rms_f(y, g_y)[0]
        return (h, _rms_f(h, g_n)[0]), ()

    return _rowwise(name, fn, [h_prev, y], [g_y, g_n], [(d, F32), (d, BF16)], [], deps=deps)


def _block_bwd(h, y, d_up, d_n, g_y, g_n, name, deps=()):
    d = h.shape[1]

    def fn(h, y, d_up, d_n, g_y, g_n):
        _, hh, hr = _rms_f(h, g_n)
        dh_n, dg_n = _rms_b(d_n, g_n, hh, hr)
        dh = d_up + dh_n
        _, yh, yr = _rms_f(y, g_y)
        dy, dg_y = _rms_b(dh, g_y, yh, yr)
        return (dh, dy), (dg_y, dg_n)

    return _rowwise(name, fn, [h, y, d_up, d_n], [g_y, g_n], [(d, F32), (d, BF16)], [d, d], deps=deps)


def _loss_head(h2, z, e, target, g, name):
    d = h2.shape[1]

    def fn(h2, z, e, target, g):
        gate = jax.nn.sigmoid(z)
        y, xh, r = _rms_f(e * gate, g)
        diff = h2 + y - target
        loss = 0.5 * jnp.sum(jnp.mean(diff * diff, axis=-1, keepdims=True), axis=0, keepdims=True)
        dh3 = diff / d
        deg, dg = _rms_b(dh3, g, xh, r)
        dz = deg * e * gate * (1.0 - gate)
        de = deg * gate
        return (dh3, dz, de), (dg, jnp.broadcast_to(loss, (1, LANES)))

    return _rowwise(name, fn, [h2, z, e, target], [g], [(d, F32), (d, BF16), (d, BF16)], [d, LANES])


def _shift_down(x, s, rows):
    return jnp.where(rows >= s, pltpu.roll(x, s, axis=0), 0.0)


def _shift_up(x, s, rows):
    t = x.shape[0]
    return jnp.where(rows < t - s, pltpu.roll(x, t - s, axis=0), 0.0)


def _conv_pre(x, w, rows):
    pre = x * w[QK_CONV - 1:QK_CONV, :]
    for s in range(1, QK_CONV):
        pre = pre + _shift_down(x, s, rows) * w[QK_CONV - 1 - s:QK_CONV - s, :]
    return pre


def _conv_fwd(proj_a, w, start, name, scale_t=1.0):
    t = proj_a.shape[0]
    width = w.shape[1]
    tc = _tile(width, 256)
    off = start // tc

    def body(x_ref, w_ref, y_ref, yt_ref):
        x = x_ref[...]
        rows = lax.broadcasted_iota(jnp.int32, x.shape, 0)
        pre = _conv_pre(x, w_ref[...], rows)
        y = pre * jax.nn.sigmoid(pre)
        y_ref[...] = y
        yt_ref[...] = (y * scale_t).T.astype(yt_ref.dtype)

    return pl.pallas_call(
        body, name=name,
        out_shape=[jax.ShapeDtypeStruct((t, width), F32), jax.ShapeDtypeStruct((width, t), BF16)],
        grid=(width // tc,),
        in_specs=[pl.BlockSpec((t, tc), lambda j: (0, off + j)), pl.BlockSpec((QK_CONV, tc), lambda j: (0, j))],
        out_specs=[pl.BlockSpec((t, tc), lambda j: (0, j)), pl.BlockSpec((tc, t), lambda j: (j, 0))],
        compiler_params=_params("parallel"),
    )(proj_a, w)


def _conv_bwd(proj_a, w, dy, start, name):
    t = proj_a.shape[0]
    width = w.shape[1]
    tc = _tile(width, 256)
    off = start // tc

    def body(x_ref, w_ref, dy_ref, dx_ref, dw_ref):
        x, w = x_ref[...], w_ref[...]
        rows = lax.broadcasted_iota(jnp.int32, x.shape, 0)
        pre = _conv_pre(x, w, rows)
        sig = jax.nn.sigmoid(pre)
        dpre = dy_ref[...] * (sig * (1.0 + pre * (1.0 - sig)))
        dx = dpre * w[QK_CONV - 1:QK_CONV, :]
        dws = [jnp.sum(dpre * x, axis=0, keepdims=True)]
        for s in range(1, QK_CONV):
            dx = dx + _shift_up(dpre, s, rows) * w[QK_CONV - 1 - s:QK_CONV - s, :]
            dws.append(jnp.sum(dpre * _shift_down(x, s, rows), axis=0, keepdims=True))
        dx_ref[...] = dx.astype(dx_ref.dtype)
        for s in range(QK_CONV):
            dw_ref[QK_CONV - 1 - s:QK_CONV - s, :] = dws[s]

    return pl.pallas_call(
        body, name=name,
        out_shape=[jax.ShapeDtypeStruct((t, width), BF16), jax.ShapeDtypeStruct((QK_CONV, width), F32)],
        grid=(width // tc,),
        in_specs=[pl.BlockSpec((t, tc), lambda j: (0, off + j)), pl.BlockSpec((QK_CONV, tc), lambda j: (0, j)),
                  pl.BlockSpec((t, tc), lambda j: (0, j))],
        out_specs=[pl.BlockSpec((t, tc), lambda j: (0, j)), pl.BlockSpec((QK_CONV, tc), lambda j: (0, j))],
        compiler_params=_params("parallel"),
    )(proj_a, w, dy)


def _dot(a, b, dims):
    return lax.dot_general(a.astype(BF16), b.astype(BF16), (dims, ((), ())), preferred_element_type=F32)


def _dot_split(ones, x):
    hi = x.astype(BF16)
    rest = x - hi.astype(F32)
    mid = rest.astype(BF16)
    lo = (rest - mid.astype(F32)).astype(BF16)
    dot = lambda part: lax.dot_general(ones, part, (NN, ((), ())), preferred_element_type=F32)
    return dot(hi) + dot(mid) + dot(lo)


def _chunk_iotas():
    rows = lax.broadcasted_iota(jnp.int32, (CHUNK, CHUNK), 0)
    cols = lax.broadcasted_iota(jnp.int32, (CHUNK, CHUNK), 1)
    return rows, cols


def _gates_fwd(proj_b, bias, col_block, heads, name):
    t = proj_b.shape[0]

    def body(x_ref, b_ref, gb_ref, gbt_ref, cols_ref):
        rows, cols = _chunk_iotas()
        tri = (cols <= rows).astype(BF16)
        for ci in range(t // CHUNK):
            sl = slice(ci * CHUNK, (ci + 1) * CHUNK)
            pre = x_ref[sl, :] + b_ref[...]
            log_f = jnp.minimum(pre, 0.0) - jnp.log(1.0 + jnp.exp(-jnp.abs(pre)))
            log_f = jnp.where((cols >= heads) & (cols < 2 * heads), log_f, 0.0)
            tile = jnp.where(cols < heads, pre, _dot_split(tri, log_f))
            gb_ref[sl, :] = tile
            gbt_ref[:, sl] = tile.T
            for h in range(2 * heads):
                cols_ref[h, sl, :] = jnp.broadcast_to(tile[:, h:h + 1], (CHUNK, LANES))

    return pl.pallas_call(
        body, name=name,
        out_shape=[jax.ShapeDtypeStruct((t, LANES), F32), jax.ShapeDtypeStruct((LANES, t), F32),
                   jax.ShapeDtypeStruct((2 * heads, t, LANES), F32)],
        grid=(1,),
        in_specs=[pl.BlockSpec((t, LANES), lambda i: (0, col_block)), pl.BlockSpec((1, LANES), lambda i: (0, 0))],
        out_specs=[pl.BlockSpec((t, LANES), lambda i: (0, 0)), pl.BlockSpec((LANES, t), lambda i: (0, 0)),
                   pl.BlockSpec((2 * heads, t, LANES), lambda i: (0, 0, 0))],
        compiler_params=_params("arbitrary"),
    )(proj_b, bias)


def _gates_bwd(proj_b, bias, dcol, drow, col_block, heads, name):
    t = proj_b.shape[0]

    def body(x_ref, b_ref, dc_ref, dr_ref, dx_ref, db_ref):
        rows, cols = _chunk_iotas()
        later = (cols >= rows).astype(BF16)
        total = jnp.zeros((1, LANES), F32)
        for ci in range(t // CHUNK):
            sl = slice(ci * CHUNK, (ci + 1) * CHUNK)
            by_row = dr_ref[:, sl].T
            d = dc_ref[sl, :] + jnp.where(cols < heads, by_row,
                                          jnp.where(cols < 2 * heads, -pltpu.roll(by_row, heads, axis=1), 0.0))
            d_log_f = _dot_split(later, jnp.where(cols >= heads, d, 0.0))
            pre = x_ref[sl, :] + b_ref[...]
            dx = jnp.where(cols < heads, d, jnp.where(cols < 2 * heads, d_log_f * jax.nn.sigmoid(-pre), 0.0))
            dx_ref[sl, :] = dx.astype(dx_ref.dtype)
            total = total + jnp.sum(dx, axis=0, keepdims=True)
        db_ref[...] = total

    return pl.pallas_call(
        body, name=name,
        out_shape=[jax.ShapeDtypeStruct((t, LANES), BF16), jax.ShapeDtypeStruct((1, LANES), F32)],
        grid=(1,),
        in_specs=[pl.BlockSpec((t, LANES), lambda i: (0, col_block)), pl.BlockSpec((1, LANES), lambda i: (0, 0)),
                  pl.BlockSpec((t, LANES), lambda i: (0, 0)), pl.BlockSpec((LANES, t), lambda i: (0, 0))],
        out_specs=[pl.BlockSpec((t, LANES), lambda i: (0, 0)), pl.BlockSpec((1, LANES), lambda i: (0, 0))],
        compiler_params=_params("arbitrary"),
    )(proj_b, bias, dcol, drow)


def _gate_views(li_ref, b_ref, gbt_ref, s, head, heads, sl):
    li_c, b_c = li_ref[s, sl, :], b_ref[s, sl, :]
    li_r = jnp.broadcast_to(gbt_ref[head, :, sl], (CHUNK, CHUNK))
    b_r = jnp.broadcast_to(gbt_ref[heads + head, :, sl], (CHUNK, CHUNK))
    return li_c, b_c, li_r, b_r


def _chunk_state(kt, k, v, li_c, b_c, ct, nrow, m_prev):
    b_tot = b_c[CHUNK - 1:CHUNK, 0:1]
    a = b_tot - b_c[:, 0:1] + li_c[:, 0:1]
    m_new = jnp.maximum(b_tot + m_prev, jnp.max(a, axis=0, keepdims=True))
    dec = jnp.exp(b_tot + m_prev - m_new)
    w = jnp.exp(a - m_new)
    wv = w * v
    ct_new = dec * ct + _dot(kt, wv, NN)
    n_new = dec * nrow + jnp.sum(w * k, axis=0, keepdims=True)
    return ct_new, n_new, m_new, dec, w, wv


def _chunk_scores(q, k, b_c, li_r, b_r, m_prev):
    rows, cols = _chunk_iotas()
    dlog = jnp.where(cols <= rows, b_c - b_r + li_r, -jnp.inf)
    inter = b_c[:, 0:1] + m_prev
    mt = jnp.maximum(inter, jnp.max(dlog, axis=-1, keepdims=True))
    dw = jnp.exp(dlog - mt)
    iw = jnp.exp(inter - mt)
    a = _dot(q, k, NT)
    return dw, iw, mt, a, a * dw


def _to_row(column):
    rows, cols = _chunk_iotas()
    return jnp.sum(jnp.where(rows == cols, jnp.broadcast_to(column, (CHUNK, CHUNK)), 0.0), axis=0, keepdims=True)


def _head_out(h, o, g):
    cell, hh, r = _rms_f(h, g)
    sig = jax.nn.sigmoid(o)
    return sig * cell, (cell, hh, r, sig)


HEADS_PER_STEP = 2


def _mlstm_fwd(qc, kc, kct, proj_a, gcols, gbt, g_mlstm, heads, d_model, name, dep):
    t = qc.shape[0]
    dh = g_mlstm.shape[1] // heads
    nc = t // CHUNK
    scale = dh ** -0.5
    hp = HEADS_PER_STEP

    def body(q_ref, k_ref, kt_ref, v_ref, o_ref, li_ref, b_ref, gbt_ref, g_ref, dep_ref, out_ref, ct_out, nm_out,
             ct_ref, n_ref, m_ref):
        del dep_ref
        step = pl.program_id(0)
        ct_ref[...] = jnp.zeros_like(ct_ref)
        n_ref[...] = jnp.zeros_like(n_ref)
        m_ref[...] = jnp.zeros_like(m_ref)
        tile_row = lax.broadcasted_iota(jnp.int32, (8, LANES), 0)

        def chunk(ci, carry):
            sl = pl.ds(pl.multiple_of(ci * CHUNK, CHUNK), CHUNK)
            for s in range(hp):
                cs = slice(s * dh, (s + 1) * dh)
                q, k, v, kt = q_ref[sl, cs], k_ref[sl, cs] * scale, v_ref[sl, cs], kt_ref[cs, sl]
                li_c, b_c, li_r, b_r = _gate_views(li_ref, b_ref, gbt_ref, s, step * hp + s, heads, sl)
                ct, nrow, m_prev = ct_ref[s], n_ref[s, 0:1, :], m_ref[s, 0:1, 0:1]
                ct_out[s, ci] = ct
                nm_out[s, ci] = jnp.where(tile_row == 0, nrow, jnp.where(tile_row == 1, m_prev, 0.0))
                dw, iw, mt, _, sm = _chunk_scores(q, k, b_c, li_r, b_r, m_prev)
                iq = iw * q
                num = _dot(jnp.concatenate([iq, sm], axis=1), jnp.concatenate([ct, v], axis=0), NN)
                den = jnp.sum(iq * nrow + sm, axis=-1, keepdims=True)
                h = num / jnp.maximum(jnp.abs(den), jnp.exp(-mt))
                y, _ = _head_out(h, o_ref[sl, cs], g_ref[:, cs])
                out_ref[sl, cs] = y.astype(out_ref.dtype)
                ct_new, n_new, m_new, _, _, _ = _chunk_state(kt, k, v, li_c, b_c, ct, nrow, m_prev)
                ct_ref[s] = ct_new
                n_ref[s, 0:1, :] = n_new
                m_ref[s, 0:1, 0:1] = m_new
            return carry

        lax.fori_loop(0, nc, chunk, 0, unroll=2)

    col = lambda off: pl.BlockSpec((t, hp * dh), lambda h: (0, off // hp + h))
    slab = lambda off: pl.BlockSpec((hp, t, LANES), lambda h: (off // hp + h, 0, 0))
    return pl.pallas_call(
        body, name=name,
        out_shape=[jax.ShapeDtypeStruct((t, d_model), BF16), jax.ShapeDtypeStruct((heads, nc, dh, dh), F32),
                   jax.ShapeDtypeStruct((heads, nc, 8, LANES), F32)],
        grid=(heads // hp,),
        in_specs=[col(0), col(0), pl.BlockSpec((hp * dh, t), lambda h: (h, 0)), col(2 * heads), col(3 * heads),
                  slab(0), slab(heads), pl.BlockSpec((2 * heads, 1, t), lambda h: (0, 0, 0)),
                  pl.BlockSpec((1, hp * dh), lambda h: (0, h)), ANY_SPEC],
        out_specs=[pl.BlockSpec((t, hp * dh), lambda h: (0, h)),
                   pl.BlockSpec((hp, nc, dh, dh), lambda h: (h, 0, 0, 0)),
                   pl.BlockSpec((hp, nc, 8, LANES), lambda h: (h, 0, 0, 0))],
        scratch_shapes=[pltpu.VMEM((hp, dh, dh), F32), pltpu.VMEM((hp, 8, dh), F32), pltpu.VMEM((hp, 8, LANES), F32)],
        compiler_params=_params("arbitrary"),
    )(qc, kc, kct, proj_a, proj_a, gcols, gcols, gbt, g_mlstm, dep)


def _mlstm_bwd(qc, kc, qct, kct, proj_a, gcols, gbt, g_mlstm, states, dhcat, heads, name, dep):
    t = qc.shape[0]
    wm = g_mlstm.shape[1]
    dh = wm // heads
    nc = t // CHUNK
    scale = dh ** -0.5

    def body(q_ref, k_ref, qt_ref, kt_ref, v_ref, o_ref, li_ref, b_ref, gbt_ref, g_ref, ct_all, nm_all, dy_ref, dep_ref,
             dq_ref, dk_ref, dv_ref, do_ref, dcol_ref, drow_ref, dg_ref, dct_ref, dn_ref):
        del dep_ref
        head = pl.program_id(0)
        g = g_ref[...]

        @pl.when(head == 0)
        def _():
            dcol_ref[...] = jnp.zeros_like(dcol_ref)
            drow_ref[...] = jnp.zeros_like(drow_ref)

        def load(ci):
            sl = pl.ds(pl.multiple_of(ci * CHUNK, CHUNK), CHUNK)
            return (sl, q_ref[sl, :], k_ref[sl, :] * scale, v_ref[sl, :], kt_ref[:, sl]) \
                + _gate_views(li_ref, b_ref, gbt_ref, 0, head, heads, sl)

        dct_ref[...] = jnp.zeros_like(dct_ref)
        dn_ref[...] = jnp.zeros_like(dn_ref)
        dg_ref[...] = jnp.zeros_like(dg_ref)
        rows, cols = _chunk_iotas()
        last = lax.broadcasted_iota(jnp.int32, (CHUNK, 1), 0) == CHUNK - 1
        lane = lax.broadcasted_iota(jnp.int32, (CHUNK, LANES), 1)

        def bwd(step, carry):
            ci = nc - 1 - step
            sl, q, k, v, kt, li_c, b_c, li_r, b_r = load(ci)
            qt = qt_ref[:, sl]
            ct, nrow, m_prev = ct_all[ci], nm_all[ci, 0:1, :], nm_all[ci, 1:2, 0:1]
            dw, iw, mt, a, sm = _chunk_scores(q, k, b_c, li_r, b_r, m_prev)
            qc_ = _dot(q, ct, NN)
            qn = jnp.sum(q * nrow, axis=-1, keepdims=True)
            num = iw * qc_ + _dot(sm, v, NN)
            den = iw * qn + jnp.sum(sm, axis=-1, keepdims=True)
            floor = jnp.exp(-mt)
            dn = jnp.maximum(jnp.abs(den), floor)
            h = num / dn
            _, (cell, hh, r, sig) = _head_out(h, o_ref[sl, :], g)
            dy = dy_ref[sl, :]
            dcell = dy * sig
            do_ref[sl, :] = (dy * cell * sig * (1.0 - sig)).astype(do_ref.dtype)
            dh_, dg = _rms_b(dcell, g, hh, r)
            dg_ref[...] += dg
            dnum = dh_ / dn
            ddn = -jnp.sum(dh_ * h, axis=-1, keepdims=True) / dn
            dden = jnp.where(jnp.abs(den) >= floor, ddn * jnp.sign(den), 0.0)
            idn = iw * dnum
            idd = iw * dden
            dw_t = jnp.exp(jnp.where(cols >= rows, b_r - b_c + li_c, -jnp.inf) - _to_row(mt))
            sm_t = _dot(k, q, NT) * dw_t
            dsm = _dot(dnum, v, NT) + dden
            da_ = dsm * dw
            da_t = (_dot(v, dnum, NT) + _to_row(dden)) * dw_t
            e = da_ * a
            dq = _dot(idn, ct, NT) + _dot(da_, k, NN) + idd * nrow
            dk = _dot(da_t, q, NN)
            dv = _dot(sm_t, dnum, NN)
            dct = _dot(qt, idn, NN)
            dnr = jnp.sum(q * idd, axis=0, keepdims=True)
            db = jnp.sum(e + idn * qc_, axis=-1, keepdims=True) + idd * qn
            _, _, _, dec, w, wv = _chunk_state(kt, k, v, li_c, b_c, ct, nrow, m_prev)
            dct_new, dn_new = dct_ref[...], dn_ref[0:1, :]
            ddec = jnp.sum(jnp.sum(dct_new * ct, axis=-1, keepdims=True), axis=0, keepdims=True) \
                + jnp.sum(dn_new * nrow, axis=-1, keepdims=True)
            dk = dk + _dot(wv, dct_new, NT) + w * dn_new
            dwv = _dot(k, dct_new, NN)
            dv = dv + w * dwv
            dwt = jnp.sum(dwv * v + k * dn_new, axis=-1, keepdims=True)
            da = dwt * w
            dbtot = jnp.sum(da, axis=0, keepdims=True) + ddec * dec
            db = db - da + jnp.where(last, dbtot, 0.0)
            dct_ref[...] = dec * dct_new + dct
            dn_ref[0:1, :] = dec * dn_new + dnr
            dq_ref[sl, :] = dq
            dk_ref[sl, :] = dk * scale
            dv_ref[sl, :] = dv.astype(dv_ref.dtype)
            dcol_ref[sl, :] += jnp.where(lane == head, da, 0.0) + jnp.where(lane == heads + head, db, 0.0)
            drow_ref[head, :, sl] = jnp.sum(e, axis=0, keepdims=True)
            return carry

        lax.fori_loop(0, nc, bwd, 0, unroll=8)

    col = lambda off: pl.BlockSpec((t, dh), lambda h: (0, off + h))
    rowb = pl.BlockSpec((dh, t), lambda h: (h, 0))
    slab = lambda off: pl.BlockSpec((1, t, LANES), lambda h: (off + h, 0, 0))
    vec = pl.BlockSpec((1, dh), lambda h: (0, h))
    return pl.pallas_call(
        body, name=name,
        out_shape=[jax.ShapeDtypeStruct((t, wm), F32)] * 2 + [jax.ShapeDtypeStruct((t, wm), BF16)] * 2
        + [jax.ShapeDtypeStruct((t, LANES), F32), jax.ShapeDtypeStruct((heads, 1, t), F32),
           jax.ShapeDtypeStruct((1, wm), F32)],
        grid=(heads,),
        in_specs=[col(0), col(0), rowb, rowb, col(2 * heads), col(3 * heads), slab(0), slab(heads),
                  pl.BlockSpec((2 * heads, 1, t), lambda h: (0, 0, 0)), vec,
                  pl.BlockSpec((None, nc, dh, dh), lambda h: (h, 0, 0, 0)),
                  pl.BlockSpec((None, nc, 8, LANES), lambda h: (h, 0, 0, 0)), col(0), ANY_SPEC],
        out_specs=[col(0), col(0), col(0), col(0), pl.BlockSpec((t, LANES), lambda h: (0, 0)),
                   pl.BlockSpec((heads, 1, t), lambda h: (0, 0, 0)), vec],
        scratch_shapes=[pltpu.VMEM((dh, dh), F32), pltpu.VMEM((8, dh), F32)],
        compiler_params=_params("arbitrary"),
    )(qc, kc, qct, kct, proj_a, proj_a, gcols, gcols, gbt, g_mlstm, *states, dhcat, dep)


def _pool_window(gi):
    return jnp.where(gi == 0, 2.0, jnp.where(gi == 1, 4.0, jnp.where(gi == 2, 8.0, 16.0))).astype(F32)


def _pool_select(gi, levels):
    return jnp.where(gi == 0, levels[0], jnp.where(gi == 1, levels[1], jnp.where(gi == 2, levels[2], levels[3])))


def _pooled(u, gi, rows):
    s, levels = u, []
    for lvl in range(POOL_GROUPS):
        s = s + _shift_down(s, 1 << lvl, rows)
        levels.append(s)
    cnt = jnp.minimum((rows + 1).astype(F32), _pool_window(gi))
    return _pool_select(gi, levels) / cnt - u, cnt


def _pool_fwd(proj_b, w_pool, scale, hcat, name):
    t, d_model = hcat.shape
    g, cg, _ = w_pool.shape
    off = (d_model - g * cg) // cg

    def body(u_ref, w_ref, s_ref, hcat_ref, out_ref):
        del hcat_ref
        gi = pl.program_id(0)
        u = u_ref[...]
        rows = lax.broadcasted_iota(jnp.int32, u.shape, 0)
        pooled, _ = _pooled(u, gi, rows)
        out_ref[...] = (_dot(pooled, w_ref[...], NN) * s_ref[...]).astype(out_ref.dtype)

    return pl.pallas_call(
        body, name=name,
        out_shape=jax.ShapeDtypeStruct((t, d_model), BF16),
        grid=(g,),
        in_specs=[pl.BlockSpec((t, cg), lambda gi: (0, gi)), pl.BlockSpec((None, cg, cg), lambda gi: (gi, 0, 0)),
                  pl.BlockSpec((1, cg), lambda gi: (0, gi)), HBM_SPEC],
        out_specs=pl.BlockSpec((t, cg), lambda gi: (0, off + gi)),
        input_output_aliases={3: 0},
        compiler_params=_params("arbitrary"),
    )(proj_b, w_pool, scale, hcat)


def _pool_bwd(proj_b, w_pool, scale, dhcat, name):
    t, d_model = dhcat.shape
    g, cg, _ = w_pool.shape
    off = (d_model - g * cg) // cg

    def body(u_ref, w_ref, s_ref, dy_ref, du_ref, dw_ref, ds_ref):
        gi = pl.program_id(0)
        u, w, dy = u_ref[...], w_ref[...], dy_ref[...]
        rows = lax.broadcasted_iota(jnp.int32, u.shape, 0)
        pooled, cnt = _pooled(u, gi, rows)
        mixed = _dot(pooled, w, NN)
        ds_ref[...] = jnp.sum(dy * mixed, axis=0, keepdims=True)
        dmixed = dy * s_ref[...]
        dw_ref[...] = _dot(pooled, dmixed, TN).astype(dw_ref.dtype)
        dpooled = _dot(dmixed, w, NT)
        s, levels = dpooled / cnt, []
        for lvl in range(POOL_GROUPS):
            s = s + _shift_up(s, 1 << lvl, rows)
            levels.append(s)
        du_ref[...] = (_pool_select(gi, levels) - dpooled).astype(du_ref.dtype)

    return pl.pallas_call(
        body, name=name,
        out_shape=[jax.ShapeDtypeStruct((t, g * cg), BF16), jax.ShapeDtypeStruct((g, cg, cg), BF16),
                   jax.ShapeDtypeStruct((1, g * cg), F32)],
        grid=(g,),
        in_specs=[pl.BlockSpec((t, cg), lambda gi: (0, gi)), pl.BlockSpec((None, cg, cg), lambda gi: (gi, 0, 0)),
                  pl.BlockSpec((1, cg), lambda gi: (0, gi)), pl.BlockSpec((t, cg), lambda gi: (0, off + gi))],
        out_specs=[pl.BlockSpec((t, cg), lambda gi: (0, gi)), pl.BlockSpec((None, cg, cg), lambda gi: (gi, 0, 0)),
                   pl.BlockSpec((1, cg), lambda gi: (0, gi))],
        compiler_params=_params("arbitrary"),
    )(proj_b, w_pool, scale, dhcat)


def _pad_cols(a, width):
    return jnp.pad(a, ((0, 0), (0, width - a.shape[1])))


W_IN_COL_TILE = 512


def _w_in_rows(pieces, main, gates, name):
    n, shard, d = pieces.shape
    pool = n * shard - main - gates
    tc = _tile(d, W_IN_COL_TILE)

    def body(x_ref, a_ref, b_ref, flat):
        for j in range(n):
            flat[j * shard:(j + 1) * shard, :] = x_ref[j]
        a_ref[...] = flat[:main, :]
        b_ref[:pool, :] = flat[main + gates:, :]
        b_ref[pool:pool + gates, :] = flat[main:main + gates, :]
        b_ref[pool + gates:, :] = jnp.zeros((LANES - gates, tc), b_ref.dtype)

    return pl.pallas_call(
        body, name=name,
        out_shape=[jax.ShapeDtypeStruct((main, d), pieces.dtype), jax.ShapeDtypeStruct((pool + LANES, d), pieces.dtype)],
        grid=(d // tc,),
        in_specs=[pl.BlockSpec((n, shard, tc), lambda j: (0, 0, j))],
        out_specs=[pl.BlockSpec((main, tc), lambda j: (0, j)), pl.BlockSpec((pool + LANES, tc), lambda j: (0, j))],
        scratch_shapes=[pltpu.VMEM((n * shard, tc), pieces.dtype)],
        compiler_params=_params("parallel"),
    )(pieces)


def _w_in_pieces(rows_main, rows_rest, shard, gates, name):
    main, d = rows_main.shape
    pool = rows_rest.shape[0] - LANES
    n = (main + gates + pool) // shard
    tc = _tile(d, W_IN_COL_TILE)

    def body(a_ref, b_ref, o_ref, flat):
        flat[:main, :] = a_ref[...]
        flat[main:main + gates, :] = b_ref[pool:pool + gates, :]
        flat[main + gates:, :] = b_ref[:pool, :]
        for j in range(n):
            o_ref[j] = flat[j * shard:(j + 1) * shard, :]

    return pl.pallas_call(
        body, name=name,
        out_shape=jax.ShapeDtypeStruct((n, shard, d), rows_main.dtype),
        grid=(d // tc,),
        in_specs=[pl.BlockSpec((main, tc), lambda j: (0, j)), pl.BlockSpec((pool + LANES, tc), lambda j: (0, j))],
        out_specs=pl.BlockSpec((n, shard, tc), lambda j: (0, 0, j)),
        scratch_shapes=[pltpu.VMEM((n * shard, tc), rows_main.dtype)],
        compiler_params=_params("parallel"),
    )(rows_main, rows_rest)


def kernel(x, p, w_in, b_gates, w_qk_conv, g_mlstm, w_pool, pool_scale, w_out, g_mix_pre, g_mix_post, w_ff1, w_ff2, g_ff_pre, g_ff_post, w_ple_proj, w_ple_gate, g_ple_gate, g_ple_post, loss_target, m_w_in, m_b_gates, m_w_qk_conv, m_g_mlstm, m_w_pool, m_pool_scale, m_w_out, m_g_mix_pre, m_g_mix_post, m_w_ff1, m_w_ff2, m_g_ff_pre, m_g_ff_post, m_w_ple_proj, m_w_ple_gate, m_g_ple_gate, m_g_ple_post, v_w_in, v_b_gates, v_w_qk_conv, v_g_mlstm, v_w_pool, v_pool_scale, v_w_out, v_g_mix_pre, v_g_mix_post, v_w_ff1, v_w_ff2, v_g_ff_pre, v_g_ff_post, v_w_ple_proj, v_w_ple_gate, v_g_ple_gate, v_g_ple_post):
    weights = dict(w_in=w_in, b_gates=b_gates, w_qk_conv=w_qk_conv, g_mlstm=g_mlstm, w_pool=w_pool,
                   pool_scale=pool_scale, w_out=w_out, g_mix_pre=g_mix_pre, g_mix_post=g_mix_post, w_ff1=w_ff1,
                   w_ff2=w_ff2, g_ff_pre=g_ff_pre, g_ff_post=g_ff_post, w_ple_proj=w_ple_proj,
                   w_ple_gate=w_ple_gate, g_ple_gate=g_ple_gate, g_ple_post=g_ple_post)
    mom1 = dict(w_in=m_w_in, b_gates=m_b_gates, w_qk_conv=m_w_qk_conv, g_mlstm=m_g_mlstm, w_pool=m_w_pool,
                pool_scale=m_pool_scale, w_out=m_w_out, g_mix_pre=m_g_mix_pre, g_mix_post=m_g_mix_post,
                w_ff1=m_w_ff1, w_ff2=m_w_ff2, g_ff_pre=m_g_ff_pre, g_ff_post=m_g_ff_post,
                w_ple_proj=m_w_ple_proj, w_ple_gate=m_w_ple_gate, g_ple_gate=m_g_ple_gate, g_ple_post=m_g_ple_post)
    mom2 = dict(w_in=v_w_in, b_gates=v_b_gates, w_qk_conv=v_w_qk_conv, g_mlstm=v_g_mlstm, w_pool=v_w_pool,
                pool_scale=v_pool_scale, w_out=v_w_out, g_mix_pre=v_g_mix_pre, g_mix_post=v_g_mix_post,
                w_ff1=v_w_ff1, w_ff2=v_w_ff2, g_ff_pre=v_g_ff_pre, g_ff_post=v_g_ff_post,
                w_ple_proj=v_w_ple_proj, w_ple_gate=v_w_ple_gate, g_ple_gate=v_g_ple_gate, g_ple_post=v_g_ple_post)
    order = list(weights)

    t, d = x.shape[1], x.shape[2]
    heads = MLSTM_HEADS
    wm = g_mlstm.shape[1]
    pw = pool_scale.shape[1]
    cg = pw // POOL_GROUPS
    in_shard = w_in.shape[2]
    xs, ps, tgt = x[0], p[0, 0], loss_target[0]

    mx, my, mc = lax.axis_index("x"), lax.axis_index("y"), lax.axis_index("c")
    core = jnp.reshape(mc, (1,)).astype(jnp.int32)
    chip = jnp.reshape(2 * mx + my, (1,)).astype(jnp.int32)

    first = [w_in[0].T.astype(BF16), w_qk_conv[0], w_pool[0].reshape(POOL_GROUPS * w_pool.shape[2], cg).astype(BF16)]
    started, token = _gather_start([[w_ple_proj[0].astype(BF16)], first], "gather_start_in")
    anchor = token[0, 0]
    later = lambda w: (w[0] + anchor).astype(BF16)
    ff2_shard = later(w_ff2)
    rest, token = _gather_start([[later(w_out)], [later(w_ff1)], [ff2_shard[:, :d // 2]], [ff2_shard[:, d // 2:]],
                                 [later(w_ple_gate)]], "gather_start_rest")
    started = dict(zip(["ple_proj", "in", "out", "ff1", "ff2_lo", "ff2_hi", "ple_gate"], started + rest))
    bias = _pad_cols(b_gates, LANES)
    small = ["b_gates", "g_mlstm", "pool_scale", "g_mix_pre", "g_mix_post", "g_ff_pre", "g_ff_post", "g_ple_gate",
             "g_ple_post"]
    pack = lambda src: _pad_rows16(jnp.concatenate([_pad_cols(src[k] + anchor, d) for k in small], axis=0))
    small_state = [pack(weights), pack(mom1), pack(mom2)]
    in_state = [a[0].T + anchor for a in (w_in, m_w_in, v_w_in)]

    xn = _norm_fwd(xs, g_mix_pre, "norm_mix_pre", deps=[token])
    (g_pproj,) = _forward_pair(_gather_wait(started["ple_proj"], xn, "gather_wait_ple_proj"), "gather_pair_ple_proj")
    e = _mm_nn_pieces("ple_proj", ps, g_pproj)[0]
    g_in, g_conv, g_pool = _forward_pair(
        _gather_wait(started["in"], [e] + small_state + in_state, "gather_wait_in"), "gather_pair_in")
    in_t_a, in_t_b = _w_in_rows(g_in, 4 * wm, 2 * heads, "w_in_rows")
    conv_w = g_conv.transpose(1, 0, 2).reshape(QK_CONV, 2 * wm)
    pool_w = g_pool.reshape(N_DEV, POOL_GROUPS, cg // N_DEV, cg).transpose(1, 0, 2, 3).reshape(POOL_GROUPS, cg, cg)
    proj_a = _mm_nt("proj_qkvo", xn, in_t_a, n_rows=4 * wm)
    proj_b = _mm_nt("proj_pool_gates", xn, in_t_b)
    qc, qct = _conv_fwd(proj_a, conv_w[:, :wm], 0, "q_conv")
    kc, kct = _conv_fwd(proj_a, conv_w[:, wm:], wm, "k_conv", scale_t=(wm // heads) ** -0.5)
    gb, gbt, gcols = _gates_fwd(proj_b, bias, pw // LANES, heads, "gates")
    gbt = gbt[:2 * heads].reshape(2 * heads, 1, t)
    fwd_out = _forward_start(_gather_wait(started["out"], gb, "gather_wait_out"), "gather_pair_start_out")
    hcat, *mlstm_states = _mlstm_fwd(qc, kc, kct, proj_a, gcols, gbt, g_mlstm, heads, d, "mlstm", fwd_out["token"])
    hcat = _pool_fwd(proj_b, pool_w, pool_scale, hcat, "pool")
    (g_out,) = _forward_wait(fwd_out, hcat, "gather_pair_wait_out")
    out_w = g_out.reshape(d, d)
    mix = _mm_nn("mix_out", hcat, out_w)[0]
    fwd_ff1 = _forward_start(_gather_wait(started["ff1"], mix, "gather_wait_ff1"), "gather_pair_start_ff1")
    h1, hn = _block_fwd(xs, mix, g_mix_post, g_ff_pre, "residual_mix", deps=[fwd_ff1["token"]])
    (g_ff1,) = _forward_wait(fwd_ff1, hn, "gather_pair_wait_ff1")
    relu_a, act = _mm_nn_pieces("ff1", hn, g_ff1, out_dtype=[BF16, BF16],
                                epilogue=lambda acc: (jnp.maximum(acc, 0.0), jnp.square(jnp.maximum(acc, 0.0))))
    (g_ff2_lo,) = _forward_pair(_gather_wait(started["ff2_lo"], act, "gather_wait_ff2_lo"), "gather_pair_ff2_lo")
    ff2_lo = g_ff2_lo.reshape(-1, d // 2)
    ff = _mm_nn("ff2_lo", act, ff2_lo, tn=256, tk=4 * d, out_cols=d)[0]
    (g_ff2_hi,) = _forward_pair(_gather_wait(started["ff2_hi"], ff, "gather_wait_ff2_hi"), "gather_pair_ff2_hi")
    ff2_hi = g_ff2_hi.reshape(-1, d // 2)
    ff = _mm_nn("ff2_hi", act, ff2_hi, tn=256, tk=4 * d, out_cols=d, out_at=d // 2, into=ff)[0]
    fwd_ple = _forward_start(_gather_wait(started["ple_gate"], ff, "gather_wait_ple"), "gather_pair_start_ple")
    h2, hg = _block_fwd(h1, ff, g_ff_post, g_ple_gate, "residual_ff", deps=[fwd_ple["token"]])
    (g_pgate,) = _forward_wait(fwd_ple, hg, "gather_pair_wait_ple")
    pgate_w = g_pgate.reshape(d, d)
    z = _mm_nn("ple_gate", hg, pgate_w)[0]
    dh3, dz, de, dg_ple_post, loss_part = _loss_head(h2, z, e, tgt, g_ple_post, "loss_head")
    loss = lax.psum(loss_part[0, 0], MESH_AXES)

    out = {}

    def state(k):
        if k == "w_in":
            return in_state
        return [a.reshape(-1, a.shape[-1]) for a in (weights[k], mom1[k], mom2[k])]

    def pair_begin(pieces, tag):
        keys = list(pieces)
        return keys, _pair_start([pieces[k] for k in keys], "reduce_pair_start_" + tag)

    def chip_begin(group, after, tag):
        keys, begun = group
        mine, got = _pair_wait(begun, after, "reduce_pair_wait_" + tag)
        sums = [_pair_sum(p_, g, core, "pair_sum_" + k) for k, p_, g in zip(keys, mine, got)]
        return keys, _chip_start(sums, "reduce_start_" + tag)

    def reduce_finish(group, after, tag):
        keys, begun = group
        sums, got = _chip_wait(begun, after, "reduce_wait_" + tag)
        for k, s, g in zip(keys, sums, got):
            res = _adamw_shard(*state(k), s, g, chip, "adamw_" + k)
            out[k] = [r.T[None] if k == "w_in" else r.reshape(weights[k].shape) for r in res]

    token_of = lambda group: group[1]["token"]
    dhg = _mm_nt("d_ple_gate_in", dz, pgate_w)
    gw_pgate = _mm_tn("gw_ple_gate", hg, dz)
    gw_pproj = _mm_tn("gw_ple_proj", ps, de, pieces=N_DEV)
    dh2, dff, dg_ff_post, dg_ple_gate = _block_bwd(h2, ff, dh3, dhg, g_ff_post, g_ple_gate, "d_residual_ff")
    da = _mm_nt("d_ff2_in", dff, ff2_lo, b2=ff2_hi, out_dtype=BF16, extra=[relu_a],
                epilogue=lambda acc, r: (acc * (2.0 * r.astype(F32)),))
    gw_ff2 = _mm_tn("gw_ff2", act, dff)
    ff2_keys = ["w_ff2", "w_ple_gate", "w_ple_proj"]
    pair_ff2 = pair_begin({"w_ff2": gw_ff2.reshape(N_DEV, -1, d),
                           "w_ple_gate": gw_pgate.reshape(N_DEV, d // N_DEV, d), "w_ple_proj": gw_pproj}, "ff2")
    dhn = _mm_nt_pieces("d_ff1_in", da, g_ff1, deps=[token_of(pair_ff2)])
    place = jnp.concatenate([core, chip])
    mine_ff2, got_ff2 = _pair_wait(pair_ff2[1], dhn, "reduce_pair_wait_ff2")
    gw_ff1, sums_ff2 = _mm_tn("gw_ff1", hn, da, pieces=N_DEV, side=_side_pair_sum(mine_ff2[0], got_ff2[0], place))
    sums_ple = [_pair_sum(p_, g, core, "pair_sum_" + k) for k, p_, g in zip(ff2_keys[1:], mine_ff2[1:], got_ff2[1:])]
    red_ff2 = (ff2_keys, _chip_start([sums_ff2] + sums_ple, "reduce_start_ff2"))
    pair_ff1 = pair_begin({"w_ff1": gw_ff1}, "ff1")
    dh1, dmix, dg_mix_post, dg_ff_pre = _block_bwd(h1, mix, dh2, dhn, g_mix_post, g_ff_pre, "d_residual_mix",
                                                   deps=[token_of(pair_ff1), token_of(red_ff2)])
    mine_ff1, got_ff1 = _pair_wait(pair_ff1[1], dmix, "reduce_pair_wait_ff1")
    dhcat, sums_ff1 = _mm_nt("d_mix_in", dmix, out_w, tm=t // 2, tn=d // 2,
                             side=_side_pair_sum(mine_ff1[0], got_ff1[0], place))
    red_ff1 = (["w_ff1"], _chip_start([sums_ff1], "reduce_start_ff1"))
    gw_out = _mm_tn("gw_out", hcat, dmix, deps=[token_of(red_ff1)])
    du, gw_pool, dg_pool_scale = _pool_bwd(proj_b, pool_w, pool_scale, dhcat, "d_pool")
    pair_out = pair_begin({
        "w_out": gw_out.reshape(N_DEV, d // N_DEV, d),
        "w_pool": gw_pool.reshape(POOL_GROUPS, N_DEV, cg // N_DEV, cg).transpose(1, 0, 2, 3).reshape(N_DEV, -1, cg)},
        "out")
    dq, dk, dv, do, dcol, drow, dg_mlstm = _mlstm_bwd(qc, kc, qct, kct, proj_a, gcols, gbt, g_mlstm, mlstm_states, dhcat,
                                                      heads, "d_mlstm", token_of(pair_out))
    red_out = chip_begin(pair_out, dq, "out")
    drow = jnp.pad(drow.reshape(heads, t), ((0, LANES - heads), (0, 0)))
    dgates, dbias = _gates_bwd(proj_b, bias, dcol, drow, pw // LANES, heads, "d_gates")
    dq_pre, gw_conv_q = _conv_bwd(proj_a, conv_w[:, :wm], dq, 0, "d_q_conv")
    dk_pre, gw_conv_k = _conv_bwd(proj_a, conv_w[:, wm:], dk, wm, "d_k_conv")
    gw_conv = jnp.concatenate([gw_conv_q, gw_conv_k], axis=1)
    dproj_a = jnp.concatenate([dq_pre, dk_pre, dv, do], axis=1)
    dproj_b = jnp.concatenate([du, dgates], axis=1)
    sums_ff2, got_ff2 = _chip_wait(red_ff2[1], token_of(red_out), "reduce_wait_ff2")
    gw_a, *update = _mm_tn("gw_in_qkvo", dproj_a, xn,
                           side=_side_adamw(*state("w_ff2"), sums_ff2[0], got_ff2[0], place))
    out["w_ff2"] = [r.reshape(weights["w_ff2"].shape) for r in update]
    for k, s, g in zip(ff2_keys[1:], sums_ff2[1:], got_ff2[1:]):
        out[k] = [r.reshape(weights[k].shape) for r in _adamw_shard(*state(k), s, g, chip, "adamw_" + k)]
    gw_b = _mm_tn("gw_in_pool_gates", dproj_b, xn)
    pair_in = pair_begin({
        "w_in": _w_in_pieces(gw_a, gw_b, in_shard, 2 * heads, "w_in_pieces"),
        "w_qk_conv": gw_conv.reshape(QK_CONV, N_DEV, -1).transpose(1, 0, 2).astype(BF16)}, "in")
    dxn_b = _mm_nn("d_proj_in", dproj_b, in_t_b, deps=[token_of(pair_in)])[0]
    red_in = chip_begin(pair_in, dxn_b, "in")
    sums_ff1, got_ff1 = _chip_wait(red_ff1[1], token_of(red_in), "reduce_wait_ff1")
    dxn, *update = _mm_nn("d_proj_qkvo_in", dproj_a, in_t_a, tn=512, tk=4 * wm, deps=[token_of(red_in)],
                          extra=[dxn_b], epilogue=lambda acc, prev: (acc + prev,),
                          side=_side_adamw(*state("w_ff1"), sums_ff1[0], got_ff1[0], place))
    out["w_ff1"] = [r.reshape(weights["w_ff1"].shape) for r in update]
    grad_x, dg_mix_pre = _norm_bwd(xs, dxn, dh1, g_mix_pre, "d_norm_mix_pre")

    small_grads = dict(b_gates=dbias[:, :2 * heads], g_mlstm=dg_mlstm, pool_scale=dg_pool_scale, g_mix_pre=dg_mix_pre,
                       g_mix_post=dg_mix_post, g_ff_pre=dg_ff_pre, g_ff_post=dg_ff_post, g_ple_gate=dg_ple_gate,
                       g_ple_post=dg_ple_post)
    small_begun = _everyone_start([pack(small_grads)], "small_grads_start")
    reduce_finish(red_out, small_begun["token"], "out")
    reduce_finish(red_in, out["w_out"][0], "in")
    (small_parts,) = _everyone_wait(small_begun, out["w_in"][0], "small_grads_wait")
    res = _adamw_replicated(*small_state, small_parts, "adamw_replicated")
    for i, k in enumerate(small):
        width = weights[k].shape[1]
        out[k] = [r[i:i + 1, :width] for r in res]

    return (loss, grad_x[None], *[out[k][0] for k in order], *[out[k][1] for k in order],
            *[out[k][2] for k in order], *[out[k][3] for k in order])


def _pad_rows16(a):
    return jnp.pad(a, ((0, 16 - a.shape[0]), (0, 0)))
```

```python
import jax
import jax.numpy as jnp
from jax import lax
from jax.experimental import pallas as pl
from jax.experimental.pallas import tpu as pltpu

F32 = jnp.float32
BF16 = jnp.bfloat16
EPS = 1e-6
N_DEV = 8
N_CHIP = 4
LANES = 128
VMEM_LIMIT = 56 * 1024 * 1024
MLSTM_HEADS = 8
POOL_GROUPS = 4
QK_CONV = 4
CHUNK = 128
ROW_TILE = 256
ADAM_LR, ADAM_B1, ADAM_B2, ADAM_EPS, ADAM_WD, ADAM_STEP = 0.001, 0.9, 0.999, 1e-08, 0.01, 10
MESH_AXES = ("x", "y", "c")
MESH = pl.DeviceIdType.MESH
HBM_SPEC = pl.BlockSpec(memory_space=pltpu.HBM)
SEM_SPEC = pl.BlockSpec(memory_space=pltpu.SEMAPHORE)
DATAFLOW = pltpu.SideEffectType.DATAFLOW_SIDE_EFFECTING


def _params(*sem):
    if sem:
        return pltpu.CompilerParams(dimension_semantics=sem, vmem_limit_bytes=VMEM_LIMIT)
    return pltpu.CompilerParams(vmem_limit_bytes=VMEM_LIMIT)


def _tile(n, pref):
    if n <= pref:
        return n
    t = (pref // LANES) * LANES
    while t >= LANES:
        if n % t == 0:
            return t
        t -= LANES
    return n


def _hbm(a):
    return pltpu.with_memory_space_constraint(a, pltpu.HBM)


def _mesh_place():
    x, y, c = lax.axis_index("x"), lax.axis_index("y"), lax.axis_index("c")
    return x, y, c, [(1 - x, y), (x, 1 - y), (1 - x, 1 - y)]


def _chip_copies(srcs, lands, send, recv):
    x, y, c, chips = _mesh_place()
    return [pltpu.make_async_remote_copy(
        src_ref=srcs[w].at[2 * chip[0] + chip[1]], dst_ref=lands[w].at[k],
        send_sem=send.at[3 * w + k], recv_sem=recv.at[3 * w + k],
        device_id=(*chip, c), device_id_type=MESH)
        for k, chip in enumerate(chips) for w in range(len(srcs))]


def _chip_start(sums, name):
    return _split_start(_chip_copies, sums, [lax.empty((3,) + a.shape[1:], a.dtype) for a in sums], 3, name,
                        peers=_same_core_of_other_chips)


def _chip_wait(started, after, name):
    return _split_wait(_chip_copies, started, after, name)


def _pair_start(pieces, name):
    return _split_start(_pair_copies, pieces, [lax.empty((N_CHIP,) + a.shape[1:], a.dtype) for a in pieces], N_CHIP, name,
                        peers=_sibling)


def _pair_wait(started, after, name):
    return _split_wait(_pair_copies, started, after, name)


def _pair_copies(srcs, lands, send, recv):
    x, y, c, _ = _mesh_place()
    return [pltpu.make_async_remote_copy(
        src_ref=srcs[w].at[2 * ch + 1 - c], dst_ref=lands[w].at[ch],
        send_sem=send.at[N_CHIP * w + ch], recv_sem=recv.at[N_CHIP * w + ch],
        device_id=(x, y, 1 - c), device_id_type=MESH)
        for w in range(len(srcs)) for ch in range(N_CHIP)]


OWN_BARRIER_IDS = {name: i for i, name in enumerate([
    "reduce_pair_start_ff2", "reduce_pair_start_ff1", "reduce_pair_start_out", "reduce_pair_start_in",
    "gather_pair_start_out", "gather_pair_start_ff1", "gather_pair_start_ple",
    "reduce_start_ff2", "reduce_start_ff1", "reduce_start_out", "reduce_start_in",
    "gather_start_in", "gather_start_rest", "gather_pair_in", "gather_pair_ff2_lo", "gather_pair_ff2_hi",
    "gather_pair_ple_proj"])}


def _shake_hands(peers):
    barrier = pltpu.get_barrier_semaphore()
    for peer in peers:
        pl.semaphore_signal(barrier, inc=1, device_id=peer, device_id_type=MESH)
    pl.semaphore_wait(barrier, len(peers))


def _sibling():
    x, y, c, _ = _mesh_place()
    return [(x, y, 1 - c)]


def _same_core_of_other_chips():
    x, y, c, chips = _mesh_place()
    return [(*chip, c) for chip in chips]


def _split_start(copies_of, srcs, lands, per_array, name, peers=None):
    ns, nb = len(srcs), len(srcs) + len(lands)
    n_copies = per_array * len(lands)

    def body(*refs):
        if peers is not None:
            _shake_hands(peers())
        for cp in copies_of(refs[:ns], refs[ns:nb], refs[nb], refs[nb + 1]):
            cp.start()
        refs[-1][...] = jnp.zeros_like(refs[-1])

    params = dict(has_side_effects=DATAFLOW)
    if peers is not None:
        params["collective_id"] = OWN_BARRIER_IDS[name]
    res = pl.pallas_call(
        body, name=name,
        out_shape=[pltpu.SemaphoreType.DMA((n_copies,)), pltpu.SemaphoreType.DMA((n_copies,))]
        + [pltpu.HBM(a.shape, a.dtype) for a in list(srcs) + list(lands)] + [jax.ShapeDtypeStruct((8, LANES), F32)],
        in_specs=[HBM_SPEC] * nb,
        out_specs=[SEM_SPEC, SEM_SPEC] + [HBM_SPEC] * nb + [pl.BlockSpec(memory_space=pltpu.VMEM)],
        input_output_aliases={i: 2 + i for i in range(nb)},
        compiler_params=pltpu.CompilerParams(**params),
    )(*[_hbm(a) for a in srcs], *[_hbm(a) for a in lands])
    return dict(send=res[0], recv=res[1], srcs=list(res[2:2 + ns]), lands=list(res[2 + ns:2 + nb]), token=res[-1])


def _split_wait(copies_of, started, after, name):
    ns = len(started["srcs"])
    nb = ns + len(started["lands"])

    def body(*refs):
        for cp in copies_of(refs[:ns], refs[ns:nb], refs[nb], refs[nb + 1]):
            cp.wait_send()
            cp.wait_recv()

    res = pl.pallas_call(
        body, name=name,
        out_shape=[pltpu.HBM(a.shape, a.dtype) for a in started["srcs"] + started["lands"]],
        in_specs=[HBM_SPEC] * nb + [SEM_SPEC, SEM_SPEC, ANY_SPEC],
        out_specs=[HBM_SPEC] * nb,
        input_output_aliases={i: i for i in range(nb)},
        compiler_params=pltpu.CompilerParams(has_side_effects=DATAFLOW),
    )(*started["srcs"], *started["lands"], started["send"], started["recv"], after)
    return list(res[:ns]), list(res[ns:])


def _forward_copies(srcs, lands, send, recv):
    del srcs
    x, y, c, chips = _mesh_place()
    copies = []
    for w in range(len(lands)):
        for j, chip in enumerate(chips):
            rows = lands[w].at[4 * chip[0] + 2 * chip[1] + c]
            copies.append(pltpu.make_async_remote_copy(
                src_ref=rows, dst_ref=rows, send_sem=send.at[3 * w + j], recv_sem=recv.at[3 * w + j],
                device_id=(x, y, 1 - c), device_id_type=MESH))
    return copies


def _forward_start(lands, name):
    return _split_start(_forward_copies, [], lands, 3, name, peers=_sibling)


def _forward_wait(started, after, name):
    return _split_wait(_forward_copies, started, after, name)[1]


def _everyone_copies(srcs, lands, send, recv):
    x, y, c, _ = _mesh_place()
    me = 4 * x + 2 * y + c
    copies = []
    for w in range(len(srcs)):
        for k in range(N_DEV - 1):
            flip = k + 1
            peer = (1 - x if flip & 4 else x, 1 - y if flip & 2 else y, 1 - c if flip & 1 else c)
            copies.append(pltpu.make_async_remote_copy(
                src_ref=srcs[w], dst_ref=lands[w].at[me], send_sem=send.at[7 * w + k], recv_sem=recv.at[7 * w + k],
                device_id=peer, device_id_type=MESH))
    return copies


def _everyone_start(blocks, name):
    me = 4 * lax.axis_index("x") + 2 * lax.axis_index("y") + lax.axis_index("c")
    lands = [lax.dynamic_update_slice(lax.empty((N_DEV,) + a.shape, a.dtype), a[None], (me,) + (0,) * a.ndim)
             for a in blocks]
    return _split_start(_everyone_copies, blocks, lands, N_DEV - 1, name)


def _everyone_wait(started, after, name):
    return _split_wait(_everyone_copies, started, after, name)[1]


def _gather_copies(srcs, lands, send, recv):
    x, y, c, chips = _mesh_place()
    me = 4 * x + 2 * y + c
    targets = [(x, y, 1 - c)] + [(*chip, c) for chip in chips]
    return [pltpu.make_async_remote_copy(
        src_ref=srcs[w], dst_ref=lands[w].at[me], send_sem=send.at[4 * w + k], recv_sem=recv.at[4 * w + k],
        device_id=to, device_id_type=MESH)
        for w in range(len(srcs)) for k, to in enumerate(targets)]


def _gather_start(groups, name):
    sizes = [len(g) for g in groups]
    flat = [a for g in groups for a in g]
    n, ng = len(flat), len(groups)
    me = 4 * lax.axis_index("x") + 2 * lax.axis_index("y") + lax.axis_index("c")
    lands = [lax.dynamic_update_slice(lax.empty((N_DEV,) + a.shape, a.dtype), a[None], (me,) + (0,) * a.ndim)
             for a in flat]

    def body(*refs):
        _shake_hands(_sibling() + _same_core_of_other_chips())
        srcs, zones = refs[:n], refs[n:2 * n]
        sems = refs[2 * n:2 * n + 2 * ng]
        token = refs[-1]
        lo = 0
        for gi, size in enumerate(sizes):
            for cp in _gather_copies(srcs[lo:lo + size], zones[lo:lo + size], sems[2 * gi], sems[2 * gi + 1]):
                cp.start()
            lo += size
        token[...] = jnp.zeros_like(token)

    sem_shapes = []
    for size in sizes:
        sem_shapes += [pltpu.SemaphoreType.DMA((4 * size,))] * 2
    res = pl.pallas_call(
        body, name=name,
        out_shape=sem_shapes + [pltpu.HBM(a.shape, a.dtype) for a in flat + lands] + [jax.ShapeDtypeStruct((8, LANES), F32)],
        in_specs=[HBM_SPEC] * (2 * n),
        out_specs=[SEM_SPEC] * (2 * ng) + [HBM_SPEC] * (2 * n) + [pl.BlockSpec(memory_space=pltpu.VMEM)],
        input_output_aliases={i: 2 * ng + i for i in range(2 * n)},
        compiler_params=pltpu.CompilerParams(has_side_effects=DATAFLOW, collective_id=OWN_BARRIER_IDS[name]),
    )(*[_hbm(a) for a in flat], *[_hbm(a) for a in lands])
    out, lo = [], 0
    for gi, size in enumerate(sizes):
        out.append(dict(send=res[2 * gi], recv=res[2 * gi + 1],
                        srcs=list(res[2 * ng + lo:2 * ng + lo + size]),
                        lands=list(res[2 * ng + n + lo:2 * ng + n + lo + size])))
        lo += size
    return out, res[-1]


def _gather_wait(started, after, name):
    n = len(started["srcs"])
    after = list(after) if isinstance(after, (list, tuple)) else [after]

    def body(*refs):
        srcs, zones = refs[:n], refs[n:2 * n]
        send, recv = refs[2 * n], refs[2 * n + 1]
        for cp in _gather_copies(srcs, zones, send, recv):
            cp.wait_send()
            cp.wait_recv()

    res = pl.pallas_call(
        body, name=name,
        out_shape=[pltpu.HBM(a.shape, a.dtype) for a in started["srcs"] + started["lands"]],
        in_specs=[HBM_SPEC] * (2 * n) + [SEM_SPEC, SEM_SPEC] + [ANY_SPEC] * len(after),
        out_specs=[HBM_SPEC] * (2 * n),
        input_output_aliases={i: i for i in range(2 * n)},
        compiler_params=pltpu.CompilerParams(has_side_effects=DATAFLOW),
    )(*started["srcs"], *started["lands"], started["send"], started["recv"], *after)
    return list(res[n:])


def _forward_pair(lands, name):
    n = len(lands)

    def body(*refs):
        _shake_hands(_sibling())
        zones = refs[:n]
        send, recv = refs[2 * n:]
        x, y, c, chips = _mesh_place()

        def copy(w, j, core):
            rows = zones[w].at[4 * chips[j][0] + 2 * chips[j][1] + core]
            return pltpu.make_async_remote_copy(
                src_ref=rows, dst_ref=rows, send_sem=send.at[3 * w + j], recv_sem=recv.at[3 * w + j],
                device_id=(x, y, 1 - c), device_id_type=MESH)

        sends = [copy(w, j, c) for w in range(n) for j in range(3)]
        for cp in sends:
            cp.start()
        for w in range(n):
            for j in range(3):
                copy(w, j, 1 - c).wait_recv()
        for cp in sends:
            cp.wait_send()

    return list(pl.pallas_call(
        body, name=name,
        out_shape=[jax.ShapeDtypeStruct(a.shape, a.dtype) for a in lands],
        in_specs=[HBM_SPEC] * n, out_specs=[HBM_SPEC] * n,
        input_output_aliases={i: i for i in range(n)},
        scratch_shapes=[pltpu.SemaphoreType.DMA((3 * n,)), pltpu.SemaphoreType.DMA((3 * n,))],
        compiler_params=pltpu.CompilerParams(collective_id=OWN_BARRIER_IDS[name]),
    )(*lands))


def _pair_sum(pieces, got, core, name):
    _, rows, cols = pieces.shape
    tr, tc = _tile_2d(rows, cols)

    def body(core_ref, mine_ref, got_ref, out_ref):
        del core_ref
        out_ref[...] = (mine_ref[...].astype(F32) + got_ref[...].astype(F32)).astype(out_ref.dtype)

    return pl.pallas_call(
        body, name=name,
        out_shape=jax.ShapeDtypeStruct((N_CHIP, rows, cols), pieces.dtype),
        grid_spec=pltpu.PrefetchScalarGridSpec(
            num_scalar_prefetch=1, grid=(N_CHIP, rows // tr, cols // tc),
            in_specs=[pl.BlockSpec((None, tr, tc), lambda ch, i, j, core: (2 * ch + core[0], i, j)),
                      pl.BlockSpec((None, tr, tc), lambda ch, i, j, core: (ch, i, j))],
            out_specs=pl.BlockSpec((None, tr, tc), lambda ch, i, j, core: (ch, i, j))),
        compiler_params=_params("parallel", "parallel", "parallel"),
    )(core, pieces, got)


def _tile_2d(rows, cols, budget=2 * 1024 * 1024):
    want = max(16, budget // (4 * cols))
    if rows <= want:
        return rows, cols
    t = (want // 16) * 16
    while t >= 16:
        if rows % t == 0:
            return t, cols
        t -= 16
    return rows, _tile(cols, max(LANES, budget // (4 * rows)))


def _adamw_math(w, g, m, v):
    m = ADAM_B1 * m + (1.0 - ADAM_B1) * g
    v = ADAM_B2 * v + (1.0 - ADAM_B2) * (g * g)
    m_hat = m / (1.0 - ADAM_B1 ** ADAM_STEP)
    v_hat = v / (1.0 - ADAM_B2 ** ADAM_STEP)
    delta = -ADAM_LR * (m_hat / (jnp.sqrt(v_hat) + ADAM_EPS) + ADAM_WD * w)
    return delta, m, v


def _adamw_shard(w, m, v, sums, got, chip, name):
    rows, cols = w.shape
    tr, tc = _tile_2d(rows, cols, budget=1024 * 1024)

    def body(chip_ref, w_ref, m_ref, v_ref, own_ref, got_ref, g_out, d_out, m_out, v_out):
        del chip_ref
        g = own_ref[...].astype(F32)
        for k in range(3):
            g = g + got_ref[k].astype(F32)
        delta, m_new, v_new = _adamw_math(w_ref[...], g, m_ref[...], v_ref[...])
        g_out[...] = g
        d_out[...] = delta
        m_out[...] = m_new
        v_out[...] = v_new

    blk = pl.BlockSpec((tr, tc), lambda i, j, chip: (i, j))
    return pl.pallas_call(
        body, name=name,
        out_shape=[jax.ShapeDtypeStruct((rows, cols), F32)] * 4,
        grid_spec=pltpu.PrefetchScalarGridSpec(
            num_scalar_prefetch=1, grid=(rows // tr, cols // tc),
            in_specs=[blk, blk, blk,
                      pl.BlockSpec((None, tr, tc), lambda i, j, chip: (chip[0], i, j)),
                      pl.BlockSpec((3, tr, tc), lambda i, j, chip: (0, i, j))],
            out_specs=[blk] * 4),
        compiler_params=_params("parallel", "parallel"),
    )(chip, w, m, v, sums, got)


def _adamw_replicated(w, m, v, parts, name):
    rows, cols = w.shape

    def body(w_ref, m_ref, v_ref, parts_ref, g_out, d_out, m_out, v_out):
        g = parts_ref[0]
        for d in range(1, N_DEV):
            g = g + parts_ref[d]
        delta, m_new, v_new = _adamw_math(w_ref[...], g, m_ref[...], v_ref[...])
        g_out[...] = g
        d_out[...] = delta
        m_out[...] = m_new
        v_out[...] = v_new

    return pl.pallas_call(
        body, name=name,
        out_shape=[jax.ShapeDtypeStruct((rows, cols), F32)] * 4,
        compiler_params=_params(),
    )(w, m, v, parts)


NN = ((1,), (0,))
NT = ((1,), (1,))
TN = ((0,), (0,))


ANY_SPEC = pl.BlockSpec(memory_space=pl.ANY)


def _mm(name, a, b, *, dims, grid, a_spec, b_spec, outs, extra=(), extra_specs=(), epilogue=None, acc_shape=None,
        deps=(), b_pieces=0, side=None, b2=None, into=None):
    nk = grid[2]
    n_extra, n_out = len(extra), len(outs)
    side_ins = side["ins"] if side else []
    side_outs = side["outs"] if side else []
    first_side = 2 + n_extra + len(deps)
    tail = ([b2] if b2 is not None else []) + ([into] if into is not None else [])
    first_out = first_side + len(side_ins) + len(tail)
    assert not side or nk == 1

    assert not side or (n_out == 1 and not b_pieces and not tail)
    SIDE_SLICES = 4

    def body(*refs):
        dot = lambda lhs, rhs: lax.dot_general(lhs.astype(BF16), rhs.astype(BF16), (dims, ((), ())),
                                               preferred_element_type=F32)
        if side:
            refs = refs[1:]
            a_ref, b_ref, out_ref = refs[0], refs[1], refs[first_out]
            ins, sides = refs[first_side:first_out], refs[first_out + 1:first_out + 1 + len(side_outs)]
            rows = out_ref.shape[0] // SIDE_SLICES
            srows = sides[0].shape[-2] // SIDE_SLICES
            for part in range(SIDE_SLICES):
                rs = slice(part * srows, (part + 1) * srows)
                for val, o in zip(side["fn"](*[r[..., rs, :] for r in ins]), sides):
                    o[rs, :] = val.astype(o.dtype)
                ms = slice(part * rows, (part + 1) * rows)
                lhs = a_ref[:, ms] if dims == TN else a_ref[ms, :]
                acc = dot(lhs, b_ref[...])
                if epilogue:
                    (acc,) = epilogue(acc, *[e[ms, :] for e in refs[2:2 + n_extra]])
                out_ref[ms, :] = acc.astype(out_ref.dtype)
            return
        a_ref, b_ref = refs[0], refs[1]
        ex = refs[2:2 + n_extra]
        out_refs = refs[first_out:first_out + n_out]
        if b_pieces:
            ks = b_ref.shape[-1]
            part = dot(a_ref[:, 0:ks], b_ref[0])
            for piece in range(1, b_pieces):
                part = part + dot(a_ref[:, piece * ks:(piece + 1) * ks], b_ref[piece])
        elif b2 is not None:
            ks = b_ref.shape[-1]
            part = dot(a_ref[:, 0:ks], b_ref[...]) + dot(a_ref[:, ks:2 * ks], refs[first_side][...])
        else:
            part = dot(a_ref[...], b_ref[...])

        def finish(acc):
            res = epilogue(acc, *[e[...] for e in ex]) if epilogue else (acc,)
            for val, o in zip(res, out_refs):
                o[...] = val.astype(o.dtype)

        if nk == 1:
            finish(part)
        else:
            acc_ref = refs[-1]
            k = pl.program_id(2)

            @pl.when(k == 0)
            def _():
                acc_ref[...] = part

            @pl.when(k > 0)
            def _():
                acc_ref[...] += part

            @pl.when(k == nk - 1)
            def _():
                finish(acc_ref[...])

    scratch = [pltpu.VMEM(acc_shape, F32)] if nk > 1 else []
    out_shape = [jax.ShapeDtypeStruct(s, d) for s, d, _ in outs]
    in_specs = [a_spec, b_spec, *extra_specs] + [ANY_SPEC] * len(deps)
    out_specs = [sp for _, _, sp in outs]
    if not side:
        in_specs += ([b_spec] if b2 is not None else []) + ([ANY_SPEC] if into is not None else [])
        return list(pl.pallas_call(
            body, name=name, out_shape=out_shape, grid=grid, in_specs=in_specs, out_specs=out_specs,
            scratch_shapes=scratch, compiler_params=_params("parallel", "parallel", "arbitrary"),
            input_output_aliases={len(in_specs) - 1: 0} if into is not None else {},
        )(a, b, *extra, *deps, *tail))

    def with_scalars(spec):
        if spec.index_map is None:
            return spec
        return pl.BlockSpec(spec.block_shape, lambda i, j, k, s, f=spec.index_map: f(i, j, k))

    def by_step(block, index_map):
        return pl.BlockSpec(block, lambda i, j, k, s: index_map((i * grid[1] + j) * grid[2] + k, s))

    return list(pl.pallas_call(
        body, name=name,
        out_shape=out_shape + [jax.ShapeDtypeStruct(s, d) for s, d, _, _ in side_outs],
        grid_spec=pltpu.PrefetchScalarGridSpec(
            num_scalar_prefetch=1, grid=grid,
            in_specs=[with_scalars(sp) for sp in in_specs] + [by_step(blk, im) for _, blk, im in side_ins],
            out_specs=[with_scalars(sp) for sp in out_specs] + [by_step(blk, im) for _, _, blk, im in side_outs],
            scratch_shapes=scratch),
        compiler_params=_params("arbitrary", "arbitrary", "arbitrary"),
    )(side["scalars"], a, b, *extra, *deps, *[arr for arr, _, _ in side_ins]))


def _side_pair_sum(pieces, got, place):
    _, rows, cols = pieces.shape

    def build(steps):
        per_chip = steps // N_CHIP
        tr = rows // per_chip
        assert per_chip * N_CHIP == steps and tr * per_chip == rows and tr % 16 == 0
        blk = (None, tr, cols)
        here = lambda st, s: (st // per_chip, st % per_chip, 0)
        return dict(scalars=place,
                    ins=[(pieces, blk, lambda st, s: (2 * (st // per_chip) + s[0], st % per_chip, 0)), (got, blk, here)],
                    outs=[((N_CHIP, rows, cols), pieces.dtype, blk, here)],
                    fn=lambda mine, other: (mine.astype(F32) + other.astype(F32),))
    return build


def _side_adamw(w, m, v, sums, got, place):
    rows, cols = w.shape

    def build(steps):
        tr = rows // steps
        assert tr * steps == rows and tr % 16 == 0
        blk, here = (tr, cols), (lambda st, s: (st, 0))

        def fn(w, m, v, own, got):
            g = own.astype(F32)
            for k in range(3):
                g = g + got[k].astype(F32)
            delta, m_new, v_new = _adamw_math(w, g, m, v)
            return g, delta, m_new, v_new

        return dict(scalars=place,
                    ins=[(w, blk, here), (m, blk, here), (v, blk, here),
                         (sums, (None, tr, cols), lambda st, s: (s[1], st, 0)),
                         (got, (3, tr, cols), lambda st, s: (0, st, 0))],
                    outs=[((rows, cols), F32, blk, here)] * 4, fn=fn)
    return build


def _mm_nn(name, a, b, out_dtype=F32, tm=1024, tn=1024, tk=2048, epilogue=None, n_out=1, extra=(), deps=(), side=None,
           out_cols=None, out_at=0, into=None):
    m, kd = a.shape
    n = b.shape[1]
    tm, tn, tk = _tile(m, tm), _tile(n, tn), _tile(kd, tk)
    dts = out_dtype if isinstance(out_dtype, (list, tuple)) else [out_dtype] * n_out
    shift = out_at // tn
    o_spec = pl.BlockSpec((tm, tn), lambda i, j, k: (i, j + shift))
    grid = (m // tm, n // tn, kd // tk)
    return _mm(name, a, b, dims=NN, grid=grid,
               a_spec=pl.BlockSpec((tm, tk), lambda i, j, k: (i, k)),
               b_spec=pl.BlockSpec((tk, tn), lambda i, j, k: (k, j)),
               outs=[((m, out_cols or n), dt, o_spec) for dt in dts], extra=extra, extra_specs=[o_spec] * len(extra),
               epilogue=epilogue, acc_shape=(tm, tn), deps=deps, side=side and side(grid[0] * grid[1] * grid[2]),
               into=into)


def _mm_nn_pieces(name, a, b, out_dtype=F32, tm=1024, tk=2048, epilogue=None, n_out=1):
    m, kd = a.shape
    npc, _, ns = b.shape
    tm, tk = _tile(m, tm), _tile(kd, tk)
    tn = _tile(ns, 1024)
    r = ns // tn
    dts = out_dtype if isinstance(out_dtype, (list, tuple)) else [out_dtype] * n_out
    o_spec = pl.BlockSpec((tm, tn), lambda i, j, k: (i, j))
    return _mm(name, a, b, dims=NN, grid=(m // tm, npc * r, kd // tk),
               a_spec=pl.BlockSpec((tm, tk), lambda i, j, k: (i, k)),
               b_spec=pl.BlockSpec((None, tk, tn), lambda i, j, k: (j // r, k, j % r)),
               outs=[((m, npc * ns), dt, o_spec) for dt in dts], epilogue=epilogue, acc_shape=(tm, tn))


def _mm_nt(name, a, b, out_dtype=F32, tm=1024, tn=1024, tk=2048, epilogue=None, extra=(), n_rows=None, deps=(),
           side=None, b2=None):
    m, kd = a.shape
    n = n_rows or b.shape[0]
    tm, tn, tk = _tile(m, tm), _tile(n, tn), _tile(kd, tk)
    o_spec = pl.BlockSpec((tm, tn), lambda i, j, k: (i, j))
    if b2 is not None:
        tk = kd
        b_spec = pl.BlockSpec((tn, kd // 2), lambda i, j, k: (j, 0))
    else:
        b_spec = pl.BlockSpec((tn, tk), lambda i, j, k: (j, k))
    grid = (m // tm, n // tn, kd // tk)
    res = _mm(name, a, b, dims=NT, grid=grid,
              a_spec=pl.BlockSpec((tm, tk), lambda i, j, k: (i, k)), b_spec=b_spec,
              outs=[((m, n), out_dtype, o_spec)], extra=extra, extra_specs=[o_spec] * len(extra),
              epilogue=epilogue, acc_shape=(tm, tn), deps=deps, side=side and side(grid[0] * grid[1] * grid[2]), b2=b2)
    return res if side else res[0]


def _mm_nt_pieces(name, a, b, out_dtype=F32, tm=1024, tn=256, deps=()):
    m = a.shape[0]
    npc, n, ks = b.shape
    tm, tn = _tile(m, tm), _tile(n, tn)
    return _mm(name, a, b, dims=NT, grid=(m // tm, n // tn, 1),
               a_spec=pl.BlockSpec((tm, npc * ks), lambda i, j, k: (i, 0)),
               b_spec=pl.BlockSpec((npc, tn, ks), lambda i, j, k: (0, j, 0)),
               outs=[((m, n), out_dtype, pl.BlockSpec((tm, tn), lambda i, j, k: (i, j)))],
               deps=deps, b_pieces=npc)[0]


def _mm_tn(name, a, b, out_dtype=BF16, tm=1024, tn=1024, pieces=0, deps=(), side=None):
    t, m = a.shape
    n = b.shape[1]
    tm = _tile(m, tm)
    if pieces:
        ns = n // pieces
        tn = _tile(ns, tn)
        r = ns // tn
        out = ((pieces, m, ns), out_dtype, pl.BlockSpec((None, tm, tn), lambda i, j, k: (j // r, i, j % r)))
    else:
        tn = _tile(n, tn)
        out = ((m, n), out_dtype, pl.BlockSpec((tm, tn), lambda i, j, k: (i, j)))
    grid = (m // tm, n // tn, 1)
    res = _mm(name, a, b, dims=TN, grid=grid,
              a_spec=pl.BlockSpec((t, tm), lambda i, j, k: (0, i)),
              b_spec=pl.BlockSpec((t, tn), lambda i, j, k: (0, j)),
              outs=[out], deps=deps, side=side and side(grid[0] * grid[1]))
    return res if side else res[0]


def _rms_f(x, g):
    r = lax.rsqrt(jnp.mean(x * x, axis=-1, keepdims=True) + EPS)
    xh = x * r
    return xh * g, xh, r


def _rms_b(dy, g, xh, r):
    dxh = dy * g
    dx = r * (dxh - xh * jnp.mean(dxh * xh, axis=-1, keepdims=True))
    dg = jnp.sum(dy * xh, axis=0, keepdims=True)
    return dx, dg


def _rowwise(name, fn, row_ins, vec_ins, row_outs, vec_outs, deps=()):
    t = row_ins[0].shape[0]
    tr = min(ROW_TILE, t)
    nr, nv, no = len(row_ins), len(vec_ins), len(row_outs)
    first_out = nr + nv + len(deps)

    def body(*refs):
        ri, vi = refs[:nr], refs[nr:nr + nv]
        ro, vo = refs[first_out:first_out + no], refs[first_out + no:]
        routs, vouts = fn(*[r[...] for r in ri], *[v[...] for v in vi])
        for o, val in zip(ro, routs):
            o[...] = val.astype(o.dtype)
        i = pl.program_id(0)
        for o, val in zip(vo, vouts):
            @pl.when(i == 0)
            def _():
                o[...] = val

            @pl.when(i > 0)
            def _():
                o[...] += val

    res = pl.pallas_call(
        body, name=name,
        out_shape=[jax.ShapeDtypeStruct((t, w), d) for w, d in row_outs]
        + [jax.ShapeDtypeStruct((1, w), F32) for w in vec_outs],
        grid=(t // tr,),
        in_specs=[pl.BlockSpec((tr, a.shape[1]), lambda i: (i, 0)) for a in row_ins]
        + [pl.BlockSpec((1, a.shape[1]), lambda i: (0, 0)) for a in vec_ins] + [ANY_SPEC] * len(deps),
        out_specs=[pl.BlockSpec((tr, w), lambda i: (i, 0)) for w, _ in row_outs]
        + [pl.BlockSpec((1, w), lambda i: (0, 0)) for w in vec_outs],
        compiler_params=_params("arbitrary"),
    )(*row_ins, *vec_ins, *deps)
    return list(res)


def _norm_fwd(x, g, name, deps=()):
    d = x.shape[1]
    return _rowwise(name, lambda x, g: ((_rms_f(x, g)[0],), ()), [x], [g], [(d, BF16)], [], deps=deps)[0]


def _norm_bwd(x, dxn, dres, g, name):
    d = x.shape[1]

    def fn(x, dxn, dres, g):
        _, xh, r = _rms_f(x, g)
        dx, dg = _rms_b(dxn, g, xh, r)
        return (dres + dx,), (dg,)

    return _rowwise(name, fn, [x, dxn, dres], [g], [(d, F32)], [d])


def _block_fwd(h_prev, y, g_y, g_n, name, deps=()):
    d = h_prev.shape[1]

    def fn(h_prev, y, g_y, g_n):
        h = h_prev + _rms_f(y, g_y)[0]
        return (h, _rms_f(h, g_n)[0]), ()

    return _rowwise(name, fn, [h_prev, y], [g_y, g_n], [(d, F32), (d, BF16)], [], deps=deps)


def _block_bwd(h, y, d_up, d_n, g_y, g_n, name, deps=()):
    d = h.shape[1]

    def fn(h, y, d_up, d_n, g_y, g_n):
        _, hh, hr = _rms_f(h, g_n)
        dh_n, dg_n = _rms_b(d_n, g_n, hh, hr)
        dh = d_up + dh_n
        _, yh, yr = _rms_f(y, g_y)
        dy, dg_y = _rms_b(dh, g_y, yh, yr)
        return (dh, dy), (dg_y, dg_n)

    return _rowwise(name, fn, [h, y, d_up, d_n], [g_y, g_n], [(d, F32), (d, BF16)], [d, d], deps=deps)


def _loss_head(h2, z, e, target, g, name):
    d = h2.shape[1]

    def fn(h2, z, e, target, g):
        gate = jax.nn.sigmoid(z)
        y, xh, r = _rms_f(e * gate, g)
        diff = h2 + y - target
        loss = 0.5 * jnp.sum(jnp.mean(diff * diff, axis=-1, keepdims=True), axis=0, keepdims=True)
        dh3 = diff / d
        deg, dg = _rms_b(dh3, g, xh, r)
        dz = deg * e * gate * (1.0 - gate)
        de = deg * gate
        return (dh3, dz, de), (dg, jnp.broadcast_to(loss, (1, LANES)))

    return _rowwise(name, fn, [h2, z, e, target], [g], [(d, F32), (d, BF16), (d, BF16)], [d, LANES])


def _shift_down(x, s, rows):
    return jnp.where(rows >= s, pltpu.roll(x, s, axis=0), 0.0)


def _shift_up(x, s, rows):
    t = x.shape[0]
    return jnp.where(rows < t - s, pltpu.roll(x, t - s, axis=0), 0.0)


def _conv_pre(x, w, rows):
    pre = x * w[QK_CONV - 1:QK_CONV, :]
    for s in range(1, QK_CONV):
        pre = pre + _shift_down(x, s, rows) * w[QK_CONV - 1 - s:QK_CONV - s, :]
    return pre


def _conv_fwd(proj_a, w, start, name, scale_t=1.0):
    t = proj_a.shape[0]
    width = w.shape[1]
    tc = _tile(width, 256)
    off = start // tc

    def body(x_ref, w_ref, y_ref, yt_ref):
        x = x_ref[...]
        rows = lax.broadcasted_iota(jnp.int32, x.shape, 0)
        pre = _conv_pre(x, w_ref[...], rows)
        y = pre * jax.nn.sigmoid(pre)
        y_ref[...] = y
        yt_ref[...] = (y * scale_t).T.astype(yt_ref.dtype)

    return pl.pallas_call(
        body, name=name,
        out_shape=[jax.ShapeDtypeStruct((t, width), F32), jax.ShapeDtypeStruct((width, t), BF16)],
        grid=(width // tc,),
        in_specs=[pl.BlockSpec((t, tc), lambda j: (0, off + j)), pl.BlockSpec((QK_CONV, tc), lambda j: (0, j))],
        out_specs=[pl.BlockSpec((t, tc), lambda j: (0, j)), pl.BlockSpec((tc, t), lambda j: (j, 0))],
        compiler_params=_params("parallel"),
    )(proj_a, w)


def _conv_bwd(proj_a, w, dy, start, name):
    t = proj_a.shape[0]
    width = w.shape[1]
    tc = _tile(width, 256)
    off = start // tc

    def body(x_ref, w_ref, dy_ref, dx_ref, dw_ref):
        x, w = x_ref[...], w_ref[...]
        rows = lax.broadcasted_iota(jnp.int32, x.shape, 0)
        pre = _conv_pre(x, w, rows)
        sig = jax.nn.sigmoid(pre)
        dpre = dy_ref[...] * (sig * (1.0 + pre * (1.0 - sig)))
        dx = dpre * w[QK_CONV - 1:QK_CONV, :]
        dws = [jnp.sum(dpre * x, axis=0, keepdims=True)]
        for s in range(1, QK_CONV):
            dx = dx + _shift_up(dpre, s, rows) * w[QK_CONV - 1 - s:QK_CONV - s, :]
            dws.append(jnp.sum(dpre * _shift_down(x, s, rows), axis=0, keepdims=True))
        dx_ref[...] = dx.astype(dx_ref.dtype)
        for s in range(QK_CONV):
            dw_ref[QK_CONV - 1 - s:QK_CONV - s, :] = dws[s]

    return pl.pallas_call(
        body, name=name,
        out_shape=[jax.ShapeDtypeStruct((t, width), BF16), jax.ShapeDtypeStruct((QK_CONV, width), F32)],
        grid=(width // tc,),
        in_specs=[pl.BlockSpec((t, tc), lambda j: (0, off + j)), pl.BlockSpec((QK_CONV, tc), lambda j: (0, j)),
                  pl.BlockSpec((t, tc), lambda j: (0, j))],
        out_specs=[pl.BlockSpec((t, tc), lambda j: (0, j)), pl.BlockSpec((QK_CONV, tc), lambda j: (0, j))],
        compiler_params=_params("parallel"),
    )(proj_a, w, dy)


def _dot(a, b, dims):
    return lax.dot_general(a.astype(BF16), b.astype(BF16), (dims, ((), ())), preferred_element_type=F32)


def _dot_split(ones, x):
    hi = x.astype(BF16)
    rest = x - hi.astype(F32)
    mid = rest.astype(BF16)
    lo = (rest - mid.astype(F32)).astype(BF16)
    dot = lambda part: lax.dot_general(ones, part, (NN, ((), ())), preferred_element_type=F32)
    return dot(hi) + dot(mid) + dot(lo)


def _chunk_iotas():
    rows = lax.broadcasted_iota(jnp.int32, (CHUNK, CHUNK), 0)
    cols = lax.broadcasted_iota(jnp.int32, (CHUNK, CHUNK), 1)
    return rows, cols


def _gates_fwd(proj_b, bias, col_block, heads, name):
    t = proj_b.shape[0]

    def body(x_ref, b_ref, gb_ref, gbt_ref, cols_ref):
        rows, cols = _chunk_iotas()
        tri = (cols <= rows).astype(BF16)
        for ci in range(t // CHUNK):
            sl = slice(ci * CHUNK, (ci + 1) * CHUNK)
            pre = x_ref[sl, :] + b_ref[...]
            log_f = jnp.minimum(pre, 0.0) - jnp.log(1.0 + jnp.exp(-jnp.abs(pre)))
            log_f = jnp.where((cols >= heads) & (cols < 2 * heads), log_f, 0.0)
            tile = jnp.where(cols < heads, pre, _dot_split(tri, log_f))
            gb_ref[sl, :] = tile
            gbt_ref[:, sl] = tile.T
            for h in range(2 * heads):
                cols_ref[h, sl, :] = jnp.broadcast_to(tile[:, h:h + 1], (CHUNK, LANES))

    return pl.pallas_call(
        body, name=name,
        out_shape=[jax.ShapeDtypeStruct((t, LANES), F32), jax.ShapeDtypeStruct((LANES, t), F32),
                   jax.ShapeDtypeStruct((2 * heads, t, LANES), F32)],
        grid=(1,),
        in_specs=[pl.BlockSpec((t, LANES), lambda i: (0, col_block)), pl.BlockSpec((1, LANES), lambda i: (0, 0))],
        out_specs=[pl.BlockSpec((t, LANES), lambda i: (0, 0)), pl.BlockSpec((LANES, t), lambda i: (0, 0)),
                   pl.BlockSpec((2 * heads, t, LANES), lambda i: (0, 0, 0))],
        compiler_params=_params("arbitrary"),
    )(proj_b, bias)


def _gates_bwd(proj_b, bias, dcol, drow, col_block, heads, name):
    t = proj_b.shape[0]

    def body(x_ref, b_ref, dc_ref, dr_ref, dx_ref, db_ref):
        rows, cols = _chunk_iotas()
        later = (cols >= rows).astype(BF16)
        total = jnp.zeros((1, LANES), F32)
        for ci in range(t // CHUNK):
            sl = slice(ci * CHUNK, (ci + 1) * CHUNK)
            by_row = dr_ref[:, sl].T
            d = dc_ref[sl, :] + jnp.where(cols < heads, by_row,
                                          jnp.where(cols < 2 * heads, -pltpu.roll(by_row, heads, axis=1), 0.0))
            d_log_f = _dot_split(later, jnp.where(cols >= heads, d, 0.0))
            pre = x_ref[sl, :] + b_ref[...]
            dx = jnp.where(cols < heads, d, jnp.where(cols < 2 * heads, d_log_f * jax.nn.sigmoid(-pre), 0.0))
            dx_ref[sl, :] = dx.astype(dx_ref.dtype)
            total = total + jnp.sum(dx, axis=0, keepdims=True)
        db_ref[...] = total

    return pl.pallas_call(
        body, name=name,
        out_shape=[jax.ShapeDtypeStruct((t, LANES), BF16), jax.ShapeDtypeStruct((1, LANES), F32)],
        grid=(1,),
        in_specs=[pl.BlockSpec((t, LANES), lambda i: (0, col_block)), pl.BlockSpec((1, LANES), lambda i: (0, 0)),
                  pl.BlockSpec((t, LANES), lambda i: (0, 0)), pl.BlockSpec((LANES, t), lambda i: (0, 0))],
        out_specs=[pl.BlockSpec((t, LANES), lambda i: (0, 0)), pl.BlockSpec((1, LANES), lambda i: (0, 0))],
        compiler_params=_params("arbitrary"),
    )(proj_b, bias, dcol, drow)


def _gate_views(li_ref, b_ref, gbt_ref, s, head, heads, sl):
    li_c, b_c = li_ref[s, sl, :], b_ref[s, sl, :]
    li_r = jnp.broadcast_to(gbt_ref[head, :, sl], (CHUNK, CHUNK))
    b_r = jnp.broadcast_to(gbt_ref[heads + head, :, sl], (CHUNK, CHUNK))
    return li_c, b_c, li_r, b_r


def _chunk_state(kt, k, v, li_c, b_c, ct, nrow, m_prev):
    b_tot = b_c[CHUNK - 1:CHUNK, 0:1]
    a = b_tot - b_c[:, 0:1] + li_c[:, 0:1]
    m_new = jnp.maximum(b_tot + m_prev, jnp.max(a, axis=0, keepdims=True))
    dec = jnp.exp(b_tot + m_prev - m_new)
    w = jnp.exp(a - m_new)
    wv = w * v
    ct_new = dec * ct + _dot(kt, wv, NN)
    n_new = dec * nrow + jnp.sum(w * k, axis=0, keepdims=True)
    return ct_new, n_new, m_new, dec, w, wv


def _chunk_scores(q, k, b_c, li_r, b_r, m_prev):
    rows, cols = _chunk_iotas()
    dlog = jnp.where(cols <= rows, b_c - b_r + li_r, -jnp.inf)
    inter = b_c[:, 0:1] + m_prev
    mt = jnp.maximum(inter, jnp.max(dlog, axis=-1, keepdims=True))
    dw = jnp.exp(dlog - mt)
    iw = jnp.exp(inter - mt)
    a = _dot(q, k, NT)
    return dw, iw, mt, a, a * dw


def _to_row(column):
    rows, cols = _chunk_iotas()
    return jnp.sum(jnp.where(rows == cols, jnp.broadcast_to(column, (CHUNK, CHUNK)), 0.0), axis=0, keepdims=True)


def _head_out(h, o, g):
    cell, hh, r = _rms_f(h, g)
    sig = jax.nn.sigmoid(o)
    return sig * cell, (cell, hh, r, sig)


HEADS_PER_STEP = 2


def _mlstm_fwd(qc, kc, kct, proj_a, gcols, gbt, g_mlstm, heads, d_model, name, dep):
    t = qc.shape[0]
    dh = g_mlstm.shape[1] // heads
    nc = t // CHUNK
    scale = dh ** -0.5
    hp = HEADS_PER_STEP

    def body(q_ref, k_ref, kt_ref, v_ref, o_ref, li_ref, b_ref, gbt_ref, g_ref, dep_ref, out_ref, ct_out, nm_out,
             ct_ref, n_ref, m_ref):
        del dep_ref
        step = pl.program_id(0)
        ct_ref[...] = jnp.zeros_like(ct_ref)
        n_ref[...] = jnp.zeros_like(n_ref)
        m_ref[...] = jnp.zeros_like(m_ref)
        tile_row = lax.broadcasted_iota(jnp.int32, (8, LANES), 0)

        def chunk(ci, carry):
            sl = pl.ds(pl.multiple_of(ci * CHUNK, CHUNK), CHUNK)
            for s in range(hp):
                cs = slice(s * dh, (s + 1) * dh)
                q, k, v, kt = q_ref[sl, cs], k_ref[sl, cs] * scale, v_ref[sl, cs], kt_ref[cs, sl]
                li_c, b_c, li_r, b_r = _gate_views(li_ref, b_ref, gbt_ref, s, step * hp + s, heads, sl)
                ct, nrow, m_prev = ct_ref[s], n_ref[s, 0:1, :], m_ref[s, 0:1, 0:1]
                ct_out[s, ci] = ct
                nm_out[s, ci] = jnp.where(tile_row == 0, nrow, jnp.where(tile_row == 1, m_prev, 0.0))
                dw, iw, mt, _, sm = _chunk_scores(q, k, b_c, li_r, b_r, m_prev)
                iq = iw * q
                num = _dot(jnp.concatenate([iq, sm], axis=1), jnp.concatenate([ct, v], axis=0), NN)
                den = jnp.sum(iq * nrow + sm, axis=-1, keepdims=True)
                h = num / jnp.maximum(jnp.abs(den), jnp.exp(-mt))
                y, _ = _head_out(h, o_ref[sl, cs], g_ref[:, cs])
                out_ref[sl, cs] = y.astype(out_ref.dtype)
                ct_new, n_new, m_new, _, _, _ = _chunk_state(kt, k, v, li_c, b_c, ct, nrow, m_prev)
                ct_ref[s] = ct_new
                n_ref[s, 0:1, :] = n_new
                m_ref[s, 0:1, 0:1] = m_new
            return carry

        lax.fori_loop(0, nc, chunk, 0, unroll=2)

    col = lambda off: pl.BlockSpec((t, hp * dh), lambda h: (0, off // hp + h))
    slab = lambda off: pl.BlockSpec((hp, t, LANES), lambda h: (off // hp + h, 0, 0))
    return pl.pallas_call(
        body, name=name,
        out_shape=[jax.ShapeDtypeStruct((t, d_model), BF16), jax.ShapeDtypeStruct((heads, nc, dh, dh), F32),
                   jax.ShapeDtypeStruct((heads, nc, 8, LANES), F32)],
        grid=(heads // hp,),
        in_specs=[col(0), col(0), pl.BlockSpec((hp * dh, t), lambda h: (h, 0)), col(2 * heads), col(3 * heads),
                  slab(0), slab(heads), pl.BlockSpec((2 * heads, 1, t), lambda h: (0, 0, 0)),
                  pl.BlockSpec((1, hp * dh), lambda h: (0, h)), ANY_SPEC],
        out_specs=[pl.BlockSpec((t, hp * dh), lambda h: (0, h)),
                   pl.BlockSpec((hp, nc, dh, dh), lambda h: (h, 0, 0, 0)),
                   pl.BlockSpec((hp, nc, 8, LANES), lambda h: (h, 0, 0, 0))],
        scratch_shapes=[pltpu.VMEM((hp, dh, dh), F32), pltpu.VMEM((hp, 8, dh), F32), pltpu.VMEM((hp, 8, LANES), F32)],
        compiler_params=_params("arbitrary"),
    )(qc, kc, kct, proj_a, proj_a, gcols, gcols, gbt, g_mlstm, dep)


def _mlstm_bwd(qc, kc, qct, kct, proj_a, gcols, gbt, g_mlstm, states, dhcat, heads, name, dep):
    t = qc.shape[0]
    wm = g_mlstm.shape[1]
    dh = wm // heads
    nc = t // CHUNK
    scale = dh ** -0.5

    def body(q_ref, k_ref, qt_ref, kt_ref, v_ref, o_ref, li_ref, b_ref, gbt_ref, g_ref, ct_all, nm_all, dy_ref, dep_ref,
             dq_ref, dk_ref, dv_ref, do_ref, dcol_ref, drow_ref, dg_ref, dct_ref, dn_ref):
        del dep_ref
        head = pl.program_id(0)
        g = g_ref[...]

        @pl.when(head == 0)
        def _():
            dcol_ref[...] = jnp.zeros_like(dcol_ref)
            drow_ref[...] = jnp.zeros_like(drow_ref)

        def load(ci):
            sl = pl.ds(pl.multiple_of(ci * CHUNK, CHUNK), CHUNK)
            return (sl, q_ref[sl, :], k_ref[sl, :] * scale, v_ref[sl, :], kt_ref[:, sl]) \
                + _gate_views(li_ref, b_ref, gbt_ref, 0, head, heads, sl)

        dct_ref[...] = jnp.zeros_like(dct_ref)
        dn_ref[...] = jnp.zeros_like(dn_ref)
        dg_ref[...] = jnp.zeros_like(dg_ref)
        rows, cols = _chunk_iotas()
        last = lax.broadcasted_iota(jnp.int32, (CHUNK, 1), 0) == CHUNK - 1
        lane = lax.broadcasted_iota(jnp.int32, (CHUNK, LANES), 1)

        def bwd(step, carry):
            ci = nc - 1 - step
            sl, q, k, v, kt, li_c, b_c, li_r, b_r = load(ci)
            qt = qt_ref[:, sl]
            ct, nrow, m_prev = ct_all[ci], nm_all[ci, 0:1, :], nm_all[ci, 1:2, 0:1]
            dw, iw, mt, a, sm = _chunk_scores(q, k, b_c, li_r, b_r, m_prev)
            qc_ = _dot(q, ct, NN)
            qn = jnp.sum(q * nrow, axis=-1, keepdims=True)
            num = iw * qc_ + _dot(sm, v, NN)
            den = iw * qn + jnp.sum(sm, axis=-1, keepdims=True)
            floor = jnp.exp(-mt)
            dn = jnp.maximum(jnp.abs(den), floor)
            h = num / dn
            _, (cell, hh, r, sig) = _head_out(h, o_ref[sl, :], g)
            dy = dy_ref[sl, :]
            dcell = dy * sig
            do_ref[sl, :] = (dy * cell * sig * (1.0 - sig)).astype(do_ref.dtype)
            dh_, dg = _rms_b(dcell, g, hh, r)
            dg_ref[...] += dg
            dnum = dh_ / dn
            ddn = -jnp.sum(dh_ * h, axis=-1, keepdims=True) / dn
            dden = jnp.where(jnp.abs(den) >= floor, ddn * jnp.sign(den), 0.0)
            idn = iw * dnum
            idd = iw * dden
            dw_t = jnp.exp(jnp.where(cols >= rows, b_r - b_c + li_c, -jnp.inf) - _to_row(mt))
            sm_t = _dot(k, q, NT) * dw_t
            dsm = _dot(dnum, v, NT) + dden
            da_ = dsm * dw
            da_t = (_dot(v, dnum, NT) + _to_row(dden)) * dw_t
            e = da_ * a
            dq = _dot(idn, ct, NT) + _dot(da_, k, NN) + idd * nrow
            dk = _dot(da_t, q, NN)
            dv = _dot(sm_t, dnum, NN)
            dct = _dot(qt, idn, NN)
            dnr = jnp.sum(q * idd, axis=0, keepdims=True)
            db = jnp.sum(e + idn * qc_, axis=-1, keepdims=True) + idd * qn
            _, _, _, dec, w, wv = _chunk_state(kt, k, v, li_c, b_c, ct, nrow, m_prev)
            dct_new, dn_new = dct_ref[...], dn_ref[0:1, :]
            ddec = jnp.sum(jnp.sum(dct_new * ct, axis=-1, keepdims=True), axis=0, keepdims=True) \
                + jnp.sum(dn_new * nrow, axis=-1, keepdims=True)
            dk = dk + _dot(wv, dct_new, NT) + w * dn_new
            dwv = _dot(k, dct_new, NN)
            dv = dv + w * dwv
            dwt = jnp.sum(dwv * v + k * dn_new, axis=-1, keepdims=True)
            da = dwt * w
            dbtot = jnp.sum(da, axis=0, keepdims=True) + ddec * dec
            db = db - da + jnp.where(last, dbtot, 0.0)
            dct_ref[...] = dec * dct_new + dct
            dn_ref[0:1, :] = dec * dn_new + dnr
            dq_ref[sl, :] = dq
            dk_ref[sl, :] = dk * scale
            dv_ref[sl, :] = dv.astype(dv_ref.dtype)
            dcol_ref[sl, :] += jnp.where(lane == head, da, 0.0) + jnp.where(lane == heads + head, db, 0.0)
            drow_ref[head, :, sl] = jnp.sum(e, axis=0, keepdims=True)
            return carry

        lax.fori_loop(0, nc, bwd, 0, unroll=8)

    col = lambda off: pl.BlockSpec((t, dh), lambda h: (0, off + h))
    rowb = pl.BlockSpec((dh, t), lambda h: (h, 0))
    slab = lambda off: pl.BlockSpec((1, t, LANES), lambda h: (off + h, 0, 0))
    vec = pl.BlockSpec((1, dh), lambda h: (0, h))
    return pl.pallas_call(
        body, name=name,
        out_shape=[jax.ShapeDtypeStruct((t, wm), F32)] * 2 + [jax.ShapeDtypeStruct((t, wm), BF16)] * 2
        + [jax.ShapeDtypeStruct((t, LANES), F32), jax.ShapeDtypeStruct((heads, 1, t), F32),
           jax.ShapeDtypeStruct((1, wm), F32)],
        grid=(heads,),
        in_specs=[col(0), col(0), rowb, rowb, col(2 * heads), col(3 * heads), slab(0), slab(heads),
                  pl.BlockSpec((2 * heads, 1, t), lambda h: (0, 0, 0)), vec,
                  pl.BlockSpec((None, nc, dh, dh), lambda h: (h, 0, 0, 0)),
                  pl.BlockSpec((None, nc, 8, LANES), lambda h: (h, 0, 0, 0)), col(0), ANY_SPEC],
        out_specs=[col(0), col(0), col(0), col(0), pl.BlockSpec((t, LANES), lambda h: (0, 0)),
                   pl.BlockSpec((heads, 1, t), lambda h: (0, 0, 0)), vec],
        scratch_shapes=[pltpu.VMEM((dh, dh), F32), pltpu.VMEM((8, dh), F32)],
        compiler_params=_params("arbitrary"),
    )(qc, kc, qct, kct, proj_a, proj_a, gcols, gcols, gbt, g_mlstm, *states, dhcat, dep)


def _pool_window(gi):
    return jnp.where(gi == 0, 2.0, jnp.where(gi == 1, 4.0, jnp.where(gi == 2, 8.0, 16.0))).astype(F32)


def _pool_select(gi, levels):
    return jnp.where(gi == 0, levels[0], jnp.where(gi == 1, levels[1], jnp.where(gi == 2, levels[2], levels[3])))


def _pooled(u, gi, rows):
    s, levels = u, []
    for lvl in range(POOL_GROUPS):
        s = s + _shift_down(s, 1 << lvl, rows)
        levels.append(s)
    cnt = jnp.minimum((rows + 1).astype(F32), _pool_window(gi))
    return _pool_select(gi, levels) / cnt - u, cnt


def _pool_fwd(proj_b, w_pool, scale, hcat, name):
    t, d_model = hcat.shape
    g, cg, _ = w_pool.shape
    off = (d_model - g * cg) // cg

    def body(u_ref, w_ref, s_ref, hcat_ref, out_ref):
        del hcat_ref
        gi = pl.program_id(0)
        u = u_ref[...]
        rows = lax.broadcasted_iota(jnp.int32, u.shape, 0)
        pooled, _ = _pooled(u, gi, rows)
        out_ref[...] = (_dot(pooled, w_ref[...], NN) * s_ref[...]).astype(out_ref.dtype)

    return pl.pallas_call(
        body, name=name,
        out_shape=jax.ShapeDtypeStruct((t, d_model), BF16),
        grid=(g,),
        in_specs=[pl.BlockSpec((t, cg), lambda gi: (0, gi)), pl.BlockSpec((None, cg, cg), lambda gi: (gi, 0, 0)),
                  pl.BlockSpec((1, cg), lambda gi: (0, gi)), HBM_SPEC],
        out_specs=pl.BlockSpec((t, cg), lambda gi: (0, off + gi)),
        input_output_aliases={3: 0},
        compiler_params=_params("arbitrary"),
    )(proj_b, w_pool, scale, hcat)


def _pool_bwd(proj_b, w_pool, scale, dhcat, name):
    t, d_model = dhcat.shape
    g, cg, _ = w_pool.shape
    off = (d_model - g * cg) // cg

    def body(u_ref, w_ref, s_ref, dy_ref, du_ref, dw_ref, ds_ref):
        gi = pl.program_id(0)
        u, w, dy = u_ref[...], w_ref[...], dy_ref[...]
        rows = lax.broadcasted_iota(jnp.int32, u.shape, 0)
        pooled, cnt = _pooled(u, gi, rows)
        mixed = _dot(pooled, w, NN)
        ds_ref[...] = jnp.sum(dy * mixed, axis=0, keepdims=True)
        dmixed = dy * s_ref[...]
        dw_ref[...] = _dot(pooled, dmixed, TN).astype(dw_ref.dtype)
        dpooled = _dot(dmixed, w, NT)
        s, levels = dpooled / cnt, []
        for lvl in range(POOL_GROUPS):
            s = s + _shift_up(s, 1 << lvl, rows)
            levels.append(s)
        du_ref[...] = (_pool_select(gi, levels) - dpooled).astype(du_ref.dtype)

    return pl.pallas_call(
        body, name=name,
        out_shape=[jax.ShapeDtypeStruct((t, g * cg), BF16), jax.ShapeDtypeStruct((g, cg, cg), BF16),
                   jax.ShapeDtypeStruct((1, g * cg), F32)],
        grid=(g,),
        in_specs=[pl.BlockSpec((t, cg), lambda gi: (0, gi)), pl.BlockSpec((None, cg, cg), lambda gi: (gi, 0, 0)),
                  pl.BlockSpec((1, cg), lambda gi: (0, gi)), pl.BlockSpec((t, cg), lambda gi: (0, off + gi))],
        out_specs=[pl.BlockSpec((t, cg), lambda gi: (0, gi)), pl.BlockSpec((None, cg, cg), lambda gi: (gi, 0, 0)),
                   pl.BlockSpec((1, cg), lambda gi: (0, gi))],
        compiler_params=_params("arbitrary"),
    )(proj_b, w_pool, scale, dhcat)


def _pad_cols(a, width):
    return jnp.pad(a, ((0, 0), (0, width - a.shape[1])))


W_IN_COL_TILE = 512


def _w_in_rows(pieces, main, gates, name):
    n, shard, d = pieces.shape
    pool = n * shard - main - gates
    tc = _tile(d, W_IN_COL_TILE)

    def body(x_ref, a_ref, b_ref, flat):
        for j in range(n):
            flat[j * shard:(j + 1) * shard, :] = x_ref[j]
        a_ref[...] = flat[:main, :]
        b_ref[:pool, :] = flat[main + gates:, :]
        b_ref[pool:pool + gates, :] = flat[main:main + gates, :]
        b_ref[pool + gates:, :] = jnp.zeros((LANES - gates, tc), b_ref.dtype)

    return pl.pallas_call(
        body, name=name,
        out_shape=[jax.ShapeDtypeStruct((main, d), pieces.dtype), jax.ShapeDtypeStruct((pool + LANES, d), pieces.dtype)],
        grid=(d // tc,),
        in_specs=[pl.BlockSpec((n, shard, tc), lambda j: (0, 0, j))],
        out_specs=[pl.BlockSpec((main, tc), lambda j: (0, j)), pl.BlockSpec((pool + LANES, tc), lambda j: (0, j))],
        scratch_shapes=[pltpu.VMEM((n * shard, tc), pieces.dtype)],
        compiler_params=_params("parallel"),
    )(pieces)


def _w_in_pieces(rows_main, rows_rest, shard, gates, name):
    main, d = rows_main.shape
    pool = rows_rest.shape[0] - LANES
    n = (main + gates + pool) // shard
    tc = _tile(d, W_IN_COL_TILE)

    def body(a_ref, b_ref, o_ref, flat):
        flat[:main, :] = a_ref[...]
        flat[main:main + gates, :] = b_ref[pool:pool + gates, :]
        flat[main + gates:, :] = b_ref[:pool, :]
        for j in range(n):
            o_ref[j] = flat[j * shard:(j + 1) * shard, :]

    return pl.pallas_call(
        body, name=name,
        out_shape=jax.ShapeDtypeStruct((n, shard, d), rows_main.dtype),
        grid=(d // tc,),
        in_specs=[pl.BlockSpec((main, tc), lambda j: (0, j)), pl.BlockSpec((pool + LANES, tc), lambda j: (0, j))],
        out_specs=pl.BlockSpec((n, shard, tc), lambda j: (0, 0, j)),
        scratch_shapes=[pltpu.VMEM((n * shard, tc), rows_main.dtype)],
        compiler_params=_params("parallel"),
    )(rows_main, rows_rest)


def kernel(x, p, w_in, b_gates, w_qk_conv, g_mlstm, w_pool, pool_scale, w_out, g_mix_pre, g_mix_post, w_ff1, w_ff2, g_ff_pre, g_ff_post, w_ple_proj, w_ple_gate, g_ple_gate, g_ple_post, loss_target, m_w_in, m_b_gates, m_w_qk_conv, m_g_mlstm, m_w_pool, m_pool_scale, m_w_out, m_g_mix_pre, m_g_mix_post, m_w_ff1, m_w_ff2, m_g_ff_pre, m_g_ff_post, m_w_ple_proj, m_w_ple_gate, m_g_ple_gate, m_g_ple_post, v_w_in, v_b_gates, v_w_qk_conv, v_g_mlstm, v_w_pool, v_pool_scale, v_w_out, v_g_mix_pre, v_g_mix_post, v_w_ff1, v_w_ff2, v_g_ff_pre, v_g_ff_post, v_w_ple_proj, v_w_ple_gate, v_g_ple_gate, v_g_ple_post):
    weights = dict(w_in=w_in, b_gates=b_gates, w_qk_conv=w_qk_conv, g_mlstm=g_mlstm, w_pool=w_pool,
                   pool_scale=pool_scale, w_out=w_out, g_mix_pre=g_mix_pre, g_mix_post=g_mix_post, w_ff1=w_ff1,
                   w_ff2=w_ff2, g_ff_pre=g_ff_pre, g_ff_post=g_ff_post, w_ple_proj=w_ple_proj,
                   w_ple_gate=w_ple_gate, g_ple_gate=g_ple_gate, g_ple_post=g_ple_post)
    mom1 = dict(w_in=m_w_in, b_gates=m_b_gates, w_qk_conv=m_w_qk_conv, g_mlstm=m_g_mlstm, w_pool=m_w_pool,
                pool_scale=m_pool_scale, w_out=m_w_out, g_mix_pre=m_g_mix_pre, g_mix_post=m_g_mix_post,
                w_ff1=m_w_ff1, w_ff2=m_w_ff2, g_ff_pre=m_g_ff_pre, g_ff_post=m_g_ff_post,
                w_ple_proj=m_w_ple_proj, w_ple_gate=m_w_ple_gate, g_ple_gate=m_g_ple_gate, g_ple_post=m_g_ple_post)
    mom2 = dict(w_in=v_w_in, b_gates=v_b_gates, w_qk_conv=v_w_qk_conv, g_mlstm=v_g_mlstm, w_pool=v_w_pool,
                pool_scale=v_pool_scale, w_out=v_w_out, g_mix_pre=v_g_mix_pre, g_mix_post=v_g_mix_post,
                w_ff1=v_w_ff1, w_ff2=v_w_ff2, g_ff_pre=v_g_ff_pre, g_ff_post=v_g_ff_post,
                w_ple_proj=v_w_ple_proj, w_ple_gate=v_w_ple_gate, g_ple_gate=v_g_ple_gate, g_ple_post=v_g_ple_post)
    order = list(weights)

    t, d = x.shape[1], x.shape[2]
    heads = MLSTM_HEADS
    wm = g_mlstm.shape[1]
    pw = pool_scale.shape[1]
    cg = pw // POOL_GROUPS
    in_shard = w_in.shape[2]
    xs, ps, tgt = x[0], p[0, 0], loss_target[0]

    mx, my, mc = lax.axis_index("x"), lax.axis_index("y"), lax.axis_index("c")
    core = jnp.reshape(mc, (1,)).astype(jnp.int32)
    chip = jnp.reshape(2 * mx + my, (1,)).astype(jnp.int32)

    first = [w_in[0].T.astype(BF16), w_qk_conv[0], w_pool[0].reshape(POOL_GROUPS * w_pool.shape[2], cg).astype(BF16)]
    started, token = _gather_start([[w_ple_proj[0].astype(BF16)], first], "gather_start_in")
    anchor = token[0, 0]
    later = lambda w: (w[0] + anchor).astype(BF16)
    ff2_shard = later(w_ff2)
    rest, token = _gather_start([[later(w_out)], [later(w_ff1)], [ff2_shard[:, :d // 2]], [ff2_shard[:, d // 2:]],
                                 [later(w_ple_gate)]], "gather_start_rest")
    started = dict(zip(["ple_proj", "in", "out", "ff1", "ff2_lo", "ff2_hi", "ple_gate"], started + rest))
    bias = _pad_cols(b_gates, LANES)
    small = ["b_gates", "g_mlstm", "pool_scale", "g_mix_pre", "g_mix_post", "g_ff_pre", "g_ff_post", "g_ple_gate",
             "g_ple_post"]
    pack = lambda src, more=(): _pad_rows16(jnp.concatenate(
        [_pad_cols(src[k] + anchor, d) for k in small] + [_pad_cols(a, d) for a in more], axis=0))
    small_state = [pack(weights), pack(mom1), pack(mom2)]
    in_state = [a[0].T + anchor for a in (w_in, m_w_in, v_w_in)]

    xn = _norm_fwd(xs, g_mix_pre, "norm_mix_pre", deps=[token])
    (g_pproj,) = _forward_pair(_gather_wait(started["ple_proj"], xn, "gather_wait_ple_proj"), "gather_pair_ple_proj")
    e = _mm_nn_pieces("ple_proj", ps, g_pproj)[0]
    g_in, g_conv, g_pool = _forward_pair(
        _gather_wait(started["in"], [e] + small_state + in_state, "gather_wait_in"), "gather_pair_in")
    in_t_a, in_t_b = _w_in_rows(g_in, 4 * wm, 2 * heads, "w_in_rows")
    conv_w = g_conv.transpose(1, 0, 2).reshape(QK_CONV, 2 * wm)
    pool_w = g_pool.reshape(N_DEV, POOL_GROUPS, cg // N_DEV, cg).transpose(1, 0, 2, 3).reshape(POOL_GROUPS, cg, cg)
    proj_a = _mm_nt("proj_qkvo", xn, in_t_a, n_rows=4 * wm)
    proj_b = _mm_nt("proj_pool_gates", xn, in_t_b)
    qc, qct = _conv_fwd(proj_a, conv_w[:, :wm], 0, "q_conv")
    kc, kct = _conv_fwd(proj_a, conv_w[:, wm:], wm, "k_conv", scale_t=(wm // heads) ** -0.5)
    gb, gbt, gcols = _gates_fwd(proj_b, bias, pw // LANES, heads, "gates")
    gbt = gbt[:2 * heads].reshape(2 * heads, 1, t)
    fwd_out = _forward_start(_gather_wait(started["out"], gb, "gather_wait_out"), "gather_pair_start_out")
    hcat, *mlstm_states = _mlstm_fwd(qc, kc, kct, proj_a, gcols, gbt, g_mlstm, heads, d, "mlstm", fwd_out["token"])
    hcat = _pool_fwd(proj_b, pool_w, pool_scale, hcat, "pool")
    (g_out,) = _forward_wait(fwd_out, hcat, "gather_pair_wait_out")
    out_w = g_out.reshape(d, d)
    mix = _mm_nn("mix_out", hcat, out_w)[0]
    fwd_ff1 = _forward_start(_gather_wait(started["ff1"], mix, "gather_wait_ff1"), "gather_pair_start_ff1")
    h1, hn = _block_fwd(xs, mix, g_mix_post, g_ff_pre, "residual_mix", deps=[fwd_ff1["token"]])
    (g_ff1,) = _forward_wait(fwd_ff1, hn, "gather_pair_wait_ff1")
    relu_a, act = _mm_nn_pieces("ff1", hn, g_ff1, out_dtype=[BF16, BF16],
                                epilogue=lambda acc: (jnp.maximum(acc, 0.0), jnp.square(jnp.maximum(acc, 0.0))))
    (g_ff2_lo,) = _forward_pair(_gather_wait(started["ff2_lo"], act, "gather_wait_ff2_lo"), "gather_pair_ff2_lo")
    ff2_lo = g_ff2_lo.reshape(-1, d // 2)
    ff = _mm_nn("ff2_lo", act, ff2_lo, tn=256, tk=4 * d, out_cols=d)[0]
    (g_ff2_hi,) = _forward_pair(_gather_wait(started["ff2_hi"], ff, "gather_wait_ff2_hi"), "gather_pair_ff2_hi")
    ff2_hi = g_ff2_hi.reshape(-1, d // 2)
    ff = _mm_nn("ff2_hi", act, ff2_hi, tn=256, tk=4 * d, out_cols=d, out_at=d // 2, into=ff)[0]
    fwd_ple = _forward_start(_gather_wait(started["ple_gate"], ff, "gather_wait_ple"), "gather_pair_start_ple")
    h2, hg = _block_fwd(h1, ff, g_ff_post, g_ple_gate, "residual_ff", deps=[fwd_ple["token"]])
    (g_pgate,) = _forward_wait(fwd_ple, hg, "gather_pair_wait_ple")
    pgate_w = g_pgate.reshape(d, d)
    z = _mm_nn("ple_gate", hg, pgate_w)[0]
    dh3, dz, de, dg_ple_post, loss_part = _loss_head(h2, z, e, tgt, g_ple_post, "loss_head")

    out = {}

    def state(k):
        if k == "w_in":
            return in_state
        return [a.reshape(-1, a.shape[-1]) for a in (weights[k], mom1[k], mom2[k])]

    def pair_begin(pieces, tag):
        keys = list(pieces)
        return keys, _pair_start([pieces[k] for k in keys], "reduce_pair_start_" + tag)

    def chip_begin(group, after, tag):
        keys, begun = group
        mine, got = _pair_wait(begun, after, "reduce_pair_wait_" + tag)
        sums = [_pair_sum(p_, g, core, "pair_sum_" + k) for k, p_, g in zip(keys, mine, got)]
        return keys, _chip_start(sums, "reduce_start_" + tag)

    def reduce_finish(group, after, tag):
        keys, begun = group
        sums, got = _chip_wait(begun, after, "reduce_wait_" + tag)
        for k, s, g in zip(keys, sums, got):
            res = _adamw_shard(*state(k), s, g, chip, "adamw_" + k)
            out[k] = [r.T[None] if k == "w_in" else r.reshape(weights[k].shape) for r in res]

    token_of = lambda group: group[1]["token"]
    dhg = _mm_nt("d_ple_gate_in", dz, pgate_w)
    gw_pgate = _mm_tn("gw_ple_gate", hg, dz)
    gw_pproj = _mm_tn("gw_ple_proj", ps, de, pieces=N_DEV)
    dh2, dff, dg_ff_post, dg_ple_gate = _block_bwd(h2, ff, dh3, dhg, g_ff_post, g_ple_gate, "d_residual_ff")
    da = _mm_nt("d_ff2_in", dff, ff2_lo, b2=ff2_hi, out_dtype=BF16, extra=[relu_a],
                epilogue=lambda acc, r: (acc * (2.0 * r.astype(F32)),))
    gw_ff2 = _mm_tn("gw_ff2", act, dff)
    ff2_keys = ["w_ff2", "w_ple_gate", "w_ple_proj"]
    pair_ff2 = pair_begin({"w_ff2": gw_ff2.reshape(N_DEV, -1, d),
                           "w_ple_gate": gw_pgate.reshape(N_DEV, d // N_DEV, d), "w_ple_proj": gw_pproj}, "ff2")
    dhn = _mm_nt_pieces("d_ff1_in", da, g_ff1, deps=[token_of(pair_ff2)])
    place = jnp.concatenate([core, chip])
    mine_ff2, got_ff2 = _pair_wait(pair_ff2[1], dhn, "reduce_pair_wait_ff2")
    gw_ff1, sums_ff2 = _mm_tn("gw_ff1", hn, da, pieces=N_DEV, side=_side_pair_sum(mine_ff2[0], got_ff2[0], place))
    sums_ple = [_pair_sum(p_, g, core, "pair_sum_" + k) for k, p_, g in zip(ff2_keys[1:], mine_ff2[1:], got_ff2[1:])]
    red_ff2 = (ff2_keys, _chip_start([sums_ff2] + sums_ple, "reduce_start_ff2"))
    pair_ff1 = pair_begin({"w_ff1": gw_ff1}, "ff1")
    dh1, dmix, dg_mix_post, dg_ff_pre = _block_bwd(h1, mix, dh2, dhn, g_mix_post, g_ff_pre, "d_residual_mix",
                                                   deps=[token_of(pair_ff1), token_of(red_ff2)])
    mine_ff1, got_ff1 = _pair_wait(pair_ff1[1], dmix, "reduce_pair_wait_ff1")
    dhcat, sums_ff1 = _mm_nt("d_mix_in", dmix, out_w, tm=t // 2, tn=d // 2,
                             side=_side_pair_sum(mine_ff1[0], got_ff1[0], place))
    red_ff1 = (["w_ff1"], _chip_start([sums_ff1], "reduce_start_ff1"))
    gw_out = _mm_tn("gw_out", hcat, dmix, deps=[token_of(red_ff1)])
    du, gw_pool, dg_pool_scale = _pool_bwd(proj_b, pool_w, pool_scale, dhcat, "d_pool")
    pair_out = pair_begin({
        "w_out": gw_out.reshape(N_DEV, d // N_DEV, d),
        "w_pool": gw_pool.reshape(POOL_GROUPS, N_DEV, cg // N_DEV, cg).transpose(1, 0, 2, 3).reshape(N_DEV, -1, cg)},
        "out")
    dq, dk, dv, do, dcol, drow, dg_mlstm = _mlstm_bwd(qc, kc, qct, kct, proj_a, gcols, gbt, g_mlstm, mlstm_states, dhcat,
                                                      heads, "d_mlstm", token_of(pair_out))
    red_out = chip_begin(pair_out, dq, "out")
    drow = jnp.pad(drow.reshape(heads, t), ((0, LANES - heads), (0, 0)))
    dgates, dbias = _gates_bwd(proj_b, bias, dcol, drow, pw // LANES, heads, "d_gates")
    dq_pre, gw_conv_q = _conv_bwd(proj_a, conv_w[:, :wm], dq, 0, "d_q_conv")
    dk_pre, gw_conv_k = _conv_bwd(proj_a, conv_w[:, wm:], dk, wm, "d_k_conv")
    gw_conv = jnp.concatenate([gw_conv_q, gw_conv_k], axis=1)
    dproj_a = jnp.concatenate([dq_pre, dk_pre, dv, do], axis=1)
    dproj_b = jnp.concatenate([du, dgates], axis=1)
    sums_ff2, got_ff2 = _chip_wait(red_ff2[1], token_of(red_out), "reduce_wait_ff2")
    gw_a, *update = _mm_tn("gw_in_qkvo", dproj_a, xn,
                           side=_side_adamw(*state("w_ff2"), sums_ff2[0], got_ff2[0], place))
    out["w_ff2"] = [r.reshape(weights["w_ff2"].shape) for r in update]
    for k, s, g in zip(ff2_keys[1:], sums_ff2[1:], got_ff2[1:]):
        out[k] = [r.reshape(weights[k].shape) for r in _adamw_shard(*state(k), s, g, chip, "adamw_" + k)]
    gw_b = _mm_tn("gw_in_pool_gates", dproj_b, xn)
    pair_in = pair_begin({
        "w_in": _w_in_pieces(gw_a, gw_b, in_shard, 2 * heads, "w_in_pieces"),
        "w_qk_conv": gw_conv.reshape(QK_CONV, N_DEV, -1).transpose(1, 0, 2).astype(BF16)}, "in")
    dxn_b = _mm_nn("d_proj_in", dproj_b, in_t_b, deps=[token_of(pair_in)])[0]
    red_in = chip_begin(pair_in, dxn_b, "in")
    sums_ff1, got_ff1 = _chip_wait(red_ff1[1], token_of(red_in), "reduce_wait_ff1")
    dxn, *update = _mm_nn("d_proj_qkvo_in", dproj_a, in_t_a, tn=512, tk=4 * wm, deps=[token_of(red_in)],
                          extra=[dxn_b], epilogue=lambda acc, prev: (acc + prev,),
                          side=_side_adamw(*state("w_ff1"), sums_ff1[0], got_ff1[0], place))
    out["w_ff1"] = [r.reshape(weights["w_ff1"].shape) for r in update]
    grad_x, dg_mix_pre = _norm_bwd(xs, dxn, dh1, g_mix_pre, "d_norm_mix_pre")

    small_grads = dict(b_gates=dbias[:, :2 * heads], g_mlstm=dg_mlstm, pool_scale=dg_pool_scale, g_mix_pre=dg_mix_pre,
                       g_mix_post=dg_mix_post, g_ff_pre=dg_ff_pre, g_ff_post=dg_ff_post, g_ple_gate=dg_ple_gate,
                       g_ple_post=dg_ple_post)
    small_begun = _everyone_start([pack(small_grads, [loss_part])], "small_grads_start")
    reduce_finish(red_out, small_begun["token"], "out")
    reduce_finish(red_in, out["w_out"][0], "in")
    (small_parts,) = _everyone_wait(small_begun, out["w_in"][0], "small_grads_wait")
    res = _adamw_replicated(*small_state, small_parts, "adamw_replicated")
    loss = res[0][len(small), 0]
    for i, k in enumerate(small):
        width = weights[k].shape[1]
        out[k] = [r[i:i + 1, :width] for r in res]

    return (loss, grad_x[None], *[out[k][0] for k in order], *[out[k][1] for k in order],
            *[out[k][2] for k in order], *[out[k][3] for k in order])


def _pad_rows16(a):
    return jnp.pad(a, ((0, 16 - a.shape[0]), (0, 0)))
```

```python
import jax
import jax.numpy as jnp
from jax import lax
from jax.experimental import pallas as pl
from jax.experimental.pallas import tpu as pltpu

F32 = jnp.float32
BF16 = jnp.bfloat16
EPS = 1e-6
N_DEV = 8
N_CHIP = 4
LANES = 128
VMEM_LIMIT = 56 * 1024 * 1024
MLSTM_HEADS = 8
POOL_GROUPS = 4
QK_CONV = 4
CHUNK = 128
ROW_TILE = 256
ADAM_LR, ADAM_B1, ADAM_B2, ADAM_EPS, ADAM_WD, ADAM_STEP = 0.001, 0.9, 0.999, 1e-08, 0.01, 10
MESH_AXES = ("x", "y", "c")
MESH = pl.DeviceIdType.MESH
HBM_SPEC = pl.BlockSpec(memory_space=pltpu.HBM)
SEM_SPEC = pl.BlockSpec(memory_space=pltpu.SEMAPHORE)
DATAFLOW = pltpu.SideEffectType.DATAFLOW_SIDE_EFFECTING


def _params(*sem):
    if sem:
        return pltpu.CompilerParams(dimension_semantics=sem, vmem_limit_bytes=VMEM_LIMIT)
    return pltpu.CompilerParams(vmem_limit_bytes=VMEM_LIMIT)


def _tile(n, pref):
    if n <= pref:
        return n
    t = (pref // LANES) * LANES
    while t >= LANES:
        if n % t == 0:
            return t
        t -= LANES
    return n


def _hbm(a):
    return pltpu.with_memory_space_constraint(a, pltpu.HBM)


def _mesh_place():
    x, y, c = lax.axis_index("x"), lax.axis_index("y"), lax.axis_index("c")
    return x, y, c, [(1 - x, y), (x, 1 - y), (1 - x, 1 - y)]


def _chip_copies(srcs, lands, send, recv):
    x, y, c, chips = _mesh_place()
    return [pltpu.make_async_remote_copy(
        src_ref=srcs[w].at[2 * chip[0] + chip[1]], dst_ref=lands[w].at[k],
        send_sem=send.at[3 * w + k], recv_sem=recv.at[3 * w + k],
        device_id=(*chip, c), device_id_type=MESH)
        for k, chip in enumerate(chips) for w in range(len(srcs))]


def _chip_start(sums, name):
    return _split_start(_chip_copies, sums, [lax.empty((3,) + a.shape[1:], a.dtype) for a in sums], 3, name,
                        peers=_same_core_of_other_chips)


def _chip_wait(started, after, name):
    return _split_wait(_chip_copies, started, after, name)


def _pair_start(pieces, name):
    return _split_start(_pair_copies, pieces, [lax.empty((N_CHIP,) + a.shape[1:], a.dtype) for a in pieces], N_CHIP, name,
                        peers=_sibling)


def _pair_wait(started, after, name):
    return _split_wait(_pair_copies, started, after, name)


def _pair_copies(srcs, lands, send, recv):
    x, y, c, _ = _mesh_place()
    return [pltpu.make_async_remote_copy(
        src_ref=srcs[w].at[2 * ch + 1 - c], dst_ref=lands[w].at[ch],
        send_sem=send.at[N_CHIP * w + ch], recv_sem=recv.at[N_CHIP * w + ch],
        device_id=(x, y, 1 - c), device_id_type=MESH)
        for w in range(len(srcs)) for ch in range(N_CHIP)]


OWN_BARRIER_IDS = {name: i for i, name in enumerate([
    "reduce_pair_start_ff2", "reduce_pair_start_ff1", "reduce_pair_start_out", "reduce_pair_start_in",
    "gather_pair_start_out", "gather_pair_start_ff1", "gather_pair_start_ple",
    "reduce_start_ff2", "reduce_start_ff1", "reduce_start_out", "reduce_start_in",
    "gather_start_in", "gather_start_rest", "gather_pair_in", "gather_pair_ff2_lo", "gather_pair_ff2_hi",
    "gather_pair_ple_proj"])}


def _shake_hands(peers):
    barrier = pltpu.get_barrier_semaphore()
    for peer in peers:
        pl.semaphore_signal(barrier, inc=1, device_id=peer, device_id_type=MESH)
    pl.semaphore_wait(barrier, len(peers))


def _sibling():
    x, y, c, _ = _mesh_place()
    return [(x, y, 1 - c)]


def _same_core_of_other_chips():
    x, y, c, chips = _mesh_place()
    return [(*chip, c) for chip in chips]


def _split_start(copies_of, srcs, lands, per_array, name, peers=None):
    ns, nb = len(srcs), len(srcs) + len(lands)
    n_copies = per_array * len(lands)

    def body(*refs):
        if peers is not None:
            _shake_hands(peers())
        for cp in copies_of(refs[:ns], refs[ns:nb], refs[nb], refs[nb + 1]):
            cp.start()
        refs[-1][...] = jnp.zeros_like(refs[-1])

    params = dict(has_side_effects=DATAFLOW)
    if peers is not None:
        params["collective_id"] = OWN_BARRIER_IDS[name]
    res = pl.pallas_call(
        body, name=name,
        out_shape=[pltpu.SemaphoreType.DMA((n_copies,)), pltpu.SemaphoreType.DMA((n_copies,))]
        + [pltpu.HBM(a.shape, a.dtype) for a in list(srcs) + list(lands)] + [jax.ShapeDtypeStruct((8, LANES), F32)],
        in_specs=[HBM_SPEC] * nb,
        out_specs=[SEM_SPEC, SEM_SPEC] + [HBM_SPEC] * nb + [pl.BlockSpec(memory_space=pltpu.VMEM)],
        input_output_aliases={i: 2 + i for i in range(nb)},
        compiler_params=pltpu.CompilerParams(**params),
    )(*[_hbm(a) for a in srcs], *[_hbm(a) for a in lands])
    return dict(send=res[0], recv=res[1], srcs=list(res[2:2 + ns]), lands=list(res[2 + ns:2 + nb]), token=res[-1])


def _split_wait(copies_of, started, after, name):
    ns = len(started["srcs"])
    nb = ns + len(started["lands"])

    def body(*refs):
        for cp in copies_of(refs[:ns], refs[ns:nb], refs[nb], refs[nb + 1]):
            cp.wait_send()
            cp.wait_recv()

    res = pl.pallas_call(
        body, name=name,
        out_shape=[pltpu.HBM(a.shape, a.dtype) for a in started["srcs"] + started["lands"]],
        in_specs=[HBM_SPEC] * nb + [SEM_SPEC, SEM_SPEC, ANY_SPEC],
        out_specs=[HBM_SPEC] * nb,
        input_output_aliases={i: i for i in range(nb)},
        compiler_params=pltpu.CompilerParams(has_side_effects=DATAFLOW),
    )(*started["srcs"], *started["lands"], started["send"], started["recv"], after)
    return list(res[:ns]), list(res[ns:])


def _forward_copies(srcs, lands, send, recv):
    del srcs
    x, y, c, chips = _mesh_place()
    copies = []
    for w in range(len(lands)):
        for j, chip in enumerate(chips):
            rows = lands[w].at[4 * chip[0] + 2 * chip[1] + c]
            copies.append(pltpu.make_async_remote_copy(
                src_ref=rows, dst_ref=rows, send_sem=send.at[3 * w + j], recv_sem=recv.at[3 * w + j],
                device_id=(x, y, 1 - c), device_id_type=MESH))
    return copies


def _forward_start(lands, name):
    return _split_start(_forward_copies, [], lands, 3, name, peers=_sibling)


def _forward_wait(started, after, name):
    return _split_wait(_forward_copies, started, after, name)[1]


def _everyone_copies(srcs, lands, send, recv):
    x, y, c, _ = _mesh_place()
    me = 4 * x + 2 * y + c
    copies = []
    for w in range(len(srcs)):
        for k in range(N_DEV - 1):
            flip = k + 1
            peer = (1 - x if flip & 4 else x, 1 - y if flip & 2 else y, 1 - c if flip & 1 else c)
            copies.append(pltpu.make_async_remote_copy(
                src_ref=srcs[w], dst_ref=lands[w].at[me], send_sem=send.at[7 * w + k], recv_sem=recv.at[7 * w + k],
                device_id=peer, device_id_type=MESH))
    return copies


def _everyone_start(blocks, name):
    me = 4 * lax.axis_index("x") + 2 * lax.axis_index("y") + lax.axis_index("c")
    lands = [lax.dynamic_update_slice(lax.empty((N_DEV,) + a.shape, a.dtype), a[None], (me,) + (0,) * a.ndim)
             for a in blocks]
    return _split_start(_everyone_copies, blocks, lands, N_DEV - 1, name)


def _everyone_wait(started, after, name):
    return _split_wait(_everyone_copies, started, after, name)[1]


def _gather_copies(srcs, lands, send, recv):
    x, y, c, chips = _mesh_place()
    me = 4 * x + 2 * y + c
    targets = [(x, y, 1 - c)] + [(*chip, c) for chip in chips]
    return [pltpu.make_async_remote_copy(
        src_ref=srcs[w], dst_ref=lands[w].at[me], send_sem=send.at[4 * w + k], recv_sem=recv.at[4 * w + k],
        device_id=to, device_id_type=MESH)
        for w in range(len(srcs)) for k, to in enumerate(targets)]


def _gather_start(groups, name):
    sizes = [len(g) for g in groups]
    flat = [a for g in groups for a in g]
    n, ng = len(flat), len(groups)
    me = 4 * lax.axis_index("x") + 2 * lax.axis_index("y") + lax.axis_index("c")
    lands = [lax.dynamic_update_slice(lax.empty((N_DEV,) + a.shape, a.dtype), a[None], (me,) + (0,) * a.ndim)
             for a in flat]

    def body(*refs):
        _shake_hands(_sibling() + _same_core_of_other_chips())
        srcs, zones = refs[:n], refs[n:2 * n]
        sems = refs[2 * n:2 * n + 2 * ng]
        token = refs[-1]
        lo = 0
        for gi, size in enumerate(sizes):
            for cp in _gather_copies(srcs[lo:lo + size], zones[lo:lo + size], sems[2 * gi], sems[2 * gi + 1]):
                cp.start()
            lo += size
        token[...] = jnp.zeros_like(token)

    sem_shapes = []
    for size in sizes:
        sem_shapes += [pltpu.SemaphoreType.DMA((4 * size,))] * 2
    res = pl.pallas_call(
        body, name=name,
        out_shape=sem_shapes + [pltpu.HBM(a.shape, a.dtype) for a in flat + lands] + [jax.ShapeDtypeStruct((8, LANES), F32)],
        in_specs=[HBM_SPEC] * (2 * n),
        out_specs=[SEM_SPEC] * (2 * ng) + [HBM_SPEC] * (2 * n) + [pl.BlockSpec(memory_space=pltpu.VMEM)],
        input_output_aliases={i: 2 * ng + i for i in range(2 * n)},
        compiler_params=pltpu.CompilerParams(has_side_effects=DATAFLOW, collective_id=OWN_BARRIER_IDS[name]),
    )(*[_hbm(a) for a in flat], *[_hbm(a) for a in lands])
    out, lo = [], 0
    for gi, size in enumerate(sizes):
        out.append(dict(send=res[2 * gi], recv=res[2 * gi + 1],
                        srcs=list(res[2 * ng + lo:2 * ng + lo + size]),
                        lands=list(res[2 * ng + n + lo:2 * ng + n + lo + size])))
        lo += size
    return out, res[-1]


def _gather_wait(started, after, name):
    n = len(started["srcs"])
    after = list(after) if isinstance(after, (list, tuple)) else [after]

    def body(*refs):
        srcs, zones = refs[:n], refs[n:2 * n]
        send, recv = refs[2 * n], refs[2 * n + 1]
        for cp in _gather_copies(srcs, zones, send, recv):
            cp.wait_send()
            cp.wait_recv()

    res = pl.pallas_call(
        body, name=name,
        out_shape=[pltpu.HBM(a.shape, a.dtype) for a in started["srcs"] + started["lands"]],
        in_specs=[HBM_SPEC] * (2 * n) + [SEM_SPEC, SEM_SPEC] + [ANY_SPEC] * len(after),
        out_specs=[HBM_SPEC] * (2 * n),
        input_output_aliases={i: i for i in range(2 * n)},
        compiler_params=pltpu.CompilerParams(has_side_effects=DATAFLOW),
    )(*started["srcs"], *started["lands"], started["send"], started["recv"], *after)
    return list(res[n:])


def _forward_pair(lands, name):
    n = len(lands)

    def body(*refs):
        _shake_hands(_sibling())
        zones = refs[:n]
        send, recv = refs[2 * n:]
        x, y, c, chips = _mesh_place()

        def copy(w, j, core):
            rows = zones[w].at[4 * chips[j][0] + 2 * chips[j][1] + core]
            return pltpu.make_async_remote_copy(
                src_ref=rows, dst_ref=rows, send_sem=send.at[3 * w + j], recv_sem=recv.at[3 * w + j],
                device_id=(x, y, 1 - c), device_id_type=MESH)

        sends = [copy(w, j, c) for w in range(n) for j in range(3)]
        for cp in sends:
            cp.start()
        for w in range(n):
            for j in range(3):
                copy(w, j, 1 - c).wait_recv()
        for cp in sends:
            cp.wait_send()

    return list(pl.pallas_call(
        body, name=name,
        out_shape=[jax.ShapeDtypeStruct(a.shape, a.dtype) for a in lands],
        in_specs=[HBM_SPEC] * n, out_specs=[HBM_SPEC] * n,
        input_output_aliases={i: i for i in range(n)},
        scratch_shapes=[pltpu.SemaphoreType.DMA((3 * n,)), pltpu.SemaphoreType.DMA((3 * n,))],
        compiler_params=pltpu.CompilerParams(collective_id=OWN_BARRIER_IDS[name]),
    )(*lands))


def _pair_sum(pieces, got, core, name):
    _, rows, cols = pieces.shape
    tr, tc = _tile_2d(rows, cols)

    def body(core_ref, mine_ref, got_ref, out_ref):
        del core_ref
        out_ref[...] = (mine_ref[...].astype(F32) + got_ref[...].astype(F32)).astype(out_ref.dtype)

    return pl.pallas_call(
        body, name=name,
        out_shape=jax.ShapeDtypeStruct((N_CHIP, rows, cols), pieces.dtype),
        grid_spec=pltpu.PrefetchScalarGridSpec(
            num_scalar_prefetch=1, grid=(N_CHIP, rows // tr, cols // tc),
            in_specs=[pl.BlockSpec((None, tr, tc), lambda ch, i, j, core: (2 * ch + core[0], i, j)),
                      pl.BlockSpec((None, tr, tc), lambda ch, i, j, core: (ch, i, j))],
            out_specs=pl.BlockSpec((None, tr, tc), lambda ch, i, j, core: (ch, i, j))),
        compiler_params=_params("parallel", "parallel", "parallel"),
    )(core, pieces, got)


def _tile_2d(rows, cols, budget=2 * 1024 * 1024):
    want = max(16, budget // (4 * cols))
    if rows <= want:
        return rows, cols
    t = (want // 16) * 16
    while t >= 16:
        if rows % t == 0:
            return t, cols
        t -= 16
    return rows, _tile(cols, max(LANES, budget // (4 * rows)))


def _adamw_math(w, g, m, v):
    m = ADAM_B1 * m + (1.0 - ADAM_B1) * g
    v = ADAM_B2 * v + (1.0 - ADAM_B2) * (g * g)
    m_hat = m / (1.0 - ADAM_B1 ** ADAM_STEP)
    v_hat = v / (1.0 - ADAM_B2 ** ADAM_STEP)
    delta = -ADAM_LR * (m_hat / (jnp.sqrt(v_hat) + ADAM_EPS) + ADAM_WD * w)
    return delta, m, v


def _adamw_shard(w, m, v, sums, got, chip, name):
    rows, cols = w.shape
    tr, tc = _tile_2d(rows, cols, budget=1024 * 1024)

    def body(chip_ref, w_ref, m_ref, v_ref, own_ref, got_ref, g_out, d_out, m_out, v_out):
        del chip_ref
        g = own_ref[...].astype(F32)
        for k in range(3):
            g = g + got_ref[k].astype(F32)
        delta, m_new, v_new = _adamw_math(w_ref[...], g, m_ref[...], v_ref[...])
        g_out[...] = g
        d_out[...] = delta
        m_out[...] = m_new
        v_out[...] = v_new

    blk = pl.BlockSpec((tr, tc), lambda i, j, chip: (i, j))
    return pl.pallas_call(
        body, name=name,
        out_shape=[jax.ShapeDtypeStruct((rows, cols), F32)] * 4,
        grid_spec=pltpu.PrefetchScalarGridSpec(
            num_scalar_prefetch=1, grid=(rows // tr, cols // tc),
            in_specs=[blk, blk, blk,
                      pl.BlockSpec((None, tr, tc), lambda i, j, chip: (chip[0], i, j)),
                      pl.BlockSpec((3, tr, tc), lambda i, j, chip: (0, i, j))],
            out_specs=[blk] * 4),
        compiler_params=_params("parallel", "parallel"),
    )(chip, w, m, v, sums, got)


def _adamw_replicated(w, m, v, parts, name):
    rows, cols = w.shape

    def body(w_ref, m_ref, v_ref, parts_ref, g_out, d_out, m_out, v_out):
        g = parts_ref[0]
        for d in range(1, N_DEV):
            g = g + parts_ref[d]
        delta, m_new, v_new = _adamw_math(w_ref[...], g, m_ref[...], v_ref[...])
        g_out[...] = g
        d_out[...] = delta
        m_out[...] = m_new
        v_out[...] = v_new

    return pl.pallas_call(
        body, name=name,
        out_shape=[jax.ShapeDtypeStruct((rows, cols), F32)] * 4,
        compiler_params=_params(),
    )(w, m, v, parts)


NN = ((1,), (0,))
NT = ((1,), (1,))
TN = ((0,), (0,))


ANY_SPEC = pl.BlockSpec(memory_space=pl.ANY)


def _mm(name, a, b, *, dims, grid, a_spec, b_spec, outs, extra=(), extra_specs=(), epilogue=None, acc_shape=None,
        deps=(), b_pieces=0, side=None, b2=None, into=None):
    nk = grid[2]
    n_extra, n_out = len(extra), len(outs)
    side_ins = side["ins"] if side else []
    side_outs = side["outs"] if side else []
    first_side = 2 + n_extra + len(deps)
    tail = ([b2] if b2 is not None else []) + ([into] if into is not None else [])
    first_out = first_side + len(side_ins) + len(tail)
    assert not side or nk == 1

    assert not side or (n_out == 1 and not b_pieces and not tail)
    SIDE_SLICES = 4

    def body(*refs):
        dot = lambda lhs, rhs: lax.dot_general(lhs.astype(BF16), rhs.astype(BF16), (dims, ((), ())),
                                               preferred_element_type=F32)
        if side:
            refs = refs[1:]
            a_ref, b_ref, out_ref = refs[0], refs[1], refs[first_out]
            ins, sides = refs[first_side:first_out], refs[first_out + 1:first_out + 1 + len(side_outs)]
            rows = out_ref.shape[0] // SIDE_SLICES
            srows = sides[0].shape[-2] // SIDE_SLICES
            for part in range(SIDE_SLICES):
                rs = slice(part * srows, (part + 1) * srows)
                for val, o in zip(side["fn"](*[r[..., rs, :] for r in ins]), sides):
                    o[rs, :] = val.astype(o.dtype)
                ms = slice(part * rows, (part + 1) * rows)
                lhs = a_ref[:, ms] if dims == TN else a_ref[ms, :]
                acc = dot(lhs, b_ref[...])
                if epilogue:
                    (acc,) = epilogue(acc, *[e[ms, :] for e in refs[2:2 + n_extra]])
                out_ref[ms, :] = acc.astype(out_ref.dtype)
            return
        a_ref, b_ref = refs[0], refs[1]
        ex = refs[2:2 + n_extra]
        out_refs = refs[first_out:first_out + n_out]
        if b_pieces:
            ks = b_ref.shape[-1]
            part = dot(a_ref[:, 0:ks], b_ref[0])
            for piece in range(1, b_pieces):
                part = part + dot(a_ref[:, piece * ks:(piece + 1) * ks], b_ref[piece])
        elif b2 is not None:
            ks = b_ref.shape[-1]
            part = dot(a_ref[:, 0:ks], b_ref[...]) + dot(a_ref[:, ks:2 * ks], refs[first_side][...])
        else:
            part = dot(a_ref[...], b_ref[...])

        def finish(acc):
            res = epilogue(acc, *[e[...] for e in ex]) if epilogue else (acc,)
            for val, o in zip(res, out_refs):
                o[...] = val.astype(o.dtype)

        if nk == 1:
            finish(part)
        else:
            acc_ref = refs[-1]
            k = pl.program_id(2)

            @pl.when(k == 0)
            def _():
                acc_ref[...] = part

            @pl.when(k > 0)
            def _():
                acc_ref[...] += part

            @pl.when(k == nk - 1)
            def _():
                finish(acc_ref[...])

    scratch = [pltpu.VMEM(acc_shape, F32)] if nk > 1 else []
    out_shape = [jax.ShapeDtypeStruct(s, d) for s, d, _ in outs]
    in_specs = [a_spec, b_spec, *extra_specs] + [ANY_SPEC] * len(deps)
    out_specs = [sp for _, _, sp in outs]
    if not side:
        in_specs += ([b_spec] if b2 is not None else []) + ([ANY_SPEC] if into is not None else [])
        return list(pl.pallas_call(
            body, name=name, out_shape=out_shape, grid=grid, in_specs=in_specs, out_specs=out_specs,
            scratch_shapes=scratch, compiler_params=_params("parallel", "parallel", "arbitrary"),
            input_output_aliases={len(in_specs) - 1: 0} if into is not None else {},
        )(a, b, *extra, *deps, *tail))

    def with_scalars(spec):
        if spec.index_map is None:
            return spec
        return pl.BlockSpec(spec.block_shape, lambda i, j, k, s, f=spec.index_map: f(i, j, k))

    def by_step(block, index_map):
        return pl.BlockSpec(block, lambda i, j, k, s: index_map((i * grid[1] + j) * grid[2] + k, s))

    return list(pl.pallas_call(
        body, name=name,
        out_shape=out_shape + [jax.ShapeDtypeStruct(s, d) for s, d, _, _ in side_outs],
        grid_spec=pltpu.PrefetchScalarGridSpec(
            num_scalar_prefetch=1, grid=grid,
            in_specs=[with_scalars(sp) for sp in in_specs] + [by_step(blk, im) for _, blk, im in side_ins],
            out_specs=[with_scalars(sp) for sp in out_specs] + [by_step(blk, im) for _, _, blk, im in side_outs],
            scratch_shapes=scratch),
        compiler_params=_params("arbitrary", "arbitrary", "arbitrary"),
    )(side["scalars"], a, b, *extra, *deps, *[arr for arr, _, _ in side_ins]))


def _side_pair_sum(pieces, got, place):
    _, rows, cols = pieces.shape

    def build(steps):
        per_chip = steps // N_CHIP
        tr = rows // per_chip
        assert per_chip * N_CHIP == steps and tr * per_chip == rows and tr % 16 == 0
        blk = (None, tr, cols)
        here = lambda st, s: (st // per_chip, st % per_chip, 0)
        return dict(scalars=place,
                    ins=[(pieces, blk, lambda st, s: (2 * (st // per_chip) + s[0], st % per_chip, 0)), (got, blk, here)],
                    outs=[((N_CHIP, rows, cols), pieces.dtype, blk, here)],
                    fn=lambda mine, other: (mine.astype(F32) + other.astype(F32),))
    return build


def _side_adamw(w, m, v, sums, got, place):
    rows, cols = w.shape

    def build(steps):
        tr = rows // steps
        assert tr * steps == rows and tr % 16 == 0
        blk, here = (tr, cols), (lambda st, s: (st, 0))

        def fn(w, m, v, own, got):
            g = own.astype(F32)
            for k in range(3):
                g = g + got[k].astype(F32)
            delta, m_new, v_new = _adamw_math(w, g, m, v)
            return g, delta, m_new, v_new

        return dict(scalars=place,
                    ins=[(w, blk, here), (m, blk, here), (v, blk, here),
                         (sums, (None, tr, cols), lambda st, s: (s[1], st, 0)),
                         (got, (3, tr, cols), lambda st, s: (0, st, 0))],
                    outs=[((rows, cols), F32, blk, here)] * 4, fn=fn)
    return build


def _mm_nn(name, a, b, out_dtype=F32, tm=1024, tn=1024, tk=2048, epilogue=None, n_out=1, extra=(), deps=(), side=None,
           out_cols=None, out_at=0, into=None):
    m, kd = a.shape
    n = b.shape[1]
    tm, tn, tk = _tile(m, tm), _tile(n, tn), _tile(kd, tk)
    dts = out_dtype if isinstance(out_dtype, (list, tuple)) else [out_dtype] * n_out
    shift = out_at // tn
    o_spec = pl.BlockSpec((tm, tn), lambda i, j, k: (i, j + shift))
    grid = (m // tm, n // tn, kd // tk)
    return _mm(name, a, b, dims=NN, grid=grid,
               a_spec=pl.BlockSpec((tm, tk), lambda i, j, k: (i, k)),
               b_spec=pl.BlockSpec((tk, tn), lambda i, j, k: (k, j)),
               outs=[((m, out_cols or n), dt, o_spec) for dt in dts], extra=extra, extra_specs=[o_spec] * len(extra),
               epilogue=epilogue, acc_shape=(tm, tn), deps=deps, side=side and side(grid[0] * grid[1] * grid[2]),
               into=into)


def _mm_nn_pieces(name, a, b, out_dtype=F32, tm=1024, tk=2048, epilogue=None, n_out=1):
    m, kd = a.shape
    npc, _, ns = b.shape
    tm, tk = _tile(m, tm), _tile(kd, tk)
    tn = _tile(ns, 1024)
    r = ns // tn
    dts = out_dtype if isinstance(out_dtype, (list, tuple)) else [out_dtype] * n_out
    o_spec = pl.BlockSpec((tm, tn), lambda i, j, k: (i, j))
    return _mm(name, a, b, dims=NN, grid=(m // tm, npc * r, kd // tk),
               a_spec=pl.BlockSpec((tm, tk), lambda i, j, k: (i, k)),
               b_spec=pl.BlockSpec((None, tk, tn), lambda i, j, k: (j // r, k, j % r)),
               outs=[((m, npc * ns), dt, o_spec) for dt in dts], epilogue=epilogue, acc_shape=(tm, tn))


def _mm_nt(name, a, b, out_dtype=F32, tm=1024, tn=1024, tk=2048, epilogue=None, extra=(), n_rows=None, deps=(),
           side=None, b2=None):
    m, kd = a.shape
    n = n_rows or b.shape[0]
    tm, tn, tk = _tile(m, tm), _tile(n, tn), _tile(kd, tk)
    o_spec = pl.BlockSpec((tm, tn), lambda i, j, k: (i, j))
    if b2 is not None:
        tk = kd
        b_spec = pl.BlockSpec((tn, kd // 2), lambda i, j, k: (j, 0))
    else:
        b_spec = pl.BlockSpec((tn, tk), lambda i, j, k: (j, k))
    grid = (m // tm, n // tn, kd // tk)
    res = _mm(name, a, b, dims=NT, grid=grid,
              a_spec=pl.BlockSpec((tm, tk), lambda i, j, k: (i, k)), b_spec=b_spec,
              outs=[((m, n), out_dtype, o_spec)], extra=extra, extra_specs=[o_spec] * len(extra),
              epilogue=epilogue, acc_shape=(tm, tn), deps=deps, side=side and side(grid[0] * grid[1] * grid[2]), b2=b2)
    return res if side else res[0]


def _mm_nt_pieces(name, a, b, out_dtype=F32, tm=1024, tn=256, deps=()):
    m = a.shape[0]
    npc, n, ks = b.shape
    tm, tn = _tile(m, tm), _tile(n, tn)
    return _mm(name, a, b, dims=NT, grid=(m // tm, n // tn, 1),
               a_spec=pl.BlockSpec((tm, npc * ks), lambda i, j, k: (i, 0)),
               b_spec=pl.BlockSpec((npc, tn, ks), lambda i, j, k: (0, j, 0)),
               outs=[((m, n), out_dtype, pl.BlockSpec((tm, tn), lambda i, j, k: (i, j)))],
               deps=deps, b_pieces=npc)[0]


def _mm_tn(name, a, b, out_dtype=BF16, tm=1024, tn=1024, pieces=0, deps=(), side=None):
    t, m = a.shape
    n = b.shape[1]
    tm = _tile(m, tm)
    if pieces:
        ns = n // pieces
        tn = _tile(ns, tn)
        r = ns // tn
        out = ((pieces, m, ns), out_dtype, pl.BlockSpec((None, tm, tn), lambda i, j, k: (j // r, i, j % r)))
    else:
        tn = _tile(n, tn)
        out = ((m, n), out_dtype, pl.BlockSpec((tm, tn), lambda i, j, k: (i, j)))
    grid = (m // tm, n // tn, 1)
    res = _mm(name, a, b, dims=TN, grid=grid,
              a_spec=pl.BlockSpec((t, tm), lambda i, j, k: (0, i)),
              b_spec=pl.BlockSpec((t, tn), lambda i, j, k: (0, j)),
              outs=[out], deps=deps, side=side and side(grid[0] * grid[1]))
    return res if side else res[0]


def _rms_f(x, g):
    r = lax.rsqrt(jnp.mean(x * x, axis=-1, keepdims=True) + EPS)
    xh = x * r
    return xh * g, xh, r


def _rms_b(dy, g, xh, r):
    dxh = dy * g
    dx = r * (dxh - xh * jnp.mean(dxh * xh, axis=-1, keepdims=True))
    dg = jnp.sum(dy * xh, axis=0, keepdims=True)
    return dx, dg


def _rowwise(name, fn, row_ins, vec_ins, row_outs, vec_outs, deps=()):
    t = row_ins[0].shape[0]
    tr = min(ROW_TILE, t)
    nr, nv, no = len(row_ins), len(vec_ins), len(row_outs)
    first_out = nr + nv + len(deps)

    def body(*refs):
        ri, vi = refs[:nr], refs[nr:nr + nv]
        ro, vo = refs[first_out:first_out + no], refs[first_out + no:]
        routs, vouts = fn(*[r[...] for r in ri], *[v[...] for v in vi])
        for o, val in zip(ro, routs):
            o[...] = val.astype(o.dtype)
        i = pl.program_id(0)
        for o, val in zip(vo, vouts):
            @pl.when(i == 0)
            def _():
                o[...] = val

            @pl.when(i > 0)
            def _():
                o[...] += val

    res = pl.pallas_call(
        body, name=name,
        out_shape=[jax.ShapeDtypeStruct((t, w), d) for w, d in row_outs]
        + [jax.ShapeDtypeStruct((1, w), F32) for w in vec_outs],
        grid=(t // tr,),
        in_specs=[pl.BlockSpec((tr, a.shape[1]), lambda i: (i, 0)) for a in row_ins]
        + [pl.BlockSpec((1, a.shape[1]), lambda i: (0, 0)) for a in vec_ins] + [ANY_SPEC] * len(deps),
        out_specs=[pl.BlockSpec((tr, w), lambda i: (i, 0)) for w, _ in row_outs]
        + [pl.BlockSpec((1, w), lambda i: (0, 0)) for w in vec_outs],
        compiler_params=_params("arbitrary"),
    )(*row_ins, *vec_ins, *deps)
    return list(res)


def _norm_fwd(x, g, name, deps=()):
    d = x.shape[1]
    return _rowwise(name, lambda x, g: ((_rms_f(x, g)[0],), ()), [x], [g], [(d, BF16)], [], deps=deps)[0]


def _norm_bwd(x, dxn, dres, g, name):
    d = x.shape[1]

    def fn(x, dxn, dres, g):
        _, xh, r = _rms_f(x, g)
        dx, dg = _rms_b(dxn, g, xh, r)
        return (dres + dx,), (dg,)

    return _rowwise(name, fn, [x, dxn, dres], [g], [(d, F32)], [d])


def _block_fwd(h_prev, y, g_y, g_n, name, deps=()):
    d = h_prev.shape[1]

    def fn(h_prev, y, g_y, g_n):
        h = h_prev + _rms_f(y, g_y)[0]
        return (h, _rms_f(h, g_n)[0]), ()

    return _rowwise(name, fn, [h_prev, y], [g_y, g_n], [(d, F32), (d, BF16)], [], deps=deps)


def _block_bwd(h, y, d_up, d_n, g_y, g_n, name, deps=()):
    d = h.shape[1]

    def fn(h, y, d_up, d_n, g_y, g_n):
        _, hh, hr = _rms_f(h, g_n)
        dh_n, dg_n = _rms_b(d_n, g_n, hh, hr)
        dh = d_up + dh_n
        _, yh, yr = _rms_f(y, g_y)
        dy, dg_y = _rms_b(dh, g_y, yh, yr)
        return (dh, dy), (dg_y, dg_n)

    return _rowwise(name, fn, [h, y, d_up, d_n], [g_y, g_n], [(d, F32), (d, BF16)], [d, d], deps=deps)


def _loss_head(h2, z, e, target, g, name):
    d = h2.shape[1]

    def fn(h2, z, e, target, g):
        gate = jax.nn.sigmoid(z)
        y, xh, r = _rms_f(e * gate, g)
        diff = h2 + y - target
        loss = 0.5 * jnp.sum(jnp.mean(diff * diff, axis=-1, keepdims=True), axis=0, keepdims=True)
        dh3 = diff / d
        deg, dg = _rms_b(dh3, g, xh, r)
        dz = deg * e * gate * (1.0 - gate)
        de = deg * gate
        return (dh3, dz, de), (dg, jnp.broadcast_to(loss, (1, LANES)))

    return _rowwise(name, fn, [h2, z, e, target], [g], [(d, F32), (d, BF16), (d, BF16)], [d, LANES])


def _shift_down(x, s, rows):
    return jnp.where(rows >= s, pltpu.roll(x, s, axis=0), 0.0)


def _shift_up(x, s, rows):
    t = x.shape[0]
    return jnp.where(rows < t - s, pltpu.roll(x, t - s, axis=0), 0.0)


def _conv_pre(x, w, rows):
    pre = x * w[QK_CONV - 1:QK_CONV, :]
    for s in range(1, QK_CONV):
        pre = pre + _shift_down(x, s, rows) * w[QK_CONV - 1 - s:QK_CONV - s, :]
    return pre


def _conv_fwd(proj_a, w, start, name, scale_t=1.0):
    t = proj_a.shape[0]
    width = w.shape[1]
    tc = _tile(width, 256)
    off = start // tc

    def body(x_ref, w_ref, y_ref, yt_ref):
        x = x_ref[...]
        rows = lax.broadcasted_iota(jnp.int32, x.shape, 0)
        pre = _conv_pre(x, w_ref[...], rows)
        y = pre * jax.nn.sigmoid(pre)
        y_ref[...] = y
        yt_ref[...] = (y * scale_t).T.astype(yt_ref.dtype)

    return pl.pallas_call(
        body, name=name,
        out_shape=[jax.ShapeDtypeStruct((t, width), F32), jax.ShapeDtypeStruct((width, t), BF16)],
        grid=(width // tc,),
        in_specs=[pl.BlockSpec((t, tc), lambda j: (0, off + j)), pl.BlockSpec((QK_CONV, tc), lambda j: (0, j))],
        out_specs=[pl.BlockSpec((t, tc), lambda j: (0, j)), pl.BlockSpec((tc, t), lambda j: (j, 0))],
        compiler_params=_params("parallel"),
    )(proj_a, w)


def _conv_bwd(proj_a, w, dy, start, name, into=None):
    t, total = proj_a.shape
    width = w.shape[1]
    tc = _tile(width, 256)
    off = start // tc
    tail = [] if into is None else [into]

    def body(x_ref, w_ref, dy_ref, *refs):
        dx_ref, dw_ref = refs[len(tail):]
        x, w = x_ref[...], w_ref[...]
        rows = lax.broadcasted_iota(jnp.int32, x.shape, 0)
        pre = _conv_pre(x, w, rows)
        sig = jax.nn.sigmoid(pre)
        dpre = dy_ref[...] * (sig * (1.0 + pre * (1.0 - sig)))
        dx = dpre * w[QK_CONV - 1:QK_CONV, :]
        dws = [jnp.sum(dpre * x, axis=0, keepdims=True)]
        for s in range(1, QK_CONV):
            dx = dx + _shift_up(dpre, s, rows) * w[QK_CONV - 1 - s:QK_CONV - s, :]
            dws.append(jnp.sum(dpre * _shift_down(x, s, rows), axis=0, keepdims=True))
        dx_ref[...] = dx.astype(dx_ref.dtype)
        for s in range(QK_CONV):
            dw_ref[QK_CONV - 1 - s:QK_CONV - s, :] = dws[s]

    return pl.pallas_call(
        body, name=name,
        out_shape=[jax.ShapeDtypeStruct((t, total), BF16), jax.ShapeDtypeStruct((QK_CONV, width), F32)],
        grid=(width // tc,),
        in_specs=[pl.BlockSpec((t, tc), lambda j: (0, off + j)), pl.BlockSpec((QK_CONV, tc), lambda j: (0, j)),
                  pl.BlockSpec((t, tc), lambda j: (0, j))] + [ANY_SPEC] * len(tail),
        out_specs=[pl.BlockSpec((t, tc), lambda j: (0, off + j)), pl.BlockSpec((QK_CONV, tc), lambda j: (0, j))],
        input_output_aliases={3: 0} if tail else {},
        compiler_params=_params("parallel"),
    )(proj_a, w, dy, *tail)


def _dot(a, b, dims):
    return lax.dot_general(a.astype(BF16), b.astype(BF16), (dims, ((), ())), preferred_element_type=F32)


def _dot_split(ones, x):
    hi = x.astype(BF16)
    rest = x - hi.astype(F32)
    mid = rest.astype(BF16)
    lo = (rest - mid.astype(F32)).astype(BF16)
    dot = lambda part: lax.dot_general(ones, part, (NN, ((), ())), preferred_element_type=F32)
    return dot(hi) + dot(mid) + dot(lo)


def _chunk_iotas():
    rows = lax.broadcasted_iota(jnp.int32, (CHUNK, CHUNK), 0)
    cols = lax.broadcasted_iota(jnp.int32, (CHUNK, CHUNK), 1)
    return rows, cols


def _gates_fwd(proj_b, bias, col_block, heads, name):
    t = proj_b.shape[0]

    def body(x_ref, b_ref, gb_ref, gbt_ref, cols_ref):
        rows, cols = _chunk_iotas()
        tri = (cols <= rows).astype(BF16)
        for ci in range(t // CHUNK):
            sl = slice(ci * CHUNK, (ci + 1) * CHUNK)
            pre = x_ref[sl, :] + b_ref[...]
            log_f = jnp.minimum(pre, 0.0) - jnp.log(1.0 + jnp.exp(-jnp.abs(pre)))
            log_f = jnp.where((cols >= heads) & (cols < 2 * heads), log_f, 0.0)
            tile = jnp.where(cols < heads, pre, _dot_split(tri, log_f))
            gb_ref[sl, :] = tile
            gbt_ref[:, sl] = tile.T
            for h in range(2 * heads):
                cols_ref[h, sl, :] = jnp.broadcast_to(tile[:, h:h + 1], (CHUNK, LANES))

    return pl.pallas_call(
        body, name=name,
        out_shape=[jax.ShapeDtypeStruct((t, LANES), F32), jax.ShapeDtypeStruct((LANES, t), F32),
                   jax.ShapeDtypeStruct((2 * heads, t, LANES), F32)],
        grid=(1,),
        in_specs=[pl.BlockSpec((t, LANES), lambda i: (0, col_block)), pl.BlockSpec((1, LANES), lambda i: (0, 0))],
        out_specs=[pl.BlockSpec((t, LANES), lambda i: (0, 0)), pl.BlockSpec((LANES, t), lambda i: (0, 0)),
                   pl.BlockSpec((2 * heads, t, LANES), lambda i: (0, 0, 0))],
        compiler_params=_params("arbitrary"),
    )(proj_b, bias)


def _gates_bwd(proj_b, bias, dcol, drow, col_block, heads, name):
    t = proj_b.shape[0]

    def body(x_ref, b_ref, dc_ref, dr_ref, dx_ref, db_ref):
        rows, cols = _chunk_iotas()
        later = (cols >= rows).astype(BF16)
        total = jnp.zeros((1, LANES), F32)
        for ci in range(t // CHUNK):
            sl = slice(ci * CHUNK, (ci + 1) * CHUNK)
            by_row = dr_ref[:, sl].T
            d = dc_ref[sl, :] + jnp.where(cols < heads, by_row,
                                          jnp.where(cols < 2 * heads, -pltpu.roll(by_row, heads, axis=1), 0.0))
            d_log_f = _dot_split(later, jnp.where(cols >= heads, d, 0.0))
            pre = x_ref[sl, :] + b_ref[...]
            dx = jnp.where(cols < heads, d, jnp.where(cols < 2 * heads, d_log_f * jax.nn.sigmoid(-pre), 0.0))
            dx_ref[sl, :] = dx.astype(dx_ref.dtype)
            total = total + jnp.sum(dx, axis=0, keepdims=True)
        db_ref[...] = total

    return pl.pallas_call(
        body, name=name,
        out_shape=[jax.ShapeDtypeStruct((t, LANES), BF16), jax.ShapeDtypeStruct((1, LANES), F32)],
        grid=(1,),
        in_specs=[pl.BlockSpec((t, LANES), lambda i: (0, col_block)), pl.BlockSpec((1, LANES), lambda i: (0, 0)),
                  pl.BlockSpec((t, LANES), lambda i: (0, 0)), pl.BlockSpec((LANES, t), lambda i: (0, 0))],
        out_specs=[pl.BlockSpec((t, LANES), lambda i: (0, 0)), pl.BlockSpec((1, LANES), lambda i: (0, 0))],
        compiler_params=_params("arbitrary"),
    )(proj_b, bias, dcol, drow)


def _gate_views(li_ref, b_ref, gbt_ref, s, head, heads, sl):
    li_c, b_c = li_ref[s, sl, :], b_ref[s, sl, :]
    li_r = jnp.broadcast_to(gbt_ref[head, :, sl], (CHUNK, CHUNK))
    b_r = jnp.broadcast_to(gbt_ref[heads + head, :, sl], (CHUNK, CHUNK))
    return li_c, b_c, li_r, b_r


def _chunk_state(kt, k, v, li_c, b_c, ct, nrow, m_prev):
    b_tot = b_c[CHUNK - 1:CHUNK, 0:1]
    a = b_tot - b_c[:, 0:1] + li_c[:, 0:1]
    m_new = jnp.maximum(b_tot + m_prev, jnp.max(a, axis=0, keepdims=True))
    dec = jnp.exp(b_tot + m_prev - m_new)
    w = jnp.exp(a - m_new)
    wv = w * v
    ct_new = dec * ct + _dot(kt, wv, NN)
    n_new = dec * nrow + jnp.sum(w * k, axis=0, keepdims=True)
    return ct_new, n_new, m_new, dec, w, wv


def _chunk_scores(q, k, b_c, li_r, b_r, m_prev):
    rows, cols = _chunk_iotas()
    dlog = jnp.where(cols <= rows, b_c - b_r + li_r, -jnp.inf)
    inter = b_c[:, 0:1] + m_prev
    mt = jnp.maximum(inter, jnp.max(dlog, axis=-1, keepdims=True))
    dw = jnp.exp(dlog - mt)
    iw = jnp.exp(inter - mt)
    a = _dot(q, k, NT)
    return dw, iw, mt, a, a * dw


def _to_row(column):
    rows, cols = _chunk_iotas()
    return jnp.sum(jnp.where(rows == cols, jnp.broadcast_to(column, (CHUNK, CHUNK)), 0.0), axis=0, keepdims=True)


def _head_out(h, o, g):
    cell, hh, r = _rms_f(h, g)
    sig = jax.nn.sigmoid(o)
    return sig * cell, (cell, hh, r, sig)


HEADS_PER_STEP = 2


def _mlstm_fwd(qc, kc, kct, proj_a, gcols, gbt, g_mlstm, heads, d_model, name, dep):
    t = qc.shape[0]
    dh = g_mlstm.shape[1] // heads
    nc = t // CHUNK
    scale = dh ** -0.5
    hp = HEADS_PER_STEP

    def body(q_ref, k_ref, kt_ref, v_ref, o_ref, li_ref, b_ref, gbt_ref, g_ref, dep_ref, out_ref, ct_out, nm_out,
             ct_ref, n_ref, m_ref):
        del dep_ref
        step = pl.program_id(0)
        ct_ref[...] = jnp.zeros_like(ct_ref)
        n_ref[...] = jnp.zeros_like(n_ref)
        m_ref[...] = jnp.zeros_like(m_ref)
        tile_row = lax.broadcasted_iota(jnp.int32, (8, LANES), 0)

        def chunk(ci, carry):
            sl = pl.ds(pl.multiple_of(ci * CHUNK, CHUNK), CHUNK)
            for s in range(hp):
                cs = slice(s * dh, (s + 1) * dh)
                q, k, v, kt = q_ref[sl, cs], k_ref[sl, cs] * scale, v_ref[sl, cs], kt_ref[cs, sl]
                li_c, b_c, li_r, b_r = _gate_views(li_ref, b_ref, gbt_ref, s, step * hp + s, heads, sl)
                ct, nrow, m_prev = ct_ref[s], n_ref[s, 0:1, :], m_ref[s, 0:1, 0:1]
                ct_out[s, ci] = ct
                nm_out[s, ci] = jnp.where(tile_row == 0, nrow, jnp.where(tile_row == 1, m_prev, 0.0))
                dw, iw, mt, _, sm = _chunk_scores(q, k, b_c, li_r, b_r, m_prev)
                iq = iw * q
                num = _dot(jnp.concatenate([iq, sm], axis=1), jnp.concatenate([ct, v], axis=0), NN)
                den = jnp.sum(iq * nrow + sm, axis=-1, keepdims=True)
                h = num / jnp.maximum(jnp.abs(den), jnp.exp(-mt))
                y, _ = _head_out(h, o_ref[sl, cs], g_ref[:, cs])
                out_ref[sl, cs] = y.astype(out_ref.dtype)
                ct_new, n_new, m_new, _, _, _ = _chunk_state(kt, k, v, li_c, b_c, ct, nrow, m_prev)
                ct_ref[s] = ct_new
                n_ref[s, 0:1, :] = n_new
                m_ref[s, 0:1, 0:1] = m_new
            return carry

        lax.fori_loop(0, nc, chunk, 0, unroll=2)

    col = lambda off: pl.BlockSpec((t, hp * dh), lambda h: (0, off // hp + h))
    slab = lambda off: pl.BlockSpec((hp, t, LANES), lambda h: (off // hp + h, 0, 0))
    return pl.pallas_call(
        body, name=name,
        out_shape=[jax.ShapeDtypeStruct((t, d_model), BF16), jax.ShapeDtypeStruct((heads, nc, dh, dh), F32),
                   jax.ShapeDtypeStruct((heads, nc, 8, LANES), F32)],
        grid=(heads // hp,),
        in_specs=[col(0), col(0), pl.BlockSpec((hp * dh, t), lambda h: (h, 0)), col(2 * heads), col(3 * heads),
                  slab(0), slab(heads), pl.BlockSpec((2 * heads, 1, t), lambda h: (0, 0, 0)),
                  pl.BlockSpec((1, hp * dh), lambda h: (0, h)), ANY_SPEC],
        out_specs=[pl.BlockSpec((t, hp * dh), lambda h: (0, h)),
                   pl.BlockSpec((hp, nc, dh, dh), lambda h: (h, 0, 0, 0)),
                   pl.BlockSpec((hp, nc, 8, LANES), lambda h: (h, 0, 0, 0))],
        scratch_shapes=[pltpu.VMEM((hp, dh, dh), F32), pltpu.VMEM((hp, 8, dh), F32), pltpu.VMEM((hp, 8, LANES), F32)],
        compiler_params=_params("arbitrary"),
    )(qc, kc, kct, proj_a, proj_a, gcols, gcols, gbt, g_mlstm, dep)


def _mlstm_bwd(qc, kc, qct, kct, proj_a, gcols, gbt, g_mlstm, states, dhcat, heads, name, dep):
    t = qc.shape[0]
    wm = g_mlstm.shape[1]
    dh = wm // heads
    nc = t // CHUNK
    scale = dh ** -0.5

    def body(q_ref, k_ref, qt_ref, kt_ref, v_ref, o_ref, li_ref, b_ref, gbt_ref, g_ref, ct_all, nm_all, dy_ref, dep_ref,
             dq_ref, dk_ref, dv_ref, do_ref, dcol_ref, drow_ref, dg_ref, dct_ref, dn_ref):
        del dep_ref
        head = pl.program_id(0)
        g = g_ref[...]

        @pl.when(head == 0)
        def _():
            dcol_ref[...] = jnp.zeros_like(dcol_ref)
            drow_ref[...] = jnp.zeros_like(drow_ref)

        def load(ci):
            sl = pl.ds(pl.multiple_of(ci * CHUNK, CHUNK), CHUNK)
            return (sl, q_ref[sl, :], k_ref[sl, :] * scale, v_ref[sl, :], kt_ref[:, sl]) \
                + _gate_views(li_ref, b_ref, gbt_ref, 0, head, heads, sl)

        dct_ref[...] = jnp.zeros_like(dct_ref)
        dn_ref[...] = jnp.zeros_like(dn_ref)
        dg_ref[...] = jnp.zeros_like(dg_ref)
        rows, cols = _chunk_iotas()
        last = lax.broadcasted_iota(jnp.int32, (CHUNK, 1), 0) == CHUNK - 1
        lane = lax.broadcasted_iota(jnp.int32, (CHUNK, LANES), 1)

        def bwd(step, carry):
            ci = nc - 1 - step
            sl, q, k, v, kt, li_c, b_c, li_r, b_r = load(ci)
            qt = qt_ref[:, sl]
            ct, nrow, m_prev = ct_all[ci], nm_all[ci, 0:1, :], nm_all[ci, 1:2, 0:1]
            dw, iw, mt, a, sm = _chunk_scores(q, k, b_c, li_r, b_r, m_prev)
            qc_ = _dot(q, ct, NN)
            qn = jnp.sum(q * nrow, axis=-1, keepdims=True)
            num = iw * qc_ + _dot(sm, v, NN)
            den = iw * qn + jnp.sum(sm, axis=-1, keepdims=True)
            floor = jnp.exp(-mt)
            dn = jnp.maximum(jnp.abs(den), floor)
            h = num / dn
            _, (cell, hh, r, sig) = _head_out(h, o_ref[sl, :], g)
            dy = dy_ref[sl, :]
            dcell = dy * sig
            do_ref[sl, :] = (dy * cell * sig * (1.0 - sig)).astype(do_ref.dtype)
            dh_, dg = _rms_b(dcell, g, hh, r)
            dg_ref[...] += dg
            dnum = dh_ / dn
            ddn = -jnp.sum(dh_ * h, axis=-1, keepdims=True) / dn
            dden = jnp.where(jnp.abs(den) >= floor, ddn * jnp.sign(den), 0.0)
            idn = iw * dnum
            idd = iw * dden
            dw_t = jnp.exp(jnp.where(cols >= rows, b_r - b_c + li_c, -jnp.inf) - _to_row(mt))
            sm_t = _dot(k, q, NT) * dw_t
            dsm = _dot(dnum, v, NT) + dden
            da_ = dsm * dw
            da_t = (_dot(v, dnum, NT) + _to_row(dden)) * dw_t
            e = da_ * a
            dq = _dot(idn, ct, NT) + _dot(da_, k, NN) + idd * nrow
            dk = _dot(da_t, q, NN)
            dv = _dot(sm_t, dnum, NN)
            dct = _dot(qt, idn, NN)
            dnr = jnp.sum(q * idd, axis=0, keepdims=True)
            db = jnp.sum(e + idn * qc_, axis=-1, keepdims=True) + idd * qn
            _, _, _, dec, w, wv = _chunk_state(kt, k, v, li_c, b_c, ct, nrow, m_prev)
            dct_new, dn_new = dct_ref[...], dn_ref[0:1, :]
            ddec = jnp.sum(jnp.sum(dct_new * ct, axis=-1, keepdims=True), axis=0, keepdims=True) \
                + jnp.sum(dn_new * nrow, axis=-1, keepdims=True)
            dk = dk + _dot(wv, dct_new, NT) + w * dn_new
            dwv = _dot(k, dct_new, NN)
            dv = dv + w * dwv
            dwt = jnp.sum(dwv * v + k * dn_new, axis=-1, keepdims=True)
            da = dwt * w
            dbtot = jnp.sum(da, axis=0, keepdims=True) + ddec * dec
            db = db - da + jnp.where(last, dbtot, 0.0)
            dct_ref[...] = dec * dct_new + dct
            dn_ref[0:1, :] = dec * dn_new + dnr
            dq_ref[sl, :] = dq
            dk_ref[sl, :] = dk * scale
            dv_ref[sl, :] = dv.astype(dv_ref.dtype)
            dcol_ref[sl, :] += jnp.where(lane == head, da, 0.0) + jnp.where(lane == heads + head, db, 0.0)
            drow_ref[head, :, sl] = jnp.sum(e, axis=0, keepdims=True)
            return carry

        lax.fori_loop(0, nc, bwd, 0, unroll=8)

    col = lambda off: pl.BlockSpec((t, dh), lambda h: (0, off + h))
    rowb = pl.BlockSpec((dh, t), lambda h: (h, 0))
    slab = lambda off: pl.BlockSpec((1, t, LANES), lambda h: (off + h, 0, 0))
    vec = pl.BlockSpec((1, dh), lambda h: (0, h))
    return pl.pallas_call(
        body, name=name,
        out_shape=[jax.ShapeDtypeStruct((t, wm), F32)] * 2 + [jax.ShapeDtypeStruct((t, wm), BF16)] * 2
        + [jax.ShapeDtypeStruct((t, LANES), F32), jax.ShapeDtypeStruct((heads, 1, t), F32),
           jax.ShapeDtypeStruct((1, wm), F32)],
        grid=(heads,),
        in_specs=[col(0), col(0), rowb, rowb, col(2 * heads), col(3 * heads), slab(0), slab(heads),
                  pl.BlockSpec((2 * heads, 1, t), lambda h: (0, 0, 0)), vec,
                  pl.BlockSpec((None, nc, dh, dh), lambda h: (h, 0, 0, 0)),
                  pl.BlockSpec((None, nc, 8, LANES), lambda h: (h, 0, 0, 0)), col(0), ANY_SPEC],
        out_specs=[col(0), col(0), col(0), col(0), pl.BlockSpec((t, LANES), lambda h: (0, 0)),
                   pl.BlockSpec((heads, 1, t), lambda h: (0, 0, 0)), vec],
        scratch_shapes=[pltpu.VMEM((dh, dh), F32), pltpu.VMEM((8, dh), F32)],
        compiler_params=_params("arbitrary"),
    )(qc, kc, qct, kct, proj_a, proj_a, gcols, gcols, gbt, g_mlstm, *states, dhcat, dep)


def _pool_window(gi):
    return jnp.where(gi == 0, 2.0, jnp.where(gi == 1, 4.0, jnp.where(gi == 2, 8.0, 16.0))).astype(F32)


def _pool_select(gi, levels):
    return jnp.where(gi == 0, levels[0], jnp.where(gi == 1, levels[1], jnp.where(gi == 2, levels[2], levels[3])))


def _pooled(u, gi, rows):
    s, levels = u, []
    for lvl in range(POOL_GROUPS):
        s = s + _shift_down(s, 1 << lvl, rows)
        levels.append(s)
    cnt = jnp.minimum((rows + 1).astype(F32), _pool_window(gi))
    return _pool_select(gi, levels) / cnt - u, cnt


def _pool_fwd(proj_b, w_pool, scale, hcat, name):
    t, d_model = hcat.shape
    g, cg, _ = w_pool.shape
    off = (d_model - g * cg) // cg

    def body(u_ref, w_ref, s_ref, hcat_ref, out_ref):
        del hcat_ref
        gi = pl.program_id(0)
        u = u_ref[...]
        rows = lax.broadcasted_iota(jnp.int32, u.shape, 0)
        pooled, _ = _pooled(u, gi, rows)
        out_ref[...] = (_dot(pooled, w_ref[...], NN) * s_ref[...]).astype(out_ref.dtype)

    return pl.pallas_call(
        body, name=name,
        out_shape=jax.ShapeDtypeStruct((t, d_model), BF16),
        grid=(g,),
        in_specs=[pl.BlockSpec((t, cg), lambda gi: (0, gi)), pl.BlockSpec((None, cg, cg), lambda gi: (gi, 0, 0)),
                  pl.BlockSpec((1, cg), lambda gi: (0, gi)), HBM_SPEC],
        out_specs=pl.BlockSpec((t, cg), lambda gi: (0, off + gi)),
        input_output_aliases={3: 0},
        compiler_params=_params("arbitrary"),
    )(proj_b, w_pool, scale, hcat)


def _pool_bwd(proj_b, w_pool, scale, dhcat, name):
    t, d_model = dhcat.shape
    g, cg, _ = w_pool.shape
    off = (d_model - g * cg) // cg

    def body(u_ref, w_ref, s_ref, dy_ref, du_ref, dw_ref, ds_ref):
        gi = pl.program_id(0)
        u, w, dy = u_ref[...], w_ref[...], dy_ref[...]
        rows = lax.broadcasted_iota(jnp.int32, u.shape, 0)
        pooled, cnt = _pooled(u, gi, rows)
        mixed = _dot(pooled, w, NN)
        ds_ref[...] = jnp.sum(dy * mixed, axis=0, keepdims=True)
        dmixed = dy * s_ref[...]
        dw_ref[...] = _dot(pooled, dmixed, TN).astype(dw_ref.dtype)
        dpooled = _dot(dmixed, w, NT)
        s, levels = dpooled / cnt, []
        for lvl in range(POOL_GROUPS):
            s = s + _shift_up(s, 1 << lvl, rows)
            levels.append(s)
        du_ref[...] = (_pool_select(gi, levels) - dpooled).astype(du_ref.dtype)

    return pl.pallas_call(
        body, name=name,
        out_shape=[jax.ShapeDtypeStruct((t, g * cg), BF16), jax.ShapeDtypeStruct((g, cg, cg), BF16),
                   jax.ShapeDtypeStruct((1, g * cg), F32)],
        grid=(g,),
        in_specs=[pl.BlockSpec((t, cg), lambda gi: (0, gi)), pl.BlockSpec((None, cg, cg), lambda gi: (gi, 0, 0)),
                  pl.BlockSpec((1, cg), lambda gi: (0, gi)), pl.BlockSpec((t, cg), lambda gi: (0, off + gi))],
        out_specs=[pl.BlockSpec((t, cg), lambda gi: (0, gi)), pl.BlockSpec((None, cg, cg), lambda gi: (gi, 0, 0)),
                   pl.BlockSpec((1, cg), lambda gi: (0, gi))],
        compiler_params=_params("arbitrary"),
    )(proj_b, w_pool, scale, dhcat)


def _pad_cols(a, width):
    return jnp.pad(a, ((0, 0), (0, width - a.shape[1])))


W_IN_COL_TILE = 512


def _w_in_rows(pieces, main, gates, name):
    n, shard, d = pieces.shape
    pool = n * shard - main - gates
    tc = _tile(d, W_IN_COL_TILE)

    def body(x_ref, a_ref, b_ref, flat):
        for j in range(n):
            flat[j * shard:(j + 1) * shard, :] = x_ref[j]
        a_ref[...] = flat[:main, :]
        b_ref[:pool, :] = flat[main + gates:, :]
        b_ref[pool:pool + gates, :] = flat[main:main + gates, :]
        b_ref[pool + gates:, :] = jnp.zeros((LANES - gates, tc), b_ref.dtype)

    return pl.pallas_call(
        body, name=name,
        out_shape=[jax.ShapeDtypeStruct((main, d), pieces.dtype), jax.ShapeDtypeStruct((pool + LANES, d), pieces.dtype)],
        grid=(d // tc,),
        in_specs=[pl.BlockSpec((n, shard, tc), lambda j: (0, 0, j))],
        out_specs=[pl.BlockSpec((main, tc), lambda j: (0, j)), pl.BlockSpec((pool + LANES, tc), lambda j: (0, j))],
        scratch_shapes=[pltpu.VMEM((n * shard, tc), pieces.dtype)],
        compiler_params=_params("parallel"),
    )(pieces)


def _w_in_pieces(rows_main, rows_rest, shard, gates, name):
    main, d = rows_main.shape
    pool = rows_rest.shape[0] - LANES
    n = (main + gates + pool) // shard
    tc = _tile(d, W_IN_COL_TILE)

    def body(a_ref, b_ref, o_ref, flat):
        flat[:main, :] = a_ref[...]
        flat[main:main + gates, :] = b_ref[pool:pool + gates, :]
        flat[main + gates:, :] = b_ref[:pool, :]
        for j in range(n):
            o_ref[j] = flat[j * shard:(j + 1) * shard, :]

    return pl.pallas_call(
        body, name=name,
        out_shape=jax.ShapeDtypeStruct((n, shard, d), rows_main.dtype),
        grid=(d // tc,),
        in_specs=[pl.BlockSpec((main, tc), lambda j: (0, j)), pl.BlockSpec((pool + LANES, tc), lambda j: (0, j))],
        out_specs=pl.BlockSpec((n, shard, tc), lambda j: (0, 0, j)),
        scratch_shapes=[pltpu.VMEM((n * shard, tc), rows_main.dtype)],
        compiler_params=_params("parallel"),
    )(rows_main, rows_rest)


def kernel(x, p, w_in, b_gates, w_qk_conv, g_mlstm, w_pool, pool_scale, w_out, g_mix_pre, g_mix_post, w_ff1, w_ff2, g_ff_pre, g_ff_post, w_ple_proj, w_ple_gate, g_ple_gate, g_ple_post, loss_target, m_w_in, m_b_gates, m_w_qk_conv, m_g_mlstm, m_w_pool, m_pool_scale, m_w_out, m_g_mix_pre, m_g_mix_post, m_w_ff1, m_w_ff2, m_g_ff_pre, m_g_ff_post, m_w_ple_proj, m_w_ple_gate, m_g_ple_gate, m_g_ple_post, v_w_in, v_b_gates, v_w_qk_conv, v_g_mlstm, v_w_pool, v_pool_scale, v_w_out, v_g_mix_pre, v_g_mix_post, v_w_ff1, v_w_ff2, v_g_ff_pre, v_g_ff_post, v_w_ple_proj, v_w_ple_gate, v_g_ple_gate, v_g_ple_post):
    weights = dict(w_in=w_in, b_gates=b_gates, w_qk_conv=w_qk_conv, g_mlstm=g_mlstm, w_pool=w_pool,
                   pool_scale=pool_scale, w_out=w_out, g_mix_pre=g_mix_pre, g_mix_post=g_mix_post, w_ff1=w_ff1,
                   w_ff2=w_ff2, g_ff_pre=g_ff_pre, g_ff_post=g_ff_post, w_ple_proj=w_ple_proj,
                   w_ple_gate=w_ple_gate, g_ple_gate=g_ple_gate, g_ple_post=g_ple_post)
    mom1 = dict(w_in=m_w_in, b_gates=m_b_gates, w_qk_conv=m_w_qk_conv, g_mlstm=m_g_mlstm, w_pool=m_w_pool,
                pool_scale=m_pool_scale, w_out=m_w_out, g_mix_pre=m_g_mix_pre, g_mix_post=m_g_mix_post,
                w_ff1=m_w_ff1, w_ff2=m_w_ff2, g_ff_pre=m_g_ff_pre, g_ff_post=m_g_ff_post,
                w_ple_proj=m_w_ple_proj, w_ple_gate=m_w_ple_gate, g_ple_gate=m_g_ple_gate, g_ple_post=m_g_ple_post)
    mom2 = dict(w_in=v_w_in, b_gates=v_b_gates, w_qk_conv=v_w_qk_conv, g_mlstm=v_g_mlstm, w_pool=v_w_pool,
                pool_scale=v_pool_scale, w_out=v_w_out, g_mix_pre=v_g_mix_pre, g_mix_post=v_g_mix_post,
                w_ff1=v_w_ff1, w_ff2=v_w_ff2, g_ff_pre=v_g_ff_pre, g_ff_post=v_g_ff_post,
                w_ple_proj=v_w_ple_proj, w_ple_gate=v_w_ple_gate, g_ple_gate=v_g_ple_gate, g_ple_post=v_g_ple_post)
    order = list(weights)

    t, d = x.shape[1], x.shape[2]
    heads = MLSTM_HEADS
    wm = g_mlstm.shape[1]
    pw = pool_scale.shape[1]
    cg = pw // POOL_GROUPS
    in_shard = w_in.shape[2]
    xs, ps, tgt = x[0], p[0, 0], loss_target[0]

    mx, my, mc = lax.axis_index("x"), lax.axis_index("y"), lax.axis_index("c")
    core = jnp.reshape(mc, (1,)).astype(jnp.int32)
    chip = jnp.reshape(2 * mx + my, (1,)).astype(jnp.int32)

    first = [w_in[0].T.astype(BF16), w_qk_conv[0], w_pool[0].reshape(POOL_GROUPS * w_pool.shape[2], cg).astype(BF16)]
    started, token = _gather_start([[w_ple_proj[0].astype(BF16)], first], "gather_start_in")
    anchor = token[0, 0]
    later = lambda w: (w[0] + anchor).astype(BF16)
    ff2_shard = later(w_ff2)
    rest, token = _gather_start([[later(w_out)], [later(w_ff1)], [ff2_shard[:, :d // 2]], [ff2_shard[:, d // 2:]],
                                 [later(w_ple_gate)]], "gather_start_rest")
    started = dict(zip(["ple_proj", "in", "out", "ff1", "ff2_lo", "ff2_hi", "ple_gate"], started + rest))
    bias = _pad_cols(b_gates, LANES)
    small = ["b_gates", "g_mlstm", "pool_scale", "g_mix_pre", "g_mix_post", "g_ff_pre", "g_ff_post", "g_ple_gate",
             "g_ple_post"]
    pack = lambda src, more=(): _pad_rows16(jnp.concatenate(
        [_pad_cols(src[k] + anchor, d) for k in small] + [_pad_cols(a, d) for a in more], axis=0))
    small_state = [pack(weights), pack(mom1), pack(mom2)]
    in_state = [a[0].T + anchor for a in (w_in, m_w_in, v_w_in)]

    xn = _norm_fwd(xs, g_mix_pre, "norm_mix_pre", deps=[token])
    (g_pproj,) = _forward_pair(_gather_wait(started["ple_proj"], xn, "gather_wait_ple_proj"), "gather_pair_ple_proj")
    e = _mm_nn_pieces("ple_proj", ps, g_pproj)[0]
    g_in, g_conv, g_pool = _forward_pair(
        _gather_wait(started["in"], [e] + small_state + in_state, "gather_wait_in"), "gather_pair_in")
    in_t_a, in_t_b = _w_in_rows(g_in, 4 * wm, 2 * heads, "w_in_rows")
    conv_w = g_conv.transpose(1, 0, 2).reshape(QK_CONV, 2 * wm)
    pool_w = g_pool.reshape(N_DEV, POOL_GROUPS, cg // N_DEV, cg).transpose(1, 0, 2, 3).reshape(POOL_GROUPS, cg, cg)
    proj_a = _mm_nt("proj_qkvo", xn, in_t_a, n_rows=4 * wm)
    proj_b = _mm_nt("proj_pool_gates", xn, in_t_b)
    qc, qct = _conv_fwd(proj_a, conv_w[:, :wm], 0, "q_conv")
    kc, kct = _conv_fwd(proj_a, conv_w[:, wm:], wm, "k_conv", scale_t=(wm // heads) ** -0.5)
    gb, gbt, gcols = _gates_fwd(proj_b, bias, pw // LANES, heads, "gates")
    gbt = gbt[:2 * heads].reshape(2 * heads, 1, t)
    fwd_out = _forward_start(_gather_wait(started["out"], gb, "gather_wait_out"), "gather_pair_start_out")
    hcat, *mlstm_states = _mlstm_fwd(qc, kc, kct, proj_a, gcols, gbt, g_mlstm, heads, d, "mlstm", fwd_out["token"])
    hcat = _pool_fwd(proj_b, pool_w, pool_scale, hcat, "pool")
    (g_out,) = _forward_wait(fwd_out, hcat, "gather_pair_wait_out")
    out_w = g_out.reshape(d, d)
    mix = _mm_nn("mix_out", hcat, out_w)[0]
    fwd_ff1 = _forward_start(_gather_wait(started["ff1"], mix, "gather_wait_ff1"), "gather_pair_start_ff1")
    h1, hn = _block_fwd(xs, mix, g_mix_post, g_ff_pre, "residual_mix", deps=[fwd_ff1["token"]])
    (g_ff1,) = _forward_wait(fwd_ff1, hn, "gather_pair_wait_ff1")
    relu_a, act = _mm_nn_pieces("ff1", hn, g_ff1, out_dtype=[BF16, BF16],
                                epilogue=lambda acc: (jnp.maximum(acc, 0.0), jnp.square(jnp.maximum(acc, 0.0))))
    (g_ff2_lo,) = _forward_pair(_gather_wait(started["ff2_lo"], act, "gather_wait_ff2_lo"), "gather_pair_ff2_lo")
    ff2_lo = g_ff2_lo.reshape(-1, d // 2)
    ff = _mm_nn("ff2_lo", act, ff2_lo, tn=256, tk=4 * d, out_cols=d)[0]
    (g_ff2_hi,) = _forward_pair(_gather_wait(started["ff2_hi"], ff, "gather_wait_ff2_hi"), "gather_pair_ff2_hi")
    ff2_hi = g_ff2_hi.reshape(-1, d // 2)
    ff = _mm_nn("ff2_hi", act, ff2_hi, tn=256, tk=4 * d, out_cols=d, out_at=d // 2, into=ff)[0]
    fwd_ple = _forward_start(_gather_wait(started["ple_gate"], ff, "gather_wait_ple"), "gather_pair_start_ple")
    h2, hg = _block_fwd(h1, ff, g_ff_post, g_ple_gate, "residual_ff", deps=[fwd_ple["token"]])
    (g_pgate,) = _forward_wait(fwd_ple, hg, "gather_pair_wait_ple")
    pgate_w = g_pgate.reshape(d, d)
    z = _mm_nn("ple_gate", hg, pgate_w)[0]
    dh3, dz, de, dg_ple_post, loss_part = _loss_head(h2, z, e, tgt, g_ple_post, "loss_head")

    out = {}

    def state(k):
        if k == "w_in":
            return in_state
        return [a.reshape(-1, a.shape[-1]) for a in (weights[k], mom1[k], mom2[k])]

    def pair_begin(pieces, tag):
        keys = list(pieces)
        return keys, _pair_start([pieces[k] for k in keys], "reduce_pair_start_" + tag)

    def chip_begin(group, after, tag):
        keys, begun = group
        mine, got = _pair_wait(begun, after, "reduce_pair_wait_" + tag)
        sums = [_pair_sum(p_, g, core, "pair_sum_" + k) for k, p_, g in zip(keys, mine, got)]
        return keys, _chip_start(sums, "reduce_start_" + tag)

    def reduce_finish(group, after, tag):
        keys, begun = group
        sums, got = _chip_wait(begun, after, "reduce_wait_" + tag)
        for k, s, g in zip(keys, sums, got):
            res = _adamw_shard(*state(k), s, g, chip, "adamw_" + k)
            out[k] = [r.T[None] if k == "w_in" else r.reshape(weights[k].shape) for r in res]

    token_of = lambda group: group[1]["token"]
    dhg = _mm_nt("d_ple_gate_in", dz, pgate_w)
    gw_pgate = _mm_tn("gw_ple_gate", hg, dz)
    gw_pproj = _mm_tn("gw_ple_proj", ps, de, pieces=N_DEV)
    dh2, dff, dg_ff_post, dg_ple_gate = _block_bwd(h2, ff, dh3, dhg, g_ff_post, g_ple_gate, "d_residual_ff")
    da = _mm_nt("d_ff2_in", dff, ff2_lo, b2=ff2_hi, out_dtype=BF16, extra=[relu_a],
                epilogue=lambda acc, r: (acc * (2.0 * r.astype(F32)),))
    gw_ff2 = _mm_tn("gw_ff2", act, dff)
    ff2_keys = ["w_ff2", "w_ple_gate", "w_ple_proj"]
    pair_ff2 = pair_begin({"w_ff2": gw_ff2.reshape(N_DEV, -1, d),
                           "w_ple_gate": gw_pgate.reshape(N_DEV, d // N_DEV, d), "w_ple_proj": gw_pproj}, "ff2")
    dhn = _mm_nt_pieces("d_ff1_in", da, g_ff1, deps=[token_of(pair_ff2)])
    place = jnp.concatenate([core, chip])
    mine_ff2, got_ff2 = _pair_wait(pair_ff2[1], dhn, "reduce_pair_wait_ff2")
    gw_ff1, sums_ff2 = _mm_tn("gw_ff1", hn, da, pieces=N_DEV, side=_side_pair_sum(mine_ff2[0], got_ff2[0], place))
    sums_ple = [_pair_sum(p_, g, core, "pair_sum_" + k) for k, p_, g in zip(ff2_keys[1:], mine_ff2[1:], got_ff2[1:])]
    red_ff2 = (ff2_keys, _chip_start([sums_ff2] + sums_ple, "reduce_start_ff2"))
    pair_ff1 = pair_begin({"w_ff1": gw_ff1}, "ff1")
    dh1, dmix, dg_mix_post, dg_ff_pre = _block_bwd(h1, mix, dh2, dhn, g_mix_post, g_ff_pre, "d_residual_mix",
                                                   deps=[token_of(pair_ff1), token_of(red_ff2)])
    mine_ff1, got_ff1 = _pair_wait(pair_ff1[1], dmix, "reduce_pair_wait_ff1")
    dhcat, sums_ff1 = _mm_nt("d_mix_in", dmix, out_w, tm=t // 2, tn=d // 2,
                             side=_side_pair_sum(mine_ff1[0], got_ff1[0], place))
    red_ff1 = (["w_ff1"], _chip_start([sums_ff1], "reduce_start_ff1"))
    gw_out = _mm_tn("gw_out", hcat, dmix, deps=[token_of(red_ff1)])
    du, gw_pool, dg_pool_scale = _pool_bwd(proj_b, pool_w, pool_scale, dhcat, "d_pool")
    pair_out = pair_begin({
        "w_out": gw_out.reshape(N_DEV, d // N_DEV, d),
        "w_pool": gw_pool.reshape(POOL_GROUPS, N_DEV, cg // N_DEV, cg).transpose(1, 0, 2, 3).reshape(N_DEV, -1, cg)},
        "out")
    dq, dk, dv, do, dcol, drow, dg_mlstm = _mlstm_bwd(qc, kc, qct, kct, proj_a, gcols, gbt, g_mlstm, mlstm_states, dhcat,
                                                      heads, "d_mlstm", token_of(pair_out))
    red_out = chip_begin(pair_out, dq, "out")
    drow = jnp.pad(drow.reshape(heads, t), ((0, LANES - heads), (0, 0)))
    dgates, dbias = _gates_bwd(proj_b, bias, dcol, drow, pw // LANES, heads, "d_gates")
    dproj_a, gw_conv_q = _conv_bwd(proj_a, conv_w[:, :wm], dq, 0, "d_q_conv")
    dproj_a, gw_conv_k = _conv_bwd(proj_a, conv_w[:, wm:], dk, wm, "d_k_conv", into=dproj_a)
    gw_conv = jnp.concatenate([gw_conv_q, gw_conv_k], axis=1)
    dproj_a = lax.dynamic_update_slice(lax.dynamic_update_slice(dproj_a, dv, (0, 2 * wm)), do, (0, 3 * wm))
    dproj_b = jnp.concatenate([du, dgates], axis=1)
    sums_ff2, got_ff2 = _chip_wait(red_ff2[1], token_of(red_out), "reduce_wait_ff2")
    gw_a, *update = _mm_tn("gw_in_qkvo", dproj_a, xn,
                           side=_side_adamw(*state("w_ff2"), sums_ff2[0], got_ff2[0], place))
    out["w_ff2"] = [r.reshape(weights["w_ff2"].shape) for r in update]
    for k, s, g in zip(ff2_keys[1:], sums_ff2[1:], got_ff2[1:]):
        out[k] = [r.reshape(weights[k].shape) for r in _adamw_shard(*state(k), s, g, chip, "adamw_" + k)]
    gw_b = _mm_tn("gw_in_pool_gates", dproj_b, xn)
    pair_in = pair_begin({
        "w_in": _w_in_pieces(gw_a, gw_b, in_shard, 2 * heads, "w_in_pieces"),
        "w_qk_conv": gw_conv.reshape(QK_CONV, N_DEV, -1).transpose(1, 0, 2).astype(BF16)}, "in")
    dxn_b = _mm_nn("d_proj_in", dproj_b, in_t_b, deps=[token_of(pair_in)])[0]
    red_in = chip_begin(pair_in, dxn_b, "in")
    sums_ff1, got_ff1 = _chip_wait(red_ff1[1], token_of(red_in), "reduce_wait_ff1")
    dxn, *update = _mm_nn("d_proj_qkvo_in", dproj_a, in_t_a, tn=512, tk=4 * wm, deps=[token_of(red_in)],
                          extra=[dxn_b], epilogue=lambda acc, prev: (acc + prev,),
                          side=_side_adamw(*state("w_ff1"), sums_ff1[0], got_ff1[0], place))
    out["w_ff1"] = [r.reshape(weights["w_ff1"].shape) for r in update]
    grad_x, dg_mix_pre = _norm_bwd(xs, dxn, dh1, g_mix_pre, "d_norm_mix_pre")

    small_grads = dict(b_gates=dbias[:, :2 * heads], g_mlstm=dg_mlstm, pool_scale=dg_pool_scale, g_mix_pre=dg_mix_pre,
                       g_mix_post=dg_mix_post, g_ff_pre=dg_ff_pre, g_ff_post=dg_ff_post, g_ple_gate=dg_ple_gate,
                       g_ple_post=dg_ple_post)
    small_begun = _everyone_start([pack(small_grads, [loss_part])], "small_grads_start")
    reduce_finish(red_out, small_begun["token"], "out")
    reduce_finish(red_in, out["w_out"][0], "in")
    (small_parts,) = _everyone_wait(small_begun, out["w_in"][0], "small_grads_wait")
    res = _adamw_replicated(*small_state, small_parts, "adamw_replicated")
    loss = res[0][len(small), 0]
    for i, k in enumerate(small):
        width = weights[k].shape[1]
        out[k] = [r[i:i + 1, :width] for r in res]

    return (loss, grad_x[None], *[out[k][0] for k in order], *[out[k][1] for k in order],
            *[out[k][2] for k in order], *[out[k][3] for k in order])


def _pad_rows16(a):
    return jnp.pad(a, ((0, 16 - a.shape[0]), (0, 0)))
```
